```python
import math
import jax
import jax.numpy as jnp
from jax import lax
import numpy as np

D_MODEL = 1024
BATCH = 8
SEQ = 4096
DEPTH = 1
DEC_BATCH = 128
DEC_SEQ = 8
PAST_LEN = 16384
PAGE_SIZE = 128

SSM_WIDTH = D_MODEL // 2
SSM_GROUP = 16
SSM_GROUPS = SSM_WIDTH // SSM_GROUP
SSM_STATE = 64
DT_MIN = 0.001
DT_MAX = 0.1
SWA_HEADS = 8
SWA_KV_HEADS = 2
SWA_REP = SWA_HEADS // SWA_KV_HEADS
SWA_HEAD_DIM = 64
SWA_WIDTH = SWA_HEADS * SWA_HEAD_DIM
SWA_KV_WIDTH = SWA_KV_HEADS * SWA_HEAD_DIM
WINDOW = 128
REL_BUCKETS = 32
REL_MAX_DIST = 128
MEM_TOKENS = 256
MEM_HEADS = 4
MEM_HEAD_DIM = 128
MEM_WIDTH = MEM_HEADS * MEM_HEAD_DIM
N_BRANCHES = 3
IN_SPLITS = (SSM_WIDTH, SWA_WIDTH, SWA_KV_WIDTH, SWA_KV_WIDTH, MEM_WIDTH)
IN_COLS = sum(IN_SPLITS) + N_BRANCHES * D_MODEL
N_EXPERT_GROUPS = 4
EXPERTS_PER_GROUP = 8
N_EXPERTS = N_EXPERT_GROUPS * EXPERTS_PER_GROUP
TOP_K_IN_GROUP = 2
D_EXPERT = D_MODEL // 4
EPS = 1e-6
NEG_INF = -1e30

kernel_name = 'hybrid_s5_swa_sink_memx_hmoe_step'


def rmsnorm(x, g):
    xf = x.astype(jnp.float32)
    xf = xf * lax.rsqrt(jnp.mean(xf * xf, axis=-1, keepdims=True) + EPS)
    return xf.astype(x.dtype) * g


def _ssm_combine(e1, e2):
    a1r, a1i, b1r, b1i = e1
    a2r, a2i, b2r, b2i = e2
    return (a2r * a1r - a2i * a1i,
            a2r * a1i + a2i * a1r,
            a2r * b1r - a2i * b1i + b2r,
            a2r * b1i + a2i * b1r + b2i)


def s5_scan(u, h0_re, h0_im, lp):
    f32 = jnp.float32
    bsz, t = u.shape[:2]
    uf = u.astype(f32).reshape(bsz, t, SSM_GROUPS, SSM_GROUP)
    lr = lp['lam_re'].astype(f32)
    li = lp['lam_im'].astype(f32)
    dt = jnp.exp(lp['log_dt'].astype(f32))[:, None]
    mag = jnp.exp(lr * dt)
    a_re = mag * jnp.cos(li * dt)
    a_im = mag * jnp.sin(li * dt)
    den = lr * lr + li * li
    f_re = ((a_re - 1.0) * lr + a_im * li) / den
    f_im = (a_im * lr - (a_re - 1.0) * li) / den
    br = lp['bm_re'].astype(f32)
    bi = lp['bm_im'].astype(f32)
    bb_re = f_re[..., None] * br - f_im[..., None] * bi
    bb_im = f_re[..., None] * bi + f_im[..., None] * br
    bu_re = jnp.einsum('btgh,gph->btgp', uf, bb_re)
    bu_im = jnp.einsum('btgh,gph->btgp', uf, bb_im)
    aa_re = jnp.broadcast_to(a_re, bu_re.shape)
    aa_im = jnp.broadcast_to(a_im, bu_im.shape)
    cum_a_re, cum_a_im, cum_b_re, cum_b_im = lax.associative_scan(
        _ssm_combine, (aa_re, aa_im, bu_re, bu_im), axis=1)
    h0r = h0_re.astype(f32)[:, None]
    h0i = h0_im.astype(f32)[:, None]
    h_re = cum_a_re * h0r - cum_a_im * h0i + cum_b_re
    h_im = cum_a_re * h0i + cum_a_im * h0r + cum_b_im
    y = (jnp.einsum('btgp,ghp->btgh', h_re, lp['cm_re'].astype(f32))
         - jnp.einsum('btgp,ghp->btgh', h_im, lp['cm_im'].astype(f32))
         + lp['d_skip'].astype(f32).reshape(SSM_GROUPS, SSM_GROUP) * uf)
    return y.reshape(bsz, t, SSM_WIDTH).astype(u.dtype), h_re[:, -1], h_im[:, -1]


def t5_bucket(dist):
    n = jnp.maximum(dist, 0)
    max_exact = REL_BUCKETS // 2
    nf = jnp.maximum(n, 1).astype(jnp.float32)
    large = max_exact + (jnp.log(nf / max_exact) / math.log(REL_MAX_DIST / max_exact)
                         * (REL_BUCKETS - max_exact)).astype(jnp.int32)
    large = jnp.minimum(large, REL_BUCKETS - 1)
    return jnp.where(n < max_exact, n, large)


def sink_band_attention(q, k, v, kmask, sinks, rel_table):
    f32 = jnp.float32
    tq = q.shape[-3]
    tk = k.shape[-3]
    qg = q.reshape(q.shape[:-2] + (SWA_KV_HEADS, SWA_REP, SWA_HEAD_DIM))
    s = jnp.einsum('...qgrd,...kgd->...grqk', qg, k).astype(f32) * (SWA_HEAD_DIM ** -0.5)
    dist = jnp.arange(tq)[:, None] + WINDOW - jnp.arange(tk)[None, :]
    bias = rel_table[t5_bucket(dist)].astype(f32)
    bias = jnp.moveaxis(bias, -1, 0).reshape(SWA_KV_HEADS, SWA_REP, tq, tk)
    valid = (dist >= 0) & (dist < WINDOW) & kmask
    s = jnp.where(valid, s + bias, NEG_INF)
    sink = sinks.astype(f32).reshape(SWA_KV_HEADS, SWA_REP, 1, 1)
    m = jnp.maximum(jnp.max(s, axis=-1, keepdims=True), sink)
    e = jnp.exp(s - m)
    p = e / (jnp.sum(e, axis=-1, keepdims=True) + jnp.exp(sink - m))
    o = jnp.einsum('...grqk,...kgd->...qgrd', p.astype(v.dtype), v)
    return o.reshape(o.shape[:-3] + (SWA_WIDTH,))


def swa_prompt(q, k, v, sinks, rel_table):
    b, t = q.shape[:2]
    nblk = t // WINDOW
    qb = q.reshape(b, nblk, WINDOW, SWA_HEADS, SWA_HEAD_DIM)
    pad = jnp.zeros((b, WINDOW, SWA_KV_HEADS, SWA_HEAD_DIM), k.dtype)
    kb = jnp.concatenate([pad, k], axis=1).reshape(b, nblk + 1, WINDOW, SWA_KV_HEADS, SWA_HEAD_DIM)
    vb = jnp.concatenate([pad, v], axis=1).reshape(b, nblk + 1, WINDOW, SWA_KV_HEADS, SWA_HEAD_DIM)
    kk = jnp.concatenate([kb[:, :-1], kb[:, 1:]], axis=2)
    vv = jnp.concatenate([vb[:, :-1], vb[:, 1:]], axis=2)
    kpos = jnp.arange(nblk)[:, None] * WINDOW + jnp.arange(2 * WINDOW)[None, :] - WINDOW
    kmask = (kpos >= 0)[:, None, None, None, :]
    o = sink_band_attention(qb, kk, vv, kmask, sinks, rel_table)
    return o.reshape(b, t, SWA_WIDTH)


def mem_project(mem, lp):
    b, n = mem.shape[:2]
    kv = rmsnorm(mem, lp['mem_norm_g']) @ lp['w_mem_kv']
    k, v = jnp.split(kv, 2, axis=-1)
    return (k.reshape(b, n, MEM_HEADS, MEM_HEAD_DIM), v.reshape(b, n, MEM_HEADS, MEM_HEAD_DIM))


def mem_attention(q, k, v):
    s = jnp.einsum('bqhd,bkhd->bhqk', q, k).astype(jnp.float32) * (MEM_HEAD_DIM ** -0.5)
    p = jax.nn.softmax(s, axis=-1).astype(v.dtype)
    o = jnp.einsum('bhqk,bkhd->bqhd', p, v)
    return o.reshape(o.shape[:2] + (MEM_WIDTH,))


def hier_moe(x, lp):
    f32 = jnp.float32
    shp = x.shape
    xt = x.reshape(-1, D_MODEL)
    g_logits = (xt @ lp['w_rg']).astype(f32) + lp['b_rg'].astype(f32)
    g_prob = jax.nn.softmax(g_logits, axis=-1)
    g_w, g_idx = lax.top_k(g_prob, 1)
    e_logits = ((xt @ lp['w_rexp']).astype(f32) + lp['b_rexp'].astype(f32)).reshape(
        -1, N_EXPERT_GROUPS, EXPERTS_PER_GROUP)
    e_in = jnp.einsum('ng,nge->ne', jax.nn.one_hot(g_idx[:, 0], N_EXPERT_GROUPS, dtype=f32), e_logits)
    e_prob = jax.nn.softmax(e_in, axis=-1)
    e_w, e_idx = lax.top_k(e_prob, TOP_K_IN_GROUP)
    w = e_w / jnp.sum(e_w, axis=-1, keepdims=True) * g_w
    expert_id = g_idx * EXPERTS_PER_GROUP + e_idx
    combine = jnp.einsum('nk,nke->ne', w, jax.nn.one_hot(expert_id, N_EXPERTS, dtype=f32)).astype(x.dtype)
    y = jnp.zeros_like(xt)
    for e in range(N_EXPERTS):
        h = jax.nn.silu(xt @ lp['w_e_gate'][e]) * (xt @ lp['w_e_up'][e])
        y = y + combine[:, e:e + 1] * (h @ lp['w_e_down'][e])
    return y.reshape(shp)


def trunk_layer(x, h0_re, h0_im, swa_k_past, swa_v_past, mem_k, mem_v, lp, rel_table):
    b, t = x.shape[:2]
    xn = rmsnorm(x, lp['norm1_g'])
    split_at = [int(i) for i in np.cumsum(IN_SPLITS)]
    u, q, k, v, qm, gates = jnp.split(xn @ lp['w_in'], split_at, axis=-1)
    y_ssm, h_re, h_im = s5_scan(u, h0_re, h0_im, lp)
    z = jax.nn.gelu(y_ssm)
    z = z * jax.nn.sigmoid(z @ lp['w_glu'] + lp['b_glu'])
    q = q.reshape(b, t, SWA_HEADS, SWA_HEAD_DIM)
    k = k.reshape(b, t, SWA_KV_HEADS, SWA_HEAD_DIM)
    v = v.reshape(b, t, SWA_KV_HEADS, SWA_HEAD_DIM)
    if swa_k_past is None:
        o_swa = swa_prompt(q, k, v, lp['sinks'], rel_table)
        k_buf, v_buf = k[:, -WINDOW:], v[:, -WINDOW:]
    else:
        kk = jnp.concatenate([swa_k_past.astype(k.dtype), k], axis=1)
        vv = jnp.concatenate([swa_v_past.astype(v.dtype), v], axis=1)
        o_swa = sink_band_attention(q, kk, vv, True, lp['sinks'], rel_table)
        k_buf, v_buf = kk[:, -WINDOW:], vv[:, -WINDOW:]
    o_mem = mem_attention(qm.reshape(b, t, MEM_HEADS, MEM_HEAD_DIM), mem_k.astype(qm.dtype), mem_v.astype(qm.dtype))
    g = jax.nn.sigmoid(gates.reshape(b, t, N_BRANCHES, D_MODEL))
    merged = (g[:, :, 0] * (z @ lp['w_br_ssm'])
              + g[:, :, 1] * (o_swa @ lp['w_br_swa'])
              + g[:, :, 2] * (o_mem @ lp['w_br_mem']))
    x = x + merged @ lp['w_out']
    x = x + hier_moe(rmsnorm(x, lp['norm2_g']), lp)
    return x, h_re, h_im, k_buf, v_buf


def setup_inputs(seed: int = 0) -> dict:
    key = jax.random.key(seed)
    ks = jax.random.split(key, 38)
    f32 = jnp.float32

    def nrm(i, shape, scale=1.0):
        return jax.random.normal(ks[i], shape, f32) * scale

    n_idx = jnp.arange(SSM_STATE, dtype=f32)
    return {
        'x_prompt': nrm(0, (BATCH, SEQ, D_MODEL)),
        'x_sample': nrm(1, (DEC_BATCH, DEC_SEQ, D_MODEL)),
        'cache_swa_k': nrm(2, (DEPTH, DEC_BATCH, WINDOW, SWA_KV_HEADS, SWA_HEAD_DIM)),
        'cache_swa_v': nrm(3, (DEPTH, DEC_BATCH, WINDOW, SWA_KV_HEADS, SWA_HEAD_DIM)),
        'state_ssm_re': nrm(4, (DEPTH, DEC_BATCH, SSM_GROUPS, SSM_STATE), 0.3),
        'state_ssm_im': nrm(5, (DEPTH, DEC_BATCH, SSM_GROUPS, SSM_STATE), 0.3),
        'cache_mem_k': nrm(6, (DEPTH, DEC_BATCH, MEM_TOKENS, MEM_HEADS, MEM_HEAD_DIM)),
        'cache_mem_v': nrm(7, (DEPTH, DEC_BATCH, MEM_TOKENS, MEM_HEADS, MEM_HEAD_DIM)),
        'mem_prompt': nrm(8, (BATCH, MEM_TOKENS, D_MODEL)),
        'norm1_g': 1.0 + nrm(9, (DEPTH, D_MODEL), 0.01),
        'w_in': nrm(10, (DEPTH, D_MODEL, IN_COLS), D_MODEL ** -0.5),
        'lam_re': -0.5 + nrm(11, (DEPTH, SSM_GROUPS, SSM_STATE), 0.01),
        'lam_im': math.pi * n_idx + nrm(12, (DEPTH, SSM_GROUPS, SSM_STATE), 0.01),
        'log_dt': jax.random.uniform(ks[13], (DEPTH, SSM_GROUPS), f32, math.log(DT_MIN), math.log(DT_MAX)),
        'bm_re': nrm(14, (DEPTH, SSM_GROUPS, SSM_STATE, SSM_GROUP), (2 * SSM_GROUP) ** -0.5),
        'bm_im': nrm(15, (DEPTH, SSM_GROUPS, SSM_STATE, SSM_GROUP), (2 * SSM_GROUP) ** -0.5),
        'cm_re': nrm(16, (DEPTH, SSM_GROUPS, SSM_GROUP, SSM_STATE), (2 * SSM_STATE) ** -0.5),
        'cm_im': nrm(17, (DEPTH, SSM_GROUPS, SSM_GROUP, SSM_STATE), (2 * SSM_STATE) ** -0.5),
        'd_skip': nrm(18, (DEPTH, SSM_WIDTH)),
        'w_glu': nrm(19, (DEPTH, SSM_WIDTH, SSM_WIDTH), SSM_WIDTH ** -0.5),
        'b_glu': nrm(20, (DEPTH, SSM_WIDTH), 0.01),
        'sinks': nrm(21, (DEPTH, SWA_HEADS), 0.5),
        'rel_table': nrm(22, (REL_BUCKETS, SWA_HEADS), 0.1),
        'mem_norm_g': 1.0 + nrm(23, (DEPTH, D_MODEL), 0.01),
        'w_mem_kv': nrm(24, (DEPTH, D_MODEL, 2 * MEM_WIDTH), D_MODEL ** -0.5),
        'w_br_ssm': nrm(25, (DEPTH, SSM_WIDTH, D_MODEL), SSM_WIDTH ** -0.5),
        'w_br_swa': nrm(26, (DEPTH, SWA_WIDTH, D_MODEL), SWA_WIDTH ** -0.5),
        'w_br_mem': nrm(27, (DEPTH, MEM_WIDTH, D_MODEL), MEM_WIDTH ** -0.5),
        'w_out': nrm(28, (DEPTH, D_MODEL, D_MODEL), 0.5 * D_MODEL ** -0.5),
        'norm2_g': 1.0 + nrm(29, (DEPTH, D_MODEL), 0.01),
        'w_rg': nrm(30, (DEPTH, D_MODEL, N_EXPERT_GROUPS), D_MODEL ** -0.5),
        'b_rg': nrm(31, (DEPTH, N_EXPERT_GROUPS), 0.01),
        'w_rexp': nrm(32, (DEPTH, D_MODEL, N_EXPERTS), D_MODEL ** -0.5),
        'b_rexp': nrm(33, (DEPTH, N_EXPERTS), 0.01),
        'w_e_gate': nrm(34, (DEPTH, N_EXPERTS, D_MODEL, D_EXPERT), D_MODEL ** -0.5),
        'w_e_up': nrm(35, (DEPTH, N_EXPERTS, D_MODEL, D_EXPERT), D_MODEL ** -0.5),
        'w_e_down': nrm(36, (DEPTH, N_EXPERTS, D_EXPERT, D_MODEL), D_EXPERT ** -0.5),
        'final_norm_g': 1.0 + nrm(37, (D_MODEL,), 0.01),
    }


def reference(x_prompt, x_sample, cache_swa_k, cache_swa_v, state_ssm_re, state_ssm_im,
              cache_mem_k, cache_mem_v, mem_prompt,
              norm1_g, w_in, lam_re, lam_im, log_dt, bm_re, bm_im, cm_re, cm_im, d_skip,
              w_glu, b_glu, sinks, rel_table, mem_norm_g, w_mem_kv, w_br_ssm, w_br_swa, w_br_mem,
              w_out, norm2_g, w_rg, b_rg, w_rexp, b_rexp, w_e_gate, w_e_up, w_e_down, final_norm_g):
    layer_params = {
        'norm1_g': norm1_g, 'w_in': w_in, 'lam_re': lam_re, 'lam_im': lam_im, 'log_dt': log_dt,
        'bm_re': bm_re, 'bm_im': bm_im, 'cm_re': cm_re, 'cm_im': cm_im, 'd_skip': d_skip,
        'w_glu': w_glu, 'b_glu': b_glu, 'sinks': sinks, 'mem_norm_g': mem_norm_g, 'w_mem_kv': w_mem_kv,
        'w_br_ssm': w_br_ssm, 'w_br_swa': w_br_swa, 'w_br_mem': w_br_mem, 'w_out': w_out,
        'norm2_g': norm2_g, 'w_rg': w_rg, 'b_rg': b_rg, 'w_rexp': w_rexp, 'b_rexp': b_rexp,
        'w_e_gate': w_e_gate, 'w_e_up': w_e_up, 'w_e_down': w_e_down,
    }
    xp, xs = x_prompt, x_sample
    zero_state = jnp.zeros((x_prompt.shape[0], SSM_GROUPS, SSM_STATE), jnp.float32)
    pk, pv, pre, pim, pmk, pmv = [], [], [], [], [], []
    sk, sv, sre, sim = [], [], [], []
    for l in range(DEPTH):
        lp = {name: arr[l] for name, arr in layer_params.items()}
        mk, mv = mem_project(mem_prompt, lp)
        xp, hr, hi, kb, vb = trunk_layer(xp, zero_state, zero_state, None, None, mk, mv, lp, rel_table)
        xs, hrs, his, kbs, vbs = trunk_layer(xs, state_ssm_re[l], state_ssm_im[l], cache_swa_k[l], cache_swa_v[l],
                                             cache_mem_k[l], cache_mem_v[l], lp, rel_table)
        pk.append(kb); pv.append(vb)
        pre.append(hr.astype(x_prompt.dtype)); pim.append(hi.astype(x_prompt.dtype))
        pmk.append(mk); pmv.append(mv)
        sk.append(kbs.astype(cache_swa_k.dtype)); sv.append(vbs.astype(cache_swa_v.dtype))
        sre.append(hrs.astype(state_ssm_re.dtype)); sim.append(his.astype(state_ssm_im.dtype))
    y_prompt = rmsnorm(xp, final_norm_g)
    y_sample = rmsnorm(xs, final_norm_g)
    new_swa_k_prompt = jnp.stack(pk, axis=0)
    new_swa_v_prompt = jnp.stack(pv, axis=0)
    new_ssm_re_prompt = jnp.stack(pre, axis=0)
    new_ssm_im_prompt = jnp.stack(pim, axis=0)
    new_mem_k_prompt = jnp.stack(pmk, axis=0)
    new_mem_v_prompt = jnp.stack(pmv, axis=0)
    new_swa_k_sample = jnp.stack(sk, axis=0)
    new_swa_v_sample = jnp.stack(sv, axis=0)
    new_ssm_re_sample = jnp.stack(sre, axis=0)
    new_ssm_im_sample = jnp.stack(sim, axis=0)
    return (y_prompt, y_sample,
            new_swa_k_prompt, new_swa_v_prompt, new_ssm_re_prompt, new_ssm_im_prompt,
            new_mem_k_prompt, new_mem_v_prompt,
            new_swa_k_sample, new_swa_v_sample, new_ssm_re_sample, new_ssm_im_sample)
```

```python
import functools
import math

import jax
import jax.numpy as jnp
from jax import lax
from jax.experimental import pallas as pl
from jax.experimental.pallas import tpu as pltpu

F32 = jnp.float32
BF16 = jnp.bfloat16

D_MODEL = 1024
SSM_WIDTH = 512
SSM_GROUP = 16
SSM_GROUPS = 32
SSM_STATE = 64
SWA_HEADS = 8
SWA_KV_HEADS = 2
SWA_REP = 4
SWA_HEAD_DIM = 64
SWA_WIDTH = 512
SWA_KV_WIDTH = 128
WINDOW = 128
REL_BUCKETS = 32
REL_MAX_DIST = 128
MEM_TOKENS = 256
MEM_HEADS = 4
MEM_HEAD_DIM = 128
MEM_WIDTH = 512
N_EXPERT_GROUPS = 4
EXPERTS_PER_GROUP = 8
N_EXPERTS = 32
D_EXPERT = 256
EPS = 1e-6
NEG_INF = -1e30

LANES = 128
GROUPS_PER_TILE = LANES // SSM_GROUP
N_CH_TILES = SSM_WIDTH // LANES
STATE_TILE = GROUPS_PER_TILE * SSM_STATE
VMEM_LIMIT = 56 * 1024 * 1024

_TRANS_B = (((1,), (1,)), ((), ()))


def _cparams(*sem):
    return pltpu.CompilerParams(dimension_semantics=sem, vmem_limit_bytes=VMEM_LIMIT)


def _rms(x, g):
    return (x * lax.rsqrt(jnp.mean(x * x, axis=-1, keepdims=True) + EPS)) * g


def _dot(a, b):
    return jnp.dot(a, b, preferred_element_type=F32)


def _norm_proj_kernel(x_ref, g_ref, w_ref, *out_refs, splits, dtypes):
    xb = _rms(x_ref[...], g_ref[...]).astype(BF16)
    off = 0
    outs = iter(out_refs)
    for width, dts in zip(splits, dtypes):
        r = _dot(xb, w_ref[:, off:off + width])
        for dt in dts:
            next(outs)[...] = r.astype(dt)
        off += width


def _norm_proj(x, g, w, splits, dtypes, tile):
    n, d = x.shape
    tile = min(tile, n)
    flat = [(wd, dt) for wd, dts in zip(splits, dtypes) for dt in dts]
    return pl.pallas_call(
        functools.partial(_norm_proj_kernel, splits=tuple(splits), dtypes=tuple(dtypes)),
        grid=(n // tile,),
        in_specs=[pl.BlockSpec((tile, d), lambda i: (i, 0)),
                  pl.BlockSpec((1, d), lambda i: (0, 0)),
                  pl.BlockSpec((d, sum(splits)), lambda i: (0, 0))],
        out_specs=[pl.BlockSpec((tile, wd), lambda i: (i, 0)) for wd, _ in flat],
        out_shape=[jax.ShapeDtypeStruct((n, wd), dt) for wd, dt in flat],
        compiler_params=_cparams("parallel"),
        name="norm_proj",
    )(x, g, w)


def _s5_weights(lam_re, lam_im, log_dt, bm_re, bm_im, cm_re, cm_im, L):
    hp = lax.Precision.HIGHEST
    lr, li = lam_re.astype(F32), lam_im.astype(F32)
    dt = jnp.exp(log_dt.astype(F32))[:, None]
    mag = jnp.exp(lr * dt)
    a_re = mag * jnp.cos(li * dt)
    a_im = mag * jnp.sin(li * dt)
    den = lr * lr + li * li
    f_re = ((a_re - 1.0) * lr + a_im * li) / den
    f_im = (a_im * lr - (a_re - 1.0) * li) / den
    br, bi = bm_re.astype(F32), bm_im.astype(F32)
    bb_re = f_re[..., None] * br - f_im[..., None] * bi
    bb_im = f_re[..., None] * bi + f_im[..., None] * br
    pr, pi = [jnp.ones_like(a_re)], [jnp.zeros_like(a_im)]
    for _ in range(L):
        pr.append(pr[-1] * a_re - pi[-1] * a_im)
        pi.append(pr[-2] * a_im + pi[-1] * a_re)
    ap_re, ap_im = jnp.stack(pr), jnp.stack(pi)
    cr, ci = cm_re.astype(F32), cm_im.astype(F32)
    ca_re = cr[None] * ap_re[:, :, None, :] - ci[None] * ap_im[:, :, None, :]
    ca_im = cr[None] * ap_im[:, :, None, :] + ci[None] * ap_re[:, :, None, :]
    eye = jnp.eye(GROUPS_PER_TILE, dtype=F32)
    nt, gt, P, H = N_CH_TILES, GROUPS_PER_TILE, SSM_STATE, SSM_GROUP

    rev_re = jnp.stack([pr[L - 1 - s] for s in range(L)])
    rev_im = jnp.stack([pi[L - 1 - s] for s in range(L)])
    ws_re = rev_re[..., None] * bb_re[None] - rev_im[..., None] * bb_im[None]
    ws_im = rev_re[..., None] * bb_im[None] + rev_im[..., None] * bb_re[None]

    def st_in(w):
        w = w.reshape(L, nt, gt, P, H).transpose(1, 0, 2, 4, 3)
        w = w[:, :, :, :, None, :] * eye[None, None, :, None, :, None]
        return w.reshape(nt, L * LANES, STATE_TILE)

    w_st = jnp.concatenate([st_in(ws_re), st_in(ws_im)], axis=-1).astype(BF16)

    def st_out(w):
        w = w.reshape(L, nt, gt, H, P).transpose(1, 2, 4, 0, 3)
        w = w[:, :, :, :, None, :] * eye[None, :, None, None, :, None]
        return w.reshape(nt, STATE_TILE, L * LANES)

    w_out = jnp.concatenate([st_out(ca_re[1:]), st_out(-ca_im[1:])], axis=1).astype(BF16)

    k_lag = (jnp.einsum('tghp,gpk->tghk', ca_re[:L], bb_re, precision=hp)
             - jnp.einsum('tghp,gpk->tghk', ca_im[:L], bb_im, precision=hp))
    lag = jnp.arange(L)[None, :] - jnp.arange(L)[:, None]
    k_st = jnp.where((lag >= 0)[:, :, None, None, None], k_lag[jnp.clip(lag, 0, L - 1)], 0.0)
    k_st = k_st.reshape(L, L, nt, gt, H, H).transpose(2, 0, 3, 5, 1, 4)
    toep = k_st[:, :, :, :, :, None, :] * eye[None, None, :, None, None, :, None]
    toep = toep.reshape(nt, L * LANES, L * LANES).astype(BF16)

    def per_tile(v):
        return v.reshape(nt, 1, STATE_TILE)

    return w_st, w_out, toep, per_tile(ap_re[L]), per_tile(ap_im[L])


def _to_chunks(u, nb, nc, L):
    return (u.reshape(nb, nc, L, N_CH_TILES, LANES).transpose(0, 1, 3, 2, 4)
            .reshape(nb * nc, N_CH_TILES * L * LANES))


def _from_chunks(y, nb, nc, L):
    return (y.reshape(nb, nc, N_CH_TILES, L, LANES).transpose(0, 1, 3, 2, 4)
            .reshape(nb * nc * L, SSM_WIDTH))


def _s5_state_in_kernel(x_ref, w_ref, d_ref):
    d_ref[...] = _dot(x_ref[...], w_ref[...])


def _s5_state_in(xc, w_st, L, row_tile):
    rows = xc.shape[0]
    row_tile = min(row_tile, rows)
    lk = L * LANES
    return pl.pallas_call(
        _s5_state_in_kernel,
        grid=(N_CH_TILES, rows // row_tile),
        in_specs=[pl.BlockSpec((row_tile, lk), lambda j, r: (r, j)),
                  pl.BlockSpec((None, lk, 2 * STATE_TILE), lambda j, r: (j, 0, 0))],
        out_specs=pl.BlockSpec((row_tile, 2 * STATE_TILE), lambda j, r: (r, j)),
        out_shape=jax.ShapeDtypeStruct((rows, N_CH_TILES * 2 * STATE_TILE), F32),
        compiler_params=_cparams("parallel", "parallel"),
        name="s5_state_in",
    )(xc, w_st)


def _s5_scan_kernel(d_ref, are_ref, aim_ref, hs_ref, fin_ref, *, nc, nb):
    ar = jnp.broadcast_to(are_ref[...], (nb, STATE_TILE))
    ai = jnp.broadcast_to(aim_ref[...], (nb, STATE_TILE))

    def body(c, carry):
        hr, hi = carry
        hs_ref[c, :, 0:STATE_TILE] = hr
        hs_ref[c, :, STATE_TILE:2 * STATE_TILE] = hi
        d = d_ref[c]
        nr = ar * hr - ai * hi + d[:, 0:STATE_TILE]
        ni = ar * hi + ai * hr + d[:, STATE_TILE:2 * STATE_TILE]
        return nr, ni

    zero = jnp.zeros((nb, STATE_TILE), F32)
    hr, hi = lax.fori_loop(0, nc, body, (zero, zero))
    fin_ref[:, 0:STATE_TILE] = hr
    fin_ref[:, STATE_TILE:2 * STATE_TILE] = hi


def _s5_scan(d_tm, a_re, a_im):
    nc, nb, _ = d_tm.shape
    st2 = 2 * STATE_TILE
    return pl.pallas_call(
        functools.partial(_s5_scan_kernel, nc=nc, nb=nb),
        grid=(N_CH_TILES,),
        in_specs=[pl.BlockSpec((nc, nb, st2), lambda j: (0, 0, j)),
                  pl.BlockSpec((None, 1, STATE_TILE), lambda j: (j, 0, 0)),
                  pl.BlockSpec((None, 1, STATE_TILE), lambda j: (j, 0, 0))],
        out_specs=[pl.BlockSpec((nc, nb, st2), lambda j: (0, 0, j)),
                   pl.BlockSpec((None, nb, st2), lambda j: (j, 0, 0))],
        out_shape=[jax.ShapeDtypeStruct((nc, nb, N_CH_TILES * st2), F32),
                   jax.ShapeDtypeStruct((N_CH_TILES, nb, st2), F32)],
        compiler_params=_cparams("parallel"),
        name="s5_scan",
    )(d_tm, a_re, a_im)


def _s5_out_kernel(x_ref, h_ref, t_ref, wo_ref, y_ref):
    y_ref[...] = _dot(x_ref[...], t_ref[...]) + _dot(h_ref[...].astype(BF16), wo_ref[...])


def _s5_out(xc, hs, toep, w_out, L, row_tile):
    rows = xc.shape[0]
    row_tile = min(row_tile, rows)
    lk = L * LANES
    st2 = 2 * STATE_TILE
    return pl.pallas_call(
        _s5_out_kernel,
        grid=(N_CH_TILES, rows // row_tile),
        in_specs=[pl.BlockSpec((row_tile, lk), lambda j, r: (r, j)),
                  pl.BlockSpec((row_tile, st2), lambda j, r: (r, j)),
                  pl.BlockSpec((None, lk, lk), lambda j, r: (j, 0, 0)),
                  pl.BlockSpec((None, st2, lk), lambda j, r: (j, 0, 0))],
        out_specs=pl.BlockSpec((row_tile, lk), lambda j, r: (r, j)),
        out_shape=jax.ShapeDtypeStruct((rows, N_CH_TILES * lk), F32),
        compiler_params=_cparams("parallel", "parallel"),
        name="s5_out",
    )(xc, hs, toep, w_out)


def _s5_single_chunk_kernel(x_ref, h0_ref, are_ref, aim_ref, ws_ref, t_ref, wo_ref, y_ref, fin_ref):
    x = x_ref[...]
    h0 = h0_ref[...]
    hr, hi = h0[:, 0:STATE_TILE], h0[:, STATE_TILE:2 * STATE_TILE]
    ar, ai = are_ref[...], aim_ref[...]
    d = _dot(x, ws_ref[...])
    fin_ref[:, 0:STATE_TILE] = ar * hr - ai * hi + d[:, 0:STATE_TILE]
    fin_ref[:, STATE_TILE:2 * STATE_TILE] = ar * hi + ai * hr + d[:, STATE_TILE:2 * STATE_TILE]
    y_ref[...] = _dot(x, t_ref[...]) + _dot(h0.astype(BF16), wo_ref[...])


def _s5_single_chunk(xc, h0, a_re, a_im, w_st, toep, w_out, L):
    rows = xc.shape[0]
    lk = L * LANES
    st2 = 2 * STATE_TILE
    return pl.pallas_call(
        _s5_single_chunk_kernel,
        grid=(N_CH_TILES,),
        in_specs=[pl.BlockSpec((rows, lk), lambda j: (0, j)),
                  pl.BlockSpec((rows, st2), lambda j: (0, j)),
                  pl.BlockSpec((None, 1, STATE_TILE), lambda j: (j, 0, 0)),
                  pl.BlockSpec((None, 1, STATE_TILE), lambda j: (j, 0, 0)),
                  pl.BlockSpec((None, lk, st2), lambda j: (j, 0, 0)),
                  pl.BlockSpec((None, lk, lk), lambda j: (j, 0, 0)),
                  pl.BlockSpec((None, st2, lk), lambda j: (j, 0, 0))],
        out_specs=[pl.BlockSpec((rows, lk), lambda j: (0, j)),
                   pl.BlockSpec((rows, st2), lambda j: (0, j))],
        out_shape=[jax.ShapeDtypeStruct((rows, N_CH_TILES * lk), F32),
                   jax.ShapeDtypeStruct((rows, N_CH_TILES * st2), F32)],
        compiler_params=_cparams("parallel"),
        name="s5_single_chunk",
    )(xc, h0, a_re, a_im, w_st, toep, w_out)


def _state_to_tiles(h_re, h_im):
    nb = h_re.shape[0]
    r = h_re.astype(F32).reshape(nb, N_CH_TILES, STATE_TILE)
    i = h_im.astype(F32).reshape(nb, N_CH_TILES, STATE_TILE)
    return jnp.concatenate([r, i], axis=-1).reshape(nb, N_CH_TILES * 2 * STATE_TILE)


def _tiles_to_state(h):
    nb = h.shape[0]
    h = h.reshape(nb, N_CH_TILES, 2, GROUPS_PER_TILE, SSM_STATE)
    return (h[:, :, 0].reshape(nb, SSM_GROUPS, SSM_STATE), h[:, :, 1].reshape(nb, SSM_GROUPS, SSM_STATE))


def _t5_bucket(dist):
    n = jnp.maximum(dist, 0)
    max_exact = REL_BUCKETS // 2
    nf = jnp.maximum(n, 1).astype(F32)
    large = max_exact + (jnp.log(nf / max_exact) / math.log(REL_MAX_DIST / max_exact)
                         * (REL_BUCKETS - max_exact)).astype(jnp.int32)
    large = jnp.minimum(large, REL_BUCKETS - 1)
    return jnp.where(n < max_exact, n, large)


def _softmax_sink(s, sink):
    m = jnp.maximum(jnp.max(s, axis=-1, keepdims=True), sink)
    e = jnp.exp(s - m)
    den = jnp.sum(e, axis=-1, keepdims=True) + jnp.exp(sink - m)
    return e * (1.0 / den)


def _swa_prompt_kernel(sink_ref, q_ref, kp_ref, kc_ref, vp_ref, vc_ref, bias_ref, o_ref):
    blk = pl.program_id(1)
    kk = jnp.concatenate([kp_ref[...], kc_ref[...]], axis=0).astype(BF16)
    vv = jnp.concatenate([vp_ref[...], vc_ref[...]], axis=0).astype(BF16)
    row = lax.broadcasted_iota(jnp.int32, (WINDOW, 2 * WINDOW), 0)
    col = lax.broadcasted_iota(jnp.int32, (WINDOW, 2 * WINDOW), 1)
    dist = row + WINDOW - col
    valid = (dist >= 0) & (dist < WINDOW) & ((col >= WINDOW) | (blk > 0))
    outs = []
    for h in range(SWA_HEADS):
        s = lax.dot_general(q_ref[:, h * LANES:(h + 1) * LANES], kk, _TRANS_B, preferred_element_type=F32)
        s = jnp.where(valid, s + bias_ref[h], NEG_INF)
        p = _softmax_sink(s, sink_ref[h]).astype(BF16)
        outs.append(_dot(p, vv))
    lane = lax.broadcasted_iota(jnp.int32, (WINDOW, LANES), 1)
    for t in range(SWA_REP):
        o_ref[:, t * LANES:(t + 1) * LANES] = jnp.where(lane < SWA_HEAD_DIM, outs[t], outs[t + SWA_REP]).astype(BF16)


def _swa_prompt(qz, k, v, bias, sinks, nb, t):
    nblk = t // WINDOW
    cur = lambda b, i: (b * nblk + i, 0)
    prev = lambda b, i: (b * nblk + jnp.maximum(i - 1, 0), 0)
    return pl.pallas_call(
        _swa_prompt_kernel,
        grid=(nb, nblk),
        in_specs=[pl.BlockSpec(memory_space=pltpu.SMEM),
                  pl.BlockSpec((WINDOW, SWA_HEADS * LANES), cur),
                  pl.BlockSpec((WINDOW, SWA_KV_WIDTH), prev),
                  pl.BlockSpec((WINDOW, SWA_KV_WIDTH), cur),
                  pl.BlockSpec((WINDOW, SWA_KV_WIDTH), prev),
                  pl.BlockSpec((WINDOW, SWA_KV_WIDTH), cur),
                  pl.BlockSpec((SWA_HEADS, WINDOW, 2 * WINDOW), lambda b, i: (0, 0, 0))],
        out_specs=pl.BlockSpec((WINDOW, SWA_WIDTH), cur),
        out_shape=jax.ShapeDtypeStruct((nb * t, SWA_WIDTH), BF16),
        compiler_params=_cparams("parallel", "parallel"),
        name="swa_prompt",
    )(sinks, qz, k, k, v, v, bias)


def _swa_decode_kernel(q_ref, k_ref, v_ref, bias_ref, sink_ref, o_ref, *, seqs, tq):
    rows, keys = q_ref.shape[1], k_ref.shape[1]
    qi = lax.broadcasted_iota(jnp.int32, (rows, keys), 0) % tq
    col = lax.broadcasted_iota(jnp.int32, (rows, keys), 1)
    dist = qi + WINDOW - col
    valid = (dist >= 0) & (dist < WINDOW)
    bias = bias_ref[...]
    sink = sink_ref[...]
    for s_i in range(seqs):
        kk = k_ref[s_i].astype(BF16)
        s = lax.dot_general(q_ref[s_i], kk, _TRANS_B, preferred_element_type=F32)
        s = jnp.where(valid, s + bias, NEG_INF)
        p = _softmax_sink(s, sink).astype(BF16)
        o_ref[s_i] = _dot(p, v_ref[s_i].astype(BF16))


def _swa_decode(qz, k_all, v_all, bias, sink_rows, tq, seqs):
    nseq, rows, _ = qz.shape
    keys = k_all.shape[1]
    return pl.pallas_call(
        functools.partial(_swa_decode_kernel, seqs=seqs, tq=tq),
        grid=(nseq // seqs,),
        in_specs=[pl.BlockSpec((seqs, rows, LANES), lambda i: (i, 0, 0)),
                  pl.BlockSpec((seqs, keys, LANES), lambda i: (i, 0, 0)),
                  pl.BlockSpec((seqs, keys, LANES), lambda i: (i, 0, 0)),
                  pl.BlockSpec((rows, keys), lambda i: (0, 0)),
                  pl.BlockSpec((rows, 1), lambda i: (0, 0))],
        out_specs=pl.BlockSpec((seqs, rows, LANES), lambda i: (i, 0, 0)),
        out_shape=jax.ShapeDtypeStruct((nseq, rows, LANES), F32),
        compiler_params=_cparams("parallel"),
        name="swa_decode",
    )(qz, k_all, v_all, bias, sink_rows)


def _softmax(s):
    m = jnp.max(s, axis=-1, keepdims=True)
    e = jnp.exp(s - m)
    return e * (1.0 / jnp.sum(e, axis=-1, keepdims=True))


def _mem_prompt_kernel(q_ref, k_ref, v_ref, o_ref):
    scale = MEM_HEAD_DIM ** -0.5
    for h in range(MEM_HEADS):
        sl = slice(h * MEM_HEAD_DIM, (h + 1) * MEM_HEAD_DIM)
        s = lax.dot_general(q_ref[:, sl], k_ref[:, sl].astype(BF16), _TRANS_B, preferred_element_type=F32) * scale
        p = _softmax(s).astype(BF16)
        o_ref[:, sl] = _dot(p, v_ref[:, sl].astype(BF16)).astype(BF16)


def _mem_prompt(qm, mk, mv, nb, t, tile):
    tile = min(tile, t)
    nt = t // tile
    return pl.pallas_call(
        _mem_prompt_kernel,
        grid=(nb, nt),
        in_specs=[pl.BlockSpec((tile, MEM_WIDTH), lambda b, i: (b * nt + i, 0)),
                  pl.BlockSpec((MEM_TOKENS, MEM_WIDTH), lambda b, i: (b, 0)),
                  pl.BlockSpec((MEM_TOKENS, MEM_WIDTH), lambda b, i: (b, 0))],
        out_specs=pl.BlockSpec((tile, MEM_WIDTH), lambda b, i: (b * nt + i, 0)),
        out_shape=jax.ShapeDtypeStruct((nb * t, MEM_WIDTH), BF16),
        compiler_params=_cparams("parallel", "parallel"),
        name="mem_prompt",
    )(qm, mk, mv)


def _mem_decode_kernel(q_ref, k_ref, v_ref, o_ref, *, seqs):
    scale = MEM_HEAD_DIM ** -0.5
    for s_i in range(seqs):
        s = lax.dot_general(q_ref[s_i], k_ref[s_i].astype(BF16), _TRANS_B, preferred_element_type=F32) * scale
        p = _softmax(s).astype(BF16)
        o_ref[s_i] = _dot(p, v_ref[s_i].astype(BF16)).astype(BF16)


def _mem_decode(qb, k, v, seqs):
    nseq, rows, _ = qb.shape
    return pl.pallas_call(
        functools.partial(_mem_decode_kernel, seqs=seqs),
        grid=(nseq // seqs,),
        in_specs=[pl.BlockSpec((seqs, rows, MEM_WIDTH), lambda i: (i, 0, 0)),
                  pl.BlockSpec((seqs, MEM_TOKENS, MEM_WIDTH), lambda i: (i, 0, 0)),
                  pl.BlockSpec((seqs, MEM_TOKENS, MEM_WIDTH), lambda i: (i, 0, 0))],
        out_specs=pl.BlockSpec((seqs, rows, MEM_WIDTH), lambda i: (i, 0, 0)),
        out_shape=jax.ShapeDtypeStruct((nseq, rows, MEM_WIDTH), BF16),
        compiler_params=_cparams("parallel"),
        name="mem_decode",
    )(qb, k, v)


ROUTER_ROWS = 40


def _merge_kernel(x_ref, u_ref, y_ref, os_ref, om_ref, g1_ref, wg_ref, dsk_ref, wglu_ref, bglu_ref,
                  wbs_ref, wbw_ref, wbm_ref, wout_ref, g2_ref, wr_ref, br_ref,
                  h_ref, xn2_ref, comb_ref):
    x = x_ref[...]
    tt = x.shape[0]
    xb = _rms(x, g1_ref[...]).astype(BF16)
    z = jax.nn.gelu(y_ref[...] + dsk_ref[...] * u_ref[...])
    z = z * jax.nn.sigmoid(_dot(z.astype(BF16), wglu_ref[...]) + bglu_ref[...])
    merged = jax.nn.sigmoid(_dot(xb, wg_ref[:, 0:D_MODEL])) * _dot(z.astype(BF16), wbs_ref[...])
    merged = merged + jax.nn.sigmoid(_dot(xb, wg_ref[:, D_MODEL:2 * D_MODEL])) * _dot(os_ref[...], wbw_ref[...])
    merged = merged + jax.nn.sigmoid(_dot(xb, wg_ref[:, 2 * D_MODEL:3 * D_MODEL])) * _dot(om_ref[...], wbm_ref[...])
    h = x + _dot(merged.astype(BF16), wout_ref[...])
    h_ref[...] = h
    xn2 = _rms(h, g2_ref[...]).astype(BF16)
    xn2_ref[...] = xn2

    lt = lax.dot_general(wr_ref[...], xn2, _TRANS_B, preferred_element_type=F32) + br_ref[...]
    gl = lt[N_EXPERTS:N_EXPERTS + N_EXPERT_GROUPS]
    ge = jnp.exp(gl - jnp.max(gl, axis=0, keepdims=True))
    gp = ge / jnp.sum(ge, axis=0, keepdims=True)
    gw = jnp.max(gp, axis=0, keepdims=True)
    gidx = jnp.full((1, tt), N_EXPERT_GROUPS - 1, jnp.int32)
    for r in range(N_EXPERT_GROUPS - 2, -1, -1):
        gidx = jnp.where(gp[r:r + 1] == gw, r, gidx)
    ein = lt[(N_EXPERT_GROUPS - 1) * EXPERTS_PER_GROUP:N_EXPERTS]
    for r in range(N_EXPERT_GROUPS - 2, -1, -1):
        ein = jnp.where(gidx == r, lt[r * EXPERTS_PER_GROUP:(r + 1) * EXPERTS_PER_GROUP], ein)
    ee = jnp.exp(ein - jnp.max(ein, axis=0, keepdims=True))
    ep = ee / jnp.sum(ee, axis=0, keepdims=True)
    rowi = lax.broadcasted_iota(jnp.int32, (EXPERTS_PER_GROUP, tt), 0)
    p1 = jnp.max(ep, axis=0, keepdims=True)
    e1 = jnp.min(jnp.where(ep == p1, rowi, EXPERTS_PER_GROUP), axis=0, keepdims=True)
    ep2 = jnp.where(rowi == e1, -1.0, ep)
    p2 = jnp.max(ep2, axis=0, keepdims=True)
    e2 = jnp.min(jnp.where(ep2 == p2, rowi, EXPERTS_PER_GROUP), axis=0, keepdims=True)
    tot = p1 + p2
    w1 = p1 / tot * gw
    w2 = p2 / tot * gw
    r32 = lax.broadcasted_iota(jnp.int32, (N_EXPERTS, tt), 0)
    comb_ref[...] = (jnp.where(r32 == gidx * EXPERTS_PER_GROUP + e1, w1, 0.0)
                     + jnp.where(r32 == gidx * EXPERTS_PER_GROUP + e2, w2, 0.0))


def _merge(x, u, y, o_swa, o_mem, p, tile):
    n = x.shape[0]
    tile = min(tile, n)
    row = lambda i: (i, 0)
    const = lambda i: (0, 0)
    full = lambda a: pl.BlockSpec(a.shape, const)
    weights = [p['g1'], p['w_gates'], p['d_skip'], p['w_glu'], p['b_glu'], p['w_br_ssm'], p['w_br_swa'],
               p['w_br_mem'], p['w_out'], p['g2'], p['w_router'], p['b_router']]
    return pl.pallas_call(
        _merge_kernel,
        grid=(n // tile,),
        in_specs=[pl.BlockSpec((tile, D_MODEL), row), pl.BlockSpec((tile, SSM_WIDTH), row),
                  pl.BlockSpec((tile, SSM_WIDTH), row), pl.BlockSpec((tile, SWA_WIDTH), row),
                  pl.BlockSpec((tile, MEM_WIDTH), row)] + [full(w) for w in weights],
        out_specs=[pl.BlockSpec((tile, D_MODEL), row), pl.BlockSpec((tile, D_MODEL), row),
                   pl.BlockSpec((N_EXPERTS, tile), lambda i: (0, i))],
        out_shape=[jax.ShapeDtypeStruct((n, D_MODEL), F32), jax.ShapeDtypeStruct((n, D_MODEL), BF16),
                   jax.ShapeDtypeStruct((N_EXPERTS, n), F32)],
        compiler_params=_cparams("parallel"),
        name="merge_router",
    )(x, u, y, o_swa, o_mem, *weights)


def _moe_kernel(xn2_ref, comb_ref, wgu_ref, wd_ref, h_ref, gf_ref, o_ref, acc_ref):
    e = pl.program_id(1)

    @pl.when(e == 0)
    def _():
        acc_ref[...] = jnp.zeros_like(acc_ref)

    hgu = _dot(xn2_ref[...], wgu_ref[...])
    hh = jax.nn.silu(hgu[:, 0:D_EXPERT]) * hgu[:, D_EXPERT:2 * D_EXPERT]
    o = _dot(hh.astype(BF16), wd_ref[...])
    lane = lax.broadcasted_iota(jnp.int32, comb_ref.shape, 1)
    c = jnp.sum(jnp.where(lane == e, comb_ref[...], 0.0), axis=1, keepdims=True)
    acc_ref[...] += c * o

    @pl.when(e == N_EXPERTS - 1)
    def _():
        o_ref[...] = _rms(h_ref[...] + acc_ref[...], gf_ref[...])


def _moe(xn2, comb, w_gu, w_d, h, gf, tile):
    n = h.shape[0]
    tile = min(tile, n)
    return pl.pallas_call(
        _moe_kernel,
        grid=(n // tile, N_EXPERTS),
        in_specs=[pl.BlockSpec((tile, D_MODEL), lambda i, e: (i, 0)),
                  pl.BlockSpec((tile, N_EXPERTS), lambda i, e: (i, 0)),
                  pl.BlockSpec((None, D_MODEL, 2 * D_EXPERT), lambda i, e: (e, 0, 0)),
                  pl.BlockSpec((None, D_EXPERT, D_MODEL), lambda i, e: (e, 0, 0)),
                  pl.BlockSpec((tile, D_MODEL), lambda i, e: (i, 0)),
                  pl.BlockSpec((1, D_MODEL), lambda i, e: (0, 0))],
        out_specs=pl.BlockSpec((tile, D_MODEL), lambda i, e: (i, 0)),
        out_shape=jax.ShapeDtypeStruct((n, D_MODEL), F32),
        scratch_shapes=[pltpu.VMEM((tile, D_MODEL), F32)],
        compiler_params=_cparams("parallel", "arbitrary"),
        name="moe_final_norm",
    )(xn2, comb, w_gu, w_d, h, gf)


PROMPT_CHUNK = 16


def _prep_in_weights(w_in):
    o = 0
    w_u = w_in[:, o:o + SSM_WIDTH]; o += SSM_WIDTH
    w_q = w_in[:, o:o + SWA_WIDTH]; o += SWA_WIDTH
    w_k = w_in[:, o:o + SWA_KV_WIDTH]; o += SWA_KV_WIDTH
    w_v = w_in[:, o:o + SWA_KV_WIDTH]; o += SWA_KV_WIDTH
    w_qm = w_in[:, o:o + MEM_WIDTH]; o += MEM_WIDTH
    w_g = w_in[:, o:]
    wq = (w_q * (SWA_HEAD_DIM ** -0.5)).reshape(D_MODEL, SWA_KV_HEADS, SWA_REP, 1, SWA_HEAD_DIM)
    sel = jnp.eye(SWA_KV_HEADS, dtype=w_q.dtype).reshape(1, SWA_KV_HEADS, 1, SWA_KV_HEADS, 1)
    w_qz = (wq * sel).reshape(D_MODEL, SWA_HEADS * LANES)
    w_main = jnp.concatenate([w_u, w_qz, w_k, w_v, w_qm], axis=1).astype(BF16)
    return w_main, w_g.astype(BF16)


IN_SPLITS = (SSM_WIDTH, SWA_HEADS * LANES, SWA_KV_WIDTH, SWA_KV_WIDTH, MEM_WIDTH)
IN_DTYPES = ((F32, BF16), (BF16,), (F32,), (F32,), (BF16,))


def _swa_out_perm():
    idx = []
    for t in range(SWA_REP):
        for half in range(SWA_KV_HEADS):
            h = t + SWA_REP * half
            idx.extend(range(h * SWA_HEAD_DIM, (h + 1) * SWA_HEAD_DIM))
    return jnp.asarray(idx, jnp.int32)


def kernel(x_prompt, x_sample, cache_swa_k, cache_swa_v, state_ssm_re, state_ssm_im, cache_mem_k, cache_mem_v, mem_prompt, norm1_g, w_in, lam_re, lam_im, log_dt, bm_re, bm_im, cm_re, cm_im, d_skip, w_glu, b_glu, sinks, rel_table, mem_norm_g, w_mem_kv, w_br_ssm, w_br_swa, w_br_mem, w_out, norm2_g, w_rg, b_rg, w_rexp, b_rexp, w_e_gate, w_e_up, w_e_down, final_norm_g):
    nb, t, _ = x_prompt.shape
    ns, ts, _ = x_sample.shape
    l = 0

    w_main, w_gates = _prep_in_weights(w_in[l])
    perm = _swa_out_perm()
    w_router = jnp.concatenate([w_rexp[l].T, w_rg[l].T,
                                jnp.zeros((ROUTER_ROWS - N_EXPERTS - N_EXPERT_GROUPS, D_MODEL), F32)], axis=0).astype(BF16)
    b_router = jnp.concatenate([b_rexp[l], b_rg[l],
                                jnp.zeros((ROUTER_ROWS - N_EXPERTS - N_EXPERT_GROUPS,), F32)]).reshape(ROUTER_ROWS, 1)
    mp = {
        'g1': norm1_g[l].reshape(1, D_MODEL), 'w_gates': w_gates, 'd_skip': d_skip[l].reshape(1, SSM_WIDTH),
        'w_glu': w_glu[l].astype(BF16), 'b_glu': b_glu[l].reshape(1, SSM_WIDTH),
        'w_br_ssm': w_br_ssm[l].astype(BF16), 'w_br_swa': w_br_swa[l][perm].astype(BF16),
        'w_br_mem': w_br_mem[l].astype(BF16), 'w_out': w_out[l].astype(BF16),
        'g2': norm2_g[l].reshape(1, D_MODEL), 'w_router': w_router, 'b_router': b_router,
    }
    w_gu = jnp.concatenate([w_e_gate[l], w_e_up[l]], axis=-1).astype(BF16)
    w_d = w_e_down[l].astype(BF16)
    gf = final_norm_g.reshape(1, D_MODEL)
    s5_args = (lam_re[l], lam_im[l], log_dt[l], bm_re[l], bm_im[l], cm_re[l], cm_im[l])

    kq = jnp.arange(WINDOW)[:, None] + WINDOW - jnp.arange(2 * WINDOW)[None, :]
    bias_p = jnp.moveaxis(rel_table[_t5_bucket(kq)].astype(F32), -1, 0)
    keys_s = WINDOW + 2 * ts
    ks = jnp.arange(ts)[:, None] + WINDOW - jnp.arange(keys_s)[None, :]
    bias_s = jnp.moveaxis(rel_table[_t5_bucket(ks)].astype(F32), -1, 0).reshape(SWA_HEADS * ts, keys_s)
    sink_rows = jnp.repeat(sinks[l].astype(F32), ts).reshape(SWA_HEADS * ts, 1)

    n = nb * t
    xp = x_prompt.reshape(n, D_MODEL)
    mk, mv = _norm_proj(mem_prompt.reshape(nb * MEM_TOKENS, D_MODEL), mem_norm_g[l].reshape(1, D_MODEL),
                        w_mem_kv[l].astype(BF16), (MEM_WIDTH, MEM_WIDTH), ((F32,), (F32,)), 512)
    u, ub, qz, k, v, qm = _norm_proj(xp, mp['g1'], w_main, IN_SPLITS, IN_DTYPES, 512)

    L = PROMPT_CHUNK
    nc = t // L
    w_st, w_so, toep, a_re, a_im = _s5_weights(*s5_args, L)
    xc = _to_chunks(ub, nb, nc, L)
    d = _s5_state_in(xc, w_st, L, 512)
    d_tm = d.reshape(nb, nc, -1).transpose(1, 0, 2)
    hs_tm, fin = _s5_scan(d_tm, a_re, a_im)
    hs = hs_tm.transpose(1, 0, 2).reshape(nb * nc, -1)
    y_ssm = _from_chunks(_s5_out(xc, hs, toep, w_so, L, 512), nb, nc, L)
    p_re, p_im = _tiles_to_state(fin.transpose(1, 0, 2))

    o_swa = _swa_prompt(qz, k, v, bias_p, sinks[l].astype(F32), nb, t)
    o_mem = _mem_prompt(qm, mk, mv, nb, t, 512)
    h, xn2, comb = _merge(xp, u, y_ssm, o_swa, o_mem, mp, 256)
    y_prompt = _moe(xn2, comb.T, w_gu, w_d, h, gf, 1024).reshape(nb, t, D_MODEL)

    k4 = k.reshape(nb, t, SWA_KV_HEADS, SWA_HEAD_DIM)
    v4 = v.reshape(nb, t, SWA_KV_HEADS, SWA_HEAD_DIM)
    new_k_p, new_v_p = k4[:, -WINDOW:][None], v4[:, -WINDOW:][None]
    new_mk = mk.reshape(1, nb, MEM_TOKENS, MEM_HEADS, MEM_HEAD_DIM)
    new_mv = mv.reshape(1, nb, MEM_TOKENS, MEM_HEADS, MEM_HEAD_DIM)

    m = ns * ts
    xs = x_sample.reshape(m, D_MODEL)
    us, ubs, qzs, k_s, v_s, qms = _norm_proj(xs, mp['g1'], w_main, IN_SPLITS, IN_DTYPES, 256)
    w_st8, w_so8, toep8, a8_re, a8_im = _s5_weights(*s5_args, ts)
    xcs = _to_chunks(ubs, ns, 1, ts)
    h0 = _state_to_tiles(state_ssm_re[l], state_ssm_im[l])
    ycs, fins = _s5_single_chunk(xcs, h0, a8_re, a8_im, w_st8, toep8, w_so8, ts)
    ys_ssm = _from_chunks(ycs, ns, 1, ts)
    s_re, s_im = _tiles_to_state(fins.reshape(ns, N_CH_TILES, 2 * STATE_TILE))

    kk_all = jnp.concatenate([cache_swa_k[l].reshape(ns, WINDOW, SWA_KV_WIDTH).astype(F32),
                              k_s.reshape(ns, ts, SWA_KV_WIDTH)], axis=1)
    vv_all = jnp.concatenate([cache_swa_v[l].reshape(ns, WINDOW, SWA_KV_WIDTH).astype(F32),
                              v_s.reshape(ns, ts, SWA_KV_WIDTH)], axis=1)
    pad = jnp.zeros((ns, keys_s - WINDOW - ts, SWA_KV_WIDTH), F32)
    q_rows = qzs.reshape(ns, ts, SWA_HEADS, LANES).transpose(0, 2, 1, 3).reshape(ns, SWA_HEADS * ts, LANES)
    o_dec = _swa_decode(q_rows, jnp.concatenate([kk_all, pad], axis=1), jnp.concatenate([vv_all, pad], axis=1),
                        bias_s, sink_rows, ts, 8)
    o_dec = o_dec.reshape(ns, SWA_KV_HEADS, SWA_REP, ts, SWA_KV_HEADS, SWA_HEAD_DIM)
    o_dec = jnp.stack([o_dec[:, g, :, :, g] for g in range(SWA_KV_HEADS)], axis=1)
    o_swa_s = o_dec.transpose(0, 3, 2, 1, 4).reshape(m, SWA_WIDTH).astype(BF16)

    qb = qms.reshape(ns, ts, MEM_HEADS, 1, MEM_HEAD_DIM).transpose(0, 2, 1, 3, 4)
    qb = qb * jnp.eye(MEM_HEADS, dtype=BF16).reshape(1, MEM_HEADS, 1, MEM_HEADS, 1)
    qb = qb.reshape(ns, MEM_HEADS * ts, MEM_WIDTH)
    o_md = _mem_decode(qb, cache_mem_k[l].reshape(ns, MEM_TOKENS, MEM_WIDTH),
                       cache_mem_v[l].reshape(ns, MEM_TOKENS, MEM_WIDTH), 8)
    o_md = o_md.reshape(ns, MEM_HEADS, ts, MEM_HEADS, MEM_HEAD_DIM)
    o_mem_s = jnp.stack([o_md[:, hh, :, hh] for hh in range(MEM_HEADS)], axis=2).reshape(m, MEM_WIDTH)

    hs_, xn2s, combs = _merge(xs, us, ys_ssm, o_swa_s, o_mem_s, mp, 256)
    y_sample = _moe(xn2s, combs.T, w_gu, w_d, hs_, gf, 1024).reshape(ns, ts, D_MODEL)

    new_k_s = kk_all[:, -WINDOW:].reshape(1, ns, WINDOW, SWA_KV_HEADS, SWA_HEAD_DIM).astype(cache_swa_k.dtype)
    new_v_s = vv_all[:, -WINDOW:].reshape(1, ns, WINDOW, SWA_KV_HEADS, SWA_HEAD_DIM).astype(cache_swa_v.dtype)

    return (y_prompt, y_sample,
            new_k_p, new_v_p, p_re[None], p_im[None], new_mk, new_mv,
            new_k_s, new_v_s, s_re[None].astype(state_ssm_re.dtype), s_im[None].astype(state_ssm_im.dtype))
```

```python
import functools
import math

import numpy as np
import jax
import jax.numpy as jnp
from jax import lax
from jax.experimental import pallas as pl
from jax.experimental.pallas import tpu as pltpu

F32 = jnp.float32
BF16 = jnp.bfloat16

D_MODEL = 1024
SSM_WIDTH = 512
SSM_GROUP = 16
SSM_GROUPS = 32
SSM_STATE = 64
SWA_HEADS = 8
SWA_KV_HEADS = 2
SWA_REP = 4
SWA_HEAD_DIM = 64
SWA_WIDTH = 512
SWA_KV_WIDTH = 128
WINDOW = 128
REL_BUCKETS = 32
REL_MAX_DIST = 128
MEM_TOKENS = 256
MEM_HEADS = 4
MEM_HEAD_DIM = 128
MEM_WIDTH = 512
N_EXPERT_GROUPS = 4
EXPERTS_PER_GROUP = 8
N_EXPERTS = 32
D_EXPERT = 256
EPS = 1e-6
NEG_INF = -1e30

LANES = 128
GROUPS_PER_TILE = LANES // SSM_GROUP
N_CH_TILES = SSM_WIDTH // LANES
STATE_TILE = GROUPS_PER_TILE * SSM_STATE
VMEM_LIMIT = 56 * 1024 * 1024
S5_CHUNK = 8

_TRANS_B = (((1,), (1,)), ((), ()))


def _cparams(*sem):
    return pltpu.CompilerParams(dimension_semantics=sem, vmem_limit_bytes=VMEM_LIMIT)


def _rms(x, g):
    return (x * lax.rsqrt(jnp.mean(x * x, axis=-1, keepdims=True) + EPS)) * g


def _dot(a, b):
    return jnp.dot(a, b, preferred_element_type=F32)


def _norm_proj_kernel(x_ref, g_ref, w_ref, *out_refs, splits, dtypes):
    xb = _rms(x_ref[...], g_ref[...]).astype(BF16)
    off = 0
    outs = iter(out_refs)
    for width, dts in zip(splits, dtypes):
        r = _dot(xb, w_ref[:, off:off + width])
        for dt in dts:
            next(outs)[...] = r.astype(dt)
        off += width


def _norm_proj(x, g, w, splits, dtypes, tile):
    n, d = x.shape
    tile = min(tile, n)
    flat = [(wd, dt) for wd, dts in zip(splits, dtypes) for dt in dts]
    return pl.pallas_call(
        functools.partial(_norm_proj_kernel, splits=tuple(splits), dtypes=tuple(dtypes)),
        grid=(n // tile,),
        in_specs=[pl.BlockSpec((tile, d), lambda i: (i, 0)),
                  pl.BlockSpec((1, d), lambda i: (0, 0)),
                  pl.BlockSpec((d, sum(splits)), lambda i: (0, 0))],
        out_specs=[pl.BlockSpec((tile, wd), lambda i: (i, 0)) for wd, _ in flat],
        out_shape=[jax.ShapeDtypeStruct((n, wd), dt) for wd, dt in flat],
        compiler_params=_cparams("parallel"),
        name="norm_proj",
    )(x, g, w)


def _group_mask(rows_per_group, cols_per_group):
    r = np.arange(GROUPS_PER_TILE * rows_per_group)[:, None] // rows_per_group
    c = np.arange(GROUPS_PER_TILE * cols_per_group)[None, :] // cols_per_group
    return jnp.asarray(r == c, F32)


def _s5_weights(lam_re, lam_im, log_dt, bm_re, bm_im, cm_re, cm_im, L):
    hp = lax.Precision.HIGHEST
    nt, gt, P, H = N_CH_TILES, GROUPS_PER_TILE, SSM_STATE, SSM_GROUP
    lr, li = lam_re.astype(F32), lam_im.astype(F32)
    dt = jnp.exp(log_dt.astype(F32))[:, None]
    mag = jnp.exp(lr * dt)
    a_re = mag * jnp.cos(li * dt)
    a_im = mag * jnp.sin(li * dt)
    den = lr * lr + li * li
    f_re = ((a_re - 1.0) * lr + a_im * li) / den
    f_im = (a_im * lr - (a_re - 1.0) * li) / den
    br, bi = bm_re.astype(F32), bm_im.astype(F32)
    bb_re = f_re[..., None] * br - f_im[..., None] * bi
    bb_im = f_re[..., None] * bi + f_im[..., None] * br
    pr, pi = [jnp.ones_like(a_re)], [jnp.zeros_like(a_im)]
    for _ in range(L):
        pr.append(pr[-1] * a_re - pi[-1] * a_im)
        pi.append(pr[-2] * a_im + pi[-1] * a_re)
    cr, ci = cm_re.astype(F32), cm_im.astype(F32)
    ca_re = [cr * pr[k][:, None, :] - ci * pi[k][:, None, :] for k in range(L + 1)]
    ca_im = [cr * pi[k][:, None, :] + ci * pr[k][:, None, :] for k in range(L + 1)]

    def expand(w, rows_per_group, cols_per_group):
        w = w.reshape(nt, gt * rows_per_group, cols_per_group)
        return jnp.tile(w, (1, 1, gt)) * _group_mask(rows_per_group, cols_per_group)

    st_re, st_im = [], []
    for s in range(L):
        k = L - 1 - s
        w_re = pr[k][..., None] * bb_re - pi[k][..., None] * bb_im
        w_im = pr[k][..., None] * bb_im + pi[k][..., None] * bb_re
        st_re.append(expand(w_re.transpose(0, 2, 1), H, P))
        st_im.append(expand(w_im.transpose(0, 2, 1), H, P))
    w_st = jnp.concatenate([jnp.concatenate(st_re, axis=1), jnp.concatenate(st_im, axis=1)], axis=2).astype(BF16)

    so_re = jnp.concatenate([expand(ca_re[t + 1].transpose(0, 2, 1), P, H) for t in range(L)], axis=2)
    so_im = jnp.concatenate([expand(-ca_im[t + 1].transpose(0, 2, 1), P, H) for t in range(L)], axis=2)
    w_out = jnp.concatenate([so_re, so_im], axis=1).astype(BF16)

    blocks = []
    for tau in range(L):
        k_lag = (jnp.einsum('ghp,gpk->gkh', ca_re[tau], bb_re, precision=hp)
                 - jnp.einsum('ghp,gpk->gkh', ca_im[tau], bb_im, precision=hp))
        blocks.append(expand(k_lag, H, H).astype(BF16))
    zero = jnp.zeros_like(blocks[0])
    toep = jnp.concatenate(
        [jnp.concatenate([blocks[t - s] if t >= s else zero for t in range(L)], axis=2) for s in range(L)], axis=1)

    def per_tile(v):
        return v.reshape(nt, 1, STATE_TILE)

    return w_st, w_out, toep, per_tile(pr[L]), per_tile(pi[L])


def _to_chunks(u, nb, nc, L):
    return (u.reshape(nb, nc, L, N_CH_TILES, LANES).transpose(1, 0, 3, 2, 4)
            .reshape(nc * nb, N_CH_TILES * L * LANES))


def _from_chunks(y, nb, nc, L):
    return (y.reshape(nc, nb, N_CH_TILES, L, LANES).transpose(1, 0, 3, 2, 4)
            .reshape(nb * nc * L, SSM_WIDTH))


def _s5_state_in_kernel(x_ref, w_ref, d_ref):
    d_ref[...] = _dot(x_ref[...], w_ref[...])


def _s5_state_in(xc, w_st, L, row_tile):
    rows = xc.shape[0]
    row_tile = min(row_tile, rows)
    lk = L * LANES
    return pl.pallas_call(
        _s5_state_in_kernel,
        grid=(N_CH_TILES, rows // row_tile),
        in_specs=[pl.BlockSpec((row_tile, lk), lambda j, r: (r, j)),
                  pl.BlockSpec((None, lk, 2 * STATE_TILE), lambda j, r: (j, 0, 0))],
        out_specs=pl.BlockSpec((row_tile, 2 * STATE_TILE), lambda j, r: (r, j)),
        out_shape=jax.ShapeDtypeStruct((rows, N_CH_TILES * 2 * STATE_TILE), F32),
        compiler_params=_cparams("parallel", "parallel"),
        name="s5_state_in",
    )(xc, w_st)


def _s5_scan_kernel(d_ref, h0_ref, are_ref, aim_ref, hs_ref, fin_ref, hr_ref, hi_ref, *, cb, nb):
    ci = pl.program_id(1)

    @pl.when(ci == 0)
    def _():
        hr_ref[...] = h0_ref[:, 0:STATE_TILE]
        hi_ref[...] = h0_ref[:, STATE_TILE:2 * STATE_TILE]

    ar = jnp.broadcast_to(are_ref[...], (nb, STATE_TILE))
    ai = jnp.broadcast_to(aim_ref[...], (nb, STATE_TILE))

    def body(c, carry):
        hr, hi = carry
        hs_ref[c, :, 0:STATE_TILE] = hr
        hs_ref[c, :, STATE_TILE:2 * STATE_TILE] = hi
        d = d_ref[c]
        return (ar * hr - ai * hi + d[:, 0:STATE_TILE],
                ar * hi + ai * hr + d[:, STATE_TILE:2 * STATE_TILE])

    hr, hi = lax.fori_loop(0, cb, body, (hr_ref[...], hi_ref[...]))
    hr_ref[...] = hr
    hi_ref[...] = hi

    @pl.when(ci == pl.num_programs(1) - 1)
    def _():
        fin_ref[:, 0:STATE_TILE] = hr
        fin_ref[:, STATE_TILE:2 * STATE_TILE] = hi


def _s5_scan(d, h0, a_re, a_im, chunk_block):
    nc, nb, _ = d.shape
    cb = min(chunk_block, nc)
    st2 = 2 * STATE_TILE
    return pl.pallas_call(
        functools.partial(_s5_scan_kernel, cb=cb, nb=nb),
        grid=(N_CH_TILES, nc // cb),
        in_specs=[pl.BlockSpec((cb, nb, st2), lambda j, c: (c, 0, j)),
                  pl.BlockSpec((nb, st2), lambda j, c: (0, j)),
                  pl.BlockSpec((None, 1, STATE_TILE), lambda j, c: (j, 0, 0)),
                  pl.BlockSpec((None, 1, STATE_TILE), lambda j, c: (j, 0, 0))],
        out_specs=[pl.BlockSpec((cb, nb, st2), lambda j, c: (c, 0, j)),
                   pl.BlockSpec((nb, st2), lambda j, c: (0, j))],
        out_shape=[jax.ShapeDtypeStruct((nc, nb, N_CH_TILES * st2), F32),
                   jax.ShapeDtypeStruct((nb, N_CH_TILES * st2), F32)],
        scratch_shapes=[pltpu.VMEM((nb, STATE_TILE), F32), pltpu.VMEM((nb, STATE_TILE), F32)],
        compiler_params=_cparams("parallel", "arbitrary"),
        name="s5_scan",
    )(d, h0, a_re, a_im)


def _s5_out_kernel(x_ref, h_ref, t_ref, wo_ref, y_ref):
    y_ref[...] = _dot(x_ref[...], t_ref[...]) + _dot(h_ref[...].astype(BF16), wo_ref[...])


def _s5_out(xc, hs, toep, w_out, L, row_tile):
    rows = xc.shape[0]
    row_tile = min(row_tile, rows)
    lk = L * LANES
    st2 = 2 * STATE_TILE
    return pl.pallas_call(
        _s5_out_kernel,
        grid=(N_CH_TILES, rows // row_tile),
        in_specs=[pl.BlockSpec((row_tile, lk), lambda j, r: (r, j)),
                  pl.BlockSpec((row_tile, st2), lambda j, r: (r, j)),
                  pl.BlockSpec((None, lk, lk), lambda j, r: (j, 0, 0)),
                  pl.BlockSpec((None, st2, lk), lambda j, r: (j, 0, 0))],
        out_specs=pl.BlockSpec((row_tile, lk), lambda j, r: (r, j)),
        out_shape=jax.ShapeDtypeStruct((rows, N_CH_TILES * lk), F32),
        compiler_params=_cparams("parallel", "parallel"),
        name="s5_out",
    )(xc, hs, toep, w_out)


def _s5(ub, h0, weights, nb, nc, L):
    w_st, w_so, toep, a_re, a_im = weights
    xc = _to_chunks(ub, nb, nc, L)
    d = _s5_state_in(xc, w_st, L, 512)
    hs, fin = _s5_scan(d.reshape(nc, nb, -1), h0, a_re, a_im, 64)
    y = _s5_out(xc, hs.reshape(nc * nb, -1), toep, w_so, L, 512)
    return _from_chunks(y, nb, nc, L), fin


def _state_to_tiles(h_re, h_im):
    nb = h_re.shape[0]
    r = h_re.astype(F32).reshape(nb, N_CH_TILES, STATE_TILE)
    i = h_im.astype(F32).reshape(nb, N_CH_TILES, STATE_TILE)
    return jnp.concatenate([r, i], axis=-1).reshape(nb, N_CH_TILES * 2 * STATE_TILE)


def _tiles_to_state(h):
    nb = h.shape[0]
    h = h.reshape(nb, N_CH_TILES, 2, GROUPS_PER_TILE, SSM_STATE)
    return (h[:, :, 0].reshape(nb, SSM_GROUPS, SSM_STATE), h[:, :, 1].reshape(nb, SSM_GROUPS, SSM_STATE))


def _t5_bucket(dist):
    n = np.maximum(dist, 0)
    max_exact = REL_BUCKETS // 2
    nf = np.maximum(n, 1).astype(np.float32)
    large = max_exact + (np.log(nf / np.float32(max_exact)) / np.float32(math.log(REL_MAX_DIST / max_exact))
                         * np.float32(REL_BUCKETS - max_exact)).astype(np.int32)
    large = np.minimum(large, REL_BUCKETS - 1)
    return np.where(n < max_exact, n, large)


def _rel_bias(rel_table, dist):
    bucket = _t5_bucket(dist)
    tab = rel_table.astype(F32)
    out = jnp.zeros((SWA_HEADS,) + dist.shape, F32)
    for b in range(REL_BUCKETS):
        sel = jnp.asarray(bucket == b)
        if bool((bucket == b).any()):
            out = jnp.where(sel[None], tab[b].reshape((SWA_HEADS,) + (1,) * dist.ndim), out)
    return out


def _softmax_sink(s, sink):
    m = jnp.maximum(jnp.max(s, axis=-1, keepdims=True), sink)
    e = jnp.exp(s - m)
    den = jnp.sum(e, axis=-1, keepdims=True) + jnp.exp(sink - m)
    return e * (1.0 / den)


def _swa_prompt_kernel(sink_ref, q_ref, kp_ref, kc_ref, vp_ref, vc_ref, bias_ref, o_ref):
    blk = pl.program_id(1)
    kk = jnp.concatenate([kp_ref[...], kc_ref[...]], axis=0).astype(BF16)
    vv = jnp.concatenate([vp_ref[...], vc_ref[...]], axis=0).astype(BF16)
    row = lax.broadcasted_iota(jnp.int32, (WINDOW, 2 * WINDOW), 0)
    col = lax.broadcasted_iota(jnp.int32, (WINDOW, 2 * WINDOW), 1)
    dist = row + WINDOW - col
    valid = (dist >= 0) & (dist < WINDOW) & ((col >= WINDOW) | (blk > 0))
    outs = []
    for h in range(SWA_HEADS):
        s = lax.dot_general(q_ref[:, h * LANES:(h + 1) * LANES], kk, _TRANS_B, preferred_element_type=F32)
        s = jnp.where(valid, s + bias_ref[h], NEG_INF)
        p = _softmax_sink(s, sink_ref[h]).astype(BF16)
        outs.append(_dot(p, vv))
    lane = lax.broadcasted_iota(jnp.int32, (WINDOW, LANES), 1)
    for t in range(SWA_REP):
        o_ref[:, t * LANES:(t + 1) * LANES] = jnp.where(lane < SWA_HEAD_DIM, outs[t], outs[t + SWA_REP]).astype(BF16)


def _swa_prompt(qz, k, v, bias, sinks, nb, t):
    nblk = t // WINDOW
    cur = lambda b, i: (b * nblk + i, 0)
    prev = lambda b, i: (b * nblk + jnp.maximum(i - 1, 0), 0)
    return pl.pallas_call(
        _swa_prompt_kernel,
        grid=(nb, nblk),
        in_specs=[pl.BlockSpec(memory_space=pltpu.SMEM),
                  pl.BlockSpec((WINDOW, SWA_HEADS * LANES), cur),
                  pl.BlockSpec((WINDOW, SWA_KV_WIDTH), prev),
                  pl.BlockSpec((WINDOW, SWA_KV_WIDTH), cur),
                  pl.BlockSpec((WINDOW, SWA_KV_WIDTH), prev),
                  pl.BlockSpec((WINDOW, SWA_KV_WIDTH), cur),
                  pl.BlockSpec((SWA_HEADS, WINDOW, 2 * WINDOW), lambda b, i: (0, 0, 0))],
        out_specs=pl.BlockSpec((WINDOW, SWA_WIDTH), cur),
        out_shape=jax.ShapeDtypeStruct((nb * t, SWA_WIDTH), BF16),
        compiler_params=_cparams("parallel", "parallel"),
        name="swa_prompt",
    )(sinks, qz, k, k, v, v, bias)


def _swa_decode_kernel(q_ref, k_ref, v_ref, bias_ref, sink_ref, o_ref, *, seqs, tq):
    rows, keys = q_ref.shape[1], k_ref.shape[1]
    qi = lax.broadcasted_iota(jnp.int32, (rows, keys), 0) % tq
    col = lax.broadcasted_iota(jnp.int32, (rows, keys), 1)
    dist = qi + WINDOW - col
    valid = (dist >= 0) & (dist < WINDOW)
    bias = bias_ref[...]
    sink = sink_ref[...]
    for s_i in range(seqs):
        kk = k_ref[s_i].astype(BF16)
        s = lax.dot_general(q_ref[s_i], kk, _TRANS_B, preferred_element_type=F32)
        s = jnp.where(valid, s + bias, NEG_INF)
        p = _softmax_sink(s, sink).astype(BF16)
        o_ref[s_i] = _dot(p, v_ref[s_i].astype(BF16))


def _swa_decode(qz, k_all, v_all, bias, sink_rows, tq, seqs):
    nseq, rows, _ = qz.shape
    keys = k_all.shape[1]
    seqs = min(seqs, nseq)
    return pl.pallas_call(
        functools.partial(_swa_decode_kernel, seqs=seqs, tq=tq),
        grid=(nseq // seqs,),
        in_specs=[pl.BlockSpec((seqs, rows, LANES), lambda i: (i, 0, 0)),
                  pl.BlockSpec((seqs, keys, LANES), lambda i: (i, 0, 0)),
                  pl.BlockSpec((seqs, keys, LANES), lambda i: (i, 0, 0)),
                  pl.BlockSpec((rows, keys), lambda i: (0, 0)),
                  pl.BlockSpec((rows, 1), lambda i: (0, 0))],
        out_specs=pl.BlockSpec((seqs, rows, LANES), lambda i: (i, 0, 0)),
        out_shape=jax.ShapeDtypeStruct((nseq, rows, LANES), F32),
        compiler_params=_cparams("parallel"),
        name="swa_decode",
    )(qz, k_all, v_all, bias, sink_rows)


def _softmax(s):
    m = jnp.max(s, axis=-1, keepdims=True)
    e = jnp.exp(s - m)
    return e * (1.0 / jnp.sum(e, axis=-1, keepdims=True))


def _mem_prompt_kernel(q_ref, k_ref, v_ref, o_ref):
    scale = MEM_HEAD_DIM ** -0.5
    for h in range(MEM_HEADS):
        sl = slice(h * MEM_HEAD_DIM, (h + 1) * MEM_HEAD_DIM)
        s = lax.dot_general(q_ref[:, sl], k_ref[:, sl].astype(BF16), _TRANS_B, preferred_element_type=F32) * scale
        p = _softmax(s).astype(BF16)
        o_ref[:, sl] = _dot(p, v_ref[:, sl].astype(BF16)).astype(BF16)


def _mem_prompt(qm, mk, mv, nb, t, tile):
    tile = min(tile, t)
    nt = t // tile
    return pl.pallas_call(
        _mem_prompt_kernel,
        grid=(nb, nt),
        in_specs=[pl.BlockSpec((tile, MEM_WIDTH), lambda b, i: (b * nt + i, 0)),
                  pl.BlockSpec((MEM_TOKENS, MEM_WIDTH), lambda b, i: (b, 0)),
                  pl.BlockSpec((MEM_TOKENS, MEM_WIDTH), lambda b, i: (b, 0))],
        out_specs=pl.BlockSpec((tile, MEM_WIDTH), lambda b, i: (b * nt + i, 0)),
        out_shape=jax.ShapeDtypeStruct((nb * t, MEM_WIDTH), BF16),
        compiler_params=_cparams("parallel", "parallel"),
        name="mem_prompt",
    )(qm, mk, mv)


def _mem_decode_kernel(q_ref, k_ref, v_ref, o_ref, *, seqs):
    scale = MEM_HEAD_DIM ** -0.5
    for s_i in range(seqs):
        s = lax.dot_general(q_ref[s_i], k_ref[s_i].astype(BF16), _TRANS_B, preferred_element_type=F32) * scale
        p = _softmax(s).astype(BF16)
        o_ref[s_i] = _dot(p, v_ref[s_i].astype(BF16)).astype(BF16)


def _mem_decode(qb, k, v, seqs):
    nseq, rows, _ = qb.shape
    seqs = min(seqs, nseq)
    return pl.pallas_call(
        functools.partial(_mem_decode_kernel, seqs=seqs),
        grid=(nseq // seqs,),
        in_specs=[pl.BlockSpec((seqs, rows, MEM_WIDTH), lambda i: (i, 0, 0)),
                  pl.BlockSpec((seqs, MEM_TOKENS, MEM_WIDTH), lambda i: (i, 0, 0)),
                  pl.BlockSpec((seqs, MEM_TOKENS, MEM_WIDTH), lambda i: (i, 0, 0))],
        out_specs=pl.BlockSpec((seqs, rows, MEM_WIDTH), lambda i: (i, 0, 0)),
        out_shape=jax.ShapeDtypeStruct((nseq, rows, MEM_WIDTH), BF16),
        compiler_params=_cparams("parallel"),
        name="mem_decode",
    )(qb, k, v)


ROUTER_ROWS = 40


def _merge_kernel(x_ref, u_ref, y_ref, os_ref, om_ref, g1_ref, wg_ref, dsk_ref, wglu_ref, bglu_ref,
                  wbs_ref, wbw_ref, wbm_ref, wout_ref, g2_ref, wr_ref, br_ref,
                  h_ref, xn2_ref, comb_ref):
    x = x_ref[...]
    tt = x.shape[0]
    xb = _rms(x, g1_ref[...]).astype(BF16)
    z = jax.nn.gelu(y_ref[...] + dsk_ref[...] * u_ref[...])
    z = z * jax.nn.sigmoid(_dot(z.astype(BF16), wglu_ref[...]) + bglu_ref[...])
    merged = jax.nn.sigmoid(_dot(xb, wg_ref[:, 0:D_MODEL])) * _dot(z.astype(BF16), wbs_ref[...])
    merged = merged + jax.nn.sigmoid(_dot(xb, wg_ref[:, D_MODEL:2 * D_MODEL])) * _dot(os_ref[...], wbw_ref[...])
    merged = merged + jax.nn.sigmoid(_dot(xb, wg_ref[:, 2 * D_MODEL:3 * D_MODEL])) * _dot(om_ref[...], wbm_ref[...])
    h = x + _dot(merged.astype(BF16), wout_ref[...])
    h_ref[...] = h
    xn2 = _rms(h, g2_ref[...]).astype(BF16)
    xn2_ref[...] = xn2

    lt = lax.dot_general(wr_ref[...], xn2, _TRANS_B, preferred_element_type=F32) + br_ref[...]
    gl = lt[N_EXPERTS:N_EXPERTS + N_EXPERT_GROUPS]
    ge = jnp.exp(gl - jnp.max(gl, axis=0, keepdims=True))
    gp = ge / jnp.sum(ge, axis=0, keepdims=True)
    gw = jnp.max(gp, axis=0, keepdims=True)
    gidx = jnp.full((1, tt), N_EXPERT_GROUPS - 1, jnp.int32)
    for r in range(N_EXPERT_GROUPS - 2, -1, -1):
        gidx = jnp.where(gp[r:r + 1] == gw, r, gidx)
    ein = lt[(N_EXPERT_GROUPS - 1) * EXPERTS_PER_GROUP:N_EXPERTS]
    for r in range(N_EXPERT_GROUPS - 2, -1, -1):
        ein = jnp.where(gidx == r, lt[r * EXPERTS_PER_GROUP:(r + 1) * EXPERTS_PER_GROUP], ein)
    ee = jnp.exp(ein - jnp.max(ein, axis=0, keepdims=True))
    ep = ee / jnp.sum(ee, axis=0, keepdims=True)
    rowi = lax.broadcasted_iota(jnp.int32, (EXPERTS_PER_GROUP, tt), 0)
    p1 = jnp.max(ep, axis=0, keepdims=True)
    e1 = jnp.min(jnp.where(ep == p1, rowi, EXPERTS_PER_GROUP), axis=0, keepdims=True)
    ep2 = jnp.where(rowi == e1, -1.0, ep)
    p2 = jnp.max(ep2, axis=0, keepdims=True)
    e2 = jnp.min(jnp.where(ep2 == p2, rowi, EXPERTS_PER_GROUP), axis=0, keepdims=True)
    tot = p1 + p2
    w1 = p1 / tot * gw
    w2 = p2 / tot * gw
    r32 = lax.broadcasted_iota(jnp.int32, (N_EXPERTS, tt), 0)
    comb_ref[...] = (jnp.where(r32 == gidx * EXPERTS_PER_GROUP + e1, w1, 0.0)
                     + jnp.where(r32 == gidx * EXPERTS_PER_GROUP + e2, w2, 0.0))


def _merge(x, u, y, o_swa, o_mem, p, tile):
    n = x.shape[0]
    tile = min(tile, n)
    row = lambda i: (i, 0)
    const = lambda i: (0, 0)
    full = lambda a: pl.BlockSpec(a.shape, const)
    weights = [p['g1'], p['w_gates'], p['d_skip'], p['w_glu'], p['b_glu'], p['w_br_ssm'], p['w_br_swa'],
               p['w_br_mem'], p['w_out'], p['g2'], p['w_router'], p['b_router']]
    return pl.pallas_call(
        _merge_kernel,
        grid=(n // tile,),
        in_specs=[pl.BlockSpec((tile, D_MODEL), row), pl.BlockSpec((tile, SSM_WIDTH), row),
                  pl.BlockSpec((tile, SSM_WIDTH), row), pl.BlockSpec((tile, SWA_WIDTH), row),
                  pl.BlockSpec((tile, MEM_WIDTH), row)] + [full(w) for w in weights],
        out_specs=[pl.BlockSpec((tile, D_MODEL), row), pl.BlockSpec((tile, D_MODEL), row),
                   pl.BlockSpec((N_EXPERTS, tile), lambda i: (0, i))],
        out_shape=[jax.ShapeDtypeStruct((n, D_MODEL), F32), jax.ShapeDtypeStruct((n, D_MODEL), BF16),
                   jax.ShapeDtypeStruct((N_EXPERTS, n), F32)],
        compiler_params=_cparams("parallel"),
        name="merge_router",
    )(x, u, y, o_swa, o_mem, *weights)


def _moe_kernel(xn2_ref, comb_ref, wgu_ref, wd_ref, h_ref, gf_ref, o_ref, acc_ref):
    e = pl.program_id(1)

    @pl.when(e == 0)
    def _():
        acc_ref[...] = jnp.zeros_like(acc_ref)

    hgu = _dot(xn2_ref[...], wgu_ref[...])
    hh = jax.nn.silu(hgu[:, 0:D_EXPERT]) * hgu[:, D_EXPERT:2 * D_EXPERT]
    o = _dot(hh.astype(BF16), wd_ref[...])
    lane = lax.broadcasted_iota(jnp.int32, comb_ref.shape, 1)
    c = jnp.sum(jnp.where(lane == e, comb_ref[...], 0.0), axis=1, keepdims=True)
    acc_ref[...] += c * o

    @pl.when(e == N_EXPERTS - 1)
    def _():
        o_ref[...] = _rms(h_ref[...] + acc_ref[...], gf_ref[...])


def _moe(xn2, comb, w_gu, w_d, h, gf, tile):
    n = h.shape[0]
    tile = min(tile, n)
    return pl.pallas_call(
        _moe_kernel,
        grid=(n // tile, N_EXPERTS),
        in_specs=[pl.BlockSpec((tile, D_MODEL), lambda i, e: (i, 0)),
                  pl.BlockSpec((tile, N_EXPERTS), lambda i, e: (i, 0)),
                  pl.BlockSpec((None, D_MODEL, 2 * D_EXPERT), lambda i, e: (e, 0, 0)),
                  pl.BlockSpec((None, D_EXPERT, D_MODEL), lambda i, e: (e, 0, 0)),
                  pl.BlockSpec((tile, D_MODEL), lambda i, e: (i, 0)),
                  pl.BlockSpec((1, D_MODEL), lambda i, e: (0, 0))],
        out_specs=pl.BlockSpec((tile, D_MODEL), lambda i, e: (i, 0)),
        out_shape=jax.ShapeDtypeStruct((n, D_MODEL), F32),
        scratch_shapes=[pltpu.VMEM((tile, D_MODEL), F32)],
        compiler_params=_cparams("parallel", "arbitrary"),
        name="moe_final_norm",
    )(xn2, comb, w_gu, w_d, h, gf)


def _prep_in_weights(w_in):
    o = 0
    w_u = w_in[:, o:o + SSM_WIDTH]; o += SSM_WIDTH
    w_q = w_in[:, o:o + SWA_WIDTH]; o += SWA_WIDTH
    w_k = w_in[:, o:o + SWA_KV_WIDTH]; o += SWA_KV_WIDTH
    w_v = w_in[:, o:o + SWA_KV_WIDTH]; o += SWA_KV_WIDTH
    w_qm = w_in[:, o:o + MEM_WIDTH]; o += MEM_WIDTH
    w_g = w_in[:, o:]
    zeros = jnp.zeros((D_MODEL, SWA_HEAD_DIM), w_in.dtype)
    tiles = []
    for h in range(SWA_HEADS):
        qh = w_q[:, h * SWA_HEAD_DIM:(h + 1) * SWA_HEAD_DIM] * (SWA_HEAD_DIM ** -0.5)
        tiles.extend([qh, zeros] if h // SWA_REP == 0 else [zeros, qh])
    w_main = jnp.concatenate([w_u] + tiles + [w_k, w_v, w_qm], axis=1).astype(BF16)
    return w_main, w_g.astype(BF16)


IN_SPLITS = (SSM_WIDTH, SWA_HEADS * LANES, SWA_KV_WIDTH, SWA_KV_WIDTH, MEM_WIDTH)
IN_DTYPES = ((F32, BF16), (BF16,), (F32,), (F32,), (BF16,))


def kernel(x_prompt, x_sample, cache_swa_k, cache_swa_v, state_ssm_re, state_ssm_im, cache_mem_k, cache_mem_v, mem_prompt, norm1_g, w_in, lam_re, lam_im, log_dt, bm_re, bm_im, cm_re, cm_im, d_skip, w_glu, b_glu, sinks, rel_table, mem_norm_g, w_mem_kv, w_br_ssm, w_br_swa, w_br_mem, w_out, norm2_g, w_rg, b_rg, w_rexp, b_rexp, w_e_gate, w_e_up, w_e_down, final_norm_g):
    nb, t, _ = x_prompt.shape
    ns, ts, _ = x_sample.shape
    l = 0
    L = S5_CHUNK

    w_main, w_gates = _prep_in_weights(w_in[l])
    w_swa = (w_br_swa[l].reshape(SWA_KV_HEADS, SWA_REP, SWA_HEAD_DIM, D_MODEL).transpose(1, 0, 2, 3)
             .reshape(SWA_WIDTH, D_MODEL))
    pad_rows = ROUTER_ROWS - N_EXPERTS - N_EXPERT_GROUPS
    w_router = jnp.concatenate([w_rexp[l].T, w_rg[l].T, jnp.zeros((pad_rows, D_MODEL), F32)], axis=0).astype(BF16)
    b_router = jnp.concatenate([b_rexp[l], b_rg[l], jnp.zeros((pad_rows,), F32)]).reshape(ROUTER_ROWS, 1)
    mp = {
        'g1': norm1_g[l].reshape(1, D_MODEL), 'w_gates': w_gates, 'd_skip': d_skip[l].reshape(1, SSM_WIDTH),
        'w_glu': w_glu[l].astype(BF16), 'b_glu': b_glu[l].reshape(1, SSM_WIDTH),
        'w_br_ssm': w_br_ssm[l].astype(BF16), 'w_br_swa': w_swa.astype(BF16),
        'w_br_mem': w_br_mem[l].astype(BF16), 'w_out': w_out[l].astype(BF16),
        'g2': norm2_g[l].reshape(1, D_MODEL), 'w_router': w_router, 'b_router': b_router,
    }
    w_gu = jnp.concatenate([w_e_gate[l], w_e_up[l]], axis=-1).astype(BF16)
    w_d = w_e_down[l].astype(BF16)
    gf = final_norm_g.reshape(1, D_MODEL)
    s5_w = _s5_weights(lam_re[l], lam_im[l], log_dt[l], bm_re[l], bm_im[l], cm_re[l], cm_im[l], L)

    bias_p = _rel_bias(rel_table, np.arange(WINDOW)[:, None] + WINDOW - np.arange(2 * WINDOW)[None, :])
    keys_s = WINDOW + 2 * ts
    bias_s = _rel_bias(rel_table, np.arange(ts)[:, None] + WINDOW - np.arange(keys_s)[None, :])
    bias_s = bias_s.reshape(SWA_HEADS * ts, keys_s)
    sink_rows = jnp.repeat(sinks[l].astype(F32), ts).reshape(SWA_HEADS * ts, 1)

    n = nb * t
    xp = x_prompt.reshape(n, D_MODEL)
    mk, mv = _norm_proj(mem_prompt.reshape(nb * MEM_TOKENS, D_MODEL), mem_norm_g[l].reshape(1, D_MODEL),
                        w_mem_kv[l].astype(BF16), (MEM_WIDTH, MEM_WIDTH), ((F32,), (F32,)), 512)
    u, ub, qz, k, v, qm = _norm_proj(xp, mp['g1'], w_main, IN_SPLITS, IN_DTYPES, 512)

    y_ssm, fin = _s5(ub, jnp.zeros((nb, N_CH_TILES * 2 * STATE_TILE), F32), s5_w, nb, t // L, L)
    p_re, p_im = _tiles_to_state(fin)

    o_swa = _swa_prompt(qz, k, v, bias_p, sinks[l].astype(F32), nb, t)
    o_mem = _mem_prompt(qm, mk, mv, nb, t, 512)
    h, xn2, comb = _merge(xp, u, y_ssm, o_swa, o_mem, mp, 256)
    y_prompt = _moe(xn2, comb.T, w_gu, w_d, h, gf, 1024).reshape(nb, t, D_MODEL)

    k4 = k.reshape(nb, t, SWA_KV_HEADS, SWA_HEAD_DIM)
    v4 = v.reshape(nb, t, SWA_KV_HEADS, SWA_HEAD_DIM)
    new_k_p, new_v_p = k4[:, -WINDOW:][None], v4[:, -WINDOW:][None]
    new_mk = mk.reshape(1, nb, MEM_TOKENS, MEM_HEADS, MEM_HEAD_DIM)
    new_mv = mv.reshape(1, nb, MEM_TOKENS, MEM_HEADS, MEM_HEAD_DIM)

    m = ns * ts
    xs = x_sample.reshape(m, D_MODEL)
    us, ubs, qzs, k_s, v_s, qms = _norm_proj(xs, mp['g1'], w_main, IN_SPLITS, IN_DTYPES, 256)
    ys_ssm, fins = _s5(ubs, _state_to_tiles(state_ssm_re[l], state_ssm_im[l]), s5_w, ns, ts // L, L)
    s_re, s_im = _tiles_to_state(fins)

    kk_all = jnp.concatenate([cache_swa_k[l].reshape(ns, WINDOW, SWA_KV_WIDTH).astype(F32),
                              k_s.reshape(ns, ts, SWA_KV_WIDTH)], axis=1)
    vv_all = jnp.concatenate([cache_swa_v[l].reshape(ns, WINDOW, SWA_KV_WIDTH).astype(F32),
                              v_s.reshape(ns, ts, SWA_KV_WIDTH)], axis=1)
    pad = jnp.zeros((ns, keys_s - WINDOW - ts, SWA_KV_WIDTH), F32)
    q_rows = qzs.reshape(ns, ts, SWA_HEADS, LANES).transpose(0, 2, 1, 3).reshape(ns, SWA_HEADS * ts, LANES)
    o_dec = _swa_decode(q_rows, jnp.concatenate([kk_all, pad], axis=1), jnp.concatenate([vv_all, pad], axis=1),
                        bias_s, sink_rows, ts, 8)
    o_dec = o_dec.reshape(ns, SWA_KV_HEADS, SWA_REP, ts, SWA_KV_HEADS, SWA_HEAD_DIM)
    o_dec = jnp.stack([o_dec[:, g, :, :, g] for g in range(SWA_KV_HEADS)], axis=1)
    o_swa_s = o_dec.transpose(0, 3, 2, 1, 4).reshape(m, SWA_WIDTH).astype(BF16)

    qb = qms.reshape(ns, ts, MEM_HEADS, 1, MEM_HEAD_DIM).transpose(0, 2, 1, 3, 4)
    qb = qb * jnp.eye(MEM_HEADS, dtype=BF16).reshape(1, MEM_HEADS, 1, MEM_HEADS, 1)
    qb = qb.reshape(ns, MEM_HEADS * ts, MEM_WIDTH)
    o_md = _mem_decode(qb, cache_mem_k[l].reshape(ns, MEM_TOKENS, MEM_WIDTH),
                       cache_mem_v[l].reshape(ns, MEM_TOKENS, MEM_WIDTH), 8)
    o_md = o_md.reshape(ns, MEM_HEADS, ts, MEM_HEADS, MEM_HEAD_DIM)
    o_mem_s = jnp.stack([o_md[:, hh, :, hh] for hh in range(MEM_HEADS)], axis=2).reshape(m, MEM_WIDTH)

    hs_, xn2s, combs = _merge(xs, us, ys_ssm, o_swa_s, o_mem_s, mp, 256)
    y_sample = _moe(xn2s, combs.T, w_gu, w_d, hs_, gf, 1024).reshape(ns, ts, D_MODEL)

    new_k_s = kk_all[:, -WINDOW:].reshape(1, ns, WINDOW, SWA_KV_HEADS, SWA_HEAD_DIM).astype(cache_swa_k.dtype)
    new_v_s = vv_all[:, -WINDOW:].reshape(1, ns, WINDOW, SWA_KV_HEADS, SWA_HEAD_DIM).astype(cache_swa_v.dtype)

    return (y_prompt, y_sample,
            new_k_p, new_v_p, p_re[None], p_im[None], new_mk, new_mv,
            new_k_s, new_v_s, s_re[None].astype(state_ssm_re.dtype), s_im[None].astype(state_ssm_im.dtype))
```

```python
import functools
import math

import numpy as np
import jax
import jax.numpy as jnp
from jax import lax
from jax.experimental import pallas as pl
from jax.experimental.pallas import tpu as pltpu
from jax.experimental.pallas import tpu_sc as plsc

F32 = jnp.float32
BF16 = jnp.bfloat16

D_MODEL = 1024
SSM_WIDTH = 512
SSM_GROUP = 16
SSM_GROUPS = 32
SSM_STATE = 64
SWA_HEADS = 8
SWA_KV_HEADS = 2
SWA_REP = 4
SWA_HEAD_DIM = 64
SWA_WIDTH = 512
SWA_KV_WIDTH = 128
WINDOW = 128
REL_BUCKETS = 32
REL_MAX_DIST = 128
MEM_TOKENS = 256
MEM_HEADS = 4
MEM_HEAD_DIM = 128
MEM_WIDTH = 512
N_EXPERT_GROUPS = 4
EXPERTS_PER_GROUP = 8
N_EXPERTS = 32
D_EXPERT = 256
EPS = 1e-6
NEG_INF = -1e30

LANES = 128
GROUPS_PER_TILE = LANES // SSM_GROUP
N_CH_TILES = SSM_WIDTH // LANES
STATE_TILE = GROUPS_PER_TILE * SSM_STATE
VMEM_LIMIT = 56 * 1024 * 1024
S5_CHUNK = 8

_TRANS_B = (((1,), (1,)), ((), ()))


def _cparams(*sem):
    return pltpu.CompilerParams(dimension_semantics=sem, vmem_limit_bytes=VMEM_LIMIT)


def _rms(x, g):
    return (x * lax.rsqrt(jnp.mean(x * x, axis=-1, keepdims=True) + EPS)) * g


def _dot(a, b):
    return jnp.dot(a, b, preferred_element_type=F32)


def _norm_proj_kernel(x_ref, g_ref, w_ref, *out_refs, splits, dtypes):
    xb = _rms(x_ref[...], g_ref[...]).astype(BF16)
    off = 0
    outs = iter(out_refs)
    for width, dts in zip(splits, dtypes):
        r = _dot(xb, w_ref[:, off:off + width])
        for dt in dts:
            next(outs)[...] = r.astype(dt)
        off += width


def _norm_proj(x, g, w, splits, dtypes, tile):
    n, d = x.shape
    tile = min(tile, n)
    flat = [(wd, dt) for wd, dts in zip(splits, dtypes) for dt in dts]
    return pl.pallas_call(
        functools.partial(_norm_proj_kernel, splits=tuple(splits), dtypes=tuple(dtypes)),
        grid=(n // tile,),
        in_specs=[pl.BlockSpec((tile, d), lambda i: (i, 0)),
                  pl.BlockSpec((1, d), lambda i: (0, 0)),
                  pl.BlockSpec((d, sum(splits)), lambda i: (0, 0))],
        out_specs=[pl.BlockSpec((tile, wd), lambda i: (i, 0)) for wd, _ in flat],
        out_shape=[jax.ShapeDtypeStruct((n, wd), dt) for wd, dt in flat],
        compiler_params=_cparams("parallel"),
        name="norm_proj",
    )(x, g, w)


def _group_mask(rows_per_group, cols_per_group):
    r = np.arange(GROUPS_PER_TILE * rows_per_group)[:, None] // rows_per_group
    c = np.arange(GROUPS_PER_TILE * cols_per_group)[None, :] // cols_per_group
    return jnp.asarray(r == c, F32)


def _s5_weights(lam_re, lam_im, log_dt, bm_re, bm_im, cm_re, cm_im, L):
    hp = lax.Precision.HIGHEST
    nt, gt, P, H = N_CH_TILES, GROUPS_PER_TILE, SSM_STATE, SSM_GROUP
    lr, li = lam_re.astype(F32), lam_im.astype(F32)
    dt = jnp.exp(log_dt.astype(F32))[:, None]
    mag = jnp.exp(lr * dt)
    a_re = mag * jnp.cos(li * dt)
    a_im = mag * jnp.sin(li * dt)
    den = lr * lr + li * li
    f_re = ((a_re - 1.0) * lr + a_im * li) / den
    f_im = (a_im * lr - (a_re - 1.0) * li) / den
    br, bi = bm_re.astype(F32), bm_im.astype(F32)
    bb_re = f_re[..., None] * br - f_im[..., None] * bi
    bb_im = f_re[..., None] * bi + f_im[..., None] * br
    pr, pi = [jnp.ones_like(a_re)], [jnp.zeros_like(a_im)]
    for _ in range(L):
        pr.append(pr[-1] * a_re - pi[-1] * a_im)
        pi.append(pr[-2] * a_im + pi[-1] * a_re)
    cr, ci = cm_re.astype(F32), cm_im.astype(F32)
    ca_re = [cr * pr[k][:, None, :] - ci * pi[k][:, None, :] for k in range(L + 1)]
    ca_im = [cr * pi[k][:, None, :] + ci * pr[k][:, None, :] for k in range(L + 1)]

    def expand(w, rows_per_group, cols_per_group):
        w = w.reshape(nt, gt * rows_per_group, cols_per_group)
        return jnp.tile(w, (1, 1, gt)) * _group_mask(rows_per_group, cols_per_group)

    st_re, st_im = [], []
    for s in range(L):
        k = L - 1 - s
        w_re = pr[k][..., None] * bb_re - pi[k][..., None] * bb_im
        w_im = pr[k][..., None] * bb_im + pi[k][..., None] * bb_re
        st_re.append(expand(w_re.transpose(0, 2, 1), H, P))
        st_im.append(expand(w_im.transpose(0, 2, 1), H, P))
    w_st = jnp.concatenate([jnp.concatenate(st_re, axis=1), jnp.concatenate(st_im, axis=1)], axis=2).astype(BF16)

    so_re = jnp.concatenate([expand(ca_re[t + 1].transpose(0, 2, 1), P, H) for t in range(L)], axis=2)
    so_im = jnp.concatenate([expand(-ca_im[t + 1].transpose(0, 2, 1), P, H) for t in range(L)], axis=2)
    w_out = jnp.concatenate([so_re, so_im], axis=1).astype(BF16)

    blocks = []
    for tau in range(L):
        k_lag = (jnp.einsum('ghp,gpk->gkh', ca_re[tau], bb_re, precision=hp)
                 - jnp.einsum('ghp,gpk->gkh', ca_im[tau], bb_im, precision=hp))
        blocks.append(expand(k_lag, H, H).astype(BF16))
    zero = jnp.zeros_like(blocks[0])
    toep = jnp.concatenate(
        [jnp.concatenate([blocks[t - s] if t >= s else zero for t in range(L)], axis=2) for s in range(L)], axis=1)

    def per_tile(v):
        return v.reshape(nt, 1, STATE_TILE)

    return w_st, w_out, toep, per_tile(pr[L]), per_tile(pi[L])


def _to_chunks(u, nb, nc, L):
    return (u.reshape(nb, nc, L, N_CH_TILES, LANES).transpose(1, 0, 3, 2, 4)
            .reshape(nc * nb, N_CH_TILES * L * LANES))


def _from_chunks(y, nb, nc, L):
    return (y.reshape(nc, nb, N_CH_TILES, L, LANES).transpose(1, 0, 3, 2, 4)
            .reshape(nb * nc * L, SSM_WIDTH))


def _s5_state_in_kernel(x_ref, w_ref, d_ref):
    d_ref[...] = _dot(x_ref[...], w_ref[...])


def _s5_state_in(xc, w_st, L, row_tile):
    rows = xc.shape[0]
    row_tile = min(row_tile, rows)
    lk = L * LANES
    return pl.pallas_call(
        _s5_state_in_kernel,
        grid=(N_CH_TILES, rows // row_tile),
        in_specs=[pl.BlockSpec((row_tile, lk), lambda j, r: (r, j)),
                  pl.BlockSpec((None, lk, 2 * STATE_TILE), lambda j, r: (j, 0, 0))],
        out_specs=pl.BlockSpec((row_tile, 2 * STATE_TILE), lambda j, r: (r, j)),
        out_shape=jax.ShapeDtypeStruct((rows, N_CH_TILES * 2 * STATE_TILE), F32),
        compiler_params=_cparams("parallel", "parallel"),
        name="s5_state_in",
    )(xc, w_st)


def _s5_scan_kernel(d_ref, h0_ref, are_ref, aim_ref, hs_ref, fin_ref, hr_ref, hi_ref, *, cb, nb):
    ci = pl.program_id(1)

    @pl.when(ci == 0)
    def _():
        hr_ref[...] = h0_ref[:, 0:STATE_TILE]
        hi_ref[...] = h0_ref[:, STATE_TILE:2 * STATE_TILE]

    ar = jnp.broadcast_to(are_ref[...], (nb, STATE_TILE))
    ai = jnp.broadcast_to(aim_ref[...], (nb, STATE_TILE))

    def body(c, carry):
        hr, hi = carry
        hs_ref[c, :, 0:STATE_TILE] = hr
        hs_ref[c, :, STATE_TILE:2 * STATE_TILE] = hi
        d = d_ref[c]
        return (ar * hr - ai * hi + d[:, 0:STATE_TILE],
                ar * hi + ai * hr + d[:, STATE_TILE:2 * STATE_TILE])

    hr, hi = lax.fori_loop(0, cb, body, (hr_ref[...], hi_ref[...]))
    hr_ref[...] = hr
    hi_ref[...] = hi

    @pl.when(ci == pl.num_programs(1) - 1)
    def _():
        fin_ref[:, 0:STATE_TILE] = hr
        fin_ref[:, STATE_TILE:2 * STATE_TILE] = hi


def _s5_scan(d, h0, a_re, a_im, chunk_block):
    nc, nb, _ = d.shape
    cb = min(chunk_block, nc)
    st2 = 2 * STATE_TILE
    return pl.pallas_call(
        functools.partial(_s5_scan_kernel, cb=cb, nb=nb),
        grid=(N_CH_TILES, nc // cb),
        in_specs=[pl.BlockSpec((cb, nb, st2), lambda j, c: (c, 0, j)),
                  pl.BlockSpec((nb, st2), lambda j, c: (0, j)),
                  pl.BlockSpec((None, 1, STATE_TILE), lambda j, c: (j, 0, 0)),
                  pl.BlockSpec((None, 1, STATE_TILE), lambda j, c: (j, 0, 0))],
        out_specs=[pl.BlockSpec((cb, nb, st2), lambda j, c: (c, 0, j)),
                   pl.BlockSpec((nb, st2), lambda j, c: (0, j))],
        out_shape=[jax.ShapeDtypeStruct((nc, nb, N_CH_TILES * st2), F32),
                   jax.ShapeDtypeStruct((nb, N_CH_TILES * st2), F32)],
        scratch_shapes=[pltpu.VMEM((nb, STATE_TILE), F32), pltpu.VMEM((nb, STATE_TILE), F32)],
        compiler_params=_cparams("parallel", "arbitrary"),
        name="s5_scan",
    )(d, h0, a_re, a_im)


def _s5_out_kernel(x_ref, h_ref, t_ref, wo_ref, y_ref):
    y_ref[...] = _dot(x_ref[...], t_ref[...]) + _dot(h_ref[...].astype(BF16), wo_ref[...])


def _s5_out(xc, hs, toep, w_out, L, row_tile):
    rows = xc.shape[0]
    row_tile = min(row_tile, rows)
    lk = L * LANES
    st2 = 2 * STATE_TILE
    return pl.pallas_call(
        _s5_out_kernel,
        grid=(N_CH_TILES, rows // row_tile),
        in_specs=[pl.BlockSpec((row_tile, lk), lambda j, r: (r, j)),
                  pl.BlockSpec((row_tile, st2), lambda j, r: (r, j)),
                  pl.BlockSpec((None, lk, lk), lambda j, r: (j, 0, 0)),
                  pl.BlockSpec((None, st2, lk), lambda j, r: (j, 0, 0))],
        out_specs=pl.BlockSpec((row_tile, lk), lambda j, r: (r, j)),
        out_shape=jax.ShapeDtypeStruct((rows, N_CH_TILES * lk), F32),
        compiler_params=_cparams("parallel", "parallel"),
        name="s5_out",
    )(xc, hs, toep, w_out)


def _s5(ub, h0, weights, nb, nc, L):
    w_st, w_so, toep, a_re, a_im = weights
    xc = _to_chunks(ub, nb, nc, L)
    d = _s5_state_in(xc, w_st, L, 512)
    hs, fin = _s5_scan(d.reshape(nc, nb, -1), h0, a_re, a_im, 64)
    y = _s5_out(xc, hs.reshape(nc * nb, -1), toep, w_so, L, 512)
    return _from_chunks(y, nb, nc, L), fin


def _state_to_tiles(h_re, h_im):
    nb = h_re.shape[0]
    r = h_re.astype(F32).reshape(nb, N_CH_TILES, STATE_TILE)
    i = h_im.astype(F32).reshape(nb, N_CH_TILES, STATE_TILE)
    return jnp.concatenate([r, i], axis=-1).reshape(nb, N_CH_TILES * 2 * STATE_TILE)


def _tiles_to_state(h):
    nb = h.shape[0]
    h = h.reshape(nb, N_CH_TILES, 2, GROUPS_PER_TILE, SSM_STATE)
    return (h[:, :, 0].reshape(nb, SSM_GROUPS, SSM_STATE), h[:, :, 1].reshape(nb, SSM_GROUPS, SSM_STATE))


def _t5_bucket(dist):
    n = np.maximum(dist, 0)
    max_exact = REL_BUCKETS // 2
    nf = np.maximum(n, 1).astype(np.float32)
    large = max_exact + (np.log(nf / np.float32(max_exact)) / np.float32(math.log(REL_MAX_DIST / max_exact))
                         * np.float32(REL_BUCKETS - max_exact)).astype(np.int32)
    large = np.minimum(large, REL_BUCKETS - 1)
    return np.where(n < max_exact, n, large)


def _rel_bias(rel_table, dist):
    bucket = _t5_bucket(dist)
    tab = rel_table.astype(F32)
    out = jnp.zeros((SWA_HEADS,) + dist.shape, F32)
    for b in range(REL_BUCKETS):
        sel = jnp.asarray(bucket == b)
        if bool((bucket == b).any()):
            out = jnp.where(sel[None], tab[b].reshape((SWA_HEADS,) + (1,) * dist.ndim), out)
    return out


def _softmax_sink(s, sink):
    m = jnp.maximum(jnp.max(s, axis=-1, keepdims=True), sink)
    e = jnp.exp(s - m)
    den = jnp.sum(e, axis=-1, keepdims=True) + jnp.exp(sink - m)
    return e * (1.0 / den)


def _swa_prompt_kernel(sink_ref, q_ref, kp_ref, kc_ref, vp_ref, vc_ref, bias_ref, o_ref):
    blk = pl.program_id(1)
    kk = jnp.concatenate([kp_ref[...], kc_ref[...]], axis=0).astype(BF16)
    vv = jnp.concatenate([vp_ref[...], vc_ref[...]], axis=0).astype(BF16)
    row = lax.broadcasted_iota(jnp.int32, (WINDOW, 2 * WINDOW), 0)
    col = lax.broadcasted_iota(jnp.int32, (WINDOW, 2 * WINDOW), 1)
    dist = row + WINDOW - col
    valid = (dist >= 0) & (dist < WINDOW) & ((col >= WINDOW) | (blk > 0))
    outs = []
    for h in range(SWA_HEADS):
        s = lax.dot_general(q_ref[:, h * LANES:(h + 1) * LANES], kk, _TRANS_B, preferred_element_type=F32)
        s = jnp.where(valid, s + bias_ref[h], NEG_INF)
        p = _softmax_sink(s, sink_ref[h]).astype(BF16)
        outs.append(_dot(p, vv))
    lane = lax.broadcasted_iota(jnp.int32, (WINDOW, LANES), 1)
    for t in range(SWA_REP):
        o_ref[:, t * LANES:(t + 1) * LANES] = jnp.where(lane < SWA_HEAD_DIM, outs[t], outs[t + SWA_REP]).astype(BF16)


def _swa_prompt(qz, k, v, bias, sinks, nb, t):
    nblk = t // WINDOW
    cur = lambda b, i: (b * nblk + i, 0)
    prev = lambda b, i: (b * nblk + jnp.maximum(i - 1, 0), 0)
    return pl.pallas_call(
        _swa_prompt_kernel,
        grid=(nb, nblk),
        in_specs=[pl.BlockSpec(memory_space=pltpu.SMEM),
                  pl.BlockSpec((WINDOW, SWA_HEADS * LANES), cur),
                  pl.BlockSpec((WINDOW, SWA_KV_WIDTH), prev),
                  pl.BlockSpec((WINDOW, SWA_KV_WIDTH), cur),
                  pl.BlockSpec((WINDOW, SWA_KV_WIDTH), prev),
                  pl.BlockSpec((WINDOW, SWA_KV_WIDTH), cur),
                  pl.BlockSpec((SWA_HEADS, WINDOW, 2 * WINDOW), lambda b, i: (0, 0, 0))],
        out_specs=pl.BlockSpec((WINDOW, SWA_WIDTH), cur),
        out_shape=jax.ShapeDtypeStruct((nb * t, SWA_WIDTH), BF16),
        compiler_params=_cparams("parallel", "parallel"),
        name="swa_prompt",
    )(sinks, qz, k, k, v, v, bias)


def _swa_decode_kernel(q_ref, k_ref, v_ref, bias_ref, sink_ref, o_ref, *, seqs, tq):
    rows, keys = q_ref.shape[1], k_ref.shape[1]
    qi = lax.broadcasted_iota(jnp.int32, (rows, keys), 0) % tq
    col = lax.broadcasted_iota(jnp.int32, (rows, keys), 1)
    dist = qi + WINDOW - col
    valid = (dist >= 0) & (dist < WINDOW)
    bias = bias_ref[...]
    sink = sink_ref[...]
    for s_i in range(seqs):
        kk = k_ref[s_i].astype(BF16)
        s = lax.dot_general(q_ref[s_i], kk, _TRANS_B, preferred_element_type=F32)
        s = jnp.where(valid, s + bias, NEG_INF)
        p = _softmax_sink(s, sink).astype(BF16)
        o_ref[s_i] = _dot(p, v_ref[s_i].astype(BF16))


def _swa_decode(qz, k_all, v_all, bias, sink_rows, tq, seqs):
    nseq, rows, _ = qz.shape
    keys = k_all.shape[1]
    seqs = min(seqs, nseq)
    return pl.pallas_call(
        functools.partial(_swa_decode_kernel, seqs=seqs, tq=tq),
        grid=(nseq // seqs,),
        in_specs=[pl.BlockSpec((seqs, rows, LANES), lambda i: (i, 0, 0)),
                  pl.BlockSpec((seqs, keys, LANES), lambda i: (i, 0, 0)),
                  pl.BlockSpec((seqs, keys, LANES), lambda i: (i, 0, 0)),
                  pl.BlockSpec((rows, keys), lambda i: (0, 0)),
                  pl.BlockSpec((rows, 1), lambda i: (0, 0))],
        out_specs=pl.BlockSpec((seqs, rows, LANES), lambda i: (i, 0, 0)),
        out_shape=jax.ShapeDtypeStruct((nseq, rows, LANES), F32),
        compiler_params=_cparams("parallel"),
        name="swa_decode",
    )(qz, k_all, v_all, bias, sink_rows)


def _softmax(s):
    m = jnp.max(s, axis=-1, keepdims=True)
    e = jnp.exp(s - m)
    return e * (1.0 / jnp.sum(e, axis=-1, keepdims=True))


def _mem_prompt_kernel(q_ref, k_ref, v_ref, o_ref):
    scale = MEM_HEAD_DIM ** -0.5
    for h in range(MEM_HEADS):
        sl = slice(h * MEM_HEAD_DIM, (h + 1) * MEM_HEAD_DIM)
        s = lax.dot_general(q_ref[:, sl], k_ref[:, sl].astype(BF16), _TRANS_B, preferred_element_type=F32) * scale
        p = _softmax(s).astype(BF16)
        o_ref[:, sl] = _dot(p, v_ref[:, sl].astype(BF16)).astype(BF16)


def _mem_prompt(qm, mk, mv, nb, t, tile):
    tile = min(tile, t)
    nt = t // tile
    return pl.pallas_call(
        _mem_prompt_kernel,
        grid=(nb, nt),
        in_specs=[pl.BlockSpec((tile, MEM_WIDTH), lambda b, i: (b * nt + i, 0)),
                  pl.BlockSpec((MEM_TOKENS, MEM_WIDTH), lambda b, i: (b, 0)),
                  pl.BlockSpec((MEM_TOKENS, MEM_WIDTH), lambda b, i: (b, 0))],
        out_specs=pl.BlockSpec((tile, MEM_WIDTH), lambda b, i: (b * nt + i, 0)),
        out_shape=jax.ShapeDtypeStruct((nb * t, MEM_WIDTH), BF16),
        compiler_params=_cparams("parallel", "parallel"),
        name="mem_prompt",
    )(qm, mk, mv)


def _mem_decode_kernel(q_ref, k_ref, v_ref, o_ref, *, seqs):
    scale = MEM_HEAD_DIM ** -0.5
    for s_i in range(seqs):
        s = lax.dot_general(q_ref[s_i], k_ref[s_i].astype(BF16), _TRANS_B, preferred_element_type=F32) * scale
        p = _softmax(s).astype(BF16)
        o_ref[s_i] = _dot(p, v_ref[s_i].astype(BF16)).astype(BF16)


def _mem_decode(qb, k, v, seqs):
    nseq, rows, _ = qb.shape
    seqs = min(seqs, nseq)
    return pl.pallas_call(
        functools.partial(_mem_decode_kernel, seqs=seqs),
        grid=(nseq // seqs,),
        in_specs=[pl.BlockSpec((seqs, rows, MEM_WIDTH), lambda i: (i, 0, 0)),
                  pl.BlockSpec((seqs, MEM_TOKENS, MEM_WIDTH), lambda i: (i, 0, 0)),
                  pl.BlockSpec((seqs, MEM_TOKENS, MEM_WIDTH), lambda i: (i, 0, 0))],
        out_specs=pl.BlockSpec((seqs, rows, MEM_WIDTH), lambda i: (i, 0, 0)),
        out_shape=jax.ShapeDtypeStruct((nseq, rows, MEM_WIDTH), BF16),
        compiler_params=_cparams("parallel"),
        name="mem_decode",
    )(qb, k, v)


ROUTER_ROWS = 40
ROUTE_ROWS = 8
HALF = D_MODEL // 2


def _pack_halves(xb):
    hi = pltpu.bitcast(xb[:, 0:HALF].astype(F32), jnp.int32)
    lo = pltpu.bitcast(xb[:, HALF:D_MODEL].astype(F32), jnp.int32)
    return hi | lax.shift_right_logical(lo, jnp.int32(16))


def _unpack_halves(p):
    hi = pltpu.bitcast(p & jnp.int32(-65536), F32).astype(BF16)
    lo = pltpu.bitcast(lax.shift_left(p, jnp.int32(16)), F32).astype(BF16)
    return hi, lo


def _merge_kernel(x_ref, u_ref, y_ref, os_ref, om_ref, g1_ref, wg_ref, dsk_ref, wglu_ref, bglu_ref,
                  wbs_ref, wbw_ref, wbm_ref, wout_ref, g2_ref, wr_ref, br_ref,
                  h_ref, xn2_ref, route_ref):
    x = x_ref[...]
    tt = x.shape[0]
    xb = _rms(x, g1_ref[...]).astype(BF16)
    z = jax.nn.gelu(y_ref[...] + dsk_ref[...] * u_ref[...])
    z = z * jax.nn.sigmoid(_dot(z.astype(BF16), wglu_ref[...]) + bglu_ref[...])
    merged = jax.nn.sigmoid(_dot(xb, wg_ref[:, 0:D_MODEL])) * _dot(z.astype(BF16), wbs_ref[...])
    merged = merged + jax.nn.sigmoid(_dot(xb, wg_ref[:, D_MODEL:2 * D_MODEL])) * _dot(os_ref[...], wbw_ref[...])
    merged = merged + jax.nn.sigmoid(_dot(xb, wg_ref[:, 2 * D_MODEL:3 * D_MODEL])) * _dot(om_ref[...], wbm_ref[...])
    h = x + _dot(merged.astype(BF16), wout_ref[...])
    h_ref[...] = h
    xn2 = _rms(h, g2_ref[...]).astype(BF16)
    xn2_ref[...] = _pack_halves(xn2)

    lt = lax.dot_general(wr_ref[...], xn2, _TRANS_B, preferred_element_type=F32) + br_ref[...]
    gl = lt[N_EXPERTS:N_EXPERTS + N_EXPERT_GROUPS]
    ge = jnp.exp(gl - jnp.max(gl, axis=0, keepdims=True))
    gp = ge / jnp.sum(ge, axis=0, keepdims=True)
    gw = jnp.max(gp, axis=0, keepdims=True)
    gidx = jnp.full((1, tt), N_EXPERT_GROUPS - 1, jnp.int32)
    for r in range(N_EXPERT_GROUPS - 2, -1, -1):
        gidx = jnp.where(gp[r:r + 1] == gw, r, gidx)
    ein = lt[(N_EXPERT_GROUPS - 1) * EXPERTS_PER_GROUP:N_EXPERTS]
    for r in range(N_EXPERT_GROUPS - 2, -1, -1):
        ein = jnp.where(gidx == r, lt[r * EXPERTS_PER_GROUP:(r + 1) * EXPERTS_PER_GROUP], ein)
    ee = jnp.exp(ein - jnp.max(ein, axis=0, keepdims=True))
    ep = ee / jnp.sum(ee, axis=0, keepdims=True)
    rowi = lax.broadcasted_iota(jnp.int32, (EXPERTS_PER_GROUP, tt), 0)
    p1 = jnp.max(ep, axis=0, keepdims=True)
    e1 = jnp.min(jnp.where(ep == p1, rowi, EXPERTS_PER_GROUP), axis=0, keepdims=True)
    ep2 = jnp.where(rowi == e1, -1.0, ep)
    p2 = jnp.max(ep2, axis=0, keepdims=True)
    e2 = jnp.min(jnp.where(ep2 == p2, rowi, EXPERTS_PER_GROUP), axis=0, keepdims=True)
    tot = p1 + p2
    w1 = p1 / tot * gw
    w2 = p2 / tot * gw
    id1 = (gidx * EXPERTS_PER_GROUP + e1).astype(F32)
    id2 = (gidx * EXPERTS_PER_GROUP + e2).astype(F32)
    route_ref[...] = jnp.concatenate([id1, id2, w1, w2, jnp.zeros((ROUTE_ROWS - 4, tt), F32)], axis=0)


def _merge(x, u, y, o_swa, o_mem, p, tile):
    n = x.shape[0]
    tile = min(tile, n)
    row = lambda i: (i, 0)
    const = lambda i: (0, 0)
    full = lambda a: pl.BlockSpec(a.shape, const)
    weights = [p['g1'], p['w_gates'], p['d_skip'], p['w_glu'], p['b_glu'], p['w_br_ssm'], p['w_br_swa'],
               p['w_br_mem'], p['w_out'], p['g2'], p['w_router'], p['b_router']]
    return pl.pallas_call(
        _merge_kernel,
        grid=(n // tile,),
        in_specs=[pl.BlockSpec((tile, D_MODEL), row), pl.BlockSpec((tile, SSM_WIDTH), row),
                  pl.BlockSpec((tile, SSM_WIDTH), row), pl.BlockSpec((tile, SWA_WIDTH), row),
                  pl.BlockSpec((tile, MEM_WIDTH), row)] + [full(w) for w in weights],
        out_specs=[pl.BlockSpec((tile, D_MODEL), row), pl.BlockSpec((tile, HALF), row),
                   pl.BlockSpec((ROUTE_ROWS, tile), lambda i: (0, i))],
        out_shape=[jax.ShapeDtypeStruct((n, D_MODEL), F32), jax.ShapeDtypeStruct((n, HALF), jnp.int32),
                   jax.ShapeDtypeStruct((ROUTE_ROWS, n), F32)],
        compiler_params=_cparams("parallel"),
        name="merge_router",
    )(x, u, y, o_swa, o_mem, *weights)


def _expert_mlp(xp, wgu_ref, wd_ref):
    hi, lo = _unpack_halves(xp)
    hgu = _dot(hi, wgu_ref[0:HALF, :]) + _dot(lo, wgu_ref[HALF:D_MODEL, :])
    hh = jax.nn.silu(hgu[:, 0:D_EXPERT]) * hgu[:, D_EXPERT:2 * D_EXPERT]
    return _dot(hh.astype(BF16), wd_ref[...])


def _moe_kernel(xn2_ref, rt_ref, wgu_ref, wd_ref, h_ref, gf_ref, o_ref, acc_ref):
    e = pl.program_id(1)

    @pl.when(e == 0)
    def _():
        acc_ref[...] = jnp.zeros_like(acc_ref)

    o = _expert_mlp(xn2_ref[...], wgu_ref, wd_ref)
    ef = e.astype(F32)
    c = (jnp.where(rt_ref[:, 0:1] == ef, rt_ref[:, 2:3], 0.0)
         + jnp.where(rt_ref[:, 1:2] == ef, rt_ref[:, 3:4], 0.0))
    acc_ref[...] += c * o

    @pl.when(e == N_EXPERTS - 1)
    def _():
        o_ref[...] = _rms(h_ref[...] + acc_ref[...], gf_ref[...])


def _moe(xn2, route_t, w_gu, w_d, h, gf, tile):
    n = h.shape[0]
    tile = min(tile, n)
    return pl.pallas_call(
        _moe_kernel,
        grid=(n // tile, N_EXPERTS),
        in_specs=[pl.BlockSpec((tile, HALF), lambda i, e: (i, 0)),
                  pl.BlockSpec((tile, ROUTE_ROWS), lambda i, e: (i, 0)),
                  pl.BlockSpec((None, D_MODEL, 2 * D_EXPERT), lambda i, e: (e, 0, 0)),
                  pl.BlockSpec((None, D_EXPERT, D_MODEL), lambda i, e: (e, 0, 0)),
                  pl.BlockSpec((tile, D_MODEL), lambda i, e: (i, 0)),
                  pl.BlockSpec((1, D_MODEL), lambda i, e: (0, 0))],
        out_specs=pl.BlockSpec((tile, D_MODEL), lambda i, e: (i, 0)),
        out_shape=jax.ShapeDtypeStruct((n, D_MODEL), F32),
        scratch_shapes=[pltpu.VMEM((tile, D_MODEL), F32)],
        compiler_params=_cparams("parallel", "arbitrary"),
        name="moe_final_norm",
    )(xn2, route_t, w_gu, w_d, h, gf)


EXPERT_ROW_TILE = 256
SC_CORES = 2
SC_SUBCORES = 16
SC_WORKERS = SC_CORES * SC_SUBCORES
SC_SCATTER_ROWS = 64
SC_GATHER_ROWS = 32


def _route_rank_kernel(r_ref, rank_ref, cnt_ref, base_ref):
    i = pl.program_id(0)
    tt = r_ref.shape[1]

    @pl.when(i == 0)
    def _():
        base_ref[...] = jnp.zeros_like(base_ref)

    ids = r_ref[0:2, :].astype(jnp.int32)
    e_iota = lax.broadcasted_iota(jnp.int32, (N_EXPERTS, tt), 0)
    oh1 = jnp.where(e_iota == ids[0:1], 1.0, 0.0)
    oh2 = jnp.where(e_iota == ids[1:2], 1.0, 0.0)
    before = (lax.broadcasted_iota(jnp.int32, (tt, tt), 0) < lax.broadcasted_iota(jnp.int32, (tt, tt), 1))
    tri = jnp.where(before, 1.0, 0.0).astype(BF16)
    c1 = _dot(oh1.astype(BF16), tri)
    c2 = _dot(oh2.astype(BF16), tri)
    tot1 = jnp.sum(oh1, axis=1, keepdims=True)
    tot2 = jnp.sum(oh2, axis=1, keepdims=True)
    base = base_ref[:, 0:1]
    rank1 = jnp.sum(oh1 * (base + c1), axis=0, keepdims=True)
    rank2 = jnp.sum(oh2 * (base + tot1 + c2), axis=0, keepdims=True)
    rank_ref[...] = jnp.concatenate([rank1, rank2, jnp.zeros((ROUTE_ROWS - 2, tt), F32)], axis=0).astype(jnp.int32)
    new_base = jnp.broadcast_to(base + tot1 + tot2, base_ref.shape)
    base_ref[...] = new_base
    cnt_ref[...] = new_base.astype(jnp.int32)


def _route_rank(route, tile):
    n = route.shape[1]
    tile = min(tile, n)
    return pl.pallas_call(
        _route_rank_kernel,
        grid=(n // tile,),
        in_specs=[pl.BlockSpec((ROUTE_ROWS, tile), lambda i: (0, i))],
        out_specs=[pl.BlockSpec((ROUTE_ROWS, tile), lambda i: (0, i)),
                   pl.BlockSpec((N_EXPERTS, LANES), lambda i: (0, 0))],
        out_shape=[jax.ShapeDtypeStruct((ROUTE_ROWS, n), jnp.int32),
                   jax.ShapeDtypeStruct((N_EXPERTS, LANES), jnp.int32)],
        scratch_shapes=[pltpu.VMEM((N_EXPERTS, LANES), F32)],
        compiler_params=_cparams("arbitrary"),
        name="route_rank",
    )(route)


def _sc_mesh():
    return plsc.VectorSubcoreMesh(core_axis_name="core", subcore_axis_name="subcore")


def _sc_scatter_pairs(x, pos, rows_out):
    n, d = x.shape
    per_w = n // SC_WORKERS
    window = min(SC_SCATTER_ROWS, per_w)

    @pl.kernel(out_type=jax.ShapeDtypeStruct((rows_out, d), x.dtype), mesh=_sc_mesh(),
               scratch_types=[pltpu.VMEM((window,), jnp.int32), pltpu.VMEM((window,), jnp.int32),
                              pltpu.VMEM((window, d), x.dtype), pltpu.SemaphoreType.DMA])
    def scatter(x_hbm, p_hbm, o_hbm, i1_v, i2_v, rows_v, sem):
        wid = lax.axis_index("subcore") * SC_CORES + lax.axis_index("core")

        @pl.loop(0, per_w // window)
        def _(j):
            base = wid * per_w + j * window
            pltpu.sync_copy(p_hbm.at[pl.ds(base, window)], i1_v)
            pltpu.sync_copy(p_hbm.at[pl.ds(n + base, window)], i2_v)
            pltpu.sync_copy(x_hbm.at[pl.ds(base, window)], rows_v)
            pltpu.async_copy(rows_v, o_hbm.at[i1_v], sem).wait()
            pltpu.async_copy(rows_v, o_hbm.at[i2_v], sem).wait()

    return scatter(x, pos)


def _sc_gather_rows(table, idx):
    m = idx.shape[0]
    d = table.shape[1]
    per_w = m // SC_WORKERS
    window = min(SC_GATHER_ROWS, per_w)

    @pl.kernel(out_type=jax.ShapeDtypeStruct((m, d), table.dtype), mesh=_sc_mesh(),
               scratch_types=[pltpu.VMEM((window,), jnp.int32), pltpu.VMEM((window, d), table.dtype),
                              pltpu.SemaphoreType.DMA])
    def gather(t_hbm, i_hbm, o_hbm, i_v, rows_v, sem):
        wid = lax.axis_index("subcore") * SC_CORES + lax.axis_index("core")

        @pl.loop(0, per_w // window)
        def _(j):
            base = wid * per_w + j * window
            pltpu.sync_copy(i_hbm.at[pl.ds(base, window)], i_v)
            pltpu.async_copy(t_hbm.at[i_v], rows_v, sem).wait()
            pltpu.sync_copy(rows_v, o_hbm.at[pl.ds(base, window)])

    return gather(table, idx)


def _expert_tiles_kernel(te_ref, nu_ref, x_ref, wgu_ref, wd_ref, o_ref):
    @pl.when(pl.program_id(0) < nu_ref[0])
    def _():
        o_ref[...] = _expert_mlp(x_ref[...], wgu_ref, wd_ref)


def _expert_tiles(tile_expert, n_used, xs, w_gu, w_d):
    rows = xs.shape[0]
    tm = EXPERT_ROW_TILE
    grid_spec = pltpu.PrefetchScalarGridSpec(
        num_scalar_prefetch=2,
        grid=(rows // tm,),
        in_specs=[pl.BlockSpec((tm, HALF), lambda i, te, nu: (i, 0)),
                  pl.BlockSpec((None, D_MODEL, 2 * D_EXPERT), lambda i, te, nu: (te[i], 0, 0)),
                  pl.BlockSpec((None, D_EXPERT, D_MODEL), lambda i, te, nu: (te[i], 0, 0))],
        out_specs=pl.BlockSpec((tm, D_MODEL), lambda i, te, nu: (i, 0)),
    )
    return pl.pallas_call(
        _expert_tiles_kernel,
        grid_spec=grid_spec,
        out_shape=jax.ShapeDtypeStruct((rows, D_MODEL), F32),
        compiler_params=_cparams("arbitrary"),
        name="expert_tiles",
    )(tile_expert, n_used, xs, w_gu, w_d)


def _combine_kernel(h_ref, o1_ref, o2_ref, rt_ref, gf_ref, y_ref):
    moe = rt_ref[:, 2:3] * o1_ref[...] + rt_ref[:, 3:4] * o2_ref[...]
    y_ref[...] = _rms(h_ref[...] + moe, gf_ref[...])


def _combine(h, o12, route_t, gf, tile):
    n = h.shape[0]
    tile = min(tile, n)
    nt = n // tile
    return pl.pallas_call(
        _combine_kernel,
        grid=(nt,),
        in_specs=[pl.BlockSpec((tile, D_MODEL), lambda i: (i, 0)),
                  pl.BlockSpec((tile, D_MODEL), lambda i: (i, 0)),
                  pl.BlockSpec((tile, D_MODEL), lambda i: (i + nt, 0)),
                  pl.BlockSpec((tile, ROUTE_ROWS), lambda i: (i, 0)),
                  pl.BlockSpec((1, D_MODEL), lambda i: (0, 0))],
        out_specs=pl.BlockSpec((tile, D_MODEL), lambda i: (i, 0)),
        out_shape=jax.ShapeDtypeStruct((n, D_MODEL), F32),
        compiler_params=_cparams("parallel"),
        name="combine_final_norm",
    )(h, o12, o12, route_t, gf)


def _sparse_moe(xn2p, route, h, w_gu, w_d, gf):
    n = h.shape[0]
    tm = EXPERT_ROW_TILE
    rows = 2 * n + N_EXPERTS * tm
    rank, cnt = _route_rank(route, 512)
    counts = cnt[:, 0]
    padded = (counts + tm - 1) // tm * tm
    ends = jnp.cumsum(padded)
    starts = ends - padded
    ids = route[0:2].astype(jnp.int32)
    pos = (starts[ids] + rank[0:2]).reshape(2 * n)
    tile_start = jnp.arange(rows // tm, dtype=jnp.int32) * tm
    tile_expert = jnp.minimum(jnp.searchsorted(ends, tile_start, side='right'), N_EXPERTS - 1).astype(jnp.int32)
    n_used = (ends[-1:] // tm).astype(jnp.int32)
    xs = _sc_scatter_pairs(xn2p, pos, rows)
    os_ = _expert_tiles(tile_expert, n_used, xs, w_gu, w_d)
    o12 = _sc_gather_rows(os_, pos)
    return _combine(h, o12, route.T, gf, 512)


def _prep_in_weights(w_in):
    o = 0
    w_u = w_in[:, o:o + SSM_WIDTH]; o += SSM_WIDTH
    w_q = w_in[:, o:o + SWA_WIDTH]; o += SWA_WIDTH
    w_k = w_in[:, o:o + SWA_KV_WIDTH]; o += SWA_KV_WIDTH
    w_v = w_in[:, o:o + SWA_KV_WIDTH]; o += SWA_KV_WIDTH
    w_qm = w_in[:, o:o + MEM_WIDTH]; o += MEM_WIDTH
    w_g = w_in[:, o:]
    zeros = jnp.zeros((D_MODEL, SWA_HEAD_DIM), w_in.dtype)
    tiles = []
    for h in range(SWA_HEADS):
        qh = w_q[:, h * SWA_HEAD_DIM:(h + 1) * SWA_HEAD_DIM] * (SWA_HEAD_DIM ** -0.5)
        tiles.extend([qh, zeros] if h // SWA_REP == 0 else [zeros, qh])
    w_main = jnp.concatenate([w_u] + tiles + [w_k, w_v, w_qm], axis=1).astype(BF16)
    return w_main, w_g.astype(BF16)


IN_SPLITS = (SSM_WIDTH, SWA_HEADS * LANES, SWA_KV_WIDTH, SWA_KV_WIDTH, MEM_WIDTH)
IN_DTYPES = ((F32, BF16), (BF16,), (F32,), (F32,), (BF16,))


def kernel(x_prompt, x_sample, cache_swa_k, cache_swa_v, state_ssm_re, state_ssm_im, cache_mem_k, cache_mem_v, mem_prompt, norm1_g, w_in, lam_re, lam_im, log_dt, bm_re, bm_im, cm_re, cm_im, d_skip, w_glu, b_glu, sinks, rel_table, mem_norm_g, w_mem_kv, w_br_ssm, w_br_swa, w_br_mem, w_out, norm2_g, w_rg, b_rg, w_rexp, b_rexp, w_e_gate, w_e_up, w_e_down, final_norm_g):
    nb, t, _ = x_prompt.shape
    ns, ts, _ = x_sample.shape
    l = 0
    L = S5_CHUNK

    w_main, w_gates = _prep_in_weights(w_in[l])
    w_swa = (w_br_swa[l].reshape(SWA_KV_HEADS, SWA_REP, SWA_HEAD_DIM, D_MODEL).transpose(1, 0, 2, 3)
             .reshape(SWA_WIDTH, D_MODEL))
    pad_rows = ROUTER_ROWS - N_EXPERTS - N_EXPERT_GROUPS
    w_router = jnp.concatenate([w_rexp[l].T, w_rg[l].T, jnp.zeros((pad_rows, D_MODEL), F32)], axis=0).astype(BF16)
    b_router = jnp.concatenate([b_rexp[l], b_rg[l], jnp.zeros((pad_rows,), F32)]).reshape(ROUTER_ROWS, 1)
    mp = {
        'g1': norm1_g[l].reshape(1, D_MODEL), 'w_gates': w_gates, 'd_skip': d_skip[l].reshape(1, SSM_WIDTH),
        'w_glu': w_glu[l].astype(BF16), 'b_glu': b_glu[l].reshape(1, SSM_WIDTH),
        'w_br_ssm': w_br_ssm[l].astype(BF16), 'w_br_swa': w_swa.astype(BF16),
        'w_br_mem': w_br_mem[l].astype(BF16), 'w_out': w_out[l].astype(BF16),
        'g2': norm2_g[l].reshape(1, D_MODEL), 'w_router': w_router, 'b_router': b_router,
    }
    w_gu = jnp.concatenate([w_e_gate[l], w_e_up[l]], axis=-1).astype(BF16)
    w_d = w_e_down[l].astype(BF16)
    gf = final_norm_g.reshape(1, D_MODEL)
    s5_w = _s5_weights(lam_re[l], lam_im[l], log_dt[l], bm_re[l], bm_im[l], cm_re[l], cm_im[l], L)

    bias_p = _rel_bias(rel_table, np.arange(WINDOW)[:, None] + WINDOW - np.arange(2 * WINDOW)[None, :])
    keys_s = WINDOW + 2 * ts
    bias_s = _rel_bias(rel_table, np.arange(ts)[:, None] + WINDOW - np.arange(keys_s)[None, :])
    bias_s = bias_s.reshape(SWA_HEADS * ts, keys_s)
    sink_rows = jnp.repeat(sinks[l].astype(F32), ts).reshape(SWA_HEADS * ts, 1)

    n = nb * t
    xp = x_prompt.reshape(n, D_MODEL)
    mk, mv = _norm_proj(mem_prompt.reshape(nb * MEM_TOKENS, D_MODEL), mem_norm_g[l].reshape(1, D_MODEL),
                        w_mem_kv[l].astype(BF16), (MEM_WIDTH, MEM_WIDTH), ((F32,), (F32,)), 512)
    u, ub, qz, k, v, qm = _norm_proj(xp, mp['g1'], w_main, IN_SPLITS, IN_DTYPES, 512)

    y_ssm, fin = _s5(ub, jnp.zeros((nb, N_CH_TILES * 2 * STATE_TILE), F32), s5_w, nb, t // L, L)
    p_re, p_im = _tiles_to_state(fin)

    o_swa = _swa_prompt(qz, k, v, bias_p, sinks[l].astype(F32), nb, t)
    o_mem = _mem_prompt(qm, mk, mv, nb, t, 512)
    h, xn2p, route = _merge(xp, u, y_ssm, o_swa, o_mem, mp, 256)
    y_prompt = _sparse_moe(xn2p, route, h, w_gu, w_d, gf).reshape(nb, t, D_MODEL)

    k4 = k.reshape(nb, t, SWA_KV_HEADS, SWA_HEAD_DIM)
    v4 = v.reshape(nb, t, SWA_KV_HEADS, SWA_HEAD_DIM)
    new_k_p, new_v_p = k4[:, -WINDOW:][None], v4[:, -WINDOW:][None]
    new_mk = mk.reshape(1, nb, MEM_TOKENS, MEM_HEADS, MEM_HEAD_DIM)
    new_mv = mv.reshape(1, nb, MEM_TOKENS, MEM_HEADS, MEM_HEAD_DIM)

    m = ns * ts
    xs = x_sample.reshape(m, D_MODEL)
    us, ubs, qzs, k_s, v_s, qms = _norm_proj(xs, mp['g1'], w_main, IN_SPLITS, IN_DTYPES, 256)
    ys_ssm, fins = _s5(ubs, _state_to_tiles(state_ssm_re[l], state_ssm_im[l]), s5_w, ns, ts // L, L)
    s_re, s_im = _tiles_to_state(fins)

    kk_all = jnp.concatenate([cache_swa_k[l].reshape(ns, WINDOW, SWA_KV_WIDTH).astype(F32),
                              k_s.reshape(ns, ts, SWA_KV_WIDTH)], axis=1)
    vv_all = jnp.concatenate([cache_swa_v[l].reshape(ns, WINDOW, SWA_KV_WIDTH).astype(F32),
                              v_s.reshape(ns, ts, SWA_KV_WIDTH)], axis=1)
    pad = jnp.zeros((ns, keys_s - WINDOW - ts, SWA_KV_WIDTH), F32)
    q_rows = qzs.reshape(ns, ts, SWA_HEADS, LANES).transpose(0, 2, 1, 3).reshape(ns, SWA_HEADS * ts, LANES)
    o_dec = _swa_decode(q_rows, jnp.concatenate([kk_all, pad], axis=1), jnp.concatenate([vv_all, pad], axis=1),
                        bias_s, sink_rows, ts, 8)
    o_dec = o_dec.reshape(ns, SWA_KV_HEADS, SWA_REP, ts, SWA_KV_HEADS, SWA_HEAD_DIM)
    o_dec = jnp.stack([o_dec[:, g, :, :, g] for g in range(SWA_KV_HEADS)], axis=1)
    o_swa_s = o_dec.transpose(0, 3, 2, 1, 4).reshape(m, SWA_WIDTH).astype(BF16)

    qb = qms.reshape(ns, ts, MEM_HEADS, 1, MEM_HEAD_DIM).transpose(0, 2, 1, 3, 4)
    qb = qb * jnp.eye(MEM_HEADS, dtype=BF16).reshape(1, MEM_HEADS, 1, MEM_HEADS, 1)
    qb = qb.reshape(ns, MEM_HEADS * ts, MEM_WIDTH)
    o_md = _mem_decode(qb, cache_mem_k[l].reshape(ns, MEM_TOKENS, MEM_WIDTH),
                       cache_mem_v[l].reshape(ns, MEM_TOKENS, MEM_WIDTH), 8)
    o_md = o_md.reshape(ns, MEM_HEADS, ts, MEM_HEADS, MEM_HEAD_DIM)
    o_mem_s = jnp.stack([o_md[:, hh, :, hh] for hh in range(MEM_HEADS)], axis=2).reshape(m, MEM_WIDTH)

    hs_, xn2ps, routes = _merge(xs, us, ys_ssm, o_swa_s, o_mem_s, mp, 256)
    y_sample = _moe(xn2ps, routes.T, w_gu, w_d, hs_, gf, 1024).reshape(ns, ts, D_MODEL)

    new_k_s = kk_all[:, -WINDOW:].reshape(1, ns, WINDOW, SWA_KV_HEADS, SWA_HEAD_DIM).astype(cache_swa_k.dtype)
    new_v_s = vv_all[:, -WINDOW:].reshape(1, ns, WINDOW, SWA_KV_HEADS, SWA_HEAD_DIM).astype(cache_swa_v.dtype)

    return (y_prompt, y_sample,
            new_k_p, new_v_p, p_re[None], p_im[None], new_mk, new_mv,
            new_k_s, new_v_s, s_re[None].astype(state_ssm_re.dtype), s_im[None].astype(state_ssm_im.dtype))
```

```python
import functools
import math

import numpy as np
import jax
import jax.numpy as jnp
from jax import lax
from jax.experimental import pallas as pl
from jax.experimental.pallas import tpu as pltpu
from jax.experimental.pallas import tpu_sc as plsc

F32 = jnp.float32
BF16 = jnp.bfloat16

D_MODEL = 1024
SSM_WIDTH = 512
SSM_GROUP = 16
SSM_GROUPS = 32
SSM_STATE = 64
SWA_HEADS = 8
SWA_KV_HEADS = 2
SWA_REP = 4
SWA_HEAD_DIM = 64
SWA_WIDTH = 512
SWA_KV_WIDTH = 128
WINDOW = 128
REL_BUCKETS = 32
REL_MAX_DIST = 128
MEM_TOKENS = 256
MEM_HEADS = 4
MEM_HEAD_DIM = 128
MEM_WIDTH = 512
N_EXPERT_GROUPS = 4
EXPERTS_PER_GROUP = 8
N_EXPERTS = 32
D_EXPERT = 256
EPS = 1e-6
NEG_INF = -1e30

LANES = 128
GROUPS_PER_TILE = LANES // SSM_GROUP
N_CH_TILES = SSM_WIDTH // LANES
STATE_TILE = GROUPS_PER_TILE * SSM_STATE
VMEM_LIMIT = 56 * 1024 * 1024
S5_CHUNK = 8

_TRANS_B = (((1,), (1,)), ((), ()))


def _cparams(*sem):
    return pltpu.CompilerParams(dimension_semantics=sem, vmem_limit_bytes=VMEM_LIMIT)


def _rms(x, g):
    return (x * lax.rsqrt(jnp.mean(x * x, axis=-1, keepdims=True) + EPS)) * g


def _dot(a, b):
    return jnp.dot(a, b, preferred_element_type=F32)


def _norm_proj_kernel(x_ref, g_ref, w_ref, *out_refs, splits, dtypes):
    xb = _rms(x_ref[...], g_ref[...]).astype(BF16)
    off = 0
    outs = iter(out_refs)
    for width, dts in zip(splits, dtypes):
        r = _dot(xb, w_ref[:, off:off + width])
        for dt in dts:
            next(outs)[...] = r.astype(dt)
        off += width


def _norm_proj(x, g, w, splits, dtypes, tile):
    n, d = x.shape
    tile = min(tile, n)
    flat = [(wd, dt) for wd, dts in zip(splits, dtypes) for dt in dts]
    return pl.pallas_call(
        functools.partial(_norm_proj_kernel, splits=tuple(splits), dtypes=tuple(dtypes)),
        grid=(n // tile,),
        in_specs=[pl.BlockSpec((tile, d), lambda i: (i, 0)),
                  pl.BlockSpec((1, d), lambda i: (0, 0)),
                  pl.BlockSpec((d, sum(splits)), lambda i: (0, 0))],
        out_specs=[pl.BlockSpec((tile, wd), lambda i: (i, 0)) for wd, _ in flat],
        out_shape=[jax.ShapeDtypeStruct((n, wd), dt) for wd, dt in flat],
        compiler_params=_cparams("parallel"),
        name="norm_proj",
    )(x, g, w)


def _group_mask(rows_per_group, cols_per_group):
    r = np.arange(GROUPS_PER_TILE * rows_per_group)[:, None] // rows_per_group
    c = np.arange(GROUPS_PER_TILE * cols_per_group)[None, :] // cols_per_group
    return jnp.asarray(r == c, F32)


def _s5_weights(lam_re, lam_im, log_dt, bm_re, bm_im, cm_re, cm_im, L):
    hp = lax.Precision.HIGHEST
    nt, gt, P, H = N_CH_TILES, GROUPS_PER_TILE, SSM_STATE, SSM_GROUP
    lr, li = lam_re.astype(F32), lam_im.astype(F32)
    dt = jnp.exp(log_dt.astype(F32))[:, None]
    mag = jnp.exp(lr * dt)
    a_re = mag * jnp.cos(li * dt)
    a_im = mag * jnp.sin(li * dt)
    den = lr * lr + li * li
    f_re = ((a_re - 1.0) * lr + a_im * li) / den
    f_im = (a_im * lr - (a_re - 1.0) * li) / den
    br, bi = bm_re.astype(F32), bm_im.astype(F32)
    bb_re = f_re[..., None] * br - f_im[..., None] * bi
    bb_im = f_re[..., None] * bi + f_im[..., None] * br
    pr, pi = [jnp.ones_like(a_re)], [jnp.zeros_like(a_im)]
    for _ in range(L):
        pr.append(pr[-1] * a_re - pi[-1] * a_im)
        pi.append(pr[-2] * a_im + pi[-1] * a_re)
    cr, ci = cm_re.astype(F32), cm_im.astype(F32)
    ca_re = [cr * pr[k][:, None, :] - ci * pi[k][:, None, :] for k in range(L + 1)]
    ca_im = [cr * pi[k][:, None, :] + ci * pr[k][:, None, :] for k in range(L + 1)]

    def expand(w, rows_per_group, cols_per_group):
        w = w.reshape(nt, gt * rows_per_group, cols_per_group)
        return jnp.tile(w, (1, 1, gt)) * _group_mask(rows_per_group, cols_per_group)

    st_re, st_im = [], []
    for s in range(L):
        k = L - 1 - s
        w_re = pr[k][..., None] * bb_re - pi[k][..., None] * bb_im
        w_im = pr[k][..., None] * bb_im + pi[k][..., None] * bb_re
        st_re.append(expand(w_re.transpose(0, 2, 1), H, P))
        st_im.append(expand(w_im.transpose(0, 2, 1), H, P))
    w_st = jnp.concatenate([jnp.concatenate(st_re, axis=1), jnp.concatenate(st_im, axis=1)], axis=2).astype(BF16)

    so_re = jnp.concatenate([expand(ca_re[t + 1].transpose(0, 2, 1), P, H) for t in range(L)], axis=2)
    so_im = jnp.concatenate([expand(-ca_im[t + 1].transpose(0, 2, 1), P, H) for t in range(L)], axis=2)
    w_out = jnp.concatenate([so_re, so_im], axis=1).astype(BF16)

    blocks = []
    for tau in range(L):
        k_lag = (jnp.einsum('ghp,gpk->gkh', ca_re[tau], bb_re, precision=hp)
                 - jnp.einsum('ghp,gpk->gkh', ca_im[tau], bb_im, precision=hp))
        blocks.append(expand(k_lag, H, H).astype(BF16))
    zero = jnp.zeros_like(blocks[0])
    toep = jnp.concatenate(
        [jnp.concatenate([blocks[t - s] if t >= s else zero for t in range(L)], axis=2) for s in range(L)], axis=1)

    def per_tile(v):
        return v.reshape(nt, 1, STATE_TILE)

    return w_st, w_out, toep, per_tile(pr[L]), per_tile(pi[L])


def _to_chunks(u, nb, nc, L):
    return (u.reshape(nb, nc, L, N_CH_TILES, LANES).transpose(1, 0, 3, 2, 4)
            .reshape(nc * nb, N_CH_TILES * L * LANES))


def _from_chunks(y, nb, nc, L):
    return (y.reshape(nc, nb, N_CH_TILES, L, LANES).transpose(1, 0, 3, 2, 4)
            .reshape(nb * nc * L, SSM_WIDTH))


def _s5_state_in_kernel(x_ref, w_ref, d_ref):
    d_ref[...] = _dot(x_ref[...], w_ref[...])


def _s5_state_in(xc, w_st, L, row_tile):
    rows = xc.shape[0]
    row_tile = min(row_tile, rows)
    lk = L * LANES
    return pl.pallas_call(
        _s5_state_in_kernel,
        grid=(N_CH_TILES, rows // row_tile),
        in_specs=[pl.BlockSpec((row_tile, lk), lambda j, r: (r, j)),
                  pl.BlockSpec((None, lk, 2 * STATE_TILE), lambda j, r: (j, 0, 0))],
        out_specs=pl.BlockSpec((row_tile, 2 * STATE_TILE), lambda j, r: (r, j)),
        out_shape=jax.ShapeDtypeStruct((rows, N_CH_TILES * 2 * STATE_TILE), F32),
        compiler_params=_cparams("parallel", "parallel"),
        name="s5_state_in",
    )(xc, w_st)


def _s5_scan_kernel(d_ref, h0_ref, are_ref, aim_ref, hs_ref, fin_ref, hr_ref, hi_ref, *, cb, nb):
    ci = pl.program_id(1)

    @pl.when(ci == 0)
    def _():
        hr_ref[...] = h0_ref[:, 0:STATE_TILE]
        hi_ref[...] = h0_ref[:, STATE_TILE:2 * STATE_TILE]

    ar = jnp.broadcast_to(are_ref[...], (nb, STATE_TILE))
    ai = jnp.broadcast_to(aim_ref[...], (nb, STATE_TILE))

    def body(c, carry):
        hr, hi = carry
        hs_ref[c, :, 0:STATE_TILE] = hr
        hs_ref[c, :, STATE_TILE:2 * STATE_TILE] = hi
        d = d_ref[c]
        return (ar * hr - ai * hi + d[:, 0:STATE_TILE],
                ar * hi + ai * hr + d[:, STATE_TILE:2 * STATE_TILE])

    hr, hi = lax.fori_loop(0, cb, body, (hr_ref[...], hi_ref[...]))
    hr_ref[...] = hr
    hi_ref[...] = hi

    @pl.when(ci == pl.num_programs(1) - 1)
    def _():
        fin_ref[:, 0:STATE_TILE] = hr
        fin_ref[:, STATE_TILE:2 * STATE_TILE] = hi


def _s5_scan(d, h0, a_re, a_im, chunk_block):
    nc, nb, _ = d.shape
    cb = min(chunk_block, nc)
    st2 = 2 * STATE_TILE
    return pl.pallas_call(
        functools.partial(_s5_scan_kernel, cb=cb, nb=nb),
        grid=(N_CH_TILES, nc // cb),
        in_specs=[pl.BlockSpec((cb, nb, st2), lambda j, c: (c, 0, j)),
                  pl.BlockSpec((nb, st2), lambda j, c: (0, j)),
                  pl.BlockSpec((None, 1, STATE_TILE), lambda j, c: (j, 0, 0)),
                  pl.BlockSpec((None, 1, STATE_TILE), lambda j, c: (j, 0, 0))],
        out_specs=[pl.BlockSpec((cb, nb, st2), lambda j, c: (c, 0, j)),
                   pl.BlockSpec((nb, st2), lambda j, c: (0, j))],
        out_shape=[jax.ShapeDtypeStruct((nc, nb, N_CH_TILES * st2), F32),
                   jax.ShapeDtypeStruct((nb, N_CH_TILES * st2), F32)],
        scratch_shapes=[pltpu.VMEM((nb, STATE_TILE), F32), pltpu.VMEM((nb, STATE_TILE), F32)],
        compiler_params=_cparams("parallel", "arbitrary"),
        name="s5_scan",
    )(d, h0, a_re, a_im)


def _s5_out_kernel(x_ref, h_ref, t_ref, wo_ref, y_ref):
    y_ref[...] = _dot(x_ref[...], t_ref[...]) + _dot(h_ref[...].astype(BF16), wo_ref[...])


def _s5_out(xc, hs, toep, w_out, L, row_tile):
    rows = xc.shape[0]
    row_tile = min(row_tile, rows)
    lk = L * LANES
    st2 = 2 * STATE_TILE
    return pl.pallas_call(
        _s5_out_kernel,
        grid=(N_CH_TILES, rows // row_tile),
        in_specs=[pl.BlockSpec((row_tile, lk), lambda j, r: (r, j)),
                  pl.BlockSpec((row_tile, st2), lambda j, r: (r, j)),
                  pl.BlockSpec((None, lk, lk), lambda j, r: (j, 0, 0)),
                  pl.BlockSpec((None, st2, lk), lambda j, r: (j, 0, 0))],
        out_specs=pl.BlockSpec((row_tile, lk), lambda j, r: (r, j)),
        out_shape=jax.ShapeDtypeStruct((rows, N_CH_TILES * lk), F32),
        compiler_params=_cparams("parallel", "parallel"),
        name="s5_out",
    )(xc, hs, toep, w_out)


def _s5(ub, h0, weights, nb, nc, L):
    w_st, w_so, toep, a_re, a_im = weights
    xc = _to_chunks(ub, nb, nc, L)
    d = _s5_state_in(xc, w_st, L, 512)
    hs, fin = _s5_scan(d.reshape(nc, nb, -1), h0, a_re, a_im, 64)
    y = _s5_out(xc, hs.reshape(nc * nb, -1), toep, w_so, L, 512)
    return _from_chunks(y, nb, nc, L), fin


def _state_to_tiles(h_re, h_im):
    nb = h_re.shape[0]
    r = h_re.astype(F32).reshape(nb, N_CH_TILES, STATE_TILE)
    i = h_im.astype(F32).reshape(nb, N_CH_TILES, STATE_TILE)
    return jnp.concatenate([r, i], axis=-1).reshape(nb, N_CH_TILES * 2 * STATE_TILE)


def _tiles_to_state(h):
    nb = h.shape[0]
    h = h.reshape(nb, N_CH_TILES, 2, GROUPS_PER_TILE, SSM_STATE)
    return (h[:, :, 0].reshape(nb, SSM_GROUPS, SSM_STATE), h[:, :, 1].reshape(nb, SSM_GROUPS, SSM_STATE))


def _t5_bucket(dist):
    n = np.maximum(dist, 0)
    max_exact = REL_BUCKETS // 2
    nf = np.maximum(n, 1).astype(np.float32)
    large = max_exact + (np.log(nf / np.float32(max_exact)) / np.float32(math.log(REL_MAX_DIST / max_exact))
                         * np.float32(REL_BUCKETS - max_exact)).astype(np.int32)
    large = np.minimum(large, REL_BUCKETS - 1)
    return np.where(n < max_exact, n, large)


def _rel_bias(rel_table, dist):
    bucket = _t5_bucket(dist)
    tab = rel_table.astype(F32)
    out = jnp.zeros((SWA_HEADS,) + dist.shape, F32)
    for b in range(REL_BUCKETS):
        sel = jnp.asarray(bucket == b)
        if bool((bucket == b).any()):
            out = jnp.where(sel[None], tab[b].reshape((SWA_HEADS,) + (1,) * dist.ndim), out)
    return out


def _softmax_sink(s, sink):
    m = jnp.maximum(jnp.max(s, axis=-1, keepdims=True), sink)
    e = jnp.exp(s - m)
    den = jnp.sum(e, axis=-1, keepdims=True) + jnp.exp(sink - m)
    return e * (1.0 / den)


def _swa_prompt_kernel(sink_ref, q_ref, kp_ref, kc_ref, vp_ref, vc_ref, bias_ref, o_ref):
    blk = pl.program_id(1)
    kk = jnp.concatenate([kp_ref[...], kc_ref[...]], axis=0).astype(BF16)
    vv = jnp.concatenate([vp_ref[...], vc_ref[...]], axis=0).astype(BF16)
    row = lax.broadcasted_iota(jnp.int32, (WINDOW, 2 * WINDOW), 0)
    col = lax.broadcasted_iota(jnp.int32, (WINDOW, 2 * WINDOW), 1)
    dist = row + WINDOW - col
    valid = (dist >= 0) & (dist < WINDOW) & ((col >= WINDOW) | (blk > 0))
    outs = []
    for h in range(SWA_HEADS):
        s = lax.dot_general(q_ref[:, h * LANES:(h + 1) * LANES], kk, _TRANS_B, preferred_element_type=F32)
        s = jnp.where(valid, s + bias_ref[h], NEG_INF)
        p = _softmax_sink(s, sink_ref[h]).astype(BF16)
        outs.append(_dot(p, vv))
    lane = lax.broadcasted_iota(jnp.int32, (WINDOW, LANES), 1)
    for t in range(SWA_REP):
        o_ref[:, t * LANES:(t + 1) * LANES] = jnp.where(lane < SWA_HEAD_DIM, outs[t], outs[t + SWA_REP]).astype(BF16)


def _swa_prompt(qz, k, v, bias, sinks, nb, t):
    nblk = t // WINDOW
    cur = lambda b, i: (b * nblk + i, 0)
    prev = lambda b, i: (b * nblk + jnp.maximum(i - 1, 0), 0)
    return pl.pallas_call(
        _swa_prompt_kernel,
        grid=(nb, nblk),
        in_specs=[pl.BlockSpec(memory_space=pltpu.SMEM),
                  pl.BlockSpec((WINDOW, SWA_HEADS * LANES), cur),
                  pl.BlockSpec((WINDOW, SWA_KV_WIDTH), prev),
                  pl.BlockSpec((WINDOW, SWA_KV_WIDTH), cur),
                  pl.BlockSpec((WINDOW, SWA_KV_WIDTH), prev),
                  pl.BlockSpec((WINDOW, SWA_KV_WIDTH), cur),
                  pl.BlockSpec((SWA_HEADS, WINDOW, 2 * WINDOW), lambda b, i: (0, 0, 0))],
        out_specs=pl.BlockSpec((WINDOW, SWA_WIDTH), cur),
        out_shape=jax.ShapeDtypeStruct((nb * t, SWA_WIDTH), BF16),
        compiler_params=_cparams("parallel", "parallel"),
        name="swa_prompt",
    )(sinks, qz, k, k, v, v, bias)


def _swa_decode_kernel(q_ref, k_ref, v_ref, bias_ref, sink_ref, o_ref, *, seqs, tq):
    rows, keys = q_ref.shape[1], k_ref.shape[1]
    qi = lax.broadcasted_iota(jnp.int32, (rows, keys), 0) % tq
    col = lax.broadcasted_iota(jnp.int32, (rows, keys), 1)
    dist = qi + WINDOW - col
    valid = (dist >= 0) & (dist < WINDOW)
    bias = bias_ref[...]
    sink = sink_ref[...]
    for s_i in range(seqs):
        kk = k_ref[s_i].astype(BF16)
        s = lax.dot_general(q_ref[s_i], kk, _TRANS_B, preferred_element_type=F32)
        s = jnp.where(valid, s + bias, NEG_INF)
        p = _softmax_sink(s, sink).astype(BF16)
        o_ref[s_i] = _dot(p, v_ref[s_i].astype(BF16))


def _swa_decode(qz, k_all, v_all, bias, sink_rows, tq, seqs):
    nseq, rows, _ = qz.shape
    keys = k_all.shape[1]
    seqs = min(seqs, nseq)
    return pl.pallas_call(
        functools.partial(_swa_decode_kernel, seqs=seqs, tq=tq),
        grid=(nseq // seqs,),
        in_specs=[pl.BlockSpec((seqs, rows, LANES), lambda i: (i, 0, 0)),
                  pl.BlockSpec((seqs, keys, LANES), lambda i: (i, 0, 0)),
                  pl.BlockSpec((seqs, keys, LANES), lambda i: (i, 0, 0)),
                  pl.BlockSpec((rows, keys), lambda i: (0, 0)),
                  pl.BlockSpec((rows, 1), lambda i: (0, 0))],
        out_specs=pl.BlockSpec((seqs, rows, LANES), lambda i: (i, 0, 0)),
        out_shape=jax.ShapeDtypeStruct((nseq, rows, LANES), F32),
        compiler_params=_cparams("parallel"),
        name="swa_decode",
    )(qz, k_all, v_all, bias, sink_rows)


def _softmax(s):
    m = jnp.max(s, axis=-1, keepdims=True)
    e = jnp.exp(s - m)
    return e * (1.0 / jnp.sum(e, axis=-1, keepdims=True))


def _mem_prompt_kernel(q_ref, k_ref, v_ref, o_ref):
    scale = MEM_HEAD_DIM ** -0.5
    for h in range(MEM_HEADS):
        sl = slice(h * MEM_HEAD_DIM, (h + 1) * MEM_HEAD_DIM)
        s = lax.dot_general(q_ref[:, sl], k_ref[:, sl].astype(BF16), _TRANS_B, preferred_element_type=F32) * scale
        p = _softmax(s).astype(BF16)
        o_ref[:, sl] = _dot(p, v_ref[:, sl].astype(BF16)).astype(BF16)


def _mem_prompt(qm, mk, mv, nb, t, tile):
    tile = min(tile, t)
    nt = t // tile
    return pl.pallas_call(
        _mem_prompt_kernel,
        grid=(nb, nt),
        in_specs=[pl.BlockSpec((tile, MEM_WIDTH), lambda b, i: (b * nt + i, 0)),
                  pl.BlockSpec((MEM_TOKENS, MEM_WIDTH), lambda b, i: (b, 0)),
                  pl.BlockSpec((MEM_TOKENS, MEM_WIDTH), lambda b, i: (b, 0))],
        out_specs=pl.BlockSpec((tile, MEM_WIDTH), lambda b, i: (b * nt + i, 0)),
        out_shape=jax.ShapeDtypeStruct((nb * t, MEM_WIDTH), BF16),
        compiler_params=_cparams("parallel", "parallel"),
        name="mem_prompt",
    )(qm, mk, mv)


def _mem_decode_kernel(q_ref, k_ref, v_ref, o_ref, *, seqs):
    scale = MEM_HEAD_DIM ** -0.5
    for s_i in range(seqs):
        for h in range(MEM_HEADS):
            sl = slice(h * MEM_HEAD_DIM, (h + 1) * MEM_HEAD_DIM)
            s = lax.dot_general(q_ref[s_i, :, sl].astype(BF16), k_ref[s_i, :, h, :].astype(BF16), _TRANS_B,
                                preferred_element_type=F32) * scale
            p = _softmax(s).astype(BF16)
            o_ref[s_i, :, sl] = _dot(p, v_ref[s_i, :, h, :].astype(BF16))


def _mem_decode(q, k, v, layer, seqs):
    nseq, tq, _ = q.shape
    seqs = min(seqs, nseq)
    cache = pl.BlockSpec((None, seqs, MEM_TOKENS, MEM_HEADS, MEM_HEAD_DIM), lambda i: (layer, i, 0, 0, 0))
    return pl.pallas_call(
        functools.partial(_mem_decode_kernel, seqs=seqs),
        grid=(nseq // seqs,),
        in_specs=[pl.BlockSpec((seqs, tq, MEM_WIDTH), lambda i: (i, 0, 0)), cache, cache],
        out_specs=pl.BlockSpec((seqs, tq, MEM_WIDTH), lambda i: (i, 0, 0)),
        out_shape=jax.ShapeDtypeStruct((nseq, tq, MEM_WIDTH), F32),
        compiler_params=_cparams("parallel"),
        name="mem_decode",
    )(q, k, v)


ROUTER_ROWS = 40
ROUTE_ROWS = 8
HALF = D_MODEL // 2


def _pack_halves(xb):
    hi = pltpu.bitcast(xb[:, 0:HALF].astype(F32), jnp.int32)
    lo = pltpu.bitcast(xb[:, HALF:D_MODEL].astype(F32), jnp.int32)
    return hi | lax.shift_right_logical(lo, jnp.int32(16))


def _unpack_halves(p):
    hi = pltpu.bitcast(p & jnp.int32(-65536), F32).astype(BF16)
    lo = pltpu.bitcast(lax.shift_left(p, jnp.int32(16)), F32).astype(BF16)
    return hi, lo


def _merge_kernel(x_ref, u_ref, y_ref, os_ref, om_ref, g1_ref, wg_ref, dsk_ref, wglu_ref, bglu_ref,
                  wbs_ref, wbw_ref, wbm_ref, wout_ref, g2_ref, wr_ref, br_ref,
                  h_ref, xn2_ref, route_ref):
    x = x_ref[...]
    tt = x.shape[0]
    xb = _rms(x, g1_ref[...]).astype(BF16)
    z = jax.nn.gelu(y_ref[...] + dsk_ref[...] * u_ref[...])
    z = z * jax.nn.sigmoid(_dot(z.astype(BF16), wglu_ref[...]) + bglu_ref[...])
    merged = jax.nn.sigmoid(_dot(xb, wg_ref[:, 0:D_MODEL])) * _dot(z.astype(BF16), wbs_ref[...])
    merged = merged + jax.nn.sigmoid(_dot(xb, wg_ref[:, D_MODEL:2 * D_MODEL])) * _dot(os_ref[...], wbw_ref[...])
    merged = merged + jax.nn.sigmoid(_dot(xb, wg_ref[:, 2 * D_MODEL:3 * D_MODEL])) * _dot(om_ref[...], wbm_ref[...])
    h = x + _dot(merged.astype(BF16), wout_ref[...])
    h_ref[...] = h
    xn2 = _rms(h, g2_ref[...]).astype(BF16)
    xn2_ref[...] = _pack_halves(xn2)

    lt = lax.dot_general(wr_ref[...], xn2, _TRANS_B, preferred_element_type=F32) + br_ref[...]
    gl = lt[N_EXPERTS:N_EXPERTS + N_EXPERT_GROUPS]
    ge = jnp.exp(gl - jnp.max(gl, axis=0, keepdims=True))
    gp = ge / jnp.sum(ge, axis=0, keepdims=True)
    gw = jnp.max(gp, axis=0, keepdims=True)
    gidx = jnp.full((1, tt), N_EXPERT_GROUPS - 1, jnp.int32)
    for r in range(N_EXPERT_GROUPS - 2, -1, -1):
        gidx = jnp.where(gp[r:r + 1] == gw, r, gidx)
    ein = lt[(N_EXPERT_GROUPS - 1) * EXPERTS_PER_GROUP:N_EXPERTS]
    for r in range(N_EXPERT_GROUPS - 2, -1, -1):
        ein = jnp.where(gidx == r, lt[r * EXPERTS_PER_GROUP:(r + 1) * EXPERTS_PER_GROUP], ein)
    ee = jnp.exp(ein - jnp.max(ein, axis=0, keepdims=True))
    ep = ee / jnp.sum(ee, axis=0, keepdims=True)
    rowi = lax.broadcasted_iota(jnp.int32, (EXPERTS_PER_GROUP, tt), 0)
    p1 = jnp.max(ep, axis=0, keepdims=True)
    e1 = jnp.min(jnp.where(ep == p1, rowi, EXPERTS_PER_GROUP), axis=0, keepdims=True)
    ep2 = jnp.where(rowi == e1, -1.0, ep)
    p2 = jnp.max(ep2, axis=0, keepdims=True)
    e2 = jnp.min(jnp.where(ep2 == p2, rowi, EXPERTS_PER_GROUP), axis=0, keepdims=True)
    tot = p1 + p2
    w1 = p1 / tot * gw
    w2 = p2 / tot * gw
    id1 = (gidx * EXPERTS_PER_GROUP + e1).astype(F32)
    id2 = (gidx * EXPERTS_PER_GROUP + e2).astype(F32)
    route_ref[...] = jnp.concatenate([id1, id2, w1, w2, jnp.zeros((ROUTE_ROWS - 4, tt), F32)], axis=0)


def _merge(x, u, y, o_swa, o_mem, p, tile):
    n = x.shape[0]
    tile = min(tile, n)
    row = lambda i: (i, 0)
    const = lambda i: (0, 0)
    full = lambda a: pl.BlockSpec(a.shape, const)
    weights = [p['g1'], p['w_gates'], p['d_skip'], p['w_glu'], p['b_glu'], p['w_br_ssm'], p['w_br_swa'],
               p['w_br_mem'], p['w_out'], p['g2'], p['w_router'], p['b_router']]
    return pl.pallas_call(
        _merge_kernel,
        grid=(n // tile,),
        in_specs=[pl.BlockSpec((tile, D_MODEL), row), pl.BlockSpec((tile, SSM_WIDTH), row),
                  pl.BlockSpec((tile, SSM_WIDTH), row), pl.BlockSpec((tile, SWA_WIDTH), row),
                  pl.BlockSpec((tile, MEM_WIDTH), row)] + [full(w) for w in weights],
        out_specs=[pl.BlockSpec((tile, D_MODEL), row), pl.BlockSpec((tile, HALF), row),
                   pl.BlockSpec((ROUTE_ROWS, tile), lambda i: (0, i))],
        out_shape=[jax.ShapeDtypeStruct((n, D_MODEL), F32), jax.ShapeDtypeStruct((n, HALF), jnp.int32),
                   jax.ShapeDtypeStruct((ROUTE_ROWS, n), F32)],
        compiler_params=_cparams("parallel"),
        name="merge_router",
    )(x, u, y, o_swa, o_mem, *weights)


def _expert_mlp(xp, wgu_ref, wd_ref):
    hi, lo = _unpack_halves(xp)
    hgu = _dot(hi, wgu_ref[0:HALF, :]) + _dot(lo, wgu_ref[HALF:D_MODEL, :])
    hh = jax.nn.silu(hgu[:, 0:D_EXPERT]) * hgu[:, D_EXPERT:2 * D_EXPERT]
    return _dot(hh.astype(BF16), wd_ref[...])


def _moe_kernel(xn2_ref, rt_ref, wgu_ref, wd_ref, h_ref, gf_ref, o_ref, acc_ref):
    e = pl.program_id(1)

    @pl.when(e == 0)
    def _():
        acc_ref[...] = jnp.zeros_like(acc_ref)

    o = _expert_mlp(xn2_ref[...], wgu_ref, wd_ref)
    ef = e.astype(F32)
    c = (jnp.where(rt_ref[:, 0:1] == ef, rt_ref[:, 2:3], 0.0)
         + jnp.where(rt_ref[:, 1:2] == ef, rt_ref[:, 3:4], 0.0))
    acc_ref[...] += c * o

    @pl.when(e == N_EXPERTS - 1)
    def _():
        o_ref[...] = _rms(h_ref[...] + acc_ref[...], gf_ref[...])


def _moe(xn2, route_t, w_gu, w_d, h, gf, tile):
    n = h.shape[0]
    tile = min(tile, n)
    return pl.pallas_call(
        _moe_kernel,
        grid=(n // tile, N_EXPERTS),
        in_specs=[pl.BlockSpec((tile, HALF), lambda i, e: (i, 0)),
                  pl.BlockSpec((tile, ROUTE_ROWS), lambda i, e: (i, 0)),
                  pl.BlockSpec((None, D_MODEL, 2 * D_EXPERT), lambda i, e: (e, 0, 0)),
                  pl.BlockSpec((None, D_EXPERT, D_MODEL), lambda i, e: (e, 0, 0)),
                  pl.BlockSpec((tile, D_MODEL), lambda i, e: (i, 0)),
                  pl.BlockSpec((1, D_MODEL), lambda i, e: (0, 0))],
        out_specs=pl.BlockSpec((tile, D_MODEL), lambda i, e: (i, 0)),
        out_shape=jax.ShapeDtypeStruct((n, D_MODEL), F32),
        scratch_shapes=[pltpu.VMEM((tile, D_MODEL), F32)],
        compiler_params=_cparams("parallel", "arbitrary"),
        name="moe_final_norm",
    )(xn2, route_t, w_gu, w_d, h, gf)


EXPERT_ROW_TILE = 512
SC_CORES = 2
SC_SUBCORES = 16
SC_WORKERS = SC_CORES * SC_SUBCORES
SC_SCATTER_ROWS = 64
SC_GATHER_ROWS = 32


def _route_rank_kernel(r_ref, rank_ref, cnt_ref, base_ref):
    i = pl.program_id(0)
    tt = r_ref.shape[1]

    @pl.when(i == 0)
    def _():
        base_ref[...] = jnp.zeros_like(base_ref)

    ids = r_ref[0:2, :].astype(jnp.int32)
    e_iota = lax.broadcasted_iota(jnp.int32, (N_EXPERTS, tt), 0)
    oh1 = jnp.where(e_iota == ids[0:1], 1.0, 0.0)
    oh2 = jnp.where(e_iota == ids[1:2], 1.0, 0.0)
    before = (lax.broadcasted_iota(jnp.int32, (tt, tt), 0) < lax.broadcasted_iota(jnp.int32, (tt, tt), 1))
    tri = jnp.where(before, 1.0, 0.0).astype(BF16)
    c1 = _dot(oh1.astype(BF16), tri)
    c2 = _dot(oh2.astype(BF16), tri)
    tot1 = jnp.sum(oh1, axis=1, keepdims=True)
    tot2 = jnp.sum(oh2, axis=1, keepdims=True)
    base = base_ref[:, 0:1]
    rank1 = jnp.sum(oh1 * (base + c1), axis=0, keepdims=True)
    rank2 = jnp.sum(oh2 * (base + tot1 + c2), axis=0, keepdims=True)
    rank_ref[...] = jnp.concatenate([rank1, rank2, jnp.zeros((ROUTE_ROWS - 2, tt), F32)], axis=0).astype(jnp.int32)
    new_base = jnp.broadcast_to(base + tot1 + tot2, base_ref.shape)
    base_ref[...] = new_base
    cnt_ref[...] = new_base.astype(jnp.int32)


def _route_rank(route, tile):
    n = route.shape[1]
    tile = min(tile, n)
    return pl.pallas_call(
        _route_rank_kernel,
        grid=(n // tile,),
        in_specs=[pl.BlockSpec((ROUTE_ROWS, tile), lambda i: (0, i))],
        out_specs=[pl.BlockSpec((ROUTE_ROWS, tile), lambda i: (0, i)),
                   pl.BlockSpec((N_EXPERTS, LANES), lambda i: (0, 0))],
        out_shape=[jax.ShapeDtypeStruct((ROUTE_ROWS, n), jnp.int32),
                   jax.ShapeDtypeStruct((N_EXPERTS, LANES), jnp.int32)],
        scratch_shapes=[pltpu.VMEM((N_EXPERTS, LANES), F32)],
        compiler_params=_cparams("arbitrary"),
        name="route_rank",
    )(route)


def _sc_mesh():
    return plsc.VectorSubcoreMesh(core_axis_name="core", subcore_axis_name="subcore")


def _sc_scatter_pairs(x, pos, rows_out):
    n, d = x.shape
    per_w = n // SC_WORKERS
    window = min(SC_SCATTER_ROWS, per_w)

    @pl.kernel(out_type=jax.ShapeDtypeStruct((rows_out, d), x.dtype), mesh=_sc_mesh(),
               scratch_types=[pltpu.VMEM((window,), jnp.int32), pltpu.VMEM((window,), jnp.int32),
                              pltpu.VMEM((window, d), x.dtype), pltpu.SemaphoreType.DMA])
    def scatter(x_hbm, p_hbm, o_hbm, i1_v, i2_v, rows_v, sem):
        wid = lax.axis_index("subcore") * SC_CORES + lax.axis_index("core")

        @pl.loop(0, per_w // window)
        def _(j):
            base = wid * per_w + j * window
            pltpu.sync_copy(p_hbm.at[pl.ds(base, window)], i1_v)
            pltpu.sync_copy(p_hbm.at[pl.ds(n + base, window)], i2_v)
            pltpu.sync_copy(x_hbm.at[pl.ds(base, window)], rows_v)
            pltpu.async_copy(rows_v, o_hbm.at[i1_v], sem).wait()
            pltpu.async_copy(rows_v, o_hbm.at[i2_v], sem).wait()

    return scatter(x, pos)


def _sc_gather_rows(table, idx):
    m = idx.shape[0]
    d = table.shape[1]
    per_w = m // SC_WORKERS
    window = min(SC_GATHER_ROWS, per_w)

    @pl.kernel(out_type=jax.ShapeDtypeStruct((m, d), table.dtype), mesh=_sc_mesh(),
               scratch_types=[pltpu.VMEM((window,), jnp.int32), pltpu.VMEM((window, d), table.dtype),
                              pltpu.SemaphoreType.DMA])
    def gather(t_hbm, i_hbm, o_hbm, i_v, rows_v, sem):
        wid = lax.axis_index("subcore") * SC_CORES + lax.axis_index("core")

        @pl.loop(0, per_w // window)
        def _(j):
            base = wid * per_w + j * window
            pltpu.sync_copy(i_hbm.at[pl.ds(base, window)], i_v)
            pltpu.async_copy(t_hbm.at[i_v], rows_v, sem).wait()
            pltpu.sync_copy(rows_v, o_hbm.at[pl.ds(base, window)])

    return gather(table, idx)


def _expert_tiles_kernel(te_ref, nu_ref, x_ref, wgu_ref, wd_ref, o_ref):
    @pl.when(pl.program_id(0) < nu_ref[0])
    def _():
        o_ref[...] = _expert_mlp(x_ref[...], wgu_ref, wd_ref)


def _expert_tiles(tile_expert, n_used, xs, w_gu, w_d):
    rows = xs.shape[0]
    tm = EXPERT_ROW_TILE
    grid_spec = pltpu.PrefetchScalarGridSpec(
        num_scalar_prefetch=2,
        grid=(rows // tm,),
        in_specs=[pl.BlockSpec((tm, HALF), lambda i, te, nu: (i, 0)),
                  pl.BlockSpec((None, D_MODEL, 2 * D_EXPERT), lambda i, te, nu: (te[i], 0, 0)),
                  pl.BlockSpec((None, D_EXPERT, D_MODEL), lambda i, te, nu: (te[i], 0, 0))],
        out_specs=pl.BlockSpec((tm, D_MODEL), lambda i, te, nu: (i, 0)),
    )
    return pl.pallas_call(
        _expert_tiles_kernel,
        grid_spec=grid_spec,
        out_shape=jax.ShapeDtypeStruct((rows, D_MODEL), F32),
        compiler_params=_cparams("arbitrary"),
        name="expert_tiles",
    )(tile_expert, n_used, xs, w_gu, w_d)


def _combine_kernel(h_ref, o1_ref, o2_ref, rt_ref, gf_ref, y_ref):
    moe = rt_ref[:, 2:3] * o1_ref[...] + rt_ref[:, 3:4] * o2_ref[...]
    y_ref[...] = _rms(h_ref[...] + moe, gf_ref[...])


def _combine(h, o12, route_t, gf, tile):
    n = h.shape[0]
    tile = min(tile, n)
    nt = n // tile
    return pl.pallas_call(
        _combine_kernel,
        grid=(nt,),
        in_specs=[pl.BlockSpec((tile, D_MODEL), lambda i: (i, 0)),
                  pl.BlockSpec((tile, D_MODEL), lambda i: (i, 0)),
                  pl.BlockSpec((tile, D_MODEL), lambda i: (i + nt, 0)),
                  pl.BlockSpec((tile, ROUTE_ROWS), lambda i: (i, 0)),
                  pl.BlockSpec((1, D_MODEL), lambda i: (0, 0))],
        out_specs=pl.BlockSpec((tile, D_MODEL), lambda i: (i, 0)),
        out_shape=jax.ShapeDtypeStruct((n, D_MODEL), F32),
        compiler_params=_cparams("parallel"),
        name="combine_final_norm",
    )(h, o12, o12, route_t, gf)


def _sparse_moe(xn2p, route, h, w_gu, w_d, gf):
    n = h.shape[0]
    tm = EXPERT_ROW_TILE
    rows = 2 * n + N_EXPERTS * tm
    rank, cnt = _route_rank(route, 512)
    counts = cnt[:, 0]
    padded = (counts + tm - 1) // tm * tm
    e_idx = jnp.arange(N_EXPERTS, dtype=jnp.int32)
    starts = jnp.sum(jnp.where(e_idx[None, :] < e_idx[:, None], padded[None, :], 0), axis=1)
    ends = starts + padded
    ids = route[0:2].astype(jnp.int32)
    start_of = jnp.sum(jnp.where(ids[None] == e_idx[:, None, None], starts[:, None, None], 0), axis=0)
    pos = (start_of + rank[0:2]).reshape(2 * n)
    tile_start = jnp.arange(rows // tm, dtype=jnp.int32) * tm
    tile_expert = jnp.minimum(jnp.sum((tile_start[:, None] >= ends[None, :]).astype(jnp.int32), axis=1),
                              N_EXPERTS - 1)
    n_used = (ends[-1:] // tm).astype(jnp.int32)
    xs = _sc_scatter_pairs(xn2p, pos, rows)
    os_ = _expert_tiles(tile_expert, n_used, xs, w_gu, w_d)
    o12 = _sc_gather_rows(os_, pos)
    return _combine(h, o12, route.T, gf, 512)


def _prep_in_weights(w_in):
    o = 0
    w_u = w_in[:, o:o + SSM_WIDTH]; o += SSM_WIDTH
    w_q = w_in[:, o:o + SWA_WIDTH]; o += SWA_WIDTH
    w_k = w_in[:, o:o + SWA_KV_WIDTH]; o += SWA_KV_WIDTH
    w_v = w_in[:, o:o + SWA_KV_WIDTH]; o += SWA_KV_WIDTH
    w_qm = w_in[:, o:o + MEM_WIDTH]; o += MEM_WIDTH
    w_g = w_in[:, o:]
    zeros = jnp.zeros((D_MODEL, SWA_HEAD_DIM), w_in.dtype)
    tiles = []
    for h in range(SWA_HEADS):
        qh = w_q[:, h * SWA_HEAD_DIM:(h + 1) * SWA_HEAD_DIM] * (SWA_HEAD_DIM ** -0.5)
        tiles.extend([qh, zeros] if h // SWA_REP == 0 else [zeros, qh])
    w_main = jnp.concatenate([w_u] + tiles + [w_k, w_v, w_qm], axis=1).astype(BF16)
    return w_main, w_g.astype(BF16)


IN_SPLITS = (SSM_WIDTH, SWA_HEADS * LANES, SWA_KV_WIDTH, SWA_KV_WIDTH, MEM_WIDTH)
IN_DTYPES = ((F32, BF16), (BF16,), (F32,), (F32,), (BF16,))


def kernel(x_prompt, x_sample, cache_swa_k, cache_swa_v, state_ssm_re, state_ssm_im, cache_mem_k, cache_mem_v, mem_prompt, norm1_g, w_in, lam_re, lam_im, log_dt, bm_re, bm_im, cm_re, cm_im, d_skip, w_glu, b_glu, sinks, rel_table, mem_norm_g, w_mem_kv, w_br_ssm, w_br_swa, w_br_mem, w_out, norm2_g, w_rg, b_rg, w_rexp, b_rexp, w_e_gate, w_e_up, w_e_down, final_norm_g):
    nb, t, _ = x_prompt.shape
    ns, ts, _ = x_sample.shape
    l = 0
    L = S5_CHUNK

    w_main, w_gates = _prep_in_weights(w_in[l])
    w_swa = (w_br_swa[l].reshape(SWA_KV_HEADS, SWA_REP, SWA_HEAD_DIM, D_MODEL).transpose(1, 0, 2, 3)
             .reshape(SWA_WIDTH, D_MODEL))
    pad_rows = ROUTER_ROWS - N_EXPERTS - N_EXPERT_GROUPS
    w_router = jnp.concatenate([w_rexp[l].T, w_rg[l].T, jnp.zeros((pad_rows, D_MODEL), F32)], axis=0).astype(BF16)
    b_router = jnp.concatenate([b_rexp[l], b_rg[l], jnp.zeros((pad_rows,), F32)]).reshape(ROUTER_ROWS, 1)
    mp = {
        'g1': norm1_g[l].reshape(1, D_MODEL), 'w_gates': w_gates, 'd_skip': d_skip[l].reshape(1, SSM_WIDTH),
        'w_glu': w_glu[l].astype(BF16), 'b_glu': b_glu[l].reshape(1, SSM_WIDTH),
        'w_br_ssm': w_br_ssm[l].astype(BF16), 'w_br_swa': w_swa.astype(BF16),
        'w_br_mem': w_br_mem[l].astype(BF16), 'w_out': w_out[l].astype(BF16),
        'g2': norm2_g[l].reshape(1, D_MODEL), 'w_router': w_router, 'b_router': b_router,
    }
    w_gu = jnp.concatenate([w_e_gate[l], w_e_up[l]], axis=-1).astype(BF16)
    w_d = w_e_down[l].astype(BF16)
    gf = final_norm_g.reshape(1, D_MODEL)
    s5_w = _s5_weights(lam_re[l], lam_im[l], log_dt[l], bm_re[l], bm_im[l], cm_re[l], cm_im[l], L)

    bias_p = _rel_bias(rel_table, np.arange(WINDOW)[:, None] + WINDOW - np.arange(2 * WINDOW)[None, :])
    keys_s = WINDOW + 2 * ts
    bias_s = _rel_bias(rel_table, np.arange(ts)[:, None] + WINDOW - np.arange(keys_s)[None, :])
    bias_s = bias_s.reshape(SWA_HEADS * ts, keys_s)
    sink_rows = jnp.repeat(sinks[l].astype(F32), ts).reshape(SWA_HEADS * ts, 1)

    n = nb * t
    xp = x_prompt.reshape(n, D_MODEL)
    mk, mv = _norm_proj(mem_prompt.reshape(nb * MEM_TOKENS, D_MODEL), mem_norm_g[l].reshape(1, D_MODEL),
                        w_mem_kv[l].astype(BF16), (MEM_WIDTH, MEM_WIDTH), ((F32,), (F32,)), 512)
    u, ub, qz, k, v, qm = _norm_proj(xp, mp['g1'], w_main, IN_SPLITS, IN_DTYPES, 512)

    y_ssm, fin = _s5(ub, jnp.zeros((nb, N_CH_TILES * 2 * STATE_TILE), F32), s5_w, nb, t // L, L)
    p_re, p_im = _tiles_to_state(fin)

    o_swa = _swa_prompt(qz, k, v, bias_p, sinks[l].astype(F32), nb, t)
    o_mem = _mem_prompt(qm, mk, mv, nb, t, 512)
    h, xn2p, route = _merge(xp, u, y_ssm, o_swa, o_mem, mp, 256)
    y_prompt = _sparse_moe(xn2p, route, h, w_gu, w_d, gf).reshape(nb, t, D_MODEL)

    k4 = k.reshape(nb, t, SWA_KV_HEADS, SWA_HEAD_DIM)
    v4 = v.reshape(nb, t, SWA_KV_HEADS, SWA_HEAD_DIM)
    new_k_p, new_v_p = k4[:, -WINDOW:][None], v4[:, -WINDOW:][None]
    new_mk = mk.reshape(1, nb, MEM_TOKENS, MEM_HEADS, MEM_HEAD_DIM)
    new_mv = mv.reshape(1, nb, MEM_TOKENS, MEM_HEADS, MEM_HEAD_DIM)

    m = ns * ts
    xs = x_sample.reshape(m, D_MODEL)
    us, ubs, qzs, k_s, v_s, qms = _norm_proj(xs, mp['g1'], w_main, IN_SPLITS, IN_DTYPES, 256)
    ys_ssm, fins = _s5(ubs, _state_to_tiles(state_ssm_re[l], state_ssm_im[l]), s5_w, ns, ts // L, L)
    s_re, s_im = _tiles_to_state(fins)

    kk_all = jnp.concatenate([cache_swa_k[l].reshape(ns, WINDOW, SWA_KV_WIDTH).astype(F32),
                              k_s.reshape(ns, ts, SWA_KV_WIDTH)], axis=1)
    vv_all = jnp.concatenate([cache_swa_v[l].reshape(ns, WINDOW, SWA_KV_WIDTH).astype(F32),
                              v_s.reshape(ns, ts, SWA_KV_WIDTH)], axis=1)
    pad = jnp.zeros((ns, keys_s - WINDOW - ts, SWA_KV_WIDTH), F32)
    q_rows = qzs.reshape(ns, ts, SWA_HEADS, LANES).transpose(0, 2, 1, 3).reshape(ns, SWA_HEADS * ts, LANES)
    o_dec = _swa_decode(q_rows, jnp.concatenate([kk_all, pad], axis=1), jnp.concatenate([vv_all, pad], axis=1),
                        bias_s, sink_rows, ts, 8)
    o_dec = o_dec.reshape(ns, SWA_KV_HEADS, SWA_REP, ts, SWA_KV_HEADS, SWA_HEAD_DIM)
    o_dec = jnp.stack([o_dec[:, g, :, :, g] for g in range(SWA_KV_HEADS)], axis=1)
    o_swa_s = o_dec.transpose(0, 3, 2, 1, 4).reshape(m, SWA_WIDTH).astype(BF16)

    o_mem_s = _mem_decode(qms.astype(F32).reshape(ns, ts, MEM_WIDTH), cache_mem_k, cache_mem_v, l, 4)
    o_mem_s = o_mem_s.reshape(m, MEM_WIDTH).astype(BF16)

    hs_, xn2ps, routes = _merge(xs, us, ys_ssm, o_swa_s, o_mem_s, mp, 256)
    y_sample = _moe(xn2ps, routes.T, w_gu, w_d, hs_, gf, 1024).reshape(ns, ts, D_MODEL)

    new_k_s = kk_all[:, -WINDOW:].reshape(1, ns, WINDOW, SWA_KV_HEADS, SWA_HEAD_DIM).astype(cache_swa_k.dtype)
    new_v_s = vv_all[:, -WINDOW:].reshape(1, ns, WINDOW, SWA_KV_HEADS, SWA_HEAD_DIM).astype(cache_swa_v.dtype)

    return (y_prompt, y_sample,
            new_k_p, new_v_p, p_re[None], p_im[None], new_mk, new_mv,
            new_k_s, new_v_s, s_re[None].astype(state_ssm_re.dtype), s_im[None].astype(state_ssm_im.dtype))
```

```python
import functools
import math

import numpy as np
import jax
import jax.numpy as jnp
from jax import lax
from jax.experimental import pallas as pl
from jax.experimental.pallas import tpu as pltpu
from jax.experimental.pallas import tpu_sc as plsc

F32 = jnp.float32
BF16 = jnp.bfloat16

D_MODEL = 1024
SSM_WIDTH = 512
SSM_GROUP = 16
SSM_GROUPS = 32
SSM_STATE = 64
SWA_HEADS = 8
SWA_KV_HEADS = 2
SWA_REP = 4
SWA_HEAD_DIM = 64
SWA_WIDTH = 512
SWA_KV_WIDTH = 128
WINDOW = 128
REL_BUCKETS = 32
REL_MAX_DIST = 128
MEM_TOKENS = 256
MEM_HEADS = 4
MEM_HEAD_DIM = 128
MEM_WIDTH = 512
N_EXPERT_GROUPS = 4
EXPERTS_PER_GROUP = 8
N_EXPERTS = 32
D_EXPERT = 256
EPS = 1e-6
NEG_INF = -1e30

LANES = 128
GROUPS_PER_TILE = LANES // SSM_GROUP
N_CH_TILES = SSM_WIDTH // LANES
STATE_TILE = GROUPS_PER_TILE * SSM_STATE
VMEM_LIMIT = 56 * 1024 * 1024
S5_CHUNK = 8

_TRANS_B = (((1,), (1,)), ((), ()))


def _cparams(*sem):
    return pltpu.CompilerParams(dimension_semantics=sem, vmem_limit_bytes=VMEM_LIMIT)


def _rms(x, g):
    return (x * lax.rsqrt(jnp.mean(x * x, axis=-1, keepdims=True) + EPS)) * g


def _dot(a, b):
    return jnp.dot(a, b, preferred_element_type=F32)


def _norm_proj_kernel(x_ref, g_ref, w_ref, *out_refs, splits, dtypes):
    xb = _rms(x_ref[...], g_ref[...]).astype(BF16)
    off = 0
    outs = iter(out_refs)
    for width, dts in zip(splits, dtypes):
        r = _dot(xb, w_ref[:, off:off + width])
        for dt in dts:
            next(outs)[...] = r.astype(dt)
        off += width


def _norm_proj(x, g, w, splits, dtypes, tile):
    n, d = x.shape
    tile = min(tile, n)
    flat = [(wd, dt) for wd, dts in zip(splits, dtypes) for dt in dts]
    return pl.pallas_call(
        functools.partial(_norm_proj_kernel, splits=tuple(splits), dtypes=tuple(dtypes)),
        grid=(n // tile,),
        in_specs=[pl.BlockSpec((tile, d), lambda i: (i, 0)),
                  pl.BlockSpec((1, d), lambda i: (0, 0)),
                  pl.BlockSpec((d, sum(splits)), lambda i: (0, 0))],
        out_specs=[pl.BlockSpec((tile, wd), lambda i: (i, 0)) for wd, _ in flat],
        out_shape=[jax.ShapeDtypeStruct((n, wd), dt) for wd, dt in flat],
        compiler_params=_cparams("parallel"),
        name="norm_proj",
    )(x, g, w)


def _group_mask(rows_per_group, cols_per_group):
    r = np.arange(GROUPS_PER_TILE * rows_per_group)[:, None] // rows_per_group
    c = np.arange(GROUPS_PER_TILE * cols_per_group)[None, :] // cols_per_group
    return jnp.asarray(r == c, F32)


def _s5_weights(lam_re, lam_im, log_dt, bm_re, bm_im, cm_re, cm_im, L):
    hp = lax.Precision.HIGHEST
    nt, gt, P, H = N_CH_TILES, GROUPS_PER_TILE, SSM_STATE, SSM_GROUP
    lr, li = lam_re.astype(F32), lam_im.astype(F32)
    dt = jnp.exp(log_dt.astype(F32))[:, None]
    mag = jnp.exp(lr * dt)
    a_re = mag * jnp.cos(li * dt)
    a_im = mag * jnp.sin(li * dt)
    den = lr * lr + li * li
    f_re = ((a_re - 1.0) * lr + a_im * li) / den
    f_im = (a_im * lr - (a_re - 1.0) * li) / den
    br, bi = bm_re.astype(F32), bm_im.astype(F32)
    bb_re = f_re[..., None] * br - f_im[..., None] * bi
    bb_im = f_re[..., None] * bi + f_im[..., None] * br
    pr, pi = [jnp.ones_like(a_re)], [jnp.zeros_like(a_im)]
    for _ in range(L):
        pr.append(pr[-1] * a_re - pi[-1] * a_im)
        pi.append(pr[-2] * a_im + pi[-1] * a_re)
    cr, ci = cm_re.astype(F32), cm_im.astype(F32)
    ca_re = [cr * pr[k][:, None, :] - ci * pi[k][:, None, :] for k in range(L + 1)]
    ca_im = [cr * pi[k][:, None, :] + ci * pr[k][:, None, :] for k in range(L + 1)]

    def expand(w, rows_per_group, cols_per_group):
        w = w.reshape(nt, gt * rows_per_group, cols_per_group)
        return jnp.tile(w, (1, 1, gt)) * _group_mask(rows_per_group, cols_per_group)

    st_re, st_im = [], []
    for s in range(L):
        k = L - 1 - s
        w_re = pr[k][..., None] * bb_re - pi[k][..., None] * bb_im
        w_im = pr[k][..., None] * bb_im + pi[k][..., None] * bb_re
        st_re.append(expand(w_re.transpose(0, 2, 1), H, P))
        st_im.append(expand(w_im.transpose(0, 2, 1), H, P))
    w_st = jnp.concatenate([jnp.concatenate(st_re, axis=1), jnp.concatenate(st_im, axis=1)], axis=2).astype(BF16)

    so_re = jnp.concatenate([expand(ca_re[t + 1].transpose(0, 2, 1), P, H) for t in range(L)], axis=2)
    so_im = jnp.concatenate([expand(-ca_im[t + 1].transpose(0, 2, 1), P, H) for t in range(L)], axis=2)
    w_out = jnp.concatenate([so_re, so_im], axis=1).astype(BF16)

    blocks = []
    for tau in range(L):
        k_lag = (jnp.einsum('ghp,gpk->gkh', ca_re[tau], bb_re, precision=hp)
                 - jnp.einsum('ghp,gpk->gkh', ca_im[tau], bb_im, precision=hp))
        blocks.append(expand(k_lag, H, H).astype(BF16))
    zero = jnp.zeros_like(blocks[0])
    toep = jnp.concatenate(
        [jnp.concatenate([blocks[t - s] if t >= s else zero for t in range(L)], axis=2) for s in range(L)], axis=1)

    def per_tile(v):
        return v.reshape(nt, 1, STATE_TILE)

    return w_st, w_out, toep, per_tile(pr[L]), per_tile(pi[L])


def _to_chunks(u, nb, nc, L):
    return (u.reshape(nb, nc, L, N_CH_TILES, LANES).transpose(1, 0, 3, 2, 4)
            .reshape(nc * nb, N_CH_TILES * L * LANES))


def _from_chunks(y, nb, nc, L):
    return (y.reshape(nc, nb, N_CH_TILES, L, LANES).transpose(1, 0, 3, 2, 4)
            .reshape(nb * nc * L, SSM_WIDTH))


def _s5_state_in_kernel(x_ref, w_ref, d_ref):
    d_ref[...] = _dot(x_ref[...], w_ref[...])


def _s5_state_in(xc, w_st, L, row_tile):
    rows = xc.shape[0]
    row_tile = min(row_tile, rows)
    lk = L * LANES
    return pl.pallas_call(
        _s5_state_in_kernel,
        grid=(N_CH_TILES, rows // row_tile),
        in_specs=[pl.BlockSpec((row_tile, lk), lambda j, r: (r, j)),
                  pl.BlockSpec((None, lk, 2 * STATE_TILE), lambda j, r: (j, 0, 0))],
        out_specs=pl.BlockSpec((row_tile, 2 * STATE_TILE), lambda j, r: (r, j)),
        out_shape=jax.ShapeDtypeStruct((rows, N_CH_TILES * 2 * STATE_TILE), F32),
        compiler_params=_cparams("parallel", "parallel"),
        name="s5_state_in",
    )(xc, w_st)


def _s5_scan_kernel(d_ref, h0_ref, are_ref, aim_ref, hs_ref, fin_ref, hr_ref, hi_ref, *, cb, nb):
    ci = pl.program_id(1)

    @pl.when(ci == 0)
    def _():
        hr_ref[...] = h0_ref[:, 0:STATE_TILE]
        hi_ref[...] = h0_ref[:, STATE_TILE:2 * STATE_TILE]

    ar = jnp.broadcast_to(are_ref[...], (nb, STATE_TILE))
    ai = jnp.broadcast_to(aim_ref[...], (nb, STATE_TILE))

    def body(c, carry):
        hr, hi = carry
        hs_ref[c, :, 0:STATE_TILE] = hr
        hs_ref[c, :, STATE_TILE:2 * STATE_TILE] = hi
        d = d_ref[c]
        return (ar * hr - ai * hi + d[:, 0:STATE_TILE],
                ar * hi + ai * hr + d[:, STATE_TILE:2 * STATE_TILE])

    hr, hi = lax.fori_loop(0, cb, body, (hr_ref[...], hi_ref[...]))
    hr_ref[...] = hr
    hi_ref[...] = hi

    @pl.when(ci == pl.num_programs(1) - 1)
    def _():
        fin_ref[:, 0:STATE_TILE] = hr
        fin_ref[:, STATE_TILE:2 * STATE_TILE] = hi


def _s5_scan(d, h0, a_re, a_im, chunk_block):
    nc, nb, _ = d.shape
    cb = min(chunk_block, nc)
    st2 = 2 * STATE_TILE
    return pl.pallas_call(
        functools.partial(_s5_scan_kernel, cb=cb, nb=nb),
        grid=(N_CH_TILES, nc // cb),
        in_specs=[pl.BlockSpec((cb, nb, st2), lambda j, c: (c, 0, j)),
                  pl.BlockSpec((nb, st2), lambda j, c: (0, j)),
                  pl.BlockSpec((None, 1, STATE_TILE), lambda j, c: (j, 0, 0)),
                  pl.BlockSpec((None, 1, STATE_TILE), lambda j, c: (j, 0, 0))],
        out_specs=[pl.BlockSpec((cb, nb, st2), lambda j, c: (c, 0, j)),
                   pl.BlockSpec((nb, st2), lambda j, c: (0, j))],
        out_shape=[jax.ShapeDtypeStruct((nc, nb, N_CH_TILES * st2), F32),
                   jax.ShapeDtypeStruct((nb, N_CH_TILES * st2), F32)],
        scratch_shapes=[pltpu.VMEM((nb, STATE_TILE), F32), pltpu.VMEM((nb, STATE_TILE), F32)],
        compiler_params=_cparams("parallel", "arbitrary"),
        name="s5_scan",
    )(d, h0, a_re, a_im)


def _s5_out_kernel(x_ref, h_ref, t_ref, wo_ref, y_ref):
    y_ref[...] = _dot(x_ref[...], t_ref[...]) + _dot(h_ref[...].astype(BF16), wo_ref[...])


def _s5_out(xc, hs, toep, w_out, L, row_tile):
    rows = xc.shape[0]
    row_tile = min(row_tile, rows)
    lk = L * LANES
    st2 = 2 * STATE_TILE
    return pl.pallas_call(
        _s5_out_kernel,
        grid=(N_CH_TILES, rows // row_tile),
        in_specs=[pl.BlockSpec((row_tile, lk), lambda j, r: (r, j)),
                  pl.BlockSpec((row_tile, st2), lambda j, r: (r, j)),
                  pl.BlockSpec((None, lk, lk), lambda j, r: (j, 0, 0)),
                  pl.BlockSpec((None, st2, lk), lambda j, r: (j, 0, 0))],
        out_specs=pl.BlockSpec((row_tile, lk), lambda j, r: (r, j)),
        out_shape=jax.ShapeDtypeStruct((rows, N_CH_TILES * lk), F32),
        compiler_params=_cparams("parallel", "parallel"),
        name="s5_out",
    )(xc, hs, toep, w_out)


def _s5(ub, h0, weights, nb, nc, L):
    w_st, w_so, toep, a_re, a_im = weights
    xc = _to_chunks(ub, nb, nc, L)
    d = _s5_state_in(xc, w_st, L, 512)
    hs, fin = _s5_scan(d.reshape(nc, nb, -1), h0, a_re, a_im, 64)
    y = _s5_out(xc, hs.reshape(nc * nb, -1), toep, w_so, L, 512)
    return _from_chunks(y, nb, nc, L), fin


def _state_to_tiles(h_re, h_im):
    nb = h_re.shape[0]
    r = h_re.astype(F32).reshape(nb, N_CH_TILES, STATE_TILE)
    i = h_im.astype(F32).reshape(nb, N_CH_TILES, STATE_TILE)
    return jnp.concatenate([r, i], axis=-1).reshape(nb, N_CH_TILES * 2 * STATE_TILE)


def _tiles_to_state(h):
    nb = h.shape[0]
    h = h.reshape(nb, N_CH_TILES, 2, GROUPS_PER_TILE, SSM_STATE)
    return (h[:, :, 0].reshape(nb, SSM_GROUPS, SSM_STATE), h[:, :, 1].reshape(nb, SSM_GROUPS, SSM_STATE))


def _t5_bucket(dist):
    n = np.maximum(dist, 0)
    max_exact = REL_BUCKETS // 2
    nf = np.maximum(n, 1).astype(np.float32)
    large = max_exact + (np.log(nf / np.float32(max_exact)) / np.float32(math.log(REL_MAX_DIST / max_exact))
                         * np.float32(REL_BUCKETS - max_exact)).astype(np.int32)
    large = np.minimum(large, REL_BUCKETS - 1)
    return np.where(n < max_exact, n, large)


def _rel_bias(rel_table, dist):
    bucket = _t5_bucket(dist)
    tab = rel_table.astype(F32)
    out = jnp.zeros((SWA_HEADS,) + dist.shape, F32)
    for b in range(REL_BUCKETS):
        sel = jnp.asarray(bucket == b)
        if bool((bucket == b).any()):
            out = jnp.where(sel[None], tab[b].reshape((SWA_HEADS,) + (1,) * dist.ndim), out)
    return out


def _softmax_sink(s, sink):
    m = jnp.maximum(jnp.max(s, axis=-1, keepdims=True), sink)
    e = jnp.exp(s - m)
    den = jnp.sum(e, axis=-1, keepdims=True) + jnp.exp(sink - m)
    return e * (1.0 / den)


def _swa_prompt_kernel(sink_ref, q_ref, kp_ref, kc_ref, vp_ref, vc_ref, bias_ref, o_ref, kk_ref, vv_ref, *, qblocks):
    step = pl.program_id(1)
    kk_ref[0:WINDOW, :] = kp_ref[...].astype(BF16)
    kk_ref[WINDOW:, :] = kc_ref[...].astype(BF16)
    vv_ref[0:WINDOW, :] = vp_ref[...].astype(BF16)
    vv_ref[WINDOW:, :] = vc_ref[...].astype(BF16)
    row = lax.broadcasted_iota(jnp.int32, (WINDOW, 2 * WINDOW), 0)
    col = lax.broadcasted_iota(jnp.int32, (WINDOW, 2 * WINDOW), 1)
    dist = row + WINDOW - col
    band = (dist >= 0) & (dist < WINDOW)
    lane = lax.broadcasted_iota(jnp.int32, (WINDOW, LANES), 1)
    low = lane < SWA_HEAD_DIM

    def block(j, carry):
        r0 = pl.multiple_of(j * WINDOW, WINDOW)
        kk = kk_ref[pl.ds(r0, 2 * WINDOW), :]
        vv = vv_ref[pl.ds(r0, 2 * WINDOW), :]
        valid = band & ((col >= WINDOW) | (step * qblocks + j > 0))
        for t in range(SWA_REP):
            q2 = q_ref[pl.ds(r0, WINDOW), t * LANES:(t + 1) * LANES]
            outs = []
            for half in range(SWA_KV_HEADS):
                h = t + SWA_REP * half
                qh = jnp.where(low if half == 0 else jnp.logical_not(low), q2, jnp.zeros_like(q2))
                s = lax.dot_general(qh, kk, _TRANS_B, preferred_element_type=F32)
                s = jnp.where(valid, s + bias_ref[h], NEG_INF)
                p = _softmax_sink(s, sink_ref[h]).astype(BF16)
                outs.append(_dot(p, vv))
            o_ref[pl.ds(r0, WINDOW), t * LANES:(t + 1) * LANES] = jnp.where(low, outs[0], outs[1]).astype(BF16)
        return carry

    lax.fori_loop(0, qblocks, block, 0)


def _swa_prompt(q, k, v, bias, sinks, nb, t, qblocks):
    nstep = t // (WINDOW * qblocks)
    rows = WINDOW * qblocks
    cur = lambda b, i: (b * nstep + i, 0)
    prev = lambda b, i: (b * nstep * qblocks + jnp.maximum(i * qblocks - 1, 0), 0)
    return pl.pallas_call(
        functools.partial(_swa_prompt_kernel, qblocks=qblocks),
        grid=(nb, nstep),
        in_specs=[pl.BlockSpec(memory_space=pltpu.SMEM),
                  pl.BlockSpec((rows, SWA_WIDTH), cur),
                  pl.BlockSpec((WINDOW, SWA_KV_WIDTH), prev),
                  pl.BlockSpec((rows, SWA_KV_WIDTH), cur),
                  pl.BlockSpec((WINDOW, SWA_KV_WIDTH), prev),
                  pl.BlockSpec((rows, SWA_KV_WIDTH), cur),
                  pl.BlockSpec((SWA_HEADS, WINDOW, 2 * WINDOW), lambda b, i: (0, 0, 0))],
        out_specs=pl.BlockSpec((rows, SWA_WIDTH), cur),
        out_shape=jax.ShapeDtypeStruct((nb * t, SWA_WIDTH), BF16),
        scratch_shapes=[pltpu.VMEM((rows + WINDOW, SWA_KV_WIDTH), BF16),
                        pltpu.VMEM((rows + WINDOW, SWA_KV_WIDTH), BF16)],
        compiler_params=_cparams("parallel", "parallel"),
        name="swa_prompt",
    )(sinks, q, k, k, v, v, bias)


def _swa_decode_kernel(q_ref, k_ref, v_ref, bias_ref, sink_ref, o_ref, *, seqs, tq):
    rows, keys = q_ref.shape[1], k_ref.shape[1]
    qi = lax.broadcasted_iota(jnp.int32, (rows, keys), 0) % tq
    col = lax.broadcasted_iota(jnp.int32, (rows, keys), 1)
    dist = qi + WINDOW - col
    valid = (dist >= 0) & (dist < WINDOW)
    bias = bias_ref[...]
    sink = sink_ref[...]
    for s_i in range(seqs):
        kk = k_ref[s_i].astype(BF16)
        s = lax.dot_general(q_ref[s_i], kk, _TRANS_B, preferred_element_type=F32)
        s = jnp.where(valid, s + bias, NEG_INF)
        p = _softmax_sink(s, sink).astype(BF16)
        o_ref[s_i] = _dot(p, v_ref[s_i].astype(BF16))


def _swa_decode(qz, k_all, v_all, bias, sink_rows, tq, seqs):
    nseq, rows, _ = qz.shape
    keys = k_all.shape[1]
    seqs = min(seqs, nseq)
    return pl.pallas_call(
        functools.partial(_swa_decode_kernel, seqs=seqs, tq=tq),
        grid=(nseq // seqs,),
        in_specs=[pl.BlockSpec((seqs, rows, LANES), lambda i: (i, 0, 0)),
                  pl.BlockSpec((seqs, keys, LANES), lambda i: (i, 0, 0)),
                  pl.BlockSpec((seqs, keys, LANES), lambda i: (i, 0, 0)),
                  pl.BlockSpec((rows, keys), lambda i: (0, 0)),
                  pl.BlockSpec((rows, 1), lambda i: (0, 0))],
        out_specs=pl.BlockSpec((seqs, rows, LANES), lambda i: (i, 0, 0)),
        out_shape=jax.ShapeDtypeStruct((nseq, rows, LANES), F32),
        compiler_params=_cparams("parallel"),
        name="swa_decode",
    )(qz, k_all, v_all, bias, sink_rows)


def _softmax(s):
    m = jnp.max(s, axis=-1, keepdims=True)
    e = jnp.exp(s - m)
    return e * (1.0 / jnp.sum(e, axis=-1, keepdims=True))


def _mem_prompt_kernel(q_ref, k_ref, v_ref, o_ref):
    scale = MEM_HEAD_DIM ** -0.5
    for h in range(MEM_HEADS):
        sl = slice(h * MEM_HEAD_DIM, (h + 1) * MEM_HEAD_DIM)
        s = lax.dot_general(q_ref[:, sl], k_ref[:, sl].astype(BF16), _TRANS_B, preferred_element_type=F32) * scale
        p = _softmax(s).astype(BF16)
        o_ref[:, sl] = _dot(p, v_ref[:, sl].astype(BF16)).astype(BF16)


def _mem_prompt(qm, mk, mv, nb, t, tile):
    tile = min(tile, t)
    nt = t // tile
    return pl.pallas_call(
        _mem_prompt_kernel,
        grid=(nb, nt),
        in_specs=[pl.BlockSpec((tile, MEM_WIDTH), lambda b, i: (b * nt + i, 0)),
                  pl.BlockSpec((MEM_TOKENS, MEM_WIDTH), lambda b, i: (b, 0)),
                  pl.BlockSpec((MEM_TOKENS, MEM_WIDTH), lambda b, i: (b, 0))],
        out_specs=pl.BlockSpec((tile, MEM_WIDTH), lambda b, i: (b * nt + i, 0)),
        out_shape=jax.ShapeDtypeStruct((nb * t, MEM_WIDTH), BF16),
        compiler_params=_cparams("parallel", "parallel"),
        name="mem_prompt",
    )(qm, mk, mv)


def _mem_decode_kernel(q_ref, k_ref, v_ref, o_ref, *, seqs):
    scale = MEM_HEAD_DIM ** -0.5
    for s_i in range(seqs):
        for h in range(MEM_HEADS):
            sl = slice(h * MEM_HEAD_DIM, (h + 1) * MEM_HEAD_DIM)
            s = lax.dot_general(q_ref[s_i, :, sl].astype(BF16), k_ref[s_i, :, h, :].astype(BF16), _TRANS_B,
                                preferred_element_type=F32) * scale
            p = _softmax(s).astype(BF16)
            o_ref[s_i, :, sl] = _dot(p, v_ref[s_i, :, h, :].astype(BF16))


def _mem_decode(q, k, v, layer, seqs):
    nseq, tq, _ = q.shape
    seqs = min(seqs, nseq)
    cache = pl.BlockSpec((None, seqs, MEM_TOKENS, MEM_HEADS, MEM_HEAD_DIM), lambda i: (layer, i, 0, 0, 0))
    return pl.pallas_call(
        functools.partial(_mem_decode_kernel, seqs=seqs),
        grid=(nseq // seqs,),
        in_specs=[pl.BlockSpec((seqs, tq, MEM_WIDTH), lambda i: (i, 0, 0)), cache, cache],
        out_specs=pl.BlockSpec((seqs, tq, MEM_WIDTH), lambda i: (i, 0, 0)),
        out_shape=jax.ShapeDtypeStruct((nseq, tq, MEM_WIDTH), F32),
        compiler_params=_cparams("parallel"),
        name="mem_decode",
    )(q, k, v)


ROUTER_ROWS = 40
ROUTE_ROWS = 8
HALF = D_MODEL // 2


def _pack_halves(xb):
    hi = pltpu.bitcast(xb[:, 0:HALF].astype(F32), jnp.int32)
    lo = pltpu.bitcast(xb[:, HALF:D_MODEL].astype(F32), jnp.int32)
    return hi | lax.shift_right_logical(lo, jnp.int32(16))


def _unpack_halves(p):
    hi = pltpu.bitcast(p & jnp.int32(-65536), F32).astype(BF16)
    lo = pltpu.bitcast(lax.shift_left(p, jnp.int32(16)), F32).astype(BF16)
    return hi, lo


def _merge_kernel(x_ref, u_ref, y_ref, os_ref, om_ref, g1_ref, wg_ref, dsk_ref, wglu_ref, bglu_ref,
                  wbs_ref, wbw_ref, wbm_ref, wout_ref, g2_ref, wr_ref, br_ref,
                  h_ref, xn2_ref, route_ref):
    x = x_ref[...]
    tt = x.shape[0]
    xb = _rms(x, g1_ref[...]).astype(BF16)
    z = jax.nn.gelu(y_ref[...] + dsk_ref[...] * u_ref[...])
    z = z * jax.nn.sigmoid(_dot(z.astype(BF16), wglu_ref[...]) + bglu_ref[...])
    merged = jax.nn.sigmoid(_dot(xb, wg_ref[:, 0:D_MODEL])) * _dot(z.astype(BF16), wbs_ref[...])
    merged = merged + jax.nn.sigmoid(_dot(xb, wg_ref[:, D_MODEL:2 * D_MODEL])) * _dot(os_ref[...], wbw_ref[...])
    merged = merged + jax.nn.sigmoid(_dot(xb, wg_ref[:, 2 * D_MODEL:3 * D_MODEL])) * _dot(om_ref[...], wbm_ref[...])
    h = x + _dot(merged.astype(BF16), wout_ref[...])
    h_ref[...] = h
    xn2 = _rms(h, g2_ref[...]).astype(BF16)
    xn2_ref[...] = _pack_halves(xn2)

    lt = lax.dot_general(wr_ref[...], xn2, _TRANS_B, preferred_element_type=F32) + br_ref[...]
    gl = lt[N_EXPERTS:N_EXPERTS + N_EXPERT_GROUPS]
    ge = jnp.exp(gl - jnp.max(gl, axis=0, keepdims=True))
    gp = ge / jnp.sum(ge, axis=0, keepdims=True)
    gw = jnp.max(gp, axis=0, keepdims=True)
    gidx = jnp.full((1, tt), N_EXPERT_GROUPS - 1, jnp.int32)
    for r in range(N_EXPERT_GROUPS - 2, -1, -1):
        gidx = jnp.where(gp[r:r + 1] == gw, r, gidx)
    ein = lt[(N_EXPERT_GROUPS - 1) * EXPERTS_PER_GROUP:N_EXPERTS]
    for r in range(N_EXPERT_GROUPS - 2, -1, -1):
        ein = jnp.where(gidx == r, lt[r * EXPERTS_PER_GROUP:(r + 1) * EXPERTS_PER_GROUP], ein)
    ee = jnp.exp(ein - jnp.max(ein, axis=0, keepdims=True))
    ep = ee / jnp.sum(ee, axis=0, keepdims=True)
    rowi = lax.broadcasted_iota(jnp.int32, (EXPERTS_PER_GROUP, tt), 0)
    p1 = jnp.max(ep, axis=0, keepdims=True)
    e1 = jnp.min(jnp.where(ep == p1, rowi, EXPERTS_PER_GROUP), axis=0, keepdims=True)
    ep2 = jnp.where(rowi == e1, -1.0, ep)
    p2 = jnp.max(ep2, axis=0, keepdims=True)
    e2 = jnp.min(jnp.where(ep2 == p2, rowi, EXPERTS_PER_GROUP), axis=0, keepdims=True)
    tot = p1 + p2
    w1 = p1 / tot * gw
    w2 = p2 / tot * gw
    id1 = (gidx * EXPERTS_PER_GROUP + e1).astype(F32)
    id2 = (gidx * EXPERTS_PER_GROUP + e2).astype(F32)
    route_ref[...] = jnp.concatenate([id1, id2, w1, w2, jnp.zeros((ROUTE_ROWS - 4, tt), F32)], axis=0)


def _merge(x, u, y, o_swa, o_mem, p, tile):
    n = x.shape[0]
    tile = min(tile, n)
    row = lambda i: (i, 0)
    const = lambda i: (0, 0)
    full = lambda a: pl.BlockSpec(a.shape, const, pipeline_mode=pl.Buffered(1))
    weights = [p['g1'], p['w_gates'], p['d_skip'], p['w_glu'], p['b_glu'], p['w_br_ssm'], p['w_br_swa'],
               p['w_br_mem'], p['w_out'], p['g2'], p['w_router'], p['b_router']]
    return pl.pallas_call(
        _merge_kernel,
        grid=(n // tile,),
        in_specs=[pl.BlockSpec((tile, D_MODEL), row), pl.BlockSpec((tile, SSM_WIDTH), row),
                  pl.BlockSpec((tile, SSM_WIDTH), row), pl.BlockSpec((tile, SWA_WIDTH), row),
                  pl.BlockSpec((tile, MEM_WIDTH), row)] + [full(w) for w in weights],
        out_specs=[pl.BlockSpec((tile, D_MODEL), row), pl.BlockSpec((tile, HALF), row),
                   pl.BlockSpec((ROUTE_ROWS, tile), lambda i: (0, i))],
        out_shape=[jax.ShapeDtypeStruct((n, D_MODEL), F32), jax.ShapeDtypeStruct((n, HALF), jnp.int32),
                   jax.ShapeDtypeStruct((ROUTE_ROWS, n), F32)],
        compiler_params=_cparams("parallel"),
        name="merge_router",
    )(x, u, y, o_swa, o_mem, *weights)


def _expert_mlp(xp, wgu_ref, wd_ref):
    hi, lo = _unpack_halves(xp)
    hgu = _dot(hi, wgu_ref[0:HALF, :]) + _dot(lo, wgu_ref[HALF:D_MODEL, :])
    hh = jax.nn.silu(hgu[:, 0:D_EXPERT]) * hgu[:, D_EXPERT:2 * D_EXPERT]
    return _dot(hh.astype(BF16), wd_ref[...])


def _moe_kernel(xn2_ref, rt_ref, wgu_ref, wd_ref, h_ref, gf_ref, o_ref, acc_ref):
    e = pl.program_id(1)

    @pl.when(e == 0)
    def _():
        acc_ref[...] = jnp.zeros_like(acc_ref)

    o = _expert_mlp(xn2_ref[...], wgu_ref, wd_ref)
    ef = e.astype(F32)
    c = (jnp.where(rt_ref[:, 0:1] == ef, rt_ref[:, 2:3], 0.0)
         + jnp.where(rt_ref[:, 1:2] == ef, rt_ref[:, 3:4], 0.0))
    acc_ref[...] += c * o

    @pl.when(e == N_EXPERTS - 1)
    def _():
        o_ref[...] = _rms(h_ref[...] + acc_ref[...], gf_ref[...])


def _moe(xn2, route_t, w_gu, w_d, h, gf, tile):
    n = h.shape[0]
    tile = min(tile, n)
    return pl.pallas_call(
        _moe_kernel,
        grid=(n // tile, N_EXPERTS),
        in_specs=[pl.BlockSpec((tile, HALF), lambda i, e: (i, 0)),
                  pl.BlockSpec((tile, ROUTE_ROWS), lambda i, e: (i, 0)),
                  pl.BlockSpec((None, D_MODEL, 2 * D_EXPERT), lambda i, e: (e, 0, 0)),
                  pl.BlockSpec((None, D_EXPERT, D_MODEL), lambda i, e: (e, 0, 0)),
                  pl.BlockSpec((tile, D_MODEL), lambda i, e: (i, 0)),
                  pl.BlockSpec((1, D_MODEL), lambda i, e: (0, 0))],
        out_specs=pl.BlockSpec((tile, D_MODEL), lambda i, e: (i, 0)),
        out_shape=jax.ShapeDtypeStruct((n, D_MODEL), F32),
        scratch_shapes=[pltpu.VMEM((tile, D_MODEL), F32)],
        compiler_params=_cparams("parallel", "arbitrary"),
        name="moe_final_norm",
    )(xn2, route_t, w_gu, w_d, h, gf)


EXPERT_ROW_TILE = 512
SC_CORES = 2
SC_SUBCORES = 16
SC_WORKERS = SC_CORES * SC_SUBCORES
SC_SCATTER_ROWS = 64
SC_GATHER_ROWS = 32


def _route_rank_kernel(r_ref, rank_ref, cnt_ref, base_ref):
    i = pl.program_id(0)
    tt = r_ref.shape[1]

    @pl.when(i == 0)
    def _():
        base_ref[...] = jnp.zeros_like(base_ref)

    ids = r_ref[0:2, :].astype(jnp.int32)
    e_iota = lax.broadcasted_iota(jnp.int32, (N_EXPERTS, tt), 0)
    oh1 = jnp.where(e_iota == ids[0:1], 1.0, 0.0)
    oh2 = jnp.where(e_iota == ids[1:2], 1.0, 0.0)
    before = (lax.broadcasted_iota(jnp.int32, (tt, tt), 0) < lax.broadcasted_iota(jnp.int32, (tt, tt), 1))
    tri = jnp.where(before, 1.0, 0.0).astype(BF16)
    c1 = _dot(oh1.astype(BF16), tri)
    c2 = _dot(oh2.astype(BF16), tri)
    tot1 = jnp.sum(oh1, axis=1, keepdims=True)
    tot2 = jnp.sum(oh2, axis=1, keepdims=True)
    base = base_ref[:, 0:1]
    rank1 = jnp.sum(oh1 * (base + c1), axis=0, keepdims=True)
    rank2 = jnp.sum(oh2 * (base + tot1 + c2), axis=0, keepdims=True)
    rank_ref[...] = jnp.concatenate([rank1, rank2, jnp.zeros((ROUTE_ROWS - 2, tt), F32)], axis=0).astype(jnp.int32)
    new_base = jnp.broadcast_to(base + tot1 + tot2, base_ref.shape)
    base_ref[...] = new_base
    cnt_ref[...] = new_base.astype(jnp.int32)


def _route_rank(route, tile):
    n = route.shape[1]
    tile = min(tile, n)
    return pl.pallas_call(
        _route_rank_kernel,
        grid=(n // tile,),
        in_specs=[pl.BlockSpec((ROUTE_ROWS, tile), lambda i: (0, i))],
        out_specs=[pl.BlockSpec((ROUTE_ROWS, tile), lambda i: (0, i)),
                   pl.BlockSpec((N_EXPERTS, LANES), lambda i: (0, 0))],
        out_shape=[jax.ShapeDtypeStruct((ROUTE_ROWS, n), jnp.int32),
                   jax.ShapeDtypeStruct((N_EXPERTS, LANES), jnp.int32)],
        scratch_shapes=[pltpu.VMEM((N_EXPERTS, LANES), F32)],
        compiler_params=_cparams("arbitrary"),
        name="route_rank",
    )(route)


def _sc_mesh():
    return plsc.VectorSubcoreMesh(core_axis_name="core", subcore_axis_name="subcore")


def _sc_scatter_pairs(x, pos, rows_out):
    n, d = x.shape
    per_w = n // SC_WORKERS
    window = min(SC_SCATTER_ROWS, per_w)

    @pl.kernel(out_type=jax.ShapeDtypeStruct((rows_out, d), x.dtype), mesh=_sc_mesh(),
               scratch_types=[pltpu.VMEM((window,), jnp.int32), pltpu.VMEM((window,), jnp.int32),
                              pltpu.VMEM((window, d), x.dtype), pltpu.SemaphoreType.DMA])
    def scatter(x_hbm, p_hbm, o_hbm, i1_v, i2_v, rows_v, sem):
        wid = lax.axis_index("subcore") * SC_CORES + lax.axis_index("core")

        @pl.loop(0, per_w // window)
        def _(j):
            base = wid * per_w + j * window
            pltpu.sync_copy(p_hbm.at[pl.ds(base, window)], i1_v)
            pltpu.sync_copy(p_hbm.at[pl.ds(n + base, window)], i2_v)
            pltpu.sync_copy(x_hbm.at[pl.ds(base, window)], rows_v)
            pltpu.async_copy(rows_v, o_hbm.at[i1_v], sem).wait()
            pltpu.async_copy(rows_v, o_hbm.at[i2_v], sem).wait()

    return scatter(x, pos)


def _sc_gather_rows(table, idx):
    m = idx.shape[0]
    d = table.shape[1]
    per_w = m // SC_WORKERS
    window = min(SC_GATHER_ROWS, per_w)

    @pl.kernel(out_type=jax.ShapeDtypeStruct((m, d), table.dtype), mesh=_sc_mesh(),
               scratch_types=[pltpu.VMEM((window,), jnp.int32), pltpu.VMEM((window, d), table.dtype),
                              pltpu.SemaphoreType.DMA])
    def gather(t_hbm, i_hbm, o_hbm, i_v, rows_v, sem):
        wid = lax.axis_index("subcore") * SC_CORES + lax.axis_index("core")

        @pl.loop(0, per_w // window)
        def _(j):
            base = wid * per_w + j * window
            pltpu.sync_copy(i_hbm.at[pl.ds(base, window)], i_v)
            pltpu.async_copy(t_hbm.at[i_v], rows_v, sem).wait()
            pltpu.sync_copy(rows_v, o_hbm.at[pl.ds(base, window)])

    return gather(table, idx)


def _expert_tiles_kernel(te_ref, nu_ref, x_ref, wgu_ref, wd_ref, o_ref):
    @pl.when(pl.program_id(0) < nu_ref[0])
    def _():
        o_ref[...] = _expert_mlp(x_ref[...], wgu_ref, wd_ref)


def _expert_tiles(tile_expert, n_used, xs, w_gu, w_d):
    rows = xs.shape[0]
    tm = EXPERT_ROW_TILE
    grid_spec = pltpu.PrefetchScalarGridSpec(
        num_scalar_prefetch=2,
        grid=(rows // tm,),
        in_specs=[pl.BlockSpec((tm, HALF), lambda i, te, nu: (i, 0)),
                  pl.BlockSpec((None, D_MODEL, 2 * D_EXPERT), lambda i, te, nu: (te[i], 0, 0)),
                  pl.BlockSpec((None, D_EXPERT, D_MODEL), lambda i, te, nu: (te[i], 0, 0))],
        out_specs=pl.BlockSpec((tm, D_MODEL), lambda i, te, nu: (i, 0)),
    )
    return pl.pallas_call(
        _expert_tiles_kernel,
        grid_spec=grid_spec,
        out_shape=jax.ShapeDtypeStruct((rows, D_MODEL), F32),
        compiler_params=_cparams("arbitrary"),
        name="expert_tiles",
    )(tile_expert, n_used, xs, w_gu, w_d)


def _combine_kernel(h_ref, o1_ref, o2_ref, rt_ref, gf_ref, y_ref):
    moe = rt_ref[:, 2:3] * o1_ref[...] + rt_ref[:, 3:4] * o2_ref[...]
    y_ref[...] = _rms(h_ref[...] + moe, gf_ref[...])


def _combine(h, o12, route_t, gf, tile):
    n = h.shape[0]
    tile = min(tile, n)
    nt = n // tile
    return pl.pallas_call(
        _combine_kernel,
        grid=(nt,),
        in_specs=[pl.BlockSpec((tile, D_MODEL), lambda i: (i, 0)),
                  pl.BlockSpec((tile, D_MODEL), lambda i: (i, 0)),
                  pl.BlockSpec((tile, D_MODEL), lambda i: (i + nt, 0)),
                  pl.BlockSpec((tile, ROUTE_ROWS), lambda i: (i, 0)),
                  pl.BlockSpec((1, D_MODEL), lambda i: (0, 0))],
        out_specs=pl.BlockSpec((tile, D_MODEL), lambda i: (i, 0)),
        out_shape=jax.ShapeDtypeStruct((n, D_MODEL), F32),
        compiler_params=_cparams("parallel"),
        name="combine_final_norm",
    )(h, o12, o12, route_t, gf)


def _sparse_moe(xn2p, route, h, w_gu, w_d, gf):
    n = h.shape[0]
    tm = EXPERT_ROW_TILE
    rows = 2 * n + N_EXPERTS * tm
    rank, cnt = _route_rank(route, 512)
    counts = cnt[:, 0]
    padded = (counts + tm - 1) // tm * tm
    e_idx = jnp.arange(N_EXPERTS, dtype=jnp.int32)
    starts = jnp.sum(jnp.where(e_idx[None, :] < e_idx[:, None], padded[None, :], 0), axis=1)
    ends = starts + padded
    ids = route[0:2].astype(jnp.int32)
    start_of = jnp.sum(jnp.where(ids[None] == e_idx[:, None, None], starts[:, None, None], 0), axis=0)
    pos = (start_of + rank[0:2]).reshape(2 * n)
    tile_start = jnp.arange(rows // tm, dtype=jnp.int32) * tm
    tile_expert = jnp.minimum(jnp.sum((tile_start[:, None] >= ends[None, :]).astype(jnp.int32), axis=1),
                              N_EXPERTS - 1)
    n_used = (ends[-1:] // tm).astype(jnp.int32)
    xs = _sc_scatter_pairs(xn2p, pos, rows)
    os_ = _expert_tiles(tile_expert, n_used, xs, w_gu, w_d)
    o12 = _sc_gather_rows(os_, pos)
    return _combine(h, o12, route.T, gf, 512)


def _prep_in_weights(w_in):
    o = 0
    w_u = w_in[:, o:o + SSM_WIDTH]; o += SSM_WIDTH
    w_q = w_in[:, o:o + SWA_WIDTH]; o += SWA_WIDTH
    w_k = w_in[:, o:o + SWA_KV_WIDTH]; o += SWA_KV_WIDTH
    w_v = w_in[:, o:o + SWA_KV_WIDTH]; o += SWA_KV_WIDTH
    w_qm = w_in[:, o:o + MEM_WIDTH]; o += MEM_WIDTH
    w_g = w_in[:, o:]
    wq = (w_q * (SWA_HEAD_DIM ** -0.5)).reshape(D_MODEL, SWA_KV_HEADS, SWA_REP, SWA_HEAD_DIM)
    wq = wq.transpose(0, 2, 1, 3).reshape(D_MODEL, SWA_WIDTH)
    w_main = jnp.concatenate([w_u, wq, w_k, w_v, w_qm], axis=1).astype(BF16)
    return w_main, w_g.astype(BF16)


IN_SPLITS = (SSM_WIDTH, SWA_WIDTH, SWA_KV_WIDTH, SWA_KV_WIDTH, MEM_WIDTH)
IN_DTYPES = ((F32, BF16), (BF16,), (F32,), (F32,), (BF16,))


def kernel(x_prompt, x_sample, cache_swa_k, cache_swa_v, state_ssm_re, state_ssm_im, cache_mem_k, cache_mem_v, mem_prompt, norm1_g, w_in, lam_re, lam_im, log_dt, bm_re, bm_im, cm_re, cm_im, d_skip, w_glu, b_glu, sinks, rel_table, mem_norm_g, w_mem_kv, w_br_ssm, w_br_swa, w_br_mem, w_out, norm2_g, w_rg, b_rg, w_rexp, b_rexp, w_e_gate, w_e_up, w_e_down, final_norm_g):
    nb, t, _ = x_prompt.shape
    ns, ts, _ = x_sample.shape
    l = 0
    L = S5_CHUNK

    w_main, w_gates = _prep_in_weights(w_in[l])
    w_swa = (w_br_swa[l].reshape(SWA_KV_HEADS, SWA_REP, SWA_HEAD_DIM, D_MODEL).transpose(1, 0, 2, 3)
             .reshape(SWA_WIDTH, D_MODEL))
    pad_rows = ROUTER_ROWS - N_EXPERTS - N_EXPERT_GROUPS
    w_router = jnp.concatenate([w_rexp[l].T, w_rg[l].T, jnp.zeros((pad_rows, D_MODEL), F32)], axis=0).astype(BF16)
    b_router = jnp.concatenate([b_rexp[l], b_rg[l], jnp.zeros((pad_rows,), F32)]).reshape(ROUTER_ROWS, 1)
    mp = {
        'g1': norm1_g[l].reshape(1, D_MODEL), 'w_gates': w_gates, 'd_skip': d_skip[l].reshape(1, SSM_WIDTH),
        'w_glu': w_glu[l].astype(BF16), 'b_glu': b_glu[l].reshape(1, SSM_WIDTH),
        'w_br_ssm': w_br_ssm[l].astype(BF16), 'w_br_swa': w_swa.astype(BF16),
        'w_br_mem': w_br_mem[l].astype(BF16), 'w_out': w_out[l].astype(BF16),
        'g2': norm2_g[l].reshape(1, D_MODEL), 'w_router': w_router, 'b_router': b_router,
    }
    w_gu = jnp.concatenate([w_e_gate[l], w_e_up[l]], axis=-1).astype(BF16)
    w_d = w_e_down[l].astype(BF16)
    gf = final_norm_g.reshape(1, D_MODEL)
    s5_w = _s5_weights(lam_re[l], lam_im[l], log_dt[l], bm_re[l], bm_im[l], cm_re[l], cm_im[l], L)

    bias_p = _rel_bias(rel_table, np.arange(WINDOW)[:, None] + WINDOW - np.arange(2 * WINDOW)[None, :])
    keys_s = WINDOW + 2 * ts
    bias_s = _rel_bias(rel_table, np.arange(ts)[:, None] + WINDOW - np.arange(keys_s)[None, :])
    bias_s = bias_s.reshape(SWA_HEADS * ts, keys_s)
    sink_rows = jnp.repeat(sinks[l].astype(F32), ts).reshape(SWA_HEADS * ts, 1)

    n = nb * t
    xp = x_prompt.reshape(n, D_MODEL)
    mk, mv = _norm_proj(mem_prompt.reshape(nb * MEM_TOKENS, D_MODEL), mem_norm_g[l].reshape(1, D_MODEL),
                        w_mem_kv[l].astype(BF16), (MEM_WIDTH, MEM_WIDTH), ((F32,), (F32,)), 512)
    u, ub, qz, k, v, qm = _norm_proj(xp, mp['g1'], w_main, IN_SPLITS, IN_DTYPES, 512)

    y_ssm, fin = _s5(ub, jnp.zeros((nb, N_CH_TILES * 2 * STATE_TILE), F32), s5_w, nb, t // L, L)
    p_re, p_im = _tiles_to_state(fin)

    o_swa = _swa_prompt(qz, k, v, bias_p, sinks[l].astype(F32), nb, t, 4)
    o_mem = _mem_prompt(qm, mk, mv, nb, t, 512)
    h, xn2p, route = _merge(xp, u, y_ssm, o_swa, o_mem, mp, 512)
    y_prompt = _sparse_moe(xn2p, route, h, w_gu, w_d, gf).reshape(nb, t, D_MODEL)

    k4 = k.reshape(nb, t, SWA_KV_HEADS, SWA_HEAD_DIM)
    v4 = v.reshape(nb, t, SWA_KV_HEADS, SWA_HEAD_DIM)
    new_k_p, new_v_p = k4[:, -WINDOW:][None], v4[:, -WINDOW:][None]
    new_mk = mk.reshape(1, nb, MEM_TOKENS, MEM_HEADS, MEM_HEAD_DIM)
    new_mv = mv.reshape(1, nb, MEM_TOKENS, MEM_HEADS, MEM_HEAD_DIM)

    m = ns * ts
    xs = x_sample.reshape(m, D_MODEL)
    us, ubs, qzs, k_s, v_s, qms = _norm_proj(xs, mp['g1'], w_main, IN_SPLITS, IN_DTYPES, 256)
    ys_ssm, fins = _s5(ubs, _state_to_tiles(state_ssm_re[l], state_ssm_im[l]), s5_w, ns, ts // L, L)
    s_re, s_im = _tiles_to_state(fins)

    kk_all = jnp.concatenate([cache_swa_k[l].reshape(ns, WINDOW, SWA_KV_WIDTH).astype(F32),
                              k_s.reshape(ns, ts, SWA_KV_WIDTH)], axis=1)
    vv_all = jnp.concatenate([cache_swa_v[l].reshape(ns, WINDOW, SWA_KV_WIDTH).astype(F32),
                              v_s.reshape(ns, ts, SWA_KV_WIDTH)], axis=1)
    pad = jnp.zeros((ns, keys_s - WINDOW - ts, SWA_KV_WIDTH), F32)
    q5 = qzs.reshape(ns, ts, SWA_REP, SWA_KV_HEADS, SWA_HEAD_DIM)
    zq = jnp.zeros((ns, ts, SWA_REP, SWA_HEAD_DIM), BF16)
    q_rows = jnp.concatenate([jnp.concatenate([q5[:, :, :, 0], zq], axis=-1),
                              jnp.concatenate([zq, q5[:, :, :, 1]], axis=-1)], axis=2)
    q_rows = q_rows.transpose(0, 2, 1, 3).reshape(ns, SWA_HEADS * ts, LANES)
    o_dec = _swa_decode(q_rows, jnp.concatenate([kk_all, pad], axis=1), jnp.concatenate([vv_all, pad], axis=1),
                        bias_s, sink_rows, ts, 8)
    o_dec = o_dec.reshape(ns, SWA_KV_HEADS, SWA_REP, ts, SWA_KV_HEADS, SWA_HEAD_DIM)
    o_dec = jnp.stack([o_dec[:, g, :, :, g] for g in range(SWA_KV_HEADS)], axis=1)
    o_swa_s = o_dec.transpose(0, 3, 2, 1, 4).reshape(m, SWA_WIDTH).astype(BF16)

    o_mem_s = _mem_decode(qms.astype(F32).reshape(ns, ts, MEM_WIDTH), cache_mem_k, cache_mem_v, l, 4)
    o_mem_s = o_mem_s.reshape(m, MEM_WIDTH).astype(BF16)

    hs_, xn2ps, routes = _merge(xs, us, ys_ssm, o_swa_s, o_mem_s, mp, 256)
    y_sample = _moe(xn2ps, routes.T, w_gu, w_d, hs_, gf, 1024).reshape(ns, ts, D_MODEL)

    new_k_s = kk_all[:, -WINDOW:].reshape(1, ns, WINDOW, SWA_KV_HEADS, SWA_HEAD_DIM).astype(cache_swa_k.dtype)
    new_v_s = vv_all[:, -WINDOW:].reshape(1, ns, WINDOW, SWA_KV_HEADS, SWA_HEAD_DIM).astype(cache_swa_v.dtype)

    return (y_prompt, y_sample,
            new_k_p, new_v_p, p_re[None], p_im[None], new_mk, new_mv,
            new_k_s, new_v_s, s_re[None].astype(state_ssm_re.dtype), s_im[None].astype(state_ssm_im.dtype))
```

```python
import functools
import math

import numpy as np
import jax
import jax.numpy as jnp
from jax import lax
from jax.experimental import pallas as pl
from jax.experimental.pallas import tpu as pltpu
from jax.experimental.pallas import tpu_sc as plsc

F32 = jnp.float32
BF16 = jnp.bfloat16

D_MODEL = 1024
SSM_WIDTH = 512
SSM_GROUP = 16
SSM_GROUPS = 32
SSM_STATE = 64
SWA_HEADS = 8
SWA_KV_HEADS = 2
SWA_REP = 4
SWA_HEAD_DIM = 64
SWA_WIDTH = 512
SWA_KV_WIDTH = 128
WINDOW = 128
REL_BUCKETS = 32
REL_MAX_DIST = 128
MEM_TOKENS = 256
MEM_HEADS = 4
MEM_HEAD_DIM = 128
MEM_WIDTH = 512
N_EXPERT_GROUPS = 4
EXPERTS_PER_GROUP = 8
N_EXPERTS = 32
D_EXPERT = 256
EPS = 1e-6
NEG_INF = -1e30

LANES = 128
GROUPS_PER_TILE = LANES // SSM_GROUP
N_CH_TILES = SSM_WIDTH // LANES
STATE_TILE = GROUPS_PER_TILE * SSM_STATE
VMEM_LIMIT = 56 * 1024 * 1024
S5_CHUNK = 8

_TRANS_B = (((1,), (1,)), ((), ()))


def _cparams(*sem):
    return pltpu.CompilerParams(dimension_semantics=sem, vmem_limit_bytes=VMEM_LIMIT)


def _rms(x, g):
    return (x * lax.rsqrt(jnp.mean(x * x, axis=-1, keepdims=True) + EPS)) * g


def _dot(a, b):
    return jnp.dot(a, b, preferred_element_type=F32)


def _norm_proj_kernel(x_ref, g_ref, w_ref, *out_refs, splits, dtypes):
    xb = _rms(x_ref[...], g_ref[...]).astype(BF16)
    off = 0
    outs = iter(out_refs)
    for width, dts in zip(splits, dtypes):
        r = _dot(xb, w_ref[:, off:off + width])
        for dt in dts:
            next(outs)[...] = r.astype(dt)
        off += width


def _norm_proj(x, g, w, splits, dtypes, tile):
    n, d = x.shape
    tile = min(tile, n)
    flat = [(wd, dt) for wd, dts in zip(splits, dtypes) for dt in dts]
    return pl.pallas_call(
        functools.partial(_norm_proj_kernel, splits=tuple(splits), dtypes=tuple(dtypes)),
        grid=(n // tile,),
        in_specs=[pl.BlockSpec((tile, d), lambda i: (i, 0)),
                  pl.BlockSpec((1, d), lambda i: (0, 0)),
                  pl.BlockSpec((d, sum(splits)), lambda i: (0, 0))],
        out_specs=[pl.BlockSpec((tile, wd), lambda i: (i, 0)) for wd, _ in flat],
        out_shape=[jax.ShapeDtypeStruct((n, wd), dt) for wd, dt in flat],
        compiler_params=_cparams("parallel"),
        name="norm_proj",
    )(x, g, w)


def _group_mask(rows_per_group, cols_per_group):
    r = np.arange(GROUPS_PER_TILE * rows_per_group)[:, None] // rows_per_group
    c = np.arange(GROUPS_PER_TILE * cols_per_group)[None, :] // cols_per_group
    return jnp.asarray(r == c, F32)


def _s5_weights(lam_re, lam_im, log_dt, bm_re, bm_im, cm_re, cm_im, L):
    hp = lax.Precision.HIGHEST
    nt, gt, P, H = N_CH_TILES, GROUPS_PER_TILE, SSM_STATE, SSM_GROUP
    lr, li = lam_re.astype(F32), lam_im.astype(F32)
    dt = jnp.exp(log_dt.astype(F32))[:, None]
    mag = jnp.exp(lr * dt)
    a_re = mag * jnp.cos(li * dt)
    a_im = mag * jnp.sin(li * dt)
    den = lr * lr + li * li
    f_re = ((a_re - 1.0) * lr + a_im * li) / den
    f_im = (a_im * lr - (a_re - 1.0) * li) / den
    br, bi = bm_re.astype(F32), bm_im.astype(F32)
    bb_re = f_re[..., None] * br - f_im[..., None] * bi
    bb_im = f_re[..., None] * bi + f_im[..., None] * br
    pr, pi = [jnp.ones_like(a_re)], [jnp.zeros_like(a_im)]
    for _ in range(L):
        pr.append(pr[-1] * a_re - pi[-1] * a_im)
        pi.append(pr[-2] * a_im + pi[-1] * a_re)
    ap_re, ap_im = jnp.stack(pr), jnp.stack(pi)
    cr, ci = cm_re.astype(F32), cm_im.astype(F32)
    ca_re = cr[None] * ap_re[:, :, None, :] - ci[None] * ap_im[:, :, None, :]
    ca_im = cr[None] * ap_im[:, :, None, :] + ci[None] * ap_re[:, :, None, :]

    def expand(w, rows_per_group, cols_per_group):
        w = w.reshape(w.shape[0], nt, gt * rows_per_group, cols_per_group)
        return jnp.tile(w, (1, 1, 1, gt)) * _group_mask(rows_per_group, cols_per_group)

    rev_re = jnp.stack([pr[L - 1 - s] for s in range(L)])
    rev_im = jnp.stack([pi[L - 1 - s] for s in range(L)])
    ws_re = rev_re[..., None] * bb_re[None] - rev_im[..., None] * bb_im[None]
    ws_im = rev_re[..., None] * bb_im[None] + rev_im[..., None] * bb_re[None]
    w_st = jnp.concatenate([expand(ws_re.transpose(0, 1, 3, 2), H, P),
                            expand(ws_im.transpose(0, 1, 3, 2), H, P)], axis=3)
    w_st = w_st.transpose(1, 0, 2, 3).reshape(nt, L * LANES, 2 * STATE_TILE).astype(BF16)

    w_out = jnp.concatenate([expand(ca_re[1:].transpose(0, 1, 3, 2), P, H),
                             expand(-ca_im[1:].transpose(0, 1, 3, 2), P, H)], axis=2)
    w_out = w_out.astype(BF16).transpose(1, 2, 0, 3).reshape(nt, 2 * STATE_TILE, L * LANES)

    k_lag = (jnp.einsum('tghp,gpk->tgkh', ca_re[:L], bb_re, precision=hp)
             - jnp.einsum('tghp,gpk->tgkh', ca_im[:L], bb_im, precision=hp))
    blocks = expand(k_lag, H, H).astype(BF16)
    zero = jnp.zeros_like(blocks[0])
    toep = jnp.concatenate(
        [jnp.concatenate([blocks[t - s] if t >= s else zero for t in range(L)], axis=2) for s in range(L)], axis=1)

    def per_tile(v):
        return v.reshape(nt, 1, STATE_TILE)

    return w_st, w_out, toep, per_tile(pr[L]), per_tile(pi[L])


def _to_chunks(u, nb, nc, L):
    return (u.reshape(nb, nc, L, N_CH_TILES, LANES).transpose(1, 0, 3, 2, 4)
            .reshape(nc * nb, N_CH_TILES * L * LANES))


def _from_chunks(y, nb, nc, L):
    return (y.reshape(nc, nb, N_CH_TILES, L, LANES).transpose(1, 0, 3, 2, 4)
            .reshape(nb * nc * L, SSM_WIDTH))


def _s5_state_in_kernel(x_ref, w_ref, d_ref):
    d_ref[...] = _dot(x_ref[...], w_ref[...])


def _s5_state_in(xc, w_st, L, row_tile):
    rows = xc.shape[0]
    row_tile = min(row_tile, rows)
    lk = L * LANES
    return pl.pallas_call(
        _s5_state_in_kernel,
        grid=(N_CH_TILES, rows // row_tile),
        in_specs=[pl.BlockSpec((row_tile, lk), lambda j, r: (r, j)),
                  pl.BlockSpec((None, lk, 2 * STATE_TILE), lambda j, r: (j, 0, 0))],
        out_specs=pl.BlockSpec((row_tile, 2 * STATE_TILE), lambda j, r: (r, j)),
        out_shape=jax.ShapeDtypeStruct((rows, N_CH_TILES * 2 * STATE_TILE), F32),
        compiler_params=_cparams("parallel", "parallel"),
        name="s5_state_in",
    )(xc, w_st)


def _s5_scan_kernel(d_ref, h0_ref, are_ref, aim_ref, hs_ref, fin_ref, hr_ref, hi_ref, *, cb, nb):
    ci = pl.program_id(1)

    @pl.when(ci == 0)
    def _():
        hr_ref[...] = h0_ref[:, 0:STATE_TILE]
        hi_ref[...] = h0_ref[:, STATE_TILE:2 * STATE_TILE]

    ar = jnp.broadcast_to(are_ref[...], (nb, STATE_TILE))
    ai = jnp.broadcast_to(aim_ref[...], (nb, STATE_TILE))

    def body(c, carry):
        hr, hi = carry
        hs_ref[c, :, 0:STATE_TILE] = hr
        hs_ref[c, :, STATE_TILE:2 * STATE_TILE] = hi
        d = d_ref[c]
        return (ar * hr - ai * hi + d[:, 0:STATE_TILE],
                ar * hi + ai * hr + d[:, STATE_TILE:2 * STATE_TILE])

    hr, hi = lax.fori_loop(0, cb, body, (hr_ref[...], hi_ref[...]))
    hr_ref[...] = hr
    hi_ref[...] = hi

    @pl.when(ci == pl.num_programs(1) - 1)
    def _():
        fin_ref[:, 0:STATE_TILE] = hr
        fin_ref[:, STATE_TILE:2 * STATE_TILE] = hi


def _s5_scan(d, h0, a_re, a_im, chunk_block):
    nc, nb, _ = d.shape
    cb = min(chunk_block, nc)
    st2 = 2 * STATE_TILE
    return pl.pallas_call(
        functools.partial(_s5_scan_kernel, cb=cb, nb=nb),
        grid=(N_CH_TILES, nc // cb),
        in_specs=[pl.BlockSpec((cb, nb, st2), lambda j, c: (c, 0, j)),
                  pl.BlockSpec((nb, st2), lambda j, c: (0, j)),
                  pl.BlockSpec((None, 1, STATE_TILE), lambda j, c: (j, 0, 0)),
                  pl.BlockSpec((None, 1, STATE_TILE), lambda j, c: (j, 0, 0))],
        out_specs=[pl.BlockSpec((cb, nb, st2), lambda j, c: (c, 0, j)),
                   pl.BlockSpec((nb, st2), lambda j, c: (0, j))],
        out_shape=[jax.ShapeDtypeStruct((nc, nb, N_CH_TILES * st2), F32),
                   jax.ShapeDtypeStruct((nb, N_CH_TILES * st2), F32)],
        scratch_shapes=[pltpu.VMEM((nb, STATE_TILE), F32), pltpu.VMEM((nb, STATE_TILE), F32)],
        compiler_params=_cparams("parallel", "arbitrary"),
        name="s5_scan",
    )(d, h0, a_re, a_im)


def _s5_out_kernel(x_ref, h_ref, t_ref, wo_ref, y_ref):
    y_ref[...] = _dot(x_ref[...], t_ref[...]) + _dot(h_ref[...].astype(BF16), wo_ref[...])


def _s5_out(xc, hs, toep, w_out, L, row_tile):
    rows = xc.shape[0]
    row_tile = min(row_tile, rows)
    lk = L * LANES
    st2 = 2 * STATE_TILE
    return pl.pallas_call(
        _s5_out_kernel,
        grid=(N_CH_TILES, rows // row_tile),
        in_specs=[pl.BlockSpec((row_tile, lk), lambda j, r: (r, j)),
                  pl.BlockSpec((row_tile, st2), lambda j, r: (r, j)),
                  pl.BlockSpec((None, lk, lk), lambda j, r: (j, 0, 0)),
                  pl.BlockSpec((None, st2, lk), lambda j, r: (j, 0, 0))],
        out_specs=pl.BlockSpec((row_tile, lk), lambda j, r: (r, j)),
        out_shape=jax.ShapeDtypeStruct((rows, N_CH_TILES * lk), F32),
        compiler_params=_cparams("parallel", "parallel"),
        name="s5_out",
    )(xc, hs, toep, w_out)


def _s5(ub, h0, weights, nb, nc, L):
    w_st, w_so, toep, a_re, a_im = weights
    xc = _to_chunks(ub, nb, nc, L)
    d = _s5_state_in(xc, w_st, L, 512)
    hs, fin = _s5_scan(d.reshape(nc, nb, -1), h0, a_re, a_im, 64)
    y = _s5_out(xc, hs.reshape(nc * nb, -1), toep, w_so, L, 512)
    return _from_chunks(y, nb, nc, L), fin


def _state_to_tiles(h_re, h_im):
    nb = h_re.shape[0]
    r = h_re.astype(F32).reshape(nb, N_CH_TILES, STATE_TILE)
    i = h_im.astype(F32).reshape(nb, N_CH_TILES, STATE_TILE)
    return jnp.concatenate([r, i], axis=-1).reshape(nb, N_CH_TILES * 2 * STATE_TILE)


def _tiles_to_state(h):
    nb = h.shape[0]
    h = h.reshape(nb, N_CH_TILES, 2, GROUPS_PER_TILE, SSM_STATE)
    return (h[:, :, 0].reshape(nb, SSM_GROUPS, SSM_STATE), h[:, :, 1].reshape(nb, SSM_GROUPS, SSM_STATE))


def _t5_bucket(dist):
    n = np.maximum(dist, 0)
    max_exact = REL_BUCKETS // 2
    nf = np.maximum(n, 1).astype(np.float32)
    large = max_exact + (np.log(nf / np.float32(max_exact)) / np.float32(math.log(REL_MAX_DIST / max_exact))
                         * np.float32(REL_BUCKETS - max_exact)).astype(np.int32)
    large = np.minimum(large, REL_BUCKETS - 1)
    return np.where(n < max_exact, n, large)


def _rel_bias(rel_table, dist):
    bucket = _t5_bucket(dist)
    tab = rel_table.astype(F32)
    out = jnp.zeros((SWA_HEADS,) + dist.shape, F32)
    for b in range(REL_BUCKETS):
        sel = jnp.asarray(bucket == b)
        if bool((bucket == b).any()):
            out = jnp.where(sel[None], tab[b].reshape((SWA_HEADS,) + (1,) * dist.ndim), out)
    return out


def _softmax_sink(s, sink):
    m = jnp.maximum(jnp.max(s, axis=-1, keepdims=True), sink)
    e = jnp.exp(s - m)
    den = jnp.sum(e, axis=-1, keepdims=True) + jnp.exp(sink - m)
    return e * (1.0 / den)


def _swa_prompt_kernel(sink_ref, q_ref, kp_ref, kc_ref, vp_ref, vc_ref, bias_ref, o_ref, kk_ref, vv_ref, *, qblocks):
    step = pl.program_id(1)
    kk_ref[0:WINDOW, :] = kp_ref[...].astype(BF16)
    kk_ref[WINDOW:, :] = kc_ref[...].astype(BF16)
    vv_ref[0:WINDOW, :] = vp_ref[...].astype(BF16)
    vv_ref[WINDOW:, :] = vc_ref[...].astype(BF16)
    row = lax.broadcasted_iota(jnp.int32, (WINDOW, 2 * WINDOW), 0)
    col = lax.broadcasted_iota(jnp.int32, (WINDOW, 2 * WINDOW), 1)
    dist = row + WINDOW - col
    band = (dist >= 0) & (dist < WINDOW)
    lane = lax.broadcasted_iota(jnp.int32, (WINDOW, LANES), 1)
    low = lane < SWA_HEAD_DIM

    def block(j, carry):
        r0 = pl.multiple_of(j * WINDOW, WINDOW)
        kk = kk_ref[pl.ds(r0, 2 * WINDOW), :]
        vv = vv_ref[pl.ds(r0, 2 * WINDOW), :]
        valid = band & ((col >= WINDOW) | (step * qblocks + j > 0))
        for t in range(SWA_REP):
            q2 = q_ref[pl.ds(r0, WINDOW), t * LANES:(t + 1) * LANES]
            outs = []
            for half in range(SWA_KV_HEADS):
                h = t + SWA_REP * half
                qh = jnp.where(low if half == 0 else jnp.logical_not(low), q2, jnp.zeros_like(q2))
                s = lax.dot_general(qh, kk, _TRANS_B, preferred_element_type=F32)
                s = jnp.where(valid, s + bias_ref[h], NEG_INF)
                p = _softmax_sink(s, sink_ref[h]).astype(BF16)
                outs.append(_dot(p, vv))
            o_ref[pl.ds(r0, WINDOW), t * LANES:(t + 1) * LANES] = jnp.where(low, outs[0], outs[1]).astype(BF16)
        return carry

    lax.fori_loop(0, qblocks, block, 0)


def _swa_prompt(q, k, v, bias, sinks, nb, t, qblocks):
    nstep = t // (WINDOW * qblocks)
    rows = WINDOW * qblocks
    cur = lambda b, i: (b * nstep + i, 0)
    prev = lambda b, i: (b * nstep * qblocks + jnp.maximum(i * qblocks - 1, 0), 0)
    return pl.pallas_call(
        functools.partial(_swa_prompt_kernel, qblocks=qblocks),
        grid=(nb, nstep),
        in_specs=[pl.BlockSpec(memory_space=pltpu.SMEM),
                  pl.BlockSpec((rows, SWA_WIDTH), cur),
                  pl.BlockSpec((WINDOW, SWA_KV_WIDTH), prev),
                  pl.BlockSpec((rows, SWA_KV_WIDTH), cur),
                  pl.BlockSpec((WINDOW, SWA_KV_WIDTH), prev),
                  pl.BlockSpec((rows, SWA_KV_WIDTH), cur),
                  pl.BlockSpec((SWA_HEADS, WINDOW, 2 * WINDOW), lambda b, i: (0, 0, 0))],
        out_specs=pl.BlockSpec((rows, SWA_WIDTH), cur),
        out_shape=jax.ShapeDtypeStruct((nb * t, SWA_WIDTH), BF16),
        scratch_shapes=[pltpu.VMEM((rows + WINDOW, SWA_KV_WIDTH), BF16),
                        pltpu.VMEM((rows + WINDOW, SWA_KV_WIDTH), BF16)],
        compiler_params=_cparams("parallel", "parallel"),
        name="swa_prompt",
    )(sinks, q, k, k, v, v, bias)


def _swa_decode_kernel(q_ref, k_ref, v_ref, bias_ref, sink_ref, o_ref, *, seqs, tq):
    rows, keys = q_ref.shape[1], k_ref.shape[1]
    qi = lax.broadcasted_iota(jnp.int32, (rows, keys), 0) % tq
    col = lax.broadcasted_iota(jnp.int32, (rows, keys), 1)
    dist = qi + WINDOW - col
    valid = (dist >= 0) & (dist < WINDOW)
    bias = bias_ref[...]
    sink = sink_ref[...]
    for s_i in range(seqs):
        kk = k_ref[s_i].astype(BF16)
        s = lax.dot_general(q_ref[s_i], kk, _TRANS_B, preferred_element_type=F32)
        s = jnp.where(valid, s + bias, NEG_INF)
        p = _softmax_sink(s, sink).astype(BF16)
        o_ref[s_i] = _dot(p, v_ref[s_i].astype(BF16))


def _swa_decode(qz, k_all, v_all, bias, sink_rows, tq, seqs):
    nseq, rows, _ = qz.shape
    keys = k_all.shape[1]
    seqs = min(seqs, nseq)
    return pl.pallas_call(
        functools.partial(_swa_decode_kernel, seqs=seqs, tq=tq),
        grid=(nseq // seqs,),
        in_specs=[pl.BlockSpec((seqs, rows, LANES), lambda i: (i, 0, 0)),
                  pl.BlockSpec((seqs, keys, LANES), lambda i: (i, 0, 0)),
                  pl.BlockSpec((seqs, keys, LANES), lambda i: (i, 0, 0)),
                  pl.BlockSpec((rows, keys), lambda i: (0, 0)),
                  pl.BlockSpec((rows, 1), lambda i: (0, 0))],
        out_specs=pl.BlockSpec((seqs, rows, LANES), lambda i: (i, 0, 0)),
        out_shape=jax.ShapeDtypeStruct((nseq, rows, LANES), F32),
        compiler_params=_cparams("parallel"),
        name="swa_decode",
    )(qz, k_all, v_all, bias, sink_rows)


def _softmax(s):
    m = jnp.max(s, axis=-1, keepdims=True)
    e = jnp.exp(s - m)
    return e * (1.0 / jnp.sum(e, axis=-1, keepdims=True))


def _mem_prompt_kernel(q_ref, k_ref, v_ref, o_ref):
    scale = MEM_HEAD_DIM ** -0.5
    for h in range(MEM_HEADS):
        sl = slice(h * MEM_HEAD_DIM, (h + 1) * MEM_HEAD_DIM)
        s = lax.dot_general(q_ref[:, sl], k_ref[:, sl].astype(BF16), _TRANS_B, preferred_element_type=F32) * scale
        p = _softmax(s).astype(BF16)
        o_ref[:, sl] = _dot(p, v_ref[:, sl].astype(BF16)).astype(BF16)


def _mem_prompt(qm, mk, mv, nb, t, tile):
    tile = min(tile, t)
    nt = t // tile
    return pl.pallas_call(
        _mem_prompt_kernel,
        grid=(nb, nt),
        in_specs=[pl.BlockSpec((tile, MEM_WIDTH), lambda b, i: (b * nt + i, 0)),
                  pl.BlockSpec((MEM_TOKENS, MEM_WIDTH), lambda b, i: (b, 0)),
                  pl.BlockSpec((MEM_TOKENS, MEM_WIDTH), lambda b, i: (b, 0))],
        out_specs=pl.BlockSpec((tile, MEM_WIDTH), lambda b, i: (b * nt + i, 0)),
        out_shape=jax.ShapeDtypeStruct((nb * t, MEM_WIDTH), BF16),
        compiler_params=_cparams("parallel", "parallel"),
        name="mem_prompt",
    )(qm, mk, mv)


def _mem_decode_kernel(q_ref, k_ref, v_ref, o_ref, *, seqs):
    scale = MEM_HEAD_DIM ** -0.5
    for s_i in range(seqs):
        for h in range(MEM_HEADS):
            sl = slice(h * MEM_HEAD_DIM, (h + 1) * MEM_HEAD_DIM)
            s = lax.dot_general(q_ref[s_i, :, sl].astype(BF16), k_ref[s_i, :, h, :].astype(BF16), _TRANS_B,
                                preferred_element_type=F32) * scale
            p = _softmax(s).astype(BF16)
            o_ref[s_i, :, sl] = _dot(p, v_ref[s_i, :, h, :].astype(BF16))


def _mem_decode(q, k, v, layer, seqs):
    nseq, tq, _ = q.shape
    seqs = min(seqs, nseq)
    cache = pl.BlockSpec((None, seqs, MEM_TOKENS, MEM_HEADS, MEM_HEAD_DIM), lambda i: (layer, i, 0, 0, 0))
    return pl.pallas_call(
        functools.partial(_mem_decode_kernel, seqs=seqs),
        grid=(nseq // seqs,),
        in_specs=[pl.BlockSpec((seqs, tq, MEM_WIDTH), lambda i: (i, 0, 0)), cache, cache],
        out_specs=pl.BlockSpec((seqs, tq, MEM_WIDTH), lambda i: (i, 0, 0)),
        out_shape=jax.ShapeDtypeStruct((nseq, tq, MEM_WIDTH), F32),
        compiler_params=_cparams("parallel"),
        name="mem_decode",
    )(q, k, v)


ROUTER_ROWS = 40
ROUTE_ROWS = 8
HALF = D_MODEL // 2


def _pack_halves(xb):
    hi = pltpu.bitcast(xb[:, 0:HALF].astype(F32), jnp.int32)
    lo = pltpu.bitcast(xb[:, HALF:D_MODEL].astype(F32), jnp.int32)
    return hi | lax.shift_right_logical(lo, jnp.int32(16))


def _unpack_halves(p):
    hi = pltpu.bitcast(p & jnp.int32(-65536), F32).astype(BF16)
    lo = pltpu.bitcast(lax.shift_left(p, jnp.int32(16)), F32).astype(BF16)
    return hi, lo


def _merge_kernel(x_ref, u_ref, y_ref, os_ref, om_ref, g1_ref, wg_ref, dsk_ref, wglu_ref, bglu_ref,
                  wbs_ref, wbw_ref, wbm_ref, wout_ref, g2_ref, wr_ref, br_ref,
                  h_ref, xn2_ref, route_ref):
    x = x_ref[...]
    tt = x.shape[0]
    xb = _rms(x, g1_ref[...]).astype(BF16)
    z = jax.nn.gelu(y_ref[...] + dsk_ref[...] * u_ref[...])
    z = z * jax.nn.sigmoid(_dot(z.astype(BF16), wglu_ref[...]) + bglu_ref[...])
    merged = jax.nn.sigmoid(_dot(xb, wg_ref[:, 0:D_MODEL])) * _dot(z.astype(BF16), wbs_ref[...])
    merged = merged + jax.nn.sigmoid(_dot(xb, wg_ref[:, D_MODEL:2 * D_MODEL])) * _dot(os_ref[...], wbw_ref[...])
    merged = merged + jax.nn.sigmoid(_dot(xb, wg_ref[:, 2 * D_MODEL:3 * D_MODEL])) * _dot(om_ref[...], wbm_ref[...])
    h = x + _dot(merged.astype(BF16), wout_ref[...])
    h_ref[...] = h
    xn2 = _rms(h, g2_ref[...]).astype(BF16)
    xn2_ref[...] = _pack_halves(xn2)

    lt = lax.dot_general(wr_ref[...], xn2, _TRANS_B, preferred_element_type=F32) + br_ref[...]
    gl = lt[N_EXPERTS:N_EXPERTS + N_EXPERT_GROUPS]
    ge = jnp.exp(gl - jnp.max(gl, axis=0, keepdims=True))
    gp = ge / jnp.sum(ge, axis=0, keepdims=True)
    gw = jnp.max(gp, axis=0, keepdims=True)
    gidx = jnp.full((1, tt), N_EXPERT_GROUPS - 1, jnp.int32)
    for r in range(N_EXPERT_GROUPS - 2, -1, -1):
        gidx = jnp.where(gp[r:r + 1] == gw, r, gidx)
    ein = lt[(N_EXPERT_GROUPS - 1) * EXPERTS_PER_GROUP:N_EXPERTS]
    for r in range(N_EXPERT_GROUPS - 2, -1, -1):
        ein = jnp.where(gidx == r, lt[r * EXPERTS_PER_GROUP:(r + 1) * EXPERTS_PER_GROUP], ein)
    ee = jnp.exp(ein - jnp.max(ein, axis=0, keepdims=True))
    ep = ee / jnp.sum(ee, axis=0, keepdims=True)
    rowi = lax.broadcasted_iota(jnp.int32, (EXPERTS_PER_GROUP, tt), 0)
    p1 = jnp.max(ep, axis=0, keepdims=True)
    e1 = jnp.min(jnp.where(ep == p1, rowi, EXPERTS_PER_GROUP), axis=0, keepdims=True)
    ep2 = jnp.where(rowi == e1, -1.0, ep)
    p2 = jnp.max(ep2, axis=0, keepdims=True)
    e2 = jnp.min(jnp.where(ep2 == p2, rowi, EXPERTS_PER_GROUP), axis=0, keepdims=True)
    tot = p1 + p2
    w1 = p1 / tot * gw
    w2 = p2 / tot * gw
    id1 = (gidx * EXPERTS_PER_GROUP + e1).astype(F32)
    id2 = (gidx * EXPERTS_PER_GROUP + e2).astype(F32)
    route_ref[...] = jnp.concatenate([id1, id2, w1, w2, jnp.zeros((ROUTE_ROWS - 4, tt), F32)], axis=0)


def _merge(x, u, y, o_swa, o_mem, p, tile):
    n = x.shape[0]
    tile = min(tile, n)
    row = lambda i: (i, 0)
    const = lambda i: (0, 0)
    full = lambda a: pl.BlockSpec(a.shape, const, pipeline_mode=pl.Buffered(1))
    weights = [p['g1'], p['w_gates'], p['d_skip'], p['w_glu'], p['b_glu'], p['w_br_ssm'], p['w_br_swa'],
               p['w_br_mem'], p['w_out'], p['g2'], p['w_router'], p['b_router']]
    return pl.pallas_call(
        _merge_kernel,
        grid=(n // tile,),
        in_specs=[pl.BlockSpec((tile, D_MODEL), row), pl.BlockSpec((tile, SSM_WIDTH), row),
                  pl.BlockSpec((tile, SSM_WIDTH), row), pl.BlockSpec((tile, SWA_WIDTH), row),
                  pl.BlockSpec((tile, MEM_WIDTH), row)] + [full(w) for w in weights],
        out_specs=[pl.BlockSpec((tile, D_MODEL), row), pl.BlockSpec((tile, HALF), row),
                   pl.BlockSpec((ROUTE_ROWS, tile), lambda i: (0, i))],
        out_shape=[jax.ShapeDtypeStruct((n, D_MODEL), F32), jax.ShapeDtypeStruct((n, HALF), jnp.int32),
                   jax.ShapeDtypeStruct((ROUTE_ROWS, n), F32)],
        compiler_params=_cparams("parallel"),
        name="merge_router",
    )(x, u, y, o_swa, o_mem, *weights)


def _expert_mlp(xp, wg, wu, wd):
    hi, lo = _unpack_halves(xp)
    g = _dot(hi, wg[0:HALF, :]) + _dot(lo, wg[HALF:D_MODEL, :])
    u = _dot(hi, wu[0:HALF, :]) + _dot(lo, wu[HALF:D_MODEL, :])
    hh = jax.nn.silu(g) * u
    return _dot(hh.astype(BF16), wd[...])


def _moe_kernel(xn2_ref, rt_ref, wg_ref, wu_ref, wd_ref, h_ref, gf_ref, o_ref, acc_ref):
    e = pl.program_id(1)

    @pl.when(e == 0)
    def _():
        acc_ref[...] = jnp.zeros_like(acc_ref)

    o = _expert_mlp(xn2_ref[...], wg_ref[...].astype(BF16), wu_ref[...].astype(BF16), wd_ref[...].astype(BF16))
    ef = e.astype(F32)
    c = (jnp.where(rt_ref[:, 0:1] == ef, rt_ref[:, 2:3], 0.0)
         + jnp.where(rt_ref[:, 1:2] == ef, rt_ref[:, 3:4], 0.0))
    acc_ref[...] += c * o

    @pl.when(e == N_EXPERTS - 1)
    def _():
        o_ref[...] = _rms(h_ref[...] + acc_ref[...], gf_ref[...])


def _moe(xn2, route_t, w_g, w_u, w_d, h, gf, tile):
    n = h.shape[0]
    tile = min(tile, n)
    return pl.pallas_call(
        _moe_kernel,
        grid=(n // tile, N_EXPERTS),
        in_specs=[pl.BlockSpec((tile, HALF), lambda i, e: (i, 0)),
                  pl.BlockSpec((tile, ROUTE_ROWS), lambda i, e: (i, 0)),
                  pl.BlockSpec((None, D_MODEL, D_EXPERT), lambda i, e: (e, 0, 0)),
                  pl.BlockSpec((None, D_MODEL, D_EXPERT), lambda i, e: (e, 0, 0)),
                  pl.BlockSpec((None, D_EXPERT, D_MODEL), lambda i, e: (e, 0, 0)),
                  pl.BlockSpec((tile, D_MODEL), lambda i, e: (i, 0)),
                  pl.BlockSpec((1, D_MODEL), lambda i, e: (0, 0))],
        out_specs=pl.BlockSpec((tile, D_MODEL), lambda i, e: (i, 0)),
        out_shape=jax.ShapeDtypeStruct((n, D_MODEL), F32),
        scratch_shapes=[pltpu.VMEM((tile, D_MODEL), F32)],
        compiler_params=_cparams("parallel", "arbitrary"),
        name="moe_final_norm",
    )(xn2, route_t, w_g, w_u, w_d, h, gf)


EXPERT_ROW_TILE = 512
SC_CORES = 2
SC_SUBCORES = 16
SC_WORKERS = SC_CORES * SC_SUBCORES
SC_SCATTER_ROWS = 64
SC_GATHER_ROWS = 64


def _route_rank_kernel(r_ref, rank_ref, cnt_ref, base_ref):
    i = pl.program_id(0)
    tt = r_ref.shape[1]

    @pl.when(i == 0)
    def _():
        base_ref[...] = jnp.zeros_like(base_ref)

    ids = r_ref[0:2, :].astype(jnp.int32)
    e_iota = lax.broadcasted_iota(jnp.int32, (N_EXPERTS, tt), 0)
    oh1 = jnp.where(e_iota == ids[0:1], 1.0, 0.0)
    oh2 = jnp.where(e_iota == ids[1:2], 1.0, 0.0)
    before = (lax.broadcasted_iota(jnp.int32, (tt, tt), 0) < lax.broadcasted_iota(jnp.int32, (tt, tt), 1))
    tri = jnp.where(before, 1.0, 0.0).astype(BF16)
    c1 = _dot(oh1.astype(BF16), tri)
    c2 = _dot(oh2.astype(BF16), tri)
    tot1 = jnp.sum(oh1, axis=1, keepdims=True)
    tot2 = jnp.sum(oh2, axis=1, keepdims=True)
    base = base_ref[:, 0:1]
    rank1 = jnp.sum(oh1 * (base + c1), axis=0, keepdims=True)
    rank2 = jnp.sum(oh2 * (base + tot1 + c2), axis=0, keepdims=True)
    rank_ref[...] = jnp.concatenate([rank1, rank2, jnp.zeros((ROUTE_ROWS - 2, tt), F32)], axis=0).astype(jnp.int32)
    new_base = jnp.broadcast_to(base + tot1 + tot2, base_ref.shape)
    base_ref[...] = new_base
    cnt_ref[...] = new_base.astype(jnp.int32)


def _route_rank(route, tile):
    n = route.shape[1]
    tile = min(tile, n)
    return pl.pallas_call(
        _route_rank_kernel,
        grid=(n // tile,),
        in_specs=[pl.BlockSpec((ROUTE_ROWS, tile), lambda i: (0, i))],
        out_specs=[pl.BlockSpec((ROUTE_ROWS, tile), lambda i: (0, i)),
                   pl.BlockSpec((N_EXPERTS, LANES), lambda i: (0, 0))],
        out_shape=[jax.ShapeDtypeStruct((ROUTE_ROWS, n), jnp.int32),
                   jax.ShapeDtypeStruct((N_EXPERTS, LANES), jnp.int32)],
        scratch_shapes=[pltpu.VMEM((N_EXPERTS, LANES), F32)],
        compiler_params=_cparams("arbitrary"),
        name="route_rank",
    )(route)


def _sc_mesh():
    return plsc.VectorSubcoreMesh(core_axis_name="core", subcore_axis_name="subcore")


def _sc_scatter_pairs(x, pos, rows_out):
    n, d = x.shape
    per_w = n // SC_WORKERS
    window = min(SC_SCATTER_ROWS, per_w)

    @pl.kernel(out_type=jax.ShapeDtypeStruct((rows_out, d), x.dtype), mesh=_sc_mesh(),
               scratch_types=[pltpu.VMEM((window,), jnp.int32), pltpu.VMEM((window,), jnp.int32),
                              pltpu.VMEM((window, d), x.dtype), pltpu.SemaphoreType.DMA])
    def scatter(x_hbm, p_hbm, o_hbm, i1_v, i2_v, rows_v, sem):
        wid = lax.axis_index("subcore") * SC_CORES + lax.axis_index("core")

        @pl.loop(0, per_w // window)
        def _(j):
            base = wid * per_w + j * window
            pltpu.sync_copy(p_hbm.at[pl.ds(base, window)], i1_v)
            pltpu.sync_copy(p_hbm.at[pl.ds(n + base, window)], i2_v)
            pltpu.sync_copy(x_hbm.at[pl.ds(base, window)], rows_v)
            pltpu.async_copy(rows_v, o_hbm.at[i1_v], sem).wait()
            pltpu.async_copy(rows_v, o_hbm.at[i2_v], sem).wait()

    return scatter(x, pos)


def _sc_gather_rows(table, idx):
    m = idx.shape[0]
    d = table.shape[1]
    per_w = m // SC_WORKERS
    window = min(SC_GATHER_ROWS, per_w)

    @pl.kernel(out_type=jax.ShapeDtypeStruct((m, d), table.dtype), mesh=_sc_mesh(),
               scratch_types=[pltpu.VMEM((window,), jnp.int32), pltpu.VMEM((window, d), table.dtype),
                              pltpu.SemaphoreType.DMA])
    def gather(t_hbm, i_hbm, o_hbm, i_v, rows_v, sem):
        wid = lax.axis_index("subcore") * SC_CORES + lax.axis_index("core")

        @pl.loop(0, per_w // window)
        def _(j):
            base = wid * per_w + j * window
            pltpu.sync_copy(i_hbm.at[pl.ds(base, window)], i_v)
            pltpu.async_copy(t_hbm.at[i_v], rows_v, sem).wait()
            pltpu.sync_copy(rows_v, o_hbm.at[pl.ds(base, window)])

    return gather(table, idx)


def _expert_tiles_kernel(te_ref, nu_ref, x_ref, wg_ref, wu_ref, wd_ref, o_ref, wg_s, wu_s, wd_s):
    i = pl.program_id(0)

    @pl.when(i < nu_ref[0])
    def _():
        @pl.when(jnp.logical_or(i == 0, te_ref[i] != te_ref[jnp.maximum(i - 1, 0)]))
        def _():
            wg_s[...] = wg_ref[...].astype(BF16)
            wu_s[...] = wu_ref[...].astype(BF16)
            wd_s[...] = wd_ref[...].astype(BF16)

        o_ref[...] = _pack_halves(_expert_mlp(x_ref[...], wg_s, wu_s, wd_s).astype(BF16))


def _expert_tiles(tile_expert, n_used, xs, w_g, w_u, w_d):
    rows = xs.shape[0]
    tm = EXPERT_ROW_TILE
    grid_spec = pltpu.PrefetchScalarGridSpec(
        num_scalar_prefetch=2,
        grid=(rows // tm,),
        in_specs=[pl.BlockSpec((tm, HALF), lambda i, te, nu: (i, 0)),
                  pl.BlockSpec((None, D_MODEL, D_EXPERT), lambda i, te, nu: (te[i], 0, 0)),
                  pl.BlockSpec((None, D_MODEL, D_EXPERT), lambda i, te, nu: (te[i], 0, 0)),
                  pl.BlockSpec((None, D_EXPERT, D_MODEL), lambda i, te, nu: (te[i], 0, 0))],
        out_specs=pl.BlockSpec((tm, HALF), lambda i, te, nu: (i, 0)),
        scratch_shapes=[pltpu.VMEM((D_MODEL, D_EXPERT), BF16), pltpu.VMEM((D_MODEL, D_EXPERT), BF16),
                        pltpu.VMEM((D_EXPERT, D_MODEL), BF16)],
    )
    return pl.pallas_call(
        _expert_tiles_kernel,
        grid_spec=grid_spec,
        out_shape=jax.ShapeDtypeStruct((rows, HALF), jnp.int32),
        compiler_params=_cparams("arbitrary"),
        name="expert_tiles",
    )(tile_expert, n_used, xs, w_g, w_u, w_d)


def _unpack_f32(p):
    return pltpu.bitcast(p & jnp.int32(-65536), F32), pltpu.bitcast(lax.shift_left(p, jnp.int32(16)), F32)


def _combine_kernel(h_ref, o1_ref, o2_ref, rt_ref, gf_ref, y_ref):
    w1, w2 = rt_ref[:, 2:3], rt_ref[:, 3:4]
    a_lo, a_hi = _unpack_f32(o1_ref[...])
    b_lo, b_hi = _unpack_f32(o2_ref[...])
    y_lo = h_ref[:, 0:HALF] + (w1 * a_lo + w2 * b_lo)
    y_hi = h_ref[:, HALF:D_MODEL] + (w1 * a_hi + w2 * b_hi)
    ms = (jnp.sum(y_lo * y_lo, axis=-1, keepdims=True) + jnp.sum(y_hi * y_hi, axis=-1, keepdims=True)) / D_MODEL
    inv = lax.rsqrt(ms + EPS)
    y_ref[:, 0:HALF] = (y_lo * inv) * gf_ref[:, 0:HALF]
    y_ref[:, HALF:D_MODEL] = (y_hi * inv) * gf_ref[:, HALF:D_MODEL]


def _combine(h, o12, route_t, gf, tile):
    n = h.shape[0]
    tile = min(tile, n)
    nt = n // tile
    return pl.pallas_call(
        _combine_kernel,
        grid=(nt,),
        in_specs=[pl.BlockSpec((tile, D_MODEL), lambda i: (i, 0)),
                  pl.BlockSpec((tile, HALF), lambda i: (i, 0)),
                  pl.BlockSpec((tile, HALF), lambda i: (i + nt, 0)),
                  pl.BlockSpec((tile, ROUTE_ROWS), lambda i: (i, 0)),
                  pl.BlockSpec((1, D_MODEL), lambda i: (0, 0))],
        out_specs=pl.BlockSpec((tile, D_MODEL), lambda i: (i, 0)),
        out_shape=jax.ShapeDtypeStruct((n, D_MODEL), F32),
        compiler_params=_cparams("parallel"),
        name="combine_final_norm",
    )(h, o12, o12, route_t, gf)


def _sparse_moe(xn2p, route, h, w_g, w_u, w_d, gf):
    n = h.shape[0]
    tm = EXPERT_ROW_TILE
    rows = 2 * n + N_EXPERTS * tm
    rank, cnt = _route_rank(route, 512)
    counts = cnt[:, 0]
    padded = (counts + tm - 1) // tm * tm
    e_idx = jnp.arange(N_EXPERTS, dtype=jnp.int32)
    starts = jnp.sum(jnp.where(e_idx[None, :] < e_idx[:, None], padded[None, :], 0), axis=1)
    ends = starts + padded
    ids = route[0:2].astype(jnp.int32)
    start_of = jnp.sum(jnp.where(ids[None] == e_idx[:, None, None], starts[:, None, None], 0), axis=0)
    pos = (start_of + rank[0:2]).reshape(2 * n)
    tile_start = jnp.arange(rows // tm, dtype=jnp.int32) * tm
    tile_expert = jnp.minimum(jnp.sum((tile_start[:, None] >= ends[None, :]).astype(jnp.int32), axis=1),
                              N_EXPERTS - 1)
    n_used = (ends[-1:] // tm).astype(jnp.int32)
    xs = _sc_scatter_pairs(xn2p, pos, rows)
    os_ = _expert_tiles(tile_expert, n_used, xs, w_g, w_u, w_d)
    o12 = _sc_gather_rows(os_, pos)
    return _combine(h, o12, route.T, gf, 512)


def _prep_in_weights(w_in):
    o = 0
    w_u = w_in[:, o:o + SSM_WIDTH]; o += SSM_WIDTH
    w_q = w_in[:, o:o + SWA_WIDTH]; o += SWA_WIDTH
    w_k = w_in[:, o:o + SWA_KV_WIDTH]; o += SWA_KV_WIDTH
    w_v = w_in[:, o:o + SWA_KV_WIDTH]; o += SWA_KV_WIDTH
    w_qm = w_in[:, o:o + MEM_WIDTH]; o += MEM_WIDTH
    w_g = w_in[:, o:]
    wq = (w_q * (SWA_HEAD_DIM ** -0.5)).reshape(D_MODEL, SWA_KV_HEADS, SWA_REP, SWA_HEAD_DIM)
    wq = wq.transpose(0, 2, 1, 3).reshape(D_MODEL, SWA_WIDTH)
    w_main = jnp.concatenate([w_u, wq, w_k, w_v, w_qm], axis=1).astype(BF16)
    return w_main, w_g.astype(BF16)


IN_SPLITS = (SSM_WIDTH, SWA_WIDTH, SWA_KV_WIDTH, SWA_KV_WIDTH, MEM_WIDTH)
IN_DTYPES = ((F32, BF16), (BF16,), (F32,), (F32,), (BF16,))


def kernel(x_prompt, x_sample, cache_swa_k, cache_swa_v, state_ssm_re, state_ssm_im, cache_mem_k, cache_mem_v, mem_prompt, norm1_g, w_in, lam_re, lam_im, log_dt, bm_re, bm_im, cm_re, cm_im, d_skip, w_glu, b_glu, sinks, rel_table, mem_norm_g, w_mem_kv, w_br_ssm, w_br_swa, w_br_mem, w_out, norm2_g, w_rg, b_rg, w_rexp, b_rexp, w_e_gate, w_e_up, w_e_down, final_norm_g):
    nb, t, _ = x_prompt.shape
    ns, ts, _ = x_sample.shape
    l = 0
    L = S5_CHUNK

    w_main, w_gates = _prep_in_weights(w_in[l])
    w_swa = (w_br_swa[l].reshape(SWA_KV_HEADS, SWA_REP, SWA_HEAD_DIM, D_MODEL).transpose(1, 0, 2, 3)
             .reshape(SWA_WIDTH, D_MODEL))
    pad_rows = ROUTER_ROWS - N_EXPERTS - N_EXPERT_GROUPS
    w_router = jnp.concatenate([w_rexp[l].T, w_rg[l].T, jnp.zeros((pad_rows, D_MODEL), F32)], axis=0).astype(BF16)
    b_router = jnp.concatenate([b_rexp[l], b_rg[l], jnp.zeros((pad_rows,), F32)]).reshape(ROUTER_ROWS, 1)
    mp = {
        'g1': norm1_g[l].reshape(1, D_MODEL), 'w_gates': w_gates, 'd_skip': d_skip[l].reshape(1, SSM_WIDTH),
        'w_glu': w_glu[l].astype(BF16), 'b_glu': b_glu[l].reshape(1, SSM_WIDTH),
        'w_br_ssm': w_br_ssm[l].astype(BF16), 'w_br_swa': w_swa.astype(BF16),
        'w_br_mem': w_br_mem[l].astype(BF16), 'w_out': w_out[l].astype(BF16),
        'g2': norm2_g[l].reshape(1, D_MODEL), 'w_router': w_router, 'b_router': b_router,
    }
    w_g, w_u, w_d = w_e_gate[l], w_e_up[l], w_e_down[l]
    gf = final_norm_g.reshape(1, D_MODEL)
    s5_w = _s5_weights(lam_re[l], lam_im[l], log_dt[l], bm_re[l], bm_im[l], cm_re[l], cm_im[l], L)

    bias_p = _rel_bias(rel_table, np.arange(WINDOW)[:, None] + WINDOW - np.arange(2 * WINDOW)[None, :])
    keys_s = WINDOW + 2 * ts
    bias_s = _rel_bias(rel_table, np.arange(ts)[:, None] + WINDOW - np.arange(keys_s)[None, :])
    bias_s = bias_s.reshape(SWA_HEADS * ts, keys_s)
    sink_rows = jnp.repeat(sinks[l].astype(F32), ts).reshape(SWA_HEADS * ts, 1)

    n = nb * t
    xp = x_prompt.reshape(n, D_MODEL)
    mk, mv = _norm_proj(mem_prompt.reshape(nb * MEM_TOKENS, D_MODEL), mem_norm_g[l].reshape(1, D_MODEL),
                        w_mem_kv[l].astype(BF16), (MEM_WIDTH, MEM_WIDTH), ((F32,), (F32,)), 512)
    u, ub, qz, k, v, qm = _norm_proj(xp, mp['g1'], w_main, IN_SPLITS, IN_DTYPES, 512)

    y_ssm, fin = _s5(ub, jnp.zeros((nb, N_CH_TILES * 2 * STATE_TILE), F32), s5_w, nb, t // L, L)
    p_re, p_im = _tiles_to_state(fin)

    o_swa = _swa_prompt(qz, k, v, bias_p, sinks[l].astype(F32), nb, t, 4)
    o_mem = _mem_prompt(qm, mk, mv, nb, t, 512)
    h, xn2p, route = _merge(xp, u, y_ssm, o_swa, o_mem, mp, 512)
    y_prompt = _sparse_moe(xn2p, route, h, w_g, w_u, w_d, gf).reshape(nb, t, D_MODEL)

    k4 = k.reshape(nb, t, SWA_KV_HEADS, SWA_HEAD_DIM)
    v4 = v.reshape(nb, t, SWA_KV_HEADS, SWA_HEAD_DIM)
    new_k_p, new_v_p = k4[:, -WINDOW:][None], v4[:, -WINDOW:][None]
    new_mk = mk.reshape(1, nb, MEM_TOKENS, MEM_HEADS, MEM_HEAD_DIM)
    new_mv = mv.reshape(1, nb, MEM_TOKENS, MEM_HEADS, MEM_HEAD_DIM)

    m = ns * ts
    xs = x_sample.reshape(m, D_MODEL)
    us, ubs, qzs, k_s, v_s, qms = _norm_proj(xs, mp['g1'], w_main, IN_SPLITS, IN_DTYPES, 256)
    ys_ssm, fins = _s5(ubs, _state_to_tiles(state_ssm_re[l], state_ssm_im[l]), s5_w, ns, ts // L, L)
    s_re, s_im = _tiles_to_state(fins)

    kk_all = jnp.concatenate([cache_swa_k[l].reshape(ns, WINDOW, SWA_KV_WIDTH).astype(F32),
                              k_s.reshape(ns, ts, SWA_KV_WIDTH)], axis=1)
    vv_all = jnp.concatenate([cache_swa_v[l].reshape(ns, WINDOW, SWA_KV_WIDTH).astype(F32),
                              v_s.reshape(ns, ts, SWA_KV_WIDTH)], axis=1)
    pad = jnp.zeros((ns, keys_s - WINDOW - ts, SWA_KV_WIDTH), F32)
    q5 = qzs.reshape(ns, ts, SWA_REP, SWA_KV_HEADS, SWA_HEAD_DIM)
    zq = jnp.zeros((ns, ts, SWA_REP, SWA_HEAD_DIM), BF16)
    q_rows = jnp.concatenate([jnp.concatenate([q5[:, :, :, 0], zq], axis=-1),
                              jnp.concatenate([zq, q5[:, :, :, 1]], axis=-1)], axis=2)
    q_rows = q_rows.transpose(0, 2, 1, 3).reshape(ns, SWA_HEADS * ts, LANES)
    o_dec = _swa_decode(q_rows, jnp.concatenate([kk_all, pad], axis=1), jnp.concatenate([vv_all, pad], axis=1),
                        bias_s, sink_rows, ts, 8)
    o_dec = o_dec.reshape(ns, SWA_KV_HEADS, SWA_REP, ts, SWA_KV_HEADS, SWA_HEAD_DIM)
    o_dec = jnp.stack([o_dec[:, g, :, :, g] for g in range(SWA_KV_HEADS)], axis=1)
    o_swa_s = o_dec.transpose(0, 3, 2, 1, 4).reshape(m, SWA_WIDTH).astype(BF16)

    o_mem_s = _mem_decode(qms.astype(F32).reshape(ns, ts, MEM_WIDTH), cache_mem_k, cache_mem_v, l, 4)
    o_mem_s = o_mem_s.reshape(m, MEM_WIDTH).astype(BF16)

    hs_, xn2ps, routes = _merge(xs, us, ys_ssm, o_swa_s, o_mem_s, mp, 256)
    y_sample = _moe(xn2ps, routes.T, w_g, w_u, w_d, hs_, gf, 1024).reshape(ns, ts, D_MODEL)

    new_k_s = kk_all[:, -WINDOW:].reshape(1, ns, WINDOW, SWA_KV_HEADS, SWA_HEAD_DIM).astype(cache_swa_k.dtype)
    new_v_s = vv_all[:, -WINDOW:].reshape(1, ns, WINDOW, SWA_KV_HEADS, SWA_HEAD_DIM).astype(cache_swa_v.dtype)

    return (y_prompt, y_sample,
            new_k_p, new_v_p, p_re[None], p_im[None], new_mk, new_mv,
            new_k_s, new_v_s, s_re[None].astype(state_ssm_re.dtype), s_im[None].astype(state_ssm_im.dtype))
```

```python
import functools
import math

import numpy as np
import jax
import jax.numpy as jnp
from jax import lax
from jax.experimental import pallas as pl
from jax.experimental.pallas import tpu as pltpu
from jax.experimental.pallas import tpu_sc as plsc

F32 = jnp.float32
BF16 = jnp.bfloat16

D_MODEL = 1024
SSM_WIDTH = 512
SSM_GROUP = 16
SSM_GROUPS = 32
SSM_STATE = 64
SWA_HEADS = 8
SWA_KV_HEADS = 2
SWA_REP = 4
SWA_HEAD_DIM = 64
SWA_WIDTH = 512
SWA_KV_WIDTH = 128
WINDOW = 128
REL_BUCKETS = 32
REL_MAX_DIST = 128
MEM_TOKENS = 256
MEM_HEADS = 4
MEM_HEAD_DIM = 128
MEM_WIDTH = 512
N_EXPERT_GROUPS = 4
EXPERTS_PER_GROUP = 8
N_EXPERTS = 32
D_EXPERT = 256
EPS = 1e-6
NEG_INF = -1e30

LANES = 128
GROUPS_PER_TILE = LANES // SSM_GROUP
N_CH_TILES = SSM_WIDTH // LANES
STATE_TILE = GROUPS_PER_TILE * SSM_STATE
VMEM_LIMIT = 56 * 1024 * 1024
S5_CHUNK = 8

_TRANS_B = (((1,), (1,)), ((), ()))


def _cparams(*sem):
    return pltpu.CompilerParams(dimension_semantics=sem, vmem_limit_bytes=VMEM_LIMIT)


def _rms(x, g):
    return (x * lax.rsqrt(jnp.mean(x * x, axis=-1, keepdims=True) + EPS)) * g


def _dot(a, b):
    return jnp.dot(a, b, preferred_element_type=F32)


def _norm_proj_kernel(x_ref, g_ref, w_ref, *out_refs, splits, dtypes):
    xb = _rms(x_ref[...], g_ref[...]).astype(BF16)
    off = 0
    outs = iter(out_refs)
    for width, dts in zip(splits, dtypes):
        r = _dot(xb, w_ref[:, off:off + width])
        for dt in dts:
            next(outs)[...] = r.astype(dt)
        off += width


def _norm_proj(x, g, w, splits, dtypes, tile):
    n, d = x.shape
    tile = min(tile, n)
    flat = [(wd, dt) for wd, dts in zip(splits, dtypes) for dt in dts]
    return pl.pallas_call(
        functools.partial(_norm_proj_kernel, splits=tuple(splits), dtypes=tuple(dtypes)),
        grid=(n // tile,),
        in_specs=[pl.BlockSpec((tile, d), lambda i: (i, 0)),
                  pl.BlockSpec((1, d), lambda i: (0, 0)),
                  pl.BlockSpec((d, sum(splits)), lambda i: (0, 0))],
        out_specs=[pl.BlockSpec((tile, wd), lambda i: (i, 0)) for wd, _ in flat],
        out_shape=[jax.ShapeDtypeStruct((n, wd), dt) for wd, dt in flat],
        compiler_params=_cparams("parallel"),
        name="norm_proj",
    )(x, g, w)


def _group_mask(rows_per_group, cols_per_group):
    r = np.arange(GROUPS_PER_TILE * rows_per_group)[:, None] // rows_per_group
    c = np.arange(GROUPS_PER_TILE * cols_per_group)[None, :] // cols_per_group
    return jnp.asarray(r == c, F32)


def _s5_weights(lam_re, lam_im, log_dt, bm_re, bm_im, cm_re, cm_im, L):
    hp = lax.Precision.HIGHEST
    nt, gt, P, H = N_CH_TILES, GROUPS_PER_TILE, SSM_STATE, SSM_GROUP
    lr, li = lam_re.astype(F32), lam_im.astype(F32)
    dt = jnp.exp(log_dt.astype(F32))[:, None]
    mag = jnp.exp(lr * dt)
    a_re = mag * jnp.cos(li * dt)
    a_im = mag * jnp.sin(li * dt)
    den = lr * lr + li * li
    f_re = ((a_re - 1.0) * lr + a_im * li) / den
    f_im = (a_im * lr - (a_re - 1.0) * li) / den
    br, bi = bm_re.astype(F32), bm_im.astype(F32)
    bb_re = f_re[..., None] * br - f_im[..., None] * bi
    bb_im = f_re[..., None] * bi + f_im[..., None] * br
    pr, pi = [jnp.ones_like(a_re)], [jnp.zeros_like(a_im)]
    for _ in range(L):
        pr.append(pr[-1] * a_re - pi[-1] * a_im)
        pi.append(pr[-2] * a_im + pi[-1] * a_re)
    ap_re, ap_im = jnp.stack(pr), jnp.stack(pi)
    cr, ci = cm_re.astype(F32), cm_im.astype(F32)
    ca_re = cr[None] * ap_re[:, :, None, :] - ci[None] * ap_im[:, :, None, :]
    ca_im = cr[None] * ap_im[:, :, None, :] + ci[None] * ap_re[:, :, None, :]

    def expand(w, rows_per_group, cols_per_group):
        w = w.reshape(w.shape[0], nt, gt * rows_per_group, cols_per_group)
        return jnp.tile(w, (1, 1, 1, gt)) * _group_mask(rows_per_group, cols_per_group)

    rev_re = jnp.stack([pr[L - 1 - s] for s in range(L)])
    rev_im = jnp.stack([pi[L - 1 - s] for s in range(L)])
    ws_re = rev_re[..., None] * bb_re[None] - rev_im[..., None] * bb_im[None]
    ws_im = rev_re[..., None] * bb_im[None] + rev_im[..., None] * bb_re[None]
    w_st = jnp.concatenate([expand(ws_re.transpose(0, 1, 3, 2), H, P),
                            expand(ws_im.transpose(0, 1, 3, 2), H, P)], axis=3)
    w_st = w_st.transpose(1, 0, 2, 3).reshape(nt, L * LANES, 2 * STATE_TILE).astype(BF16)

    w_out = jnp.concatenate([expand(ca_re[1:].transpose(0, 1, 3, 2), P, H),
                             expand(-ca_im[1:].transpose(0, 1, 3, 2), P, H)], axis=2)
    w_out = w_out.astype(BF16).transpose(1, 2, 0, 3).reshape(nt, 2 * STATE_TILE, L * LANES)

    k_lag = (jnp.einsum('tghp,gpk->tgkh', ca_re[:L], bb_re, precision=hp)
             - jnp.einsum('tghp,gpk->tgkh', ca_im[:L], bb_im, precision=hp))
    blocks = expand(k_lag, H, H).astype(BF16)
    zero = jnp.zeros_like(blocks[0])
    toep = jnp.concatenate(
        [jnp.concatenate([blocks[t - s] if t >= s else zero for t in range(L)], axis=2) for s in range(L)], axis=1)

    def per_tile(v):
        return v.reshape(nt, 1, STATE_TILE)

    return w_st, w_out, toep, per_tile(pr[L]), per_tile(pi[L])


def _to_chunks(u, nb, nc, L):
    return (u.reshape(nb, nc, L, N_CH_TILES, LANES).transpose(1, 0, 3, 2, 4)
            .reshape(nc * nb, N_CH_TILES * L * LANES))


def _from_chunks(y, nb, nc, L):
    return (y.reshape(nc, nb, N_CH_TILES, L, LANES).transpose(1, 0, 3, 2, 4)
            .reshape(nb * nc * L, SSM_WIDTH))


def _s5_kernel(x_ref, h0_ref, are_ref, aim_ref, ws_ref, t_ref, wo_ref, y_ref, fin_ref,
               hr_ref, hi_ref, d_ref, hs_ref, *, cb, nb):
    ci = pl.program_id(1)

    @pl.when(ci == 0)
    def _():
        hr_ref[...] = h0_ref[:, 0:STATE_TILE]
        hi_ref[...] = h0_ref[:, STATE_TILE:2 * STATE_TILE]

    x = x_ref[...]
    d_ref[...] = _dot(x, ws_ref[...])
    ar = jnp.broadcast_to(are_ref[...], (nb, STATE_TILE))
    ai = jnp.broadcast_to(aim_ref[...], (nb, STATE_TILE))

    def body(c, carry):
        hr, hi = carry
        r0 = pl.multiple_of(c * nb, nb)
        hs_ref[pl.ds(r0, nb), 0:STATE_TILE] = hr
        hs_ref[pl.ds(r0, nb), STATE_TILE:2 * STATE_TILE] = hi
        d = d_ref[pl.ds(r0, nb), :]
        return (ar * hr - ai * hi + d[:, 0:STATE_TILE],
                ar * hi + ai * hr + d[:, STATE_TILE:2 * STATE_TILE])

    hr, hi = lax.fori_loop(0, cb, body, (hr_ref[...], hi_ref[...]))
    hr_ref[...] = hr
    hi_ref[...] = hi
    y_ref[...] = _dot(x, t_ref[...]) + _dot(hs_ref[...].astype(BF16), wo_ref[...])

    @pl.when(ci == pl.num_programs(1) - 1)
    def _():
        fin_ref[:, 0:STATE_TILE] = hr
        fin_ref[:, STATE_TILE:2 * STATE_TILE] = hi


def _s5(ub, h0, weights, nb, nc, L, chunk_block):
    w_st, w_so, toep, a_re, a_im = weights
    xc = _to_chunks(ub, nb, nc, L)
    cb = min(chunk_block, nc)
    rows = cb * nb
    lk = L * LANES
    st2 = 2 * STATE_TILE
    tile_w = lambda shape: pl.BlockSpec((None,) + shape, lambda j, c: (j, 0, 0))
    y, fin = pl.pallas_call(
        functools.partial(_s5_kernel, cb=cb, nb=nb),
        grid=(N_CH_TILES, nc // cb),
        in_specs=[pl.BlockSpec((rows, lk), lambda j, c: (c, j)),
                  pl.BlockSpec((nb, st2), lambda j, c: (0, j)),
                  tile_w((1, STATE_TILE)), tile_w((1, STATE_TILE)),
                  tile_w((lk, st2)), tile_w((lk, lk)), tile_w((st2, lk))],
        out_specs=[pl.BlockSpec((rows, lk), lambda j, c: (c, j)),
                   pl.BlockSpec((nb, st2), lambda j, c: (0, j))],
        out_shape=[jax.ShapeDtypeStruct((nc * nb, N_CH_TILES * lk), F32),
                   jax.ShapeDtypeStruct((nb, N_CH_TILES * st2), F32)],
        scratch_shapes=[pltpu.VMEM((nb, STATE_TILE), F32), pltpu.VMEM((nb, STATE_TILE), F32),
                        pltpu.VMEM((rows, st2), F32), pltpu.VMEM((rows, st2), F32)],
        compiler_params=_cparams("parallel", "arbitrary"),
        name="s5_chunked_scan",
    )(xc, h0, a_re, a_im, w_st, toep, w_so)
    return _from_chunks(y, nb, nc, L), fin


def _state_to_tiles(h_re, h_im):
    nb = h_re.shape[0]
    r = h_re.astype(F32).reshape(nb, N_CH_TILES, STATE_TILE)
    i = h_im.astype(F32).reshape(nb, N_CH_TILES, STATE_TILE)
    return jnp.concatenate([r, i], axis=-1).reshape(nb, N_CH_TILES * 2 * STATE_TILE)


def _tiles_to_state(h):
    nb = h.shape[0]
    h = h.reshape(nb, N_CH_TILES, 2, GROUPS_PER_TILE, SSM_STATE)
    return (h[:, :, 0].reshape(nb, SSM_GROUPS, SSM_STATE), h[:, :, 1].reshape(nb, SSM_GROUPS, SSM_STATE))


def _t5_bucket(dist):
    n = np.maximum(dist, 0)
    max_exact = REL_BUCKETS // 2
    nf = np.maximum(n, 1).astype(np.float32)
    large = max_exact + (np.log(nf / np.float32(max_exact)) / np.float32(math.log(REL_MAX_DIST / max_exact))
                         * np.float32(REL_BUCKETS - max_exact)).astype(np.int32)
    large = np.minimum(large, REL_BUCKETS - 1)
    return np.where(n < max_exact, n, large)


def _rel_bias(rel_table, dist):
    bucket = _t5_bucket(dist)
    tab = rel_table.astype(F32)
    out = jnp.zeros((SWA_HEADS,) + dist.shape, F32)
    for b in range(REL_BUCKETS):
        sel = jnp.asarray(bucket == b)
        if bool((bucket == b).any()):
            out = jnp.where(sel[None], tab[b].reshape((SWA_HEADS,) + (1,) * dist.ndim), out)
    return out


def _softmax_sink(s, sink):
    m = jnp.maximum(jnp.max(s, axis=-1, keepdims=True), sink)
    e = jnp.exp(s - m)
    den = jnp.sum(e, axis=-1, keepdims=True) + jnp.exp(sink - m)
    return e * (1.0 / den)


def _swa_prompt_kernel(sink_ref, q_ref, kp_ref, kc_ref, vp_ref, vc_ref, bias_ref, o_ref, kk_ref, vv_ref, *, qblocks):
    step = pl.program_id(1)
    kk_ref[0:WINDOW, :] = kp_ref[...].astype(BF16)
    kk_ref[WINDOW:, :] = kc_ref[...].astype(BF16)
    vv_ref[0:WINDOW, :] = vp_ref[...].astype(BF16)
    vv_ref[WINDOW:, :] = vc_ref[...].astype(BF16)
    row = lax.broadcasted_iota(jnp.int32, (WINDOW, 2 * WINDOW), 0)
    col = lax.broadcasted_iota(jnp.int32, (WINDOW, 2 * WINDOW), 1)
    dist = row + WINDOW - col
    band = (dist >= 0) & (dist < WINDOW)
    lane = lax.broadcasted_iota(jnp.int32, (WINDOW, LANES), 1)
    low = lane < SWA_HEAD_DIM

    def block(j, carry):
        r0 = pl.multiple_of(j * WINDOW, WINDOW)
        kk = kk_ref[pl.ds(r0, 2 * WINDOW), :]
        vv = vv_ref[pl.ds(r0, 2 * WINDOW), :]
        valid = band & ((col >= WINDOW) | (step * qblocks + j > 0))
        for t in range(SWA_REP):
            q2 = q_ref[pl.ds(r0, WINDOW), t * LANES:(t + 1) * LANES]
            outs = []
            for half in range(SWA_KV_HEADS):
                h = t + SWA_REP * half
                qh = jnp.where(low if half == 0 else jnp.logical_not(low), q2, jnp.zeros_like(q2))
                s = lax.dot_general(qh, kk, _TRANS_B, preferred_element_type=F32)
                s = jnp.where(valid, s + bias_ref[h], NEG_INF)
                p = _softmax_sink(s, sink_ref[h]).astype(BF16)
                outs.append(_dot(p, vv))
            o_ref[pl.ds(r0, WINDOW), t * LANES:(t + 1) * LANES] = jnp.where(low, outs[0], outs[1]).astype(BF16)
        return carry

    lax.fori_loop(0, qblocks, block, 0)


def _swa_prompt(q, k, v, bias, sinks, nb, t, qblocks):
    nstep = t // (WINDOW * qblocks)
    rows = WINDOW * qblocks
    cur = lambda b, i: (b * nstep + i, 0)
    prev = lambda b, i: (b * nstep * qblocks + jnp.maximum(i * qblocks - 1, 0), 0)
    return pl.pallas_call(
        functools.partial(_swa_prompt_kernel, qblocks=qblocks),
        grid=(nb, nstep),
        in_specs=[pl.BlockSpec(memory_space=pltpu.SMEM),
                  pl.BlockSpec((rows, SWA_WIDTH), cur),
                  pl.BlockSpec((WINDOW, SWA_KV_WIDTH), prev),
                  pl.BlockSpec((rows, SWA_KV_WIDTH), cur),
                  pl.BlockSpec((WINDOW, SWA_KV_WIDTH), prev),
                  pl.BlockSpec((rows, SWA_KV_WIDTH), cur),
                  pl.BlockSpec((SWA_HEADS, WINDOW, 2 * WINDOW), lambda b, i: (0, 0, 0))],
        out_specs=pl.BlockSpec((rows, SWA_WIDTH), cur),
        out_shape=jax.ShapeDtypeStruct((nb * t, SWA_WIDTH), BF16),
        scratch_shapes=[pltpu.VMEM((rows + WINDOW, SWA_KV_WIDTH), BF16),
                        pltpu.VMEM((rows + WINDOW, SWA_KV_WIDTH), BF16)],
        compiler_params=_cparams("parallel", "parallel"),
        name="swa_prompt",
    )(sinks, q, k, k, v, v, bias)


def _swa_decode_kernel(q_ref, k_ref, v_ref, bias_ref, sink_ref, o_ref, *, seqs, tq):
    rows, keys = q_ref.shape[1], k_ref.shape[1]
    qi = lax.broadcasted_iota(jnp.int32, (rows, keys), 0) % tq
    col = lax.broadcasted_iota(jnp.int32, (rows, keys), 1)
    dist = qi + WINDOW - col
    valid = (dist >= 0) & (dist < WINDOW)
    bias = bias_ref[...]
    sink = sink_ref[...]
    for s_i in range(seqs):
        kk = k_ref[s_i].astype(BF16)
        s = lax.dot_general(q_ref[s_i], kk, _TRANS_B, preferred_element_type=F32)
        s = jnp.where(valid, s + bias, NEG_INF)
        p = _softmax_sink(s, sink).astype(BF16)
        o_ref[s_i] = _dot(p, v_ref[s_i].astype(BF16))


def _swa_decode(qz, k_all, v_all, bias, sink_rows, tq, seqs):
    nseq, rows, _ = qz.shape
    keys = k_all.shape[1]
    seqs = min(seqs, nseq)
    return pl.pallas_call(
        functools.partial(_swa_decode_kernel, seqs=seqs, tq=tq),
        grid=(nseq // seqs,),
        in_specs=[pl.BlockSpec((seqs, rows, LANES), lambda i: (i, 0, 0)),
                  pl.BlockSpec((seqs, keys, LANES), lambda i: (i, 0, 0)),
                  pl.BlockSpec((seqs, keys, LANES), lambda i: (i, 0, 0)),
                  pl.BlockSpec((rows, keys), lambda i: (0, 0)),
                  pl.BlockSpec((rows, 1), lambda i: (0, 0))],
        out_specs=pl.BlockSpec((seqs, rows, LANES), lambda i: (i, 0, 0)),
        out_shape=jax.ShapeDtypeStruct((nseq, rows, LANES), F32),
        compiler_params=_cparams("parallel"),
        name="swa_decode",
    )(qz, k_all, v_all, bias, sink_rows)


def _softmax(s):
    m = jnp.max(s, axis=-1, keepdims=True)
    e = jnp.exp(s - m)
    return e * (1.0 / jnp.sum(e, axis=-1, keepdims=True))


def _mem_prompt_kernel(q_ref, k_ref, v_ref, o_ref):
    scale = MEM_HEAD_DIM ** -0.5
    for h in range(MEM_HEADS):
        sl = slice(h * MEM_HEAD_DIM, (h + 1) * MEM_HEAD_DIM)
        s = lax.dot_general(q_ref[:, sl], k_ref[:, sl].astype(BF16), _TRANS_B, preferred_element_type=F32) * scale
        p = _softmax(s).astype(BF16)
        o_ref[:, sl] = _dot(p, v_ref[:, sl].astype(BF16)).astype(BF16)


def _mem_prompt(qm, mk, mv, nb, t, tile):
    tile = min(tile, t)
    nt = t // tile
    return pl.pallas_call(
        _mem_prompt_kernel,
        grid=(nb, nt),
        in_specs=[pl.BlockSpec((tile, MEM_WIDTH), lambda b, i: (b * nt + i, 0)),
                  pl.BlockSpec((MEM_TOKENS, MEM_WIDTH), lambda b, i: (b, 0)),
                  pl.BlockSpec((MEM_TOKENS, MEM_WIDTH), lambda b, i: (b, 0))],
        out_specs=pl.BlockSpec((tile, MEM_WIDTH), lambda b, i: (b * nt + i, 0)),
        out_shape=jax.ShapeDtypeStruct((nb * t, MEM_WIDTH), BF16),
        compiler_params=_cparams("parallel", "parallel"),
        name="mem_prompt",
    )(qm, mk, mv)


def _mem_decode_kernel(q_ref, k_ref, v_ref, o_ref, *, seqs):
    tq = q_ref.shape[1]
    rows, cols = MEM_HEADS * tq, MEM_TOKENS * MEM_HEADS
    k2 = k_ref.reshape(seqs, cols, MEM_HEAD_DIM)
    v2 = v_ref.reshape(seqs, cols, MEM_HEAD_DIM)
    scale = MEM_HEAD_DIM ** -0.5
    own = (lax.broadcasted_iota(jnp.int32, (rows, cols), 1) % MEM_HEADS
           == lax.broadcasted_iota(jnp.int32, (rows, cols), 0) // tq)
    for s_i in range(seqs):
        q = q_ref[s_i]
        qb = jnp.concatenate([q[:, h * MEM_HEAD_DIM:(h + 1) * MEM_HEAD_DIM] for h in range(MEM_HEADS)], axis=0)
        s = lax.dot_general(qb.astype(BF16), k2[s_i].astype(BF16), _TRANS_B, preferred_element_type=F32) * scale
        p = _softmax(jnp.where(own, s, NEG_INF)).astype(BF16)
        o = _dot(p, v2[s_i].astype(BF16))
        for h in range(MEM_HEADS):
            o_ref[s_i, :, h * MEM_HEAD_DIM:(h + 1) * MEM_HEAD_DIM] = o[h * tq:(h + 1) * tq, :]


def _mem_decode(q, k, v, layer, seqs):
    nseq, tq, _ = q.shape
    seqs = min(seqs, nseq)
    cache = pl.BlockSpec((None, seqs, MEM_TOKENS, MEM_HEADS, MEM_HEAD_DIM), lambda i: (layer, i, 0, 0, 0))
    return pl.pallas_call(
        functools.partial(_mem_decode_kernel, seqs=seqs),
        grid=(nseq // seqs,),
        in_specs=[pl.BlockSpec((seqs, tq, MEM_WIDTH), lambda i: (i, 0, 0)), cache, cache],
        out_specs=pl.BlockSpec((seqs, tq, MEM_WIDTH), lambda i: (i, 0, 0)),
        out_shape=jax.ShapeDtypeStruct((nseq, tq, MEM_WIDTH), F32),
        compiler_params=_cparams("parallel"),
        name="mem_decode",
    )(q, k, v)


ROUTER_ROWS = 40
ROUTE_ROWS = 8
HALF = D_MODEL // 2


def _pack_halves(xb):
    hi = pltpu.bitcast(xb[:, 0:HALF].astype(F32), jnp.int32)
    lo = pltpu.bitcast(xb[:, HALF:D_MODEL].astype(F32), jnp.int32)
    return hi | lax.shift_right_logical(lo, jnp.int32(16))


def _unpack_halves(p):
    hi = pltpu.bitcast(p & jnp.int32(-65536), F32).astype(BF16)
    lo = pltpu.bitcast(lax.shift_left(p, jnp.int32(16)), F32).astype(BF16)
    return hi, lo


def _merge_kernel(x_ref, u_ref, y_ref, os_ref, om_ref, g1_ref, wg_ref, dsk_ref, wglu_ref, bglu_ref,
                  wbs_ref, wbw_ref, wbm_ref, wout_ref, g2_ref, wr_ref, br_ref,
                  h_ref, xn2_ref, route_ref):
    x = x_ref[...]
    tt = x.shape[0]
    xb = _rms(x, g1_ref[...]).astype(BF16)
    z = jax.nn.gelu(y_ref[...] + dsk_ref[...] * u_ref[...])
    z = z * jax.nn.sigmoid(_dot(z.astype(BF16), wglu_ref[...]) + bglu_ref[...])
    merged = jax.nn.sigmoid(_dot(xb, wg_ref[:, 0:D_MODEL])) * _dot(z.astype(BF16), wbs_ref[...])
    merged = merged + jax.nn.sigmoid(_dot(xb, wg_ref[:, D_MODEL:2 * D_MODEL])) * _dot(os_ref[...], wbw_ref[...])
    merged = merged + jax.nn.sigmoid(_dot(xb, wg_ref[:, 2 * D_MODEL:3 * D_MODEL])) * _dot(om_ref[...], wbm_ref[...])
    h = x + _dot(merged.astype(BF16), wout_ref[...])
    h_ref[...] = h
    xn2 = _rms(h, g2_ref[...]).astype(BF16)
    xn2_ref[...] = _pack_halves(xn2)

    lt = lax.dot_general(wr_ref[...], xn2, _TRANS_B, preferred_element_type=F32) + br_ref[...]
    gl = lt[N_EXPERTS:N_EXPERTS + N_EXPERT_GROUPS]
    ge = jnp.exp(gl - jnp.max(gl, axis=0, keepdims=True))
    gp = ge / jnp.sum(ge, axis=0, keepdims=True)
    gw = jnp.max(gp, axis=0, keepdims=True)
    gidx = jnp.full((1, tt), N_EXPERT_GROUPS - 1, jnp.int32)
    for r in range(N_EXPERT_GROUPS - 2, -1, -1):
        gidx = jnp.where(gp[r:r + 1] == gw, r, gidx)
    ein = lt[(N_EXPERT_GROUPS - 1) * EXPERTS_PER_GROUP:N_EXPERTS]
    for r in range(N_EXPERT_GROUPS - 2, -1, -1):
        ein = jnp.where(gidx == r, lt[r * EXPERTS_PER_GROUP:(r + 1) * EXPERTS_PER_GROUP], ein)
    ee = jnp.exp(ein - jnp.max(ein, axis=0, keepdims=True))
    ep = ee / jnp.sum(ee, axis=0, keepdims=True)
    rowi = lax.broadcasted_iota(jnp.int32, (EXPERTS_PER_GROUP, tt), 0)
    p1 = jnp.max(ep, axis=0, keepdims=True)
    e1 = jnp.min(jnp.where(ep == p1, rowi, EXPERTS_PER_GROUP), axis=0, keepdims=True)
    ep2 = jnp.where(rowi == e1, -1.0, ep)
    p2 = jnp.max(ep2, axis=0, keepdims=True)
    e2 = jnp.min(jnp.where(ep2 == p2, rowi, EXPERTS_PER_GROUP), axis=0, keepdims=True)
    tot = p1 + p2
    w1 = p1 / tot * gw
    w2 = p2 / tot * gw
    id1 = (gidx * EXPERTS_PER_GROUP + e1).astype(F32)
    id2 = (gidx * EXPERTS_PER_GROUP + e2).astype(F32)
    route_ref[...] = jnp.concatenate([id1, id2, w1, w2, jnp.zeros((ROUTE_ROWS - 4, tt), F32)], axis=0)


def _merge(x, u, y, o_swa, o_mem, p, tile):
    n = x.shape[0]
    tile = min(tile, n)
    row = lambda i: (i, 0)
    const = lambda i: (0, 0)
    full = lambda a: pl.BlockSpec(a.shape, const, pipeline_mode=pl.Buffered(1))
    weights = [p['g1'], p['w_gates'], p['d_skip'], p['w_glu'], p['b_glu'], p['w_br_ssm'], p['w_br_swa'],
               p['w_br_mem'], p['w_out'], p['g2'], p['w_router'], p['b_router']]
    return pl.pallas_call(
        _merge_kernel,
        grid=(n // tile,),
        in_specs=[pl.BlockSpec((tile, D_MODEL), row), pl.BlockSpec((tile, SSM_WIDTH), row),
                  pl.BlockSpec((tile, SSM_WIDTH), row), pl.BlockSpec((tile, SWA_WIDTH), row),
                  pl.BlockSpec((tile, MEM_WIDTH), row)] + [full(w) for w in weights],
        out_specs=[pl.BlockSpec((tile, D_MODEL), row), pl.BlockSpec((tile, HALF), row),
                   pl.BlockSpec((ROUTE_ROWS, tile), lambda i: (0, i))],
        out_shape=[jax.ShapeDtypeStruct((n, D_MODEL), F32), jax.ShapeDtypeStruct((n, HALF), jnp.int32),
                   jax.ShapeDtypeStruct((ROUTE_ROWS, n), F32)],
        compiler_params=_cparams("parallel"),
        name="merge_router",
    )(x, u, y, o_swa, o_mem, *weights)


def _expert_mlp(xp, wg, wu, wd):
    hi, lo = _unpack_halves(xp)
    g = _dot(hi, wg[0:HALF, :]) + _dot(lo, wg[HALF:D_MODEL, :])
    u = _dot(hi, wu[0:HALF, :]) + _dot(lo, wu[HALF:D_MODEL, :])
    hh = jax.nn.silu(g) * u
    return _dot(hh.astype(BF16), wd[...])


def _moe_kernel(xn2_ref, rt_ref, wg_ref, wu_ref, wd_ref, h_ref, gf_ref, o_ref, acc_ref):
    e = pl.program_id(1)

    @pl.when(e == 0)
    def _():
        acc_ref[...] = jnp.zeros_like(acc_ref)

    o = _expert_mlp(xn2_ref[...], wg_ref[...].astype(BF16), wu_ref[...].astype(BF16), wd_ref[...].astype(BF16))
    ef = e.astype(F32)
    c = (jnp.where(rt_ref[:, 0:1] == ef, rt_ref[:, 2:3], 0.0)
         + jnp.where(rt_ref[:, 1:2] == ef, rt_ref[:, 3:4], 0.0))
    acc_ref[...] += c * o

    @pl.when(e == N_EXPERTS - 1)
    def _():
        o_ref[...] = _rms(h_ref[...] + acc_ref[...], gf_ref[...])


def _moe(xn2, route_t, w_g, w_u, w_d, h, gf, tile):
    n = h.shape[0]
    tile = min(tile, n)
    return pl.pallas_call(
        _moe_kernel,
        grid=(n // tile, N_EXPERTS),
        in_specs=[pl.BlockSpec((tile, HALF), lambda i, e: (i, 0)),
                  pl.BlockSpec((tile, ROUTE_ROWS), lambda i, e: (i, 0)),
                  pl.BlockSpec((None, D_MODEL, D_EXPERT), lambda i, e: (e, 0, 0)),
                  pl.BlockSpec((None, D_MODEL, D_EXPERT), lambda i, e: (e, 0, 0)),
                  pl.BlockSpec((None, D_EXPERT, D_MODEL), lambda i, e: (e, 0, 0)),
                  pl.BlockSpec((tile, D_MODEL), lambda i, e: (i, 0)),
                  pl.BlockSpec((1, D_MODEL), lambda i, e: (0, 0))],
        out_specs=pl.BlockSpec((tile, D_MODEL), lambda i, e: (i, 0)),
        out_shape=jax.ShapeDtypeStruct((n, D_MODEL), F32),
        scratch_shapes=[pltpu.VMEM((tile, D_MODEL), F32)],
        compiler_params=_cparams("parallel", "arbitrary"),
        name="moe_final_norm",
    )(xn2, route_t, w_g, w_u, w_d, h, gf)


EXPERT_ROW_TILE = 1024
SC_CORES = 2
SC_SUBCORES = 16
SC_WORKERS = SC_CORES * SC_SUBCORES
SC_SCATTER_ROWS = 64
SC_GATHER_ROWS = 64


def _route_rank_kernel(r_ref, rank_ref, cnt_ref, base_ref):
    i = pl.program_id(0)
    tt = r_ref.shape[1]

    @pl.when(i == 0)
    def _():
        base_ref[...] = jnp.zeros_like(base_ref)

    ids = r_ref[0:2, :].astype(jnp.int32)
    e_iota = lax.broadcasted_iota(jnp.int32, (N_EXPERTS, tt), 0)
    oh1 = jnp.where(e_iota == ids[0:1], 1.0, 0.0)
    oh2 = jnp.where(e_iota == ids[1:2], 1.0, 0.0)
    before = (lax.broadcasted_iota(jnp.int32, (tt, tt), 0) < lax.broadcasted_iota(jnp.int32, (tt, tt), 1))
    tri = jnp.where(before, 1.0, 0.0).astype(BF16)
    c1 = _dot(oh1.astype(BF16), tri)
    c2 = _dot(oh2.astype(BF16), tri)
    tot1 = jnp.sum(oh1, axis=1, keepdims=True)
    tot2 = jnp.sum(oh2, axis=1, keepdims=True)
    base = base_ref[:, 0:1]
    rank1 = jnp.sum(oh1 * (base + c1), axis=0, keepdims=True)
    rank2 = jnp.sum(oh2 * (base + tot1 + c2), axis=0, keepdims=True)
    rank_ref[...] = jnp.concatenate([rank1, rank2, jnp.zeros((ROUTE_ROWS - 2, tt), F32)], axis=0).astype(jnp.int32)
    new_base = jnp.broadcast_to(base + tot1 + tot2, base_ref.shape)
    base_ref[...] = new_base
    cnt_ref[...] = new_base.astype(jnp.int32)


def _route_rank(route, tile):
    n = route.shape[1]
    tile = min(tile, n)
    return pl.pallas_call(
        _route_rank_kernel,
        grid=(n // tile,),
        in_specs=[pl.BlockSpec((ROUTE_ROWS, tile), lambda i: (0, i))],
        out_specs=[pl.BlockSpec((ROUTE_ROWS, tile), lambda i: (0, i)),
                   pl.BlockSpec((N_EXPERTS, LANES), lambda i: (0, 0))],
        out_shape=[jax.ShapeDtypeStruct((ROUTE_ROWS, n), jnp.int32),
                   jax.ShapeDtypeStruct((N_EXPERTS, LANES), jnp.int32)],
        scratch_shapes=[pltpu.VMEM((N_EXPERTS, LANES), F32)],
        compiler_params=_cparams("arbitrary"),
        name="route_rank",
    )(route)


def _sc_mesh():
    return plsc.VectorSubcoreMesh(core_axis_name="core", subcore_axis_name="subcore")


def _sc_scatter_pairs(x, pos, rows_out):
    n, d = x.shape
    per_w = n // SC_WORKERS
    window = min(SC_SCATTER_ROWS, per_w)

    @pl.kernel(out_type=jax.ShapeDtypeStruct((rows_out, d), x.dtype), mesh=_sc_mesh(),
               scratch_types=[pltpu.VMEM((window,), jnp.int32), pltpu.VMEM((window,), jnp.int32),
                              pltpu.VMEM((window, d), x.dtype), pltpu.SemaphoreType.DMA])
    def scatter(x_hbm, p_hbm, o_hbm, i1_v, i2_v, rows_v, sem):
        wid = lax.axis_index("subcore") * SC_CORES + lax.axis_index("core")

        @pl.loop(0, per_w // window)
        def _(j):
            base = wid * per_w + j * window
            pltpu.sync_copy(p_hbm.at[pl.ds(base, window)], i1_v)
            pltpu.sync_copy(p_hbm.at[pl.ds(n + base, window)], i2_v)
            pltpu.sync_copy(x_hbm.at[pl.ds(base, window)], rows_v)
            pltpu.async_copy(rows_v, o_hbm.at[i1_v], sem).wait()
            pltpu.async_copy(rows_v, o_hbm.at[i2_v], sem).wait()

    return scatter(x, pos)


def _sc_gather_rows(table, idx):
    m = idx.shape[0]
    d = table.shape[1]
    per_w = m // SC_WORKERS
    window = min(SC_GATHER_ROWS, per_w)

    @pl.kernel(out_type=jax.ShapeDtypeStruct((m, d), table.dtype), mesh=_sc_mesh(),
               scratch_types=[pltpu.VMEM((window,), jnp.int32), pltpu.VMEM((window, d), table.dtype),
                              pltpu.SemaphoreType.DMA])
    def gather(t_hbm, i_hbm, o_hbm, i_v, rows_v, sem):
        wid = lax.axis_index("subcore") * SC_CORES + lax.axis_index("core")

        @pl.loop(0, per_w // window)
        def _(j):
            base = wid * per_w + j * window
            pltpu.sync_copy(i_hbm.at[pl.ds(base, window)], i_v)
            pltpu.async_copy(t_hbm.at[i_v], rows_v, sem).wait()
            pltpu.sync_copy(rows_v, o_hbm.at[pl.ds(base, window)])

    return gather(table, idx)


def _expert_tiles_kernel(te_ref, nu_ref, x_ref, wg_ref, wu_ref, wd_ref, o_ref, wg_s, wu_s, wd_s):
    i = pl.program_id(0)

    @pl.when(i < nu_ref[0])
    def _():
        @pl.when(jnp.logical_or(i == 0, te_ref[i] != te_ref[jnp.maximum(i - 1, 0)]))
        def _():
            wg_s[...] = wg_ref[...].astype(BF16)
            wu_s[...] = wu_ref[...].astype(BF16)
            wd_s[...] = wd_ref[...].astype(BF16)

        o_ref[...] = _pack_halves(_expert_mlp(x_ref[...], wg_s, wu_s, wd_s).astype(BF16))


def _expert_tiles(tile_expert, n_used, xs, w_g, w_u, w_d):
    rows = xs.shape[0]
    tm = EXPERT_ROW_TILE
    grid_spec = pltpu.PrefetchScalarGridSpec(
        num_scalar_prefetch=2,
        grid=(rows // tm,),
        in_specs=[pl.BlockSpec((tm, HALF), lambda i, te, nu: (i, 0)),
                  pl.BlockSpec((None, D_MODEL, D_EXPERT), lambda i, te, nu: (te[i], 0, 0)),
                  pl.BlockSpec((None, D_MODEL, D_EXPERT), lambda i, te, nu: (te[i], 0, 0)),
                  pl.BlockSpec((None, D_EXPERT, D_MODEL), lambda i, te, nu: (te[i], 0, 0))],
        out_specs=pl.BlockSpec((tm, HALF), lambda i, te, nu: (i, 0)),
        scratch_shapes=[pltpu.VMEM((D_MODEL, D_EXPERT), BF16), pltpu.VMEM((D_MODEL, D_EXPERT), BF16),
                        pltpu.VMEM((D_EXPERT, D_MODEL), BF16)],
    )
    return pl.pallas_call(
        _expert_tiles_kernel,
        grid_spec=grid_spec,
        out_shape=jax.ShapeDtypeStruct((rows, HALF), jnp.int32),
        compiler_params=_cparams("arbitrary"),
        name="expert_tiles",
    )(tile_expert, n_used, xs, w_g, w_u, w_d)


def _unpack_f32(p):
    return pltpu.bitcast(p & jnp.int32(-65536), F32), pltpu.bitcast(lax.shift_left(p, jnp.int32(16)), F32)


def _combine_kernel(h_ref, o1_ref, o2_ref, rt_ref, gf_ref, y_ref):
    w1, w2 = rt_ref[:, 2:3], rt_ref[:, 3:4]
    a_lo, a_hi = _unpack_f32(o1_ref[...])
    b_lo, b_hi = _unpack_f32(o2_ref[...])
    y_lo = h_ref[:, 0:HALF] + (w1 * a_lo + w2 * b_lo)
    y_hi = h_ref[:, HALF:D_MODEL] + (w1 * a_hi + w2 * b_hi)
    ms = (jnp.sum(y_lo * y_lo, axis=-1, keepdims=True) + jnp.sum(y_hi * y_hi, axis=-1, keepdims=True)) / D_MODEL
    inv = lax.rsqrt(ms + EPS)
    y_ref[:, 0:HALF] = (y_lo * inv) * gf_ref[:, 0:HALF]
    y_ref[:, HALF:D_MODEL] = (y_hi * inv) * gf_ref[:, HALF:D_MODEL]


def _combine(h, o12, route_t, gf, tile):
    n = h.shape[0]
    tile = min(tile, n)
    nt = n // tile
    return pl.pallas_call(
        _combine_kernel,
        grid=(nt,),
        in_specs=[pl.BlockSpec((tile, D_MODEL), lambda i: (i, 0)),
                  pl.BlockSpec((tile, HALF), lambda i: (i, 0)),
                  pl.BlockSpec((tile, HALF), lambda i: (i + nt, 0)),
                  pl.BlockSpec((tile, ROUTE_ROWS), lambda i: (i, 0)),
                  pl.BlockSpec((1, D_MODEL), lambda i: (0, 0))],
        out_specs=pl.BlockSpec((tile, D_MODEL), lambda i: (i, 0)),
        out_shape=jax.ShapeDtypeStruct((n, D_MODEL), F32),
        compiler_params=_cparams("parallel"),
        name="combine_final_norm",
    )(h, o12, o12, route_t, gf)


def _sparse_moe(xn2p, route, h, w_g, w_u, w_d, gf):
    n = h.shape[0]
    tm = EXPERT_ROW_TILE
    rows = 2 * n + N_EXPERTS * tm
    rank, cnt = _route_rank(route, 512)
    counts = cnt[:, 0]
    padded = (counts + tm - 1) // tm * tm
    e_idx = jnp.arange(N_EXPERTS, dtype=jnp.int32)
    starts = jnp.sum(jnp.where(e_idx[None, :] < e_idx[:, None], padded[None, :], 0), axis=1)
    ends = starts + padded
    ids = route[0:2].astype(jnp.int32)
    start_of = jnp.sum(jnp.where(ids[None] == e_idx[:, None, None], starts[:, None, None], 0), axis=0)
    pos = (start_of + rank[0:2]).reshape(2 * n)
    tile_start = jnp.arange(rows // tm, dtype=jnp.int32) * tm
    tile_expert = jnp.minimum(jnp.sum((tile_start[:, None] >= ends[None, :]).astype(jnp.int32), axis=1),
                              N_EXPERTS - 1)
    n_used = (ends[-1:] // tm).astype(jnp.int32)
    xs = _sc_scatter_pairs(xn2p, pos, rows)
    os_ = _expert_tiles(tile_expert, n_used, xs, w_g, w_u, w_d)
    o12 = _sc_gather_rows(os_, pos)
    return _combine(h, o12, route.T, gf, 512)


def _prep_in_weights(w_in):
    o = 0
    w_u = w_in[:, o:o + SSM_WIDTH]; o += SSM_WIDTH
    w_q = w_in[:, o:o + SWA_WIDTH]; o += SWA_WIDTH
    w_k = w_in[:, o:o + SWA_KV_WIDTH]; o += SWA_KV_WIDTH
    w_v = w_in[:, o:o + SWA_KV_WIDTH]; o += SWA_KV_WIDTH
    w_qm = w_in[:, o:o + MEM_WIDTH]; o += MEM_WIDTH
    w_g = w_in[:, o:]
    wq = (w_q * (SWA_HEAD_DIM ** -0.5)).reshape(D_MODEL, SWA_KV_HEADS, SWA_REP, SWA_HEAD_DIM)
    wq = wq.transpose(0, 2, 1, 3).reshape(D_MODEL, SWA_WIDTH)
    w_main = jnp.concatenate([w_u, wq, w_k, w_v, w_qm], axis=1).astype(BF16)
    return w_main, w_g.astype(BF16)


IN_SPLITS = (SSM_WIDTH, SWA_WIDTH, SWA_KV_WIDTH, SWA_KV_WIDTH, MEM_WIDTH)
IN_DTYPES = ((F32, BF16), (BF16,), (F32,), (F32,), (BF16,))


def kernel(x_prompt, x_sample, cache_swa_k, cache_swa_v, state_ssm_re, state_ssm_im, cache_mem_k, cache_mem_v, mem_prompt, norm1_g, w_in, lam_re, lam_im, log_dt, bm_re, bm_im, cm_re, cm_im, d_skip, w_glu, b_glu, sinks, rel_table, mem_norm_g, w_mem_kv, w_br_ssm, w_br_swa, w_br_mem, w_out, norm2_g, w_rg, b_rg, w_rexp, b_rexp, w_e_gate, w_e_up, w_e_down, final_norm_g):
    nb, t, _ = x_prompt.shape
    ns, ts, _ = x_sample.shape
    l = 0
    L = S5_CHUNK

    w_main, w_gates = _prep_in_weights(w_in[l])
    w_swa = (w_br_swa[l].reshape(SWA_KV_HEADS, SWA_REP, SWA_HEAD_DIM, D_MODEL).transpose(1, 0, 2, 3)
             .reshape(SWA_WIDTH, D_MODEL))
    pad_rows = ROUTER_ROWS - N_EXPERTS - N_EXPERT_GROUPS
    w_router = jnp.concatenate([w_rexp[l].T, w_rg[l].T, jnp.zeros((pad_rows, D_MODEL), F32)], axis=0).astype(BF16)
    b_router = jnp.concatenate([b_rexp[l], b_rg[l], jnp.zeros((pad_rows,), F32)]).reshape(ROUTER_ROWS, 1)
    mp = {
        'g1': norm1_g[l].reshape(1, D_MODEL), 'w_gates': w_gates, 'd_skip': d_skip[l].reshape(1, SSM_WIDTH),
        'w_glu': w_glu[l].astype(BF16), 'b_glu': b_glu[l].reshape(1, SSM_WIDTH),
        'w_br_ssm': w_br_ssm[l].astype(BF16), 'w_br_swa': w_swa.astype(BF16),
        'w_br_mem': w_br_mem[l].astype(BF16), 'w_out': w_out[l].astype(BF16),
        'g2': norm2_g[l].reshape(1, D_MODEL), 'w_router': w_router, 'b_router': b_router,
    }
    w_g, w_u, w_d = w_e_gate[l], w_e_up[l], w_e_down[l]
    gf = final_norm_g.reshape(1, D_MODEL)
    s5_w = _s5_weights(lam_re[l], lam_im[l], log_dt[l], bm_re[l], bm_im[l], cm_re[l], cm_im[l], L)

    bias_p = _rel_bias(rel_table, np.arange(WINDOW)[:, None] + WINDOW - np.arange(2 * WINDOW)[None, :])
    keys_s = WINDOW + 2 * ts
    bias_s = _rel_bias(rel_table, np.arange(ts)[:, None] + WINDOW - np.arange(keys_s)[None, :])
    bias_s = bias_s.reshape(SWA_HEADS * ts, keys_s)
    sink_rows = jnp.repeat(sinks[l].astype(F32), ts).reshape(SWA_HEADS * ts, 1)

    n = nb * t
    xp = x_prompt.reshape(n, D_MODEL)
    mk, mv = _norm_proj(mem_prompt.reshape(nb * MEM_TOKENS, D_MODEL), mem_norm_g[l].reshape(1, D_MODEL),
                        w_mem_kv[l].astype(BF16), (MEM_WIDTH, MEM_WIDTH), ((F32,), (F32,)), 512)
    u, ub, qz, k, v, qm = _norm_proj(xp, mp['g1'], w_main, IN_SPLITS, IN_DTYPES, 512)

    y_ssm, fin = _s5(ub, jnp.zeros((nb, N_CH_TILES * 2 * STATE_TILE), F32), s5_w, nb, t // L, L, 64)
    p_re, p_im = _tiles_to_state(fin)

    o_swa = _swa_prompt(qz, k, v, bias_p, sinks[l].astype(F32), nb, t, 4)
    o_mem = _mem_prompt(qm, mk, mv, nb, t, 512)
    h, xn2p, route = _merge(xp, u, y_ssm, o_swa, o_mem, mp, 512)
    y_prompt = _sparse_moe(xn2p, route, h, w_g, w_u, w_d, gf).reshape(nb, t, D_MODEL)

    k4 = k.reshape(nb, t, SWA_KV_HEADS, SWA_HEAD_DIM)
    v4 = v.reshape(nb, t, SWA_KV_HEADS, SWA_HEAD_DIM)
    new_k_p, new_v_p = k4[:, -WINDOW:][None], v4[:, -WINDOW:][None]
    new_mk = mk.reshape(1, nb, MEM_TOKENS, MEM_HEADS, MEM_HEAD_DIM)
    new_mv = mv.reshape(1, nb, MEM_TOKENS, MEM_HEADS, MEM_HEAD_DIM)

    m = ns * ts
    xs = x_sample.reshape(m, D_MODEL)
    us, ubs, qzs, k_s, v_s, qms = _norm_proj(xs, mp['g1'], w_main, IN_SPLITS, IN_DTYPES, 256)
    ys_ssm, fins = _s5(ubs, _state_to_tiles(state_ssm_re[l], state_ssm_im[l]), s5_w, ns, ts // L, L, 64)
    s_re, s_im = _tiles_to_state(fins)

    kk_all = jnp.concatenate([cache_swa_k[l].reshape(ns, WINDOW, SWA_KV_WIDTH).astype(F32),
                              k_s.reshape(ns, ts, SWA_KV_WIDTH)], axis=1)
    vv_all = jnp.concatenate([cache_swa_v[l].reshape(ns, WINDOW, SWA_KV_WIDTH).astype(F32),
                              v_s.reshape(ns, ts, SWA_KV_WIDTH)], axis=1)
    pad = jnp.zeros((ns, keys_s - WINDOW - ts, SWA_KV_WIDTH), F32)
    q5 = qzs.reshape(ns, ts, SWA_REP, SWA_KV_HEADS, SWA_HEAD_DIM)
    zq = jnp.zeros((ns, ts, SWA_REP, SWA_HEAD_DIM), BF16)
    q_rows = jnp.concatenate([jnp.concatenate([q5[:, :, :, 0], zq], axis=-1),
                              jnp.concatenate([zq, q5[:, :, :, 1]], axis=-1)], axis=2)
    q_rows = q_rows.transpose(0, 2, 1, 3).reshape(ns, SWA_HEADS * ts, LANES)
    o_dec = _swa_decode(q_rows, jnp.concatenate([kk_all, pad], axis=1), jnp.concatenate([vv_all, pad], axis=1),
                        bias_s, sink_rows, ts, 8)
    o_dec = o_dec.reshape(ns, SWA_KV_HEADS, SWA_REP, ts, SWA_KV_HEADS, SWA_HEAD_DIM)
    o_dec = jnp.stack([o_dec[:, g, :, :, g] for g in range(SWA_KV_HEADS)], axis=1)
    o_swa_s = o_dec.transpose(0, 3, 2, 1, 4).reshape(m, SWA_WIDTH).astype(BF16)

    o_mem_s = _mem_decode(qms.astype(F32).reshape(ns, ts, MEM_WIDTH), cache_mem_k, cache_mem_v, l, 4)
    o_mem_s = o_mem_s.reshape(m, MEM_WIDTH).astype(BF16)

    hs_, xn2ps, routes = _merge(xs, us, ys_ssm, o_swa_s, o_mem_s, mp, 256)
    y_sample = _moe(xn2ps, routes.T, w_g, w_u, w_d, hs_, gf, 1024).reshape(ns, ts, D_MODEL)

    new_k_s = kk_all[:, -WINDOW:].reshape(1, ns, WINDOW, SWA_KV_HEADS, SWA_HEAD_DIM).astype(cache_swa_k.dtype)
    new_v_s = vv_all[:, -WINDOW:].reshape(1, ns, WINDOW, SWA_KV_HEADS, SWA_HEAD_DIM).astype(cache_swa_v.dtype)

    return (y_prompt, y_sample,
            new_k_p, new_v_p, p_re[None], p_im[None], new_mk, new_mv,
            new_k_s, new_v_s, s_re[None].astype(state_ssm_re.dtype), s_im[None].astype(state_ssm_im.dtype))
```

```python
import functools
import math

import numpy as np
import jax
import jax.numpy as jnp
from jax import lax
from jax.experimental import pallas as pl
from jax.experimental.pallas import tpu as pltpu
from jax.experimental.pallas import tpu_sc as plsc

F32 = jnp.float32
BF16 = jnp.bfloat16

D_MODEL = 1024
SSM_WIDTH = 512
SSM_GROUP = 16
SSM_GROUPS = 32
SSM_STATE = 64
SWA_HEADS = 8
SWA_KV_HEADS = 2
SWA_REP = 4
SWA_HEAD_DIM = 64
SWA_WIDTH = 512
SWA_KV_WIDTH = 128
WINDOW = 128
REL_BUCKETS = 32
REL_MAX_DIST = 128
MEM_TOKENS = 256
MEM_HEADS = 4
MEM_HEAD_DIM = 128
MEM_WIDTH = 512
N_EXPERT_GROUPS = 4
EXPERTS_PER_GROUP = 8
N_EXPERTS = 32
D_EXPERT = 256
EPS = 1e-6
NEG_INF = -1e30

LANES = 128
GROUPS_PER_TILE = LANES // SSM_GROUP
N_CH_TILES = SSM_WIDTH // LANES
STATE_TILE = GROUPS_PER_TILE * SSM_STATE
VMEM_LIMIT = 56 * 1024 * 1024
S5_CHUNK = 8

_TRANS_B = (((1,), (1,)), ((), ()))


def _cparams(*sem):
    return pltpu.CompilerParams(dimension_semantics=sem, vmem_limit_bytes=VMEM_LIMIT)


def _rms(x, g):
    return (x * lax.rsqrt(jnp.mean(x * x, axis=-1, keepdims=True) + EPS)) * g


def _dot(a, b):
    return jnp.dot(a, b, preferred_element_type=F32)


def _norm_proj_kernel(x_ref, g_ref, w_ref, *out_refs, splits, dtypes):
    xb = _rms(x_ref[...], g_ref[...]).astype(BF16)
    off = 0
    outs = iter(out_refs)
    for width, dts in zip(splits, dtypes):
        r = _dot(xb, w_ref[:, off:off + width])
        for dt in dts:
            next(outs)[...] = r.astype(dt)
        off += width


def _norm_proj(x, g, w, splits, dtypes, tile):
    n, d = x.shape
    tile = min(tile, n)
    flat = [(wd, dt) for wd, dts in zip(splits, dtypes) for dt in dts]
    return pl.pallas_call(
        functools.partial(_norm_proj_kernel, splits=tuple(splits), dtypes=tuple(dtypes)),
        grid=(n // tile,),
        in_specs=[pl.BlockSpec((tile, d), lambda i: (i, 0)),
                  pl.BlockSpec((1, d), lambda i: (0, 0)),
                  pl.BlockSpec((d, sum(splits)), lambda i: (0, 0))],
        out_specs=[pl.BlockSpec((tile, wd), lambda i: (i, 0)) for wd, _ in flat],
        out_shape=[jax.ShapeDtypeStruct((n, wd), dt) for wd, dt in flat],
        compiler_params=_cparams("parallel"),
        name="norm_proj",
    )(x, g, w)


def _group_mask(rows_per_group, cols_per_group):
    r = np.arange(GROUPS_PER_TILE * rows_per_group)[:, None] // rows_per_group
    c = np.arange(GROUPS_PER_TILE * cols_per_group)[None, :] // cols_per_group
    return jnp.asarray(r == c, F32)


def _s5_weights(lam_re, lam_im, log_dt, bm_re, bm_im, cm_re, cm_im, L):
    hp = lax.Precision.HIGHEST
    nt, gt, P, H = N_CH_TILES, GROUPS_PER_TILE, SSM_STATE, SSM_GROUP
    lr, li = lam_re.astype(F32), lam_im.astype(F32)
    dt = jnp.exp(log_dt.astype(F32))[:, None]
    mag = jnp.exp(lr * dt)
    a_re = mag * jnp.cos(li * dt)
    a_im = mag * jnp.sin(li * dt)
    den = lr * lr + li * li
    f_re = ((a_re - 1.0) * lr + a_im * li) / den
    f_im = (a_im * lr - (a_re - 1.0) * li) / den
    br, bi = bm_re.astype(F32), bm_im.astype(F32)
    bb_re = f_re[..., None] * br - f_im[..., None] * bi
    bb_im = f_re[..., None] * bi + f_im[..., None] * br
    pr, pi = [jnp.ones_like(a_re)], [jnp.zeros_like(a_im)]
    for _ in range(L):
        pr.append(pr[-1] * a_re - pi[-1] * a_im)
        pi.append(pr[-2] * a_im + pi[-1] * a_re)
    ap_re, ap_im = jnp.stack(pr), jnp.stack(pi)
    cr, ci = cm_re.astype(F32), cm_im.astype(F32)
    ca_re = cr[None] * ap_re[:, :, None, :] - ci[None] * ap_im[:, :, None, :]
    ca_im = cr[None] * ap_im[:, :, None, :] + ci[None] * ap_re[:, :, None, :]

    def expand(w, rows_per_group, cols_per_group):
        w = w.reshape(w.shape[0], nt, gt * rows_per_group, cols_per_group)
        return jnp.tile(w, (1, 1, 1, gt)) * _group_mask(rows_per_group, cols_per_group)

    rev_re = jnp.stack([pr[L - 1 - s] for s in range(L)])
    rev_im = jnp.stack([pi[L - 1 - s] for s in range(L)])
    ws_re = rev_re[..., None] * bb_re[None] - rev_im[..., None] * bb_im[None]
    ws_im = rev_re[..., None] * bb_im[None] + rev_im[..., None] * bb_re[None]
    w_st = jnp.concatenate([expand(ws_re.transpose(0, 1, 3, 2), H, P),
                            expand(ws_im.transpose(0, 1, 3, 2), H, P)], axis=3)
    w_st = w_st.transpose(1, 0, 2, 3).reshape(nt, L * LANES, 2 * STATE_TILE).astype(BF16)

    w_out = jnp.concatenate([expand(ca_re[1:].transpose(0, 1, 3, 2), P, H),
                             expand(-ca_im[1:].transpose(0, 1, 3, 2), P, H)], axis=2)
    w_out = w_out.astype(BF16).transpose(1, 2, 0, 3).reshape(nt, 2 * STATE_TILE, L * LANES)

    k_lag = (jnp.einsum('tghp,gpk->tgkh', ca_re[:L], bb_re, precision=hp)
             - jnp.einsum('tghp,gpk->tgkh', ca_im[:L], bb_im, precision=hp))
    blocks = expand(k_lag, H, H).astype(BF16)
    zero = jnp.zeros_like(blocks[0])
    toep = jnp.concatenate(
        [jnp.concatenate([blocks[t - s] if t >= s else zero for t in range(L)], axis=2) for s in range(L)], axis=1)

    def per_tile(v):
        return v.reshape(nt, 1, STATE_TILE)

    return w_st, w_out, toep, per_tile(pr[L]), per_tile(pi[L])


def _to_chunks(u, nb, nc, L):
    return (u.reshape(nb, nc, L, N_CH_TILES, LANES).transpose(1, 0, 3, 2, 4)
            .reshape(nc * nb, N_CH_TILES * L * LANES))


def _from_chunks(y, nb, nc, L):
    return (y.reshape(nc, nb, N_CH_TILES, L, LANES).transpose(1, 0, 3, 2, 4)
            .reshape(nb * nc * L, SSM_WIDTH))


def _s5_kernel(x_ref, h0_ref, are_ref, aim_ref, ws_ref, t_ref, wo_ref, y_ref, fin_ref,
               hr_ref, hi_ref, d_ref, hs_ref, *, cb, nb):
    ci = pl.program_id(1)

    @pl.when(ci == 0)
    def _():
        hr_ref[...] = h0_ref[:, 0:STATE_TILE]
        hi_ref[...] = h0_ref[:, STATE_TILE:2 * STATE_TILE]

    x = x_ref[...]
    d_ref[...] = _dot(x, ws_ref[...])
    ar = jnp.broadcast_to(are_ref[...], (nb, STATE_TILE))
    ai = jnp.broadcast_to(aim_ref[...], (nb, STATE_TILE))

    def body(c, carry):
        hr, hi = carry
        r0 = pl.multiple_of(c * nb, nb)
        hs_ref[pl.ds(r0, nb), 0:STATE_TILE] = hr
        hs_ref[pl.ds(r0, nb), STATE_TILE:2 * STATE_TILE] = hi
        d = d_ref[pl.ds(r0, nb), :]
        return (ar * hr - ai * hi + d[:, 0:STATE_TILE],
                ar * hi + ai * hr + d[:, STATE_TILE:2 * STATE_TILE])

    hr, hi = lax.fori_loop(0, cb, body, (hr_ref[...], hi_ref[...]))
    hr_ref[...] = hr
    hi_ref[...] = hi
    y_ref[...] = _dot(x, t_ref[...]) + _dot(hs_ref[...].astype(BF16), wo_ref[...])

    @pl.when(ci == pl.num_programs(1) - 1)
    def _():
        fin_ref[:, 0:STATE_TILE] = hr
        fin_ref[:, STATE_TILE:2 * STATE_TILE] = hi


def _s5(ub, h0, weights, nb, nc, L, chunk_block):
    w_st, w_so, toep, a_re, a_im = weights
    xc = _to_chunks(ub, nb, nc, L)
    cb = min(chunk_block, nc)
    rows = cb * nb
    lk = L * LANES
    st2 = 2 * STATE_TILE
    tile_w = lambda shape: pl.BlockSpec((None,) + shape, lambda j, c: (j, 0, 0))
    y, fin = pl.pallas_call(
        functools.partial(_s5_kernel, cb=cb, nb=nb),
        grid=(N_CH_TILES, nc // cb),
        in_specs=[pl.BlockSpec((rows, lk), lambda j, c: (c, j)),
                  pl.BlockSpec((nb, st2), lambda j, c: (0, j)),
                  tile_w((1, STATE_TILE)), tile_w((1, STATE_TILE)),
                  tile_w((lk, st2)), tile_w((lk, lk)), tile_w((st2, lk))],
        out_specs=[pl.BlockSpec((rows, lk), lambda j, c: (c, j)),
                   pl.BlockSpec((nb, st2), lambda j, c: (0, j))],
        out_shape=[jax.ShapeDtypeStruct((nc * nb, N_CH_TILES * lk), F32),
                   jax.ShapeDtypeStruct((nb, N_CH_TILES * st2), F32)],
        scratch_shapes=[pltpu.VMEM((nb, STATE_TILE), F32), pltpu.VMEM((nb, STATE_TILE), F32),
                        pltpu.VMEM((rows, st2), F32), pltpu.VMEM((rows, st2), F32)],
        compiler_params=_cparams("parallel", "arbitrary"),
        name="s5_chunked_scan",
    )(xc, h0, a_re, a_im, w_st, toep, w_so)
    return _from_chunks(y, nb, nc, L), fin


def _state_to_tiles(h_re, h_im):
    nb = h_re.shape[0]
    r = h_re.astype(F32).reshape(nb, N_CH_TILES, STATE_TILE)
    i = h_im.astype(F32).reshape(nb, N_CH_TILES, STATE_TILE)
    return jnp.concatenate([r, i], axis=-1).reshape(nb, N_CH_TILES * 2 * STATE_TILE)


def _tiles_to_state(h):
    nb = h.shape[0]
    h = h.reshape(nb, N_CH_TILES, 2, GROUPS_PER_TILE, SSM_STATE)
    return (h[:, :, 0].reshape(nb, SSM_GROUPS, SSM_STATE), h[:, :, 1].reshape(nb, SSM_GROUPS, SSM_STATE))


def _t5_bucket(dist):
    n = np.maximum(dist, 0)
    max_exact = REL_BUCKETS // 2
    nf = np.maximum(n, 1).astype(np.float32)
    large = max_exact + (np.log(nf / np.float32(max_exact)) / np.float32(math.log(REL_MAX_DIST / max_exact))
                         * np.float32(REL_BUCKETS - max_exact)).astype(np.int32)
    large = np.minimum(large, REL_BUCKETS - 1)
    return np.where(n < max_exact, n, large)


def _rel_bias(rel_table, dist):
    bucket = _t5_bucket(dist)
    tab = rel_table.astype(F32)
    out = jnp.zeros((SWA_HEADS,) + dist.shape, F32)
    for b in range(REL_BUCKETS):
        sel = jnp.asarray(bucket == b)
        if bool((bucket == b).any()):
            out = jnp.where(sel[None], tab[b].reshape((SWA_HEADS,) + (1,) * dist.ndim), out)
    return out


def _softmax_sink(s, sink):
    m = jnp.maximum(jnp.max(s, axis=-1, keepdims=True), sink)
    e = jnp.exp(s - m)
    den = jnp.sum(e, axis=-1, keepdims=True) + jnp.exp(sink - m)
    return e * (1.0 / den)


def _swa_prompt_kernel(sink_ref, q_ref, kp_ref, kc_ref, vp_ref, vc_ref, bias_ref, o_ref, kk_ref, vv_ref, *, qblocks):
    step = pl.program_id(1)
    kk_ref[0:WINDOW, :] = kp_ref[...].astype(BF16)
    kk_ref[WINDOW:, :] = kc_ref[...].astype(BF16)
    vv_ref[0:WINDOW, :] = vp_ref[...].astype(BF16)
    vv_ref[WINDOW:, :] = vc_ref[...].astype(BF16)
    row = lax.broadcasted_iota(jnp.int32, (WINDOW, 2 * WINDOW), 0)
    col = lax.broadcasted_iota(jnp.int32, (WINDOW, 2 * WINDOW), 1)
    dist = row + WINDOW - col
    band = (dist >= 0) & (dist < WINDOW)
    lane = lax.broadcasted_iota(jnp.int32, (WINDOW, LANES), 1)
    low = lane < SWA_HEAD_DIM

    def block(j, carry):
        r0 = pl.multiple_of(j * WINDOW, WINDOW)
        kk = kk_ref[pl.ds(r0, 2 * WINDOW), :]
        vv = vv_ref[pl.ds(r0, 2 * WINDOW), :]
        valid = band & ((col >= WINDOW) | (step * qblocks + j > 0))
        for t in range(SWA_REP):
            q2 = q_ref[pl.ds(r0, WINDOW), t * LANES:(t + 1) * LANES]
            outs = []
            for half in range(SWA_KV_HEADS):
                h = t + SWA_REP * half
                qh = jnp.where(low if half == 0 else jnp.logical_not(low), q2, jnp.zeros_like(q2))
                s = lax.dot_general(qh, kk, _TRANS_B, preferred_element_type=F32)
                s = jnp.where(valid, s + bias_ref[h], NEG_INF)
                sink = sink_ref[h]
                m = jnp.maximum(jnp.max(s, axis=-1, keepdims=True), sink)
                e = jnp.exp(s - m)
                den = jnp.sum(e, axis=-1, keepdims=True) + jnp.exp(sink - m)
                outs.append(_dot(e.astype(BF16), vv) * (1.0 / den))
            o_ref[pl.ds(r0, WINDOW), t * LANES:(t + 1) * LANES] = jnp.where(low, outs[0], outs[1]).astype(BF16)
        return carry

    lax.fori_loop(0, qblocks, block, 0)


def _swa_prompt(q, k, v, bias, sinks, nb, t, qblocks):
    nstep = t // (WINDOW * qblocks)
    rows = WINDOW * qblocks
    cur = lambda b, i: (b * nstep + i, 0)
    prev = lambda b, i: (b * nstep * qblocks + jnp.maximum(i * qblocks - 1, 0), 0)
    return pl.pallas_call(
        functools.partial(_swa_prompt_kernel, qblocks=qblocks),
        grid=(nb, nstep),
        in_specs=[pl.BlockSpec(memory_space=pltpu.SMEM),
                  pl.BlockSpec((rows, SWA_WIDTH), cur),
                  pl.BlockSpec((WINDOW, SWA_KV_WIDTH), prev),
                  pl.BlockSpec((rows, SWA_KV_WIDTH), cur),
                  pl.BlockSpec((WINDOW, SWA_KV_WIDTH), prev),
                  pl.BlockSpec((rows, SWA_KV_WIDTH), cur),
                  pl.BlockSpec((SWA_HEADS, WINDOW, 2 * WINDOW), lambda b, i: (0, 0, 0))],
        out_specs=pl.BlockSpec((rows, SWA_WIDTH), cur),
        out_shape=jax.ShapeDtypeStruct((nb * t, SWA_WIDTH), BF16),
        scratch_shapes=[pltpu.VMEM((rows + WINDOW, SWA_KV_WIDTH), BF16),
                        pltpu.VMEM((rows + WINDOW, SWA_KV_WIDTH), BF16)],
        compiler_params=_cparams("parallel", "parallel"),
        name="swa_prompt",
    )(sinks, q, k, k, v, v, bias)


def _swa_decode_kernel(q_ref, k_ref, v_ref, bias_ref, sink_ref, o_ref, *, seqs, tq):
    rows, keys = q_ref.shape[1], k_ref.shape[1]
    qi = lax.broadcasted_iota(jnp.int32, (rows, keys), 0) % tq
    col = lax.broadcasted_iota(jnp.int32, (rows, keys), 1)
    dist = qi + WINDOW - col
    valid = (dist >= 0) & (dist < WINDOW)
    bias = bias_ref[...]
    sink = sink_ref[...]
    for s_i in range(seqs):
        kk = k_ref[s_i].astype(BF16)
        s = lax.dot_general(q_ref[s_i], kk, _TRANS_B, preferred_element_type=F32)
        s = jnp.where(valid, s + bias, NEG_INF)
        p = _softmax_sink(s, sink).astype(BF16)
        o_ref[s_i] = _dot(p, v_ref[s_i].astype(BF16))


def _swa_decode(qz, k_all, v_all, bias, sink_rows, tq, seqs):
    nseq, rows, _ = qz.shape
    keys = k_all.shape[1]
    seqs = min(seqs, nseq)
    return pl.pallas_call(
        functools.partial(_swa_decode_kernel, seqs=seqs, tq=tq),
        grid=(nseq // seqs,),
        in_specs=[pl.BlockSpec((seqs, rows, LANES), lambda i: (i, 0, 0)),
                  pl.BlockSpec((seqs, keys, LANES), lambda i: (i, 0, 0)),
                  pl.BlockSpec((seqs, keys, LANES), lambda i: (i, 0, 0)),
                  pl.BlockSpec((rows, keys), lambda i: (0, 0)),
                  pl.BlockSpec((rows, 1), lambda i: (0, 0))],
        out_specs=pl.BlockSpec((seqs, rows, LANES), lambda i: (i, 0, 0)),
        out_shape=jax.ShapeDtypeStruct((nseq, rows, LANES), F32),
        compiler_params=_cparams("parallel"),
        name="swa_decode",
    )(qz, k_all, v_all, bias, sink_rows)


def _softmax(s):
    m = jnp.max(s, axis=-1, keepdims=True)
    e = jnp.exp(s - m)
    return e * (1.0 / jnp.sum(e, axis=-1, keepdims=True))


def _mem_prompt_kernel(q_ref, k_ref, v_ref, o_ref, s_ref, p_ref):
    scale = MEM_HEAD_DIM ** -0.5
    heads = [slice(h * MEM_HEAD_DIM, (h + 1) * MEM_HEAD_DIM) for h in range(MEM_HEADS)]
    for h, sl in enumerate(heads):
        s_ref[h] = lax.dot_general(q_ref[:, sl], k_ref[:, sl].astype(BF16), _TRANS_B, preferred_element_type=F32)
    s = s_ref[...] * scale
    e = jnp.exp(s - jnp.max(s, axis=-1, keepdims=True))
    p_ref[...] = e.astype(BF16)
    inv = 1.0 / jnp.sum(e, axis=-1, keepdims=True)
    for h, sl in enumerate(heads):
        o_ref[:, sl] = (_dot(p_ref[h], v_ref[:, sl].astype(BF16)) * inv[h]).astype(BF16)


def _mem_prompt(qm, mk, mv, nb, t, tile):
    tile = min(tile, t)
    nt = t // tile
    return pl.pallas_call(
        _mem_prompt_kernel,
        grid=(nb, nt),
        in_specs=[pl.BlockSpec((tile, MEM_WIDTH), lambda b, i: (b * nt + i, 0)),
                  pl.BlockSpec((MEM_TOKENS, MEM_WIDTH), lambda b, i: (b, 0)),
                  pl.BlockSpec((MEM_TOKENS, MEM_WIDTH), lambda b, i: (b, 0))],
        out_specs=pl.BlockSpec((tile, MEM_WIDTH), lambda b, i: (b * nt + i, 0)),
        out_shape=jax.ShapeDtypeStruct((nb * t, MEM_WIDTH), BF16),
        scratch_shapes=[pltpu.VMEM((MEM_HEADS, tile, MEM_TOKENS), F32), pltpu.VMEM((MEM_HEADS, tile, MEM_TOKENS), BF16)],
        compiler_params=_cparams("parallel", "parallel"),
        name="mem_prompt",
    )(qm, mk, mv)


def _mem_decode_kernel(q_ref, k_ref, v_ref, o_ref, *, seqs):
    tq = q_ref.shape[1]
    rows, cols = MEM_HEADS * tq, MEM_TOKENS * MEM_HEADS
    k2 = k_ref.reshape(seqs, cols, MEM_HEAD_DIM)
    v2 = v_ref.reshape(seqs, cols, MEM_HEAD_DIM)
    scale = MEM_HEAD_DIM ** -0.5
    own = (lax.broadcasted_iota(jnp.int32, (rows, cols), 1) % MEM_HEADS
           == lax.broadcasted_iota(jnp.int32, (rows, cols), 0) // tq)
    for s_i in range(seqs):
        q = q_ref[s_i]
        qb = jnp.concatenate([q[:, h * MEM_HEAD_DIM:(h + 1) * MEM_HEAD_DIM] for h in range(MEM_HEADS)], axis=0)
        s = lax.dot_general(qb.astype(BF16), k2[s_i].astype(BF16), _TRANS_B, preferred_element_type=F32) * scale
        p = _softmax(jnp.where(own, s, NEG_INF)).astype(BF16)
        o = _dot(p, v2[s_i].astype(BF16))
        for h in range(MEM_HEADS):
            o_ref[s_i, :, h * MEM_HEAD_DIM:(h + 1) * MEM_HEAD_DIM] = o[h * tq:(h + 1) * tq, :]


def _mem_decode(q, k, v, layer, seqs):
    nseq, tq, _ = q.shape
    seqs = min(seqs, nseq)
    cache = pl.BlockSpec((None, seqs, MEM_TOKENS, MEM_HEADS, MEM_HEAD_DIM), lambda i: (layer, i, 0, 0, 0))
    return pl.pallas_call(
        functools.partial(_mem_decode_kernel, seqs=seqs),
        grid=(nseq // seqs,),
        in_specs=[pl.BlockSpec((seqs, tq, MEM_WIDTH), lambda i: (i, 0, 0)), cache, cache],
        out_specs=pl.BlockSpec((seqs, tq, MEM_WIDTH), lambda i: (i, 0, 0)),
        out_shape=jax.ShapeDtypeStruct((nseq, tq, MEM_WIDTH), F32),
        compiler_params=_cparams("parallel"),
        name="mem_decode",
    )(q, k, v)


ROUTER_ROWS = 40
ROUTE_ROWS = 8
HALF = D_MODEL // 2


def _pack_halves(xb):
    hi = pltpu.bitcast(xb[:, 0:HALF].astype(F32), jnp.int32)
    lo = pltpu.bitcast(xb[:, HALF:D_MODEL].astype(F32), jnp.int32)
    return hi | lax.shift_right_logical(lo, jnp.int32(16))


def _unpack_halves(p):
    hi = pltpu.bitcast(p & jnp.int32(-65536), F32).astype(BF16)
    lo = pltpu.bitcast(lax.shift_left(p, jnp.int32(16)), F32).astype(BF16)
    return hi, lo


def _merge_kernel(x_ref, u_ref, y_ref, os_ref, om_ref, g1_ref, wg_ref, dsk_ref, wglu_ref, bglu_ref,
                  wbs_ref, wbw_ref, wbm_ref, wout_ref, g2_ref, wr_ref, br_ref,
                  h_ref, xn2_ref, route_ref):
    x = x_ref[...]
    tt = x.shape[0]
    xb = _rms(x, g1_ref[...]).astype(BF16)
    z = jax.nn.gelu(y_ref[...] + dsk_ref[...] * u_ref[...])
    z = z * jax.nn.sigmoid(_dot(z.astype(BF16), wglu_ref[...]) + bglu_ref[...])
    merged = jax.nn.sigmoid(_dot(xb, wg_ref[:, 0:D_MODEL])) * _dot(z.astype(BF16), wbs_ref[...])
    merged = merged + jax.nn.sigmoid(_dot(xb, wg_ref[:, D_MODEL:2 * D_MODEL])) * _dot(os_ref[...], wbw_ref[...])
    merged = merged + jax.nn.sigmoid(_dot(xb, wg_ref[:, 2 * D_MODEL:3 * D_MODEL])) * _dot(om_ref[...], wbm_ref[...])
    h = x + _dot(merged.astype(BF16), wout_ref[...])
    h_ref[...] = h
    xn2 = _rms(h, g2_ref[...]).astype(BF16)
    xn2_ref[...] = _pack_halves(xn2)

    lt = lax.dot_general(wr_ref[...], xn2, _TRANS_B, preferred_element_type=F32) + br_ref[...]
    gl = lt[N_EXPERTS:N_EXPERTS + N_EXPERT_GROUPS]
    ge = jnp.exp(gl - jnp.max(gl, axis=0, keepdims=True))
    gp = ge / jnp.sum(ge, axis=0, keepdims=True)
    gw = jnp.max(gp, axis=0, keepdims=True)
    gidx = jnp.full((1, tt), N_EXPERT_GROUPS - 1, jnp.int32)
    for r in range(N_EXPERT_GROUPS - 2, -1, -1):
        gidx = jnp.where(gp[r:r + 1] == gw, r, gidx)
    ein = lt[(N_EXPERT_GROUPS - 1) * EXPERTS_PER_GROUP:N_EXPERTS]
    for r in range(N_EXPERT_GROUPS - 2, -1, -1):
        ein = jnp.where(gidx == r, lt[r * EXPERTS_PER_GROUP:(r + 1) * EXPERTS_PER_GROUP], ein)
    ee = jnp.exp(ein - jnp.max(ein, axis=0, keepdims=True))
    ep = ee / jnp.sum(ee, axis=0, keepdims=True)
    rowi = lax.broadcasted_iota(jnp.int32, (EXPERTS_PER_GROUP, tt), 0)
    p1 = jnp.max(ep, axis=0, keepdims=True)
    e1 = jnp.min(jnp.where(ep == p1, rowi, EXPERTS_PER_GROUP), axis=0, keepdims=True)
    ep2 = jnp.where(rowi == e1, -1.0, ep)
    p2 = jnp.max(ep2, axis=0, keepdims=True)
    e2 = jnp.min(jnp.where(ep2 == p2, rowi, EXPERTS_PER_GROUP), axis=0, keepdims=True)
    tot = p1 + p2
    w1 = p1 / tot * gw
    w2 = p2 / tot * gw
    id1 = (gidx * EXPERTS_PER_GROUP + e1).astype(F32)
    id2 = (gidx * EXPERTS_PER_GROUP + e2).astype(F32)
    route_ref[...] = jnp.concatenate([id1, id2, w1, w2, jnp.zeros((ROUTE_ROWS - 4, tt), F32)], axis=0)


def _merge(x, u, y, o_swa, o_mem, p, tile):
    n = x.shape[0]
    tile = min(tile, n)
    row = lambda i: (i, 0)
    const = lambda i: (0, 0)
    full = lambda a: pl.BlockSpec(a.shape, const, pipeline_mode=pl.Buffered(1))
    weights = [p['g1'], p['w_gates'], p['d_skip'], p['w_glu'], p['b_glu'], p['w_br_ssm'], p['w_br_swa'],
               p['w_br_mem'], p['w_out'], p['g2'], p['w_router'], p['b_router']]
    return pl.pallas_call(
        _merge_kernel,
        grid=(n // tile,),
        in_specs=[pl.BlockSpec((tile, D_MODEL), row), pl.BlockSpec((tile, SSM_WIDTH), row),
                  pl.BlockSpec((tile, SSM_WIDTH), row), pl.BlockSpec((tile, SWA_WIDTH), row),
                  pl.BlockSpec((tile, MEM_WIDTH), row)] + [full(w) for w in weights],
        out_specs=[pl.BlockSpec((tile, D_MODEL), row), pl.BlockSpec((tile, HALF), row),
                   pl.BlockSpec((ROUTE_ROWS, tile), lambda i: (0, i))],
        out_shape=[jax.ShapeDtypeStruct((n, D_MODEL), F32), jax.ShapeDtypeStruct((n, HALF), jnp.int32),
                   jax.ShapeDtypeStruct((ROUTE_ROWS, n), F32)],
        compiler_params=_cparams("parallel"),
        name="merge_router",
    )(x, u, y, o_swa, o_mem, *weights)


def _expert_mlp(xp, wg, wu, wd):
    hi, lo = _unpack_halves(xp)
    g = _dot(hi, wg[0:HALF, :]) + _dot(lo, wg[HALF:D_MODEL, :])
    u = _dot(hi, wu[0:HALF, :]) + _dot(lo, wu[HALF:D_MODEL, :])
    hh = jax.nn.silu(g) * u
    return _dot(hh.astype(BF16), wd[...])


def _moe_kernel(xn2_ref, rt_ref, wg_ref, wu_ref, wd_ref, h_ref, gf_ref, o_ref, acc_ref):
    e = pl.program_id(1)

    @pl.when(e == 0)
    def _():
        acc_ref[...] = jnp.zeros_like(acc_ref)

    o = _expert_mlp(xn2_ref[...], wg_ref[...].astype(BF16), wu_ref[...].astype(BF16), wd_ref[...].astype(BF16))
    ef = e.astype(F32)
    c = (jnp.where(rt_ref[:, 0:1] == ef, rt_ref[:, 2:3], 0.0)
         + jnp.where(rt_ref[:, 1:2] == ef, rt_ref[:, 3:4], 0.0))
    acc_ref[...] += c * o

    @pl.when(e == N_EXPERTS - 1)
    def _():
        o_ref[...] = _rms(h_ref[...] + acc_ref[...], gf_ref[...])


def _moe(xn2, route_t, w_g, w_u, w_d, h, gf, tile):
    n = h.shape[0]
    tile = min(tile, n)
    return pl.pallas_call(
        _moe_kernel,
        grid=(n // tile, N_EXPERTS),
        in_specs=[pl.BlockSpec((tile, HALF), lambda i, e: (i, 0)),
                  pl.BlockSpec((tile, ROUTE_ROWS), lambda i, e: (i, 0)),
                  pl.BlockSpec((None, D_MODEL, D_EXPERT), lambda i, e: (e, 0, 0)),
                  pl.BlockSpec((None, D_MODEL, D_EXPERT), lambda i, e: (e, 0, 0)),
                  pl.BlockSpec((None, D_EXPERT, D_MODEL), lambda i, e: (e, 0, 0)),
                  pl.BlockSpec((tile, D_MODEL), lambda i, e: (i, 0)),
                  pl.BlockSpec((1, D_MODEL), lambda i, e: (0, 0))],
        out_specs=pl.BlockSpec((tile, D_MODEL), lambda i, e: (i, 0)),
        out_shape=jax.ShapeDtypeStruct((n, D_MODEL), F32),
        scratch_shapes=[pltpu.VMEM((tile, D_MODEL), F32)],
        compiler_params=_cparams("parallel", "arbitrary"),
        name="moe_final_norm",
    )(xn2, route_t, w_g, w_u, w_d, h, gf)


EXPERT_ROW_TILE = 1024
SC_CORES = 2
SC_SUBCORES = 16
SC_WORKERS = SC_CORES * SC_SUBCORES
SC_SCATTER_ROWS = 64
SC_GATHER_ROWS = 64


def _route_rank_kernel(r_ref, rank_ref, cnt_ref, base_ref):
    i = pl.program_id(0)
    tt = r_ref.shape[1]

    @pl.when(i == 0)
    def _():
        base_ref[...] = jnp.zeros_like(base_ref)

    ids = r_ref[0:2, :].astype(jnp.int32)
    e_iota = lax.broadcasted_iota(jnp.int32, (N_EXPERTS, tt), 0)
    oh1 = jnp.where(e_iota == ids[0:1], 1.0, 0.0)
    oh2 = jnp.where(e_iota == ids[1:2], 1.0, 0.0)
    before = (lax.broadcasted_iota(jnp.int32, (tt, tt), 0) < lax.broadcasted_iota(jnp.int32, (tt, tt), 1))
    tri = jnp.where(before, 1.0, 0.0).astype(BF16)
    c1 = _dot(oh1.astype(BF16), tri)
    c2 = _dot(oh2.astype(BF16), tri)
    tot1 = jnp.sum(oh1, axis=1, keepdims=True)
    tot2 = jnp.sum(oh2, axis=1, keepdims=True)
    base = base_ref[:, 0:1]
    rank1 = jnp.sum(oh1 * (base + c1), axis=0, keepdims=True)
    rank2 = jnp.sum(oh2 * (base + tot1 + c2), axis=0, keepdims=True)
    rank_ref[...] = jnp.concatenate([rank1, rank2, jnp.zeros((ROUTE_ROWS - 2, tt), F32)], axis=0).astype(jnp.int32)
    new_base = jnp.broadcast_to(base + tot1 + tot2, base_ref.shape)
    base_ref[...] = new_base
    cnt_ref[...] = new_base.astype(jnp.int32)


def _route_rank(route, tile):
    n = route.shape[1]
    tile = min(tile, n)
    return pl.pallas_call(
        _route_rank_kernel,
        grid=(n // tile,),
        in_specs=[pl.BlockSpec((ROUTE_ROWS, tile), lambda i: (0, i))],
        out_specs=[pl.BlockSpec((ROUTE_ROWS, tile), lambda i: (0, i)),
                   pl.BlockSpec((N_EXPERTS, LANES), lambda i: (0, 0))],
        out_shape=[jax.ShapeDtypeStruct((ROUTE_ROWS, n), jnp.int32),
                   jax.ShapeDtypeStruct((N_EXPERTS, LANES), jnp.int32)],
        scratch_shapes=[pltpu.VMEM((N_EXPERTS, LANES), F32)],
        compiler_params=_cparams("arbitrary"),
        name="route_rank",
    )(route)


def _sc_mesh():
    return plsc.VectorSubcoreMesh(core_axis_name="core", subcore_axis_name="subcore")


def _sc_scatter_pairs(x, pos, rows_out):
    n, d = x.shape
    per_w = n // SC_WORKERS
    window = min(SC_SCATTER_ROWS, per_w)

    @pl.kernel(out_type=jax.ShapeDtypeStruct((rows_out, d), x.dtype), mesh=_sc_mesh(),
               scratch_types=[pltpu.VMEM((window,), jnp.int32), pltpu.VMEM((window,), jnp.int32),
                              pltpu.VMEM((window, d), x.dtype), pltpu.SemaphoreType.DMA, pltpu.SemaphoreType.DMA,
                              pltpu.SemaphoreType.DMA])
    def scatter(x_hbm, p_hbm, o_hbm, i1_v, i2_v, rows_v, sem_a, sem_b, sem_c):
        wid = lax.axis_index("subcore") * SC_CORES + lax.axis_index("core")

        @pl.loop(0, per_w // window)
        def _(j):
            base = wid * per_w + j * window
            load_i1 = pltpu.async_copy(p_hbm.at[pl.ds(base, window)], i1_v, sem_a)
            load_i2 = pltpu.async_copy(p_hbm.at[pl.ds(n + base, window)], i2_v, sem_b)
            load_x = pltpu.async_copy(x_hbm.at[pl.ds(base, window)], rows_v, sem_c)
            load_i1.wait()
            load_i2.wait()
            load_x.wait()
            put_1 = pltpu.async_copy(rows_v, o_hbm.at[i1_v], sem_a)
            put_2 = pltpu.async_copy(rows_v, o_hbm.at[i2_v], sem_b)
            put_1.wait()
            put_2.wait()

    return scatter(x, pos)


def _sc_gather_rows(table, idx):
    m = idx.shape[0]
    d = table.shape[1]
    per_w = m // SC_WORKERS
    window = min(SC_GATHER_ROWS, per_w)

    @pl.kernel(out_type=jax.ShapeDtypeStruct((m, d), table.dtype), mesh=_sc_mesh(),
               scratch_types=[pltpu.VMEM((window,), jnp.int32), pltpu.VMEM((window, d), table.dtype),
                              pltpu.SemaphoreType.DMA])
    def gather(t_hbm, i_hbm, o_hbm, i_v, rows_v, sem):
        wid = lax.axis_index("subcore") * SC_CORES + lax.axis_index("core")

        @pl.loop(0, per_w // window)
        def _(j):
            base = wid * per_w + j * window
            pltpu.sync_copy(i_hbm.at[pl.ds(base, window)], i_v)
            pltpu.async_copy(t_hbm.at[i_v], rows_v, sem).wait()
            pltpu.sync_copy(rows_v, o_hbm.at[pl.ds(base, window)])

    return gather(table, idx)


def _expert_tiles_kernel(te_ref, nu_ref, x_ref, wg_ref, wu_ref, wd_ref, o_ref, wg_s, wu_s, wd_s):
    i = pl.program_id(0)

    @pl.when(i < nu_ref[0])
    def _():
        @pl.when(jnp.logical_or(i == 0, te_ref[i] != te_ref[jnp.maximum(i - 1, 0)]))
        def _():
            wg_s[...] = wg_ref[...].astype(BF16)
            wu_s[...] = wu_ref[...].astype(BF16)
            wd_s[...] = wd_ref[...].astype(BF16)

        o_ref[...] = _pack_halves(_expert_mlp(x_ref[...], wg_s, wu_s, wd_s).astype(BF16))


def _expert_tiles(tile_expert, n_used, xs, w_g, w_u, w_d):
    rows = xs.shape[0]
    tm = EXPERT_ROW_TILE
    grid_spec = pltpu.PrefetchScalarGridSpec(
        num_scalar_prefetch=2,
        grid=(rows // tm,),
        in_specs=[pl.BlockSpec((tm, HALF), lambda i, te, nu: (i, 0)),
                  pl.BlockSpec((None, D_MODEL, D_EXPERT), lambda i, te, nu: (te[i], 0, 0)),
                  pl.BlockSpec((None, D_MODEL, D_EXPERT), lambda i, te, nu: (te[i], 0, 0)),
                  pl.BlockSpec((None, D_EXPERT, D_MODEL), lambda i, te, nu: (te[i], 0, 0))],
        out_specs=pl.BlockSpec((tm, HALF), lambda i, te, nu: (i, 0)),
        scratch_shapes=[pltpu.VMEM((D_MODEL, D_EXPERT), BF16), pltpu.VMEM((D_MODEL, D_EXPERT), BF16),
                        pltpu.VMEM((D_EXPERT, D_MODEL), BF16)],
    )
    return pl.pallas_call(
        _expert_tiles_kernel,
        grid_spec=grid_spec,
        out_shape=jax.ShapeDtypeStruct((rows, HALF), jnp.int32),
        compiler_params=_cparams("arbitrary"),
        name="expert_tiles",
    )(tile_expert, n_used, xs, w_g, w_u, w_d)


def _unpack_f32(p):
    return pltpu.bitcast(p & jnp.int32(-65536), F32), pltpu.bitcast(lax.shift_left(p, jnp.int32(16)), F32)


def _combine_kernel(h_ref, o1_ref, o2_ref, rt_ref, gf_ref, y_ref):
    w1, w2 = rt_ref[:, 2:3], rt_ref[:, 3:4]
    a_lo, a_hi = _unpack_f32(o1_ref[...])
    b_lo, b_hi = _unpack_f32(o2_ref[...])
    y_lo = h_ref[:, 0:HALF] + (w1 * a_lo + w2 * b_lo)
    y_hi = h_ref[:, HALF:D_MODEL] + (w1 * a_hi + w2 * b_hi)
    ms = (jnp.sum(y_lo * y_lo, axis=-1, keepdims=True) + jnp.sum(y_hi * y_hi, axis=-1, keepdims=True)) / D_MODEL
    inv = lax.rsqrt(ms + EPS)
    y_ref[:, 0:HALF] = (y_lo * inv) * gf_ref[:, 0:HALF]
    y_ref[:, HALF:D_MODEL] = (y_hi * inv) * gf_ref[:, HALF:D_MODEL]


def _combine(h, o12, route_t, gf, tile):
    n = h.shape[0]
    tile = min(tile, n)
    nt = n // tile
    return pl.pallas_call(
        _combine_kernel,
        grid=(nt,),
        in_specs=[pl.BlockSpec((tile, D_MODEL), lambda i: (i, 0)),
                  pl.BlockSpec((tile, HALF), lambda i: (i, 0)),
                  pl.BlockSpec((tile, HALF), lambda i: (i + nt, 0)),
                  pl.BlockSpec((tile, ROUTE_ROWS), lambda i: (i, 0)),
                  pl.BlockSpec((1, D_MODEL), lambda i: (0, 0))],
        out_specs=pl.BlockSpec((tile, D_MODEL), lambda i: (i, 0)),
        out_shape=jax.ShapeDtypeStruct((n, D_MODEL), F32),
        compiler_params=_cparams("parallel"),
        name="combine_final_norm",
    )(h, o12, o12, route_t, gf)


def _sparse_moe(xn2p, route, h, w_g, w_u, w_d, gf):
    n = h.shape[0]
    tm = EXPERT_ROW_TILE
    rows = 2 * n + N_EXPERTS * tm
    rank, cnt = _route_rank(route, 512)
    counts = cnt[:, 0]
    padded = (counts + tm - 1) // tm * tm
    e_idx = jnp.arange(N_EXPERTS, dtype=jnp.int32)
    starts = jnp.sum(jnp.where(e_idx[None, :] < e_idx[:, None], padded[None, :], 0), axis=1)
    ends = starts + padded
    ids = route[0:2].astype(jnp.int32)
    start_of = jnp.sum(jnp.where(ids[None] == e_idx[:, None, None], starts[:, None, None], 0), axis=0)
    pos = (start_of + rank[0:2]).reshape(2 * n)
    tile_start = jnp.arange(rows // tm, dtype=jnp.int32) * tm
    tile_expert = jnp.minimum(jnp.sum((tile_start[:, None] >= ends[None, :]).astype(jnp.int32), axis=1),
                              N_EXPERTS - 1)
    n_used = (ends[-1:] // tm).astype(jnp.int32)
    xs = _sc_scatter_pairs(xn2p, pos, rows)
    os_ = _expert_tiles(tile_expert, n_used, xs, w_g, w_u, w_d)
    o12 = _sc_gather_rows(os_, pos)
    return _combine(h, o12, route.T, gf, 512)


def _prep_in_weights(w_in):
    o = 0
    w_u = w_in[:, o:o + SSM_WIDTH]; o += SSM_WIDTH
    w_q = w_in[:, o:o + SWA_WIDTH]; o += SWA_WIDTH
    w_k = w_in[:, o:o + SWA_KV_WIDTH]; o += SWA_KV_WIDTH
    w_v = w_in[:, o:o + SWA_KV_WIDTH]; o += SWA_KV_WIDTH
    w_qm = w_in[:, o:o + MEM_WIDTH]; o += MEM_WIDTH
    w_g = w_in[:, o:]
    wq = (w_q * (SWA_HEAD_DIM ** -0.5)).reshape(D_MODEL, SWA_KV_HEADS, SWA_REP, SWA_HEAD_DIM)
    wq = wq.transpose(0, 2, 1, 3).reshape(D_MODEL, SWA_WIDTH)
    w_main = jnp.concatenate([w_u, wq, w_k, w_v, w_qm], axis=1).astype(BF16)
    return w_main, w_g.astype(BF16)


IN_SPLITS = (SSM_WIDTH, SWA_WIDTH, SWA_KV_WIDTH, SWA_KV_WIDTH, MEM_WIDTH)
IN_DTYPES = ((F32, BF16), (BF16,), (F32,), (F32,), (BF16,))


def kernel(x_prompt, x_sample, cache_swa_k, cache_swa_v, state_ssm_re, state_ssm_im, cache_mem_k, cache_mem_v, mem_prompt, norm1_g, w_in, lam_re, lam_im, log_dt, bm_re, bm_im, cm_re, cm_im, d_skip, w_glu, b_glu, sinks, rel_table, mem_norm_g, w_mem_kv, w_br_ssm, w_br_swa, w_br_mem, w_out, norm2_g, w_rg, b_rg, w_rexp, b_rexp, w_e_gate, w_e_up, w_e_down, final_norm_g):
    nb, t, _ = x_prompt.shape
    ns, ts, _ = x_sample.shape
    l = 0
    L = S5_CHUNK

    w_main, w_gates = _prep_in_weights(w_in[l])
    w_swa = (w_br_swa[l].reshape(SWA_KV_HEADS, SWA_REP, SWA_HEAD_DIM, D_MODEL).transpose(1, 0, 2, 3)
             .reshape(SWA_WIDTH, D_MODEL))
    pad_rows = ROUTER_ROWS - N_EXPERTS - N_EXPERT_GROUPS
    w_router = jnp.concatenate([w_rexp[l].T, w_rg[l].T, jnp.zeros((pad_rows, D_MODEL), F32)], axis=0).astype(BF16)
    b_router = jnp.concatenate([b_rexp[l], b_rg[l], jnp.zeros((pad_rows,), F32)]).reshape(ROUTER_ROWS, 1)
    mp = {
        'g1': norm1_g[l].reshape(1, D_MODEL), 'w_gates': w_gates, 'd_skip': d_skip[l].reshape(1, SSM_WIDTH),
        'w_glu': w_glu[l].astype(BF16), 'b_glu': b_glu[l].reshape(1, SSM_WIDTH),
        'w_br_ssm': w_br_ssm[l].astype(BF16), 'w_br_swa': w_swa.astype(BF16),
        'w_br_mem': w_br_mem[l].astype(BF16), 'w_out': w_out[l].astype(BF16),
        'g2': norm2_g[l].reshape(1, D_MODEL), 'w_router': w_router, 'b_router': b_router,
    }
    w_g, w_u, w_d = w_e_gate[l], w_e_up[l], w_e_down[l]
    gf = final_norm_g.reshape(1, D_MODEL)
    s5_w = _s5_weights(lam_re[l], lam_im[l], log_dt[l], bm_re[l], bm_im[l], cm_re[l], cm_im[l], L)

    bias_p = _rel_bias(rel_table, np.arange(WINDOW)[:, None] + WINDOW - np.arange(2 * WINDOW)[None, :])
    keys_s = WINDOW + 2 * ts
    bias_s = _rel_bias(rel_table, np.arange(ts)[:, None] + WINDOW - np.arange(keys_s)[None, :])
    bias_s = bias_s.reshape(SWA_HEADS * ts, keys_s)
    sink_rows = jnp.repeat(sinks[l].astype(F32), ts).reshape(SWA_HEADS * ts, 1)

    n = nb * t
    xp = x_prompt.reshape(n, D_MODEL)
    mk, mv = _norm_proj(mem_prompt.reshape(nb * MEM_TOKENS, D_MODEL), mem_norm_g[l].reshape(1, D_MODEL),
                        w_mem_kv[l].astype(BF16), (MEM_WIDTH, MEM_WIDTH), ((F32,), (F32,)), 512)
    u, ub, qz, k, v, qm = _norm_proj(xp, mp['g1'], w_main, IN_SPLITS, IN_DTYPES, 512)

    y_ssm, fin = _s5(ub, jnp.zeros((nb, N_CH_TILES * 2 * STATE_TILE), F32), s5_w, nb, t // L, L, 64)
    p_re, p_im = _tiles_to_state(fin)

    o_swa = _swa_prompt(qz, k, v, bias_p, sinks[l].astype(F32), nb, t, 4)
    o_mem = _mem_prompt(qm, mk, mv, nb, t, 512)
    h, xn2p, route = _merge(xp, u, y_ssm, o_swa, o_mem, mp, 512)
    y_prompt = _sparse_moe(xn2p, route, h, w_g, w_u, w_d, gf).reshape(nb, t, D_MODEL)

    k4 = k.reshape(nb, t, SWA_KV_HEADS, SWA_HEAD_DIM)
    v4 = v.reshape(nb, t, SWA_KV_HEADS, SWA_HEAD_DIM)
    new_k_p, new_v_p = k4[:, -WINDOW:][None], v4[:, -WINDOW:][None]
    new_mk = mk.reshape(1, nb, MEM_TOKENS, MEM_HEADS, MEM_HEAD_DIM)
    new_mv = mv.reshape(1, nb, MEM_TOKENS, MEM_HEADS, MEM_HEAD_DIM)

    m = ns * ts
    xs = x_sample.reshape(m, D_MODEL)
    us, ubs, qzs, k_s, v_s, qms = _norm_proj(xs, mp['g1'], w_main, IN_SPLITS, IN_DTYPES, 256)
    ys_ssm, fins = _s5(ubs, _state_to_tiles(state_ssm_re[l], state_ssm_im[l]), s5_w, ns, ts // L, L, 64)
    s_re, s_im = _tiles_to_state(fins)

    kk_all = jnp.concatenate([cache_swa_k[l].reshape(ns, WINDOW, SWA_KV_WIDTH).astype(F32),
                              k_s.reshape(ns, ts, SWA_KV_WIDTH)], axis=1)
    vv_all = jnp.concatenate([cache_swa_v[l].reshape(ns, WINDOW, SWA_KV_WIDTH).astype(F32),
                              v_s.reshape(ns, ts, SWA_KV_WIDTH)], axis=1)
    pad = jnp.zeros((ns, keys_s - WINDOW - ts, SWA_KV_WIDTH), F32)
    q5 = qzs.reshape(ns, ts, SWA_REP, SWA_KV_HEADS, SWA_HEAD_DIM)
    zq = jnp.zeros((ns, ts, SWA_REP, SWA_HEAD_DIM), BF16)
    q_rows = jnp.concatenate([jnp.concatenate([q5[:, :, :, 0], zq], axis=-1),
                              jnp.concatenate([zq, q5[:, :, :, 1]], axis=-1)], axis=2)
    q_rows = q_rows.transpose(0, 2, 1, 3).reshape(ns, SWA_HEADS * ts, LANES)
    o_dec = _swa_decode(q_rows, jnp.concatenate([kk_all, pad], axis=1), jnp.concatenate([vv_all, pad], axis=1),
                        bias_s, sink_rows, ts, 8)
    o_dec = o_dec.reshape(ns, SWA_KV_HEADS, SWA_REP, ts, SWA_KV_HEADS, SWA_HEAD_DIM)
    o_dec = jnp.stack([o_dec[:, g, :, :, g] for g in range(SWA_KV_HEADS)], axis=1)
    o_swa_s = o_dec.transpose(0, 3, 2, 1, 4).reshape(m, SWA_WIDTH).astype(BF16)

    o_mem_s = _mem_decode(qms.astype(F32).reshape(ns, ts, MEM_WIDTH), cache_mem_k, cache_mem_v, l, 4)
    o_mem_s = o_mem_s.reshape(m, MEM_WIDTH).astype(BF16)

    hs_, xn2ps, routes = _merge(xs, us, ys_ssm, o_swa_s, o_mem_s, mp, 256)
    y_sample = _moe(xn2ps, routes.T, w_g, w_u, w_d, hs_, gf, 1024).reshape(ns, ts, D_MODEL)

    new_k_s = kk_all[:, -WINDOW:].reshape(1, ns, WINDOW, SWA_KV_HEADS, SWA_HEAD_DIM).astype(cache_swa_k.dtype)
    new_v_s = vv_all[:, -WINDOW:].reshape(1, ns, WINDOW, SWA_KV_HEADS, SWA_HEAD_DIM).astype(cache_swa_v.dtype)

    return (y_prompt, y_sample,
            new_k_p, new_v_p, p_re[None], p_im[None], new_mk, new_mv,
            new_k_s, new_v_s, s_re[None].astype(state_ssm_re.dtype), s_im[None].astype(state_ssm_im.dtype))
```

```python
import functools
import math

import numpy as np
import jax
import jax.numpy as jnp
from jax import lax
from jax.experimental import pallas as pl
from jax.experimental.pallas import tpu as pltpu
from jax.experimental.pallas import tpu_sc as plsc

F32 = jnp.float32
BF16 = jnp.bfloat16

D_MODEL = 1024
SSM_WIDTH = 512
SSM_GROUP = 16
SSM_GROUPS = 32
SSM_STATE = 64
SWA_HEADS = 8
SWA_KV_HEADS = 2
SWA_REP = 4
SWA_HEAD_DIM = 64
SWA_WIDTH = 512
SWA_KV_WIDTH = 128
WINDOW = 128
REL_BUCKETS = 32
REL_MAX_DIST = 128
MEM_TOKENS = 256
MEM_HEADS = 4
MEM_HEAD_DIM = 128
MEM_WIDTH = 512
N_EXPERT_GROUPS = 4
EXPERTS_PER_GROUP = 8
N_EXPERTS = 32
D_EXPERT = 256
EPS = 1e-6
NEG_INF = -1e30

LANES = 128
GROUPS_PER_TILE = LANES // SSM_GROUP
N_CH_TILES = SSM_WIDTH // LANES
STATE_TILE = GROUPS_PER_TILE * SSM_STATE
VMEM_LIMIT = 56 * 1024 * 1024
S5_CHUNK = 8

_TRANS_B = (((1,), (1,)), ((), ()))


def _cparams(*sem):
    return pltpu.CompilerParams(dimension_semantics=sem, vmem_limit_bytes=VMEM_LIMIT)


def _rms(x, g):
    return (x * lax.rsqrt(jnp.mean(x * x, axis=-1, keepdims=True) + EPS)) * g


def _dot(a, b):
    return jnp.dot(a, b, preferred_element_type=F32)


def _norm_proj_kernel(x_ref, g_ref, w_ref, *out_refs, splits, dtypes):
    xb = _rms(x_ref[...], g_ref[...]).astype(BF16)
    off = 0
    outs = iter(out_refs)
    for width, dts in zip(splits, dtypes):
        r = _dot(xb, w_ref[:, off:off + width])
        for dt in dts:
            next(outs)[...] = r.astype(dt)
        off += width


def _norm_proj(x, g, w, splits, dtypes, tile):
    n, d = x.shape
    tile = min(tile, n)
    flat = [(wd, dt) for wd, dts in zip(splits, dtypes) for dt in dts]
    return pl.pallas_call(
        functools.partial(_norm_proj_kernel, splits=tuple(splits), dtypes=tuple(dtypes)),
        grid=(n // tile,),
        in_specs=[pl.BlockSpec((tile, d), lambda i: (i, 0)),
                  pl.BlockSpec((1, d), lambda i: (0, 0)),
                  pl.BlockSpec((d, sum(splits)), lambda i: (0, 0))],
        out_specs=[pl.BlockSpec((tile, wd), lambda i: (i, 0)) for wd, _ in flat],
        out_shape=[jax.ShapeDtypeStruct((n, wd), dt) for wd, dt in flat],
        compiler_params=_cparams("parallel"),
        name="norm_proj",
    )(x, g, w)


def _s5_weights(lam_re, lam_im, log_dt, bm_re, bm_im, cm_re, cm_im, L):
    hp = lax.Precision.HIGHEST
    nt, gt, P, H = N_CH_TILES, GROUPS_PER_TILE, SSM_STATE, SSM_GROUP
    lr, li = lam_re.astype(F32), lam_im.astype(F32)
    dt = jnp.exp(log_dt.astype(F32))[:, None]
    mag = jnp.exp(lr * dt)
    a_re = mag * jnp.cos(li * dt)
    a_im = mag * jnp.sin(li * dt)
    den = lr * lr + li * li
    f_re = ((a_re - 1.0) * lr + a_im * li) / den
    f_im = (a_im * lr - (a_re - 1.0) * li) / den
    br, bi = bm_re.astype(F32), bm_im.astype(F32)
    bb_re = f_re[..., None] * br - f_im[..., None] * bi
    bb_im = f_re[..., None] * bi + f_im[..., None] * br
    pr, pi = [jnp.ones_like(a_re)], [jnp.zeros_like(a_im)]
    for _ in range(L):
        pr.append(pr[-1] * a_re - pi[-1] * a_im)
        pi.append(pr[-2] * a_im + pi[-1] * a_re)
    ap_re, ap_im = jnp.stack(pr), jnp.stack(pi)
    cr, ci = cm_re.astype(F32), cm_im.astype(F32)
    ca_re = cr[None] * ap_re[:, :, None, :] - ci[None] * ap_im[:, :, None, :]
    ca_im = cr[None] * ap_im[:, :, None, :] + ci[None] * ap_re[:, :, None, :]

    rev_re = jnp.stack([pr[L - 1 - s] for s in range(L)])
    rev_im = jnp.stack([pi[L - 1 - s] for s in range(L)])
    ws_re = rev_re[..., None] * bb_re[None] - rev_im[..., None] * bb_im[None]
    ws_im = rev_re[..., None] * bb_im[None] + rev_im[..., None] * bb_re[None]
    c_st = jnp.concatenate([ws_re.transpose(0, 1, 3, 2).reshape(L, nt, gt * H, P),
                            ws_im.transpose(0, 1, 3, 2).reshape(L, nt, gt * H, P)], axis=3).transpose(1, 0, 2, 3)
    so = lambda ca: ca[1:].transpose(1, 3, 0, 2).reshape(nt, gt * P, L * H)
    c_so = jnp.concatenate([so(ca_re), so(-ca_im)], axis=1)
    k_lag = (jnp.einsum('tghp,gpk->gkth', ca_re[:L], bb_re, precision=hp)
             - jnp.einsum('tghp,gpk->gkth', ca_im[:L], bb_im, precision=hp))
    c_k = k_lag.reshape(nt, gt * H, L * H)
    w_st, w_out, toep = _s5_expand(c_st, c_so, c_k, L)

    def per_tile(v):
        return v.reshape(nt, 1, STATE_TILE)

    return w_st, w_out, toep, per_tile(pr[L]), per_tile(pi[L])


def _s5_expand_kernel(cst_ref, cso_ref, ck_ref, wst_ref, wso_ref, toep_ref, *, L):
    hp = lax.Precision.HIGHEST
    P, H = SSM_STATE, SSM_GROUP
    iota = lambda shape, d: lax.broadcasted_iota(jnp.int32, shape, d)
    one = lambda cond: jnp.where(cond, 1.0, 0.0).astype(F32)

    r, c = iota((2 * P, 2 * STATE_TILE), 0), iota((2 * P, 2 * STATE_TILE), 1)
    rep_st = one((r // P == c // STATE_TILE) & (r % P == c % P))
    r, c = iota((LANES, 2 * STATE_TILE), 0), iota((LANES, 2 * STATE_TILE), 1)
    own_st = one(r // H == (c % STATE_TILE) // P)
    for s in range(L):
        blk = jnp.dot(cst_ref[s], rep_st, precision=hp, preferred_element_type=F32) * own_st
        wst_ref[s * LANES:(s + 1) * LANES, :] = blk.astype(BF16)

    r, c = iota((LANES, LANES), 0), iota((LANES, LANES), 1)
    pick = [one((r // H == t) & (r % H == c % H)) for t in range(L)]
    own_k = one(r // H == c // H)
    r, c = iota((2 * STATE_TILE, LANES), 0), iota((2 * STATE_TILE, LANES), 1)
    own_so = one((r % STATE_TILE) // P == c // H)
    cso = cso_ref[...]
    for t in range(L):
        blk = jnp.dot(cso, pick[t], precision=hp, preferred_element_type=F32) * own_so
        wso_ref[:, t * LANES:(t + 1) * LANES] = blk.astype(BF16)
    ck = ck_ref[...]
    lag = [(jnp.dot(ck, pick[t], precision=hp, preferred_element_type=F32) * own_k).astype(BF16) for t in range(L)]
    zero = jnp.zeros((LANES, LANES), BF16)
    for s in range(L):
        for t in range(L):
            toep_ref[s * LANES:(s + 1) * LANES, t * LANES:(t + 1) * LANES] = lag[t - s] if t >= s else zero


def _s5_expand(c_st, c_so, c_k, L):
    lk = L * LANES
    st2 = 2 * STATE_TILE
    return pl.pallas_call(
        functools.partial(_s5_expand_kernel, L=L),
        grid=(N_CH_TILES,),
        in_specs=[pl.BlockSpec((None, L, LANES, 2 * SSM_STATE), lambda j: (j, 0, 0, 0)),
                  pl.BlockSpec((None, st2, L * SSM_GROUP), lambda j: (j, 0, 0)),
                  pl.BlockSpec((None, LANES, L * SSM_GROUP), lambda j: (j, 0, 0))],
        out_specs=[pl.BlockSpec((None, lk, st2), lambda j: (j, 0, 0)),
                   pl.BlockSpec((None, st2, lk), lambda j: (j, 0, 0)),
                   pl.BlockSpec((None, lk, lk), lambda j: (j, 0, 0))],
        out_shape=[jax.ShapeDtypeStruct((N_CH_TILES, lk, st2), BF16),
                   jax.ShapeDtypeStruct((N_CH_TILES, st2, lk), BF16),
                   jax.ShapeDtypeStruct((N_CH_TILES, lk, lk), BF16)],
        compiler_params=_cparams("parallel"),
        name="s5_expand_weights",
    )(c_st, c_so, c_k)


def _to_chunks(u, nb, nc, L):
    return (u.reshape(nb, nc, L, N_CH_TILES, LANES).transpose(1, 0, 3, 2, 4)
            .reshape(nc * nb, N_CH_TILES * L * LANES))


def _from_chunks(y, nb, nc, L):
    return (y.reshape(nc, nb, N_CH_TILES, L, LANES).transpose(1, 0, 3, 2, 4)
            .reshape(nb * nc * L, SSM_WIDTH))


def _s5_kernel(x_ref, h0_ref, are_ref, aim_ref, ws_ref, t_ref, wo_ref, y_ref, fin_ref,
               hr_ref, hi_ref, d_ref, hs_ref, *, cb, nb):
    ci = pl.program_id(1)

    @pl.when(ci == 0)
    def _():
        hr_ref[...] = h0_ref[:, 0:STATE_TILE]
        hi_ref[...] = h0_ref[:, STATE_TILE:2 * STATE_TILE]

    x = x_ref[...]
    d_ref[...] = _dot(x, ws_ref[...])
    ar = jnp.broadcast_to(are_ref[...], (nb, STATE_TILE))
    ai = jnp.broadcast_to(aim_ref[...], (nb, STATE_TILE))

    def body(c, carry):
        hr, hi = carry
        r0 = pl.multiple_of(c * nb, nb)
        hs_ref[pl.ds(r0, nb), 0:STATE_TILE] = hr
        hs_ref[pl.ds(r0, nb), STATE_TILE:2 * STATE_TILE] = hi
        d = d_ref[pl.ds(r0, nb), :]
        return (ar * hr - ai * hi + d[:, 0:STATE_TILE],
                ar * hi + ai * hr + d[:, STATE_TILE:2 * STATE_TILE])

    hr, hi = lax.fori_loop(0, cb, body, (hr_ref[...], hi_ref[...]))
    hr_ref[...] = hr
    hi_ref[...] = hi
    y_ref[...] = _dot(x, t_ref[...]) + _dot(hs_ref[...].astype(BF16), wo_ref[...])

    @pl.when(ci == pl.num_programs(1) - 1)
    def _():
        fin_ref[:, 0:STATE_TILE] = hr
        fin_ref[:, STATE_TILE:2 * STATE_TILE] = hi


def _s5(ub, h0, weights, nb, nc, L, chunk_block):
    w_st, w_so, toep, a_re, a_im = weights
    xc = _to_chunks(ub, nb, nc, L)
    cb = min(chunk_block, nc)
    rows = cb * nb
    lk = L * LANES
    st2 = 2 * STATE_TILE
    tile_w = lambda shape: pl.BlockSpec((None,) + shape, lambda j, c: (j, 0, 0))
    y, fin = pl.pallas_call(
        functools.partial(_s5_kernel, cb=cb, nb=nb),
        grid=(N_CH_TILES, nc // cb),
        in_specs=[pl.BlockSpec((rows, lk), lambda j, c: (c, j)),
                  pl.BlockSpec((nb, st2), lambda j, c: (0, j)),
                  tile_w((1, STATE_TILE)), tile_w((1, STATE_TILE)),
                  tile_w((lk, st2)), tile_w((lk, lk)), tile_w((st2, lk))],
        out_specs=[pl.BlockSpec((rows, lk), lambda j, c: (c, j)),
                   pl.BlockSpec((nb, st2), lambda j, c: (0, j))],
        out_shape=[jax.ShapeDtypeStruct((nc * nb, N_CH_TILES * lk), F32),
                   jax.ShapeDtypeStruct((nb, N_CH_TILES * st2), F32)],
        scratch_shapes=[pltpu.VMEM((nb, STATE_TILE), F32), pltpu.VMEM((nb, STATE_TILE), F32),
                        pltpu.VMEM((rows, st2), F32), pltpu.VMEM((rows, st2), F32)],
        compiler_params=_cparams("parallel", "arbitrary"),
        name="s5_chunked_scan",
    )(xc, h0, a_re, a_im, w_st, toep, w_so)
    return _from_chunks(y, nb, nc, L), fin


def _state_to_tiles(h_re, h_im):
    nb = h_re.shape[0]
    r = h_re.astype(F32).reshape(nb, N_CH_TILES, STATE_TILE)
    i = h_im.astype(F32).reshape(nb, N_CH_TILES, STATE_TILE)
    return jnp.concatenate([r, i], axis=-1).reshape(nb, N_CH_TILES * 2 * STATE_TILE)


def _tiles_to_state(h):
    nb = h.shape[0]
    h = h.reshape(nb, N_CH_TILES, 2, GROUPS_PER_TILE, SSM_STATE)
    return (h[:, :, 0].reshape(nb, SSM_GROUPS, SSM_STATE), h[:, :, 1].reshape(nb, SSM_GROUPS, SSM_STATE))


def _t5_bucket(dist):
    n = np.maximum(dist, 0)
    max_exact = REL_BUCKETS // 2
    nf = np.maximum(n, 1).astype(np.float32)
    large = max_exact + (np.log(nf / np.float32(max_exact)) / np.float32(math.log(REL_MAX_DIST / max_exact))
                         * np.float32(REL_BUCKETS - max_exact)).astype(np.int32)
    large = np.minimum(large, REL_BUCKETS - 1)
    return np.where(n < max_exact, n, large)


def _rel_bias(rel_table, dist):
    bucket = _t5_bucket(dist)
    tab = rel_table.astype(F32)
    out = jnp.zeros((SWA_HEADS,) + dist.shape, F32)
    for b in range(REL_BUCKETS):
        sel = jnp.asarray(bucket == b)
        if bool((bucket == b).any()):
            out = jnp.where(sel[None], tab[b].reshape((SWA_HEADS,) + (1,) * dist.ndim), out)
    return out


def _softmax_sink(s, sink):
    m = jnp.maximum(jnp.max(s, axis=-1, keepdims=True), sink)
    e = jnp.exp(s - m)
    den = jnp.sum(e, axis=-1, keepdims=True) + jnp.exp(sink - m)
    return e * (1.0 / den)


def _swa_prompt_kernel(sink_ref, q_ref, kp_ref, kc_ref, vp_ref, vc_ref, bias_ref, o_ref, kk_ref, vv_ref, *, qblocks):
    step = pl.program_id(1)
    kk_ref[0:WINDOW, :] = kp_ref[...].astype(BF16)
    kk_ref[WINDOW:, :] = kc_ref[...].astype(BF16)
    vv_ref[0:WINDOW, :] = vp_ref[...].astype(BF16)
    vv_ref[WINDOW:, :] = vc_ref[...].astype(BF16)
    row = lax.broadcasted_iota(jnp.int32, (WINDOW, 2 * WINDOW), 0)
    col = lax.broadcasted_iota(jnp.int32, (WINDOW, 2 * WINDOW), 1)
    dist = row + WINDOW - col
    band = (dist >= 0) & (dist < WINDOW)
    lane = lax.broadcasted_iota(jnp.int32, (WINDOW, LANES), 1)
    low = lane < SWA_HEAD_DIM

    def block(j, carry):
        r0 = pl.multiple_of(j * WINDOW, WINDOW)
        kk = kk_ref[pl.ds(r0, 2 * WINDOW), :]
        vv = vv_ref[pl.ds(r0, 2 * WINDOW), :]
        valid = band & ((col >= WINDOW) | (step * qblocks + j > 0))
        for t in range(SWA_REP):
            q2 = q_ref[pl.ds(r0, WINDOW), t * LANES:(t + 1) * LANES]
            outs = []
            for half in range(SWA_KV_HEADS):
                h = t + SWA_REP * half
                qh = jnp.where(low if half == 0 else jnp.logical_not(low), q2, jnp.zeros_like(q2))
                s = lax.dot_general(qh, kk, _TRANS_B, preferred_element_type=F32)
                s = jnp.where(valid, s + bias_ref[h], NEG_INF)
                sink = sink_ref[h]
                m = jnp.maximum(jnp.max(s, axis=-1, keepdims=True), sink)
                e = jnp.exp(s - m)
                den = jnp.sum(e, axis=-1, keepdims=True) + jnp.exp(sink - m)
                outs.append(_dot(e.astype(BF16), vv) * (1.0 / den))
            o_ref[pl.ds(r0, WINDOW), t * LANES:(t + 1) * LANES] = jnp.where(low, outs[0], outs[1]).astype(BF16)
        return carry

    lax.fori_loop(0, qblocks, block, 0)


def _swa_prompt(q, k, v, bias, sinks, nb, t, qblocks):
    nstep = t // (WINDOW * qblocks)
    rows = WINDOW * qblocks
    cur = lambda b, i: (b * nstep + i, 0)
    prev = lambda b, i: (b * nstep * qblocks + jnp.maximum(i * qblocks - 1, 0), 0)
    return pl.pallas_call(
        functools.partial(_swa_prompt_kernel, qblocks=qblocks),
        grid=(nb, nstep),
        in_specs=[pl.BlockSpec(memory_space=pltpu.SMEM),
                  pl.BlockSpec((rows, SWA_WIDTH), cur),
                  pl.BlockSpec((WINDOW, SWA_KV_WIDTH), prev),
                  pl.BlockSpec((rows, SWA_KV_WIDTH), cur),
                  pl.BlockSpec((WINDOW, SWA_KV_WIDTH), prev),
                  pl.BlockSpec((rows, SWA_KV_WIDTH), cur),
                  pl.BlockSpec((SWA_HEADS, WINDOW, 2 * WINDOW), lambda b, i: (0, 0, 0))],
        out_specs=pl.BlockSpec((rows, SWA_WIDTH), cur),
        out_shape=jax.ShapeDtypeStruct((nb * t, SWA_WIDTH), BF16),
        scratch_shapes=[pltpu.VMEM((rows + WINDOW, SWA_KV_WIDTH), BF16),
                        pltpu.VMEM((rows + WINDOW, SWA_KV_WIDTH), BF16)],
        compiler_params=_cparams("parallel", "parallel"),
        name="swa_prompt",
    )(sinks, q, k, k, v, v, bias)


def _swa_decode_kernel(q_ref, k_ref, v_ref, bias_ref, sink_ref, o_ref, *, seqs, tq):
    rows, keys = q_ref.shape[1], k_ref.shape[1]
    qi = lax.broadcasted_iota(jnp.int32, (rows, keys), 0) % tq
    col = lax.broadcasted_iota(jnp.int32, (rows, keys), 1)
    dist = qi + WINDOW - col
    valid = (dist >= 0) & (dist < WINDOW)
    bias = bias_ref[...]
    sink = sink_ref[...]
    for s_i in range(seqs):
        kk = k_ref[s_i].astype(BF16)
        s = lax.dot_general(q_ref[s_i], kk, _TRANS_B, preferred_element_type=F32)
        s = jnp.where(valid, s + bias, NEG_INF)
        p = _softmax_sink(s, sink).astype(BF16)
        o_ref[s_i] = _dot(p, v_ref[s_i].astype(BF16))


def _swa_decode(qz, k_all, v_all, bias, sink_rows, tq, seqs):
    nseq, rows, _ = qz.shape
    keys = k_all.shape[1]
    seqs = min(seqs, nseq)
    return pl.pallas_call(
        functools.partial(_swa_decode_kernel, seqs=seqs, tq=tq),
        grid=(nseq // seqs,),
        in_specs=[pl.BlockSpec((seqs, rows, LANES), lambda i: (i, 0, 0)),
                  pl.BlockSpec((seqs, keys, LANES), lambda i: (i, 0, 0)),
                  pl.BlockSpec((seqs, keys, LANES), lambda i: (i, 0, 0)),
                  pl.BlockSpec((rows, keys), lambda i: (0, 0)),
                  pl.BlockSpec((rows, 1), lambda i: (0, 0))],
        out_specs=pl.BlockSpec((seqs, rows, LANES), lambda i: (i, 0, 0)),
        out_shape=jax.ShapeDtypeStruct((nseq, rows, LANES), F32),
        compiler_params=_cparams("parallel"),
        name="swa_decode",
    )(qz, k_all, v_all, bias, sink_rows)


def _softmax(s):
    m = jnp.max(s, axis=-1, keepdims=True)
    e = jnp.exp(s - m)
    return e * (1.0 / jnp.sum(e, axis=-1, keepdims=True))


def _mem_prompt_kernel(q_ref, k_ref, v_ref, o_ref, s_ref, p_ref):
    scale = MEM_HEAD_DIM ** -0.5
    heads = [slice(h * MEM_HEAD_DIM, (h + 1) * MEM_HEAD_DIM) for h in range(MEM_HEADS)]
    for h, sl in enumerate(heads):
        s_ref[h] = lax.dot_general(q_ref[:, sl], k_ref[:, sl].astype(BF16), _TRANS_B, preferred_element_type=F32)
    s = s_ref[...] * scale
    e = jnp.exp(s - jnp.max(s, axis=-1, keepdims=True))
    p_ref[...] = e.astype(BF16)
    inv = 1.0 / jnp.sum(e, axis=-1, keepdims=True)
    for h, sl in enumerate(heads):
        o_ref[:, sl] = (_dot(p_ref[h], v_ref[:, sl].astype(BF16)) * inv[h]).astype(BF16)


def _mem_prompt(qm, mk, mv, nb, t, tile):
    tile = min(tile, t)
    nt = t // tile
    return pl.pallas_call(
        _mem_prompt_kernel,
        grid=(nb, nt),
        in_specs=[pl.BlockSpec((tile, MEM_WIDTH), lambda b, i: (b * nt + i, 0)),
                  pl.BlockSpec((MEM_TOKENS, MEM_WIDTH), lambda b, i: (b, 0)),
                  pl.BlockSpec((MEM_TOKENS, MEM_WIDTH), lambda b, i: (b, 0))],
        out_specs=pl.BlockSpec((tile, MEM_WIDTH), lambda b, i: (b * nt + i, 0)),
        out_shape=jax.ShapeDtypeStruct((nb * t, MEM_WIDTH), BF16),
        scratch_shapes=[pltpu.VMEM((MEM_HEADS, tile, MEM_TOKENS), F32), pltpu.VMEM((MEM_HEADS, tile, MEM_TOKENS), BF16)],
        compiler_params=_cparams("parallel", "parallel"),
        name="mem_prompt",
    )(qm, mk, mv)


def _mem_decode_kernel(q_ref, k_ref, v_ref, o_ref, *, seqs):
    tq = q_ref.shape[1]
    rows, cols = MEM_HEADS * tq, MEM_TOKENS * MEM_HEADS
    k2 = k_ref.reshape(seqs, cols, MEM_HEAD_DIM)
    v2 = v_ref.reshape(seqs, cols, MEM_HEAD_DIM)
    scale = MEM_HEAD_DIM ** -0.5
    own = (lax.broadcasted_iota(jnp.int32, (rows, cols), 1) % MEM_HEADS
           == lax.broadcasted_iota(jnp.int32, (rows, cols), 0) // tq)
    for s_i in range(seqs):
        q = q_ref[s_i]
        qb = jnp.concatenate([q[:, h * MEM_HEAD_DIM:(h + 1) * MEM_HEAD_DIM] for h in range(MEM_HEADS)], axis=0)
        s = lax.dot_general(qb.astype(BF16), k2[s_i].astype(BF16), _TRANS_B, preferred_element_type=F32) * scale
        p = _softmax(jnp.where(own, s, NEG_INF)).astype(BF16)
        o = _dot(p, v2[s_i].astype(BF16))
        for h in range(MEM_HEADS):
            o_ref[s_i, :, h * MEM_HEAD_DIM:(h + 1) * MEM_HEAD_DIM] = o[h * tq:(h + 1) * tq, :]


def _mem_decode(q, k, v, layer, seqs):
    nseq, tq, _ = q.shape
    seqs = min(seqs, nseq)
    cache = pl.BlockSpec((None, seqs, MEM_TOKENS, MEM_HEADS, MEM_HEAD_DIM), lambda i: (layer, i, 0, 0, 0))
    return pl.pallas_call(
        functools.partial(_mem_decode_kernel, seqs=seqs),
        grid=(nseq // seqs,),
        in_specs=[pl.BlockSpec((seqs, tq, MEM_WIDTH), lambda i: (i, 0, 0)), cache, cache],
        out_specs=pl.BlockSpec((seqs, tq, MEM_WIDTH), lambda i: (i, 0, 0)),
        out_shape=jax.ShapeDtypeStruct((nseq, tq, MEM_WIDTH), F32),
        compiler_params=_cparams("parallel"),
        name="mem_decode",
    )(q, k, v)


ROUTER_ROWS = 40
ROUTE_ROWS = 8
HALF = D_MODEL // 2


def _pack_halves(xb):
    hi = pltpu.bitcast(xb[:, 0:HALF].astype(F32), jnp.int32)
    lo = pltpu.bitcast(xb[:, HALF:D_MODEL].astype(F32), jnp.int32)
    return hi | lax.shift_right_logical(lo, jnp.int32(16))


def _unpack_halves(p):
    hi = pltpu.bitcast(p & jnp.int32(-65536), F32).astype(BF16)
    lo = pltpu.bitcast(lax.shift_left(p, jnp.int32(16)), F32).astype(BF16)
    return hi, lo


def _merge_kernel(x_ref, u_ref, y_ref, os_ref, om_ref, g1_ref, wg_ref, dsk_ref, wglu_ref, bglu_ref,
                  wbs_ref, wbw_ref, wbm_ref, wout_ref, g2_ref, wr_ref, br_ref,
                  h_ref, xn2_ref, route_ref):
    x = x_ref[...]
    tt = x.shape[0]
    xb = _rms(x, g1_ref[...]).astype(BF16)
    z = jax.nn.gelu(y_ref[...] + dsk_ref[...] * u_ref[...])
    z = z * jax.nn.sigmoid(_dot(z.astype(BF16), wglu_ref[...]) + bglu_ref[...])
    merged = jax.nn.sigmoid(_dot(xb, wg_ref[:, 0:D_MODEL])) * _dot(z.astype(BF16), wbs_ref[...])
    merged = merged + jax.nn.sigmoid(_dot(xb, wg_ref[:, D_MODEL:2 * D_MODEL])) * _dot(os_ref[...], wbw_ref[...])
    merged = merged + jax.nn.sigmoid(_dot(xb, wg_ref[:, 2 * D_MODEL:3 * D_MODEL])) * _dot(om_ref[...], wbm_ref[...])
    h = x + _dot(merged.astype(BF16), wout_ref[...])
    h_ref[...] = h
    xn2 = _rms(h, g2_ref[...]).astype(BF16)
    xn2_ref[...] = _pack_halves(xn2)

    lt = lax.dot_general(wr_ref[...], xn2, _TRANS_B, preferred_element_type=F32) + br_ref[...]
    gl = lt[N_EXPERTS:N_EXPERTS + N_EXPERT_GROUPS]
    ge = jnp.exp(gl - jnp.max(gl, axis=0, keepdims=True))
    gp = ge / jnp.sum(ge, axis=0, keepdims=True)
    gw = jnp.max(gp, axis=0, keepdims=True)
    gidx = jnp.full((1, tt), N_EXPERT_GROUPS - 1, jnp.int32)
    for r in range(N_EXPERT_GROUPS - 2, -1, -1):
        gidx = jnp.where(gp[r:r + 1] == gw, r, gidx)
    ein = lt[(N_EXPERT_GROUPS - 1) * EXPERTS_PER_GROUP:N_EXPERTS]
    for r in range(N_EXPERT_GROUPS - 2, -1, -1):
        ein = jnp.where(gidx == r, lt[r * EXPERTS_PER_GROUP:(r + 1) * EXPERTS_PER_GROUP], ein)
    ee = jnp.exp(ein - jnp.max(ein, axis=0, keepdims=True))
    ep = ee / jnp.sum(ee, axis=0, keepdims=True)
    rowi = lax.broadcasted_iota(jnp.int32, (EXPERTS_PER_GROUP, tt), 0)
    p1 = jnp.max(ep, axis=0, keepdims=True)
    e1 = jnp.min(jnp.where(ep == p1, rowi, EXPERTS_PER_GROUP), axis=0, keepdims=True)
    ep2 = jnp.where(rowi == e1, -1.0, ep)
    p2 = jnp.max(ep2, axis=0, keepdims=True)
    e2 = jnp.min(jnp.where(ep2 == p2, rowi, EXPERTS_PER_GROUP), axis=0, keepdims=True)
    tot = p1 + p2
    w1 = p1 / tot * gw
    w2 = p2 / tot * gw
    id1 = (gidx * EXPERTS_PER_GROUP + e1).astype(F32)
    id2 = (gidx * EXPERTS_PER_GROUP + e2).astype(F32)
    route_ref[...] = jnp.concatenate([id1, id2, w1, w2, jnp.zeros((ROUTE_ROWS - 4, tt), F32)], axis=0)


def _merge(x, u, y, o_swa, o_mem, p, tile):
    n = x.shape[0]
    tile = min(tile, n)
    row = lambda i: (i, 0)
    const = lambda i: (0, 0)
    full = lambda a: pl.BlockSpec(a.shape, const, pipeline_mode=pl.Buffered(1))
    weights = [p['g1'], p['w_gates'], p['d_skip'], p['w_glu'], p['b_glu'], p['w_br_ssm'], p['w_br_swa'],
               p['w_br_mem'], p['w_out'], p['g2'], p['w_router'], p['b_router']]
    return pl.pallas_call(
        _merge_kernel,
        grid=(n // tile,),
        in_specs=[pl.BlockSpec((tile, D_MODEL), row), pl.BlockSpec((tile, SSM_WIDTH), row),
                  pl.BlockSpec((tile, SSM_WIDTH), row), pl.BlockSpec((tile, SWA_WIDTH), row),
                  pl.BlockSpec((tile, MEM_WIDTH), row)] + [full(w) for w in weights],
        out_specs=[pl.BlockSpec((tile, D_MODEL), row), pl.BlockSpec((tile, HALF), row),
                   pl.BlockSpec((ROUTE_ROWS, tile), lambda i: (0, i))],
        out_shape=[jax.ShapeDtypeStruct((n, D_MODEL), F32), jax.ShapeDtypeStruct((n, HALF), jnp.int32),
                   jax.ShapeDtypeStruct((ROUTE_ROWS, n), F32)],
        compiler_params=_cparams("parallel"),
        name="merge_router",
    )(x, u, y, o_swa, o_mem, *weights)


def _expert_mlp(xp, wg, wu, wd):
    hi, lo = _unpack_halves(xp)
    g = _dot(hi, wg[0:HALF, :]) + _dot(lo, wg[HALF:D_MODEL, :])
    u = _dot(hi, wu[0:HALF, :]) + _dot(lo, wu[HALF:D_MODEL, :])
    hh = jax.nn.silu(g) * u
    return _dot(hh.astype(BF16), wd[...])


def _moe_kernel(xn2_ref, rt_ref, wg_ref, wu_ref, wd_ref, h_ref, gf_ref, o_ref, acc_ref):
    e = pl.program_id(1)

    @pl.when(e == 0)
    def _():
        acc_ref[...] = jnp.zeros_like(acc_ref)

    o = _expert_mlp(xn2_ref[...], wg_ref[...].astype(BF16), wu_ref[...].astype(BF16), wd_ref[...].astype(BF16))
    ef = e.astype(F32)
    c = (jnp.where(rt_ref[:, 0:1] == ef, rt_ref[:, 2:3], 0.0)
         + jnp.where(rt_ref[:, 1:2] == ef, rt_ref[:, 3:4], 0.0))
    acc_ref[...] += c * o

    @pl.when(e == N_EXPERTS - 1)
    def _():
        o_ref[...] = _rms(h_ref[...] + acc_ref[...], gf_ref[...])


def _moe(xn2, route_t, w_g, w_u, w_d, h, gf, tile):
    n = h.shape[0]
    tile = min(tile, n)
    return pl.pallas_call(
        _moe_kernel,
        grid=(n // tile, N_EXPERTS),
        in_specs=[pl.BlockSpec((tile, HALF), lambda i, e: (i, 0)),
                  pl.BlockSpec((tile, ROUTE_ROWS), lambda i, e: (i, 0)),
                  pl.BlockSpec((None, D_MODEL, D_EXPERT), lambda i, e: (e, 0, 0)),
                  pl.BlockSpec((None, D_MODEL, D_EXPERT), lambda i, e: (e, 0, 0)),
                  pl.BlockSpec((None, D_EXPERT, D_MODEL), lambda i, e: (e, 0, 0)),
                  pl.BlockSpec((tile, D_MODEL), lambda i, e: (i, 0)),
                  pl.BlockSpec((1, D_MODEL), lambda i, e: (0, 0))],
        out_specs=pl.BlockSpec((tile, D_MODEL), lambda i, e: (i, 0)),
        out_shape=jax.ShapeDtypeStruct((n, D_MODEL), F32),
        scratch_shapes=[pltpu.VMEM((tile, D_MODEL), F32)],
        compiler_params=_cparams("parallel", "arbitrary"),
        name="moe_final_norm",
    )(xn2, route_t, w_g, w_u, w_d, h, gf)


EXPERT_ROW_TILE = 1024
SC_CORES = 2
SC_SUBCORES = 16
SC_WORKERS = SC_CORES * SC_SUBCORES
SC_SCATTER_ROWS = 64
SC_GATHER_ROWS = 64


def _route_rank_kernel(r_ref, rank_ref, cnt_ref, base_ref):
    i = pl.program_id(0)
    tt = r_ref.shape[1]

    @pl.when(i == 0)
    def _():
        base_ref[...] = jnp.zeros_like(base_ref)

    ids = r_ref[0:2, :].astype(jnp.int32)
    e_iota = lax.broadcasted_iota(jnp.int32, (N_EXPERTS, tt), 0)
    oh1 = jnp.where(e_iota == ids[0:1], 1.0, 0.0)
    oh2 = jnp.where(e_iota == ids[1:2], 1.0, 0.0)
    before = (lax.broadcasted_iota(jnp.int32, (tt, tt), 0) < lax.broadcasted_iota(jnp.int32, (tt, tt), 1))
    tri = jnp.where(before, 1.0, 0.0).astype(BF16)
    c1 = _dot(oh1.astype(BF16), tri)
    c2 = _dot(oh2.astype(BF16), tri)
    tot1 = jnp.sum(oh1, axis=1, keepdims=True)
    tot2 = jnp.sum(oh2, axis=1, keepdims=True)
    base = base_ref[:, 0:1]
    rank1 = jnp.sum(oh1 * (base + c1), axis=0, keepdims=True)
    rank2 = jnp.sum(oh2 * (base + tot1 + c2), axis=0, keepdims=True)
    rank_ref[...] = jnp.concatenate([rank1, rank2, jnp.zeros((ROUTE_ROWS - 2, tt), F32)], axis=0).astype(jnp.int32)
    new_base = jnp.broadcast_to(base + tot1 + tot2, base_ref.shape)
    base_ref[...] = new_base
    cnt_ref[...] = new_base.astype(jnp.int32)


def _route_rank(route, tile):
    n = route.shape[1]
    tile = min(tile, n)
    return pl.pallas_call(
        _route_rank_kernel,
        grid=(n // tile,),
        in_specs=[pl.BlockSpec((ROUTE_ROWS, tile), lambda i: (0, i))],
        out_specs=[pl.BlockSpec((ROUTE_ROWS, tile), lambda i: (0, i)),
                   pl.BlockSpec((N_EXPERTS, LANES), lambda i: (0, 0))],
        out_shape=[jax.ShapeDtypeStruct((ROUTE_ROWS, n), jnp.int32),
                   jax.ShapeDtypeStruct((N_EXPERTS, LANES), jnp.int32)],
        scratch_shapes=[pltpu.VMEM((N_EXPERTS, LANES), F32)],
        compiler_params=_cparams("arbitrary"),
        name="route_rank",
    )(route)


def _sc_mesh():
    return plsc.VectorSubcoreMesh(core_axis_name="core", subcore_axis_name="subcore")


def _sc_scatter_pairs(x, pos, rows_out):
    n, d = x.shape
    per_w = n // SC_WORKERS
    window = min(SC_SCATTER_ROWS, per_w)

    @pl.kernel(out_type=jax.ShapeDtypeStruct((rows_out, d), x.dtype), mesh=_sc_mesh(),
               scratch_types=[pltpu.VMEM((window,), jnp.int32), pltpu.VMEM((window,), jnp.int32),
                              pltpu.VMEM((window, d), x.dtype), pltpu.SemaphoreType.DMA, pltpu.SemaphoreType.DMA,
                              pltpu.SemaphoreType.DMA])
    def scatter(x_hbm, p_hbm, o_hbm, i1_v, i2_v, rows_v, sem_a, sem_b, sem_c):
        wid = lax.axis_index("subcore") * SC_CORES + lax.axis_index("core")

        @pl.loop(0, per_w // window)
        def _(j):
            base = wid * per_w + j * window
            load_i1 = pltpu.async_copy(p_hbm.at[pl.ds(base, window)], i1_v, sem_a)
            load_i2 = pltpu.async_copy(p_hbm.at[pl.ds(n + base, window)], i2_v, sem_b)
            load_x = pltpu.async_copy(x_hbm.at[pl.ds(base, window)], rows_v, sem_c)
            load_i1.wait()
            load_i2.wait()
            load_x.wait()
            put_1 = pltpu.async_copy(rows_v, o_hbm.at[i1_v], sem_a)
            put_2 = pltpu.async_copy(rows_v, o_hbm.at[i2_v], sem_b)
            put_1.wait()
            put_2.wait()

    return scatter(x, pos)


def _sc_gather_rows(table, idx):
    m = idx.shape[0]
    d = table.shape[1]
    per_w = m // SC_WORKERS
    window = min(SC_GATHER_ROWS, per_w)

    @pl.kernel(out_type=jax.ShapeDtypeStruct((m, d), table.dtype), mesh=_sc_mesh(),
               scratch_types=[pltpu.VMEM((window,), jnp.int32), pltpu.VMEM((window, d), table.dtype),
                              pltpu.SemaphoreType.DMA])
    def gather(t_hbm, i_hbm, o_hbm, i_v, rows_v, sem):
        wid = lax.axis_index("subcore") * SC_CORES + lax.axis_index("core")

        @pl.loop(0, per_w // window)
        def _(j):
            base = wid * per_w + j * window
            pltpu.sync_copy(i_hbm.at[pl.ds(base, window)], i_v)
            pltpu.async_copy(t_hbm.at[i_v], rows_v, sem).wait()
            pltpu.sync_copy(rows_v, o_hbm.at[pl.ds(base, window)])

    return gather(table, idx)


def _expert_tiles_kernel(te_ref, nu_ref, x_ref, wg_ref, wu_ref, wd_ref, o_ref, wg_s, wu_s, wd_s):
    i = pl.program_id(0)

    @pl.when(i < nu_ref[0])
    def _():
        @pl.when(jnp.logical_or(i == 0, te_ref[i] != te_ref[jnp.maximum(i - 1, 0)]))
        def _():
            wg_s[...] = wg_ref[...].astype(BF16)
            wu_s[...] = wu_ref[...].astype(BF16)
            wd_s[...] = wd_ref[...].astype(BF16)

        o_ref[...] = _pack_halves(_expert_mlp(x_ref[...], wg_s, wu_s, wd_s).astype(BF16))


def _expert_tiles(tile_expert, n_used, xs, w_g, w_u, w_d):
    rows = xs.shape[0]
    tm = EXPERT_ROW_TILE
    grid_spec = pltpu.PrefetchScalarGridSpec(
        num_scalar_prefetch=2,
        grid=(rows // tm,),
        in_specs=[pl.BlockSpec((tm, HALF), lambda i, te, nu: (i, 0)),
                  pl.BlockSpec((None, D_MODEL, D_EXPERT), lambda i, te, nu: (te[i], 0, 0)),
                  pl.BlockSpec((None, D_MODEL, D_EXPERT), lambda i, te, nu: (te[i], 0, 0)),
                  pl.BlockSpec((None, D_EXPERT, D_MODEL), lambda i, te, nu: (te[i], 0, 0))],
        out_specs=pl.BlockSpec((tm, HALF), lambda i, te, nu: (i, 0)),
        scratch_shapes=[pltpu.VMEM((D_MODEL, D_EXPERT), BF16), pltpu.VMEM((D_MODEL, D_EXPERT), BF16),
                        pltpu.VMEM((D_EXPERT, D_MODEL), BF16)],
    )
    return pl.pallas_call(
        _expert_tiles_kernel,
        grid_spec=grid_spec,
        out_shape=jax.ShapeDtypeStruct((rows, HALF), jnp.int32),
        compiler_params=_cparams("arbitrary"),
        name="expert_tiles",
    )(tile_expert, n_used, xs, w_g, w_u, w_d)


def _unpack_f32(p):
    return pltpu.bitcast(p & jnp.int32(-65536), F32), pltpu.bitcast(lax.shift_left(p, jnp.int32(16)), F32)


def _combine_kernel(h_ref, o1_ref, o2_ref, rt_ref, gf_ref, y_ref):
    w1, w2 = rt_ref[:, 2:3], rt_ref[:, 3:4]
    a_lo, a_hi = _unpack_f32(o1_ref[...])
    b_lo, b_hi = _unpack_f32(o2_ref[...])
    y_lo = h_ref[:, 0:HALF] + (w1 * a_lo + w2 * b_lo)
    y_hi = h_ref[:, HALF:D_MODEL] + (w1 * a_hi + w2 * b_hi)
    ms = (jnp.sum(y_lo * y_lo, axis=-1, keepdims=True) + jnp.sum(y_hi * y_hi, axis=-1, keepdims=True)) / D_MODEL
    inv = lax.rsqrt(ms + EPS)
    y_ref[:, 0:HALF] = (y_lo * inv) * gf_ref[:, 0:HALF]
    y_ref[:, HALF:D_MODEL] = (y_hi * inv) * gf_ref[:, HALF:D_MODEL]


def _combine(h, o12, route_t, gf, tile):
    n = h.shape[0]
    tile = min(tile, n)
    nt = n // tile
    return pl.pallas_call(
        _combine_kernel,
        grid=(nt,),
        in_specs=[pl.BlockSpec((tile, D_MODEL), lambda i: (i, 0)),
                  pl.BlockSpec((tile, HALF), lambda i: (i, 0)),
                  pl.BlockSpec((tile, HALF), lambda i: (i + nt, 0)),
                  pl.BlockSpec((tile, ROUTE_ROWS), lambda i: (i, 0)),
                  pl.BlockSpec((1, D_MODEL), lambda i: (0, 0))],
        out_specs=pl.BlockSpec((tile, D_MODEL), lambda i: (i, 0)),
        out_shape=jax.ShapeDtypeStruct((n, D_MODEL), F32),
        compiler_params=_cparams("parallel"),
        name="combine_final_norm",
    )(h, o12, o12, route_t, gf)


def _sparse_moe(xn2p, route, h, w_g, w_u, w_d, gf):
    n = h.shape[0]
    tm = EXPERT_ROW_TILE
    rows = 2 * n + N_EXPERTS * tm
    rank, cnt = _route_rank(route, 512)
    counts = cnt[:, 0]
    padded = (counts + tm - 1) // tm * tm
    e_idx = jnp.arange(N_EXPERTS, dtype=jnp.int32)
    starts = jnp.sum(jnp.where(e_idx[None, :] < e_idx[:, None], padded[None, :], 0), axis=1)
    ends = starts + padded
    ids = route[0:2].astype(jnp.int32)
    start_of = jnp.sum(jnp.where(ids[None] == e_idx[:, None, None], starts[:, None, None], 0), axis=0)
    pos = (start_of + rank[0:2]).reshape(2 * n)
    tile_start = jnp.arange(rows // tm, dtype=jnp.int32) * tm
    tile_expert = jnp.minimum(jnp.sum((tile_start[:, None] >= ends[None, :]).astype(jnp.int32), axis=1),
                              N_EXPERTS - 1)
    n_used = (ends[-1:] // tm).astype(jnp.int32)
    xs = _sc_scatter_pairs(xn2p, pos, rows)
    os_ = _expert_tiles(tile_expert, n_used, xs, w_g, w_u, w_d)
    o12 = _sc_gather_rows(os_, pos)
    return _combine(h, o12, route.T, gf, 512)


def _prep_in_weights(w_in):
    o = 0
    w_u = w_in[:, o:o + SSM_WIDTH]; o += SSM_WIDTH
    w_q = w_in[:, o:o + SWA_WIDTH]; o += SWA_WIDTH
    w_k = w_in[:, o:o + SWA_KV_WIDTH]; o += SWA_KV_WIDTH
    w_v = w_in[:, o:o + SWA_KV_WIDTH]; o += SWA_KV_WIDTH
    w_qm = w_in[:, o:o + MEM_WIDTH]; o += MEM_WIDTH
    w_g = w_in[:, o:]
    wq = (w_q * (SWA_HEAD_DIM ** -0.5)).reshape(D_MODEL, SWA_KV_HEADS, SWA_REP, SWA_HEAD_DIM)
    wq = wq.transpose(0, 2, 1, 3).reshape(D_MODEL, SWA_WIDTH)
    w_main = jnp.concatenate([w_u, wq, w_k, w_v, w_qm], axis=1).astype(BF16)
    return w_main, w_g.astype(BF16)


IN_SPLITS = (SSM_WIDTH, SWA_WIDTH, SWA_KV_WIDTH, SWA_KV_WIDTH, MEM_WIDTH)
IN_DTYPES = ((F32, BF16), (BF16,), (F32,), (F32,), (BF16,))


def kernel(x_prompt, x_sample, cache_swa_k, cache_swa_v, state_ssm_re, state_ssm_im, cache_mem_k, cache_mem_v, mem_prompt, norm1_g, w_in, lam_re, lam_im, log_dt, bm_re, bm_im, cm_re, cm_im, d_skip, w_glu, b_glu, sinks, rel_table, mem_norm_g, w_mem_kv, w_br_ssm, w_br_swa, w_br_mem, w_out, norm2_g, w_rg, b_rg, w_rexp, b_rexp, w_e_gate, w_e_up, w_e_down, final_norm_g):
    nb, t, _ = x_prompt.shape
    ns, ts, _ = x_sample.shape
    l = 0
    L = S5_CHUNK

    w_main, w_gates = _prep_in_weights(w_in[l])
    w_swa = (w_br_swa[l].reshape(SWA_KV_HEADS, SWA_REP, SWA_HEAD_DIM, D_MODEL).transpose(1, 0, 2, 3)
             .reshape(SWA_WIDTH, D_MODEL))
    pad_rows = ROUTER_ROWS - N_EXPERTS - N_EXPERT_GROUPS
    w_router = jnp.concatenate([w_rexp[l].T, w_rg[l].T, jnp.zeros((pad_rows, D_MODEL), F32)], axis=0).astype(BF16)
    b_router = jnp.concatenate([b_rexp[l], b_rg[l], jnp.zeros((pad_rows,), F32)]).reshape(ROUTER_ROWS, 1)
    mp = {
        'g1': norm1_g[l].reshape(1, D_MODEL), 'w_gates': w_gates, 'd_skip': d_skip[l].reshape(1, SSM_WIDTH),
        'w_glu': w_glu[l].astype(BF16), 'b_glu': b_glu[l].reshape(1, SSM_WIDTH),
        'w_br_ssm': w_br_ssm[l].astype(BF16), 'w_br_swa': w_swa.astype(BF16),
        'w_br_mem': w_br_mem[l].astype(BF16), 'w_out': w_out[l].astype(BF16),
        'g2': norm2_g[l].reshape(1, D_MODEL), 'w_router': w_router, 'b_router': b_router,
    }
    w_g, w_u, w_d = w_e_gate[l], w_e_up[l], w_e_down[l]
    gf = final_norm_g.reshape(1, D_MODEL)
    s5_w = _s5_weights(lam_re[l], lam_im[l], log_dt[l], bm_re[l], bm_im[l], cm_re[l], cm_im[l], L)

    bias_p = _rel_bias(rel_table, np.arange(WINDOW)[:, None] + WINDOW - np.arange(2 * WINDOW)[None, :])
    keys_s = WINDOW + 2 * ts
    bias_s = _rel_bias(rel_table, np.arange(ts)[:, None] + WINDOW - np.arange(keys_s)[None, :])
    bias_s = bias_s.reshape(SWA_HEADS * ts, keys_s)
    sink_rows = jnp.repeat(sinks[l].astype(F32), ts).reshape(SWA_HEADS * ts, 1)

    n = nb * t
    xp = x_prompt.reshape(n, D_MODEL)
    mk, mv = _norm_proj(mem_prompt.reshape(nb * MEM_TOKENS, D_MODEL), mem_norm_g[l].reshape(1, D_MODEL),
                        w_mem_kv[l].astype(BF16), (MEM_WIDTH, MEM_WIDTH), ((F32,), (F32,)), 512)
    u, ub, qz, k, v, qm = _norm_proj(xp, mp['g1'], w_main, IN_SPLITS, IN_DTYPES, 512)

    y_ssm, fin = _s5(ub, jnp.zeros((nb, N_CH_TILES * 2 * STATE_TILE), F32), s5_w, nb, t // L, L, 64)
    p_re, p_im = _tiles_to_state(fin)

    o_swa = _swa_prompt(qz, k, v, bias_p, sinks[l].astype(F32), nb, t, 4)
    o_mem = _mem_prompt(qm, mk, mv, nb, t, 512)
    h, xn2p, route = _merge(xp, u, y_ssm, o_swa, o_mem, mp, 512)
    y_prompt = _sparse_moe(xn2p, route, h, w_g, w_u, w_d, gf).reshape(nb, t, D_MODEL)

    k4 = k.reshape(nb, t, SWA_KV_HEADS, SWA_HEAD_DIM)
    v4 = v.reshape(nb, t, SWA_KV_HEADS, SWA_HEAD_DIM)
    new_k_p, new_v_p = k4[:, -WINDOW:][None], v4[:, -WINDOW:][None]
    new_mk = mk.reshape(1, nb, MEM_TOKENS, MEM_HEADS, MEM_HEAD_DIM)
    new_mv = mv.reshape(1, nb, MEM_TOKENS, MEM_HEADS, MEM_HEAD_DIM)

    m = ns * ts
    xs = x_sample.reshape(m, D_MODEL)
    us, ubs, qzs, k_s, v_s, qms = _norm_proj(xs, mp['g1'], w_main, IN_SPLITS, IN_DTYPES, 256)
    ys_ssm, fins = _s5(ubs, _state_to_tiles(state_ssm_re[l], state_ssm_im[l]), s5_w, ns, ts // L, L, 64)
    s_re, s_im = _tiles_to_state(fins)

    kk_all = jnp.concatenate([cache_swa_k[l].reshape(ns, WINDOW, SWA_KV_WIDTH).astype(F32),
                              k_s.reshape(ns, ts, SWA_KV_WIDTH)], axis=1)
    vv_all = jnp.concatenate([cache_swa_v[l].reshape(ns, WINDOW, SWA_KV_WIDTH).astype(F32),
                              v_s.reshape(ns, ts, SWA_KV_WIDTH)], axis=1)
    pad = jnp.zeros((ns, keys_s - WINDOW - ts, SWA_KV_WIDTH), F32)
    q5 = qzs.reshape(ns, ts, SWA_REP, SWA_KV_HEADS, SWA_HEAD_DIM)
    zq = jnp.zeros((ns, ts, SWA_REP, SWA_HEAD_DIM), BF16)
    q_rows = jnp.concatenate([jnp.concatenate([q5[:, :, :, 0], zq], axis=-1),
                              jnp.concatenate([zq, q5[:, :, :, 1]], axis=-1)], axis=2)
    q_rows = q_rows.transpose(0, 2, 1, 3).reshape(ns, SWA_HEADS * ts, LANES)
    o_dec = _swa_decode(q_rows, jnp.concatenate([kk_all, pad], axis=1), jnp.concatenate([vv_all, pad], axis=1),
                        bias_s, sink_rows, ts, 8)
    o_dec = o_dec.reshape(ns, SWA_KV_HEADS, SWA_REP, ts, SWA_KV_HEADS, SWA_HEAD_DIM)
    o_dec = jnp.stack([o_dec[:, g, :, :, g] for g in range(SWA_KV_HEADS)], axis=1)
    o_swa_s = o_dec.transpose(0, 3, 2, 1, 4).reshape(m, SWA_WIDTH).astype(BF16)

    o_mem_s = _mem_decode(qms.astype(F32).reshape(ns, ts, MEM_WIDTH), cache_mem_k, cache_mem_v, l, 4)
    o_mem_s = o_mem_s.reshape(m, MEM_WIDTH).astype(BF16)

    hs_, xn2ps, routes = _merge(xs, us, ys_ssm, o_swa_s, o_mem_s, mp, 256)
    y_sample = _moe(xn2ps, routes.T, w_g, w_u, w_d, hs_, gf, 1024).reshape(ns, ts, D_MODEL)

    new_k_s = kk_all[:, -WINDOW:].reshape(1, ns, WINDOW, SWA_KV_HEADS, SWA_HEAD_DIM).astype(cache_swa_k.dtype)
    new_v_s = vv_all[:, -WINDOW:].reshape(1, ns, WINDOW, SWA_KV_HEADS, SWA_HEAD_DIM).astype(cache_swa_v.dtype)

    return (y_prompt, y_sample,
            new_k_p, new_v_p, p_re[None], p_im[None], new_mk, new_mv,
            new_k_s, new_v_s, s_re[None].astype(state_ssm_re.dtype), s_im[None].astype(state_ssm_im.dtype))
```

```python
import functools
import math

import numpy as np
import jax
import jax.numpy as jnp
from jax import lax
from jax.experimental import pallas as pl
from jax.experimental.pallas import tpu as pltpu
from jax.experimental.pallas import tpu_sc as plsc

F32 = jnp.float32
BF16 = jnp.bfloat16

D_MODEL = 1024
SSM_WIDTH = 512
SSM_GROUP = 16
SSM_GROUPS = 32
SSM_STATE = 64
SWA_HEADS = 8
SWA_KV_HEADS = 2
SWA_REP = 4
SWA_HEAD_DIM = 64
SWA_WIDTH = 512
SWA_KV_WIDTH = 128
WINDOW = 128
REL_BUCKETS = 32
REL_MAX_DIST = 128
MEM_TOKENS = 256
MEM_HEADS = 4
MEM_HEAD_DIM = 128
MEM_WIDTH = 512
N_EXPERT_GROUPS = 4
EXPERTS_PER_GROUP = 8
N_EXPERTS = 32
D_EXPERT = 256
EPS = 1e-6
NEG_INF = -1e30

LANES = 128
GROUPS_PER_TILE = LANES // SSM_GROUP
N_CH_TILES = SSM_WIDTH // LANES
STATE_TILE = GROUPS_PER_TILE * SSM_STATE
VMEM_LIMIT = 56 * 1024 * 1024
S5_CHUNK = 8

_TRANS_B = (((1,), (1,)), ((), ()))


def _cparams(*sem):
    return pltpu.CompilerParams(dimension_semantics=sem, vmem_limit_bytes=VMEM_LIMIT)


def _rms(x, g):
    return (x * lax.rsqrt(jnp.mean(x * x, axis=-1, keepdims=True) + EPS)) * g


def _dot(a, b):
    return jnp.dot(a, b, preferred_element_type=F32)


def _norm_proj_kernel(x_ref, g_ref, w_ref, *out_refs, splits, dtypes):
    xb = _rms(x_ref[...], g_ref[...]).astype(BF16)
    off = 0
    outs = iter(out_refs)
    for width, dts in zip(splits, dtypes):
        r = _dot(xb, w_ref[:, off:off + width])
        for dt in dts:
            next(outs)[...] = r.astype(dt)
        off += width


def _norm_proj(x, g, w, splits, dtypes, tile):
    n, d = x.shape
    tile = min(tile, n)
    flat = [(wd, dt) for wd, dts in zip(splits, dtypes) for dt in dts]
    return pl.pallas_call(
        functools.partial(_norm_proj_kernel, splits=tuple(splits), dtypes=tuple(dtypes)),
        grid=(n // tile,),
        in_specs=[pl.BlockSpec((tile, d), lambda i: (i, 0)),
                  pl.BlockSpec((1, d), lambda i: (0, 0)),
                  pl.BlockSpec((d, sum(splits)), lambda i: (0, 0))],
        out_specs=[pl.BlockSpec((tile, wd), lambda i: (i, 0)) for wd, _ in flat],
        out_shape=[jax.ShapeDtypeStruct((n, wd), dt) for wd, dt in flat],
        compiler_params=_cparams("parallel"),
        name="norm_proj",
    )(x, g, w)


def _s5_weights(lam_re, lam_im, log_dt, bm_re, bm_im, cm_re, cm_im, L):
    hp = lax.Precision.HIGHEST
    nt, gt, P, H = N_CH_TILES, GROUPS_PER_TILE, SSM_STATE, SSM_GROUP
    lr, li = lam_re.astype(F32), lam_im.astype(F32)
    dt = jnp.exp(log_dt.astype(F32))[:, None]
    mag = jnp.exp(lr * dt)
    a_re = mag * jnp.cos(li * dt)
    a_im = mag * jnp.sin(li * dt)
    den = lr * lr + li * li
    f_re = ((a_re - 1.0) * lr + a_im * li) / den
    f_im = (a_im * lr - (a_re - 1.0) * li) / den
    br, bi = bm_re.astype(F32), bm_im.astype(F32)
    bb_re = f_re[..., None] * br - f_im[..., None] * bi
    bb_im = f_re[..., None] * bi + f_im[..., None] * br
    pr, pi = [jnp.ones_like(a_re)], [jnp.zeros_like(a_im)]
    for _ in range(L):
        pr.append(pr[-1] * a_re - pi[-1] * a_im)
        pi.append(pr[-2] * a_im + pi[-1] * a_re)
    ap_re, ap_im = jnp.stack(pr), jnp.stack(pi)
    cr, ci = cm_re.astype(F32), cm_im.astype(F32)
    ca_re = cr[None] * ap_re[:, :, None, :] - ci[None] * ap_im[:, :, None, :]
    ca_im = cr[None] * ap_im[:, :, None, :] + ci[None] * ap_re[:, :, None, :]

    rev_re = jnp.stack([pr[L - 1 - s] for s in range(L)])
    rev_im = jnp.stack([pi[L - 1 - s] for s in range(L)])
    ws_re = rev_re[..., None] * bb_re[None] - rev_im[..., None] * bb_im[None]
    ws_im = rev_re[..., None] * bb_im[None] + rev_im[..., None] * bb_re[None]
    c_st = jnp.concatenate([ws_re.transpose(0, 1, 3, 2).reshape(L, nt, gt * H, P),
                            ws_im.transpose(0, 1, 3, 2).reshape(L, nt, gt * H, P)], axis=3).transpose(1, 0, 2, 3)
    so = lambda ca: ca[1:].transpose(1, 3, 0, 2).reshape(nt, gt * P, L * H)
    c_so = jnp.concatenate([so(ca_re), so(-ca_im)], axis=1)
    k_lag = (jnp.einsum('tghp,gpk->gkth', ca_re[:L], bb_re, precision=hp)
             - jnp.einsum('tghp,gpk->gkth', ca_im[:L], bb_im, precision=hp))
    c_k = k_lag.reshape(nt, gt * H, L * H)
    w_st, w_out, toep = _s5_expand(c_st, c_so, c_k, L)

    def per_tile(v):
        return v.reshape(nt, 1, STATE_TILE)

    return w_st, w_out, toep, per_tile(pr[L]), per_tile(pi[L])


def _s5_expand_kernel(cst_ref, cso_ref, ck_ref, wst_ref, wso_ref, toep_ref, *, L):
    hp = lax.Precision.HIGHEST
    P, H = SSM_STATE, SSM_GROUP
    iota = lambda shape, d: lax.broadcasted_iota(jnp.int32, shape, d)
    one = lambda cond: jnp.where(cond, 1.0, 0.0).astype(F32)

    r, c = iota((2 * P, 2 * STATE_TILE), 0), iota((2 * P, 2 * STATE_TILE), 1)
    rep_st = one((r // P == c // STATE_TILE) & (r % P == c % P))
    r, c = iota((LANES, 2 * STATE_TILE), 0), iota((LANES, 2 * STATE_TILE), 1)
    own_st = one(r // H == (c % STATE_TILE) // P)
    for s in range(L):
        blk = jnp.dot(cst_ref[s], rep_st, precision=hp, preferred_element_type=F32) * own_st
        wst_ref[s * LANES:(s + 1) * LANES, :] = blk.astype(BF16)

    r, c = iota((LANES, LANES), 0), iota((LANES, LANES), 1)
    pick = [one((r // H == t) & (r % H == c % H)) for t in range(L)]
    own_k = one(r // H == c // H)
    r, c = iota((2 * STATE_TILE, LANES), 0), iota((2 * STATE_TILE, LANES), 1)
    own_so = one((r % STATE_TILE) // P == c // H)
    cso = cso_ref[...]
    for t in range(L):
        blk = jnp.dot(cso, pick[t], precision=hp, preferred_element_type=F32) * own_so
        wso_ref[:, t * LANES:(t + 1) * LANES] = blk.astype(BF16)
    ck = ck_ref[...]
    lag = [(jnp.dot(ck, pick[t], precision=hp, preferred_element_type=F32) * own_k).astype(BF16) for t in range(L)]
    zero = jnp.zeros((LANES, LANES), BF16)
    for s in range(L):
        for t in range(L):
            toep_ref[s * LANES:(s + 1) * LANES, t * LANES:(t + 1) * LANES] = lag[t - s] if t >= s else zero


def _s5_expand(c_st, c_so, c_k, L):
    lk = L * LANES
    st2 = 2 * STATE_TILE
    return pl.pallas_call(
        functools.partial(_s5_expand_kernel, L=L),
        grid=(N_CH_TILES,),
        in_specs=[pl.BlockSpec((None, L, LANES, 2 * SSM_STATE), lambda j: (j, 0, 0, 0)),
                  pl.BlockSpec((None, st2, L * SSM_GROUP), lambda j: (j, 0, 0)),
                  pl.BlockSpec((None, LANES, L * SSM_GROUP), lambda j: (j, 0, 0))],
        out_specs=[pl.BlockSpec((None, lk, st2), lambda j: (j, 0, 0)),
                   pl.BlockSpec((None, st2, lk), lambda j: (j, 0, 0)),
                   pl.BlockSpec((None, lk, lk), lambda j: (j, 0, 0))],
        out_shape=[jax.ShapeDtypeStruct((N_CH_TILES, lk, st2), BF16),
                   jax.ShapeDtypeStruct((N_CH_TILES, st2, lk), BF16),
                   jax.ShapeDtypeStruct((N_CH_TILES, lk, lk), BF16)],
        compiler_params=_cparams("parallel"),
        name="s5_expand_weights",
    )(c_st, c_so, c_k)


def _to_chunks(u, nb, nc, L):
    return (u.reshape(nb, nc, L, N_CH_TILES, LANES).transpose(1, 0, 3, 2, 4)
            .reshape(nc * nb, N_CH_TILES * L * LANES))


def _from_chunks(y, nb, nc, L):
    return (y.reshape(nc, nb, N_CH_TILES, L, LANES).transpose(1, 0, 3, 2, 4)
            .reshape(nb * nc * L, SSM_WIDTH))


def _s5_kernel(x_ref, h0_ref, are_ref, aim_ref, ws_ref, t_ref, wo_ref, y_ref, fin_ref,
               hr_ref, hi_ref, d_ref, hs_ref, *, cb, nb):
    ci = pl.program_id(1)

    @pl.when(ci == 0)
    def _():
        hr_ref[...] = h0_ref[:, 0:STATE_TILE]
        hi_ref[...] = h0_ref[:, STATE_TILE:2 * STATE_TILE]

    x = x_ref[...]
    d_ref[...] = _dot(x, ws_ref[...])
    ar = jnp.broadcast_to(are_ref[...], (nb, STATE_TILE))
    ai = jnp.broadcast_to(aim_ref[...], (nb, STATE_TILE))

    def body(c, carry):
        hr, hi = carry
        r0 = pl.multiple_of(c * nb, nb)
        hs_ref[pl.ds(r0, nb), 0:STATE_TILE] = hr
        hs_ref[pl.ds(r0, nb), STATE_TILE:2 * STATE_TILE] = hi
        d = d_ref[pl.ds(r0, nb), :]
        return (ar * hr - ai * hi + d[:, 0:STATE_TILE],
                ar * hi + ai * hr + d[:, STATE_TILE:2 * STATE_TILE])

    hr, hi = lax.fori_loop(0, cb, body, (hr_ref[...], hi_ref[...]))
    hr_ref[...] = hr
    hi_ref[...] = hi
    y_ref[...] = _dot(x, t_ref[...]) + _dot(hs_ref[...].astype(BF16), wo_ref[...])

    @pl.when(ci == pl.num_programs(1) - 1)
    def _():
        fin_ref[:, 0:STATE_TILE] = hr
        fin_ref[:, STATE_TILE:2 * STATE_TILE] = hi


def _s5(ub, h0, weights, nb, nc, L, chunk_block):
    w_st, w_so, toep, a_re, a_im = weights
    xc = _to_chunks(ub, nb, nc, L)
    cb = min(chunk_block, nc)
    rows = cb * nb
    lk = L * LANES
    st2 = 2 * STATE_TILE
    tile_w = lambda shape: pl.BlockSpec((None,) + shape, lambda j, c: (j, 0, 0))
    y, fin = pl.pallas_call(
        functools.partial(_s5_kernel, cb=cb, nb=nb),
        grid=(N_CH_TILES, nc // cb),
        in_specs=[pl.BlockSpec((rows, lk), lambda j, c: (c, j)),
                  pl.BlockSpec((nb, st2), lambda j, c: (0, j)),
                  tile_w((1, STATE_TILE)), tile_w((1, STATE_TILE)),
                  tile_w((lk, st2)), tile_w((lk, lk)), tile_w((st2, lk))],
        out_specs=[pl.BlockSpec((rows, lk), lambda j, c: (c, j)),
                   pl.BlockSpec((nb, st2), lambda j, c: (0, j))],
        out_shape=[jax.ShapeDtypeStruct((nc * nb, N_CH_TILES * lk), F32),
                   jax.ShapeDtypeStruct((nb, N_CH_TILES * st2), F32)],
        scratch_shapes=[pltpu.VMEM((nb, STATE_TILE), F32), pltpu.VMEM((nb, STATE_TILE), F32),
                        pltpu.VMEM((rows, st2), F32), pltpu.VMEM((rows, st2), F32)],
        compiler_params=_cparams("parallel", "arbitrary"),
        name="s5_chunked_scan",
    )(xc, h0, a_re, a_im, w_st, toep, w_so)
    return _from_chunks(y, nb, nc, L), fin


def _state_to_tiles(h_re, h_im):
    nb = h_re.shape[0]
    r = h_re.astype(F32).reshape(nb, N_CH_TILES, STATE_TILE)
    i = h_im.astype(F32).reshape(nb, N_CH_TILES, STATE_TILE)
    return jnp.concatenate([r, i], axis=-1).reshape(nb, N_CH_TILES * 2 * STATE_TILE)


def _tiles_to_state(h):
    nb = h.shape[0]
    h = h.reshape(nb, N_CH_TILES, 2, GROUPS_PER_TILE, SSM_STATE)
    return (h[:, :, 0].reshape(nb, SSM_GROUPS, SSM_STATE), h[:, :, 1].reshape(nb, SSM_GROUPS, SSM_STATE))


def _t5_bucket(dist):
    n = np.maximum(dist, 0)
    max_exact = REL_BUCKETS // 2
    nf = np.maximum(n, 1).astype(np.float32)
    large = max_exact + (np.log(nf / np.float32(max_exact)) / np.float32(math.log(REL_MAX_DIST / max_exact))
                         * np.float32(REL_BUCKETS - max_exact)).astype(np.int32)
    large = np.minimum(large, REL_BUCKETS - 1)
    return np.where(n < max_exact, n, large)


def _rel_bias(rel_table, dist):
    bucket = _t5_bucket(dist)
    tab = rel_table.astype(F32)
    out = jnp.zeros((SWA_HEADS,) + dist.shape, F32)
    for b in range(REL_BUCKETS):
        sel = jnp.asarray(bucket == b)
        if bool((bucket == b).any()):
            out = jnp.where(sel[None], tab[b].reshape((SWA_HEADS,) + (1,) * dist.ndim), out)
    return out


def _softmax_sink(s, sink):
    m = jnp.maximum(jnp.max(s, axis=-1, keepdims=True), sink)
    e = jnp.exp(s - m)
    den = jnp.sum(e, axis=-1, keepdims=True) + jnp.exp(sink - m)
    return e * (1.0 / den)


def _swa_prompt_kernel(sink_ref, q_ref, kp_ref, kc_ref, vp_ref, vc_ref, bias_ref, o_ref, kk_ref, vv_ref, *, qblocks):
    step = pl.program_id(1)
    kk_ref[0:WINDOW, :] = kp_ref[...].astype(BF16)
    kk_ref[WINDOW:, :] = kc_ref[...].astype(BF16)
    vv_ref[0:WINDOW, :] = vp_ref[...].astype(BF16)
    vv_ref[WINDOW:, :] = vc_ref[...].astype(BF16)
    row = lax.broadcasted_iota(jnp.int32, (WINDOW, 2 * WINDOW), 0)
    col = lax.broadcasted_iota(jnp.int32, (WINDOW, 2 * WINDOW), 1)
    dist = row + WINDOW - col
    band = (dist >= 0) & (dist < WINDOW)
    lane = lax.broadcasted_iota(jnp.int32, (WINDOW, LANES), 1)
    low = lane < SWA_HEAD_DIM

    def block(j, carry):
        r0 = pl.multiple_of(j * WINDOW, WINDOW)
        kk = kk_ref[pl.ds(r0, 2 * WINDOW), :]
        vv = vv_ref[pl.ds(r0, 2 * WINDOW), :]
        valid = band & ((col >= WINDOW) | (step * qblocks + j > 0))
        for t in range(SWA_REP):
            q2 = q_ref[pl.ds(r0, WINDOW), t * LANES:(t + 1) * LANES]
            outs = []
            for half in range(SWA_KV_HEADS):
                h = t + SWA_REP * half
                qh = jnp.where(low if half == 0 else jnp.logical_not(low), q2, jnp.zeros_like(q2))
                s = lax.dot_general(qh, kk, _TRANS_B, preferred_element_type=F32)
                s = jnp.where(valid, s + bias_ref[h], NEG_INF)
                sink = sink_ref[h]
                m = jnp.maximum(jnp.max(s, axis=-1, keepdims=True), sink)
                e = jnp.exp(s - m)
                den = jnp.sum(e, axis=-1, keepdims=True) + jnp.exp(sink - m)
                outs.append(_dot(e.astype(BF16), vv) * (1.0 / den))
            o_ref[pl.ds(r0, WINDOW), t * LANES:(t + 1) * LANES] = jnp.where(low, outs[0], outs[1]).astype(BF16)
        return carry

    lax.fori_loop(0, qblocks, block, 0)


def _swa_prompt(q, k, v, bias, sinks, nb, t, qblocks):
    nstep = t // (WINDOW * qblocks)
    rows = WINDOW * qblocks
    cur = lambda b, i: (b * nstep + i, 0)
    prev = lambda b, i: (b * nstep * qblocks + jnp.maximum(i * qblocks - 1, 0), 0)
    return pl.pallas_call(
        functools.partial(_swa_prompt_kernel, qblocks=qblocks),
        grid=(nb, nstep),
        in_specs=[pl.BlockSpec(memory_space=pltpu.SMEM),
                  pl.BlockSpec((rows, SWA_WIDTH), cur),
                  pl.BlockSpec((WINDOW, SWA_KV_WIDTH), prev),
                  pl.BlockSpec((rows, SWA_KV_WIDTH), cur),
                  pl.BlockSpec((WINDOW, SWA_KV_WIDTH), prev),
                  pl.BlockSpec((rows, SWA_KV_WIDTH), cur),
                  pl.BlockSpec((SWA_HEADS, WINDOW, 2 * WINDOW), lambda b, i: (0, 0, 0))],
        out_specs=pl.BlockSpec((rows, SWA_WIDTH), cur),
        out_shape=jax.ShapeDtypeStruct((nb * t, SWA_WIDTH), BF16),
        scratch_shapes=[pltpu.VMEM((rows + WINDOW, SWA_KV_WIDTH), BF16),
                        pltpu.VMEM((rows + WINDOW, SWA_KV_WIDTH), BF16)],
        compiler_params=_cparams("parallel", "parallel"),
        name="swa_prompt",
    )(sinks, q, k, k, v, v, bias)


def _swa_decode_kernel(q_ref, k_ref, v_ref, bias_ref, sink_ref, o_ref, *, seqs, tq):
    rows, keys = q_ref.shape[1], k_ref.shape[1]
    qi = lax.broadcasted_iota(jnp.int32, (rows, keys), 0) % tq
    col = lax.broadcasted_iota(jnp.int32, (rows, keys), 1)
    dist = qi + WINDOW - col
    valid = (dist >= 0) & (dist < WINDOW)
    bias = bias_ref[...]
    sink = sink_ref[...]
    for s_i in range(seqs):
        kk = k_ref[s_i].astype(BF16)
        s = lax.dot_general(q_ref[s_i], kk, _TRANS_B, preferred_element_type=F32)
        s = jnp.where(valid, s + bias, NEG_INF)
        p = _softmax_sink(s, sink).astype(BF16)
        o_ref[s_i] = _dot(p, v_ref[s_i].astype(BF16))


def _swa_decode(qz, k_all, v_all, bias, sink_rows, tq, seqs):
    nseq, rows, _ = qz.shape
    keys = k_all.shape[1]
    seqs = min(seqs, nseq)
    return pl.pallas_call(
        functools.partial(_swa_decode_kernel, seqs=seqs, tq=tq),
        grid=(nseq // seqs,),
        in_specs=[pl.BlockSpec((seqs, rows, LANES), lambda i: (i, 0, 0)),
                  pl.BlockSpec((seqs, keys, LANES), lambda i: (i, 0, 0)),
                  pl.BlockSpec((seqs, keys, LANES), lambda i: (i, 0, 0)),
                  pl.BlockSpec((rows, keys), lambda i: (0, 0)),
                  pl.BlockSpec((rows, 1), lambda i: (0, 0))],
        out_specs=pl.BlockSpec((seqs, rows, LANES), lambda i: (i, 0, 0)),
        out_shape=jax.ShapeDtypeStruct((nseq, rows, LANES), F32),
        compiler_params=_cparams("parallel"),
        name="swa_decode",
    )(qz, k_all, v_all, bias, sink_rows)


def _softmax(s):
    m = jnp.max(s, axis=-1, keepdims=True)
    e = jnp.exp(s - m)
    return e * (1.0 / jnp.sum(e, axis=-1, keepdims=True))


def _mem_prompt_kernel(q_ref, k_ref, v_ref, o_ref, s_ref, p_ref):
    scale = MEM_HEAD_DIM ** -0.5
    heads = [slice(h * MEM_HEAD_DIM, (h + 1) * MEM_HEAD_DIM) for h in range(MEM_HEADS)]
    for h, sl in enumerate(heads):
        s_ref[h] = lax.dot_general(q_ref[:, sl], k_ref[:, sl].astype(BF16), _TRANS_B, preferred_element_type=F32)
    s = s_ref[...] * scale
    e = jnp.exp(s - jnp.max(s, axis=-1, keepdims=True))
    p_ref[...] = e.astype(BF16)
    inv = 1.0 / jnp.sum(e, axis=-1, keepdims=True)
    for h, sl in enumerate(heads):
        o_ref[:, sl] = (_dot(p_ref[h], v_ref[:, sl].astype(BF16)) * inv[h]).astype(BF16)


def _mem_prompt(qm, mk, mv, nb, t, tile):
    tile = min(tile, t)
    nt = t // tile
    return pl.pallas_call(
        _mem_prompt_kernel,
        grid=(nb, nt),
        in_specs=[pl.BlockSpec((tile, MEM_WIDTH), lambda b, i: (b * nt + i, 0)),
                  pl.BlockSpec((MEM_TOKENS, MEM_WIDTH), lambda b, i: (b, 0)),
                  pl.BlockSpec((MEM_TOKENS, MEM_WIDTH), lambda b, i: (b, 0))],
        out_specs=pl.BlockSpec((tile, MEM_WIDTH), lambda b, i: (b * nt + i, 0)),
        out_shape=jax.ShapeDtypeStruct((nb * t, MEM_WIDTH), BF16),
        scratch_shapes=[pltpu.VMEM((MEM_HEADS, tile, MEM_TOKENS), F32), pltpu.VMEM((MEM_HEADS, tile, MEM_TOKENS), BF16)],
        compiler_params=_cparams("parallel", "parallel"),
        name="mem_prompt",
    )(qm, mk, mv)


def _mem_decode_kernel(q_ref, k_ref, v_ref, o_ref, *, seqs):
    tq = q_ref.shape[1]
    rows, cols = MEM_HEADS * tq, MEM_TOKENS * MEM_HEADS
    k2 = k_ref.reshape(seqs, cols, MEM_HEAD_DIM)
    v2 = v_ref.reshape(seqs, cols, MEM_HEAD_DIM)
    scale = MEM_HEAD_DIM ** -0.5
    own = (lax.broadcasted_iota(jnp.int32, (rows, cols), 1) % MEM_HEADS
           == lax.broadcasted_iota(jnp.int32, (rows, cols), 0) // tq)
    for s_i in range(seqs):
        q = q_ref[s_i]
        qb = jnp.concatenate([q[:, h * MEM_HEAD_DIM:(h + 1) * MEM_HEAD_DIM] for h in range(MEM_HEADS)], axis=0)
        s = lax.dot_general(qb.astype(BF16), k2[s_i].astype(BF16), _TRANS_B, preferred_element_type=F32) * scale
        p = _softmax(jnp.where(own, s, NEG_INF)).astype(BF16)
        o = _dot(p, v2[s_i].astype(BF16))
        for h in range(MEM_HEADS):
            o_ref[s_i, :, h * MEM_HEAD_DIM:(h + 1) * MEM_HEAD_DIM] = o[h * tq:(h + 1) * tq, :]


def _mem_decode(q, k, v, layer, seqs):
    nseq, tq, _ = q.shape
    seqs = min(seqs, nseq)
    cache = pl.BlockSpec((None, seqs, MEM_TOKENS, MEM_HEADS, MEM_HEAD_DIM), lambda i: (layer, i, 0, 0, 0))
    return pl.pallas_call(
        functools.partial(_mem_decode_kernel, seqs=seqs),
        grid=(nseq // seqs,),
        in_specs=[pl.BlockSpec((seqs, tq, MEM_WIDTH), lambda i: (i, 0, 0)), cache, cache],
        out_specs=pl.BlockSpec((seqs, tq, MEM_WIDTH), lambda i: (i, 0, 0)),
        out_shape=jax.ShapeDtypeStruct((nseq, tq, MEM_WIDTH), F32),
        compiler_params=_cparams("parallel"),
        name="mem_decode",
    )(q, k, v)


ROUTER_ROWS = 40
ROUTE_ROWS = 8
HALF = D_MODEL // 2


def _pack_halves(xb):
    hi = pltpu.bitcast(xb[:, 0:HALF].astype(F32), jnp.int32)
    lo = pltpu.bitcast(xb[:, HALF:D_MODEL].astype(F32), jnp.int32)
    return hi | lax.shift_right_logical(lo, jnp.int32(16))


def _unpack_halves(p):
    hi = pltpu.bitcast(p & jnp.int32(-65536), F32).astype(BF16)
    lo = pltpu.bitcast(lax.shift_left(p, jnp.int32(16)), F32).astype(BF16)
    return hi, lo


def _merge_kernel(x_ref, u_ref, y_ref, os_ref, om_ref, g1_ref, wg_ref, dsk_ref, wglu_ref, bglu_ref,
                  wbs_ref, wbw_ref, wbm_ref, wout_ref, g2_ref, wr_ref, br_ref,
                  h_ref, xn2_ref, route_ref):
    x = x_ref[...]
    tt = x.shape[0]
    xb = _rms(x, g1_ref[...]).astype(BF16)
    z = jax.nn.gelu(y_ref[...] + dsk_ref[...] * u_ref[...])
    z = z * jax.nn.sigmoid(_dot(z.astype(BF16), wglu_ref[...]) + bglu_ref[...])
    merged = jax.nn.sigmoid(_dot(xb, wg_ref[:, 0:D_MODEL])) * _dot(z.astype(BF16), wbs_ref[...])
    merged = merged + jax.nn.sigmoid(_dot(xb, wg_ref[:, D_MODEL:2 * D_MODEL])) * _dot(os_ref[...], wbw_ref[...])
    merged = merged + jax.nn.sigmoid(_dot(xb, wg_ref[:, 2 * D_MODEL:3 * D_MODEL])) * _dot(om_ref[...], wbm_ref[...])
    h = x + _dot(merged.astype(BF16), wout_ref[...])
    h_ref[...] = h
    xn2 = _rms(h, g2_ref[...]).astype(BF16)
    xn2_ref[...] = _pack_halves(xn2)

    lt = lax.dot_general(wr_ref[...], xn2, _TRANS_B, preferred_element_type=F32) + br_ref[...]
    gl = lt[N_EXPERTS:N_EXPERTS + N_EXPERT_GROUPS]
    ge = jnp.exp(gl - jnp.max(gl, axis=0, keepdims=True))
    gp = ge / jnp.sum(ge, axis=0, keepdims=True)
    gw = jnp.max(gp, axis=0, keepdims=True)
    gidx = jnp.full((1, tt), N_EXPERT_GROUPS - 1, jnp.int32)
    for r in range(N_EXPERT_GROUPS - 2, -1, -1):
        gidx = jnp.where(gp[r:r + 1] == gw, r, gidx)
    ein = lt[(N_EXPERT_GROUPS - 1) * EXPERTS_PER_GROUP:N_EXPERTS]
    for r in range(N_EXPERT_GROUPS - 2, -1, -1):
        ein = jnp.where(gidx == r, lt[r * EXPERTS_PER_GROUP:(r + 1) * EXPERTS_PER_GROUP], ein)
    ee = jnp.exp(ein - jnp.max(ein, axis=0, keepdims=True))
    ep = ee / jnp.sum(ee, axis=0, keepdims=True)
    rowi = lax.broadcasted_iota(jnp.int32, (EXPERTS_PER_GROUP, tt), 0)
    p1 = jnp.max(ep, axis=0, keepdims=True)
    e1 = jnp.min(jnp.where(ep == p1, rowi, EXPERTS_PER_GROUP), axis=0, keepdims=True)
    ep2 = jnp.where(rowi == e1, -1.0, ep)
    p2 = jnp.max(ep2, axis=0, keepdims=True)
    e2 = jnp.min(jnp.where(ep2 == p2, rowi, EXPERTS_PER_GROUP), axis=0, keepdims=True)
    tot = p1 + p2
    w1 = p1 / tot * gw
    w2 = p2 / tot * gw
    id1 = (gidx * EXPERTS_PER_GROUP + e1).astype(F32)
    id2 = (gidx * EXPERTS_PER_GROUP + e2).astype(F32)
    route_ref[...] = jnp.concatenate([id1, id2, w1, w2, jnp.zeros((ROUTE_ROWS - 4, tt), F32)], axis=0)


def _merge(x, u, y, o_swa, o_mem, p, tile):
    n = x.shape[0]
    tile = min(tile, n)
    row = lambda i: (i, 0)
    const = lambda i: (0, 0)
    full = lambda a: pl.BlockSpec(a.shape, const, pipeline_mode=pl.Buffered(1))
    weights = [p['g1'], p['w_gates'], p['d_skip'], p['w_glu'], p['b_glu'], p['w_br_ssm'], p['w_br_swa'],
               p['w_br_mem'], p['w_out'], p['g2'], p['w_router'], p['b_router']]
    return pl.pallas_call(
        _merge_kernel,
        grid=(n // tile,),
        in_specs=[pl.BlockSpec((tile, D_MODEL), row), pl.BlockSpec((tile, SSM_WIDTH), row),
                  pl.BlockSpec((tile, SSM_WIDTH), row), pl.BlockSpec((tile, SWA_WIDTH), row),
                  pl.BlockSpec((tile, MEM_WIDTH), row)] + [full(w) for w in weights],
        out_specs=[pl.BlockSpec((tile, D_MODEL), row), pl.BlockSpec((tile, HALF), row),
                   pl.BlockSpec((ROUTE_ROWS, tile), lambda i: (0, i))],
        out_shape=[jax.ShapeDtypeStruct((n, D_MODEL), F32), jax.ShapeDtypeStruct((n, HALF), jnp.int32),
                   jax.ShapeDtypeStruct((ROUTE_ROWS, n), F32)],
        compiler_params=_cparams("parallel"),
        name="merge_router",
    )(x, u, y, o_swa, o_mem, *weights)


def _expert_mlp(xp, wg, wu, wd):
    hi, lo = _unpack_halves(xp)
    g = _dot(hi, wg[0:HALF, :]) + _dot(lo, wg[HALF:D_MODEL, :])
    u = _dot(hi, wu[0:HALF, :]) + _dot(lo, wu[HALF:D_MODEL, :])
    hh = jax.nn.silu(g) * u
    return _dot(hh.astype(BF16), wd[...])


def _moe_kernel(xn2_ref, rt_ref, wg_ref, wu_ref, wd_ref, h_ref, gf_ref, o_ref, acc_ref):
    e = pl.program_id(1)

    @pl.when(e == 0)
    def _():
        acc_ref[...] = jnp.zeros_like(acc_ref)

    o = _expert_mlp(xn2_ref[...], wg_ref[...].astype(BF16), wu_ref[...].astype(BF16), wd_ref[...].astype(BF16))
    ef = e.astype(F32)
    c = (jnp.where(rt_ref[:, 0:1] == ef, rt_ref[:, 2:3], 0.0)
         + jnp.where(rt_ref[:, 1:2] == ef, rt_ref[:, 3:4], 0.0))
    acc_ref[...] += c * o

    @pl.when(e == N_EXPERTS - 1)
    def _():
        o_ref[...] = _rms(h_ref[...] + acc_ref[...], gf_ref[...])


def _moe(xn2, route_t, w_g, w_u, w_d, h, gf, tile):
    n = h.shape[0]
    tile = min(tile, n)
    return pl.pallas_call(
        _moe_kernel,
        grid=(n // tile, N_EXPERTS),
        in_specs=[pl.BlockSpec((tile, HALF), lambda i, e: (i, 0)),
                  pl.BlockSpec((tile, ROUTE_ROWS), lambda i, e: (i, 0)),
                  pl.BlockSpec((None, D_MODEL, D_EXPERT), lambda i, e: (e, 0, 0)),
                  pl.BlockSpec((None, D_MODEL, D_EXPERT), lambda i, e: (e, 0, 0)),
                  pl.BlockSpec((None, D_EXPERT, D_MODEL), lambda i, e: (e, 0, 0)),
                  pl.BlockSpec((tile, D_MODEL), lambda i, e: (i, 0)),
                  pl.BlockSpec((1, D_MODEL), lambda i, e: (0, 0))],
        out_specs=pl.BlockSpec((tile, D_MODEL), lambda i, e: (i, 0)),
        out_shape=jax.ShapeDtypeStruct((n, D_MODEL), F32),
        scratch_shapes=[pltpu.VMEM((tile, D_MODEL), F32)],
        compiler_params=_cparams("parallel", "arbitrary"),
        name="moe_final_norm",
    )(xn2, route_t, w_g, w_u, w_d, h, gf)


EXPERT_ROW_TILE = 1024
SC_CORES = 2
SC_SUBCORES = 16
SC_WORKERS = SC_CORES * SC_SUBCORES
SC_SCATTER_ROWS = 64
SC_GATHER_ROWS = 64


def _route_rank_kernel(r_ref, rank_ref, cnt_ref, base_ref):
    i = pl.program_id(0)
    tt = r_ref.shape[1]

    @pl.when(i == 0)
    def _():
        base_ref[...] = jnp.zeros_like(base_ref)

    ids = r_ref[0:2, :].astype(jnp.int32)
    e_iota = lax.broadcasted_iota(jnp.int32, (N_EXPERTS, tt), 0)
    oh1 = jnp.where(e_iota == ids[0:1], 1.0, 0.0)
    oh2 = jnp.where(e_iota == ids[1:2], 1.0, 0.0)
    before = (lax.broadcasted_iota(jnp.int32, (tt, tt), 0) < lax.broadcasted_iota(jnp.int32, (tt, tt), 1))
    tri = jnp.where(before, 1.0, 0.0).astype(BF16)
    c1 = _dot(oh1.astype(BF16), tri)
    c2 = _dot(oh2.astype(BF16), tri)
    tot1 = jnp.sum(oh1, axis=1, keepdims=True)
    tot2 = jnp.sum(oh2, axis=1, keepdims=True)
    base = base_ref[:, 0:1]
    rank1 = jnp.sum(oh1 * (base + c1), axis=0, keepdims=True)
    rank2 = jnp.sum(oh2 * (base + tot1 + c2), axis=0, keepdims=True)
    rank_ref[...] = jnp.concatenate([rank1, rank2, jnp.zeros((ROUTE_ROWS - 2, tt), F32)], axis=0).astype(jnp.int32)
    new_base = jnp.broadcast_to(base + tot1 + tot2, base_ref.shape)
    base_ref[...] = new_base
    cnt_ref[...] = new_base.astype(jnp.int32)


def _route_rank(route, tile):
    n = route.shape[1]
    tile = min(tile, n)
    return pl.pallas_call(
        _route_rank_kernel,
        grid=(n // tile,),
        in_specs=[pl.BlockSpec((ROUTE_ROWS, tile), lambda i: (0, i))],
        out_specs=[pl.BlockSpec((ROUTE_ROWS, tile), lambda i: (0, i)),
                   pl.BlockSpec((N_EXPERTS, LANES), lambda i: (0, 0))],
        out_shape=[jax.ShapeDtypeStruct((ROUTE_ROWS, n), jnp.int32),
                   jax.ShapeDtypeStruct((N_EXPERTS, LANES), jnp.int32)],
        scratch_shapes=[pltpu.VMEM((N_EXPERTS, LANES), F32)],
        compiler_params=_cparams("arbitrary"),
        name="route_rank",
    )(route)


def _sc_mesh():
    return plsc.VectorSubcoreMesh(core_axis_name="core", subcore_axis_name="subcore")


def _sc_scatter_pairs(x, pos, rows_out):
    n, d = x.shape
    per_w = n // SC_WORKERS
    window = min(SC_SCATTER_ROWS, per_w)

    @pl.kernel(out_type=jax.ShapeDtypeStruct((rows_out, d), x.dtype), mesh=_sc_mesh(),
               scratch_types=[pltpu.VMEM((window,), jnp.int32), pltpu.VMEM((window,), jnp.int32),
                              pltpu.VMEM((window, d), x.dtype), pltpu.SemaphoreType.DMA, pltpu.SemaphoreType.DMA,
                              pltpu.SemaphoreType.DMA])
    def scatter(x_hbm, p_hbm, o_hbm, i1_v, i2_v, rows_v, sem_a, sem_b, sem_c):
        wid = lax.axis_index("subcore") * SC_CORES + lax.axis_index("core")

        @pl.loop(0, per_w // window)
        def _(j):
            base = wid * per_w + j * window
            load_i1 = pltpu.async_copy(p_hbm.at[pl.ds(base, window)], i1_v, sem_a)
            load_i2 = pltpu.async_copy(p_hbm.at[pl.ds(n + base, window)], i2_v, sem_b)
            load_x = pltpu.async_copy(x_hbm.at[pl.ds(base, window)], rows_v, sem_c)
            load_i1.wait()
            load_i2.wait()
            load_x.wait()
            put_1 = pltpu.async_copy(rows_v, o_hbm.at[i1_v], sem_a)
            put_2 = pltpu.async_copy(rows_v, o_hbm.at[i2_v], sem_b)
            put_1.wait()
            put_2.wait()

    return scatter(x, pos)


def _sc_gather_rows(table, idx):
    m = idx.shape[0]
    d = table.shape[1]
    per_w = m // SC_WORKERS
    window = min(SC_GATHER_ROWS, per_w)

    assert per_w % (2 * window) == 0

    @pl.kernel(out_type=jax.ShapeDtypeStruct((m, d), table.dtype), mesh=_sc_mesh(),
               scratch_types=[pltpu.VMEM((window,), jnp.int32), pltpu.VMEM((window,), jnp.int32),
                              pltpu.VMEM((window, d), table.dtype), pltpu.VMEM((window, d), table.dtype),
                              pltpu.SemaphoreType.DMA, pltpu.SemaphoreType.DMA])
    def gather(t_hbm, i_hbm, o_hbm, ia_v, ib_v, ra_v, rb_v, sem_a, sem_b):
        wid = lax.axis_index("subcore") * SC_CORES + lax.axis_index("core")

        @pl.loop(0, per_w // (2 * window))
        def _(j):
            base_a = wid * per_w + j * (2 * window)
            base_b = base_a + window
            idx_a = pltpu.async_copy(i_hbm.at[pl.ds(base_a, window)], ia_v, sem_a)
            idx_b = pltpu.async_copy(i_hbm.at[pl.ds(base_b, window)], ib_v, sem_b)
            idx_a.wait()
            get_a = pltpu.async_copy(t_hbm.at[ia_v], ra_v, sem_a)
            idx_b.wait()
            get_b = pltpu.async_copy(t_hbm.at[ib_v], rb_v, sem_b)
            get_a.wait()
            put_a = pltpu.async_copy(ra_v, o_hbm.at[pl.ds(base_a, window)], sem_a)
            get_b.wait()
            put_b = pltpu.async_copy(rb_v, o_hbm.at[pl.ds(base_b, window)], sem_b)
            put_a.wait()
            put_b.wait()

    return gather(table, idx)


def _expert_tiles_kernel(te_ref, nu_ref, x_ref, wg_ref, wu_ref, wd_ref, o_ref, wg_s, wu_s, wd_s):
    i = pl.program_id(0)

    @pl.when(i < nu_ref[0])
    def _():
        @pl.when(jnp.logical_or(i == 0, te_ref[i] != te_ref[jnp.maximum(i - 1, 0)]))
        def _():
            wg_s[...] = wg_ref[...].astype(BF16)
            wu_s[...] = wu_ref[...].astype(BF16)
            wd_s[...] = wd_ref[...].astype(BF16)

        o_ref[...] = _pack_halves(_expert_mlp(x_ref[...], wg_s, wu_s, wd_s).astype(BF16))


def _expert_tiles(tile_expert, n_used, xs, w_g, w_u, w_d):
    rows = xs.shape[0]
    tm = EXPERT_ROW_TILE
    grid_spec = pltpu.PrefetchScalarGridSpec(
        num_scalar_prefetch=2,
        grid=(rows // tm,),
        in_specs=[pl.BlockSpec((tm, HALF), lambda i, te, nu: (i, 0)),
                  pl.BlockSpec((None, D_MODEL, D_EXPERT), lambda i, te, nu: (te[i], 0, 0)),
                  pl.BlockSpec((None, D_MODEL, D_EXPERT), lambda i, te, nu: (te[i], 0, 0)),
                  pl.BlockSpec((None, D_EXPERT, D_MODEL), lambda i, te, nu: (te[i], 0, 0))],
        out_specs=pl.BlockSpec((tm, HALF), lambda i, te, nu: (i, 0)),
        scratch_shapes=[pltpu.VMEM((D_MODEL, D_EXPERT), BF16), pltpu.VMEM((D_MODEL, D_EXPERT), BF16),
                        pltpu.VMEM((D_EXPERT, D_MODEL), BF16)],
    )
    return pl.pallas_call(
        _expert_tiles_kernel,
        grid_spec=grid_spec,
        out_shape=jax.ShapeDtypeStruct((rows, HALF), jnp.int32),
        compiler_params=_cparams("arbitrary"),
        name="expert_tiles",
    )(tile_expert, n_used, xs, w_g, w_u, w_d)


def _unpack_f32(p):
    return pltpu.bitcast(p & jnp.int32(-65536), F32), pltpu.bitcast(lax.shift_left(p, jnp.int32(16)), F32)


def _combine_kernel(h_ref, o1_ref, o2_ref, rt_ref, gf_ref, y_ref):
    w1, w2 = rt_ref[:, 2:3], rt_ref[:, 3:4]
    a_lo, a_hi = _unpack_f32(o1_ref[...])
    b_lo, b_hi = _unpack_f32(o2_ref[...])
    y_lo = h_ref[:, 0:HALF] + (w1 * a_lo + w2 * b_lo)
    y_hi = h_ref[:, HALF:D_MODEL] + (w1 * a_hi + w2 * b_hi)
    ms = (jnp.sum(y_lo * y_lo, axis=-1, keepdims=True) + jnp.sum(y_hi * y_hi, axis=-1, keepdims=True)) / D_MODEL
    inv = lax.rsqrt(ms + EPS)
    y_ref[:, 0:HALF] = (y_lo * inv) * gf_ref[:, 0:HALF]
    y_ref[:, HALF:D_MODEL] = (y_hi * inv) * gf_ref[:, HALF:D_MODEL]


def _combine(h, o12, route_t, gf, tile):
    n = h.shape[0]
    tile = min(tile, n)
    nt = n // tile
    return pl.pallas_call(
        _combine_kernel,
        grid=(nt,),
        in_specs=[pl.BlockSpec((tile, D_MODEL), lambda i: (i, 0)),
                  pl.BlockSpec((tile, HALF), lambda i: (i, 0)),
                  pl.BlockSpec((tile, HALF), lambda i: (i + nt, 0)),
                  pl.BlockSpec((tile, ROUTE_ROWS), lambda i: (i, 0)),
                  pl.BlockSpec((1, D_MODEL), lambda i: (0, 0))],
        out_specs=pl.BlockSpec((tile, D_MODEL), lambda i: (i, 0)),
        out_shape=jax.ShapeDtypeStruct((n, D_MODEL), F32),
        compiler_params=_cparams("parallel"),
        name="combine_final_norm",
    )(h, o12, o12, route_t, gf)


def _sparse_moe(xn2p, route, h, w_g, w_u, w_d, gf, run_before_experts):
    n = h.shape[0]
    tm = EXPERT_ROW_TILE
    rows = 2 * n + N_EXPERTS * tm
    rank, cnt = _route_rank(route, 512)
    counts = cnt[:, 0]
    padded = (counts + tm - 1) // tm * tm
    e_idx = jnp.arange(N_EXPERTS, dtype=jnp.int32)
    starts = jnp.sum(jnp.where(e_idx[None, :] < e_idx[:, None], padded[None, :], 0), axis=1)
    ends = starts + padded
    ids = route[0:2].astype(jnp.int32)
    start_of = jnp.sum(jnp.where(ids[None] == e_idx[:, None, None], starts[:, None, None], 0), axis=0)
    pos = (start_of + rank[0:2]).reshape(2 * n)
    tile_start = jnp.arange(rows // tm, dtype=jnp.int32) * tm
    tile_expert = jnp.minimum(jnp.sum((tile_start[:, None] >= ends[None, :]).astype(jnp.int32), axis=1),
                              N_EXPERTS - 1)
    n_used = (ends[-1:] // tm).astype(jnp.int32)
    xs = _sc_scatter_pairs(xn2p, pos, rows)
    xs, _ = lax.optimization_barrier((xs, run_before_experts))
    os_ = _expert_tiles(tile_expert, n_used, xs, w_g, w_u, w_d)
    o12 = _sc_gather_rows(os_, pos)
    return _combine(h, o12, route.T, gf, 512)


def _prep_in_weights(w_in):
    o = 0
    w_u = w_in[:, o:o + SSM_WIDTH]; o += SSM_WIDTH
    w_q = w_in[:, o:o + SWA_WIDTH]; o += SWA_WIDTH
    w_k = w_in[:, o:o + SWA_KV_WIDTH]; o += SWA_KV_WIDTH
    w_v = w_in[:, o:o + SWA_KV_WIDTH]; o += SWA_KV_WIDTH
    w_qm = w_in[:, o:o + MEM_WIDTH]; o += MEM_WIDTH
    w_g = w_in[:, o:]
    wq = (w_q * (SWA_HEAD_DIM ** -0.5)).reshape(D_MODEL, SWA_KV_HEADS, SWA_REP, SWA_HEAD_DIM)
    wq = wq.transpose(0, 2, 1, 3).reshape(D_MODEL, SWA_WIDTH)
    w_main = jnp.concatenate([w_u, wq, w_k, w_v, w_qm], axis=1).astype(BF16)
    return w_main, w_g.astype(BF16)


IN_SPLITS = (SSM_WIDTH, SWA_WIDTH, SWA_KV_WIDTH, SWA_KV_WIDTH, MEM_WIDTH)
IN_DTYPES = ((F32, BF16), (BF16,), (F32,), (F32,), (BF16,))


def kernel(x_prompt, x_sample, cache_swa_k, cache_swa_v, state_ssm_re, state_ssm_im, cache_mem_k, cache_mem_v, mem_prompt, norm1_g, w_in, lam_re, lam_im, log_dt, bm_re, bm_im, cm_re, cm_im, d_skip, w_glu, b_glu, sinks, rel_table, mem_norm_g, w_mem_kv, w_br_ssm, w_br_swa, w_br_mem, w_out, norm2_g, w_rg, b_rg, w_rexp, b_rexp, w_e_gate, w_e_up, w_e_down, final_norm_g):
    nb, t, _ = x_prompt.shape
    ns, ts, _ = x_sample.shape
    l = 0
    L = S5_CHUNK

    w_main, w_gates = _prep_in_weights(w_in[l])
    w_swa = (w_br_swa[l].reshape(SWA_KV_HEADS, SWA_REP, SWA_HEAD_DIM, D_MODEL).transpose(1, 0, 2, 3)
             .reshape(SWA_WIDTH, D_MODEL))
    pad_rows = ROUTER_ROWS - N_EXPERTS - N_EXPERT_GROUPS
    w_router = jnp.concatenate([w_rexp[l].T, w_rg[l].T, jnp.zeros((pad_rows, D_MODEL), F32)], axis=0).astype(BF16)
    b_router = jnp.concatenate([b_rexp[l], b_rg[l], jnp.zeros((pad_rows,), F32)]).reshape(ROUTER_ROWS, 1)
    mp = {
        'g1': norm1_g[l].reshape(1, D_MODEL), 'w_gates': w_gates, 'd_skip': d_skip[l].reshape(1, SSM_WIDTH),
        'w_glu': w_glu[l].astype(BF16), 'b_glu': b_glu[l].reshape(1, SSM_WIDTH),
        'w_br_ssm': w_br_ssm[l].astype(BF16), 'w_br_swa': w_swa.astype(BF16),
        'w_br_mem': w_br_mem[l].astype(BF16), 'w_out': w_out[l].astype(BF16),
        'g2': norm2_g[l].reshape(1, D_MODEL), 'w_router': w_router, 'b_router': b_router,
    }
    w_g, w_u, w_d = w_e_gate[l], w_e_up[l], w_e_down[l]
    gf = final_norm_g.reshape(1, D_MODEL)
    s5_w = _s5_weights(lam_re[l], lam_im[l], log_dt[l], bm_re[l], bm_im[l], cm_re[l], cm_im[l], L)

    bias_p = _rel_bias(rel_table, np.arange(WINDOW)[:, None] + WINDOW - np.arange(2 * WINDOW)[None, :])
    keys_s = WINDOW + 2 * ts
    bias_s = _rel_bias(rel_table, np.arange(ts)[:, None] + WINDOW - np.arange(keys_s)[None, :])
    bias_s = bias_s.reshape(SWA_HEADS * ts, keys_s)
    sink_rows = jnp.repeat(sinks[l].astype(F32), ts).reshape(SWA_HEADS * ts, 1)

    n = nb * t
    xp = x_prompt.reshape(n, D_MODEL)
    mk, mv = _norm_proj(mem_prompt.reshape(nb * MEM_TOKENS, D_MODEL), mem_norm_g[l].reshape(1, D_MODEL),
                        w_mem_kv[l].astype(BF16), (MEM_WIDTH, MEM_WIDTH), ((F32,), (F32,)), 512)
    u, ub, qz, k, v, qm = _norm_proj(xp, mp['g1'], w_main, IN_SPLITS, IN_DTYPES, 512)

    y_ssm, fin = _s5(ub, jnp.zeros((nb, N_CH_TILES * 2 * STATE_TILE), F32), s5_w, nb, t // L, L, 64)
    p_re, p_im = _tiles_to_state(fin)

    o_swa = _swa_prompt(qz, k, v, bias_p, sinks[l].astype(F32), nb, t, 4)
    o_mem = _mem_prompt(qm, mk, mv, nb, t, 512)
    h, xn2p, route = _merge(xp, u, y_ssm, o_swa, o_mem, mp, 512)

    k4 = k.reshape(nb, t, SWA_KV_HEADS, SWA_HEAD_DIM)
    v4 = v.reshape(nb, t, SWA_KV_HEADS, SWA_HEAD_DIM)
    new_k_p, new_v_p = k4[:, -WINDOW:][None], v4[:, -WINDOW:][None]
    new_mk = mk.reshape(1, nb, MEM_TOKENS, MEM_HEADS, MEM_HEAD_DIM)
    new_mv = mv.reshape(1, nb, MEM_TOKENS, MEM_HEADS, MEM_HEAD_DIM)

    m = ns * ts
    xs = x_sample.reshape(m, D_MODEL)
    us, ubs, qzs, k_s, v_s, qms = _norm_proj(xs, mp['g1'], w_main, IN_SPLITS, IN_DTYPES, 256)
    ys_ssm, fins = _s5(ubs, _state_to_tiles(state_ssm_re[l], state_ssm_im[l]), s5_w, ns, ts // L, L, 64)
    s_re, s_im = _tiles_to_state(fins)

    kk_all = jnp.concatenate([cache_swa_k[l].reshape(ns, WINDOW, SWA_KV_WIDTH).astype(F32),
                              k_s.reshape(ns, ts, SWA_KV_WIDTH)], axis=1)
    vv_all = jnp.concatenate([cache_swa_v[l].reshape(ns, WINDOW, SWA_KV_WIDTH).astype(F32),
                              v_s.reshape(ns, ts, SWA_KV_WIDTH)], axis=1)
    pad = jnp.zeros((ns, keys_s - WINDOW - ts, SWA_KV_WIDTH), F32)
    q5 = qzs.reshape(ns, ts, SWA_REP, SWA_KV_HEADS, SWA_HEAD_DIM)
    zq = jnp.zeros((ns, ts, SWA_REP, SWA_HEAD_DIM), BF16)
    q_rows = jnp.concatenate([jnp.concatenate([q5[:, :, :, 0], zq], axis=-1),
                              jnp.concatenate([zq, q5[:, :, :, 1]], axis=-1)], axis=2)
    q_rows = q_rows.transpose(0, 2, 1, 3).reshape(ns, SWA_HEADS * ts, LANES)
    o_dec = _swa_decode(q_rows, jnp.concatenate([kk_all, pad], axis=1), jnp.concatenate([vv_all, pad], axis=1),
                        bias_s, sink_rows, ts, 8)
    o_dec = o_dec.reshape(ns, SWA_KV_HEADS, SWA_REP, ts, SWA_KV_HEADS, SWA_HEAD_DIM)
    o_dec = jnp.stack([o_dec[:, g, :, :, g] for g in range(SWA_KV_HEADS)], axis=1)
    o_swa_s = o_dec.transpose(0, 3, 2, 1, 4).reshape(m, SWA_WIDTH).astype(BF16)

    o_mem_s = _mem_decode(qms.astype(F32).reshape(ns, ts, MEM_WIDTH), cache_mem_k, cache_mem_v, l, 4)
    o_mem_s = o_mem_s.reshape(m, MEM_WIDTH).astype(BF16)

    y_prompt = _sparse_moe(xn2p, route, h, w_g, w_u, w_d, gf, (ys_ssm, o_swa_s, o_mem_s)).reshape(nb, t, D_MODEL)
    hs_, xn2ps, routes = _merge(xs, us, ys_ssm, o_swa_s, o_mem_s, mp, 256)
    y_sample = _moe(xn2ps, routes.T, w_g, w_u, w_d, hs_, gf, 1024).reshape(ns, ts, D_MODEL)

    new_k_s = kk_all[:, -WINDOW:].reshape(1, ns, WINDOW, SWA_KV_HEADS, SWA_HEAD_DIM).astype(cache_swa_k.dtype)
    new_v_s = vv_all[:, -WINDOW:].reshape(1, ns, WINDOW, SWA_KV_HEADS, SWA_HEAD_DIM).astype(cache_swa_v.dtype)

    return (y_prompt, y_sample,
            new_k_p, new_v_p, p_re[None], p_im[None], new_mk, new_mv,
            new_k_s, new_v_s, s_re[None].astype(state_ssm_re.dtype), s_im[None].astype(state_ssm_im.dtype))
```

```python
import functools
import math

import numpy as np
import jax
import jax.numpy as jnp
from jax import lax
from jax.experimental import pallas as pl
from jax.experimental.pallas import tpu as pltpu
from jax.experimental.pallas import tpu_sc as plsc

F32 = jnp.float32
BF16 = jnp.bfloat16

D_MODEL = 1024
SSM_WIDTH = 512
SSM_GROUP = 16
SSM_GROUPS = 32
SSM_STATE = 64
SWA_HEADS = 8
SWA_KV_HEADS = 2
SWA_REP = 4
SWA_HEAD_DIM = 64
SWA_WIDTH = 512
SWA_KV_WIDTH = 128
WINDOW = 128
REL_BUCKETS = 32
REL_MAX_DIST = 128
MEM_TOKENS = 256
MEM_HEADS = 4
MEM_HEAD_DIM = 128
MEM_WIDTH = 512
N_EXPERT_GROUPS = 4
EXPERTS_PER_GROUP = 8
N_EXPERTS = 32
D_EXPERT = 256
EPS = 1e-6
NEG_INF = -1e30

LANES = 128
GROUPS_PER_TILE = LANES // SSM_GROUP
N_CH_TILES = SSM_WIDTH // LANES
STATE_TILE = GROUPS_PER_TILE * SSM_STATE
VMEM_LIMIT = 56 * 1024 * 1024
S5_CHUNK = 8

_TRANS_B = (((1,), (1,)), ((), ()))


def _cparams(*sem):
    return pltpu.CompilerParams(dimension_semantics=sem, vmem_limit_bytes=VMEM_LIMIT)


def _rms(x, g):
    return (x * lax.rsqrt(jnp.mean(x * x, axis=-1, keepdims=True) + EPS)) * g


def _dot(a, b):
    return jnp.dot(a, b, preferred_element_type=F32)


def _norm_proj_kernel(x_ref, g_ref, w_ref, *out_refs, splits, dtypes):
    xb = _rms(x_ref[...], g_ref[...]).astype(BF16)
    off = 0
    outs = iter(out_refs)
    for width, dts in zip(splits, dtypes):
        r = _dot(xb, w_ref[:, off:off + width])
        for dt in dts:
            next(outs)[...] = r.astype(dt)
        off += width


def _norm_proj(x, g, w, splits, dtypes, tile):
    n, d = x.shape
    tile = min(tile, n)
    flat = [(wd, dt) for wd, dts in zip(splits, dtypes) for dt in dts]
    return pl.pallas_call(
        functools.partial(_norm_proj_kernel, splits=tuple(splits), dtypes=tuple(dtypes)),
        grid=(n // tile,),
        in_specs=[pl.BlockSpec((tile, d), lambda i: (i, 0)),
                  pl.BlockSpec((1, d), lambda i: (0, 0)),
                  pl.BlockSpec((d, sum(splits)), lambda i: (0, 0))],
        out_specs=[pl.BlockSpec((tile, wd), lambda i: (i, 0)) for wd, _ in flat],
        out_shape=[jax.ShapeDtypeStruct((n, wd), dt) for wd, dt in flat],
        compiler_params=_cparams("parallel"),
        name="norm_proj",
    )(x, g, w)


def _s5_weights(lam_re, lam_im, log_dt, bm_re, bm_im, cm_re, cm_im, L):
    hp = lax.Precision.HIGHEST
    nt, gt, P, H = N_CH_TILES, GROUPS_PER_TILE, SSM_STATE, SSM_GROUP
    lr, li = lam_re.astype(F32), lam_im.astype(F32)
    dt = jnp.exp(log_dt.astype(F32))[:, None]
    mag = jnp.exp(lr * dt)
    a_re = mag * jnp.cos(li * dt)
    a_im = mag * jnp.sin(li * dt)
    den = lr * lr + li * li
    f_re = ((a_re - 1.0) * lr + a_im * li) / den
    f_im = (a_im * lr - (a_re - 1.0) * li) / den
    br, bi = bm_re.astype(F32), bm_im.astype(F32)
    bb_re = f_re[..., None] * br - f_im[..., None] * bi
    bb_im = f_re[..., None] * bi + f_im[..., None] * br
    pr, pi = [jnp.ones_like(a_re)], [jnp.zeros_like(a_im)]
    for _ in range(L):
        pr.append(pr[-1] * a_re - pi[-1] * a_im)
        pi.append(pr[-2] * a_im + pi[-1] * a_re)
    ap_re, ap_im = jnp.stack(pr), jnp.stack(pi)
    cr, ci = cm_re.astype(F32), cm_im.astype(F32)
    ca_re = cr[None] * ap_re[:, :, None, :] - ci[None] * ap_im[:, :, None, :]
    ca_im = cr[None] * ap_im[:, :, None, :] + ci[None] * ap_re[:, :, None, :]

    rev_re = jnp.stack([pr[L - 1 - s] for s in range(L)])
    rev_im = jnp.stack([pi[L - 1 - s] for s in range(L)])
    ws_re = rev_re[..., None] * bb_re[None] - rev_im[..., None] * bb_im[None]
    ws_im = rev_re[..., None] * bb_im[None] + rev_im[..., None] * bb_re[None]
    c_st = jnp.concatenate([ws_re.transpose(0, 1, 3, 2).reshape(L, nt, gt * H, P),
                            ws_im.transpose(0, 1, 3, 2).reshape(L, nt, gt * H, P)], axis=3).transpose(1, 0, 2, 3)
    so = lambda ca: ca[1:].transpose(1, 3, 0, 2).reshape(nt, gt * P, L * H)
    c_so = jnp.concatenate([so(ca_re), so(-ca_im)], axis=1)
    k_lag = (jnp.einsum('tghp,gpk->gkth', ca_re[:L], bb_re, precision=hp)
             - jnp.einsum('tghp,gpk->gkth', ca_im[:L], bb_im, precision=hp))
    c_k = k_lag.reshape(nt, gt * H, L * H)
    w_st, w_out, toep = _s5_expand(c_st, c_so, c_k, L)

    def per_tile(v):
        return v.reshape(nt, 1, STATE_TILE)

    return w_st, w_out, toep, per_tile(pr[L]), per_tile(pi[L])


def _s5_expand_kernel(cst_ref, cso_ref, ck_ref, wst_ref, wso_ref, toep_ref, *, L):
    hp = lax.Precision.HIGHEST
    P, H = SSM_STATE, SSM_GROUP
    iota = lambda shape, d: lax.broadcasted_iota(jnp.int32, shape, d)
    one = lambda cond: jnp.where(cond, 1.0, 0.0).astype(F32)

    r, c = iota((2 * P, 2 * STATE_TILE), 0), iota((2 * P, 2 * STATE_TILE), 1)
    rep_st = one((r // P == c // STATE_TILE) & (r % P == c % P))
    r, c = iota((LANES, 2 * STATE_TILE), 0), iota((LANES, 2 * STATE_TILE), 1)
    own_st = one(r // H == (c % STATE_TILE) // P)
    for s in range(L):
        blk = jnp.dot(cst_ref[s], rep_st, precision=hp, preferred_element_type=F32) * own_st
        wst_ref[s * LANES:(s + 1) * LANES, :] = blk.astype(BF16)

    r, c = iota((LANES, LANES), 0), iota((LANES, LANES), 1)
    pick = [one((r // H == t) & (r % H == c % H)) for t in range(L)]
    own_k = one(r // H == c // H)
    r, c = iota((2 * STATE_TILE, LANES), 0), iota((2 * STATE_TILE, LANES), 1)
    own_so = one((r % STATE_TILE) // P == c // H)
    cso = cso_ref[...]
    for t in range(L):
        blk = jnp.dot(cso, pick[t], precision=hp, preferred_element_type=F32) * own_so
        wso_ref[:, t * LANES:(t + 1) * LANES] = blk.astype(BF16)
    ck = ck_ref[...]
    lag = [(jnp.dot(ck, pick[t], precision=hp, preferred_element_type=F32) * own_k).astype(BF16) for t in range(L)]
    zero = jnp.zeros((LANES, LANES), BF16)
    for s in range(L):
        for t in range(L):
            toep_ref[s * LANES:(s + 1) * LANES, t * LANES:(t + 1) * LANES] = lag[t - s] if t >= s else zero


def _s5_expand(c_st, c_so, c_k, L):
    lk = L * LANES
    st2 = 2 * STATE_TILE
    return pl.pallas_call(
        functools.partial(_s5_expand_kernel, L=L),
        grid=(N_CH_TILES,),
        in_specs=[pl.BlockSpec((None, L, LANES, 2 * SSM_STATE), lambda j: (j, 0, 0, 0)),
                  pl.BlockSpec((None, st2, L * SSM_GROUP), lambda j: (j, 0, 0)),
                  pl.BlockSpec((None, LANES, L * SSM_GROUP), lambda j: (j, 0, 0))],
        out_specs=[pl.BlockSpec((None, lk, st2), lambda j: (j, 0, 0)),
                   pl.BlockSpec((None, st2, lk), lambda j: (j, 0, 0)),
                   pl.BlockSpec((None, lk, lk), lambda j: (j, 0, 0))],
        out_shape=[jax.ShapeDtypeStruct((N_CH_TILES, lk, st2), BF16),
                   jax.ShapeDtypeStruct((N_CH_TILES, st2, lk), BF16),
                   jax.ShapeDtypeStruct((N_CH_TILES, lk, lk), BF16)],
        compiler_params=_cparams("parallel"),
        name="s5_expand_weights",
    )(c_st, c_so, c_k)


def _to_chunks(u, nb, nc, L):
    return (u.reshape(nb, nc, L, N_CH_TILES, LANES).transpose(1, 0, 3, 2, 4)
            .reshape(nc * nb, N_CH_TILES * L * LANES))


def _from_chunks(y, nb, nc, L):
    return (y.reshape(nc, nb, N_CH_TILES, L, LANES).transpose(1, 0, 3, 2, 4)
            .reshape(nb * nc * L, SSM_WIDTH))


def _s5_kernel(x_ref, h0_ref, are_ref, aim_ref, ws_ref, t_ref, wo_ref, y_ref, fin_ref,
               hr_ref, hi_ref, d_ref, hs_ref, *, cb, nb):
    ci = pl.program_id(1)

    @pl.when(ci == 0)
    def _():
        hr_ref[...] = h0_ref[:, 0:STATE_TILE]
        hi_ref[...] = h0_ref[:, STATE_TILE:2 * STATE_TILE]

    x = x_ref[...]
    d_ref[...] = _dot(x, ws_ref[...])
    ar = jnp.broadcast_to(are_ref[...], (nb, STATE_TILE))
    ai = jnp.broadcast_to(aim_ref[...], (nb, STATE_TILE))

    def body(c, carry):
        hr, hi = carry
        r0 = pl.multiple_of(c * nb, nb)
        hs_ref[pl.ds(r0, nb), 0:STATE_TILE] = hr
        hs_ref[pl.ds(r0, nb), STATE_TILE:2 * STATE_TILE] = hi
        d = d_ref[pl.ds(r0, nb), :]
        return (ar * hr - ai * hi + d[:, 0:STATE_TILE],
                ar * hi + ai * hr + d[:, STATE_TILE:2 * STATE_TILE])

    hr, hi = lax.fori_loop(0, cb, body, (hr_ref[...], hi_ref[...]))
    hr_ref[...] = hr
    hi_ref[...] = hi
    y_ref[...] = _dot(x, t_ref[...]) + _dot(hs_ref[...].astype(BF16), wo_ref[...])

    @pl.when(ci == pl.num_programs(1) - 1)
    def _():
        fin_ref[:, 0:STATE_TILE] = hr
        fin_ref[:, STATE_TILE:2 * STATE_TILE] = hi


def _s5(ub, h0, weights, nb, nc, L, chunk_block):
    w_st, w_so, toep, a_re, a_im = weights
    xc = _to_chunks(ub, nb, nc, L)
    cb = min(chunk_block, nc)
    rows = cb * nb
    lk = L * LANES
    st2 = 2 * STATE_TILE
    tile_w = lambda shape: pl.BlockSpec((None,) + shape, lambda j, c: (j, 0, 0))
    y, fin = pl.pallas_call(
        functools.partial(_s5_kernel, cb=cb, nb=nb),
        grid=(N_CH_TILES, nc // cb),
        in_specs=[pl.BlockSpec((rows, lk), lambda j, c: (c, j)),
                  pl.BlockSpec((nb, st2), lambda j, c: (0, j)),
                  tile_w((1, STATE_TILE)), tile_w((1, STATE_TILE)),
                  tile_w((lk, st2)), tile_w((lk, lk)), tile_w((st2, lk))],
        out_specs=[pl.BlockSpec((rows, lk), lambda j, c: (c, j)),
                   pl.BlockSpec((nb, st2), lambda j, c: (0, j))],
        out_shape=[jax.ShapeDtypeStruct((nc * nb, N_CH_TILES * lk), F32),
                   jax.ShapeDtypeStruct((nb, N_CH_TILES * st2), F32)],
        scratch_shapes=[pltpu.VMEM((nb, STATE_TILE), F32), pltpu.VMEM((nb, STATE_TILE), F32),
                        pltpu.VMEM((rows, st2), F32), pltpu.VMEM((rows, st2), F32)],
        compiler_params=_cparams("parallel", "arbitrary"),
        name="s5_chunked_scan",
    )(xc, h0, a_re, a_im, w_st, toep, w_so)
    return _from_chunks(y, nb, nc, L), fin


def _state_to_tiles(h_re, h_im):
    nb = h_re.shape[0]
    r = h_re.astype(F32).reshape(nb, N_CH_TILES, STATE_TILE)
    i = h_im.astype(F32).reshape(nb, N_CH_TILES, STATE_TILE)
    return jnp.concatenate([r, i], axis=-1).reshape(nb, N_CH_TILES * 2 * STATE_TILE)


def _tiles_to_state(h):
    nb = h.shape[0]
    h = h.reshape(nb, N_CH_TILES, 2, GROUPS_PER_TILE, SSM_STATE)
    return (h[:, :, 0].reshape(nb, SSM_GROUPS, SSM_STATE), h[:, :, 1].reshape(nb, SSM_GROUPS, SSM_STATE))


def _t5_bucket(dist):
    n = np.maximum(dist, 0)
    max_exact = REL_BUCKETS // 2
    nf = np.maximum(n, 1).astype(np.float32)
    large = max_exact + (np.log(nf / np.float32(max_exact)) / np.float32(math.log(REL_MAX_DIST / max_exact))
                         * np.float32(REL_BUCKETS - max_exact)).astype(np.int32)
    large = np.minimum(large, REL_BUCKETS - 1)
    return np.where(n < max_exact, n, large)


def _rel_bias(rel_table, dist):
    bucket = _t5_bucket(dist)
    tab = rel_table.astype(F32)
    out = jnp.zeros((SWA_HEADS,) + dist.shape, F32)
    for b in range(REL_BUCKETS):
        sel = jnp.asarray(bucket == b)
        if bool((bucket == b).any()):
            out = jnp.where(sel[None], tab[b].reshape((SWA_HEADS,) + (1,) * dist.ndim), out)
    return out


def _swa_prompt_kernel(sink_ref, q_ref, kp_ref, kc_ref, vp_ref, vc_ref, bias_ref, o_ref, kk_ref, vv_ref, *, qblocks):
    step = pl.program_id(1)
    kk_ref[0:WINDOW, :] = kp_ref[...].astype(BF16)
    kk_ref[WINDOW:, :] = kc_ref[...].astype(BF16)
    vv_ref[0:WINDOW, :] = vp_ref[...].astype(BF16)
    vv_ref[WINDOW:, :] = vc_ref[...].astype(BF16)
    row = lax.broadcasted_iota(jnp.int32, (WINDOW, 2 * WINDOW), 0)
    col = lax.broadcasted_iota(jnp.int32, (WINDOW, 2 * WINDOW), 1)
    dist = row + WINDOW - col
    band = (dist >= 0) & (dist < WINDOW)
    lane = lax.broadcasted_iota(jnp.int32, (WINDOW, LANES), 1)
    low = lane < SWA_HEAD_DIM

    def block(j, carry):
        r0 = pl.multiple_of(j * WINDOW, WINDOW)
        kk = kk_ref[pl.ds(r0, 2 * WINDOW), :]
        vv = vv_ref[pl.ds(r0, 2 * WINDOW), :]
        valid = band & ((col >= WINDOW) | (step * qblocks + j > 0))
        for t in range(SWA_REP):
            q2 = q_ref[pl.ds(r0, WINDOW), t * LANES:(t + 1) * LANES]
            outs = []
            for half in range(SWA_KV_HEADS):
                h = t + SWA_REP * half
                qh = jnp.where(low if half == 0 else jnp.logical_not(low), q2, jnp.zeros_like(q2))
                s = lax.dot_general(qh, kk, _TRANS_B, preferred_element_type=F32)
                s = jnp.where(valid, s + bias_ref[h], NEG_INF)
                sink = sink_ref[h]
                m = jnp.maximum(jnp.max(s, axis=-1, keepdims=True), sink)
                e = jnp.exp(s - m)
                den = jnp.sum(e, axis=-1, keepdims=True) + jnp.exp(sink - m)
                outs.append(_dot(e.astype(BF16), vv) * (1.0 / den))
            o_ref[pl.ds(r0, WINDOW), t * LANES:(t + 1) * LANES] = jnp.where(low, outs[0], outs[1]).astype(BF16)
        return carry

    lax.fori_loop(0, qblocks, block, 0)


def _swa_prompt(q, k, v, bias, sinks, nb, t, qblocks):
    nstep = t // (WINDOW * qblocks)
    rows = WINDOW * qblocks
    cur = lambda b, i: (b * nstep + i, 0)
    prev = lambda b, i: (b * nstep * qblocks + jnp.maximum(i * qblocks - 1, 0), 0)
    return pl.pallas_call(
        functools.partial(_swa_prompt_kernel, qblocks=qblocks),
        grid=(nb, nstep),
        in_specs=[pl.BlockSpec(memory_space=pltpu.SMEM),
                  pl.BlockSpec((rows, SWA_WIDTH), cur),
                  pl.BlockSpec((WINDOW, SWA_KV_WIDTH), prev),
                  pl.BlockSpec((rows, SWA_KV_WIDTH), cur),
                  pl.BlockSpec((WINDOW, SWA_KV_WIDTH), prev),
                  pl.BlockSpec((rows, SWA_KV_WIDTH), cur),
                  pl.BlockSpec((SWA_HEADS, WINDOW, 2 * WINDOW), lambda b, i: (0, 0, 0))],
        out_specs=pl.BlockSpec((rows, SWA_WIDTH), cur),
        out_shape=jax.ShapeDtypeStruct((nb * t, SWA_WIDTH), BF16),
        scratch_shapes=[pltpu.VMEM((rows + WINDOW, SWA_KV_WIDTH), BF16),
                        pltpu.VMEM((rows + WINDOW, SWA_KV_WIDTH), BF16)],
        compiler_params=_cparams("parallel", "parallel"),
        name="swa_prompt",
    )(sinks, q, k, k, v, v, bias)


def _swa_decode_kernel(q_ref, k_ref, v_ref, bias_ref, sink_ref, o_ref, *, seqs, tq):
    rows, keys = q_ref.shape[1], k_ref.shape[1]
    qi = lax.broadcasted_iota(jnp.int32, (rows, keys), 0) % tq
    col = lax.broadcasted_iota(jnp.int32, (rows, keys), 1)
    dist = qi + WINDOW - col
    valid = (dist >= 0) & (dist < WINDOW)
    bias = bias_ref[...]
    sink = sink_ref[...]
    for s_i in range(seqs):
        kk = k_ref[s_i].astype(BF16)
        s = lax.dot_general(q_ref[s_i], kk, _TRANS_B, preferred_element_type=F32)
        s = jnp.where(valid, s + bias, NEG_INF)
        m = jnp.maximum(jnp.max(s, axis=-1, keepdims=True), sink)
        e = jnp.exp(s - m)
        den = jnp.sum(e, axis=-1, keepdims=True) + jnp.exp(sink - m)
        o_ref[s_i] = _dot(e.astype(BF16), v_ref[s_i].astype(BF16)) * (1.0 / den)


def _swa_decode(qz, k_all, v_all, bias, sink_rows, tq, seqs):
    nseq, rows, _ = qz.shape
    keys = k_all.shape[1]
    seqs = min(seqs, nseq)
    return pl.pallas_call(
        functools.partial(_swa_decode_kernel, seqs=seqs, tq=tq),
        grid=(nseq // seqs,),
        in_specs=[pl.BlockSpec((seqs, rows, LANES), lambda i: (i, 0, 0)),
                  pl.BlockSpec((seqs, keys, LANES), lambda i: (i, 0, 0)),
                  pl.BlockSpec((seqs, keys, LANES), lambda i: (i, 0, 0)),
                  pl.BlockSpec((rows, keys), lambda i: (0, 0)),
                  pl.BlockSpec((rows, 1), lambda i: (0, 0))],
        out_specs=pl.BlockSpec((seqs, rows, LANES), lambda i: (i, 0, 0)),
        out_shape=jax.ShapeDtypeStruct((nseq, rows, LANES), F32),
        compiler_params=_cparams("parallel"),
        name="swa_decode",
    )(qz, k_all, v_all, bias, sink_rows)


def _softmax(s):
    m = jnp.max(s, axis=-1, keepdims=True)
    e = jnp.exp(s - m)
    return e * (1.0 / jnp.sum(e, axis=-1, keepdims=True))


def _mem_prompt_kernel(q_ref, k_ref, v_ref, o_ref, s_ref, p_ref):
    scale = MEM_HEAD_DIM ** -0.5
    heads = [slice(h * MEM_HEAD_DIM, (h + 1) * MEM_HEAD_DIM) for h in range(MEM_HEADS)]
    for h, sl in enumerate(heads):
        s_ref[h] = lax.dot_general(q_ref[:, sl], k_ref[:, sl].astype(BF16), _TRANS_B, preferred_element_type=F32)
    s = s_ref[...] * scale
    e = jnp.exp(s - jnp.max(s, axis=-1, keepdims=True))
    p_ref[...] = e.astype(BF16)
    inv = 1.0 / jnp.sum(e, axis=-1, keepdims=True)
    for h, sl in enumerate(heads):
        o_ref[:, sl] = (_dot(p_ref[h], v_ref[:, sl].astype(BF16)) * inv[h]).astype(BF16)


def _mem_prompt(qm, mk, mv, nb, t, tile):
    tile = min(tile, t)
    nt = t // tile
    return pl.pallas_call(
        _mem_prompt_kernel,
        grid=(nb, nt),
        in_specs=[pl.BlockSpec((tile, MEM_WIDTH), lambda b, i: (b * nt + i, 0)),
                  pl.BlockSpec((MEM_TOKENS, MEM_WIDTH), lambda b, i: (b, 0)),
                  pl.BlockSpec((MEM_TOKENS, MEM_WIDTH), lambda b, i: (b, 0))],
        out_specs=pl.BlockSpec((tile, MEM_WIDTH), lambda b, i: (b * nt + i, 0)),
        out_shape=jax.ShapeDtypeStruct((nb * t, MEM_WIDTH), BF16),
        scratch_shapes=[pltpu.VMEM((MEM_HEADS, tile, MEM_TOKENS), F32), pltpu.VMEM((MEM_HEADS, tile, MEM_TOKENS), BF16)],
        compiler_params=_cparams("parallel", "parallel"),
        name="mem_prompt",
    )(qm, mk, mv)


def _mem_decode_kernel(q_ref, k_ref, v_ref, o_ref, *, seqs):
    tq = q_ref.shape[1]
    rows, cols = MEM_HEADS * tq, MEM_TOKENS * MEM_HEADS
    k2 = k_ref.reshape(seqs, cols, MEM_HEAD_DIM)
    v2 = v_ref.reshape(seqs, cols, MEM_HEAD_DIM)
    scale = MEM_HEAD_DIM ** -0.5
    own = (lax.broadcasted_iota(jnp.int32, (rows, cols), 1) % MEM_HEADS
           == lax.broadcasted_iota(jnp.int32, (rows, cols), 0) // tq)
    for s_i in range(seqs):
        q = q_ref[s_i]
        qb = jnp.concatenate([q[:, h * MEM_HEAD_DIM:(h + 1) * MEM_HEAD_DIM] for h in range(MEM_HEADS)], axis=0)
        s = lax.dot_general(qb.astype(BF16), k2[s_i].astype(BF16), _TRANS_B, preferred_element_type=F32) * scale
        p = _softmax(jnp.where(own, s, NEG_INF)).astype(BF16)
        o = _dot(p, v2[s_i].astype(BF16))
        for h in range(MEM_HEADS):
            o_ref[s_i, :, h * MEM_HEAD_DIM:(h + 1) * MEM_HEAD_DIM] = o[h * tq:(h + 1) * tq, :]


def _mem_decode(q, k, v, layer, seqs):
    nseq, tq, _ = q.shape
    seqs = min(seqs, nseq)
    cache = pl.BlockSpec((None, seqs, MEM_TOKENS, MEM_HEADS, MEM_HEAD_DIM), lambda i: (layer, i, 0, 0, 0))
    return pl.pallas_call(
        functools.partial(_mem_decode_kernel, seqs=seqs),
        grid=(nseq // seqs,),
        in_specs=[pl.BlockSpec((seqs, tq, MEM_WIDTH), lambda i: (i, 0, 0)), cache, cache],
        out_specs=pl.BlockSpec((seqs, tq, MEM_WIDTH), lambda i: (i, 0, 0)),
        out_shape=jax.ShapeDtypeStruct((nseq, tq, MEM_WIDTH), F32),
        compiler_params=_cparams("parallel"),
        name="mem_decode",
    )(q, k, v)


ROUTER_ROWS = 40
ROUTE_ROWS = 8
HALF = D_MODEL // 2


def _pack_halves(xb):
    hi = pltpu.bitcast(xb[:, 0:HALF].astype(F32), jnp.int32)
    lo = pltpu.bitcast(xb[:, HALF:D_MODEL].astype(F32), jnp.int32)
    return hi | lax.shift_right_logical(lo, jnp.int32(16))


def _unpack_halves(p):
    hi = pltpu.bitcast(p & jnp.int32(-65536), F32).astype(BF16)
    lo = pltpu.bitcast(lax.shift_left(p, jnp.int32(16)), F32).astype(BF16)
    return hi, lo


def _merge_kernel(x_ref, u_ref, y_ref, os_ref, om_ref, g1_ref, wg_ref, dsk_ref, wglu_ref, bglu_ref,
                  wbs_ref, wbw_ref, wbm_ref, wout_ref, g2_ref, wr_ref, br_ref,
                  h_ref, xn2_ref, route_ref):
    x = x_ref[...]
    tt = x.shape[0]
    xb = _rms(x, g1_ref[...]).astype(BF16)
    z = jax.nn.gelu(y_ref[...] + dsk_ref[...] * u_ref[...])
    z = z * jax.nn.sigmoid(_dot(z.astype(BF16), wglu_ref[...]) + bglu_ref[...])
    merged = jax.nn.sigmoid(_dot(xb, wg_ref[:, 0:D_MODEL])) * _dot(z.astype(BF16), wbs_ref[...])
    merged = merged + jax.nn.sigmoid(_dot(xb, wg_ref[:, D_MODEL:2 * D_MODEL])) * _dot(os_ref[...], wbw_ref[...])
    merged = merged + jax.nn.sigmoid(_dot(xb, wg_ref[:, 2 * D_MODEL:3 * D_MODEL])) * _dot(om_ref[...], wbm_ref[...])
    h = x + _dot(merged.astype(BF16), wout_ref[...])
    h_ref[...] = h
    xn2 = _rms(h, g2_ref[...]).astype(BF16)
    xn2_ref[...] = _pack_halves(xn2)

    lt = lax.dot_general(wr_ref[...], xn2, _TRANS_B, preferred_element_type=F32) + br_ref[...]
    gl = lt[N_EXPERTS:N_EXPERTS + N_EXPERT_GROUPS]
    ge = jnp.exp(gl - jnp.max(gl, axis=0, keepdims=True))
    gp = ge / jnp.sum(ge, axis=0, keepdims=True)
    gw = jnp.max(gp, axis=0, keepdims=True)
    gidx = jnp.full((1, tt), N_EXPERT_GROUPS - 1, jnp.int32)
    for r in range(N_EXPERT_GROUPS - 2, -1, -1):
        gidx = jnp.where(gp[r:r + 1] == gw, r, gidx)
    ein = lt[(N_EXPERT_GROUPS - 1) * EXPERTS_PER_GROUP:N_EXPERTS]
    for r in range(N_EXPERT_GROUPS - 2, -1, -1):
        ein = jnp.where(gidx == r, lt[r * EXPERTS_PER_GROUP:(r + 1) * EXPERTS_PER_GROUP], ein)
    ee = jnp.exp(ein - jnp.max(ein, axis=0, keepdims=True))
    ep = ee / jnp.sum(ee, axis=0, keepdims=True)
    rowi = lax.broadcasted_iota(jnp.int32, (EXPERTS_PER_GROUP, tt), 0)
    p1 = jnp.max(ep, axis=0, keepdims=True)
    e1 = jnp.min(jnp.where(ep == p1, rowi, EXPERTS_PER_GROUP), axis=0, keepdims=True)
    ep2 = jnp.where(rowi == e1, -1.0, ep)
    p2 = jnp.max(ep2, axis=0, keepdims=True)
    e2 = jnp.min(jnp.where(ep2 == p2, rowi, EXPERTS_PER_GROUP), axis=0, keepdims=True)
    tot = p1 + p2
    w1 = p1 / tot * gw
    w2 = p2 / tot * gw
    id1 = (gidx * EXPERTS_PER_GROUP + e1).astype(F32)
    id2 = (gidx * EXPERTS_PER_GROUP + e2).astype(F32)
    route_ref[...] = jnp.concatenate([id1, id2, w1, w2, jnp.zeros((ROUTE_ROWS - 4, tt), F32)], axis=0)


def _merge(x, u, y, o_swa, o_mem, p, tile):
    n = x.shape[0]
    tile = min(tile, n)
    row = lambda i: (i, 0)
    const = lambda i: (0, 0)
    full = lambda a: pl.BlockSpec(a.shape, const, pipeline_mode=pl.Buffered(1))
    weights = [p['g1'], p['w_gates'], p['d_skip'], p['w_glu'], p['b_glu'], p['w_br_ssm'], p['w_br_swa'],
               p['w_br_mem'], p['w_out'], p['g2'], p['w_router'], p['b_router']]
    return pl.pallas_call(
        _merge_kernel,
        grid=(n // tile,),
        in_specs=[pl.BlockSpec((tile, D_MODEL), row), pl.BlockSpec((tile, SSM_WIDTH), row),
                  pl.BlockSpec((tile, SSM_WIDTH), row), pl.BlockSpec((tile, SWA_WIDTH), row),
                  pl.BlockSpec((tile, MEM_WIDTH), row)] + [full(w) for w in weights],
        out_specs=[pl.BlockSpec((tile, D_MODEL), row), pl.BlockSpec((tile, HALF), row),
                   pl.BlockSpec((ROUTE_ROWS, tile), lambda i: (0, i))],
        out_shape=[jax.ShapeDtypeStruct((n, D_MODEL), F32), jax.ShapeDtypeStruct((n, HALF), jnp.int32),
                   jax.ShapeDtypeStruct((ROUTE_ROWS, n), F32)],
        compiler_params=_cparams("parallel"),
        name="merge_router",
    )(x, u, y, o_swa, o_mem, *weights)


def _expert_mlp(xp, wg, wu, wd):
    hi, lo = _unpack_halves(xp)
    g = _dot(hi, wg[0:HALF, :]) + _dot(lo, wg[HALF:D_MODEL, :])
    u = _dot(hi, wu[0:HALF, :]) + _dot(lo, wu[HALF:D_MODEL, :])
    hh = jax.nn.silu(g) * u
    return _dot(hh.astype(BF16), wd[...])


def _moe_kernel(xn2_ref, rt_ref, wg_ref, wu_ref, wd_ref, h_ref, gf_ref, o_ref, acc_ref):
    e = pl.program_id(1)

    @pl.when(e == 0)
    def _():
        acc_ref[...] = jnp.zeros_like(acc_ref)

    o = _expert_mlp(xn2_ref[...], wg_ref[...].astype(BF16), wu_ref[...].astype(BF16), wd_ref[...].astype(BF16))
    ef = e.astype(F32)
    c = (jnp.where(rt_ref[:, 0:1] == ef, rt_ref[:, 2:3], 0.0)
         + jnp.where(rt_ref[:, 1:2] == ef, rt_ref[:, 3:4], 0.0))
    acc_ref[...] += c * o

    @pl.when(e == N_EXPERTS - 1)
    def _():
        o_ref[...] = _rms(h_ref[...] + acc_ref[...], gf_ref[...])


def _moe(xn2, route_t, w_g, w_u, w_d, h, gf, tile):
    n = h.shape[0]
    tile = min(tile, n)
    return pl.pallas_call(
        _moe_kernel,
        grid=(n // tile, N_EXPERTS),
        in_specs=[pl.BlockSpec((tile, HALF), lambda i, e: (i, 0)),
                  pl.BlockSpec((tile, ROUTE_ROWS), lambda i, e: (i, 0)),
                  pl.BlockSpec((None, D_MODEL, D_EXPERT), lambda i, e: (e, 0, 0)),
                  pl.BlockSpec((None, D_MODEL, D_EXPERT), lambda i, e: (e, 0, 0)),
                  pl.BlockSpec((None, D_EXPERT, D_MODEL), lambda i, e: (e, 0, 0)),
                  pl.BlockSpec((tile, D_MODEL), lambda i, e: (i, 0)),
                  pl.BlockSpec((1, D_MODEL), lambda i, e: (0, 0))],
        out_specs=pl.BlockSpec((tile, D_MODEL), lambda i, e: (i, 0)),
        out_shape=jax.ShapeDtypeStruct((n, D_MODEL), F32),
        scratch_shapes=[pltpu.VMEM((tile, D_MODEL), F32)],
        compiler_params=_cparams("parallel", "arbitrary"),
        name="moe_final_norm",
    )(xn2, route_t, w_g, w_u, w_d, h, gf)


EXPERT_ROW_TILE = 1024
SC_CORES = 2
SC_SUBCORES = 16
SC_WORKERS = SC_CORES * SC_SUBCORES
SC_SCATTER_ROWS = 64
SC_GATHER_ROWS = 64


def _route_rank_kernel(r_ref, rank_ref, cnt_ref, base_ref):
    i = pl.program_id(0)
    tt = r_ref.shape[1]

    @pl.when(i == 0)
    def _():
        base_ref[...] = jnp.zeros_like(base_ref)

    ids = r_ref[0:2, :].astype(jnp.int32)
    e_iota = lax.broadcasted_iota(jnp.int32, (N_EXPERTS, tt), 0)
    oh1 = jnp.where(e_iota == ids[0:1], 1.0, 0.0)
    oh2 = jnp.where(e_iota == ids[1:2], 1.0, 0.0)
    before = (lax.broadcasted_iota(jnp.int32, (tt, tt), 0) < lax.broadcasted_iota(jnp.int32, (tt, tt), 1))
    tri = jnp.where(before, 1.0, 0.0).astype(BF16)
    c1 = _dot(oh1.astype(BF16), tri)
    c2 = _dot(oh2.astype(BF16), tri)
    tot1 = jnp.sum(oh1, axis=1, keepdims=True)
    tot2 = jnp.sum(oh2, axis=1, keepdims=True)
    base = base_ref[:, 0:1]
    rank1 = jnp.sum(oh1 * (base + c1), axis=0, keepdims=True)
    rank2 = jnp.sum(oh2 * (base + tot1 + c2), axis=0, keepdims=True)
    rank_ref[...] = jnp.concatenate([rank1, rank2, jnp.zeros((ROUTE_ROWS - 2, tt), F32)], axis=0).astype(jnp.int32)
    new_base = jnp.broadcast_to(base + tot1 + tot2, base_ref.shape)
    base_ref[...] = new_base
    cnt_ref[...] = new_base.astype(jnp.int32)


def _route_rank(route, tile):
    n = route.shape[1]
    tile = min(tile, n)
    return pl.pallas_call(
        _route_rank_kernel,
        grid=(n // tile,),
        in_specs=[pl.BlockSpec((ROUTE_ROWS, tile), lambda i: (0, i))],
        out_specs=[pl.BlockSpec((ROUTE_ROWS, tile), lambda i: (0, i)),
                   pl.BlockSpec((N_EXPERTS, LANES), lambda i: (0, 0))],
        out_shape=[jax.ShapeDtypeStruct((ROUTE_ROWS, n), jnp.int32),
                   jax.ShapeDtypeStruct((N_EXPERTS, LANES), jnp.int32)],
        scratch_shapes=[pltpu.VMEM((N_EXPERTS, LANES), F32)],
        compiler_params=_cparams("arbitrary"),
        name="route_rank",
    )(route)


def _sc_mesh():
    return plsc.VectorSubcoreMesh(core_axis_name="core", subcore_axis_name="subcore")


def _sc_scatter_pairs(x, pos, rows_out):
    n, d = x.shape
    per_w = n // SC_WORKERS
    window = min(SC_SCATTER_ROWS, per_w)

    @pl.kernel(out_type=jax.ShapeDtypeStruct((rows_out, d), x.dtype), mesh=_sc_mesh(),
               scratch_types=[pltpu.VMEM((window,), jnp.int32), pltpu.VMEM((window,), jnp.int32),
                              pltpu.VMEM((window, d), x.dtype), pltpu.SemaphoreType.DMA, pltpu.SemaphoreType.DMA,
                              pltpu.SemaphoreType.DMA])
    def scatter(x_hbm, p_hbm, o_hbm, i1_v, i2_v, rows_v, sem_a, sem_b, sem_c):
        wid = lax.axis_index("subcore") * SC_CORES + lax.axis_index("core")

        @pl.loop(0, per_w // window)
        def _(j):
            base = wid * per_w + j * window
            load_i1 = pltpu.async_copy(p_hbm.at[pl.ds(base, window)], i1_v, sem_a)
            load_i2 = pltpu.async_copy(p_hbm.at[pl.ds(n + base, window)], i2_v, sem_b)
            load_x = pltpu.async_copy(x_hbm.at[pl.ds(base, window)], rows_v, sem_c)
            load_i1.wait()
            load_i2.wait()
            load_x.wait()
            put_1 = pltpu.async_copy(rows_v, o_hbm.at[i1_v], sem_a)
            put_2 = pltpu.async_copy(rows_v, o_hbm.at[i2_v], sem_b)
            put_1.wait()
            put_2.wait()

    return scatter(x, pos)


def _sc_gather_rows(table, idx):
    m = idx.shape[0]
    d = table.shape[1]
    per_w = m // SC_WORKERS
    window = min(SC_GATHER_ROWS, per_w)

    assert per_w % (2 * window) == 0

    @pl.kernel(out_type=jax.ShapeDtypeStruct((m, d), table.dtype), mesh=_sc_mesh(),
               scratch_types=[pltpu.VMEM((window,), jnp.int32), pltpu.VMEM((window,), jnp.int32),
                              pltpu.VMEM((window, d), table.dtype), pltpu.VMEM((window, d), table.dtype),
                              pltpu.SemaphoreType.DMA, pltpu.SemaphoreType.DMA])
    def gather(t_hbm, i_hbm, o_hbm, ia_v, ib_v, ra_v, rb_v, sem_a, sem_b):
        wid = lax.axis_index("subcore") * SC_CORES + lax.axis_index("core")

        @pl.loop(0, per_w // (2 * window))
        def _(j):
            base_a = wid * per_w + j * (2 * window)
            base_b = base_a + window
            idx_a = pltpu.async_copy(i_hbm.at[pl.ds(base_a, window)], ia_v, sem_a)
            idx_b = pltpu.async_copy(i_hbm.at[pl.ds(base_b, window)], ib_v, sem_b)
            idx_a.wait()
            get_a = pltpu.async_copy(t_hbm.at[ia_v], ra_v, sem_a)
            idx_b.wait()
            get_b = pltpu.async_copy(t_hbm.at[ib_v], rb_v, sem_b)
            get_a.wait()
            put_a = pltpu.async_copy(ra_v, o_hbm.at[pl.ds(base_a, window)], sem_a)
            get_b.wait()
            put_b = pltpu.async_copy(rb_v, o_hbm.at[pl.ds(base_b, window)], sem_b)
            put_a.wait()
            put_b.wait()

    return gather(table, idx)


def _expert_tiles_kernel(te_ref, nu_ref, x_ref, wg_ref, wu_ref, wd_ref, o_ref, wg_s, wu_s, wd_s):
    i = pl.program_id(0)

    @pl.when(i < nu_ref[0])
    def _():
        @pl.when(jnp.logical_or(i == 0, te_ref[i] != te_ref[jnp.maximum(i - 1, 0)]))
        def _():
            wg_s[...] = wg_ref[...].astype(BF16)
            wu_s[...] = wu_ref[...].astype(BF16)
            wd_s[...] = wd_ref[...].astype(BF16)

        o_ref[...] = _pack_halves(_expert_mlp(x_ref[...], wg_s, wu_s, wd_s).astype(BF16))


def _expert_tiles(tile_expert, n_used, xs, w_g, w_u, w_d):
    rows = xs.shape[0]
    tm = EXPERT_ROW_TILE
    grid_spec = pltpu.PrefetchScalarGridSpec(
        num_scalar_prefetch=2,
        grid=(rows // tm,),
        in_specs=[pl.BlockSpec((tm, HALF), lambda i, te, nu: (i, 0)),
                  pl.BlockSpec((None, D_MODEL, D_EXPERT), lambda i, te, nu: (te[i], 0, 0)),
                  pl.BlockSpec((None, D_MODEL, D_EXPERT), lambda i, te, nu: (te[i], 0, 0)),
                  pl.BlockSpec((None, D_EXPERT, D_MODEL), lambda i, te, nu: (te[i], 0, 0))],
        out_specs=pl.BlockSpec((tm, HALF), lambda i, te, nu: (i, 0)),
        scratch_shapes=[pltpu.VMEM((D_MODEL, D_EXPERT), BF16), pltpu.VMEM((D_MODEL, D_EXPERT), BF16),
                        pltpu.VMEM((D_EXPERT, D_MODEL), BF16)],
    )
    return pl.pallas_call(
        _expert_tiles_kernel,
        grid_spec=grid_spec,
        out_shape=jax.ShapeDtypeStruct((rows, HALF), jnp.int32),
        compiler_params=_cparams("arbitrary"),
        name="expert_tiles",
    )(tile_expert, n_used, xs, w_g, w_u, w_d)


def _unpack_f32(p):
    return pltpu.bitcast(p & jnp.int32(-65536), F32), pltpu.bitcast(lax.shift_left(p, jnp.int32(16)), F32)


def _combine_kernel(h_ref, o1_ref, o2_ref, rt_ref, gf_ref, y_ref):
    w1, w2 = rt_ref[:, 2:3], rt_ref[:, 3:4]
    a_lo, a_hi = _unpack_f32(o1_ref[...])
    b_lo, b_hi = _unpack_f32(o2_ref[...])
    y_lo = h_ref[:, 0:HALF] + (w1 * a_lo + w2 * b_lo)
    y_hi = h_ref[:, HALF:D_MODEL] + (w1 * a_hi + w2 * b_hi)
    ms = (jnp.sum(y_lo * y_lo, axis=-1, keepdims=True) + jnp.sum(y_hi * y_hi, axis=-1, keepdims=True)) / D_MODEL
    inv = lax.rsqrt(ms + EPS)
    y_ref[:, 0:HALF] = (y_lo * inv) * gf_ref[:, 0:HALF]
    y_ref[:, HALF:D_MODEL] = (y_hi * inv) * gf_ref[:, HALF:D_MODEL]


def _combine(h, o12, route_t, gf, tile):
    n = h.shape[0]
    tile = min(tile, n)
    nt = n // tile
    return pl.pallas_call(
        _combine_kernel,
        grid=(nt,),
        in_specs=[pl.BlockSpec((tile, D_MODEL), lambda i: (i, 0)),
                  pl.BlockSpec((tile, HALF), lambda i: (i, 0)),
                  pl.BlockSpec((tile, HALF), lambda i: (i + nt, 0)),
                  pl.BlockSpec((tile, ROUTE_ROWS), lambda i: (i, 0)),
                  pl.BlockSpec((1, D_MODEL), lambda i: (0, 0))],
        out_specs=pl.BlockSpec((tile, D_MODEL), lambda i: (i, 0)),
        out_shape=jax.ShapeDtypeStruct((n, D_MODEL), F32),
        compiler_params=_cparams("parallel"),
        name="combine_final_norm",
    )(h, o12, o12, route_t, gf)


def _sparse_moe(xn2p, route, h, w_g, w_u, w_d, gf, run_before_experts):
    n = h.shape[0]
    tm = EXPERT_ROW_TILE
    rows = 2 * n + N_EXPERTS * tm
    rank, cnt = _route_rank(route, 512)
    counts = cnt[:, 0]
    padded = (counts + tm - 1) // tm * tm
    e_idx = jnp.arange(N_EXPERTS, dtype=jnp.int32)
    starts = jnp.sum(jnp.where(e_idx[None, :] < e_idx[:, None], padded[None, :], 0), axis=1)
    ends = starts + padded
    ids = route[0:2].astype(jnp.int32)
    start_of = jnp.sum(jnp.where(ids[None] == e_idx[:, None, None], starts[:, None, None], 0), axis=0)
    pos = (start_of + rank[0:2]).reshape(2 * n)
    tile_start = jnp.arange(rows // tm, dtype=jnp.int32) * tm
    tile_expert = jnp.minimum(jnp.sum((tile_start[:, None] >= ends[None, :]).astype(jnp.int32), axis=1),
                              N_EXPERTS - 1)
    n_used = (ends[-1:] // tm).astype(jnp.int32)
    xs = _sc_scatter_pairs(xn2p, pos, rows)
    xs, _ = lax.optimization_barrier((xs, run_before_experts))
    os_ = _expert_tiles(tile_expert, n_used, xs, w_g, w_u, w_d)
    o12 = _sc_gather_rows(os_, pos)
    return _combine(h, o12, route.T, gf, 512)


def _prep_in_weights(w_in):
    o = 0
    w_u = w_in[:, o:o + SSM_WIDTH]; o += SSM_WIDTH
    w_q = w_in[:, o:o + SWA_WIDTH]; o += SWA_WIDTH
    w_k = w_in[:, o:o + SWA_KV_WIDTH]; o += SWA_KV_WIDTH
    w_v = w_in[:, o:o + SWA_KV_WIDTH]; o += SWA_KV_WIDTH
    w_qm = w_in[:, o:o + MEM_WIDTH]; o += MEM_WIDTH
    w_g = w_in[:, o:]
    wq = (w_q * (SWA_HEAD_DIM ** -0.5)).reshape(D_MODEL, SWA_KV_HEADS, SWA_REP, SWA_HEAD_DIM)
    wq = wq.transpose(0, 2, 1, 3).reshape(D_MODEL, SWA_WIDTH)
    w_main = jnp.concatenate([w_u, wq, w_k, w_v, w_qm], axis=1).astype(BF16)
    return w_main, w_g.astype(BF16)


IN_SPLITS = (SSM_WIDTH, SWA_WIDTH, SWA_KV_WIDTH, SWA_KV_WIDTH, MEM_WIDTH)
IN_DTYPES = ((F32, BF16), (BF16,), (F32,), (F32,), (BF16,))


def kernel(x_prompt, x_sample, cache_swa_k, cache_swa_v, state_ssm_re, state_ssm_im, cache_mem_k, cache_mem_v, mem_prompt, norm1_g, w_in, lam_re, lam_im, log_dt, bm_re, bm_im, cm_re, cm_im, d_skip, w_glu, b_glu, sinks, rel_table, mem_norm_g, w_mem_kv, w_br_ssm, w_br_swa, w_br_mem, w_out, norm2_g, w_rg, b_rg, w_rexp, b_rexp, w_e_gate, w_e_up, w_e_down, final_norm_g):
    nb, t, _ = x_prompt.shape
    ns, ts, _ = x_sample.shape
    l = 0
    L = S5_CHUNK

    w_main, w_gates = _prep_in_weights(w_in[l])
    w_swa = (w_br_swa[l].reshape(SWA_KV_HEADS, SWA_REP, SWA_HEAD_DIM, D_MODEL).transpose(1, 0, 2, 3)
             .reshape(SWA_WIDTH, D_MODEL))
    pad_rows = ROUTER_ROWS - N_EXPERTS - N_EXPERT_GROUPS
    w_router = jnp.concatenate([w_rexp[l].T, w_rg[l].T, jnp.zeros((pad_rows, D_MODEL), F32)], axis=0).astype(BF16)
    b_router = jnp.concatenate([b_rexp[l], b_rg[l], jnp.zeros((pad_rows,), F32)]).reshape(ROUTER_ROWS, 1)
    mp = {
        'g1': norm1_g[l].reshape(1, D_MODEL), 'w_gates': w_gates, 'd_skip': d_skip[l].reshape(1, SSM_WIDTH),
        'w_glu': w_glu[l].astype(BF16), 'b_glu': b_glu[l].reshape(1, SSM_WIDTH),
        'w_br_ssm': w_br_ssm[l].astype(BF16), 'w_br_swa': w_swa.astype(BF16),
        'w_br_mem': w_br_mem[l].astype(BF16), 'w_out': w_out[l].astype(BF16),
        'g2': norm2_g[l].reshape(1, D_MODEL), 'w_router': w_router, 'b_router': b_router,
    }
    w_g, w_u, w_d = w_e_gate[l], w_e_up[l], w_e_down[l]
    gf = final_norm_g.reshape(1, D_MODEL)
    s5_w = _s5_weights(lam_re[l], lam_im[l], log_dt[l], bm_re[l], bm_im[l], cm_re[l], cm_im[l], L)

    bias_p = _rel_bias(rel_table, np.arange(WINDOW)[:, None] + WINDOW - np.arange(2 * WINDOW)[None, :])
    keys_s = WINDOW + 2 * ts
    bias_s = _rel_bias(rel_table, np.arange(ts)[:, None] + WINDOW - np.arange(keys_s)[None, :])
    bias_s = bias_s.reshape(SWA_HEADS * ts, keys_s)
    sink_rows = jnp.repeat(sinks[l].astype(F32), ts).reshape(SWA_HEADS * ts, 1)

    n = nb * t
    xp = x_prompt.reshape(n, D_MODEL)
    mk, mv = _norm_proj(mem_prompt.reshape(nb * MEM_TOKENS, D_MODEL), mem_norm_g[l].reshape(1, D_MODEL),
                        w_mem_kv[l].astype(BF16), (MEM_WIDTH, MEM_WIDTH), ((F32,), (F32,)), 512)
    u, ub, qz, k, v, qm = _norm_proj(xp, mp['g1'], w_main, IN_SPLITS, IN_DTYPES, 512)

    y_ssm, fin = _s5(ub, jnp.zeros((nb, N_CH_TILES * 2 * STATE_TILE), F32), s5_w, nb, t // L, L, 64)
    p_re, p_im = _tiles_to_state(fin)

    o_swa = _swa_prompt(qz, k, v, bias_p, sinks[l].astype(F32), nb, t, 4)
    o_mem = _mem_prompt(qm, mk, mv, nb, t, 512)
    h, xn2p, route = _merge(xp, u, y_ssm, o_swa, o_mem, mp, 512)

    k4 = k.reshape(nb, t, SWA_KV_HEADS, SWA_HEAD_DIM)
    v4 = v.reshape(nb, t, SWA_KV_HEADS, SWA_HEAD_DIM)
    new_k_p, new_v_p = k4[:, -WINDOW:][None], v4[:, -WINDOW:][None]
    new_mk = mk.reshape(1, nb, MEM_TOKENS, MEM_HEADS, MEM_HEAD_DIM)
    new_mv = mv.reshape(1, nb, MEM_TOKENS, MEM_HEADS, MEM_HEAD_DIM)

    m = ns * ts
    xs = x_sample.reshape(m, D_MODEL)
    us, ubs, qzs, k_s, v_s, qms = _norm_proj(xs, mp['g1'], w_main, IN_SPLITS, IN_DTYPES, 256)
    ys_ssm, fins = _s5(ubs, _state_to_tiles(state_ssm_re[l], state_ssm_im[l]), s5_w, ns, ts // L, L, 64)
    s_re, s_im = _tiles_to_state(fins)

    kk_all = jnp.concatenate([cache_swa_k[l].reshape(ns, WINDOW, SWA_KV_WIDTH).astype(F32),
                              k_s.reshape(ns, ts, SWA_KV_WIDTH)], axis=1)
    vv_all = jnp.concatenate([cache_swa_v[l].reshape(ns, WINDOW, SWA_KV_WIDTH).astype(F32),
                              v_s.reshape(ns, ts, SWA_KV_WIDTH)], axis=1)
    pad = jnp.zeros((ns, keys_s - WINDOW - ts, SWA_KV_WIDTH), F32)
    q5 = qzs.reshape(ns, ts, SWA_REP, SWA_KV_HEADS, SWA_HEAD_DIM)
    zq = jnp.zeros((ns, ts, SWA_REP, SWA_HEAD_DIM), BF16)
    q_rows = jnp.concatenate([jnp.concatenate([q5[:, :, :, 0], zq], axis=-1),
                              jnp.concatenate([zq, q5[:, :, :, 1]], axis=-1)], axis=2)
    q_rows = q_rows.transpose(0, 2, 1, 3).reshape(ns, SWA_HEADS * ts, LANES)
    o_dec = _swa_decode(q_rows, jnp.concatenate([kk_all, pad], axis=1), jnp.concatenate([vv_all, pad], axis=1),
                        bias_s, sink_rows, ts, 8)
    o_dec = o_dec.reshape(ns, SWA_KV_HEADS, SWA_REP, ts, SWA_KV_HEADS, SWA_HEAD_DIM)
    o_dec = jnp.stack([o_dec[:, g, :, :, g] for g in range(SWA_KV_HEADS)], axis=1)
    o_swa_s = o_dec.transpose(0, 3, 2, 1, 4).reshape(m, SWA_WIDTH).astype(BF16)

    o_mem_s = _mem_decode(qms.astype(F32).reshape(ns, ts, MEM_WIDTH), cache_mem_k, cache_mem_v, l, 8)
    o_mem_s = o_mem_s.reshape(m, MEM_WIDTH).astype(BF16)

    y_prompt = _sparse_moe(xn2p, route, h, w_g, w_u, w_d, gf, (ys_ssm, o_swa_s, o_mem_s)).reshape(nb, t, D_MODEL)
    hs_, xn2ps, routes = _merge(xs, us, ys_ssm, o_swa_s, o_mem_s, mp, 256)
    y_sample = _moe(xn2ps, routes.T, w_g, w_u, w_d, hs_, gf, 1024).reshape(ns, ts, D_MODEL)

    roll = lambda cache, new: jnp.concatenate(
        [cache[:, :, ts:], new.reshape(1, ns, ts, SWA_KV_HEADS, SWA_HEAD_DIM).astype(cache.dtype)], axis=2)
    new_k_s, new_v_s = roll(cache_swa_k, k_s), roll(cache_swa_v, v_s)

    return (y_prompt, y_sample,
            new_k_p, new_v_p, p_re[None], p_im[None], new_mk, new_mv,
            new_k_s, new_v_s, s_re[None].astype(state_ssm_re.dtype), s_im[None].astype(state_ssm_im.dtype))
```

```python
import functools
import math

import numpy as np
import jax
import jax.numpy as jnp
from jax import lax
from jax.experimental import pallas as pl
from jax.experimental.pallas import tpu as pltpu
from jax.experimental.pallas import tpu_sc as plsc

F32 = jnp.float32
BF16 = jnp.bfloat16

D_MODEL = 1024
SSM_WIDTH = 512
SSM_GROUP = 16
SSM_GROUPS = 32
SSM_STATE = 64
SWA_HEADS = 8
SWA_KV_HEADS = 2
SWA_REP = 4
SWA_HEAD_DIM = 64
SWA_WIDTH = 512
SWA_KV_WIDTH = 128
WINDOW = 128
REL_BUCKETS = 32
REL_MAX_DIST = 128
MEM_TOKENS = 256
MEM_HEADS = 4
MEM_HEAD_DIM = 128
MEM_WIDTH = 512
N_EXPERT_GROUPS = 4
EXPERTS_PER_GROUP = 8
N_EXPERTS = 32
D_EXPERT = 256
EPS = 1e-6
NEG_INF = -1e30

LANES = 128
GROUPS_PER_TILE = LANES // SSM_GROUP
N_CH_TILES = SSM_WIDTH // LANES
STATE_TILE = GROUPS_PER_TILE * SSM_STATE
VMEM_LIMIT = 56 * 1024 * 1024
S5_CHUNK = 8
S5_PANEL = 256

_TRANS_B = (((1,), (1,)), ((), ()))


def _cparams(*sem):
    return pltpu.CompilerParams(dimension_semantics=sem, vmem_limit_bytes=VMEM_LIMIT)


def _rms(x, g):
    return (x * lax.rsqrt(jnp.mean(x * x, axis=-1, keepdims=True) + EPS)) * g


def _dot(a, b):
    return jnp.dot(a, b, preferred_element_type=F32)


def _norm_proj_kernel(x_ref, g_ref, w_ref, *out_refs, splits, dtypes):
    xb = _rms(x_ref[...], g_ref[...]).astype(BF16)
    off = 0
    outs = iter(out_refs)
    for width, dts in zip(splits, dtypes):
        r = _dot(xb, w_ref[:, off:off + width])
        for dt in dts:
            next(outs)[...] = r.astype(dt)
        off += width


def _norm_proj(x, g, w, splits, dtypes, tile):
    n, d = x.shape
    tile = min(tile, n)
    flat = [(wd, dt) for wd, dts in zip(splits, dtypes) for dt in dts]
    return pl.pallas_call(
        functools.partial(_norm_proj_kernel, splits=tuple(splits), dtypes=tuple(dtypes)),
        grid=(n // tile,),
        in_specs=[pl.BlockSpec((tile, d), lambda i: (i, 0)),
                  pl.BlockSpec((1, d), lambda i: (0, 0)),
                  pl.BlockSpec((d, sum(splits)), lambda i: (0, 0))],
        out_specs=[pl.BlockSpec((tile, wd), lambda i: (i, 0)) for wd, _ in flat],
        out_shape=[jax.ShapeDtypeStruct((n, wd), dt) for wd, dt in flat],
        compiler_params=_cparams("parallel"),
        name="norm_proj",
    )(x, g, w)


def _s5_weights(lam_re, lam_im, log_dt, bm_re, bm_im, cm_re, cm_im, L):
    hp = lax.Precision.HIGHEST
    nt, gt, P, H = N_CH_TILES, GROUPS_PER_TILE, SSM_STATE, SSM_GROUP
    lr, li = lam_re.astype(F32), lam_im.astype(F32)
    dt = jnp.exp(log_dt.astype(F32))[:, None]
    mag = jnp.exp(lr * dt)
    a_re = mag * jnp.cos(li * dt)
    a_im = mag * jnp.sin(li * dt)
    den = lr * lr + li * li
    f_re = ((a_re - 1.0) * lr + a_im * li) / den
    f_im = (a_im * lr - (a_re - 1.0) * li) / den
    br, bi = bm_re.astype(F32), bm_im.astype(F32)
    bb_re = f_re[..., None] * br - f_im[..., None] * bi
    bb_im = f_re[..., None] * bi + f_im[..., None] * br
    pr, pi = [jnp.ones_like(a_re)], [jnp.zeros_like(a_im)]
    for _ in range(L):
        pr.append(pr[-1] * a_re - pi[-1] * a_im)
        pi.append(pr[-2] * a_im + pi[-1] * a_re)
    ap_re, ap_im = jnp.stack(pr), jnp.stack(pi)
    cr, ci = cm_re.astype(F32), cm_im.astype(F32)
    ca_re = cr[None] * ap_re[:, :, None, :] - ci[None] * ap_im[:, :, None, :]
    ca_im = cr[None] * ap_im[:, :, None, :] + ci[None] * ap_re[:, :, None, :]

    rev_re = jnp.stack([pr[L - 1 - s] for s in range(L)])
    rev_im = jnp.stack([pi[L - 1 - s] for s in range(L)])
    ws_re = rev_re[..., None] * bb_re[None] - rev_im[..., None] * bb_im[None]
    ws_im = rev_re[..., None] * bb_im[None] + rev_im[..., None] * bb_re[None]
    c_st = jnp.concatenate([ws_re.transpose(0, 1, 3, 2).reshape(L, nt, gt * H, P),
                            ws_im.transpose(0, 1, 3, 2).reshape(L, nt, gt * H, P)], axis=3).transpose(1, 0, 2, 3)
    so = lambda ca: ca[1:].transpose(1, 3, 0, 2).reshape(nt, gt * P, L * H)
    c_so = jnp.concatenate([so(ca_re), so(-ca_im)], axis=1)
    k_lag = (jnp.einsum('tghp,gpk->gkth', ca_re[:L], bb_re, precision=hp)
             - jnp.einsum('tghp,gpk->gkth', ca_im[:L], bb_im, precision=hp))
    c_k = k_lag.reshape(nt, gt * H, L * H)
    w_st, w_out, toep = _s5_expand(c_st, c_so, c_k, L)

    def per_tile(v):
        return v.reshape(nt, 1, STATE_TILE)

    return w_st, w_out, toep, per_tile(pr[L]), per_tile(pi[L])


def _s5_expand_kernel(cst_ref, cso_ref, ck_ref, wst_ref, wso_ref, toep_ref, *, L):
    hp = lax.Precision.HIGHEST
    P, H = SSM_STATE, SSM_GROUP
    iota = lambda shape, d: lax.broadcasted_iota(jnp.int32, shape, d)
    one = lambda cond: jnp.where(cond, 1.0, 0.0).astype(F32)

    r, c = iota((2 * P, 2 * STATE_TILE), 0), iota((2 * P, 2 * STATE_TILE), 1)
    rep_st = one((r // P == c // STATE_TILE) & (r % P == c % P))
    r, c = iota((LANES, 2 * STATE_TILE), 0), iota((LANES, 2 * STATE_TILE), 1)
    own_st = one(r // H == (c % STATE_TILE) // P)
    for s in range(L):
        blk = jnp.dot(cst_ref[s], rep_st, precision=hp, preferred_element_type=F32) * own_st
        wst_ref[s * LANES:(s + 1) * LANES, :] = blk.astype(BF16)

    r, c = iota((LANES, LANES), 0), iota((LANES, LANES), 1)
    pick = [one((r // H == t) & (r % H == c % H)) for t in range(L)]
    own_k = one(r // H == c // H)
    r, c = iota((2 * STATE_TILE, LANES), 0), iota((2 * STATE_TILE, LANES), 1)
    own_so = one((r % STATE_TILE) // P == c // H)
    cso = cso_ref[...]
    for t in range(L):
        blk = jnp.dot(cso, pick[t], precision=hp, preferred_element_type=F32) * own_so
        wso_ref[:, t * LANES:(t + 1) * LANES] = blk.astype(BF16)
    ck = ck_ref[...]
    lag = [(jnp.dot(ck, pick[t], precision=hp, preferred_element_type=F32) * own_k).astype(BF16) for t in range(L)]
    zero = jnp.zeros((LANES, LANES), BF16)
    for s in range(L):
        for t in range(L):
            toep_ref[s * LANES:(s + 1) * LANES, t * LANES:(t + 1) * LANES] = lag[t - s] if t >= s else zero


def _s5_expand(c_st, c_so, c_k, L):
    lk = L * LANES
    st2 = 2 * STATE_TILE
    return pl.pallas_call(
        functools.partial(_s5_expand_kernel, L=L),
        grid=(N_CH_TILES,),
        in_specs=[pl.BlockSpec((None, L, LANES, 2 * SSM_STATE), lambda j: (j, 0, 0, 0)),
                  pl.BlockSpec((None, st2, L * SSM_GROUP), lambda j: (j, 0, 0)),
                  pl.BlockSpec((None, LANES, L * SSM_GROUP), lambda j: (j, 0, 0))],
        out_specs=[pl.BlockSpec((None, lk, st2), lambda j: (j, 0, 0)),
                   pl.BlockSpec((None, st2, lk), lambda j: (j, 0, 0)),
                   pl.BlockSpec((None, lk, lk), lambda j: (j, 0, 0))],
        out_shape=[jax.ShapeDtypeStruct((N_CH_TILES, lk, st2), BF16),
                   jax.ShapeDtypeStruct((N_CH_TILES, st2, lk), BF16),
                   jax.ShapeDtypeStruct((N_CH_TILES, lk, lk), BF16)],
        compiler_params=_cparams("parallel"),
        name="s5_expand_weights",
    )(c_st, c_so, c_k)


def _to_chunks(u, nb, nc, L):
    return (u.reshape(nb, nc, L, N_CH_TILES, LANES).transpose(1, 0, 3, 2, 4)
            .reshape(nc * nb, N_CH_TILES * L * LANES))


def _from_chunks(y, nb, nc, L):
    return (y.reshape(nc, nb, N_CH_TILES, L, LANES).transpose(1, 0, 3, 2, 4)
            .reshape(nb * nc * L, SSM_WIDTH))


def _s5_kernel(x_ref, h0_ref, are_ref, aim_ref, ws_ref, t_ref, wo_ref, y_ref, fin_ref,
               hr_ref, hi_ref, d_ref, hs_ref, *, cb, nb):
    ci = pl.program_id(1)

    @pl.when(ci == 0)
    def _():
        hr_ref[...] = h0_ref[:, 0:STATE_TILE]
        hi_ref[...] = h0_ref[:, STATE_TILE:2 * STATE_TILE]

    x = x_ref[...]
    d_ref[...] = _dot(x, ws_ref[...])
    ar = jnp.broadcast_to(are_ref[...], (nb, STATE_TILE))
    ai = jnp.broadcast_to(aim_ref[...], (nb, STATE_TILE))

    def body(c, carry):
        hr, hi = carry
        r0 = pl.multiple_of(c * nb, nb)
        hs_ref[pl.ds(r0, nb), 0:STATE_TILE] = hr
        hs_ref[pl.ds(r0, nb), STATE_TILE:2 * STATE_TILE] = hi
        d = d_ref[pl.ds(r0, nb), :]
        return (ar * hr - ai * hi + d[:, 0:STATE_TILE],
                ar * hi + ai * hr + d[:, STATE_TILE:2 * STATE_TILE])

    hr, hi = lax.fori_loop(0, cb, body, (hr_ref[...], hi_ref[...]))
    hr_ref[...] = hr
    hi_ref[...] = hi
    hsb = hs_ref[...].astype(BF16)
    for c0 in range(0, t_ref.shape[1], S5_PANEL):
        c1 = c0 + S5_PANEL
        y_ref[:, c0:c1] = _dot(x[:, 0:c1], t_ref[0:c1, c0:c1]) + _dot(hsb, wo_ref[:, c0:c1])

    @pl.when(ci == pl.num_programs(1) - 1)
    def _():
        fin_ref[:, 0:STATE_TILE] = hr
        fin_ref[:, STATE_TILE:2 * STATE_TILE] = hi


def _s5(ub, h0, weights, nb, nc, L, chunk_block):
    w_st, w_so, toep, a_re, a_im = weights
    xc = _to_chunks(ub, nb, nc, L)
    cb = min(chunk_block, nc)
    rows = cb * nb
    lk = L * LANES
    st2 = 2 * STATE_TILE
    tile_w = lambda shape: pl.BlockSpec((None,) + shape, lambda j, c: (j, 0, 0))
    y, fin = pl.pallas_call(
        functools.partial(_s5_kernel, cb=cb, nb=nb),
        grid=(N_CH_TILES, nc // cb),
        in_specs=[pl.BlockSpec((rows, lk), lambda j, c: (c, j)),
                  pl.BlockSpec((nb, st2), lambda j, c: (0, j)),
                  tile_w((1, STATE_TILE)), tile_w((1, STATE_TILE)),
                  tile_w((lk, st2)), tile_w((lk, lk)), tile_w((st2, lk))],
        out_specs=[pl.BlockSpec((rows, lk), lambda j, c: (c, j)),
                   pl.BlockSpec((nb, st2), lambda j, c: (0, j))],
        out_shape=[jax.ShapeDtypeStruct((nc * nb, N_CH_TILES * lk), F32),
                   jax.ShapeDtypeStruct((nb, N_CH_TILES * st2), F32)],
        scratch_shapes=[pltpu.VMEM((nb, STATE_TILE), F32), pltpu.VMEM((nb, STATE_TILE), F32),
                        pltpu.VMEM((rows, st2), F32), pltpu.VMEM((rows, st2), F32)],
        compiler_params=_cparams("parallel", "arbitrary"),
        name="s5_chunked_scan",
    )(xc, h0, a_re, a_im, w_st, toep, w_so)
    return _from_chunks(y, nb, nc, L), fin


def _state_to_tiles(h_re, h_im):
    nb = h_re.shape[0]
    r = h_re.astype(F32).reshape(nb, N_CH_TILES, STATE_TILE)
    i = h_im.astype(F32).reshape(nb, N_CH_TILES, STATE_TILE)
    return jnp.concatenate([r, i], axis=-1).reshape(nb, N_CH_TILES * 2 * STATE_TILE)


def _tiles_to_state(h):
    nb = h.shape[0]
    h = h.reshape(nb, N_CH_TILES, 2, GROUPS_PER_TILE, SSM_STATE)
    return (h[:, :, 0].reshape(nb, SSM_GROUPS, SSM_STATE), h[:, :, 1].reshape(nb, SSM_GROUPS, SSM_STATE))


def _t5_bucket(dist):
    n = np.maximum(dist, 0)
    max_exact = REL_BUCKETS // 2
    nf = np.maximum(n, 1).astype(np.float32)
    large = max_exact + (np.log(nf / np.float32(max_exact)) / np.float32(math.log(REL_MAX_DIST / max_exact))
                         * np.float32(REL_BUCKETS - max_exact)).astype(np.int32)
    large = np.minimum(large, REL_BUCKETS - 1)
    return np.where(n < max_exact, n, large)


def _rel_bias(rel_table, dist):
    bucket = _t5_bucket(dist)
    tab = rel_table.astype(F32)
    out = jnp.zeros((SWA_HEADS,) + dist.shape, F32)
    for b in range(REL_BUCKETS):
        sel = jnp.asarray(bucket == b)
        if bool((bucket == b).any()):
            out = jnp.where(sel[None], tab[b].reshape((SWA_HEADS,) + (1,) * dist.ndim), out)
    return out


def _swa_prompt_kernel(sink_ref, q_ref, kp_ref, kc_ref, vp_ref, vc_ref, bias_ref, o_ref, kk_ref, vv_ref, *, qblocks):
    step = pl.program_id(1)
    kk_ref[0:WINDOW, :] = kp_ref[...].astype(BF16)
    kk_ref[WINDOW:, :] = kc_ref[...].astype(BF16)
    vv_ref[0:WINDOW, :] = vp_ref[...].astype(BF16)
    vv_ref[WINDOW:, :] = vc_ref[...].astype(BF16)
    row = lax.broadcasted_iota(jnp.int32, (WINDOW, 2 * WINDOW), 0)
    col = lax.broadcasted_iota(jnp.int32, (WINDOW, 2 * WINDOW), 1)
    dist = row + WINDOW - col
    band = (dist >= 0) & (dist < WINDOW)
    lane = lax.broadcasted_iota(jnp.int32, (WINDOW, LANES), 1)
    low = lane < SWA_HEAD_DIM

    def block(j, carry):
        r0 = pl.multiple_of(j * WINDOW, WINDOW)
        kk = kk_ref[pl.ds(r0, 2 * WINDOW), :]
        vv = vv_ref[pl.ds(r0, 2 * WINDOW), :]
        valid = band & ((col >= WINDOW) | (step * qblocks + j > 0))
        for t in range(SWA_REP):
            q2 = q_ref[pl.ds(r0, WINDOW), t * LANES:(t + 1) * LANES]
            outs = []
            for half in range(SWA_KV_HEADS):
                h = t + SWA_REP * half
                qh = jnp.where(low if half == 0 else jnp.logical_not(low), q2, jnp.zeros_like(q2))
                s = lax.dot_general(qh, kk, _TRANS_B, preferred_element_type=F32)
                s = jnp.where(valid, s + bias_ref[h], NEG_INF)
                sink = sink_ref[h]
                m = jnp.maximum(jnp.max(s, axis=-1, keepdims=True), sink)
                e = jnp.exp(s - m)
                den = jnp.sum(e, axis=-1, keepdims=True) + jnp.exp(sink - m)
                outs.append(_dot(e.astype(BF16), vv) * (1.0 / den))
            o_ref[pl.ds(r0, WINDOW), t * LANES:(t + 1) * LANES] = jnp.where(low, outs[0], outs[1]).astype(BF16)
        return carry

    lax.fori_loop(0, qblocks, block, 0)


def _swa_prompt(q, k, v, bias, sinks, nb, t, qblocks):
    nstep = t // (WINDOW * qblocks)
    rows = WINDOW * qblocks
    cur = lambda b, i: (b * nstep + i, 0)
    prev = lambda b, i: (b * nstep * qblocks + jnp.maximum(i * qblocks - 1, 0), 0)
    return pl.pallas_call(
        functools.partial(_swa_prompt_kernel, qblocks=qblocks),
        grid=(nb, nstep),
        in_specs=[pl.BlockSpec(memory_space=pltpu.SMEM),
                  pl.BlockSpec((rows, SWA_WIDTH), cur),
                  pl.BlockSpec((WINDOW, SWA_KV_WIDTH), prev),
                  pl.BlockSpec((rows, SWA_KV_WIDTH), cur),
                  pl.BlockSpec((WINDOW, SWA_KV_WIDTH), prev),
                  pl.BlockSpec((rows, SWA_KV_WIDTH), cur),
                  pl.BlockSpec((SWA_HEADS, WINDOW, 2 * WINDOW), lambda b, i: (0, 0, 0))],
        out_specs=pl.BlockSpec((rows, SWA_WIDTH), cur),
        out_shape=jax.ShapeDtypeStruct((nb * t, SWA_WIDTH), BF16),
        scratch_shapes=[pltpu.VMEM((rows + WINDOW, SWA_KV_WIDTH), BF16),
                        pltpu.VMEM((rows + WINDOW, SWA_KV_WIDTH), BF16)],
        compiler_params=_cparams("parallel", "parallel"),
        name="swa_prompt",
    )(sinks, q, k, k, v, v, bias)


def _swa_decode_kernel(q_ref, k_ref, v_ref, bias_ref, sink_ref, o_ref, *, seqs, tq):
    rows, keys = q_ref.shape[1], k_ref.shape[1]
    qi = lax.broadcasted_iota(jnp.int32, (rows, keys), 0) % tq
    col = lax.broadcasted_iota(jnp.int32, (rows, keys), 1)
    dist = qi + WINDOW - col
    valid = (dist >= 0) & (dist < WINDOW)
    bias = bias_ref[...]
    sink = sink_ref[...]
    for s_i in range(seqs):
        kk = k_ref[s_i].astype(BF16)
        s = lax.dot_general(q_ref[s_i], kk, _TRANS_B, preferred_element_type=F32)
        s = jnp.where(valid, s + bias, NEG_INF)
        m = jnp.maximum(jnp.max(s, axis=-1, keepdims=True), sink)
        e = jnp.exp(s - m)
        den = jnp.sum(e, axis=-1, keepdims=True) + jnp.exp(sink - m)
        o_ref[s_i] = _dot(e.astype(BF16), v_ref[s_i].astype(BF16)) * (1.0 / den)


def _swa_decode(qz, k_all, v_all, bias, sink_rows, tq, seqs):
    nseq, rows, _ = qz.shape
    keys = k_all.shape[1]
    seqs = min(seqs, nseq)
    return pl.pallas_call(
        functools.partial(_swa_decode_kernel, seqs=seqs, tq=tq),
        grid=(nseq // seqs,),
        in_specs=[pl.BlockSpec((seqs, rows, LANES), lambda i: (i, 0, 0)),
                  pl.BlockSpec((seqs, keys, LANES), lambda i: (i, 0, 0)),
                  pl.BlockSpec((seqs, keys, LANES), lambda i: (i, 0, 0)),
                  pl.BlockSpec((rows, keys), lambda i: (0, 0)),
                  pl.BlockSpec((rows, 1), lambda i: (0, 0))],
        out_specs=pl.BlockSpec((seqs, rows, LANES), lambda i: (i, 0, 0)),
        out_shape=jax.ShapeDtypeStruct((nseq, rows, LANES), F32),
        compiler_params=_cparams("parallel"),
        name="swa_decode",
    )(qz, k_all, v_all, bias, sink_rows)


def _softmax(s):
    m = jnp.max(s, axis=-1, keepdims=True)
    e = jnp.exp(s - m)
    return e * (1.0 / jnp.sum(e, axis=-1, keepdims=True))


def _mem_prompt_kernel(q_ref, k_ref, v_ref, o_ref, s_ref, p_ref):
    scale = MEM_HEAD_DIM ** -0.5
    heads = [slice(h * MEM_HEAD_DIM, (h + 1) * MEM_HEAD_DIM) for h in range(MEM_HEADS)]
    for h, sl in enumerate(heads):
        s_ref[h] = lax.dot_general(q_ref[:, sl], k_ref[:, sl].astype(BF16), _TRANS_B, preferred_element_type=F32)
    s = s_ref[...] * scale
    e = jnp.exp(s - jnp.max(s, axis=-1, keepdims=True))
    p_ref[...] = e.astype(BF16)
    inv = 1.0 / jnp.sum(e, axis=-1, keepdims=True)
    for h, sl in enumerate(heads):
        o_ref[:, sl] = (_dot(p_ref[h], v_ref[:, sl].astype(BF16)) * inv[h]).astype(BF16)


def _mem_prompt(qm, mk, mv, nb, t, tile):
    tile = min(tile, t)
    nt = t // tile
    return pl.pallas_call(
        _mem_prompt_kernel,
        grid=(nb, nt),
        in_specs=[pl.BlockSpec((tile, MEM_WIDTH), lambda b, i: (b * nt + i, 0)),
                  pl.BlockSpec((MEM_TOKENS, MEM_WIDTH), lambda b, i: (b, 0)),
                  pl.BlockSpec((MEM_TOKENS, MEM_WIDTH), lambda b, i: (b, 0))],
        out_specs=pl.BlockSpec((tile, MEM_WIDTH), lambda b, i: (b * nt + i, 0)),
        out_shape=jax.ShapeDtypeStruct((nb * t, MEM_WIDTH), BF16),
        scratch_shapes=[pltpu.VMEM((MEM_HEADS, tile, MEM_TOKENS), F32), pltpu.VMEM((MEM_HEADS, tile, MEM_TOKENS), BF16)],
        compiler_params=_cparams("parallel", "parallel"),
        name="mem_prompt",
    )(qm, mk, mv)


def _mem_decode_kernel(q_ref, k_ref, v_ref, o_ref, *, seqs):
    tq = q_ref.shape[1]
    rows, cols = MEM_HEADS * tq, MEM_TOKENS * MEM_HEADS
    k2 = k_ref.reshape(seqs, cols, MEM_HEAD_DIM)
    v2 = v_ref.reshape(seqs, cols, MEM_HEAD_DIM)
    scale = MEM_HEAD_DIM ** -0.5
    own = (lax.broadcasted_iota(jnp.int32, (rows, cols), 1) % MEM_HEADS
           == lax.broadcasted_iota(jnp.int32, (rows, cols), 0) // tq)
    for s_i in range(seqs):
        q = q_ref[s_i]
        qb = jnp.concatenate([q[:, h * MEM_HEAD_DIM:(h + 1) * MEM_HEAD_DIM] for h in range(MEM_HEADS)], axis=0)
        s = lax.dot_general(qb.astype(BF16), k2[s_i].astype(BF16), _TRANS_B, preferred_element_type=F32) * scale
        p = _softmax(jnp.where(own, s, NEG_INF)).astype(BF16)
        o = _dot(p, v2[s_i].astype(BF16))
        for h in range(MEM_HEADS):
            o_ref[s_i, :, h * MEM_HEAD_DIM:(h + 1) * MEM_HEAD_DIM] = o[h * tq:(h + 1) * tq, :]


def _mem_decode(q, k, v, layer, seqs):
    nseq, tq, _ = q.shape
    seqs = min(seqs, nseq)
    cache = pl.BlockSpec((None, seqs, MEM_TOKENS, MEM_HEADS, MEM_HEAD_DIM), lambda i: (layer, i, 0, 0, 0))
    return pl.pallas_call(
        functools.partial(_mem_decode_kernel, seqs=seqs),
        grid=(nseq // seqs,),
        in_specs=[pl.BlockSpec((seqs, tq, MEM_WIDTH), lambda i: (i, 0, 0)), cache, cache],
        out_specs=pl.BlockSpec((seqs, tq, MEM_WIDTH), lambda i: (i, 0, 0)),
        out_shape=jax.ShapeDtypeStruct((nseq, tq, MEM_WIDTH), F32),
        compiler_params=_cparams("parallel"),
        name="mem_decode",
    )(q, k, v)


ROUTER_ROWS = 40
ROUTE_ROWS = 8
HALF = D_MODEL // 2


def _pack_halves(xb):
    hi = pltpu.bitcast(xb[:, 0:HALF].astype(F32), jnp.int32)
    lo = pltpu.bitcast(xb[:, HALF:D_MODEL].astype(F32), jnp.int32)
    return hi | lax.shift_right_logical(lo, jnp.int32(16))


def _unpack_halves(p):
    hi = pltpu.bitcast(p & jnp.int32(-65536), F32).astype(BF16)
    lo = pltpu.bitcast(lax.shift_left(p, jnp.int32(16)), F32).astype(BF16)
    return hi, lo


def _merge_kernel(x_ref, u_ref, y_ref, os_ref, om_ref, g1_ref, wg_ref, dsk_ref, wglu_ref, bglu_ref,
                  wbs_ref, wbw_ref, wbm_ref, wout_ref, g2_ref, wr_ref, br_ref,
                  h_ref, xn2_ref, route_ref):
    x = x_ref[...]
    tt = x.shape[0]
    xb = _rms(x, g1_ref[...]).astype(BF16)
    z = jax.nn.gelu(y_ref[...] + dsk_ref[...] * u_ref[...])
    z = z * jax.nn.sigmoid(_dot(z.astype(BF16), wglu_ref[...]) + bglu_ref[...])
    merged = jax.nn.sigmoid(_dot(xb, wg_ref[:, 0:D_MODEL])) * _dot(z.astype(BF16), wbs_ref[...])
    merged = merged + jax.nn.sigmoid(_dot(xb, wg_ref[:, D_MODEL:2 * D_MODEL])) * _dot(os_ref[...], wbw_ref[...])
    merged = merged + jax.nn.sigmoid(_dot(xb, wg_ref[:, 2 * D_MODEL:3 * D_MODEL])) * _dot(om_ref[...], wbm_ref[...])
    h = x + _dot(merged.astype(BF16), wout_ref[...])
    h_ref[...] = h
    xn2 = _rms(h, g2_ref[...]).astype(BF16)
    xn2_ref[...] = _pack_halves(xn2)

    lt = lax.dot_general(wr_ref[...], xn2, _TRANS_B, preferred_element_type=F32) + br_ref[...]
    gl = lt[N_EXPERTS:N_EXPERTS + N_EXPERT_GROUPS]
    ge = jnp.exp(gl - jnp.max(gl, axis=0, keepdims=True))
    gp = ge / jnp.sum(ge, axis=0, keepdims=True)
    gw = jnp.max(gp, axis=0, keepdims=True)
    gidx = jnp.full((1, tt), N_EXPERT_GROUPS - 1, jnp.int32)
    for r in range(N_EXPERT_GROUPS - 2, -1, -1):
        gidx = jnp.where(gp[r:r + 1] == gw, r, gidx)
    ein = lt[(N_EXPERT_GROUPS - 1) * EXPERTS_PER_GROUP:N_EXPERTS]
    for r in range(N_EXPERT_GROUPS - 2, -1, -1):
        ein = jnp.where(gidx == r, lt[r * EXPERTS_PER_GROUP:(r + 1) * EXPERTS_PER_GROUP], ein)
    ee = jnp.exp(ein - jnp.max(ein, axis=0, keepdims=True))
    ep = ee / jnp.sum(ee, axis=0, keepdims=True)
    rowi = lax.broadcasted_iota(jnp.int32, (EXPERTS_PER_GROUP, tt), 0)
    p1 = jnp.max(ep, axis=0, keepdims=True)
    e1 = jnp.min(jnp.where(ep == p1, rowi, EXPERTS_PER_GROUP), axis=0, keepdims=True)
    ep2 = jnp.where(rowi == e1, -1.0, ep)
    p2 = jnp.max(ep2, axis=0, keepdims=True)
    e2 = jnp.min(jnp.where(ep2 == p2, rowi, EXPERTS_PER_GROUP), axis=0, keepdims=True)
    tot = p1 + p2
    w1 = p1 / tot * gw
    w2 = p2 / tot * gw
    id1 = (gidx * EXPERTS_PER_GROUP + e1).astype(F32)
    id2 = (gidx * EXPERTS_PER_GROUP + e2).astype(F32)
    route_ref[...] = jnp.concatenate([id1, id2, w1, w2, jnp.zeros((ROUTE_ROWS - 4, tt), F32)], axis=0)


def _merge(x, u, y, o_swa, o_mem, p, tile):
    n = x.shape[0]
    tile = min(tile, n)
    row = lambda i: (i, 0)
    const = lambda i: (0, 0)
    full = lambda a: pl.BlockSpec(a.shape, const, pipeline_mode=pl.Buffered(1))
    weights = [p['g1'], p['w_gates'], p['d_skip'], p['w_glu'], p['b_glu'], p['w_br_ssm'], p['w_br_swa'],
               p['w_br_mem'], p['w_out'], p['g2'], p['w_router'], p['b_router']]
    return pl.pallas_call(
        _merge_kernel,
        grid=(n // tile,),
        in_specs=[pl.BlockSpec((tile, D_MODEL), row), pl.BlockSpec((tile, SSM_WIDTH), row),
                  pl.BlockSpec((tile, SSM_WIDTH), row), pl.BlockSpec((tile, SWA_WIDTH), row),
                  pl.BlockSpec((tile, MEM_WIDTH), row)] + [full(w) for w in weights],
        out_specs=[pl.BlockSpec((tile, D_MODEL), row), pl.BlockSpec((tile, HALF), row),
                   pl.BlockSpec((ROUTE_ROWS, tile), lambda i: (0, i))],
        out_shape=[jax.ShapeDtypeStruct((n, D_MODEL), F32), jax.ShapeDtypeStruct((n, HALF), jnp.int32),
                   jax.ShapeDtypeStruct((ROUTE_ROWS, n), F32)],
        compiler_params=_cparams("parallel"),
        name="merge_router",
    )(x, u, y, o_swa, o_mem, *weights)


def _expert_mlp(xp, wg, wu, wd):
    hi, lo = _unpack_halves(xp)
    g = _dot(hi, wg[0:HALF, :]) + _dot(lo, wg[HALF:D_MODEL, :])
    u = _dot(hi, wu[0:HALF, :]) + _dot(lo, wu[HALF:D_MODEL, :])
    hh = jax.nn.silu(g) * u
    return _dot(hh.astype(BF16), wd[...])


def _moe_kernel(xn2_ref, rt_ref, wg_ref, wu_ref, wd_ref, h_ref, gf_ref, o_ref, acc_ref):
    e = pl.program_id(1)

    @pl.when(e == 0)
    def _():
        acc_ref[...] = jnp.zeros_like(acc_ref)

    o = _expert_mlp(xn2_ref[...], wg_ref[...].astype(BF16), wu_ref[...].astype(BF16), wd_ref[...].astype(BF16))
    ef = e.astype(F32)
    c = (jnp.where(rt_ref[:, 0:1] == ef, rt_ref[:, 2:3], 0.0)
         + jnp.where(rt_ref[:, 1:2] == ef, rt_ref[:, 3:4], 0.0))
    acc_ref[...] += c * o

    @pl.when(e == N_EXPERTS - 1)
    def _():
        o_ref[...] = _rms(h_ref[...] + acc_ref[...], gf_ref[...])


def _moe(xn2, route_t, w_g, w_u, w_d, h, gf, tile):
    n = h.shape[0]
    tile = min(tile, n)
    return pl.pallas_call(
        _moe_kernel,
        grid=(n // tile, N_EXPERTS),
        in_specs=[pl.BlockSpec((tile, HALF), lambda i, e: (i, 0)),
                  pl.BlockSpec((tile, ROUTE_ROWS), lambda i, e: (i, 0)),
                  pl.BlockSpec((None, D_MODEL, D_EXPERT), lambda i, e: (e, 0, 0)),
                  pl.BlockSpec((None, D_MODEL, D_EXPERT), lambda i, e: (e, 0, 0)),
                  pl.BlockSpec((None, D_EXPERT, D_MODEL), lambda i, e: (e, 0, 0)),
                  pl.BlockSpec((tile, D_MODEL), lambda i, e: (i, 0)),
                  pl.BlockSpec((1, D_MODEL), lambda i, e: (0, 0))],
        out_specs=pl.BlockSpec((tile, D_MODEL), lambda i, e: (i, 0)),
        out_shape=jax.ShapeDtypeStruct((n, D_MODEL), F32),
        scratch_shapes=[pltpu.VMEM((tile, D_MODEL), F32)],
        compiler_params=_cparams("parallel", "arbitrary"),
        name="moe_final_norm",
    )(xn2, route_t, w_g, w_u, w_d, h, gf)


EXPERT_ROW_TILE = 1024
SC_CORES = 2
SC_SUBCORES = 16
SC_WORKERS = SC_CORES * SC_SUBCORES
SC_SCATTER_ROWS = 64
SC_GATHER_ROWS = 64


def _route_rank_kernel(r_ref, rank_ref, cnt_ref, base_ref):
    i = pl.program_id(0)
    tt = r_ref.shape[1]

    @pl.when(i == 0)
    def _():
        base_ref[...] = jnp.zeros_like(base_ref)

    ids = r_ref[0:2, :].astype(jnp.int32)
    e_iota = lax.broadcasted_iota(jnp.int32, (N_EXPERTS, tt), 0)
    oh1 = jnp.where(e_iota == ids[0:1], 1.0, 0.0)
    oh2 = jnp.where(e_iota == ids[1:2], 1.0, 0.0)
    before = (lax.broadcasted_iota(jnp.int32, (tt, tt), 0) < lax.broadcasted_iota(jnp.int32, (tt, tt), 1))
    tri = jnp.where(before, 1.0, 0.0).astype(BF16)
    c1 = _dot(oh1.astype(BF16), tri)
    c2 = _dot(oh2.astype(BF16), tri)
    tot1 = jnp.sum(oh1, axis=1, keepdims=True)
    tot2 = jnp.sum(oh2, axis=1, keepdims=True)
    base = base_ref[:, 0:1]
    rank1 = jnp.sum(oh1 * (base + c1), axis=0, keepdims=True)
    rank2 = jnp.sum(oh2 * (base + tot1 + c2), axis=0, keepdims=True)
    rank_ref[...] = jnp.concatenate([rank1, rank2, jnp.zeros((ROUTE_ROWS - 2, tt), F32)], axis=0).astype(jnp.int32)
    new_base = jnp.broadcast_to(base + tot1 + tot2, base_ref.shape)
    base_ref[...] = new_base
    cnt_ref[...] = new_base.astype(jnp.int32)


def _route_rank(route, tile):
    n = route.shape[1]
    tile = min(tile, n)
    return pl.pallas_call(
        _route_rank_kernel,
        grid=(n // tile,),
        in_specs=[pl.BlockSpec((ROUTE_ROWS, tile), lambda i: (0, i))],
        out_specs=[pl.BlockSpec((ROUTE_ROWS, tile), lambda i: (0, i)),
                   pl.BlockSpec((N_EXPERTS, LANES), lambda i: (0, 0))],
        out_shape=[jax.ShapeDtypeStruct((ROUTE_ROWS, n), jnp.int32),
                   jax.ShapeDtypeStruct((N_EXPERTS, LANES), jnp.int32)],
        scratch_shapes=[pltpu.VMEM((N_EXPERTS, LANES), F32)],
        compiler_params=_cparams("arbitrary"),
        name="route_rank",
    )(route)


def _sc_mesh():
    return plsc.VectorSubcoreMesh(core_axis_name="core", subcore_axis_name="subcore")


def _sc_scatter_pairs(x, pos, rows_out):
    n, d = x.shape
    per_w = n // SC_WORKERS
    window = min(SC_SCATTER_ROWS, per_w)

    @pl.kernel(out_type=jax.ShapeDtypeStruct((rows_out, d), x.dtype), mesh=_sc_mesh(),
               scratch_types=[pltpu.VMEM((window,), jnp.int32), pltpu.VMEM((window,), jnp.int32),
                              pltpu.VMEM((window, d), x.dtype), pltpu.SemaphoreType.DMA, pltpu.SemaphoreType.DMA,
                              pltpu.SemaphoreType.DMA])
    def scatter(x_hbm, p_hbm, o_hbm, i1_v, i2_v, rows_v, sem_a, sem_b, sem_c):
        wid = lax.axis_index("subcore") * SC_CORES + lax.axis_index("core")

        @pl.loop(0, per_w // window)
        def _(j):
            base = wid * per_w + j * window
            load_i1 = pltpu.async_copy(p_hbm.at[pl.ds(base, window)], i1_v, sem_a)
            load_i2 = pltpu.async_copy(p_hbm.at[pl.ds(n + base, window)], i2_v, sem_b)
            load_x = pltpu.async_copy(x_hbm.at[pl.ds(base, window)], rows_v, sem_c)
            load_i1.wait()
            load_i2.wait()
            load_x.wait()
            put_1 = pltpu.async_copy(rows_v, o_hbm.at[i1_v], sem_a)
            put_2 = pltpu.async_copy(rows_v, o_hbm.at[i2_v], sem_b)
            put_1.wait()
            put_2.wait()

    return scatter(x, pos)


def _sc_gather_rows(table, idx):
    m = idx.shape[0]
    d = table.shape[1]
    per_w = m // SC_WORKERS
    window = min(SC_GATHER_ROWS, per_w)

    assert per_w % (2 * window) == 0

    @pl.kernel(out_type=jax.ShapeDtypeStruct((m, d), table.dtype), mesh=_sc_mesh(),
               scratch_types=[pltpu.VMEM((window,), jnp.int32), pltpu.VMEM((window,), jnp.int32),
                              pltpu.VMEM((window, d), table.dtype), pltpu.VMEM((window, d), table.dtype),
                              pltpu.SemaphoreType.DMA, pltpu.SemaphoreType.DMA])
    def gather(t_hbm, i_hbm, o_hbm, ia_v, ib_v, ra_v, rb_v, sem_a, sem_b):
        wid = lax.axis_index("subcore") * SC_CORES + lax.axis_index("core")

        @pl.loop(0, per_w // (2 * window))
        def _(j):
            base_a = wid * per_w + j * (2 * window)
            base_b = base_a + window
            idx_a = pltpu.async_copy(i_hbm.at[pl.ds(base_a, window)], ia_v, sem_a)
            idx_b = pltpu.async_copy(i_hbm.at[pl.ds(base_b, window)], ib_v, sem_b)
            idx_a.wait()
            get_a = pltpu.async_copy(t_hbm.at[ia_v], ra_v, sem_a)
            idx_b.wait()
            get_b = pltpu.async_copy(t_hbm.at[ib_v], rb_v, sem_b)
            get_a.wait()
            put_a = pltpu.async_copy(ra_v, o_hbm.at[pl.ds(base_a, window)], sem_a)
            get_b.wait()
            put_b = pltpu.async_copy(rb_v, o_hbm.at[pl.ds(base_b, window)], sem_b)
            put_a.wait()
            put_b.wait()

    return gather(table, idx)


def _expert_tiles_kernel(te_ref, nu_ref, x_ref, wg_ref, wu_ref, wd_ref, o_ref, wg_s, wu_s, wd_s):
    i = pl.program_id(0)

    @pl.when(i < nu_ref[0])
    def _():
        @pl.when(jnp.logical_or(i == 0, te_ref[i] != te_ref[jnp.maximum(i - 1, 0)]))
        def _():
            wg_s[...] = wg_ref[...].astype(BF16)
            wu_s[...] = wu_ref[...].astype(BF16)
            wd_s[...] = wd_ref[...].astype(BF16)

        o_ref[...] = _pack_halves(_expert_mlp(x_ref[...], wg_s, wu_s, wd_s).astype(BF16))


def _expert_tiles(tile_expert, n_used, xs, w_g, w_u, w_d):
    rows = xs.shape[0]
    tm = EXPERT_ROW_TILE
    grid_spec = pltpu.PrefetchScalarGridSpec(
        num_scalar_prefetch=2,
        grid=(rows // tm,),
        in_specs=[pl.BlockSpec((tm, HALF), lambda i, te, nu: (i, 0)),
                  pl.BlockSpec((None, D_MODEL, D_EXPERT), lambda i, te, nu: (te[i], 0, 0)),
                  pl.BlockSpec((None, D_MODEL, D_EXPERT), lambda i, te, nu: (te[i], 0, 0)),
                  pl.BlockSpec((None, D_EXPERT, D_MODEL), lambda i, te, nu: (te[i], 0, 0))],
        out_specs=pl.BlockSpec((tm, HALF), lambda i, te, nu: (i, 0)),
        scratch_shapes=[pltpu.VMEM((D_MODEL, D_EXPERT), BF16), pltpu.VMEM((D_MODEL, D_EXPERT), BF16),
                        pltpu.VMEM((D_EXPERT, D_MODEL), BF16)],
    )
    return pl.pallas_call(
        _expert_tiles_kernel,
        grid_spec=grid_spec,
        out_shape=jax.ShapeDtypeStruct((rows, HALF), jnp.int32),
        compiler_params=_cparams("arbitrary"),
        name="expert_tiles",
    )(tile_expert, n_used, xs, w_g, w_u, w_d)


def _unpack_f32(p):
    return pltpu.bitcast(p & jnp.int32(-65536), F32), pltpu.bitcast(lax.shift_left(p, jnp.int32(16)), F32)


def _combine_kernel(h_ref, o1_ref, o2_ref, rt_ref, gf_ref, y_ref):
    w1, w2 = rt_ref[:, 2:3], rt_ref[:, 3:4]
    a_lo, a_hi = _unpack_f32(o1_ref[...])
    b_lo, b_hi = _unpack_f32(o2_ref[...])
    y_lo = h_ref[:, 0:HALF] + (w1 * a_lo + w2 * b_lo)
    y_hi = h_ref[:, HALF:D_MODEL] + (w1 * a_hi + w2 * b_hi)
    ms = (jnp.sum(y_lo * y_lo, axis=-1, keepdims=True) + jnp.sum(y_hi * y_hi, axis=-1, keepdims=True)) / D_MODEL
    inv = lax.rsqrt(ms + EPS)
    y_ref[:, 0:HALF] = (y_lo * inv) * gf_ref[:, 0:HALF]
    y_ref[:, HALF:D_MODEL] = (y_hi * inv) * gf_ref[:, HALF:D_MODEL]


def _combine(h, o12, route_t, gf, tile):
    n = h.shape[0]
    tile = min(tile, n)
    nt = n // tile
    return pl.pallas_call(
        _combine_kernel,
        grid=(nt,),
        in_specs=[pl.BlockSpec((tile, D_MODEL), lambda i: (i, 0)),
                  pl.BlockSpec((tile, HALF), lambda i: (i, 0)),
                  pl.BlockSpec((tile, HALF), lambda i: (i + nt, 0)),
                  pl.BlockSpec((tile, ROUTE_ROWS), lambda i: (i, 0)),
                  pl.BlockSpec((1, D_MODEL), lambda i: (0, 0))],
        out_specs=pl.BlockSpec((tile, D_MODEL), lambda i: (i, 0)),
        out_shape=jax.ShapeDtypeStruct((n, D_MODEL), F32),
        compiler_params=_cparams("parallel"),
        name="combine_final_norm",
    )(h, o12, o12, route_t, gf)


def _sparse_moe(xn2p, route, h, w_g, w_u, w_d, gf, run_before_experts):
    n = h.shape[0]
    tm = EXPERT_ROW_TILE
    rows = 2 * n + N_EXPERTS * tm
    rank, cnt = _route_rank(route, 1024)
    counts = cnt[:, 0]
    padded = (counts + tm - 1) // tm * tm
    e_idx = jnp.arange(N_EXPERTS, dtype=jnp.int32)
    starts = jnp.sum(jnp.where(e_idx[None, :] < e_idx[:, None], padded[None, :], 0), axis=1)
    ends = starts + padded
    ids = route[0:2].astype(jnp.int32)
    start_of = jnp.sum(jnp.where(ids[None] == e_idx[:, None, None], starts[:, None, None], 0), axis=0)
    pos = (start_of + rank[0:2]).reshape(2 * n)
    tile_start = jnp.arange(rows // tm, dtype=jnp.int32) * tm
    tile_expert = jnp.minimum(jnp.sum((tile_start[:, None] >= ends[None, :]).astype(jnp.int32), axis=1),
                              N_EXPERTS - 1)
    n_used = (ends[-1:] // tm).astype(jnp.int32)
    xs = _sc_scatter_pairs(xn2p, pos, rows)
    xs, _ = lax.optimization_barrier((xs, run_before_experts))
    os_ = _expert_tiles(tile_expert, n_used, xs, w_g, w_u, w_d)
    o12 = _sc_gather_rows(os_, pos)
    return _combine(h, o12, route.T, gf, 512)


def _prep_in_weights(w_in):
    o = 0
    w_u = w_in[:, o:o + SSM_WIDTH]; o += SSM_WIDTH
    w_q = w_in[:, o:o + SWA_WIDTH]; o += SWA_WIDTH
    w_k = w_in[:, o:o + SWA_KV_WIDTH]; o += SWA_KV_WIDTH
    w_v = w_in[:, o:o + SWA_KV_WIDTH]; o += SWA_KV_WIDTH
    w_qm = w_in[:, o:o + MEM_WIDTH]; o += MEM_WIDTH
    w_g = w_in[:, o:]
    wq = (w_q * (SWA_HEAD_DIM ** -0.5)).reshape(D_MODEL, SWA_KV_HEADS, SWA_REP, SWA_HEAD_DIM)
    wq = wq.transpose(0, 2, 1, 3).reshape(D_MODEL, SWA_WIDTH)
    w_main = jnp.concatenate([w_u, wq, w_k, w_v, w_qm], axis=1).astype(BF16)
    return w_main, w_g.astype(BF16)


IN_SPLITS = (SSM_WIDTH, SWA_WIDTH, SWA_KV_WIDTH, SWA_KV_WIDTH, MEM_WIDTH)
IN_DTYPES = ((F32, BF16), (BF16,), (F32,), (F32,), (BF16,))


def kernel(x_prompt, x_sample, cache_swa_k, cache_swa_v, state_ssm_re, state_ssm_im, cache_mem_k, cache_mem_v, mem_prompt, norm1_g, w_in, lam_re, lam_im, log_dt, bm_re, bm_im, cm_re, cm_im, d_skip, w_glu, b_glu, sinks, rel_table, mem_norm_g, w_mem_kv, w_br_ssm, w_br_swa, w_br_mem, w_out, norm2_g, w_rg, b_rg, w_rexp, b_rexp, w_e_gate, w_e_up, w_e_down, final_norm_g):
    nb, t, _ = x_prompt.shape
    ns, ts, _ = x_sample.shape
    l = 0
    L = S5_CHUNK

    w_main, w_gates = _prep_in_weights(w_in[l])
    w_swa = (w_br_swa[l].reshape(SWA_KV_HEADS, SWA_REP, SWA_HEAD_DIM, D_MODEL).transpose(1, 0, 2, 3)
             .reshape(SWA_WIDTH, D_MODEL))
    pad_rows = ROUTER_ROWS - N_EXPERTS - N_EXPERT_GROUPS
    w_router = jnp.concatenate([w_rexp[l].T, w_rg[l].T, jnp.zeros((pad_rows, D_MODEL), F32)], axis=0).astype(BF16)
    b_router = jnp.concatenate([b_rexp[l], b_rg[l], jnp.zeros((pad_rows,), F32)]).reshape(ROUTER_ROWS, 1)
    mp = {
        'g1': norm1_g[l].reshape(1, D_MODEL), 'w_gates': w_gates, 'd_skip': d_skip[l].reshape(1, SSM_WIDTH),
        'w_glu': w_glu[l].astype(BF16), 'b_glu': b_glu[l].reshape(1, SSM_WIDTH),
        'w_br_ssm': w_br_ssm[l].astype(BF16), 'w_br_swa': w_swa.astype(BF16),
        'w_br_mem': w_br_mem[l].astype(BF16), 'w_out': w_out[l].astype(BF16),
        'g2': norm2_g[l].reshape(1, D_MODEL), 'w_router': w_router, 'b_router': b_router,
    }
    w_g, w_u, w_d = w_e_gate[l], w_e_up[l], w_e_down[l]
    gf = final_norm_g.reshape(1, D_MODEL)
    s5_w = _s5_weights(lam_re[l], lam_im[l], log_dt[l], bm_re[l], bm_im[l], cm_re[l], cm_im[l], L)

    bias_p = _rel_bias(rel_table, np.arange(WINDOW)[:, None] + WINDOW - np.arange(2 * WINDOW)[None, :])
    keys_s = WINDOW + 2 * ts
    bias_s = _rel_bias(rel_table, np.arange(ts)[:, None] + WINDOW - np.arange(keys_s)[None, :])
    bias_s = bias_s.reshape(SWA_HEADS * ts, keys_s)
    sink_rows = jnp.repeat(sinks[l].astype(F32), ts).reshape(SWA_HEADS * ts, 1)

    n = nb * t
    xp = x_prompt.reshape(n, D_MODEL)
    mk, mv = _norm_proj(mem_prompt.reshape(nb * MEM_TOKENS, D_MODEL), mem_norm_g[l].reshape(1, D_MODEL),
                        w_mem_kv[l].astype(BF16), (MEM_WIDTH, MEM_WIDTH), ((F32,), (F32,)), 512)
    u, ub, qz, k, v, qm = _norm_proj(xp, mp['g1'], w_main, IN_SPLITS, IN_DTYPES, 512)

    y_ssm, fin = _s5(ub, jnp.zeros((nb, N_CH_TILES * 2 * STATE_TILE), F32), s5_w, nb, t // L, L, 64)
    p_re, p_im = _tiles_to_state(fin)

    o_swa = _swa_prompt(qz, k, v, bias_p, sinks[l].astype(F32), nb, t, 4)
    o_mem = _mem_prompt(qm, mk, mv, nb, t, 512)
    h, xn2p, route = _merge(xp, u, y_ssm, o_swa, o_mem, mp, 512)

    k4 = k.reshape(nb, t, SWA_KV_HEADS, SWA_HEAD_DIM)
    v4 = v.reshape(nb, t, SWA_KV_HEADS, SWA_HEAD_DIM)
    new_k_p, new_v_p = k4[:, -WINDOW:][None], v4[:, -WINDOW:][None]
    new_mk = mk.reshape(1, nb, MEM_TOKENS, MEM_HEADS, MEM_HEAD_DIM)
    new_mv = mv.reshape(1, nb, MEM_TOKENS, MEM_HEADS, MEM_HEAD_DIM)

    m = ns * ts
    xs = x_sample.reshape(m, D_MODEL)
    us, ubs, qzs, k_s, v_s, qms = _norm_proj(xs, mp['g1'], w_main, IN_SPLITS, IN_DTYPES, 256)
    ys_ssm, fins = _s5(ubs, _state_to_tiles(state_ssm_re[l], state_ssm_im[l]), s5_w, ns, ts // L, L, 64)
    s_re, s_im = _tiles_to_state(fins)

    kk_all = jnp.concatenate([cache_swa_k[l].reshape(ns, WINDOW, SWA_KV_WIDTH).astype(F32),
                              k_s.reshape(ns, ts, SWA_KV_WIDTH)], axis=1)
    vv_all = jnp.concatenate([cache_swa_v[l].reshape(ns, WINDOW, SWA_KV_WIDTH).astype(F32),
                              v_s.reshape(ns, ts, SWA_KV_WIDTH)], axis=1)
    pad = jnp.zeros((ns, keys_s - WINDOW - ts, SWA_KV_WIDTH), F32)
    q5 = qzs.reshape(ns, ts, SWA_REP, SWA_KV_HEADS, SWA_HEAD_DIM)
    zq = jnp.zeros((ns, ts, SWA_REP, SWA_HEAD_DIM), BF16)
    q_rows = jnp.concatenate([jnp.concatenate([q5[:, :, :, 0], zq], axis=-1),
                              jnp.concatenate([zq, q5[:, :, :, 1]], axis=-1)], axis=2)
    q_rows = q_rows.transpose(0, 2, 1, 3).reshape(ns, SWA_HEADS * ts, LANES)
    o_dec = _swa_decode(q_rows, jnp.concatenate([kk_all, pad], axis=1), jnp.concatenate([vv_all, pad], axis=1),
                        bias_s, sink_rows, ts, 8)
    o_dec = o_dec.reshape(ns, SWA_KV_HEADS, SWA_REP, ts, SWA_KV_HEADS, SWA_HEAD_DIM)
    o_dec = jnp.stack([o_dec[:, g, :, :, g] for g in range(SWA_KV_HEADS)], axis=1)
    o_swa_s = o_dec.transpose(0, 3, 2, 1, 4).reshape(m, SWA_WIDTH).astype(BF16)

    o_mem_s = _mem_decode(qms.astype(F32).reshape(ns, ts, MEM_WIDTH), cache_mem_k, cache_mem_v, l, 8)
    o_mem_s = o_mem_s.reshape(m, MEM_WIDTH).astype(BF16)

    y_prompt = _sparse_moe(xn2p, route, h, w_g, w_u, w_d, gf, (ys_ssm, o_swa_s, o_mem_s)).reshape(nb, t, D_MODEL)
    hs_, xn2ps, routes = _merge(xs, us, ys_ssm, o_swa_s, o_mem_s, mp, 256)
    y_sample = _moe(xn2ps, routes.T, w_g, w_u, w_d, hs_, gf, 1024).reshape(ns, ts, D_MODEL)

    roll = lambda cache, new: jnp.concatenate(
        [cache[:, :, ts:], new.reshape(1, ns, ts, SWA_KV_HEADS, SWA_HEAD_DIM).astype(cache.dtype)], axis=2)
    new_k_s, new_v_s = roll(cache_swa_k, k_s), roll(cache_swa_v, v_s)

    return (y_prompt, y_sample,
            new_k_p, new_v_p, p_re[None], p_im[None], new_mk, new_mv,
            new_k_s, new_v_s, s_re[None].astype(state_ssm_re.dtype), s_im[None].astype(state_ssm_im.dtype))
```

```python
import functools
import math

import numpy as np
import jax
import jax.numpy as jnp
from jax import lax
from jax.experimental import pallas as pl
from jax.experimental.pallas import tpu as pltpu
from jax.experimental.pallas import tpu_sc as plsc

F32 = jnp.float32
BF16 = jnp.bfloat16

D_MODEL = 1024
SSM_WIDTH = 512
SSM_GROUP = 16
SSM_GROUPS = 32
SSM_STATE = 64
SWA_HEADS = 8
SWA_KV_HEADS = 2
SWA_REP = 4
SWA_HEAD_DIM = 64
SWA_WIDTH = 512
SWA_KV_WIDTH = 128
WINDOW = 128
REL_BUCKETS = 32
REL_MAX_DIST = 128
MEM_TOKENS = 256
MEM_HEADS = 4
MEM_HEAD_DIM = 128
MEM_WIDTH = 512
N_EXPERT_GROUPS = 4
EXPERTS_PER_GROUP = 8
N_EXPERTS = 32
D_EXPERT = 256
EPS = 1e-6
NEG_INF = -1e30

LANES = 128
GROUPS_PER_TILE = LANES // SSM_GROUP
N_CH_TILES = SSM_WIDTH // LANES
STATE_TILE = GROUPS_PER_TILE * SSM_STATE
VMEM_LIMIT = 56 * 1024 * 1024
S5_CHUNK = 8
S5_PANEL = 256

_TRANS_B = (((1,), (1,)), ((), ()))


def _cparams(*sem):
    return pltpu.CompilerParams(dimension_semantics=sem, vmem_limit_bytes=VMEM_LIMIT)


def _rms(x, g):
    return (x * lax.rsqrt(jnp.mean(x * x, axis=-1, keepdims=True) + EPS)) * g


def _dot(a, b):
    return jnp.dot(a, b, preferred_element_type=F32)


def _norm_proj_kernel(x_ref, g_ref, w_ref, *out_refs, splits, dtypes):
    xb = _rms(x_ref[...], g_ref[...]).astype(BF16)
    off = 0
    outs = iter(out_refs)
    for width, dts in zip(splits, dtypes):
        r = _dot(xb, w_ref[:, off:off + width])
        for dt in dts:
            next(outs)[...] = r.astype(dt)
        off += width


def _norm_proj(x, g, w, splits, dtypes, tile):
    n, d = x.shape
    tile = min(tile, n)
    flat = [(wd, dt) for wd, dts in zip(splits, dtypes) for dt in dts]
    return pl.pallas_call(
        functools.partial(_norm_proj_kernel, splits=tuple(splits), dtypes=tuple(dtypes)),
        grid=(n // tile,),
        in_specs=[pl.BlockSpec((tile, d), lambda i: (i, 0)),
                  pl.BlockSpec((1, d), lambda i: (0, 0)),
                  pl.BlockSpec((d, sum(splits)), lambda i: (0, 0))],
        out_specs=[pl.BlockSpec((tile, wd), lambda i: (i, 0)) for wd, _ in flat],
        out_shape=[jax.ShapeDtypeStruct((n, wd), dt) for wd, dt in flat],
        compiler_params=_cparams("parallel"),
        name="norm_proj",
    )(x, g, w)


def _s5_weights(lam_re, lam_im, log_dt, bm_re, bm_im, cm_re, cm_im, L):
    hp = lax.Precision.HIGHEST
    nt, gt, P, H = N_CH_TILES, GROUPS_PER_TILE, SSM_STATE, SSM_GROUP
    lr, li = lam_re.astype(F32), lam_im.astype(F32)
    dt = jnp.exp(log_dt.astype(F32))[:, None]
    mag = jnp.exp(lr * dt)
    a_re = mag * jnp.cos(li * dt)
    a_im = mag * jnp.sin(li * dt)
    den = lr * lr + li * li
    f_re = ((a_re - 1.0) * lr + a_im * li) / den
    f_im = (a_im * lr - (a_re - 1.0) * li) / den
    br, bi = bm_re.astype(F32), bm_im.astype(F32)
    bb_re = f_re[..., None] * br - f_im[..., None] * bi
    bb_im = f_re[..., None] * bi + f_im[..., None] * br
    pr, pi = [jnp.ones_like(a_re)], [jnp.zeros_like(a_im)]
    for _ in range(L):
        pr.append(pr[-1] * a_re - pi[-1] * a_im)
        pi.append(pr[-2] * a_im + pi[-1] * a_re)
    ap_re, ap_im = jnp.stack(pr), jnp.stack(pi)
    cr, ci = cm_re.astype(F32), cm_im.astype(F32)
    ca_re = cr[None] * ap_re[:, :, None, :] - ci[None] * ap_im[:, :, None, :]
    ca_im = cr[None] * ap_im[:, :, None, :] + ci[None] * ap_re[:, :, None, :]

    rev_re = jnp.stack([pr[L - 1 - s] for s in range(L)])
    rev_im = jnp.stack([pi[L - 1 - s] for s in range(L)])
    ws_re = rev_re[..., None] * bb_re[None] - rev_im[..., None] * bb_im[None]
    ws_im = rev_re[..., None] * bb_im[None] + rev_im[..., None] * bb_re[None]
    c_st = jnp.concatenate([ws_re.transpose(0, 1, 3, 2).reshape(L, nt, gt * H, P),
                            ws_im.transpose(0, 1, 3, 2).reshape(L, nt, gt * H, P)], axis=3).transpose(1, 0, 2, 3)
    so = lambda ca: ca[1:].transpose(1, 3, 0, 2).reshape(nt, gt * P, L * H)
    c_so = jnp.concatenate([so(ca_re), so(-ca_im)], axis=1)
    k_lag = (jnp.einsum('tghp,gpk->gkth', ca_re[:L], bb_re, precision=hp)
             - jnp.einsum('tghp,gpk->gkth', ca_im[:L], bb_im, precision=hp))
    c_k = k_lag.reshape(nt, gt * H, L * H)
    w_st, w_out, toep = _s5_expand(c_st, c_so, c_k, L)

    def per_tile(v):
        return v.reshape(nt, 1, STATE_TILE)

    return w_st, w_out, toep, per_tile(pr[L]), per_tile(pi[L])


def _s5_expand_kernel(cst_ref, cso_ref, ck_ref, wst_ref, wso_ref, toep_ref, *, L):
    hp = lax.Precision.HIGHEST
    P, H = SSM_STATE, SSM_GROUP
    iota = lambda shape, d: lax.broadcasted_iota(jnp.int32, shape, d)
    one = lambda cond: jnp.where(cond, 1.0, 0.0).astype(F32)

    r, c = iota((2 * P, 2 * STATE_TILE), 0), iota((2 * P, 2 * STATE_TILE), 1)
    rep_st = one((r // P == c // STATE_TILE) & (r % P == c % P))
    r, c = iota((LANES, 2 * STATE_TILE), 0), iota((LANES, 2 * STATE_TILE), 1)
    own_st = one(r // H == (c % STATE_TILE) // P)
    for s in range(L):
        blk = jnp.dot(cst_ref[s], rep_st, precision=hp, preferred_element_type=F32) * own_st
        wst_ref[s * LANES:(s + 1) * LANES, :] = blk.astype(BF16)

    r, c = iota((LANES, LANES), 0), iota((LANES, LANES), 1)
    pick = [one((r // H == t) & (r % H == c % H)) for t in range(L)]
    own_k = one(r // H == c // H)
    r, c = iota((2 * STATE_TILE, LANES), 0), iota((2 * STATE_TILE, LANES), 1)
    own_so = one((r % STATE_TILE) // P == c // H)
    cso = cso_ref[...]
    for t in range(L):
        blk = jnp.dot(cso, pick[t], precision=hp, preferred_element_type=F32) * own_so
        wso_ref[:, t * LANES:(t + 1) * LANES] = blk.astype(BF16)
    ck = ck_ref[...]
    lag = [(jnp.dot(ck, pick[t], precision=hp, preferred_element_type=F32) * own_k).astype(BF16) for t in range(L)]
    zero = jnp.zeros((LANES, LANES), BF16)
    for s in range(L):
        for t in range(L):
            toep_ref[s * LANES:(s + 1) * LANES, t * LANES:(t + 1) * LANES] = lag[t - s] if t >= s else zero


def _s5_expand(c_st, c_so, c_k, L):
    lk = L * LANES
    st2 = 2 * STATE_TILE
    return pl.pallas_call(
        functools.partial(_s5_expand_kernel, L=L),
        grid=(N_CH_TILES,),
        in_specs=[pl.BlockSpec((None, L, LANES, 2 * SSM_STATE), lambda j: (j, 0, 0, 0)),
                  pl.BlockSpec((None, st2, L * SSM_GROUP), lambda j: (j, 0, 0)),
                  pl.BlockSpec((None, LANES, L * SSM_GROUP), lambda j: (j, 0, 0))],
        out_specs=[pl.BlockSpec((None, lk, st2), lambda j: (j, 0, 0)),
                   pl.BlockSpec((None, st2, lk), lambda j: (j, 0, 0)),
                   pl.BlockSpec((None, lk, lk), lambda j: (j, 0, 0))],
        out_shape=[jax.ShapeDtypeStruct((N_CH_TILES, lk, st2), BF16),
                   jax.ShapeDtypeStruct((N_CH_TILES, st2, lk), BF16),
                   jax.ShapeDtypeStruct((N_CH_TILES, lk, lk), BF16)],
        compiler_params=_cparams("parallel"),
        name="s5_expand_weights",
    )(c_st, c_so, c_k)


def _to_chunks(u, nb, nc, L):
    return (u.reshape(nb, nc, L, N_CH_TILES, LANES).transpose(1, 0, 3, 2, 4)
            .reshape(nc * nb, N_CH_TILES * L * LANES))


def _from_chunks(y, nb, nc, L):
    return (y.reshape(nc, nb, N_CH_TILES, L, LANES).transpose(1, 0, 3, 2, 4)
            .reshape(nb * nc * L, SSM_WIDTH))


def _s5_kernel(x_ref, h0_ref, are_ref, aim_ref, ws_ref, t_ref, wo_ref, y_ref, fin_ref,
               hr_ref, hi_ref, d_ref, hs_ref, *, cb, nb):
    ci = pl.program_id(1)

    @pl.when(ci == 0)
    def _():
        hr_ref[...] = h0_ref[:, 0:STATE_TILE]
        hi_ref[...] = h0_ref[:, STATE_TILE:2 * STATE_TILE]

    x = x_ref[...]
    d_ref[...] = _dot(x, ws_ref[...])
    ar = jnp.broadcast_to(are_ref[...], (nb, STATE_TILE))
    ai = jnp.broadcast_to(aim_ref[...], (nb, STATE_TILE))

    def body(c, carry):
        hr, hi = carry
        r0 = pl.multiple_of(c * nb, nb)
        hs_ref[pl.ds(r0, nb), 0:STATE_TILE] = hr
        hs_ref[pl.ds(r0, nb), STATE_TILE:2 * STATE_TILE] = hi
        d = d_ref[pl.ds(r0, nb), :]
        return (ar * hr - ai * hi + d[:, 0:STATE_TILE],
                ar * hi + ai * hr + d[:, STATE_TILE:2 * STATE_TILE])

    hr, hi = lax.fori_loop(0, cb, body, (hr_ref[...], hi_ref[...]))
    hr_ref[...] = hr
    hi_ref[...] = hi
    hsb = hs_ref[...].astype(BF16)
    for c0 in range(0, t_ref.shape[1], S5_PANEL):
        c1 = c0 + S5_PANEL
        y_ref[:, c0:c1] = _dot(x[:, 0:c1], t_ref[0:c1, c0:c1]) + _dot(hsb, wo_ref[:, c0:c1])

    @pl.when(ci == pl.num_programs(1) - 1)
    def _():
        fin_ref[:, 0:STATE_TILE] = hr
        fin_ref[:, STATE_TILE:2 * STATE_TILE] = hi


def _s5(ub, h0, weights, nb, nc, L, chunk_block):
    w_st, w_so, toep, a_re, a_im = weights
    xc = _to_chunks(ub, nb, nc, L)
    cb = min(chunk_block, nc)
    rows = cb * nb
    lk = L * LANES
    st2 = 2 * STATE_TILE
    tile_w = lambda shape: pl.BlockSpec((None,) + shape, lambda j, c: (j, 0, 0))
    y, fin = pl.pallas_call(
        functools.partial(_s5_kernel, cb=cb, nb=nb),
        grid=(N_CH_TILES, nc // cb),
        in_specs=[pl.BlockSpec((rows, lk), lambda j, c: (c, j)),
                  pl.BlockSpec((nb, st2), lambda j, c: (0, j)),
                  tile_w((1, STATE_TILE)), tile_w((1, STATE_TILE)),
                  tile_w((lk, st2)), tile_w((lk, lk)), tile_w((st2, lk))],
        out_specs=[pl.BlockSpec((rows, lk), lambda j, c: (c, j)),
                   pl.BlockSpec((nb, st2), lambda j, c: (0, j))],
        out_shape=[jax.ShapeDtypeStruct((nc * nb, N_CH_TILES * lk), F32),
                   jax.ShapeDtypeStruct((nb, N_CH_TILES * st2), F32)],
        scratch_shapes=[pltpu.VMEM((nb, STATE_TILE), F32), pltpu.VMEM((nb, STATE_TILE), F32),
                        pltpu.VMEM((rows, st2), F32), pltpu.VMEM((rows, st2), F32)],
        compiler_params=_cparams("parallel", "arbitrary"),
        name="s5_chunked_scan",
    )(xc, h0, a_re, a_im, w_st, toep, w_so)
    return _from_chunks(y, nb, nc, L), fin


def _state_to_tiles(h_re, h_im):
    nb = h_re.shape[0]
    r = h_re.astype(F32).reshape(nb, N_CH_TILES, STATE_TILE)
    i = h_im.astype(F32).reshape(nb, N_CH_TILES, STATE_TILE)
    return jnp.concatenate([r, i], axis=-1).reshape(nb, N_CH_TILES * 2 * STATE_TILE)


def _tiles_to_state(h):
    nb = h.shape[0]
    h = h.reshape(nb, N_CH_TILES, 2, GROUPS_PER_TILE, SSM_STATE)
    return (h[:, :, 0].reshape(nb, SSM_GROUPS, SSM_STATE), h[:, :, 1].reshape(nb, SSM_GROUPS, SSM_STATE))


def _t5_bucket(dist):
    n = np.maximum(dist, 0)
    max_exact = REL_BUCKETS // 2
    nf = np.maximum(n, 1).astype(np.float32)
    large = max_exact + (np.log(nf / np.float32(max_exact)) / np.float32(math.log(REL_MAX_DIST / max_exact))
                         * np.float32(REL_BUCKETS - max_exact)).astype(np.int32)
    large = np.minimum(large, REL_BUCKETS - 1)
    return np.where(n < max_exact, n, large)


def _rel_bias(rel_table, dist):
    bucket = _t5_bucket(dist)
    tab = rel_table.astype(F32)
    out = jnp.zeros((SWA_HEADS,) + dist.shape, F32)
    for b in range(REL_BUCKETS):
        sel = jnp.asarray(bucket == b)
        if bool((bucket == b).any()):
            out = jnp.where(sel[None], tab[b].reshape((SWA_HEADS,) + (1,) * dist.ndim), out)
    return out


def _swa_prompt_kernel(sink_ref, q_ref, kp_ref, kc_ref, vp_ref, vc_ref, bias_ref, o_ref, kk_ref, vv_ref, *, qblocks):
    step = pl.program_id(1)
    kk_ref[0:WINDOW, :] = kp_ref[...].astype(BF16)
    kk_ref[WINDOW:, :] = kc_ref[...].astype(BF16)
    vv_ref[0:WINDOW, :] = vp_ref[...].astype(BF16)
    vv_ref[WINDOW:, :] = vc_ref[...].astype(BF16)
    row = lax.broadcasted_iota(jnp.int32, (WINDOW, 2 * WINDOW), 0)
    col = lax.broadcasted_iota(jnp.int32, (WINDOW, 2 * WINDOW), 1)
    dist = row + WINDOW - col
    band = (dist >= 0) & (dist < WINDOW)
    lane = lax.broadcasted_iota(jnp.int32, (WINDOW, LANES), 1)
    low = lane < SWA_HEAD_DIM

    def block(j, carry):
        r0 = pl.multiple_of(j * WINDOW, WINDOW)
        kk = kk_ref[pl.ds(r0, 2 * WINDOW), :]
        vv = vv_ref[pl.ds(r0, 2 * WINDOW), :]
        valid = band & ((col >= WINDOW) | (step * qblocks + j > 0))
        for t in range(SWA_REP):
            q2 = q_ref[pl.ds(r0, WINDOW), t * LANES:(t + 1) * LANES]
            outs = []
            for half in range(SWA_KV_HEADS):
                h = t + SWA_REP * half
                qh = jnp.where(low if half == 0 else jnp.logical_not(low), q2, jnp.zeros_like(q2))
                s = lax.dot_general(qh, kk, _TRANS_B, preferred_element_type=F32)
                s = jnp.where(valid, s + bias_ref[h], NEG_INF)
                sink = sink_ref[h]
                m = jnp.maximum(jnp.max(s, axis=-1, keepdims=True), sink)
                e = jnp.exp(s - m)
                den = jnp.sum(e, axis=-1, keepdims=True) + jnp.exp(sink - m)
                outs.append(_dot(e.astype(BF16), vv) * (1.0 / den))
            o_ref[pl.ds(r0, WINDOW), t * LANES:(t + 1) * LANES] = jnp.where(low, outs[0], outs[1]).astype(BF16)
        return carry

    lax.fori_loop(0, qblocks, block, 0)


def _swa_prompt(q, k, v, bias, sinks, nb, t, qblocks):
    nstep = t // (WINDOW * qblocks)
    rows = WINDOW * qblocks
    cur = lambda b, i: (b * nstep + i, 0)
    prev = lambda b, i: (b * nstep * qblocks + jnp.maximum(i * qblocks - 1, 0), 0)
    return pl.pallas_call(
        functools.partial(_swa_prompt_kernel, qblocks=qblocks),
        grid=(nb, nstep),
        in_specs=[pl.BlockSpec(memory_space=pltpu.SMEM),
                  pl.BlockSpec((rows, SWA_WIDTH), cur),
                  pl.BlockSpec((WINDOW, SWA_KV_WIDTH), prev),
                  pl.BlockSpec((rows, SWA_KV_WIDTH), cur),
                  pl.BlockSpec((WINDOW, SWA_KV_WIDTH), prev),
                  pl.BlockSpec((rows, SWA_KV_WIDTH), cur),
                  pl.BlockSpec((SWA_HEADS, WINDOW, 2 * WINDOW), lambda b, i: (0, 0, 0))],
        out_specs=pl.BlockSpec((rows, SWA_WIDTH), cur),
        out_shape=jax.ShapeDtypeStruct((nb * t, SWA_WIDTH), BF16),
        scratch_shapes=[pltpu.VMEM((rows + WINDOW, SWA_KV_WIDTH), BF16),
                        pltpu.VMEM((rows + WINDOW, SWA_KV_WIDTH), BF16)],
        compiler_params=_cparams("parallel", "parallel"),
        name="swa_prompt",
    )(sinks, q, k, k, v, v, bias)


def _swa_decode_kernel(q_ref, k_ref, v_ref, bias_ref, sink_ref, o_ref, *, seqs, tq):
    rows, keys = q_ref.shape[1], k_ref.shape[1]
    qi = lax.broadcasted_iota(jnp.int32, (rows, keys), 0) % tq
    col = lax.broadcasted_iota(jnp.int32, (rows, keys), 1)
    dist = qi + WINDOW - col
    valid = (dist >= 0) & (dist < WINDOW)
    bias = bias_ref[...]
    sink = sink_ref[...]
    for s_i in range(seqs):
        kk = k_ref[s_i].astype(BF16)
        s = lax.dot_general(q_ref[s_i], kk, _TRANS_B, preferred_element_type=F32)
        s = jnp.where(valid, s + bias, NEG_INF)
        m = jnp.maximum(jnp.max(s, axis=-1, keepdims=True), sink)
        e = jnp.exp(s - m)
        den = jnp.sum(e, axis=-1, keepdims=True) + jnp.exp(sink - m)
        o_ref[s_i] = _dot(e.astype(BF16), v_ref[s_i].astype(BF16)) * (1.0 / den)


def _swa_decode(qz, k_all, v_all, bias, sink_rows, tq, seqs):
    nseq, rows, _ = qz.shape
    keys = k_all.shape[1]
    seqs = min(seqs, nseq)
    return pl.pallas_call(
        functools.partial(_swa_decode_kernel, seqs=seqs, tq=tq),
        grid=(nseq // seqs,),
        in_specs=[pl.BlockSpec((seqs, rows, LANES), lambda i: (i, 0, 0)),
                  pl.BlockSpec((seqs, keys, LANES), lambda i: (i, 0, 0)),
                  pl.BlockSpec((seqs, keys, LANES), lambda i: (i, 0, 0)),
                  pl.BlockSpec((rows, keys), lambda i: (0, 0)),
                  pl.BlockSpec((rows, 1), lambda i: (0, 0))],
        out_specs=pl.BlockSpec((seqs, rows, LANES), lambda i: (i, 0, 0)),
        out_shape=jax.ShapeDtypeStruct((nseq, rows, LANES), F32),
        compiler_params=_cparams("parallel"),
        name="swa_decode",
    )(qz, k_all, v_all, bias, sink_rows)


def _softmax(s):
    m = jnp.max(s, axis=-1, keepdims=True)
    e = jnp.exp(s - m)
    return e * (1.0 / jnp.sum(e, axis=-1, keepdims=True))


def _mem_prompt_kernel(q_ref, k_ref, v_ref, o_ref, s_ref, p_ref):
    scale = MEM_HEAD_DIM ** -0.5
    heads = [slice(h * MEM_HEAD_DIM, (h + 1) * MEM_HEAD_DIM) for h in range(MEM_HEADS)]
    for h, sl in enumerate(heads):
        s_ref[h] = lax.dot_general(q_ref[:, sl], k_ref[:, sl].astype(BF16), _TRANS_B, preferred_element_type=F32)
    s = s_ref[...] * scale
    e = jnp.exp(s - jnp.max(s, axis=-1, keepdims=True))
    p_ref[...] = e.astype(BF16)
    inv = 1.0 / jnp.sum(e, axis=-1, keepdims=True)
    for h, sl in enumerate(heads):
        o_ref[:, sl] = (_dot(p_ref[h], v_ref[:, sl].astype(BF16)) * inv[h]).astype(BF16)


def _mem_prompt(qm, mk, mv, nb, t, tile):
    tile = min(tile, t)
    nt = t // tile
    return pl.pallas_call(
        _mem_prompt_kernel,
        grid=(nb, nt),
        in_specs=[pl.BlockSpec((tile, MEM_WIDTH), lambda b, i: (b * nt + i, 0)),
                  pl.BlockSpec((MEM_TOKENS, MEM_WIDTH), lambda b, i: (b, 0)),
                  pl.BlockSpec((MEM_TOKENS, MEM_WIDTH), lambda b, i: (b, 0))],
        out_specs=pl.BlockSpec((tile, MEM_WIDTH), lambda b, i: (b * nt + i, 0)),
        out_shape=jax.ShapeDtypeStruct((nb * t, MEM_WIDTH), BF16),
        scratch_shapes=[pltpu.VMEM((MEM_HEADS, tile, MEM_TOKENS), F32), pltpu.VMEM((MEM_HEADS, tile, MEM_TOKENS), BF16)],
        compiler_params=_cparams("parallel", "parallel"),
        name="mem_prompt",
    )(qm, mk, mv)


def _mem_decode_kernel(q_ref, k_ref, v_ref, o_ref, *, seqs):
    tq = q_ref.shape[1]
    rows, cols = MEM_HEADS * tq, MEM_TOKENS * MEM_HEADS
    k2 = k_ref.reshape(seqs, cols, MEM_HEAD_DIM)
    v2 = v_ref.reshape(seqs, cols, MEM_HEAD_DIM)
    scale = MEM_HEAD_DIM ** -0.5
    own = (lax.broadcasted_iota(jnp.int32, (rows, cols), 1) % MEM_HEADS
           == lax.broadcasted_iota(jnp.int32, (rows, cols), 0) // tq)
    for s_i in range(seqs):
        q = q_ref[s_i]
        qb = jnp.concatenate([q[:, h * MEM_HEAD_DIM:(h + 1) * MEM_HEAD_DIM] for h in range(MEM_HEADS)], axis=0)
        s = lax.dot_general(qb.astype(BF16), k2[s_i].astype(BF16), _TRANS_B, preferred_element_type=F32) * scale
        p = _softmax(jnp.where(own, s, NEG_INF)).astype(BF16)
        o = _dot(p, v2[s_i].astype(BF16))
        for h in range(MEM_HEADS):
            o_ref[s_i, :, h * MEM_HEAD_DIM:(h + 1) * MEM_HEAD_DIM] = o[h * tq:(h + 1) * tq, :]


def _mem_decode(q, k, v, layer, seqs):
    nseq, tq, _ = q.shape
    seqs = min(seqs, nseq)
    cache = pl.BlockSpec((None, seqs, MEM_TOKENS, MEM_HEADS, MEM_HEAD_DIM), lambda i: (layer, i, 0, 0, 0))
    return pl.pallas_call(
        functools.partial(_mem_decode_kernel, seqs=seqs),
        grid=(nseq // seqs,),
        in_specs=[pl.BlockSpec((seqs, tq, MEM_WIDTH), lambda i: (i, 0, 0)), cache, cache],
        out_specs=pl.BlockSpec((seqs, tq, MEM_WIDTH), lambda i: (i, 0, 0)),
        out_shape=jax.ShapeDtypeStruct((nseq, tq, MEM_WIDTH), F32),
        compiler_params=_cparams("parallel"),
        name="mem_decode",
    )(q, k, v)


ROUTER_ROWS = 40
ROUTE_ROWS = 8
HALF = D_MODEL // 2


def _pack_halves(xb):
    hi = pltpu.bitcast(xb[:, 0:HALF].astype(F32), jnp.int32)
    lo = pltpu.bitcast(xb[:, HALF:D_MODEL].astype(F32), jnp.int32)
    return hi | lax.shift_right_logical(lo, jnp.int32(16))


def _unpack_halves(p):
    hi = pltpu.bitcast(p & jnp.int32(-65536), F32).astype(BF16)
    lo = pltpu.bitcast(lax.shift_left(p, jnp.int32(16)), F32).astype(BF16)
    return hi, lo


def _merge_kernel(x_ref, u_ref, y_ref, os_ref, om_ref, g1_ref, wg_ref, dsk_ref, wglu_ref, bglu_ref,
                  wbs_ref, wbw_ref, wbm_ref, wout_ref, g2_ref, wr_ref, br_ref,
                  h_ref, xn2_ref, route_ref):
    x = x_ref[...]
    tt = x.shape[0]
    xb = _rms(x, g1_ref[...]).astype(BF16)
    z = jax.nn.gelu(y_ref[...] + dsk_ref[...] * u_ref[...])
    z = z * jax.nn.sigmoid(_dot(z.astype(BF16), wglu_ref[...]) + bglu_ref[...])
    merged = jax.nn.sigmoid(_dot(xb, wg_ref[:, 0:D_MODEL])) * _dot(z.astype(BF16), wbs_ref[...])
    merged = merged + jax.nn.sigmoid(_dot(xb, wg_ref[:, D_MODEL:2 * D_MODEL])) * _dot(os_ref[...], wbw_ref[...])
    merged = merged + jax.nn.sigmoid(_dot(xb, wg_ref[:, 2 * D_MODEL:3 * D_MODEL])) * _dot(om_ref[...], wbm_ref[...])
    h = x + _dot(merged.astype(BF16), wout_ref[...])
    h_ref[...] = h
    xn2 = _rms(h, g2_ref[...]).astype(BF16)
    xn2_ref[...] = _pack_halves(xn2)

    lt = lax.dot_general(wr_ref[...], xn2, _TRANS_B, preferred_element_type=F32) + br_ref[...]
    gl = lt[N_EXPERTS:N_EXPERTS + N_EXPERT_GROUPS]
    ge = jnp.exp(gl - jnp.max(gl, axis=0, keepdims=True))
    gp = ge / jnp.sum(ge, axis=0, keepdims=True)
    gw = jnp.max(gp, axis=0, keepdims=True)
    gidx = jnp.full((1, tt), N_EXPERT_GROUPS - 1, jnp.int32)
    for r in range(N_EXPERT_GROUPS - 2, -1, -1):
        gidx = jnp.where(gp[r:r + 1] == gw, r, gidx)
    ein = lt[(N_EXPERT_GROUPS - 1) * EXPERTS_PER_GROUP:N_EXPERTS]
    for r in range(N_EXPERT_GROUPS - 2, -1, -1):
        ein = jnp.where(gidx == r, lt[r * EXPERTS_PER_GROUP:(r + 1) * EXPERTS_PER_GROUP], ein)
    ee = jnp.exp(ein - jnp.max(ein, axis=0, keepdims=True))
    ep = ee / jnp.sum(ee, axis=0, keepdims=True)
    rowi = lax.broadcasted_iota(jnp.int32, (EXPERTS_PER_GROUP, tt), 0)
    p1 = jnp.max(ep, axis=0, keepdims=True)
    e1 = jnp.min(jnp.where(ep == p1, rowi, EXPERTS_PER_GROUP), axis=0, keepdims=True)
    ep2 = jnp.where(rowi == e1, -1.0, ep)
    p2 = jnp.max(ep2, axis=0, keepdims=True)
    e2 = jnp.min(jnp.where(ep2 == p2, rowi, EXPERTS_PER_GROUP), axis=0, keepdims=True)
    tot = p1 + p2
    w1 = p1 / tot * gw
    w2 = p2 / tot * gw
    id1 = (gidx * EXPERTS_PER_GROUP + e1).astype(F32)
    id2 = (gidx * EXPERTS_PER_GROUP + e2).astype(F32)
    route_ref[...] = jnp.concatenate([id1, id2, w1, w2, jnp.zeros((ROUTE_ROWS - 4, tt), F32)], axis=0)


def _merge(x, u, y, o_swa, o_mem, p, tile):
    n = x.shape[0]
    tile = min(tile, n)
    row = lambda i: (i, 0)
    const = lambda i: (0, 0)
    full = lambda a: pl.BlockSpec(a.shape, const, pipeline_mode=pl.Buffered(1))
    weights = [p['g1'], p['w_gates'], p['d_skip'], p['w_glu'], p['b_glu'], p['w_br_ssm'], p['w_br_swa'],
               p['w_br_mem'], p['w_out'], p['g2'], p['w_router'], p['b_router']]
    return pl.pallas_call(
        _merge_kernel,
        grid=(n // tile,),
        in_specs=[pl.BlockSpec((tile, D_MODEL), row), pl.BlockSpec((tile, SSM_WIDTH), row),
                  pl.BlockSpec((tile, SSM_WIDTH), row), pl.BlockSpec((tile, SWA_WIDTH), row),
                  pl.BlockSpec((tile, MEM_WIDTH), row)] + [full(w) for w in weights],
        out_specs=[pl.BlockSpec((tile, D_MODEL), row), pl.BlockSpec((tile, HALF), row),
                   pl.BlockSpec((ROUTE_ROWS, tile), lambda i: (0, i))],
        out_shape=[jax.ShapeDtypeStruct((n, D_MODEL), F32), jax.ShapeDtypeStruct((n, HALF), jnp.int32),
                   jax.ShapeDtypeStruct((ROUTE_ROWS, n), F32)],
        compiler_params=_cparams("parallel"),
        name="merge_router",
    )(x, u, y, o_swa, o_mem, *weights)


def _expert_mlp(xp, wg, wu, wd):
    hi, lo = _unpack_halves(xp)
    g = _dot(hi, wg[0:HALF, :]) + _dot(lo, wg[HALF:D_MODEL, :])
    u = _dot(hi, wu[0:HALF, :]) + _dot(lo, wu[HALF:D_MODEL, :])
    hh = jax.nn.silu(g) * u
    return _dot(hh.astype(BF16), wd[...])


def _moe_kernel(xn2_ref, rt_ref, wg_ref, wu_ref, wd_ref, h_ref, gf_ref, o_ref, acc_ref):
    e = pl.program_id(1)

    @pl.when(e == 0)
    def _():
        acc_ref[...] = jnp.zeros_like(acc_ref)

    o = _expert_mlp(xn2_ref[...], wg_ref[...].astype(BF16), wu_ref[...].astype(BF16), wd_ref[...].astype(BF16))
    ef = e.astype(F32)
    c = (jnp.where(rt_ref[:, 0:1] == ef, rt_ref[:, 2:3], 0.0)
         + jnp.where(rt_ref[:, 1:2] == ef, rt_ref[:, 3:4], 0.0))
    acc_ref[...] += c * o

    @pl.when(e == N_EXPERTS - 1)
    def _():
        o_ref[...] = _rms(h_ref[...] + acc_ref[...], gf_ref[...])


def _moe(xn2, route_t, w_g, w_u, w_d, h, gf, tile):
    n = h.shape[0]
    tile = min(tile, n)
    return pl.pallas_call(
        _moe_kernel,
        grid=(n // tile, N_EXPERTS),
        in_specs=[pl.BlockSpec((tile, HALF), lambda i, e: (i, 0)),
                  pl.BlockSpec((tile, ROUTE_ROWS), lambda i, e: (i, 0)),
                  pl.BlockSpec((None, D_MODEL, D_EXPERT), lambda i, e: (e, 0, 0)),
                  pl.BlockSpec((None, D_MODEL, D_EXPERT), lambda i, e: (e, 0, 0)),
                  pl.BlockSpec((None, D_EXPERT, D_MODEL), lambda i, e: (e, 0, 0)),
                  pl.BlockSpec((tile, D_MODEL), lambda i, e: (i, 0)),
                  pl.BlockSpec((1, D_MODEL), lambda i, e: (0, 0))],
        out_specs=pl.BlockSpec((tile, D_MODEL), lambda i, e: (i, 0)),
        out_shape=jax.ShapeDtypeStruct((n, D_MODEL), F32),
        scratch_shapes=[pltpu.VMEM((tile, D_MODEL), F32)],
        compiler_params=_cparams("parallel", "arbitrary"),
        name="moe_final_norm",
    )(xn2, route_t, w_g, w_u, w_d, h, gf)


EXPERT_ROW_TILE = 256
SC_CORES = 2
SC_SUBCORES = 16
SC_WORKERS = SC_CORES * SC_SUBCORES
SC_SCATTER_ROWS = 64
SC_GATHER_ROWS = 64


def _route_rank_kernel(r_ref, rank_ref, cnt_ref, base_ref):
    i = pl.program_id(0)
    tt = r_ref.shape[1]

    @pl.when(i == 0)
    def _():
        base_ref[...] = jnp.zeros_like(base_ref)

    ids = r_ref[0:2, :].astype(jnp.int32)
    e_iota = lax.broadcasted_iota(jnp.int32, (N_EXPERTS, tt), 0)
    oh1 = jnp.where(e_iota == ids[0:1], 1.0, 0.0)
    oh2 = jnp.where(e_iota == ids[1:2], 1.0, 0.0)
    before = (lax.broadcasted_iota(jnp.int32, (tt, tt), 0) < lax.broadcasted_iota(jnp.int32, (tt, tt), 1))
    tri = jnp.where(before, 1.0, 0.0).astype(BF16)
    c1 = _dot(oh1.astype(BF16), tri)
    c2 = _dot(oh2.astype(BF16), tri)
    tot1 = jnp.sum(oh1, axis=1, keepdims=True)
    tot2 = jnp.sum(oh2, axis=1, keepdims=True)
    base = base_ref[:, 0:1]
    rank1 = jnp.sum(oh1 * (base + c1), axis=0, keepdims=True)
    rank2 = jnp.sum(oh2 * (base + tot1 + c2), axis=0, keepdims=True)
    rank_ref[...] = jnp.concatenate([rank1, rank2, jnp.zeros((ROUTE_ROWS - 2, tt), F32)], axis=0).astype(jnp.int32)
    new_base = jnp.broadcast_to(base + tot1 + tot2, base_ref.shape)
    base_ref[...] = new_base
    cnt_ref[...] = new_base.astype(jnp.int32)


def _route_rank(route, tile):
    n = route.shape[1]
    tile = min(tile, n)
    return pl.pallas_call(
        _route_rank_kernel,
        grid=(n // tile,),
        in_specs=[pl.BlockSpec((ROUTE_ROWS, tile), lambda i: (0, i))],
        out_specs=[pl.BlockSpec((ROUTE_ROWS, tile), lambda i: (0, i)),
                   pl.BlockSpec((N_EXPERTS, LANES), lambda i: (0, 0))],
        out_shape=[jax.ShapeDtypeStruct((ROUTE_ROWS, n), jnp.int32),
                   jax.ShapeDtypeStruct((N_EXPERTS, LANES), jnp.int32)],
        scratch_shapes=[pltpu.VMEM((N_EXPERTS, LANES), F32)],
        compiler_params=_cparams("arbitrary"),
        name="route_rank",
    )(route)


def _sc_mesh():
    return plsc.VectorSubcoreMesh(core_axis_name="core", subcore_axis_name="subcore")


def _sc_scatter_pairs(x, pos, rows_out):
    n, d = x.shape
    per_w = n // SC_WORKERS
    window = min(SC_SCATTER_ROWS, per_w)

    @pl.kernel(out_type=jax.ShapeDtypeStruct((rows_out, d), x.dtype), mesh=_sc_mesh(),
               scratch_types=[pltpu.VMEM((window,), jnp.int32), pltpu.VMEM((window,), jnp.int32),
                              pltpu.VMEM((window, d), x.dtype), pltpu.SemaphoreType.DMA, pltpu.SemaphoreType.DMA,
                              pltpu.SemaphoreType.DMA])
    def scatter(x_hbm, p_hbm, o_hbm, i1_v, i2_v, rows_v, sem_a, sem_b, sem_c):
        wid = lax.axis_index("subcore") * SC_CORES + lax.axis_index("core")

        @pl.loop(0, per_w // window)
        def _(j):
            base = wid * per_w + j * window
            load_i1 = pltpu.async_copy(p_hbm.at[pl.ds(base, window)], i1_v, sem_a)
            load_i2 = pltpu.async_copy(p_hbm.at[pl.ds(n + base, window)], i2_v, sem_b)
            load_x = pltpu.async_copy(x_hbm.at[pl.ds(base, window)], rows_v, sem_c)
            load_i1.wait()
            load_i2.wait()
            load_x.wait()
            put_1 = pltpu.async_copy(rows_v, o_hbm.at[i1_v], sem_a)
            put_2 = pltpu.async_copy(rows_v, o_hbm.at[i2_v], sem_b)
            put_1.wait()
            put_2.wait()

    return scatter(x, pos)


def _sc_gather_rows(table, idx):
    m = idx.shape[0]
    d = table.shape[1]
    per_w = m // SC_WORKERS
    window = min(SC_GATHER_ROWS, per_w)

    assert per_w % (2 * window) == 0

    @pl.kernel(out_type=jax.ShapeDtypeStruct((m, d), table.dtype), mesh=_sc_mesh(),
               scratch_types=[pltpu.VMEM((window,), jnp.int32), pltpu.VMEM((window,), jnp.int32),
                              pltpu.VMEM((window, d), table.dtype), pltpu.VMEM((window, d), table.dtype),
                              pltpu.SemaphoreType.DMA, pltpu.SemaphoreType.DMA])
    def gather(t_hbm, i_hbm, o_hbm, ia_v, ib_v, ra_v, rb_v, sem_a, sem_b):
        wid = lax.axis_index("subcore") * SC_CORES + lax.axis_index("core")

        @pl.loop(0, per_w // (2 * window))
        def _(j):
            base_a = wid * per_w + j * (2 * window)
            base_b = base_a + window
            idx_a = pltpu.async_copy(i_hbm.at[pl.ds(base_a, window)], ia_v, sem_a)
            idx_b = pltpu.async_copy(i_hbm.at[pl.ds(base_b, window)], ib_v, sem_b)
            idx_a.wait()
            get_a = pltpu.async_copy(t_hbm.at[ia_v], ra_v, sem_a)
            idx_b.wait()
            get_b = pltpu.async_copy(t_hbm.at[ib_v], rb_v, sem_b)
            get_a.wait()
            put_a = pltpu.async_copy(ra_v, o_hbm.at[pl.ds(base_a, window)], sem_a)
            get_b.wait()
            put_b = pltpu.async_copy(rb_v, o_hbm.at[pl.ds(base_b, window)], sem_b)
            put_a.wait()
            put_b.wait()

    return gather(table, idx)


def _expert_tiles_kernel(start_ref, ntile_ref, x_hbm, wg_ref, wu_ref, wd_ref, o_hbm,
                         wg_s, wu_s, wd_s, x_buf, o_buf, in_sem, out_sem):
    e = pl.program_id(0)
    tm = x_buf.shape[1]
    row0 = start_ref[e]
    ntile = ntile_ref[e]
    wg_s[...] = wg_ref[...].astype(BF16)
    wu_s[...] = wu_ref[...].astype(BF16)
    wd_s[...] = wd_ref[...].astype(BF16)

    def rows_of(i):
        return pl.ds(pl.multiple_of(row0 + i * tm, tm), tm)

    def fetch(i, slot):
        return pltpu.make_async_copy(x_hbm.at[rows_of(i)], x_buf.at[slot], in_sem.at[slot])

    def flush(i, slot):
        return pltpu.make_async_copy(o_buf.at[slot], o_hbm.at[rows_of(i)], out_sem.at[slot])

    @pl.when(ntile > 0)
    def _():
        fetch(0, 0).start()

    def tile(i, carry):
        slot = i % 2

        @pl.when(i + 1 < ntile)
        def _():
            fetch(i + 1, 1 - slot).start()

        fetch(i, slot).wait()

        @pl.when(i >= 2)
        def _():
            flush(i - 2, slot).wait()

        o_buf[slot] = _pack_halves(_expert_mlp(x_buf[slot], wg_s, wu_s, wd_s).astype(BF16))
        flush(i, slot).start()
        return carry

    lax.fori_loop(0, ntile, tile, 0)

    @pl.when(ntile >= 2)
    def _():
        flush(ntile - 2, ntile % 2).wait()

    @pl.when(ntile >= 1)
    def _():
        flush(ntile - 1, (ntile - 1) % 2).wait()


def _expert_tiles(starts, ntiles, xs, w_g, w_u, w_d):
    rows = xs.shape[0]
    tm = EXPERT_ROW_TILE
    weight = lambda shape: pl.BlockSpec((None,) + shape, lambda e, st, nt: (e, 0, 0))
    grid_spec = pltpu.PrefetchScalarGridSpec(
        num_scalar_prefetch=2,
        grid=(N_EXPERTS,),
        in_specs=[pl.BlockSpec(memory_space=pl.ANY),
                  weight((D_MODEL, D_EXPERT)), weight((D_MODEL, D_EXPERT)), weight((D_EXPERT, D_MODEL))],
        out_specs=pl.BlockSpec(memory_space=pl.ANY),
        scratch_shapes=[pltpu.VMEM((D_MODEL, D_EXPERT), BF16), pltpu.VMEM((D_MODEL, D_EXPERT), BF16),
                        pltpu.VMEM((D_EXPERT, D_MODEL), BF16),
                        pltpu.VMEM((2, tm, HALF), jnp.int32), pltpu.VMEM((2, tm, HALF), jnp.int32),
                        pltpu.SemaphoreType.DMA((2,)), pltpu.SemaphoreType.DMA((2,))],
    )
    return pl.pallas_call(
        _expert_tiles_kernel,
        grid_spec=grid_spec,
        out_shape=jax.ShapeDtypeStruct((rows, HALF), jnp.int32),
        compiler_params=_cparams("arbitrary"),
        name="expert_tiles",
    )(starts, ntiles, xs, w_g, w_u, w_d)


def _unpack_f32(p):
    return pltpu.bitcast(p & jnp.int32(-65536), F32), pltpu.bitcast(lax.shift_left(p, jnp.int32(16)), F32)


def _combine_kernel(h_ref, o1_ref, o2_ref, rt_ref, gf_ref, y_ref):
    w1, w2 = rt_ref[:, 2:3], rt_ref[:, 3:4]
    a_lo, a_hi = _unpack_f32(o1_ref[...])
    b_lo, b_hi = _unpack_f32(o2_ref[...])
    y_lo = h_ref[:, 0:HALF] + (w1 * a_lo + w2 * b_lo)
    y_hi = h_ref[:, HALF:D_MODEL] + (w1 * a_hi + w2 * b_hi)
    ms = (jnp.sum(y_lo * y_lo, axis=-1, keepdims=True) + jnp.sum(y_hi * y_hi, axis=-1, keepdims=True)) / D_MODEL
    inv = lax.rsqrt(ms + EPS)
    y_ref[:, 0:HALF] = (y_lo * inv) * gf_ref[:, 0:HALF]
    y_ref[:, HALF:D_MODEL] = (y_hi * inv) * gf_ref[:, HALF:D_MODEL]


def _combine(h, o12, route_t, gf, tile):
    n = h.shape[0]
    tile = min(tile, n)
    nt = n // tile
    return pl.pallas_call(
        _combine_kernel,
        grid=(nt,),
        in_specs=[pl.BlockSpec((tile, D_MODEL), lambda i: (i, 0)),
                  pl.BlockSpec((tile, HALF), lambda i: (i, 0)),
                  pl.BlockSpec((tile, HALF), lambda i: (i + nt, 0)),
                  pl.BlockSpec((tile, ROUTE_ROWS), lambda i: (i, 0)),
                  pl.BlockSpec((1, D_MODEL), lambda i: (0, 0))],
        out_specs=pl.BlockSpec((tile, D_MODEL), lambda i: (i, 0)),
        out_shape=jax.ShapeDtypeStruct((n, D_MODEL), F32),
        compiler_params=_cparams("parallel"),
        name="combine_final_norm",
    )(h, o12, o12, route_t, gf)


def _sparse_moe(xn2p, route, h, w_g, w_u, w_d, gf, run_before_experts):
    n = h.shape[0]
    tm = EXPERT_ROW_TILE
    rows = 2 * n + N_EXPERTS * tm
    rank, cnt = _route_rank(route, 1024)
    counts = cnt[:, 0]
    padded = (counts + tm - 1) // tm * tm
    e_idx = jnp.arange(N_EXPERTS, dtype=jnp.int32)
    starts = jnp.sum(jnp.where(e_idx[None, :] < e_idx[:, None], padded[None, :], 0), axis=1)
    ids = route[0:2].astype(jnp.int32)
    start_of = jnp.sum(jnp.where(ids[None] == e_idx[:, None, None], starts[:, None, None], 0), axis=0)
    pos = (start_of + rank[0:2]).reshape(2 * n)
    xs = _sc_scatter_pairs(xn2p, pos, rows)
    xs, _ = lax.optimization_barrier((xs, run_before_experts))
    os_ = _expert_tiles(starts.astype(jnp.int32), (padded // tm).astype(jnp.int32), xs, w_g, w_u, w_d)
    o12 = _sc_gather_rows(os_, pos)
    return _combine(h, o12, route.T, gf, 512)


def _prep_in_weights(w_in):
    o = 0
    w_u = w_in[:, o:o + SSM_WIDTH]; o += SSM_WIDTH
    w_q = w_in[:, o:o + SWA_WIDTH]; o += SWA_WIDTH
    w_k = w_in[:, o:o + SWA_KV_WIDTH]; o += SWA_KV_WIDTH
    w_v = w_in[:, o:o + SWA_KV_WIDTH]; o += SWA_KV_WIDTH
    w_qm = w_in[:, o:o + MEM_WIDTH]; o += MEM_WIDTH
    w_g = w_in[:, o:]
    wq = (w_q * (SWA_HEAD_DIM ** -0.5)).reshape(D_MODEL, SWA_KV_HEADS, SWA_REP, SWA_HEAD_DIM)
    wq = wq.transpose(0, 2, 1, 3).reshape(D_MODEL, SWA_WIDTH)
    w_main = jnp.concatenate([w_u, wq, w_k, w_v, w_qm], axis=1).astype(BF16)
    return w_main, w_g.astype(BF16)


IN_SPLITS = (SSM_WIDTH, SWA_WIDTH, SWA_KV_WIDTH, SWA_KV_WIDTH, MEM_WIDTH)
IN_DTYPES = ((F32, BF16), (BF16,), (F32,), (F32,), (BF16,))


def kernel(x_prompt, x_sample, cache_swa_k, cache_swa_v, state_ssm_re, state_ssm_im, cache_mem_k, cache_mem_v, mem_prompt, norm1_g, w_in, lam_re, lam_im, log_dt, bm_re, bm_im, cm_re, cm_im, d_skip, w_glu, b_glu, sinks, rel_table, mem_norm_g, w_mem_kv, w_br_ssm, w_br_swa, w_br_mem, w_out, norm2_g, w_rg, b_rg, w_rexp, b_rexp, w_e_gate, w_e_up, w_e_down, final_norm_g):
    nb, t, _ = x_prompt.shape
    ns, ts, _ = x_sample.shape
    l = 0
    L = S5_CHUNK

    w_main, w_gates = _prep_in_weights(w_in[l])
    w_swa = (w_br_swa[l].reshape(SWA_KV_HEADS, SWA_REP, SWA_HEAD_DIM, D_MODEL).transpose(1, 0, 2, 3)
             .reshape(SWA_WIDTH, D_MODEL))
    pad_rows = ROUTER_ROWS - N_EXPERTS - N_EXPERT_GROUPS
    w_router = jnp.concatenate([w_rexp[l].T, w_rg[l].T, jnp.zeros((pad_rows, D_MODEL), F32)], axis=0).astype(BF16)
    b_router = jnp.concatenate([b_rexp[l], b_rg[l], jnp.zeros((pad_rows,), F32)]).reshape(ROUTER_ROWS, 1)
    mp = {
        'g1': norm1_g[l].reshape(1, D_MODEL), 'w_gates': w_gates, 'd_skip': d_skip[l].reshape(1, SSM_WIDTH),
        'w_glu': w_glu[l].astype(BF16), 'b_glu': b_glu[l].reshape(1, SSM_WIDTH),
        'w_br_ssm': w_br_ssm[l].astype(BF16), 'w_br_swa': w_swa.astype(BF16),
        'w_br_mem': w_br_mem[l].astype(BF16), 'w_out': w_out[l].astype(BF16),
        'g2': norm2_g[l].reshape(1, D_MODEL), 'w_router': w_router, 'b_router': b_router,
    }
    w_g, w_u, w_d = w_e_gate[l], w_e_up[l], w_e_down[l]
    gf = final_norm_g.reshape(1, D_MODEL)
    s5_w = _s5_weights(lam_re[l], lam_im[l], log_dt[l], bm_re[l], bm_im[l], cm_re[l], cm_im[l], L)

    bias_p = _rel_bias(rel_table, np.arange(WINDOW)[:, None] + WINDOW - np.arange(2 * WINDOW)[None, :])
    keys_s = WINDOW + 2 * ts
    bias_s = _rel_bias(rel_table, np.arange(ts)[:, None] + WINDOW - np.arange(keys_s)[None, :])
    bias_s = bias_s.reshape(SWA_HEADS * ts, keys_s)
    sink_rows = jnp.repeat(sinks[l].astype(F32), ts).reshape(SWA_HEADS * ts, 1)

    n = nb * t
    xp = x_prompt.reshape(n, D_MODEL)
    mk, mv = _norm_proj(mem_prompt.reshape(nb * MEM_TOKENS, D_MODEL), mem_norm_g[l].reshape(1, D_MODEL),
                        w_mem_kv[l].astype(BF16), (MEM_WIDTH, MEM_WIDTH), ((F32,), (F32,)), 512)
    u, ub, qz, k, v, qm = _norm_proj(xp, mp['g1'], w_main, IN_SPLITS, IN_DTYPES, 512)

    y_ssm, fin = _s5(ub, jnp.zeros((nb, N_CH_TILES * 2 * STATE_TILE), F32), s5_w, nb, t // L, L, 64)
    p_re, p_im = _tiles_to_state(fin)

    o_swa = _swa_prompt(qz, k, v, bias_p, sinks[l].astype(F32), nb, t, 4)
    o_mem = _mem_prompt(qm, mk, mv, nb, t, 512)
    h, xn2p, route = _merge(xp, u, y_ssm, o_swa, o_mem, mp, 512)

    k4 = k.reshape(nb, t, SWA_KV_HEADS, SWA_HEAD_DIM)
    v4 = v.reshape(nb, t, SWA_KV_HEADS, SWA_HEAD_DIM)
    new_k_p, new_v_p = k4[:, -WINDOW:][None], v4[:, -WINDOW:][None]
    new_mk = mk.reshape(1, nb, MEM_TOKENS, MEM_HEADS, MEM_HEAD_DIM)
    new_mv = mv.reshape(1, nb, MEM_TOKENS, MEM_HEADS, MEM_HEAD_DIM)

    m = ns * ts
    xs = x_sample.reshape(m, D_MODEL)
    us, ubs, qzs, k_s, v_s, qms = _norm_proj(xs, mp['g1'], w_main, IN_SPLITS, IN_DTYPES, 256)
    ys_ssm, fins = _s5(ubs, _state_to_tiles(state_ssm_re[l], state_ssm_im[l]), s5_w, ns, ts // L, L, 64)
    s_re, s_im = _tiles_to_state(fins)

    kk_all = jnp.concatenate([cache_swa_k[l].reshape(ns, WINDOW, SWA_KV_WIDTH).astype(F32),
                              k_s.reshape(ns, ts, SWA_KV_WIDTH)], axis=1)
    vv_all = jnp.concatenate([cache_swa_v[l].reshape(ns, WINDOW, SWA_KV_WIDTH).astype(F32),
                              v_s.reshape(ns, ts, SWA_KV_WIDTH)], axis=1)
    pad = jnp.zeros((ns, keys_s - WINDOW - ts, SWA_KV_WIDTH), F32)
    q5 = qzs.reshape(ns, ts, SWA_REP, SWA_KV_HEADS, SWA_HEAD_DIM)
    zq = jnp.zeros((ns, ts, SWA_REP, SWA_HEAD_DIM), BF16)
    q_rows = jnp.concatenate([jnp.concatenate([q5[:, :, :, 0], zq], axis=-1),
                              jnp.concatenate([zq, q5[:, :, :, 1]], axis=-1)], axis=2)
    q_rows = q_rows.transpose(0, 2, 1, 3).reshape(ns, SWA_HEADS * ts, LANES)
    o_dec = _swa_decode(q_rows, jnp.concatenate([kk_all, pad], axis=1), jnp.concatenate([vv_all, pad], axis=1),
                        bias_s, sink_rows, ts, 8)
    o_dec = o_dec.reshape(ns, SWA_KV_HEADS, SWA_REP, ts, SWA_KV_HEADS, SWA_HEAD_DIM)
    o_dec = jnp.stack([o_dec[:, g, :, :, g] for g in range(SWA_KV_HEADS)], axis=1)
    o_swa_s = o_dec.transpose(0, 3, 2, 1, 4).reshape(m, SWA_WIDTH).astype(BF16)

    o_mem_s = _mem_decode(qms.astype(F32).reshape(ns, ts, MEM_WIDTH), cache_mem_k, cache_mem_v, l, 8)
    o_mem_s = o_mem_s.reshape(m, MEM_WIDTH).astype(BF16)

    y_prompt = _sparse_moe(xn2p, route, h, w_g, w_u, w_d, gf, (ys_ssm, o_swa_s, o_mem_s)).reshape(nb, t, D_MODEL)
    hs_, xn2ps, routes = _merge(xs, us, ys_ssm, o_swa_s, o_mem_s, mp, 256)
    y_sample = _moe(xn2ps, routes.T, w_g, w_u, w_d, hs_, gf, 1024).reshape(ns, ts, D_MODEL)

    roll = lambda cache, new: jnp.concatenate(
        [cache[:, :, ts:], new.reshape(1, ns, ts, SWA_KV_HEADS, SWA_HEAD_DIM).astype(cache.dtype)], axis=2)
    new_k_s, new_v_s = roll(cache_swa_k, k_s), roll(cache_swa_v, v_s)

    return (y_prompt, y_sample,
            new_k_p, new_v_p, p_re[None], p_im[None], new_mk, new_mv,
            new_k_s, new_v_s, s_re[None].astype(state_ssm_re.dtype), s_im[None].astype(state_ssm_im.dtype))
```

```python
import functools
import math

import numpy as np
import jax
import jax.numpy as jnp
from jax import lax
from jax.experimental import pallas as pl
from jax.experimental.pallas import tpu as pltpu
from jax.experimental.pallas import tpu_sc as plsc

F32 = jnp.float32
BF16 = jnp.bfloat16

D_MODEL = 1024
SSM_WIDTH = 512
SSM_GROUP = 16
SSM_GROUPS = 32
SSM_STATE = 64
SWA_HEADS = 8
SWA_KV_HEADS = 2
SWA_REP = 4
SWA_HEAD_DIM = 64
SWA_WIDTH = 512
SWA_KV_WIDTH = 128
WINDOW = 128
REL_BUCKETS = 32
REL_MAX_DIST = 128
MEM_TOKENS = 256
MEM_HEADS = 4
MEM_HEAD_DIM = 128
MEM_WIDTH = 512
N_EXPERT_GROUPS = 4
EXPERTS_PER_GROUP = 8
N_EXPERTS = 32
D_EXPERT = 256
EPS = 1e-6
NEG_INF = -1e30

LANES = 128
GROUPS_PER_TILE = LANES // SSM_GROUP
N_CH_TILES = SSM_WIDTH // LANES
STATE_TILE = GROUPS_PER_TILE * SSM_STATE
VMEM_LIMIT = 56 * 1024 * 1024
S5_CHUNK = 8
S5_PANEL = 256

_TRANS_B = (((1,), (1,)), ((), ()))


def _cparams(*sem):
    return pltpu.CompilerParams(dimension_semantics=sem, vmem_limit_bytes=VMEM_LIMIT)


def _rms(x, g):
    return (x * lax.rsqrt(jnp.mean(x * x, axis=-1, keepdims=True) + EPS)) * g


def _dot(a, b):
    return jnp.dot(a, b, preferred_element_type=F32)


def _norm_proj_kernel(x_ref, g_ref, w_ref, *out_refs, splits, dtypes):
    xb = _rms(x_ref[...], g_ref[...]).astype(BF16)
    off = 0
    outs = iter(out_refs)
    for width, dts in zip(splits, dtypes):
        r = _dot(xb, w_ref[:, off:off + width])
        for dt in dts:
            next(outs)[...] = r.astype(dt)
        off += width


def _norm_proj(x, g, w, splits, dtypes, tile):
    n, d = x.shape
    tile = min(tile, n)
    flat = [(wd, dt) for wd, dts in zip(splits, dtypes) for dt in dts]
    return pl.pallas_call(
        functools.partial(_norm_proj_kernel, splits=tuple(splits), dtypes=tuple(dtypes)),
        grid=(n // tile,),
        in_specs=[pl.BlockSpec((tile, d), lambda i: (i, 0)),
                  pl.BlockSpec((1, d), lambda i: (0, 0)),
                  pl.BlockSpec((d, sum(splits)), lambda i: (0, 0))],
        out_specs=[pl.BlockSpec((tile, wd), lambda i: (i, 0)) for wd, _ in flat],
        out_shape=[jax.ShapeDtypeStruct((n, wd), dt) for wd, dt in flat],
        compiler_params=_cparams("parallel"),
        name="norm_proj",
    )(x, g, w)


def _s5_weights(lam_re, lam_im, log_dt, bm_re, bm_im, cm_re, cm_im, L):
    hp = lax.Precision.HIGHEST
    nt, gt, P, H = N_CH_TILES, GROUPS_PER_TILE, SSM_STATE, SSM_GROUP
    lr, li = lam_re.astype(F32), lam_im.astype(F32)
    dt = jnp.exp(log_dt.astype(F32))[:, None]
    mag = jnp.exp(lr * dt)
    a_re = mag * jnp.cos(li * dt)
    a_im = mag * jnp.sin(li * dt)
    den = lr * lr + li * li
    f_re = ((a_re - 1.0) * lr + a_im * li) / den
    f_im = (a_im * lr - (a_re - 1.0) * li) / den
    br, bi = bm_re.astype(F32), bm_im.astype(F32)
    bb_re = f_re[..., None] * br - f_im[..., None] * bi
    bb_im = f_re[..., None] * bi + f_im[..., None] * br
    pr, pi = [jnp.ones_like(a_re)], [jnp.zeros_like(a_im)]
    for _ in range(L):
        pr.append(pr[-1] * a_re - pi[-1] * a_im)
        pi.append(pr[-2] * a_im + pi[-1] * a_re)
    ap_re, ap_im = jnp.stack(pr), jnp.stack(pi)
    cr, ci = cm_re.astype(F32), cm_im.astype(F32)
    ca_re = cr[None] * ap_re[:, :, None, :] - ci[None] * ap_im[:, :, None, :]
    ca_im = cr[None] * ap_im[:, :, None, :] + ci[None] * ap_re[:, :, None, :]

    rev_re = jnp.stack([pr[L - 1 - s] for s in range(L)])
    rev_im = jnp.stack([pi[L - 1 - s] for s in range(L)])
    ws_re = rev_re[..., None] * bb_re[None] - rev_im[..., None] * bb_im[None]
    ws_im = rev_re[..., None] * bb_im[None] + rev_im[..., None] * bb_re[None]
    c_st = jnp.concatenate([ws_re.transpose(0, 1, 3, 2).reshape(L, nt, gt * H, P),
                            ws_im.transpose(0, 1, 3, 2).reshape(L, nt, gt * H, P)], axis=3).transpose(1, 0, 2, 3)
    so = lambda ca: ca[1:].transpose(1, 3, 0, 2).reshape(nt, gt * P, L * H)
    c_so = jnp.concatenate([so(ca_re), so(-ca_im)], axis=1)
    k_lag = (jnp.einsum('tghp,gpk->gkth', ca_re[:L], bb_re, precision=hp)
             - jnp.einsum('tghp,gpk->gkth', ca_im[:L], bb_im, precision=hp))
    c_k = k_lag.reshape(nt, gt * H, L * H)
    w_st, w_out, toep = _s5_expand(c_st, c_so, c_k, L)

    def per_tile(v):
        return v.reshape(nt, 1, STATE_TILE)

    return w_st, w_out, toep, per_tile(pr[L]), per_tile(pi[L])


def _s5_expand_kernel(cst_ref, cso_ref, ck_ref, wst_ref, wso_ref, toep_ref, *, L):
    hp = lax.Precision.HIGHEST
    P, H = SSM_STATE, SSM_GROUP
    iota = lambda shape, d: lax.broadcasted_iota(jnp.int32, shape, d)
    one = lambda cond: jnp.where(cond, 1.0, 0.0).astype(F32)

    r, c = iota((2 * P, 2 * STATE_TILE), 0), iota((2 * P, 2 * STATE_TILE), 1)
    rep_st = one((r // P == c // STATE_TILE) & (r % P == c % P))
    r, c = iota((LANES, 2 * STATE_TILE), 0), iota((LANES, 2 * STATE_TILE), 1)
    own_st = one(r // H == (c % STATE_TILE) // P)
    for s in range(L):
        blk = jnp.dot(cst_ref[s], rep_st, precision=hp, preferred_element_type=F32) * own_st
        wst_ref[s * LANES:(s + 1) * LANES, :] = blk.astype(BF16)

    r, c = iota((LANES, LANES), 0), iota((LANES, LANES), 1)
    pick = [one((r // H == t) & (r % H == c % H)) for t in range(L)]
    own_k = one(r // H == c // H)
    r, c = iota((2 * STATE_TILE, LANES), 0), iota((2 * STATE_TILE, LANES), 1)
    own_so = one((r % STATE_TILE) // P == c // H)
    cso = cso_ref[...]
    for t in range(L):
        blk = jnp.dot(cso, pick[t], precision=hp, preferred_element_type=F32) * own_so
        wso_ref[:, t * LANES:(t + 1) * LANES] = blk.astype(BF16)
    ck = ck_ref[...]
    lag = [(jnp.dot(ck, pick[t], precision=hp, preferred_element_type=F32) * own_k).astype(BF16) for t in range(L)]
    zero = jnp.zeros((LANES, LANES), BF16)
    for s in range(L):
        for t in range(L):
            toep_ref[s * LANES:(s + 1) * LANES, t * LANES:(t + 1) * LANES] = lag[t - s] if t >= s else zero


def _s5_expand(c_st, c_so, c_k, L):
    lk = L * LANES
    st2 = 2 * STATE_TILE
    return pl.pallas_call(
        functools.partial(_s5_expand_kernel, L=L),
        grid=(N_CH_TILES,),
        in_specs=[pl.BlockSpec((None, L, LANES, 2 * SSM_STATE), lambda j: (j, 0, 0, 0)),
                  pl.BlockSpec((None, st2, L * SSM_GROUP), lambda j: (j, 0, 0)),
                  pl.BlockSpec((None, LANES, L * SSM_GROUP), lambda j: (j, 0, 0))],
        out_specs=[pl.BlockSpec((None, lk, st2), lambda j: (j, 0, 0)),
                   pl.BlockSpec((None, st2, lk), lambda j: (j, 0, 0)),
                   pl.BlockSpec((None, lk, lk), lambda j: (j, 0, 0))],
        out_shape=[jax.ShapeDtypeStruct((N_CH_TILES, lk, st2), BF16),
                   jax.ShapeDtypeStruct((N_CH_TILES, st2, lk), BF16),
                   jax.ShapeDtypeStruct((N_CH_TILES, lk, lk), BF16)],
        compiler_params=_cparams("parallel"),
        name="s5_expand_weights",
    )(c_st, c_so, c_k)


def _to_chunks(u, nb, nc, L):
    return (u.reshape(nb, nc, L, N_CH_TILES, LANES).transpose(1, 0, 3, 2, 4)
            .reshape(nc * nb, N_CH_TILES * L * LANES))


def _from_chunks(y, nb, nc, L):
    return (y.reshape(nc, nb, N_CH_TILES, L, LANES).transpose(1, 0, 3, 2, 4)
            .reshape(nb * nc * L, SSM_WIDTH))


def _s5_kernel(x_ref, h0_ref, are_ref, aim_ref, ws_ref, t_ref, wo_ref, y_ref, fin_ref,
               hr_ref, hi_ref, d_ref, hs_ref, *, cb, nb):
    ci = pl.program_id(1)

    @pl.when(ci == 0)
    def _():
        hr_ref[...] = h0_ref[:, 0:STATE_TILE]
        hi_ref[...] = h0_ref[:, STATE_TILE:2 * STATE_TILE]

    x = x_ref[...]
    d_ref[...] = _dot(x, ws_ref[...])
    ar = jnp.broadcast_to(are_ref[...], (nb, STATE_TILE))
    ai = jnp.broadcast_to(aim_ref[...], (nb, STATE_TILE))

    def body(c, carry):
        hr, hi = carry
        r0 = pl.multiple_of(c * nb, nb)
        hs_ref[pl.ds(r0, nb), 0:STATE_TILE] = hr
        hs_ref[pl.ds(r0, nb), STATE_TILE:2 * STATE_TILE] = hi
        d = d_ref[pl.ds(r0, nb), :]
        return (ar * hr - ai * hi + d[:, 0:STATE_TILE],
                ar * hi + ai * hr + d[:, STATE_TILE:2 * STATE_TILE])

    hr, hi = lax.fori_loop(0, cb, body, (hr_ref[...], hi_ref[...]))
    hr_ref[...] = hr
    hi_ref[...] = hi
    hsb = hs_ref[...].astype(BF16)
    for c0 in range(0, t_ref.shape[1], S5_PANEL):
        c1 = c0 + S5_PANEL
        y_ref[:, c0:c1] = _dot(x[:, 0:c1], t_ref[0:c1, c0:c1]) + _dot(hsb, wo_ref[:, c0:c1])

    @pl.when(ci == pl.num_programs(1) - 1)
    def _():
        fin_ref[:, 0:STATE_TILE] = hr
        fin_ref[:, STATE_TILE:2 * STATE_TILE] = hi


def _s5(ub, h0, weights, nb, nc, L, chunk_block):
    w_st, w_so, toep, a_re, a_im = weights
    xc = _to_chunks(ub, nb, nc, L)
    cb = min(chunk_block, nc)
    rows = cb * nb
    lk = L * LANES
    st2 = 2 * STATE_TILE
    tile_w = lambda shape: pl.BlockSpec((None,) + shape, lambda j, c: (j, 0, 0))
    y, fin = pl.pallas_call(
        functools.partial(_s5_kernel, cb=cb, nb=nb),
        grid=(N_CH_TILES, nc // cb),
        in_specs=[pl.BlockSpec((rows, lk), lambda j, c: (c, j)),
                  pl.BlockSpec((nb, st2), lambda j, c: (0, j)),
                  tile_w((1, STATE_TILE)), tile_w((1, STATE_TILE)),
                  tile_w((lk, st2)), tile_w((lk, lk)), tile_w((st2, lk))],
        out_specs=[pl.BlockSpec((rows, lk), lambda j, c: (c, j)),
                   pl.BlockSpec((nb, st2), lambda j, c: (0, j))],
        out_shape=[jax.ShapeDtypeStruct((nc * nb, N_CH_TILES * lk), F32),
                   jax.ShapeDtypeStruct((nb, N_CH_TILES * st2), F32)],
        scratch_shapes=[pltpu.VMEM((nb, STATE_TILE), F32), pltpu.VMEM((nb, STATE_TILE), F32),
                        pltpu.VMEM((rows, st2), F32), pltpu.VMEM((rows, st2), F32)],
        compiler_params=_cparams("parallel", "arbitrary"),
        name="s5_chunked_scan",
    )(xc, h0, a_re, a_im, w_st, toep, w_so)
    return _from_chunks(y, nb, nc, L), fin


def _state_to_tiles(h_re, h_im):
    nb = h_re.shape[0]
    r = h_re.astype(F32).reshape(nb, N_CH_TILES, STATE_TILE)
    i = h_im.astype(F32).reshape(nb, N_CH_TILES, STATE_TILE)
    return jnp.concatenate([r, i], axis=-1).reshape(nb, N_CH_TILES * 2 * STATE_TILE)


def _tiles_to_state(h):
    nb = h.shape[0]
    h = h.reshape(nb, N_CH_TILES, 2, GROUPS_PER_TILE, SSM_STATE)
    return (h[:, :, 0].reshape(nb, SSM_GROUPS, SSM_STATE), h[:, :, 1].reshape(nb, SSM_GROUPS, SSM_STATE))


def _t5_bucket(dist):
    n = np.maximum(dist, 0)
    max_exact = REL_BUCKETS // 2
    nf = np.maximum(n, 1).astype(np.float32)
    large = max_exact + (np.log(nf / np.float32(max_exact)) / np.float32(math.log(REL_MAX_DIST / max_exact))
                         * np.float32(REL_BUCKETS - max_exact)).astype(np.int32)
    large = np.minimum(large, REL_BUCKETS - 1)
    return np.where(n < max_exact, n, large)


def _rel_bias(rel_table, dist):
    bucket = _t5_bucket(dist)
    tab = rel_table.astype(F32)
    out = jnp.zeros((SWA_HEADS,) + dist.shape, F32)
    for b in range(REL_BUCKETS):
        sel = jnp.asarray(bucket == b)
        if bool((bucket == b).any()):
            out = jnp.where(sel[None], tab[b].reshape((SWA_HEADS,) + (1,) * dist.ndim), out)
    return out


def _swa_prompt_kernel(sink_ref, q_ref, kp_ref, kc_ref, vp_ref, vc_ref, bias_ref, o_ref, kk_ref, vv_ref, *, qblocks):
    step = pl.program_id(1)
    kk_ref[0:WINDOW, :] = kp_ref[...].astype(BF16)
    kk_ref[WINDOW:, :] = kc_ref[...].astype(BF16)
    vv_ref[0:WINDOW, :] = vp_ref[...].astype(BF16)
    vv_ref[WINDOW:, :] = vc_ref[...].astype(BF16)
    row = lax.broadcasted_iota(jnp.int32, (WINDOW, 2 * WINDOW), 0)
    col = lax.broadcasted_iota(jnp.int32, (WINDOW, 2 * WINDOW), 1)
    dist = row + WINDOW - col
    band = (dist >= 0) & (dist < WINDOW)
    lane = lax.broadcasted_iota(jnp.int32, (WINDOW, LANES), 1)
    low = lane < SWA_HEAD_DIM

    def block(j, carry):
        r0 = pl.multiple_of(j * WINDOW, WINDOW)
        kk = kk_ref[pl.ds(r0, 2 * WINDOW), :]
        vv = vv_ref[pl.ds(r0, 2 * WINDOW), :]
        valid = band & ((col >= WINDOW) | (step * qblocks + j > 0))
        for t in range(SWA_REP):
            q2 = q_ref[pl.ds(r0, WINDOW), t * LANES:(t + 1) * LANES]
            outs = []
            for half in range(SWA_KV_HEADS):
                h = t + SWA_REP * half
                qh = jnp.where(low if half == 0 else jnp.logical_not(low), q2, jnp.zeros_like(q2))
                s = lax.dot_general(qh, kk, _TRANS_B, preferred_element_type=F32)
                s = jnp.where(valid, s + bias_ref[h], NEG_INF)
                sink = sink_ref[h]
                m = jnp.maximum(jnp.max(s, axis=-1, keepdims=True), sink)
                e = jnp.exp(s - m)
                den = jnp.sum(e, axis=-1, keepdims=True) + jnp.exp(sink - m)
                outs.append(_dot(e.astype(BF16), vv) * (1.0 / den))
            o_ref[pl.ds(r0, WINDOW), t * LANES:(t + 1) * LANES] = jnp.where(low, outs[0], outs[1]).astype(BF16)
        return carry

    lax.fori_loop(0, qblocks, block, 0)


def _swa_prompt(q, k, v, bias, sinks, nb, t, qblocks):
    nstep = t // (WINDOW * qblocks)
    rows = WINDOW * qblocks
    cur = lambda b, i: (b * nstep + i, 0)
    prev = lambda b, i: (b * nstep * qblocks + jnp.maximum(i * qblocks - 1, 0), 0)
    return pl.pallas_call(
        functools.partial(_swa_prompt_kernel, qblocks=qblocks),
        grid=(nb, nstep),
        in_specs=[pl.BlockSpec(memory_space=pltpu.SMEM),
                  pl.BlockSpec((rows, SWA_WIDTH), cur),
                  pl.BlockSpec((WINDOW, SWA_KV_WIDTH), prev),
                  pl.BlockSpec((rows, SWA_KV_WIDTH), cur),
                  pl.BlockSpec((WINDOW, SWA_KV_WIDTH), prev),
                  pl.BlockSpec((rows, SWA_KV_WIDTH), cur),
                  pl.BlockSpec((SWA_HEADS, WINDOW, 2 * WINDOW), lambda b, i: (0, 0, 0))],
        out_specs=pl.BlockSpec((rows, SWA_WIDTH), cur),
        out_shape=jax.ShapeDtypeStruct((nb * t, SWA_WIDTH), BF16),
        scratch_shapes=[pltpu.VMEM((rows + WINDOW, SWA_KV_WIDTH), BF16),
                        pltpu.VMEM((rows + WINDOW, SWA_KV_WIDTH), BF16)],
        compiler_params=_cparams("parallel", "parallel"),
        name="swa_prompt",
    )(sinks, q, k, k, v, v, bias)


def _swa_decode_kernel(q_ref, k_ref, v_ref, bias_ref, sink_ref, o_ref, *, seqs, tq):
    rows, keys = q_ref.shape[1], k_ref.shape[1]
    qi = lax.broadcasted_iota(jnp.int32, (rows, keys), 0) % tq
    col = lax.broadcasted_iota(jnp.int32, (rows, keys), 1)
    dist = qi + WINDOW - col
    valid = (dist >= 0) & (dist < WINDOW)
    bias = bias_ref[...]
    sink = sink_ref[...]
    for s_i in range(seqs):
        kk = k_ref[s_i].astype(BF16)
        s = lax.dot_general(q_ref[s_i], kk, _TRANS_B, preferred_element_type=F32)
        s = jnp.where(valid, s + bias, NEG_INF)
        m = jnp.maximum(jnp.max(s, axis=-1, keepdims=True), sink)
        e = jnp.exp(s - m)
        den = jnp.sum(e, axis=-1, keepdims=True) + jnp.exp(sink - m)
        o_ref[s_i] = _dot(e.astype(BF16), v_ref[s_i].astype(BF16)) * (1.0 / den)


def _swa_decode(qz, k_all, v_all, bias, sink_rows, tq, seqs):
    nseq, rows, _ = qz.shape
    keys = k_all.shape[1]
    seqs = min(seqs, nseq)
    return pl.pallas_call(
        functools.partial(_swa_decode_kernel, seqs=seqs, tq=tq),
        grid=(nseq // seqs,),
        in_specs=[pl.BlockSpec((seqs, rows, LANES), lambda i: (i, 0, 0)),
                  pl.BlockSpec((seqs, keys, LANES), lambda i: (i, 0, 0)),
                  pl.BlockSpec((seqs, keys, LANES), lambda i: (i, 0, 0)),
                  pl.BlockSpec((rows, keys), lambda i: (0, 0)),
                  pl.BlockSpec((rows, 1), lambda i: (0, 0))],
        out_specs=pl.BlockSpec((seqs, rows, LANES), lambda i: (i, 0, 0)),
        out_shape=jax.ShapeDtypeStruct((nseq, rows, LANES), F32),
        compiler_params=_cparams("parallel"),
        name="swa_decode",
    )(qz, k_all, v_all, bias, sink_rows)


def _softmax(s):
    m = jnp.max(s, axis=-1, keepdims=True)
    e = jnp.exp(s - m)
    return e * (1.0 / jnp.sum(e, axis=-1, keepdims=True))


def _mem_prompt_kernel(q_ref, k_ref, v_ref, o_ref, s_ref, p_ref):
    scale = MEM_HEAD_DIM ** -0.5
    heads = [slice(h * MEM_HEAD_DIM, (h + 1) * MEM_HEAD_DIM) for h in range(MEM_HEADS)]
    for h, sl in enumerate(heads):
        s_ref[h] = lax.dot_general(q_ref[:, sl], k_ref[:, sl].astype(BF16), _TRANS_B, preferred_element_type=F32)
    s = s_ref[...] * scale
    e = jnp.exp(s - jnp.max(s, axis=-1, keepdims=True))
    p_ref[...] = e.astype(BF16)
    inv = 1.0 / jnp.sum(e, axis=-1, keepdims=True)
    for h, sl in enumerate(heads):
        o_ref[:, sl] = (_dot(p_ref[h], v_ref[:, sl].astype(BF16)) * inv[h]).astype(BF16)


def _mem_prompt(qm, mk, mv, nb, t, tile):
    tile = min(tile, t)
    nt = t // tile
    return pl.pallas_call(
        _mem_prompt_kernel,
        grid=(nb, nt),
        in_specs=[pl.BlockSpec((tile, MEM_WIDTH), lambda b, i: (b * nt + i, 0)),
                  pl.BlockSpec((MEM_TOKENS, MEM_WIDTH), lambda b, i: (b, 0)),
                  pl.BlockSpec((MEM_TOKENS, MEM_WIDTH), lambda b, i: (b, 0))],
        out_specs=pl.BlockSpec((tile, MEM_WIDTH), lambda b, i: (b * nt + i, 0)),
        out_shape=jax.ShapeDtypeStruct((nb * t, MEM_WIDTH), BF16),
        scratch_shapes=[pltpu.VMEM((MEM_HEADS, tile, MEM_TOKENS), F32), pltpu.VMEM((MEM_HEADS, tile, MEM_TOKENS), BF16)],
        compiler_params=_cparams("parallel", "parallel"),
        name="mem_prompt",
    )(qm, mk, mv)


def _mem_decode_kernel(q_ref, k_ref, v_ref, o_ref, *, seqs):
    tq = q_ref.shape[1]
    rows, cols = MEM_HEADS * tq, MEM_TOKENS * MEM_HEADS
    k2 = k_ref.reshape(seqs, cols, MEM_HEAD_DIM)
    v2 = v_ref.reshape(seqs, cols, MEM_HEAD_DIM)
    scale = MEM_HEAD_DIM ** -0.5
    own = (lax.broadcasted_iota(jnp.int32, (rows, cols), 1) % MEM_HEADS
           == lax.broadcasted_iota(jnp.int32, (rows, cols), 0) // tq)
    for s_i in range(seqs):
        q = q_ref[s_i]
        qb = jnp.concatenate([q[:, h * MEM_HEAD_DIM:(h + 1) * MEM_HEAD_DIM] for h in range(MEM_HEADS)], axis=0)
        s = lax.dot_general(qb.astype(BF16), k2[s_i].astype(BF16), _TRANS_B, preferred_element_type=F32) * scale
        p = _softmax(jnp.where(own, s, NEG_INF)).astype(BF16)
        o = _dot(p, v2[s_i].astype(BF16))
        for h in range(MEM_HEADS):
            o_ref[s_i, :, h * MEM_HEAD_DIM:(h + 1) * MEM_HEAD_DIM] = o[h * tq:(h + 1) * tq, :]


def _mem_decode(q, k, v, layer, seqs):
    nseq, tq, _ = q.shape
    seqs = min(seqs, nseq)
    cache = pl.BlockSpec((None, seqs, MEM_TOKENS, MEM_HEADS, MEM_HEAD_DIM), lambda i: (layer, i, 0, 0, 0))
    return pl.pallas_call(
        functools.partial(_mem_decode_kernel, seqs=seqs),
        grid=(nseq // seqs,),
        in_specs=[pl.BlockSpec((seqs, tq, MEM_WIDTH), lambda i: (i, 0, 0)), cache, cache],
        out_specs=pl.BlockSpec((seqs, tq, MEM_WIDTH), lambda i: (i, 0, 0)),
        out_shape=jax.ShapeDtypeStruct((nseq, tq, MEM_WIDTH), F32),
        compiler_params=_cparams("parallel"),
        name="mem_decode",
    )(q, k, v)


ROUTER_ROWS = 40
ROUTE_ROWS = 8
HALF = D_MODEL // 2


def _pack_halves(xb):
    hi = pltpu.bitcast(xb[:, 0:HALF].astype(F32), jnp.int32)
    lo = pltpu.bitcast(xb[:, HALF:D_MODEL].astype(F32), jnp.int32)
    return hi | lax.shift_right_logical(lo, jnp.int32(16))


def _unpack_halves(p):
    hi = pltpu.bitcast(p & jnp.int32(-65536), F32).astype(BF16)
    lo = pltpu.bitcast(lax.shift_left(p, jnp.int32(16)), F32).astype(BF16)
    return hi, lo


def _merge_kernel(x_ref, u_ref, y_ref, os_ref, om_ref, g1_ref, wg_ref, dsk_ref, wglu_ref, bglu_ref,
                  wbs_ref, wbw_ref, wbm_ref, wout_ref, g2_ref, wr_ref, br_ref,
                  h_ref, xn2_ref, route_ref):
    x = x_ref[...]
    tt = x.shape[0]
    xb = _rms(x, g1_ref[...]).astype(BF16)
    z = jax.nn.gelu(y_ref[...] + dsk_ref[...] * u_ref[...])
    z = z * jax.nn.sigmoid(_dot(z.astype(BF16), wglu_ref[...]) + bglu_ref[...])
    merged = jax.nn.sigmoid(_dot(xb, wg_ref[:, 0:D_MODEL])) * _dot(z.astype(BF16), wbs_ref[...])
    merged = merged + jax.nn.sigmoid(_dot(xb, wg_ref[:, D_MODEL:2 * D_MODEL])) * _dot(os_ref[...], wbw_ref[...])
    merged = merged + jax.nn.sigmoid(_dot(xb, wg_ref[:, 2 * D_MODEL:3 * D_MODEL])) * _dot(om_ref[...], wbm_ref[...])
    h = x + _dot(merged.astype(BF16), wout_ref[...])
    h_ref[...] = h
    xn2 = _rms(h, g2_ref[...]).astype(BF16)
    xn2_ref[...] = _pack_halves(xn2)

    lt = lax.dot_general(wr_ref[...], xn2, _TRANS_B, preferred_element_type=F32) + br_ref[...]
    gl = lt[N_EXPERTS:N_EXPERTS + N_EXPERT_GROUPS]
    ge = jnp.exp(gl - jnp.max(gl, axis=0, keepdims=True))
    gp = ge / jnp.sum(ge, axis=0, keepdims=True)
    gw = jnp.max(gp, axis=0, keepdims=True)
    gidx = jnp.full((1, tt), N_EXPERT_GROUPS - 1, jnp.int32)
    for r in range(N_EXPERT_GROUPS - 2, -1, -1):
        gidx = jnp.where(gp[r:r + 1] == gw, r, gidx)
    ein = lt[(N_EXPERT_GROUPS - 1) * EXPERTS_PER_GROUP:N_EXPERTS]
    for r in range(N_EXPERT_GROUPS - 2, -1, -1):
        ein = jnp.where(gidx == r, lt[r * EXPERTS_PER_GROUP:(r + 1) * EXPERTS_PER_GROUP], ein)
    ee = jnp.exp(ein - jnp.max(ein, axis=0, keepdims=True))
    ep = ee / jnp.sum(ee, axis=0, keepdims=True)
    rowi = lax.broadcasted_iota(jnp.int32, (EXPERTS_PER_GROUP, tt), 0)
    p1 = jnp.max(ep, axis=0, keepdims=True)
    e1 = jnp.min(jnp.where(ep == p1, rowi, EXPERTS_PER_GROUP), axis=0, keepdims=True)
    ep2 = jnp.where(rowi == e1, -1.0, ep)
    p2 = jnp.max(ep2, axis=0, keepdims=True)
    e2 = jnp.min(jnp.where(ep2 == p2, rowi, EXPERTS_PER_GROUP), axis=0, keepdims=True)
    tot = p1 + p2
    w1 = p1 / tot * gw
    w2 = p2 / tot * gw
    id1 = (gidx * EXPERTS_PER_GROUP + e1).astype(F32)
    id2 = (gidx * EXPERTS_PER_GROUP + e2).astype(F32)
    route_ref[...] = jnp.concatenate([id1, id2, w1, w2, jnp.zeros((ROUTE_ROWS - 4, tt), F32)], axis=0)


def _merge(x, u, y, o_swa, o_mem, p, tile):
    n = x.shape[0]
    tile = min(tile, n)
    row = lambda i: (i, 0)
    const = lambda i: (0, 0)
    full = lambda a: pl.BlockSpec(a.shape, const, pipeline_mode=pl.Buffered(1))
    weights = [p['g1'], p['w_gates'], p['d_skip'], p['w_glu'], p['b_glu'], p['w_br_ssm'], p['w_br_swa'],
               p['w_br_mem'], p['w_out'], p['g2'], p['w_router'], p['b_router']]
    return pl.pallas_call(
        _merge_kernel,
        grid=(n // tile,),
        in_specs=[pl.BlockSpec((tile, D_MODEL), row), pl.BlockSpec((tile, SSM_WIDTH), row),
                  pl.BlockSpec((tile, SSM_WIDTH), row), pl.BlockSpec((tile, SWA_WIDTH), row),
                  pl.BlockSpec((tile, MEM_WIDTH), row)] + [full(w) for w in weights],
        out_specs=[pl.BlockSpec((tile, D_MODEL), row), pl.BlockSpec((tile, HALF), row),
                   pl.BlockSpec((ROUTE_ROWS, tile), lambda i: (0, i))],
        out_shape=[jax.ShapeDtypeStruct((n, D_MODEL), F32), jax.ShapeDtypeStruct((n, HALF), jnp.int32),
                   jax.ShapeDtypeStruct((ROUTE_ROWS, n), F32)],
        compiler_params=_cparams("parallel"),
        name="merge_router",
    )(x, u, y, o_swa, o_mem, *weights)


def _expert_mlp(xp, wg, wu, wd):
    hi, lo = _unpack_halves(xp)
    g = _dot(hi, wg[0:HALF, :]) + _dot(lo, wg[HALF:D_MODEL, :])
    u = _dot(hi, wu[0:HALF, :]) + _dot(lo, wu[HALF:D_MODEL, :])
    hh = jax.nn.silu(g) * u
    return _dot(hh.astype(BF16), wd[...])


def _moe_kernel(xn2_ref, rt_ref, wg_ref, wu_ref, wd_ref, h_ref, gf_ref, o_ref, acc_ref):
    e = pl.program_id(1)

    @pl.when(e == 0)
    def _():
        acc_ref[...] = jnp.zeros_like(acc_ref)

    o = _expert_mlp(xn2_ref[...], wg_ref[...].astype(BF16), wu_ref[...].astype(BF16), wd_ref[...].astype(BF16))
    ef = e.astype(F32)
    c = (jnp.where(rt_ref[:, 0:1] == ef, rt_ref[:, 2:3], 0.0)
         + jnp.where(rt_ref[:, 1:2] == ef, rt_ref[:, 3:4], 0.0))
    acc_ref[...] += c * o

    @pl.when(e == N_EXPERTS - 1)
    def _():
        o_ref[...] = _rms(h_ref[...] + acc_ref[...], gf_ref[...])


def _moe(xn2, route_t, w_g, w_u, w_d, h, gf, tile):
    n = h.shape[0]
    tile = min(tile, n)
    return pl.pallas_call(
        _moe_kernel,
        grid=(n // tile, N_EXPERTS),
        in_specs=[pl.BlockSpec((tile, HALF), lambda i, e: (i, 0)),
                  pl.BlockSpec((tile, ROUTE_ROWS), lambda i, e: (i, 0)),
                  pl.BlockSpec((None, D_MODEL, D_EXPERT), lambda i, e: (e, 0, 0)),
                  pl.BlockSpec((None, D_MODEL, D_EXPERT), lambda i, e: (e, 0, 0)),
                  pl.BlockSpec((None, D_EXPERT, D_MODEL), lambda i, e: (e, 0, 0)),
                  pl.BlockSpec((tile, D_MODEL), lambda i, e: (i, 0)),
                  pl.BlockSpec((1, D_MODEL), lambda i, e: (0, 0))],
        out_specs=pl.BlockSpec((tile, D_MODEL), lambda i, e: (i, 0)),
        out_shape=jax.ShapeDtypeStruct((n, D_MODEL), F32),
        scratch_shapes=[pltpu.VMEM((tile, D_MODEL), F32)],
        compiler_params=_cparams("parallel", "arbitrary"),
        name="moe_final_norm",
    )(xn2, route_t, w_g, w_u, w_d, h, gf)


EXPERT_ROW_TILE = 256
EXPERT_SLOTS = 4
SC_CORES = 2
SC_SUBCORES = 16
SC_WORKERS = SC_CORES * SC_SUBCORES
SC_SCATTER_ROWS = 64
SC_GATHER_ROWS = 64


def _route_rank_kernel(r_ref, rank_ref, cnt_ref, base_ref):
    i = pl.program_id(0)
    tt = r_ref.shape[1]

    @pl.when(i == 0)
    def _():
        base_ref[...] = jnp.zeros_like(base_ref)

    ids = r_ref[0:2, :].astype(jnp.int32)
    e_iota = lax.broadcasted_iota(jnp.int32, (N_EXPERTS, tt), 0)
    oh1 = jnp.where(e_iota == ids[0:1], 1.0, 0.0)
    oh2 = jnp.where(e_iota == ids[1:2], 1.0, 0.0)
    before = (lax.broadcasted_iota(jnp.int32, (tt, tt), 0) < lax.broadcasted_iota(jnp.int32, (tt, tt), 1))
    tri = jnp.where(before, 1.0, 0.0).astype(BF16)
    c1 = _dot(oh1.astype(BF16), tri)
    c2 = _dot(oh2.astype(BF16), tri)
    tot1 = jnp.sum(oh1, axis=1, keepdims=True)
    tot2 = jnp.sum(oh2, axis=1, keepdims=True)
    base = base_ref[:, 0:1]
    rank1 = jnp.sum(oh1 * (base + c1), axis=0, keepdims=True)
    rank2 = jnp.sum(oh2 * (base + tot1 + c2), axis=0, keepdims=True)
    rank_ref[...] = jnp.concatenate([rank1, rank2, jnp.zeros((ROUTE_ROWS - 2, tt), F32)], axis=0).astype(jnp.int32)
    new_base = jnp.broadcast_to(base + tot1 + tot2, base_ref.shape)
    base_ref[...] = new_base
    cnt_ref[...] = new_base.astype(jnp.int32)


def _route_rank(route, tile):
    n = route.shape[1]
    tile = min(tile, n)
    return pl.pallas_call(
        _route_rank_kernel,
        grid=(n // tile,),
        in_specs=[pl.BlockSpec((ROUTE_ROWS, tile), lambda i: (0, i))],
        out_specs=[pl.BlockSpec((ROUTE_ROWS, tile), lambda i: (0, i)),
                   pl.BlockSpec((N_EXPERTS, LANES), lambda i: (0, 0))],
        out_shape=[jax.ShapeDtypeStruct((ROUTE_ROWS, n), jnp.int32),
                   jax.ShapeDtypeStruct((N_EXPERTS, LANES), jnp.int32)],
        scratch_shapes=[pltpu.VMEM((N_EXPERTS, LANES), F32)],
        compiler_params=_cparams("arbitrary"),
        name="route_rank",
    )(route)


def _sc_mesh():
    return plsc.VectorSubcoreMesh(core_axis_name="core", subcore_axis_name="subcore")


def _sc_scatter_pairs(x, pos, rows_out):
    n, d = x.shape
    per_w = n // SC_WORKERS
    window = min(SC_SCATTER_ROWS, per_w)

    @pl.kernel(out_type=jax.ShapeDtypeStruct((rows_out, d), x.dtype), mesh=_sc_mesh(),
               scratch_types=[pltpu.VMEM((window,), jnp.int32), pltpu.VMEM((window,), jnp.int32),
                              pltpu.VMEM((window, d), x.dtype), pltpu.SemaphoreType.DMA, pltpu.SemaphoreType.DMA,
                              pltpu.SemaphoreType.DMA])
    def scatter(x_hbm, p_hbm, o_hbm, i1_v, i2_v, rows_v, sem_a, sem_b, sem_c):
        wid = lax.axis_index("subcore") * SC_CORES + lax.axis_index("core")

        @pl.loop(0, per_w // window)
        def _(j):
            base = wid * per_w + j * window
            load_i1 = pltpu.async_copy(p_hbm.at[pl.ds(base, window)], i1_v, sem_a)
            load_i2 = pltpu.async_copy(p_hbm.at[pl.ds(n + base, window)], i2_v, sem_b)
            load_x = pltpu.async_copy(x_hbm.at[pl.ds(base, window)], rows_v, sem_c)
            load_i1.wait()
            load_i2.wait()
            load_x.wait()
            put_1 = pltpu.async_copy(rows_v, o_hbm.at[i1_v], sem_a)
            put_2 = pltpu.async_copy(rows_v, o_hbm.at[i2_v], sem_b)
            put_1.wait()
            put_2.wait()

    return scatter(x, pos)


def _sc_gather_rows(table, idx):
    m = idx.shape[0]
    d = table.shape[1]
    per_w = m // SC_WORKERS
    window = min(SC_GATHER_ROWS, per_w)

    assert per_w % (2 * window) == 0

    @pl.kernel(out_type=jax.ShapeDtypeStruct((m, d), table.dtype), mesh=_sc_mesh(),
               scratch_types=[pltpu.VMEM((window,), jnp.int32), pltpu.VMEM((window,), jnp.int32),
                              pltpu.VMEM((window, d), table.dtype), pltpu.VMEM((window, d), table.dtype),
                              pltpu.SemaphoreType.DMA, pltpu.SemaphoreType.DMA])
    def gather(t_hbm, i_hbm, o_hbm, ia_v, ib_v, ra_v, rb_v, sem_a, sem_b):
        wid = lax.axis_index("subcore") * SC_CORES + lax.axis_index("core")

        @pl.loop(0, per_w // (2 * window))
        def _(j):
            base_a = wid * per_w + j * (2 * window)
            base_b = base_a + window
            idx_a = pltpu.async_copy(i_hbm.at[pl.ds(base_a, window)], ia_v, sem_a)
            idx_b = pltpu.async_copy(i_hbm.at[pl.ds(base_b, window)], ib_v, sem_b)
            idx_a.wait()
            get_a = pltpu.async_copy(t_hbm.at[ia_v], ra_v, sem_a)
            idx_b.wait()
            get_b = pltpu.async_copy(t_hbm.at[ib_v], rb_v, sem_b)
            get_a.wait()
            put_a = pltpu.async_copy(ra_v, o_hbm.at[pl.ds(base_a, window)], sem_a)
            get_b.wait()
            put_b = pltpu.async_copy(rb_v, o_hbm.at[pl.ds(base_b, window)], sem_b)
            put_a.wait()
            put_b.wait()

    return gather(table, idx)


def _expert_tiles_kernel(start_ref, ntile_ref, x_hbm, wg_ref, wu_ref, wd_ref, o_hbm,
                         wg_s, wu_s, wd_s, x_buf, o_buf, in_sem, out_sem):
    e = pl.program_id(0)
    tm = x_buf.shape[1]
    row0 = start_ref[e]
    ntile = ntile_ref[e]
    wg_s[...] = wg_ref[...].astype(BF16)
    wu_s[...] = wu_ref[...].astype(BF16)
    wd_s[...] = wd_ref[...].astype(BF16)

    def rows_of(i):
        return pl.ds(pl.multiple_of(row0 + i * tm, tm), tm)

    def fetch(i, slot):
        return pltpu.make_async_copy(x_hbm.at[rows_of(i)], x_buf.at[slot], in_sem.at[slot])

    def flush(i, slot):
        return pltpu.make_async_copy(o_buf.at[slot], o_hbm.at[rows_of(i)], out_sem.at[slot])

    nslot = x_buf.shape[0]
    for k in range(nslot - 1):
        @pl.when(k < ntile)
        def _(k=k):
            fetch(k, k).start()

    def tile(i, carry):
        slot = i % nslot
        ahead = i + nslot - 1

        @pl.when(ahead < ntile)
        def _():
            fetch(ahead, ahead % nslot).start()

        fetch(i, slot).wait()

        @pl.when(i >= nslot)
        def _():
            flush(i - nslot, slot).wait()

        o_buf[slot] = _pack_halves(_expert_mlp(x_buf[slot], wg_s, wu_s, wd_s).astype(BF16))
        flush(i, slot).start()
        return carry

    lax.fori_loop(0, ntile, tile, 0)

    for k in range(nslot, 0, -1):
        @pl.when(ntile >= k)
        def _(k=k):
            flush(ntile - k, (ntile - k) % nslot).wait()


def _expert_tiles(starts, ntiles, xs, w_g, w_u, w_d):
    rows = xs.shape[0]
    tm = EXPERT_ROW_TILE
    weight = lambda shape: pl.BlockSpec((None,) + shape, lambda e, st, nt: (e, 0, 0))
    grid_spec = pltpu.PrefetchScalarGridSpec(
        num_scalar_prefetch=2,
        grid=(N_EXPERTS,),
        in_specs=[pl.BlockSpec(memory_space=pl.ANY),
                  weight((D_MODEL, D_EXPERT)), weight((D_MODEL, D_EXPERT)), weight((D_EXPERT, D_MODEL))],
        out_specs=pl.BlockSpec(memory_space=pl.ANY),
        scratch_shapes=[pltpu.VMEM((D_MODEL, D_EXPERT), BF16), pltpu.VMEM((D_MODEL, D_EXPERT), BF16),
                        pltpu.VMEM((D_EXPERT, D_MODEL), BF16),
                        pltpu.VMEM((EXPERT_SLOTS, tm, HALF), jnp.int32), pltpu.VMEM((EXPERT_SLOTS, tm, HALF), jnp.int32),
                        pltpu.SemaphoreType.DMA((EXPERT_SLOTS,)), pltpu.SemaphoreType.DMA((EXPERT_SLOTS,))],
    )
    return pl.pallas_call(
        _expert_tiles_kernel,
        grid_spec=grid_spec,
        out_shape=jax.ShapeDtypeStruct((rows, HALF), jnp.int32),
        compiler_params=_cparams("arbitrary"),
        name="expert_tiles",
    )(starts, ntiles, xs, w_g, w_u, w_d)


def _unpack_f32(p):
    return pltpu.bitcast(p & jnp.int32(-65536), F32), pltpu.bitcast(lax.shift_left(p, jnp.int32(16)), F32)


def _combine_kernel(h_ref, o1_ref, o2_ref, rt_ref, gf_ref, y_ref):
    w1, w2 = rt_ref[:, 2:3], rt_ref[:, 3:4]
    a_lo, a_hi = _unpack_f32(o1_ref[...])
    b_lo, b_hi = _unpack_f32(o2_ref[...])
    y_lo = h_ref[:, 0:HALF] + (w1 * a_lo + w2 * b_lo)
    y_hi = h_ref[:, HALF:D_MODEL] + (w1 * a_hi + w2 * b_hi)
    ms = (jnp.sum(y_lo * y_lo, axis=-1, keepdims=True) + jnp.sum(y_hi * y_hi, axis=-1, keepdims=True)) / D_MODEL
    inv = lax.rsqrt(ms + EPS)
    y_ref[:, 0:HALF] = (y_lo * inv) * gf_ref[:, 0:HALF]
    y_ref[:, HALF:D_MODEL] = (y_hi * inv) * gf_ref[:, HALF:D_MODEL]


def _combine(h, o12, route_t, gf, tile):
    n = h.shape[0]
    tile = min(tile, n)
    nt = n // tile
    return pl.pallas_call(
        _combine_kernel,
        grid=(nt,),
        in_specs=[pl.BlockSpec((tile, D_MODEL), lambda i: (i, 0)),
                  pl.BlockSpec((tile, HALF), lambda i: (i, 0)),
                  pl.BlockSpec((tile, HALF), lambda i: (i + nt, 0)),
                  pl.BlockSpec((tile, ROUTE_ROWS), lambda i: (i, 0)),
                  pl.BlockSpec((1, D_MODEL), lambda i: (0, 0))],
        out_specs=pl.BlockSpec((tile, D_MODEL), lambda i: (i, 0)),
        out_shape=jax.ShapeDtypeStruct((n, D_MODEL), F32),
        compiler_params=_cparams("parallel"),
        name="combine_final_norm",
    )(h, o12, o12, route_t, gf)


def _sparse_moe(xn2p, route, h, w_g, w_u, w_d, gf, run_before_experts):
    n = h.shape[0]
    tm = EXPERT_ROW_TILE
    rows = 2 * n + N_EXPERTS * tm
    rank, cnt = _route_rank(route, 1024)
    counts = cnt[:, 0]
    padded = (counts + tm - 1) // tm * tm
    e_idx = jnp.arange(N_EXPERTS, dtype=jnp.int32)
    starts = jnp.sum(jnp.where(e_idx[None, :] < e_idx[:, None], padded[None, :], 0), axis=1)
    ids = route[0:2].astype(jnp.int32)
    start_of = jnp.sum(jnp.where(ids[None] == e_idx[:, None, None], starts[:, None, None], 0), axis=0)
    pos = (start_of + rank[0:2]).reshape(2 * n)
    xs = _sc_scatter_pairs(xn2p, pos, rows)
    xs, _ = lax.optimization_barrier((xs, run_before_experts))
    os_ = _expert_tiles(starts.astype(jnp.int32), (padded // tm).astype(jnp.int32), xs, w_g, w_u, w_d)
    o12 = _sc_gather_rows(os_, pos)
    return _combine(h, o12, route.T, gf, 512)


def _prep_in_weights(w_in):
    o = 0
    w_u = w_in[:, o:o + SSM_WIDTH]; o += SSM_WIDTH
    w_q = w_in[:, o:o + SWA_WIDTH]; o += SWA_WIDTH
    w_k = w_in[:, o:o + SWA_KV_WIDTH]; o += SWA_KV_WIDTH
    w_v = w_in[:, o:o + SWA_KV_WIDTH]; o += SWA_KV_WIDTH
    w_qm = w_in[:, o:o + MEM_WIDTH]; o += MEM_WIDTH
    w_g = w_in[:, o:]
    wq = (w_q * (SWA_HEAD_DIM ** -0.5)).reshape(D_MODEL, SWA_KV_HEADS, SWA_REP, SWA_HEAD_DIM)
    wq = wq.transpose(0, 2, 1, 3).reshape(D_MODEL, SWA_WIDTH)
    w_main = jnp.concatenate([w_u, wq, w_k, w_v, w_qm], axis=1).astype(BF16)
    return w_main, w_g.astype(BF16)


IN_SPLITS = (SSM_WIDTH, SWA_WIDTH, SWA_KV_WIDTH, SWA_KV_WIDTH, MEM_WIDTH)
IN_DTYPES = ((F32, BF16), (BF16,), (F32,), (F32,), (BF16,))


def kernel(x_prompt, x_sample, cache_swa_k, cache_swa_v, state_ssm_re, state_ssm_im, cache_mem_k, cache_mem_v, mem_prompt, norm1_g, w_in, lam_re, lam_im, log_dt, bm_re, bm_im, cm_re, cm_im, d_skip, w_glu, b_glu, sinks, rel_table, mem_norm_g, w_mem_kv, w_br_ssm, w_br_swa, w_br_mem, w_out, norm2_g, w_rg, b_rg, w_rexp, b_rexp, w_e_gate, w_e_up, w_e_down, final_norm_g):
    nb, t, _ = x_prompt.shape
    ns, ts, _ = x_sample.shape
    l = 0
    L = S5_CHUNK

    w_main, w_gates = _prep_in_weights(w_in[l])
    w_swa = (w_br_swa[l].reshape(SWA_KV_HEADS, SWA_REP, SWA_HEAD_DIM, D_MODEL).transpose(1, 0, 2, 3)
             .reshape(SWA_WIDTH, D_MODEL))
    pad_rows = ROUTER_ROWS - N_EXPERTS - N_EXPERT_GROUPS
    w_router = jnp.concatenate([w_rexp[l].T, w_rg[l].T, jnp.zeros((pad_rows, D_MODEL), F32)], axis=0).astype(BF16)
    b_router = jnp.concatenate([b_rexp[l], b_rg[l], jnp.zeros((pad_rows,), F32)]).reshape(ROUTER_ROWS, 1)
    mp = {
        'g1': norm1_g[l].reshape(1, D_MODEL), 'w_gates': w_gates, 'd_skip': d_skip[l].reshape(1, SSM_WIDTH),
        'w_glu': w_glu[l].astype(BF16), 'b_glu': b_glu[l].reshape(1, SSM_WIDTH),
        'w_br_ssm': w_br_ssm[l].astype(BF16), 'w_br_swa': w_swa.astype(BF16),
        'w_br_mem': w_br_mem[l].astype(BF16), 'w_out': w_out[l].astype(BF16),
        'g2': norm2_g[l].reshape(1, D_MODEL), 'w_router': w_router, 'b_router': b_router,
    }
    w_g, w_u, w_d = w_e_gate[l], w_e_up[l], w_e_down[l]
    gf = final_norm_g.reshape(1, D_MODEL)
    s5_w = _s5_weights(lam_re[l], lam_im[l], log_dt[l], bm_re[l], bm_im[l], cm_re[l], cm_im[l], L)

    bias_p = _rel_bias(rel_table, np.arange(WINDOW)[:, None] + WINDOW - np.arange(2 * WINDOW)[None, :])
    keys_s = WINDOW + 2 * ts
    bias_s = _rel_bias(rel_table, np.arange(ts)[:, None] + WINDOW - np.arange(keys_s)[None, :])
    bias_s = bias_s.reshape(SWA_HEADS * ts, keys_s)
    sink_rows = jnp.repeat(sinks[l].astype(F32), ts).reshape(SWA_HEADS * ts, 1)

    n = nb * t
    xp = x_prompt.reshape(n, D_MODEL)
    mk, mv = _norm_proj(mem_prompt.reshape(nb * MEM_TOKENS, D_MODEL), mem_norm_g[l].reshape(1, D_MODEL),
                        w_mem_kv[l].astype(BF16), (MEM_WIDTH, MEM_WIDTH), ((F32,), (F32,)), 512)
    u, ub, qz, k, v, qm = _norm_proj(xp, mp['g1'], w_main, IN_SPLITS, IN_DTYPES, 512)

    y_ssm, fin = _s5(ub, jnp.zeros((nb, N_CH_TILES * 2 * STATE_TILE), F32), s5_w, nb, t // L, L, 64)
    p_re, p_im = _tiles_to_state(fin)

    o_swa = _swa_prompt(qz, k, v, bias_p, sinks[l].astype(F32), nb, t, 4)
    o_mem = _mem_prompt(qm, mk, mv, nb, t, 512)
    h, xn2p, route = _merge(xp, u, y_ssm, o_swa, o_mem, mp, 512)

    k4 = k.reshape(nb, t, SWA_KV_HEADS, SWA_HEAD_DIM)
    v4 = v.reshape(nb, t, SWA_KV_HEADS, SWA_HEAD_DIM)
    new_k_p, new_v_p = k4[:, -WINDOW:][None], v4[:, -WINDOW:][None]
    new_mk = mk.reshape(1, nb, MEM_TOKENS, MEM_HEADS, MEM_HEAD_DIM)
    new_mv = mv.reshape(1, nb, MEM_TOKENS, MEM_HEADS, MEM_HEAD_DIM)

    m = ns * ts
    xs = x_sample.reshape(m, D_MODEL)
    us, ubs, qzs, k_s, v_s, qms = _norm_proj(xs, mp['g1'], w_main, IN_SPLITS, IN_DTYPES, 256)
    ys_ssm, fins = _s5(ubs, _state_to_tiles(state_ssm_re[l], state_ssm_im[l]), s5_w, ns, ts // L, L, 64)
    s_re, s_im = _tiles_to_state(fins)

    kk_all = jnp.concatenate([cache_swa_k[l].reshape(ns, WINDOW, SWA_KV_WIDTH).astype(F32),
                              k_s.reshape(ns, ts, SWA_KV_WIDTH)], axis=1)
    vv_all = jnp.concatenate([cache_swa_v[l].reshape(ns, WINDOW, SWA_KV_WIDTH).astype(F32),
                              v_s.reshape(ns, ts, SWA_KV_WIDTH)], axis=1)
    pad = jnp.zeros((ns, keys_s - WINDOW - ts, SWA_KV_WIDTH), F32)
    q5 = qzs.reshape(ns, ts, SWA_REP, SWA_KV_HEADS, SWA_HEAD_DIM)
    zq = jnp.zeros((ns, ts, SWA_REP, SWA_HEAD_DIM), BF16)
    q_rows = jnp.concatenate([jnp.concatenate([q5[:, :, :, 0], zq], axis=-1),
                              jnp.concatenate([zq, q5[:, :, :, 1]], axis=-1)], axis=2)
    q_rows = q_rows.transpose(0, 2, 1, 3).reshape(ns, SWA_HEADS * ts, LANES)
    o_dec = _swa_decode(q_rows, jnp.concatenate([kk_all, pad], axis=1), jnp.concatenate([vv_all, pad], axis=1),
                        bias_s, sink_rows, ts, 8)
    o_dec = o_dec.reshape(ns, SWA_KV_HEADS, SWA_REP, ts, SWA_KV_HEADS, SWA_HEAD_DIM)
    o_dec = jnp.stack([o_dec[:, g, :, :, g] for g in range(SWA_KV_HEADS)], axis=1)
    o_swa_s = o_dec.transpose(0, 3, 2, 1, 4).reshape(m, SWA_WIDTH).astype(BF16)

    o_mem_s = _mem_decode(qms.astype(F32).reshape(ns, ts, MEM_WIDTH), cache_mem_k, cache_mem_v, l, 8)
    o_mem_s = o_mem_s.reshape(m, MEM_WIDTH).astype(BF16)

    y_prompt = _sparse_moe(xn2p, route, h, w_g, w_u, w_d, gf, (ys_ssm, o_swa_s, o_mem_s)).reshape(nb, t, D_MODEL)
    hs_, xn2ps, routes = _merge(xs, us, ys_ssm, o_swa_s, o_mem_s, mp, 256)
    y_sample = _moe(xn2ps, routes.T, w_g, w_u, w_d, hs_, gf, 1024).reshape(ns, ts, D_MODEL)

    roll = lambda cache, new: jnp.concatenate(
        [cache[:, :, ts:], new.reshape(1, ns, ts, SWA_KV_HEADS, SWA_HEAD_DIM).astype(cache.dtype)], axis=2)
    new_k_s, new_v_s = roll(cache_swa_k, k_s), roll(cache_swa_v, v_s)

    return (y_prompt, y_sample,
            new_k_p, new_v_p, p_re[None], p_im[None], new_mk, new_mv,
            new_k_s, new_v_s, s_re[None].astype(state_ssm_re.dtype), s_im[None].astype(state_ssm_im.dtype))
```

```python
import functools
import math

import numpy as np
import jax
import jax.numpy as jnp
from jax import lax
from jax.experimental import pallas as pl
from jax.experimental.pallas import tpu as pltpu
from jax.experimental.pallas import tpu_sc as plsc

F32 = jnp.float32
BF16 = jnp.bfloat16

D_MODEL = 1024
SSM_WIDTH = 512
SSM_GROUP = 16
SSM_GROUPS = 32
SSM_STATE = 64
SWA_HEADS = 8
SWA_KV_HEADS = 2
SWA_REP = 4
SWA_HEAD_DIM = 64
SWA_WIDTH = 512
SWA_KV_WIDTH = 128
WINDOW = 128
REL_BUCKETS = 32
REL_MAX_DIST = 128
MEM_TOKENS = 256
MEM_HEADS = 4
MEM_HEAD_DIM = 128
MEM_WIDTH = 512
N_EXPERT_GROUPS = 4
EXPERTS_PER_GROUP = 8
N_EXPERTS = 32
D_EXPERT = 256
EPS = 1e-6
NEG_INF = -1e30

LANES = 128
GROUPS_PER_TILE = LANES // SSM_GROUP
N_CH_TILES = SSM_WIDTH // LANES
STATE_TILE = GROUPS_PER_TILE * SSM_STATE
VMEM_LIMIT = 56 * 1024 * 1024
S5_CHUNK = 8
S5_PANEL = 256

_TRANS_B = (((1,), (1,)), ((), ()))


def _cparams(*sem):
    return pltpu.CompilerParams(dimension_semantics=sem, vmem_limit_bytes=VMEM_LIMIT)


def _rms(x, g):
    return (x * lax.rsqrt(jnp.mean(x * x, axis=-1, keepdims=True) + EPS)) * g


def _dot(a, b):
    return jnp.dot(a, b, preferred_element_type=F32)


def _norm_proj_kernel(x_ref, g_ref, w_ref, *out_refs, splits, dtypes):
    xb = _rms(x_ref[...], g_ref[...]).astype(BF16)
    off = 0
    outs = iter(out_refs)
    for width, dts in zip(splits, dtypes):
        r = _dot(xb, w_ref[:, off:off + width])
        for dt in dts:
            next(outs)[...] = r.astype(dt)
        off += width


def _norm_proj(x, g, w, splits, dtypes, tile):
    n, d = x.shape
    tile = min(tile, n)
    flat = [(wd, dt) for wd, dts in zip(splits, dtypes) for dt in dts]
    return pl.pallas_call(
        functools.partial(_norm_proj_kernel, splits=tuple(splits), dtypes=tuple(dtypes)),
        grid=(n // tile,),
        in_specs=[pl.BlockSpec((tile, d), lambda i: (i, 0)),
                  pl.BlockSpec((1, d), lambda i: (0, 0)),
                  pl.BlockSpec((d, sum(splits)), lambda i: (0, 0))],
        out_specs=[pl.BlockSpec((tile, wd), lambda i: (i, 0)) for wd, _ in flat],
        out_shape=[jax.ShapeDtypeStruct((n, wd), dt) for wd, dt in flat],
        compiler_params=_cparams("parallel"),
        name="norm_proj",
    )(x, g, w)


def _s5_weights(lam_re, lam_im, log_dt, bm_re, bm_im, cm_re, cm_im, L):
    hp = lax.Precision.HIGHEST
    nt, gt, P, H = N_CH_TILES, GROUPS_PER_TILE, SSM_STATE, SSM_GROUP
    lr, li = lam_re.astype(F32), lam_im.astype(F32)
    dt = jnp.exp(log_dt.astype(F32))[:, None]
    mag = jnp.exp(lr * dt)
    a_re = mag * jnp.cos(li * dt)
    a_im = mag * jnp.sin(li * dt)
    den = lr * lr + li * li
    f_re = ((a_re - 1.0) * lr + a_im * li) / den
    f_im = (a_im * lr - (a_re - 1.0) * li) / den
    br, bi = bm_re.astype(F32), bm_im.astype(F32)
    bb_re = f_re[..., None] * br - f_im[..., None] * bi
    bb_im = f_re[..., None] * bi + f_im[..., None] * br
    pr, pi = [jnp.ones_like(a_re)], [jnp.zeros_like(a_im)]
    for _ in range(L):
        pr.append(pr[-1] * a_re - pi[-1] * a_im)
        pi.append(pr[-2] * a_im + pi[-1] * a_re)
    ap_re, ap_im = jnp.stack(pr), jnp.stack(pi)
    cr, ci = cm_re.astype(F32), cm_im.astype(F32)
    ca_re = cr[None] * ap_re[:, :, None, :] - ci[None] * ap_im[:, :, None, :]
    ca_im = cr[None] * ap_im[:, :, None, :] + ci[None] * ap_re[:, :, None, :]

    rev_re = jnp.stack([pr[L - 1 - s] for s in range(L)])
    rev_im = jnp.stack([pi[L - 1 - s] for s in range(L)])
    ws_re = rev_re[..., None] * bb_re[None] - rev_im[..., None] * bb_im[None]
    ws_im = rev_re[..., None] * bb_im[None] + rev_im[..., None] * bb_re[None]
    c_st = jnp.concatenate([ws_re.transpose(0, 1, 3, 2).reshape(L, nt, gt * H, P),
                            ws_im.transpose(0, 1, 3, 2).reshape(L, nt, gt * H, P)], axis=3).transpose(1, 0, 2, 3)
    so = lambda ca: ca[1:].transpose(1, 3, 0, 2).reshape(nt, gt * P, L * H)
    c_so = jnp.concatenate([so(ca_re), so(-ca_im)], axis=1)
    k_lag = (jnp.einsum('tghp,gpk->gkth', ca_re[:L], bb_re, precision=hp)
             - jnp.einsum('tghp,gpk->gkth', ca_im[:L], bb_im, precision=hp))
    c_k = k_lag.reshape(nt, gt * H, L * H)
    w_st, w_out, toep = _s5_expand(c_st, c_so, c_k, L)

    def per_tile(v):
        return v.reshape(nt, 1, STATE_TILE)

    return w_st, w_out, toep, per_tile(pr[L]), per_tile(pi[L])


def _s5_expand_kernel(cst_ref, cso_ref, ck_ref, wst_ref, wso_ref, toep_ref, *, L):
    hp = lax.Precision.HIGHEST
    P, H = SSM_STATE, SSM_GROUP
    iota = lambda shape, d: lax.broadcasted_iota(jnp.int32, shape, d)
    one = lambda cond: jnp.where(cond, 1.0, 0.0).astype(F32)

    r, c = iota((2 * P, 2 * STATE_TILE), 0), iota((2 * P, 2 * STATE_TILE), 1)
    rep_st = one((r // P == c // STATE_TILE) & (r % P == c % P))
    r, c = iota((LANES, 2 * STATE_TILE), 0), iota((LANES, 2 * STATE_TILE), 1)
    own_st = one(r // H == (c % STATE_TILE) // P)
    for s in range(L):
        blk = jnp.dot(cst_ref[s], rep_st, precision=hp, preferred_element_type=F32) * own_st
        wst_ref[s * LANES:(s + 1) * LANES, :] = blk.astype(BF16)

    r, c = iota((LANES, LANES), 0), iota((LANES, LANES), 1)
    pick = [one((r // H == t) & (r % H == c % H)) for t in range(L)]
    own_k = one(r // H == c // H)
    r, c = iota((2 * STATE_TILE, LANES), 0), iota((2 * STATE_TILE, LANES), 1)
    own_so = one((r % STATE_TILE) // P == c // H)
    cso = cso_ref[...]
    for t in range(L):
        blk = jnp.dot(cso, pick[t], precision=hp, preferred_element_type=F32) * own_so
        wso_ref[:, t * LANES:(t + 1) * LANES] = blk.astype(BF16)
    ck = ck_ref[...]
    lag = [(jnp.dot(ck, pick[t], precision=hp, preferred_element_type=F32) * own_k).astype(BF16) for t in range(L)]
    zero = jnp.zeros((LANES, LANES), BF16)
    for s in range(L):
        for t in range(L):
            toep_ref[s * LANES:(s + 1) * LANES, t * LANES:(t + 1) * LANES] = lag[t - s] if t >= s else zero


def _s5_expand(c_st, c_so, c_k, L):
    lk = L * LANES
    st2 = 2 * STATE_TILE
    return pl.pallas_call(
        functools.partial(_s5_expand_kernel, L=L),
        grid=(N_CH_TILES,),
        in_specs=[pl.BlockSpec((None, L, LANES, 2 * SSM_STATE), lambda j: (j, 0, 0, 0)),
                  pl.BlockSpec((None, st2, L * SSM_GROUP), lambda j: (j, 0, 0)),
                  pl.BlockSpec((None, LANES, L * SSM_GROUP), lambda j: (j, 0, 0))],
        out_specs=[pl.BlockSpec((None, lk, st2), lambda j: (j, 0, 0)),
                   pl.BlockSpec((None, st2, lk), lambda j: (j, 0, 0)),
                   pl.BlockSpec((None, lk, lk), lambda j: (j, 0, 0))],
        out_shape=[jax.ShapeDtypeStruct((N_CH_TILES, lk, st2), BF16),
                   jax.ShapeDtypeStruct((N_CH_TILES, st2, lk), BF16),
                   jax.ShapeDtypeStruct((N_CH_TILES, lk, lk), BF16)],
        compiler_params=_cparams("parallel"),
        name="s5_expand_weights",
    )(c_st, c_so, c_k)


def _to_chunks(u, nb, nc, L):
    return (u.reshape(nb, nc, L, N_CH_TILES, LANES).transpose(1, 0, 3, 2, 4)
            .reshape(nc * nb, N_CH_TILES * L * LANES))


def _from_chunks(y, nb, nc, L):
    return (y.reshape(nc, nb, N_CH_TILES, L, LANES).transpose(1, 0, 3, 2, 4)
            .reshape(nb * nc * L, SSM_WIDTH))


def _s5_kernel(x_ref, h0_ref, are_ref, aim_ref, ws_ref, t_ref, wo_ref, y_ref, fin_ref,
               hr_ref, hi_ref, d_ref, hs_ref, *, cb, nb):
    ci = pl.program_id(1)

    @pl.when(ci == 0)
    def _():
        hr_ref[...] = h0_ref[:, 0:STATE_TILE]
        hi_ref[...] = h0_ref[:, STATE_TILE:2 * STATE_TILE]

    x = x_ref[...]
    d_ref[...] = _dot(x, ws_ref[...])
    ar = jnp.broadcast_to(are_ref[...], (nb, STATE_TILE))
    ai = jnp.broadcast_to(aim_ref[...], (nb, STATE_TILE))

    def body(c, carry):
        hr, hi = carry
        r0 = pl.multiple_of(c * nb, nb)
        hs_ref[pl.ds(r0, nb), 0:STATE_TILE] = hr
        hs_ref[pl.ds(r0, nb), STATE_TILE:2 * STATE_TILE] = hi
        d = d_ref[pl.ds(r0, nb), :]
        return (ar * hr - ai * hi + d[:, 0:STATE_TILE],
                ar * hi + ai * hr + d[:, STATE_TILE:2 * STATE_TILE])

    hr, hi = lax.fori_loop(0, cb, body, (hr_ref[...], hi_ref[...]))
    hr_ref[...] = hr
    hi_ref[...] = hi
    hsb = hs_ref[...].astype(BF16)
    for c0 in range(0, t_ref.shape[1], S5_PANEL):
        c1 = c0 + S5_PANEL
        y_ref[:, c0:c1] = _dot(x[:, 0:c1], t_ref[0:c1, c0:c1]) + _dot(hsb, wo_ref[:, c0:c1])

    @pl.when(ci == pl.num_programs(1) - 1)
    def _():
        fin_ref[:, 0:STATE_TILE] = hr
        fin_ref[:, STATE_TILE:2 * STATE_TILE] = hi


def _s5(ub, h0, weights, nb, nc, L, chunk_block):
    w_st, w_so, toep, a_re, a_im = weights
    xc = _to_chunks(ub, nb, nc, L)
    cb = min(chunk_block, nc)
    rows = cb * nb
    lk = L * LANES
    st2 = 2 * STATE_TILE
    tile_w = lambda shape: pl.BlockSpec((None,) + shape, lambda j, c: (j, 0, 0))
    y, fin = pl.pallas_call(
        functools.partial(_s5_kernel, cb=cb, nb=nb),
        grid=(N_CH_TILES, nc // cb),
        in_specs=[pl.BlockSpec((rows, lk), lambda j, c: (c, j)),
                  pl.BlockSpec((nb, st2), lambda j, c: (0, j)),
                  tile_w((1, STATE_TILE)), tile_w((1, STATE_TILE)),
                  tile_w((lk, st2)), tile_w((lk, lk)), tile_w((st2, lk))],
        out_specs=[pl.BlockSpec((rows, lk), lambda j, c: (c, j)),
                   pl.BlockSpec((nb, st2), lambda j, c: (0, j))],
        out_shape=[jax.ShapeDtypeStruct((nc * nb, N_CH_TILES * lk), F32),
                   jax.ShapeDtypeStruct((nb, N_CH_TILES * st2), F32)],
        scratch_shapes=[pltpu.VMEM((nb, STATE_TILE), F32), pltpu.VMEM((nb, STATE_TILE), F32),
                        pltpu.VMEM((rows, st2), F32), pltpu.VMEM((rows, st2), F32)],
        compiler_params=_cparams("parallel", "arbitrary"),
        name="s5_chunked_scan",
    )(xc, h0, a_re, a_im, w_st, toep, w_so)
    return _from_chunks(y, nb, nc, L), fin


def _state_to_tiles(h_re, h_im):
    nb = h_re.shape[0]
    r = h_re.astype(F32).reshape(nb, N_CH_TILES, STATE_TILE)
    i = h_im.astype(F32).reshape(nb, N_CH_TILES, STATE_TILE)
    return jnp.concatenate([r, i], axis=-1).reshape(nb, N_CH_TILES * 2 * STATE_TILE)


def _tiles_to_state(h):
    nb = h.shape[0]
    h = h.reshape(nb, N_CH_TILES, 2, GROUPS_PER_TILE, SSM_STATE)
    return (h[:, :, 0].reshape(nb, SSM_GROUPS, SSM_STATE), h[:, :, 1].reshape(nb, SSM_GROUPS, SSM_STATE))


def _t5_bucket(dist):
    n = np.maximum(dist, 0)
    max_exact = REL_BUCKETS // 2
    nf = np.maximum(n, 1).astype(np.float32)
    large = max_exact + (np.log(nf / np.float32(max_exact)) / np.float32(math.log(REL_MAX_DIST / max_exact))
                         * np.float32(REL_BUCKETS - max_exact)).astype(np.int32)
    large = np.minimum(large, REL_BUCKETS - 1)
    return np.where(n < max_exact, n, large)


def _rel_bias(rel_table, dist):
    bucket = _t5_bucket(dist)
    tab = rel_table.astype(F32)
    out = jnp.zeros((SWA_HEADS,) + dist.shape, F32)
    for b in range(REL_BUCKETS):
        sel = jnp.asarray(bucket == b)
        if bool((bucket == b).any()):
            out = jnp.where(sel[None], tab[b].reshape((SWA_HEADS,) + (1,) * dist.ndim), out)
    return out


def _swa_prompt_kernel(sink_ref, q_ref, kp_ref, kc_ref, vp_ref, vc_ref, bias_ref, o_ref, kk_ref, vv_ref, *, qblocks):
    step = pl.program_id(1)
    kk_ref[0:WINDOW, :] = kp_ref[...].astype(BF16)
    kk_ref[WINDOW:, :] = kc_ref[...].astype(BF16)
    vv_ref[0:WINDOW, :] = vp_ref[...].astype(BF16)
    vv_ref[WINDOW:, :] = vc_ref[...].astype(BF16)
    row = lax.broadcasted_iota(jnp.int32, (WINDOW, 2 * WINDOW), 0)
    col = lax.broadcasted_iota(jnp.int32, (WINDOW, 2 * WINDOW), 1)
    dist = row + WINDOW - col
    band = (dist >= 0) & (dist < WINDOW)
    lane = lax.broadcasted_iota(jnp.int32, (WINDOW, LANES), 1)
    low = lane < SWA_HEAD_DIM

    def block(j, carry):
        r0 = pl.multiple_of(j * WINDOW, WINDOW)
        kk = kk_ref[pl.ds(r0, 2 * WINDOW), :]
        vv = vv_ref[pl.ds(r0, 2 * WINDOW), :]
        valid = band & ((col >= WINDOW) | (step * qblocks + j > 0))
        for t in range(SWA_REP):
            q2 = q_ref[pl.ds(r0, WINDOW), t * LANES:(t + 1) * LANES]
            outs = []
            for half in range(SWA_KV_HEADS):
                h = t + SWA_REP * half
                qh = jnp.where(low if half == 0 else jnp.logical_not(low), q2, jnp.zeros_like(q2))
                s = lax.dot_general(qh, kk, _TRANS_B, preferred_element_type=F32)
                s = jnp.where(valid, s + bias_ref[h], NEG_INF)
                sink = sink_ref[h]
                m = jnp.maximum(jnp.max(s, axis=-1, keepdims=True), sink)
                e = jnp.exp(s - m)
                den = jnp.sum(e, axis=-1, keepdims=True) + jnp.exp(sink - m)
                outs.append(_dot(e.astype(BF16), vv) * (1.0 / den))
            o_ref[pl.ds(r0, WINDOW), t * LANES:(t + 1) * LANES] = jnp.where(low, outs[0], outs[1]).astype(BF16)
        return carry

    lax.fori_loop(0, qblocks, block, 0)


def _swa_prompt(q, k, v, bias, sinks, nb, t, qblocks):
    nstep = t // (WINDOW * qblocks)
    rows = WINDOW * qblocks
    cur = lambda b, i: (b * nstep + i, 0)
    prev = lambda b, i: (b * nstep * qblocks + jnp.maximum(i * qblocks - 1, 0), 0)
    return pl.pallas_call(
        functools.partial(_swa_prompt_kernel, qblocks=qblocks),
        grid=(nb, nstep),
        in_specs=[pl.BlockSpec(memory_space=pltpu.SMEM),
                  pl.BlockSpec((rows, SWA_WIDTH), cur),
                  pl.BlockSpec((WINDOW, SWA_KV_WIDTH), prev),
                  pl.BlockSpec((rows, SWA_KV_WIDTH), cur),
                  pl.BlockSpec((WINDOW, SWA_KV_WIDTH), prev),
                  pl.BlockSpec((rows, SWA_KV_WIDTH), cur),
                  pl.BlockSpec((SWA_HEADS, WINDOW, 2 * WINDOW), lambda b, i: (0, 0, 0))],
        out_specs=pl.BlockSpec((rows, SWA_WIDTH), cur),
        out_shape=jax.ShapeDtypeStruct((nb * t, SWA_WIDTH), BF16),
        scratch_shapes=[pltpu.VMEM((rows + WINDOW, SWA_KV_WIDTH), BF16),
                        pltpu.VMEM((rows + WINDOW, SWA_KV_WIDTH), BF16)],
        compiler_params=_cparams("parallel", "parallel"),
        name="swa_prompt",
    )(sinks, q, k, k, v, v, bias)


def _swa_decode_kernel(q_ref, k_ref, v_ref, bias_ref, sink_ref, o_ref, *, seqs, tq):
    rows, keys = q_ref.shape[1], k_ref.shape[1]
    qi = lax.broadcasted_iota(jnp.int32, (rows, keys), 0) % tq
    col = lax.broadcasted_iota(jnp.int32, (rows, keys), 1)
    dist = qi + WINDOW - col
    valid = (dist >= 0) & (dist < WINDOW)
    bias = bias_ref[...]
    sink = sink_ref[...]
    for s_i in range(seqs):
        kk = k_ref[s_i].astype(BF16)
        s = lax.dot_general(q_ref[s_i], kk, _TRANS_B, preferred_element_type=F32)
        s = jnp.where(valid, s + bias, NEG_INF)
        m = jnp.maximum(jnp.max(s, axis=-1, keepdims=True), sink)
        e = jnp.exp(s - m)
        den = jnp.sum(e, axis=-1, keepdims=True) + jnp.exp(sink - m)
        o_ref[s_i] = _dot(e.astype(BF16), v_ref[s_i].astype(BF16)) * (1.0 / den)


def _swa_decode(qz, k_all, v_all, bias, sink_rows, tq, seqs):
    nseq, rows, _ = qz.shape
    keys = k_all.shape[1]
    seqs = min(seqs, nseq)
    return pl.pallas_call(
        functools.partial(_swa_decode_kernel, seqs=seqs, tq=tq),
        grid=(nseq // seqs,),
        in_specs=[pl.BlockSpec((seqs, rows, LANES), lambda i: (i, 0, 0)),
                  pl.BlockSpec((seqs, keys, LANES), lambda i: (i, 0, 0)),
                  pl.BlockSpec((seqs, keys, LANES), lambda i: (i, 0, 0)),
                  pl.BlockSpec((rows, keys), lambda i: (0, 0)),
                  pl.BlockSpec((rows, 1), lambda i: (0, 0))],
        out_specs=pl.BlockSpec((seqs, rows, LANES), lambda i: (i, 0, 0)),
        out_shape=jax.ShapeDtypeStruct((nseq, rows, LANES), F32),
        compiler_params=_cparams("parallel"),
        name="swa_decode",
    )(qz, k_all, v_all, bias, sink_rows)


def _softmax(s):
    m = jnp.max(s, axis=-1, keepdims=True)
    e = jnp.exp(s - m)
    return e * (1.0 / jnp.sum(e, axis=-1, keepdims=True))


def _mem_prompt_kernel(q_ref, k_ref, v_ref, o_ref, s_ref, p_ref):
    scale = MEM_HEAD_DIM ** -0.5
    heads = [slice(h * MEM_HEAD_DIM, (h + 1) * MEM_HEAD_DIM) for h in range(MEM_HEADS)]
    for h, sl in enumerate(heads):
        s_ref[h] = lax.dot_general(q_ref[:, sl], k_ref[:, sl].astype(BF16), _TRANS_B, preferred_element_type=F32)
    s = s_ref[...] * scale
    e = jnp.exp(s - jnp.max(s, axis=-1, keepdims=True))
    p_ref[...] = e.astype(BF16)
    inv = 1.0 / jnp.sum(e, axis=-1, keepdims=True)
    for h, sl in enumerate(heads):
        o_ref[:, sl] = (_dot(p_ref[h], v_ref[:, sl].astype(BF16)) * inv[h]).astype(BF16)


def _mem_prompt(qm, mk, mv, nb, t, tile):
    tile = min(tile, t)
    nt = t // tile
    return pl.pallas_call(
        _mem_prompt_kernel,
        grid=(nb, nt),
        in_specs=[pl.BlockSpec((tile, MEM_WIDTH), lambda b, i: (b * nt + i, 0)),
                  pl.BlockSpec((MEM_TOKENS, MEM_WIDTH), lambda b, i: (b, 0)),
                  pl.BlockSpec((MEM_TOKENS, MEM_WIDTH), lambda b, i: (b, 0))],
        out_specs=pl.BlockSpec((tile, MEM_WIDTH), lambda b, i: (b * nt + i, 0)),
        out_shape=jax.ShapeDtypeStruct((nb * t, MEM_WIDTH), BF16),
        scratch_shapes=[pltpu.VMEM((MEM_HEADS, tile, MEM_TOKENS), F32), pltpu.VMEM((MEM_HEADS, tile, MEM_TOKENS), BF16)],
        compiler_params=_cparams("parallel", "parallel"),
        name="mem_prompt",
    )(qm, mk, mv)


def _mem_decode_kernel(q_ref, k_ref, v_ref, o_ref, *, seqs):
    tq = q_ref.shape[1]
    rows, cols = MEM_HEADS * tq, MEM_TOKENS * MEM_HEADS
    k2 = k_ref.reshape(seqs, cols, MEM_HEAD_DIM)
    v2 = v_ref.reshape(seqs, cols, MEM_HEAD_DIM)
    scale = MEM_HEAD_DIM ** -0.5
    own = (lax.broadcasted_iota(jnp.int32, (rows, cols), 1) % MEM_HEADS
           == lax.broadcasted_iota(jnp.int32, (rows, cols), 0) // tq)
    for s_i in range(seqs):
        q = q_ref[s_i]
        qb = jnp.concatenate([q[:, h * MEM_HEAD_DIM:(h + 1) * MEM_HEAD_DIM] for h in range(MEM_HEADS)], axis=0)
        s = lax.dot_general(qb.astype(BF16), k2[s_i].astype(BF16), _TRANS_B, preferred_element_type=F32) * scale
        p = _softmax(jnp.where(own, s, NEG_INF)).astype(BF16)
        o = _dot(p, v2[s_i].astype(BF16))
        for h in range(MEM_HEADS):
            o_ref[s_i, :, h * MEM_HEAD_DIM:(h + 1) * MEM_HEAD_DIM] = o[h * tq:(h + 1) * tq, :]


def _mem_decode(q, k, v, layer, seqs):
    nseq, tq, _ = q.shape
    seqs = min(seqs, nseq)
    cache = pl.BlockSpec((None, seqs, MEM_TOKENS, MEM_HEADS, MEM_HEAD_DIM), lambda i: (layer, i, 0, 0, 0))
    return pl.pallas_call(
        functools.partial(_mem_decode_kernel, seqs=seqs),
        grid=(nseq // seqs,),
        in_specs=[pl.BlockSpec((seqs, tq, MEM_WIDTH), lambda i: (i, 0, 0)), cache, cache],
        out_specs=pl.BlockSpec((seqs, tq, MEM_WIDTH), lambda i: (i, 0, 0)),
        out_shape=jax.ShapeDtypeStruct((nseq, tq, MEM_WIDTH), F32),
        compiler_params=_cparams("parallel"),
        name="mem_decode",
    )(q, k, v)


ROUTER_ROWS = 40
ROUTE_ROWS = 8
HALF = D_MODEL // 2


def _pack_halves(xb):
    hi = pltpu.bitcast(xb[:, 0:HALF].astype(F32), jnp.int32)
    lo = pltpu.bitcast(xb[:, HALF:D_MODEL].astype(F32), jnp.int32)
    return hi | lax.shift_right_logical(lo, jnp.int32(16))


def _unpack_halves(p):
    hi = pltpu.bitcast(p & jnp.int32(-65536), F32).astype(BF16)
    lo = pltpu.bitcast(lax.shift_left(p, jnp.int32(16)), F32).astype(BF16)
    return hi, lo


def _merge_kernel(x_ref, u_ref, y_ref, os_ref, om_ref, g1_ref, wg_ref, dsk_ref, wglu_ref, bglu_ref,
                  wbs_ref, wbw_ref, wbm_ref, wout_ref, g2_ref, wr_ref, br_ref,
                  h_ref, xn2_ref, route_ref):
    x = x_ref[...]
    tt = x.shape[0]
    xb = _rms(x, g1_ref[...]).astype(BF16)
    z = jax.nn.gelu(y_ref[...] + dsk_ref[...] * u_ref[...])
    z = z * jax.nn.sigmoid(_dot(z.astype(BF16), wglu_ref[...]) + bglu_ref[...])
    merged = jax.nn.sigmoid(_dot(xb, wg_ref[:, 0:D_MODEL])) * _dot(z.astype(BF16), wbs_ref[...])
    merged = merged + jax.nn.sigmoid(_dot(xb, wg_ref[:, D_MODEL:2 * D_MODEL])) * _dot(os_ref[...], wbw_ref[...])
    merged = merged + jax.nn.sigmoid(_dot(xb, wg_ref[:, 2 * D_MODEL:3 * D_MODEL])) * _dot(om_ref[...], wbm_ref[...])
    h = x + _dot(merged.astype(BF16), wout_ref[...])
    h_ref[...] = h
    xn2 = _rms(h, g2_ref[...]).astype(BF16)
    xn2_ref[...] = _pack_halves(xn2)

    lt = lax.dot_general(wr_ref[...], xn2, _TRANS_B, preferred_element_type=F32) + br_ref[...]
    gl = lt[N_EXPERTS:N_EXPERTS + N_EXPERT_GROUPS]
    ge = jnp.exp(gl - jnp.max(gl, axis=0, keepdims=True))
    gp = ge / jnp.sum(ge, axis=0, keepdims=True)
    gw = jnp.max(gp, axis=0, keepdims=True)
    gidx = jnp.full((1, tt), N_EXPERT_GROUPS - 1, jnp.int32)
    for r in range(N_EXPERT_GROUPS - 2, -1, -1):
        gidx = jnp.where(gp[r:r + 1] == gw, r, gidx)
    ein = lt[(N_EXPERT_GROUPS - 1) * EXPERTS_PER_GROUP:N_EXPERTS]
    for r in range(N_EXPERT_GROUPS - 2, -1, -1):
        ein = jnp.where(gidx == r, lt[r * EXPERTS_PER_GROUP:(r + 1) * EXPERTS_PER_GROUP], ein)
    ee = jnp.exp(ein - jnp.max(ein, axis=0, keepdims=True))
    ep = ee / jnp.sum(ee, axis=0, keepdims=True)
    rowi = lax.broadcasted_iota(jnp.int32, (EXPERTS_PER_GROUP, tt), 0)
    p1 = jnp.max(ep, axis=0, keepdims=True)
    e1 = jnp.min(jnp.where(ep == p1, rowi, EXPERTS_PER_GROUP), axis=0, keepdims=True)
    ep2 = jnp.where(rowi == e1, -1.0, ep)
    p2 = jnp.max(ep2, axis=0, keepdims=True)
    e2 = jnp.min(jnp.where(ep2 == p2, rowi, EXPERTS_PER_GROUP), axis=0, keepdims=True)
    tot = p1 + p2
    w1 = p1 / tot * gw
    w2 = p2 / tot * gw
    id1 = (gidx * EXPERTS_PER_GROUP + e1).astype(F32)
    id2 = (gidx * EXPERTS_PER_GROUP + e2).astype(F32)
    route_ref[...] = jnp.concatenate([id1, id2, w1, w2, jnp.zeros((ROUTE_ROWS - 4, tt), F32)], axis=0)


def _merge(x, u, y, o_swa, o_mem, p, tile):
    n = x.shape[0]
    tile = min(tile, n)
    row = lambda i: (i, 0)
    const = lambda i: (0, 0)
    full = lambda a: pl.BlockSpec(a.shape, const, pipeline_mode=pl.Buffered(1))
    weights = [p['g1'], p['w_gates'], p['d_skip'], p['w_glu'], p['b_glu'], p['w_br_ssm'], p['w_br_swa'],
               p['w_br_mem'], p['w_out'], p['g2'], p['w_router'], p['b_router']]
    return pl.pallas_call(
        _merge_kernel,
        grid=(n // tile,),
        in_specs=[pl.BlockSpec((tile, D_MODEL), row), pl.BlockSpec((tile, SSM_WIDTH), row),
                  pl.BlockSpec((tile, SSM_WIDTH), row), pl.BlockSpec((tile, SWA_WIDTH), row),
                  pl.BlockSpec((tile, MEM_WIDTH), row)] + [full(w) for w in weights],
        out_specs=[pl.BlockSpec((tile, D_MODEL), row), pl.BlockSpec((tile, HALF), row),
                   pl.BlockSpec((ROUTE_ROWS, tile), lambda i: (0, i))],
        out_shape=[jax.ShapeDtypeStruct((n, D_MODEL), F32), jax.ShapeDtypeStruct((n, HALF), jnp.int32),
                   jax.ShapeDtypeStruct((ROUTE_ROWS, n), F32)],
        compiler_params=_cparams("parallel"),
        name="merge_router",
    )(x, u, y, o_swa, o_mem, *weights)


def _expert_mlp(xp, wg, wu, wd):
    hi, lo = _unpack_halves(xp)
    g = _dot(hi, wg[0:HALF, :]) + _dot(lo, wg[HALF:D_MODEL, :])
    u = _dot(hi, wu[0:HALF, :]) + _dot(lo, wu[HALF:D_MODEL, :])
    hh = jax.nn.silu(g) * u
    return _dot(hh.astype(BF16), wd[...])


def _moe_kernel(xn2_ref, rt_ref, wg_ref, wu_ref, wd_ref, h_ref, gf_ref, o_ref, acc_ref):
    e = pl.program_id(1)

    @pl.when(e == 0)
    def _():
        acc_ref[...] = jnp.zeros_like(acc_ref)

    o = _expert_mlp(xn2_ref[...], wg_ref[...].astype(BF16), wu_ref[...].astype(BF16), wd_ref[...].astype(BF16))
    ef = e.astype(F32)
    c = (jnp.where(rt_ref[:, 0:1] == ef, rt_ref[:, 2:3], 0.0)
         + jnp.where(rt_ref[:, 1:2] == ef, rt_ref[:, 3:4], 0.0))
    acc_ref[...] += c * o

    @pl.when(e == N_EXPERTS - 1)
    def _():
        o_ref[...] = _rms(h_ref[...] + acc_ref[...], gf_ref[...])


def _moe(xn2, route_t, w_g, w_u, w_d, h, gf, tile):
    n = h.shape[0]
    tile = min(tile, n)
    return pl.pallas_call(
        _moe_kernel,
        grid=(n // tile, N_EXPERTS),
        in_specs=[pl.BlockSpec((tile, HALF), lambda i, e: (i, 0)),
                  pl.BlockSpec((tile, ROUTE_ROWS), lambda i, e: (i, 0)),
                  pl.BlockSpec((None, D_MODEL, D_EXPERT), lambda i, e: (e, 0, 0)),
                  pl.BlockSpec((None, D_MODEL, D_EXPERT), lambda i, e: (e, 0, 0)),
                  pl.BlockSpec((None, D_EXPERT, D_MODEL), lambda i, e: (e, 0, 0)),
                  pl.BlockSpec((tile, D_MODEL), lambda i, e: (i, 0)),
                  pl.BlockSpec((1, D_MODEL), lambda i, e: (0, 0))],
        out_specs=pl.BlockSpec((tile, D_MODEL), lambda i, e: (i, 0)),
        out_shape=jax.ShapeDtypeStruct((n, D_MODEL), F32),
        scratch_shapes=[pltpu.VMEM((tile, D_MODEL), F32)],
        compiler_params=_cparams("parallel", "arbitrary"),
        name="moe_final_norm",
    )(xn2, route_t, w_g, w_u, w_d, h, gf)


EXPERT_ROW_TILE = 256
EXPERT_SLOTS = 4
SC_CORES = 2
SC_SUBCORES = 16
SC_WORKERS = SC_CORES * SC_SUBCORES
SC_SCATTER_ROWS = 64
SC_GATHER_ROWS = 64


def _route_rank_kernel(r_ref, rank_ref, cnt_ref, base_ref):
    i = pl.program_id(0)
    tt = r_ref.shape[1]

    @pl.when(i == 0)
    def _():
        base_ref[...] = jnp.zeros_like(base_ref)

    ids = r_ref[0:2, :].astype(jnp.int32)
    e_iota = lax.broadcasted_iota(jnp.int32, (N_EXPERTS, tt), 0)
    oh1 = jnp.where(e_iota == ids[0:1], 1.0, 0.0)
    oh2 = jnp.where(e_iota == ids[1:2], 1.0, 0.0)
    before = (lax.broadcasted_iota(jnp.int32, (tt, tt), 0) < lax.broadcasted_iota(jnp.int32, (tt, tt), 1))
    tri = jnp.where(before, 1.0, 0.0).astype(BF16)
    c1 = _dot(oh1.astype(BF16), tri)
    c2 = _dot(oh2.astype(BF16), tri)
    tot1 = jnp.sum(oh1, axis=1, keepdims=True)
    tot2 = jnp.sum(oh2, axis=1, keepdims=True)
    base = base_ref[:, 0:1]
    rank1 = jnp.sum(oh1 * (base + c1), axis=0, keepdims=True)
    rank2 = jnp.sum(oh2 * (base + tot1 + c2), axis=0, keepdims=True)
    rank_ref[...] = jnp.concatenate([rank1, rank2, jnp.zeros((ROUTE_ROWS - 2, tt), F32)], axis=0).astype(jnp.int32)
    new_base = jnp.broadcast_to(base + tot1 + tot2, base_ref.shape)
    base_ref[...] = new_base
    cnt_ref[...] = new_base.astype(jnp.int32)


def _route_rank(route, tile):
    n = route.shape[1]
    tile = min(tile, n)
    return pl.pallas_call(
        _route_rank_kernel,
        grid=(n // tile,),
        in_specs=[pl.BlockSpec((ROUTE_ROWS, tile), lambda i: (0, i))],
        out_specs=[pl.BlockSpec((ROUTE_ROWS, tile), lambda i: (0, i)),
                   pl.BlockSpec((N_EXPERTS, LANES), lambda i: (0, 0))],
        out_shape=[jax.ShapeDtypeStruct((ROUTE_ROWS, n), jnp.int32),
                   jax.ShapeDtypeStruct((N_EXPERTS, LANES), jnp.int32)],
        scratch_shapes=[pltpu.VMEM((N_EXPERTS, LANES), F32)],
        compiler_params=_cparams("arbitrary"),
        name="route_rank",
    )(route)


def _sc_mesh():
    return plsc.VectorSubcoreMesh(core_axis_name="core", subcore_axis_name="subcore")


def _sc_scatter_pairs(x, pos, rows_out):
    n, d = x.shape
    per_w = n // SC_WORKERS
    window = min(SC_SCATTER_ROWS, per_w)

    @pl.kernel(out_type=jax.ShapeDtypeStruct((rows_out, d), x.dtype), mesh=_sc_mesh(),
               scratch_types=[pltpu.VMEM((window,), jnp.int32), pltpu.VMEM((window,), jnp.int32),
                              pltpu.VMEM((window, d), x.dtype), pltpu.SemaphoreType.DMA, pltpu.SemaphoreType.DMA,
                              pltpu.SemaphoreType.DMA])
    def scatter(x_hbm, p_hbm, o_hbm, i1_v, i2_v, rows_v, sem_a, sem_b, sem_c):
        wid = lax.axis_index("subcore") * SC_CORES + lax.axis_index("core")

        @pl.loop(0, per_w // window)
        def _(j):
            base = wid * per_w + j * window
            load_i1 = pltpu.async_copy(p_hbm.at[pl.ds(base, window)], i1_v, sem_a)
            load_i2 = pltpu.async_copy(p_hbm.at[pl.ds(n + base, window)], i2_v, sem_b)
            load_x = pltpu.async_copy(x_hbm.at[pl.ds(base, window)], rows_v, sem_c)
            load_i1.wait()
            load_i2.wait()
            load_x.wait()
            put_1 = pltpu.async_copy(rows_v, o_hbm.at[i1_v], sem_a)
            put_2 = pltpu.async_copy(rows_v, o_hbm.at[i2_v], sem_b)
            put_1.wait()
            put_2.wait()

    return scatter(x, pos)


def _sc_gather_rows(table, idx):
    m = idx.shape[0]
    d = table.shape[1]
    per_w = m // SC_WORKERS
    window = min(SC_GATHER_ROWS, per_w)

    assert per_w % (2 * window) == 0

    @pl.kernel(out_type=jax.ShapeDtypeStruct((m, d), table.dtype), mesh=_sc_mesh(),
               scratch_types=[pltpu.VMEM((window,), jnp.int32), pltpu.VMEM((window,), jnp.int32),
                              pltpu.VMEM((window, d), table.dtype), pltpu.VMEM((window, d), table.dtype),
                              pltpu.SemaphoreType.DMA, pltpu.SemaphoreType.DMA])
    def gather(t_hbm, i_hbm, o_hbm, ia_v, ib_v, ra_v, rb_v, sem_a, sem_b):
        wid = lax.axis_index("subcore") * SC_CORES + lax.axis_index("core")

        @pl.loop(0, per_w // (2 * window))
        def _(j):
            base_a = wid * per_w + j * (2 * window)
            base_b = base_a + window
            idx_a = pltpu.async_copy(i_hbm.at[pl.ds(base_a, window)], ia_v, sem_a)
            idx_b = pltpu.async_copy(i_hbm.at[pl.ds(base_b, window)], ib_v, sem_b)
            idx_a.wait()
            get_a = pltpu.async_copy(t_hbm.at[ia_v], ra_v, sem_a)
            idx_b.wait()
            get_b = pltpu.async_copy(t_hbm.at[ib_v], rb_v, sem_b)
            get_a.wait()
            put_a = pltpu.async_copy(ra_v, o_hbm.at[pl.ds(base_a, window)], sem_a)
            get_b.wait()
            put_b = pltpu.async_copy(rb_v, o_hbm.at[pl.ds(base_b, window)], sem_b)
            put_a.wait()
            put_b.wait()

    return gather(table, idx)


def _expert_tiles_kernel(start_ref, ntile_ref, x_hbm, wg_ref, wu_ref, wd_ref, o_hbm,
                         wg_s, wu_s, wd_s, x_buf, o_buf, in_sem, out_sem):
    e = pl.program_id(0)
    tm = x_buf.shape[1]
    nslot = x_buf.shape[0]
    first = start_ref[e] // tm
    ntile = ntile_ref[e]
    total = start_ref[N_EXPERTS - 1] // tm + ntile_ref[N_EXPERTS - 1]
    wg_s[...] = wg_ref[...].astype(BF16)
    wu_s[...] = wu_ref[...].astype(BF16)
    wd_s[...] = wd_ref[...].astype(BF16)

    def rows_of(g):
        return pl.ds(pl.multiple_of(g * tm, tm), tm)

    def fetch(g):
        slot = g % nslot
        return pltpu.make_async_copy(x_hbm.at[rows_of(g)], x_buf.at[slot], in_sem.at[slot])

    def flush(g):
        slot = g % nslot
        return pltpu.make_async_copy(o_buf.at[slot], o_hbm.at[rows_of(g)], out_sem.at[slot])

    @pl.when(e == 0)
    def _():
        for k in range(nslot - 1):
            @pl.when(k < total)
            def _(k=k):
                fetch(k).start()

    def tile(g, carry):
        @pl.when(g + nslot - 1 < total)
        def _():
            fetch(g + nslot - 1).start()

        fetch(g).wait()

        @pl.when(g >= nslot)
        def _():
            flush(g - nslot).wait()

        slot = g % nslot
        o_buf[slot] = _pack_halves(_expert_mlp(x_buf[slot], wg_s, wu_s, wd_s).astype(BF16))
        flush(g).start()
        return carry

    lax.fori_loop(first, first + ntile, tile, 0)

    @pl.when(e == N_EXPERTS - 1)
    def _():
        for k in range(nslot, 0, -1):
            @pl.when(total >= k)
            def _(k=k):
                flush(total - k).wait()


def _expert_tiles(starts, ntiles, xs, w_g, w_u, w_d):
    rows = xs.shape[0]
    tm = EXPERT_ROW_TILE
    weight = lambda shape: pl.BlockSpec((None,) + shape, lambda e, st, nt: (e, 0, 0))
    grid_spec = pltpu.PrefetchScalarGridSpec(
        num_scalar_prefetch=2,
        grid=(N_EXPERTS,),
        in_specs=[pl.BlockSpec(memory_space=pl.ANY),
                  weight((D_MODEL, D_EXPERT)), weight((D_MODEL, D_EXPERT)), weight((D_EXPERT, D_MODEL))],
        out_specs=pl.BlockSpec(memory_space=pl.ANY),
        scratch_shapes=[pltpu.VMEM((D_MODEL, D_EXPERT), BF16), pltpu.VMEM((D_MODEL, D_EXPERT), BF16),
                        pltpu.VMEM((D_EXPERT, D_MODEL), BF16),
                        pltpu.VMEM((EXPERT_SLOTS, tm, HALF), jnp.int32), pltpu.VMEM((EXPERT_SLOTS, tm, HALF), jnp.int32),
                        pltpu.SemaphoreType.DMA((EXPERT_SLOTS,)), pltpu.SemaphoreType.DMA((EXPERT_SLOTS,))],
    )
    return pl.pallas_call(
        _expert_tiles_kernel,
        grid_spec=grid_spec,
        out_shape=jax.ShapeDtypeStruct((rows, HALF), jnp.int32),
        compiler_params=_cparams("arbitrary"),
        name="expert_tiles",
    )(starts, ntiles, xs, w_g, w_u, w_d)


def _unpack_f32(p):
    return pltpu.bitcast(p & jnp.int32(-65536), F32), pltpu.bitcast(lax.shift_left(p, jnp.int32(16)), F32)


def _combine_kernel(h_ref, o1_ref, o2_ref, rt_ref, gf_ref, y_ref):
    w1, w2 = rt_ref[:, 2:3], rt_ref[:, 3:4]
    a_lo, a_hi = _unpack_f32(o1_ref[...])
    b_lo, b_hi = _unpack_f32(o2_ref[...])
    y_lo = h_ref[:, 0:HALF] + (w1 * a_lo + w2 * b_lo)
    y_hi = h_ref[:, HALF:D_MODEL] + (w1 * a_hi + w2 * b_hi)
    ms = (jnp.sum(y_lo * y_lo, axis=-1, keepdims=True) + jnp.sum(y_hi * y_hi, axis=-1, keepdims=True)) / D_MODEL
    inv = lax.rsqrt(ms + EPS)
    y_ref[:, 0:HALF] = (y_lo * inv) * gf_ref[:, 0:HALF]
    y_ref[:, HALF:D_MODEL] = (y_hi * inv) * gf_ref[:, HALF:D_MODEL]


def _combine(h, o12, route_t, gf, tile):
    n = h.shape[0]
    tile = min(tile, n)
    nt = n // tile
    return pl.pallas_call(
        _combine_kernel,
        grid=(nt,),
        in_specs=[pl.BlockSpec((tile, D_MODEL), lambda i: (i, 0)),
                  pl.BlockSpec((tile, HALF), lambda i: (i, 0)),
                  pl.BlockSpec((tile, HALF), lambda i: (i + nt, 0)),
                  pl.BlockSpec((tile, ROUTE_ROWS), lambda i: (i, 0)),
                  pl.BlockSpec((1, D_MODEL), lambda i: (0, 0))],
        out_specs=pl.BlockSpec((tile, D_MODEL), lambda i: (i, 0)),
        out_shape=jax.ShapeDtypeStruct((n, D_MODEL), F32),
        compiler_params=_cparams("parallel"),
        name="combine_final_norm",
    )(h, o12, o12, route_t, gf)


def _sparse_moe(xn2p, route, h, w_g, w_u, w_d, gf, run_before_experts):
    n = h.shape[0]
    tm = EXPERT_ROW_TILE
    rows = 2 * n + N_EXPERTS * tm
    rank, cnt = _route_rank(route, 1024)
    counts = cnt[:, 0]
    padded = (counts + tm - 1) // tm * tm
    e_idx = jnp.arange(N_EXPERTS, dtype=jnp.int32)
    starts = jnp.sum(jnp.where(e_idx[None, :] < e_idx[:, None], padded[None, :], 0), axis=1)
    ids = route[0:2].astype(jnp.int32)
    start_of = jnp.sum(jnp.where(ids[None] == e_idx[:, None, None], starts[:, None, None], 0), axis=0)
    pos = (start_of + rank[0:2]).reshape(2 * n)
    xs = _sc_scatter_pairs(xn2p, pos, rows)
    xs, _ = lax.optimization_barrier((xs, run_before_experts))
    os_ = _expert_tiles(starts.astype(jnp.int32), (padded // tm).astype(jnp.int32), xs, w_g, w_u, w_d)
    o12 = _sc_gather_rows(os_, pos)
    return _combine(h, o12, route.T, gf, 512)


def _prep_in_weights(w_in):
    o = 0
    w_u = w_in[:, o:o + SSM_WIDTH]; o += SSM_WIDTH
    w_q = w_in[:, o:o + SWA_WIDTH]; o += SWA_WIDTH
    w_k = w_in[:, o:o + SWA_KV_WIDTH]; o += SWA_KV_WIDTH
    w_v = w_in[:, o:o + SWA_KV_WIDTH]; o += SWA_KV_WIDTH
    w_qm = w_in[:, o:o + MEM_WIDTH]; o += MEM_WIDTH
    w_g = w_in[:, o:]
    wq = (w_q * (SWA_HEAD_DIM ** -0.5)).reshape(D_MODEL, SWA_KV_HEADS, SWA_REP, SWA_HEAD_DIM)
    wq = wq.transpose(0, 2, 1, 3).reshape(D_MODEL, SWA_WIDTH)
    w_main = jnp.concatenate([w_u, wq, w_k, w_v, w_qm], axis=1).astype(BF16)
    return w_main, w_g.astype(BF16)


IN_SPLITS = (SSM_WIDTH, SWA_WIDTH, SWA_KV_WIDTH, SWA_KV_WIDTH, MEM_WIDTH)
IN_DTYPES = ((F32, BF16), (BF16,), (F32,), (F32,), (BF16,))


def kernel(x_prompt, x_sample, cache_swa_k, cache_swa_v, state_ssm_re, state_ssm_im, cache_mem_k, cache_mem_v, mem_prompt, norm1_g, w_in, lam_re, lam_im, log_dt, bm_re, bm_im, cm_re, cm_im, d_skip, w_glu, b_glu, sinks, rel_table, mem_norm_g, w_mem_kv, w_br_ssm, w_br_swa, w_br_mem, w_out, norm2_g, w_rg, b_rg, w_rexp, b_rexp, w_e_gate, w_e_up, w_e_down, final_norm_g):
    nb, t, _ = x_prompt.shape
    ns, ts, _ = x_sample.shape
    l = 0
    L = S5_CHUNK

    w_main, w_gates = _prep_in_weights(w_in[l])
    w_swa = (w_br_swa[l].reshape(SWA_KV_HEADS, SWA_REP, SWA_HEAD_DIM, D_MODEL).transpose(1, 0, 2, 3)
             .reshape(SWA_WIDTH, D_MODEL))
    pad_rows = ROUTER_ROWS - N_EXPERTS - N_EXPERT_GROUPS
    w_router = jnp.concatenate([w_rexp[l].T, w_rg[l].T, jnp.zeros((pad_rows, D_MODEL), F32)], axis=0).astype(BF16)
    b_router = jnp.concatenate([b_rexp[l], b_rg[l], jnp.zeros((pad_rows,), F32)]).reshape(ROUTER_ROWS, 1)
    mp = {
        'g1': norm1_g[l].reshape(1, D_MODEL), 'w_gates': w_gates, 'd_skip': d_skip[l].reshape(1, SSM_WIDTH),
        'w_glu': w_glu[l].astype(BF16), 'b_glu': b_glu[l].reshape(1, SSM_WIDTH),
        'w_br_ssm': w_br_ssm[l].astype(BF16), 'w_br_swa': w_swa.astype(BF16),
        'w_br_mem': w_br_mem[l].astype(BF16), 'w_out': w_out[l].astype(BF16),
        'g2': norm2_g[l].reshape(1, D_MODEL), 'w_router': w_router, 'b_router': b_router,
    }
    w_g, w_u, w_d = w_e_gate[l], w_e_up[l], w_e_down[l]
    gf = final_norm_g.reshape(1, D_MODEL)
    s5_w = _s5_weights(lam_re[l], lam_im[l], log_dt[l], bm_re[l], bm_im[l], cm_re[l], cm_im[l], L)

    bias_p = _rel_bias(rel_table, np.arange(WINDOW)[:, None] + WINDOW - np.arange(2 * WINDOW)[None, :])
    keys_s = WINDOW + 2 * ts
    bias_s = _rel_bias(rel_table, np.arange(ts)[:, None] + WINDOW - np.arange(keys_s)[None, :])
    bias_s = bias_s.reshape(SWA_HEADS * ts, keys_s)
    sink_rows = jnp.repeat(sinks[l].astype(F32), ts).reshape(SWA_HEADS * ts, 1)

    n = nb * t
    xp = x_prompt.reshape(n, D_MODEL)
    mk, mv = _norm_proj(mem_prompt.reshape(nb * MEM_TOKENS, D_MODEL), mem_norm_g[l].reshape(1, D_MODEL),
                        w_mem_kv[l].astype(BF16), (MEM_WIDTH, MEM_WIDTH), ((F32,), (F32,)), 512)
    u, ub, qz, k, v, qm = _norm_proj(xp, mp['g1'], w_main, IN_SPLITS, IN_DTYPES, 512)

    y_ssm, fin = _s5(ub, jnp.zeros((nb, N_CH_TILES * 2 * STATE_TILE), F32), s5_w, nb, t // L, L, 64)
    p_re, p_im = _tiles_to_state(fin)

    o_swa = _swa_prompt(qz, k, v, bias_p, sinks[l].astype(F32), nb, t, 4)
    o_mem = _mem_prompt(qm, mk, mv, nb, t, 512)
    h, xn2p, route = _merge(xp, u, y_ssm, o_swa, o_mem, mp, 512)

    k4 = k.reshape(nb, t, SWA_KV_HEADS, SWA_HEAD_DIM)
    v4 = v.reshape(nb, t, SWA_KV_HEADS, SWA_HEAD_DIM)
    new_k_p, new_v_p = k4[:, -WINDOW:][None], v4[:, -WINDOW:][None]
    new_mk = mk.reshape(1, nb, MEM_TOKENS, MEM_HEADS, MEM_HEAD_DIM)
    new_mv = mv.reshape(1, nb, MEM_TOKENS, MEM_HEADS, MEM_HEAD_DIM)

    m = ns * ts
    xs = x_sample.reshape(m, D_MODEL)
    us, ubs, qzs, k_s, v_s, qms = _norm_proj(xs, mp['g1'], w_main, IN_SPLITS, IN_DTYPES, 256)
    ys_ssm, fins = _s5(ubs, _state_to_tiles(state_ssm_re[l], state_ssm_im[l]), s5_w, ns, ts // L, L, 64)
    s_re, s_im = _tiles_to_state(fins)

    kk_all = jnp.concatenate([cache_swa_k[l].reshape(ns, WINDOW, SWA_KV_WIDTH).astype(F32),
                              k_s.reshape(ns, ts, SWA_KV_WIDTH)], axis=1)
    vv_all = jnp.concatenate([cache_swa_v[l].reshape(ns, WINDOW, SWA_KV_WIDTH).astype(F32),
                              v_s.reshape(ns, ts, SWA_KV_WIDTH)], axis=1)
    pad = jnp.zeros((ns, keys_s - WINDOW - ts, SWA_KV_WIDTH), F32)
    q5 = qzs.reshape(ns, ts, SWA_REP, SWA_KV_HEADS, SWA_HEAD_DIM)
    zq = jnp.zeros((ns, ts, SWA_REP, SWA_HEAD_DIM), BF16)
    q_rows = jnp.concatenate([jnp.concatenate([q5[:, :, :, 0], zq], axis=-1),
                              jnp.concatenate([zq, q5[:, :, :, 1]], axis=-1)], axis=2)
    q_rows = q_rows.transpose(0, 2, 1, 3).reshape(ns, SWA_HEADS * ts, LANES)
    o_dec = _swa_decode(q_rows, jnp.concatenate([kk_all, pad], axis=1), jnp.concatenate([vv_all, pad], axis=1),
                        bias_s, sink_rows, ts, 8)
    o_dec = o_dec.reshape(ns, SWA_KV_HEADS, SWA_REP, ts, SWA_KV_HEADS, SWA_HEAD_DIM)
    o_dec = jnp.stack([o_dec[:, g, :, :, g] for g in range(SWA_KV_HEADS)], axis=1)
    o_swa_s = o_dec.transpose(0, 3, 2, 1, 4).reshape(m, SWA_WIDTH).astype(BF16)

    o_mem_s = _mem_decode(qms.astype(F32).reshape(ns, ts, MEM_WIDTH), cache_mem_k, cache_mem_v, l, 8)
    o_mem_s = o_mem_s.reshape(m, MEM_WIDTH).astype(BF16)

    y_prompt = _sparse_moe(xn2p, route, h, w_g, w_u, w_d, gf, (ys_ssm, o_swa_s, o_mem_s)).reshape(nb, t, D_MODEL)
    hs_, xn2ps, routes = _merge(xs, us, ys_ssm, o_swa_s, o_mem_s, mp, 256)
    y_sample = _moe(xn2ps, routes.T, w_g, w_u, w_d, hs_, gf, 1024).reshape(ns, ts, D_MODEL)

    roll = lambda cache, new: jnp.concatenate(
        [cache[:, :, ts:], new.reshape(1, ns, ts, SWA_KV_HEADS, SWA_HEAD_DIM).astype(cache.dtype)], axis=2)
    new_k_s, new_v_s = roll(cache_swa_k, k_s), roll(cache_swa_v, v_s)

    return (y_prompt, y_sample,
            new_k_p, new_v_p, p_re[None], p_im[None], new_mk, new_mv,
            new_k_s, new_v_s, s_re[None].astype(state_ssm_re.dtype), s_im[None].astype(state_ssm_im.dtype))
```

```python
import functools
import math

import numpy as np
import jax
import jax.numpy as jnp
from jax import lax
from jax.experimental import pallas as pl
from jax.experimental.pallas import tpu as pltpu
from jax.experimental.pallas import tpu_sc as plsc

F32 = jnp.float32
BF16 = jnp.bfloat16

D_MODEL = 1024
SSM_WIDTH = 512
SSM_GROUP = 16
SSM_GROUPS = 32
SSM_STATE = 64
SWA_HEADS = 8
SWA_KV_HEADS = 2
SWA_REP = 4
SWA_HEAD_DIM = 64
SWA_WIDTH = 512
SWA_KV_WIDTH = 128
WINDOW = 128
REL_BUCKETS = 32
REL_MAX_DIST = 128
MEM_TOKENS = 256
MEM_HEADS = 4
MEM_HEAD_DIM = 128
MEM_WIDTH = 512
N_EXPERT_GROUPS = 4
EXPERTS_PER_GROUP = 8
N_EXPERTS = 32
D_EXPERT = 256
EPS = 1e-6
NEG_INF = -1e30

LANES = 128
GROUPS_PER_TILE = LANES // SSM_GROUP
N_CH_TILES = SSM_WIDTH // LANES
STATE_TILE = GROUPS_PER_TILE * SSM_STATE
VMEM_LIMIT = 56 * 1024 * 1024
S5_CHUNK = 8
S5_PANEL = 256

_TRANS_B = (((1,), (1,)), ((), ()))


def _cparams(*sem):
    return pltpu.CompilerParams(dimension_semantics=sem, vmem_limit_bytes=VMEM_LIMIT)


def _rms(x, g):
    return (x * lax.rsqrt(jnp.mean(x * x, axis=-1, keepdims=True) + EPS)) * g


def _dot(a, b):
    return jnp.dot(a, b, preferred_element_type=F32)


def _norm_proj_kernel(x_ref, g_ref, w_ref, *out_refs, splits, dtypes):
    xb = _rms(x_ref[...], g_ref[...]).astype(BF16)
    off = 0
    outs = iter(out_refs)
    for width, dts in zip(splits, dtypes):
        r = _dot(xb, w_ref[:, off:off + width])
        for dt in dts:
            next(outs)[...] = r.astype(dt)
        off += width


def _norm_proj(x, g, w, splits, dtypes, tile):
    n, d = x.shape
    tile = min(tile, n)
    flat = [(wd, dt) for wd, dts in zip(splits, dtypes) for dt in dts]
    return pl.pallas_call(
        functools.partial(_norm_proj_kernel, splits=tuple(splits), dtypes=tuple(dtypes)),
        grid=(n // tile,),
        in_specs=[pl.BlockSpec((tile, d), lambda i: (i, 0)),
                  pl.BlockSpec((1, d), lambda i: (0, 0)),
                  pl.BlockSpec((d, sum(splits)), lambda i: (0, 0))],
        out_specs=[pl.BlockSpec((tile, wd), lambda i: (i, 0)) for wd, _ in flat],
        out_shape=[jax.ShapeDtypeStruct((n, wd), dt) for wd, dt in flat],
        compiler_params=_cparams("parallel"),
        name="norm_proj",
    )(x, g, w)


def _s5_weights(lam_re, lam_im, log_dt, bm_re, bm_im, cm_re, cm_im, L):
    hp = lax.Precision.HIGHEST
    nt, gt, P, H = N_CH_TILES, GROUPS_PER_TILE, SSM_STATE, SSM_GROUP
    lr, li = lam_re.astype(F32), lam_im.astype(F32)
    dt = jnp.exp(log_dt.astype(F32))[:, None]
    mag = jnp.exp(lr * dt)
    a_re = mag * jnp.cos(li * dt)
    a_im = mag * jnp.sin(li * dt)
    den = lr * lr + li * li
    f_re = ((a_re - 1.0) * lr + a_im * li) / den
    f_im = (a_im * lr - (a_re - 1.0) * li) / den
    br, bi = bm_re.astype(F32), bm_im.astype(F32)
    bb_re = f_re[..., None] * br - f_im[..., None] * bi
    bb_im = f_re[..., None] * bi + f_im[..., None] * br
    pr, pi = [jnp.ones_like(a_re)], [jnp.zeros_like(a_im)]
    for _ in range(L):
        pr.append(pr[-1] * a_re - pi[-1] * a_im)
        pi.append(pr[-2] * a_im + pi[-1] * a_re)
    ap_re, ap_im = jnp.stack(pr), jnp.stack(pi)
    cr, ci = cm_re.astype(F32), cm_im.astype(F32)
    ca_re = cr[None] * ap_re[:, :, None, :] - ci[None] * ap_im[:, :, None, :]
    ca_im = cr[None] * ap_im[:, :, None, :] + ci[None] * ap_re[:, :, None, :]

    rev_re = jnp.stack([pr[L - 1 - s] for s in range(L)])
    rev_im = jnp.stack([pi[L - 1 - s] for s in range(L)])
    ws_re = rev_re[..., None] * bb_re[None] - rev_im[..., None] * bb_im[None]
    ws_im = rev_re[..., None] * bb_im[None] + rev_im[..., None] * bb_re[None]
    c_st = jnp.concatenate([ws_re.transpose(0, 1, 3, 2).reshape(L, nt, gt * H, P),
                            ws_im.transpose(0, 1, 3, 2).reshape(L, nt, gt * H, P)], axis=3).transpose(1, 0, 2, 3)
    so = lambda ca: ca[1:].transpose(1, 3, 0, 2).reshape(nt, gt * P, L * H)
    c_so = jnp.concatenate([so(ca_re), so(-ca_im)], axis=1)
    k_lag = (jnp.einsum('tghp,gpk->gkth', ca_re[:L], bb_re, precision=hp)
             - jnp.einsum('tghp,gpk->gkth', ca_im[:L], bb_im, precision=hp))
    c_k = k_lag.reshape(nt, gt * H, L * H)
    w_st, w_out, toep = _s5_expand(c_st, c_so, c_k, L)

    def per_tile(v):
        return v.reshape(nt, 1, STATE_TILE)

    return w_st, w_out, toep, per_tile(pr[L]), per_tile(pi[L])


def _s5_expand_kernel(cst_ref, cso_ref, ck_ref, wst_ref, wso_ref, toep_ref, *, L):
    hp = lax.Precision.HIGHEST
    P, H = SSM_STATE, SSM_GROUP
    iota = lambda shape, d: lax.broadcasted_iota(jnp.int32, shape, d)
    one = lambda cond: jnp.where(cond, 1.0, 0.0).astype(F32)

    r, c = iota((2 * P, 2 * STATE_TILE), 0), iota((2 * P, 2 * STATE_TILE), 1)
    rep_st = one((r // P == c // STATE_TILE) & (r % P == c % P))
    r, c = iota((LANES, 2 * STATE_TILE), 0), iota((LANES, 2 * STATE_TILE), 1)
    own_st = one(r // H == (c % STATE_TILE) // P)
    for s in range(L):
        blk = jnp.dot(cst_ref[s], rep_st, precision=hp, preferred_element_type=F32) * own_st
        wst_ref[s * LANES:(s + 1) * LANES, :] = blk.astype(BF16)

    r, c = iota((LANES, LANES), 0), iota((LANES, LANES), 1)
    pick = [one((r // H == t) & (r % H == c % H)) for t in range(L)]
    own_k = one(r // H == c // H)
    r, c = iota((2 * STATE_TILE, LANES), 0), iota((2 * STATE_TILE, LANES), 1)
    own_so = one((r % STATE_TILE) // P == c // H)
    cso = cso_ref[...]
    for t in range(L):
        blk = jnp.dot(cso, pick[t], precision=hp, preferred_element_type=F32) * own_so
        wso_ref[:, t * LANES:(t + 1) * LANES] = blk.astype(BF16)
    ck = ck_ref[...]
    lag = [(jnp.dot(ck, pick[t], precision=hp, preferred_element_type=F32) * own_k).astype(BF16) for t in range(L)]
    zero = jnp.zeros((LANES, LANES), BF16)
    for s in range(L):
        for t in range(L):
            toep_ref[s * LANES:(s + 1) * LANES, t * LANES:(t + 1) * LANES] = lag[t - s] if t >= s else zero


def _s5_expand(c_st, c_so, c_k, L):
    lk = L * LANES
    st2 = 2 * STATE_TILE
    return pl.pallas_call(
        functools.partial(_s5_expand_kernel, L=L),
        grid=(N_CH_TILES,),
        in_specs=[pl.BlockSpec((None, L, LANES, 2 * SSM_STATE), lambda j: (j, 0, 0, 0)),
                  pl.BlockSpec((None, st2, L * SSM_GROUP), lambda j: (j, 0, 0)),
                  pl.BlockSpec((None, LANES, L * SSM_GROUP), lambda j: (j, 0, 0))],
        out_specs=[pl.BlockSpec((None, lk, st2), lambda j: (j, 0, 0)),
                   pl.BlockSpec((None, st2, lk), lambda j: (j, 0, 0)),
                   pl.BlockSpec((None, lk, lk), lambda j: (j, 0, 0))],
        out_shape=[jax.ShapeDtypeStruct((N_CH_TILES, lk, st2), BF16),
                   jax.ShapeDtypeStruct((N_CH_TILES, st2, lk), BF16),
                   jax.ShapeDtypeStruct((N_CH_TILES, lk, lk), BF16)],
        compiler_params=_cparams("parallel"),
        name="s5_expand_weights",
    )(c_st, c_so, c_k)


def _to_chunks(u, nb, nc, L):
    return (u.reshape(nb, nc, L, N_CH_TILES, LANES).transpose(1, 0, 3, 2, 4)
            .reshape(nc * nb, N_CH_TILES * L * LANES))


def _from_chunks(y, nb, nc, L):
    return (y.reshape(nc, nb, N_CH_TILES, L, LANES).transpose(1, 0, 3, 2, 4)
            .reshape(nb * nc * L, SSM_WIDTH))


def _s5_kernel(x_ref, h0_ref, are_ref, aim_ref, ws_ref, t_ref, wo_ref, y_ref, fin_ref,
               hr_ref, hi_ref, d_ref, hs_ref, *, cb, nb):
    ci = pl.program_id(1)

    @pl.when(ci == 0)
    def _():
        hr_ref[...] = h0_ref[:, 0:STATE_TILE]
        hi_ref[...] = h0_ref[:, STATE_TILE:2 * STATE_TILE]

    x = x_ref[...]
    d_ref[...] = _dot(x, ws_ref[...])
    ar = jnp.broadcast_to(are_ref[...], (nb, STATE_TILE))
    ai = jnp.broadcast_to(aim_ref[...], (nb, STATE_TILE))

    def body(c, carry):
        hr, hi = carry
        r0 = pl.multiple_of(c * nb, nb)
        hs_ref[pl.ds(r0, nb), 0:STATE_TILE] = hr
        hs_ref[pl.ds(r0, nb), STATE_TILE:2 * STATE_TILE] = hi
        d = d_ref[pl.ds(r0, nb), :]
        return (ar * hr - ai * hi + d[:, 0:STATE_TILE],
                ar * hi + ai * hr + d[:, STATE_TILE:2 * STATE_TILE])

    hr, hi = lax.fori_loop(0, cb, body, (hr_ref[...], hi_ref[...]))
    hr_ref[...] = hr
    hi_ref[...] = hi
    hsb = hs_ref[...].astype(BF16)
    for c0 in range(0, t_ref.shape[1], S5_PANEL):
        c1 = c0 + S5_PANEL
        y_ref[:, c0:c1] = _dot(x[:, 0:c1], t_ref[0:c1, c0:c1]) + _dot(hsb, wo_ref[:, c0:c1])

    @pl.when(ci == pl.num_programs(1) - 1)
    def _():
        fin_ref[:, 0:STATE_TILE] = hr
        fin_ref[:, STATE_TILE:2 * STATE_TILE] = hi


def _s5(ub, h0, weights, nb, nc, L, chunk_block):
    w_st, w_so, toep, a_re, a_im = weights
    xc = _to_chunks(ub, nb, nc, L)
    cb = min(chunk_block, nc)
    rows = cb * nb
    lk = L * LANES
    st2 = 2 * STATE_TILE
    tile_w = lambda shape: pl.BlockSpec((None,) + shape, lambda j, c: (j, 0, 0))
    y, fin = pl.pallas_call(
        functools.partial(_s5_kernel, cb=cb, nb=nb),
        grid=(N_CH_TILES, nc // cb),
        in_specs=[pl.BlockSpec((rows, lk), lambda j, c: (c, j)),
                  pl.BlockSpec((nb, st2), lambda j, c: (0, j)),
                  tile_w((1, STATE_TILE)), tile_w((1, STATE_TILE)),
                  tile_w((lk, st2)), tile_w((lk, lk)), tile_w((st2, lk))],
        out_specs=[pl.BlockSpec((rows, lk), lambda j, c: (c, j)),
                   pl.BlockSpec((nb, st2), lambda j, c: (0, j))],
        out_shape=[jax.ShapeDtypeStruct((nc * nb, N_CH_TILES * lk), F32),
                   jax.ShapeDtypeStruct((nb, N_CH_TILES * st2), F32)],
        scratch_shapes=[pltpu.VMEM((nb, STATE_TILE), F32), pltpu.VMEM((nb, STATE_TILE), F32),
                        pltpu.VMEM((rows, st2), F32), pltpu.VMEM((rows, st2), F32)],
        compiler_params=_cparams("parallel", "arbitrary"),
        name="s5_chunked_scan",
    )(xc, h0, a_re, a_im, w_st, toep, w_so)
    return _from_chunks(y, nb, nc, L), fin


def _state_to_tiles(h_re, h_im):
    nb = h_re.shape[0]
    r = h_re.astype(F32).reshape(nb, N_CH_TILES, STATE_TILE)
    i = h_im.astype(F32).reshape(nb, N_CH_TILES, STATE_TILE)
    return jnp.concatenate([r, i], axis=-1).reshape(nb, N_CH_TILES * 2 * STATE_TILE)


def _tiles_to_state(h):
    nb = h.shape[0]
    h = h.reshape(nb, N_CH_TILES, 2, GROUPS_PER_TILE, SSM_STATE)
    return (h[:, :, 0].reshape(nb, SSM_GROUPS, SSM_STATE), h[:, :, 1].reshape(nb, SSM_GROUPS, SSM_STATE))


def _t5_bucket(dist):
    n = np.maximum(dist, 0)
    max_exact = REL_BUCKETS // 2
    nf = np.maximum(n, 1).astype(np.float32)
    large = max_exact + (np.log(nf / np.float32(max_exact)) / np.float32(math.log(REL_MAX_DIST / max_exact))
                         * np.float32(REL_BUCKETS - max_exact)).astype(np.int32)
    large = np.minimum(large, REL_BUCKETS - 1)
    return np.where(n < max_exact, n, large)


def _rel_bias(rel_table, dist):
    bucket = _t5_bucket(dist)
    tab = rel_table.astype(F32)
    out = jnp.zeros((SWA_HEADS,) + dist.shape, F32)
    for b in range(REL_BUCKETS):
        sel = jnp.asarray(bucket == b)
        if bool((bucket == b).any()):
            out = jnp.where(sel[None], tab[b].reshape((SWA_HEADS,) + (1,) * dist.ndim), out)
    return out


def _swa_prompt_kernel(sink_ref, q_ref, kp_ref, kc_ref, vp_ref, vc_ref, bias_ref, o_ref, kk_ref, vv_ref, *, qblocks):
    step = pl.program_id(1)
    kk_ref[0:WINDOW, :] = kp_ref[...].astype(BF16)
    kk_ref[WINDOW:, :] = kc_ref[...].astype(BF16)
    vv_ref[0:WINDOW, :] = vp_ref[...].astype(BF16)
    vv_ref[WINDOW:, :] = vc_ref[...].astype(BF16)
    row = lax.broadcasted_iota(jnp.int32, (WINDOW, 2 * WINDOW), 0)
    col = lax.broadcasted_iota(jnp.int32, (WINDOW, 2 * WINDOW), 1)
    dist = row + WINDOW - col
    band = (dist >= 0) & (dist < WINDOW)
    lane = lax.broadcasted_iota(jnp.int32, (WINDOW, LANES), 1)
    low = lane < SWA_HEAD_DIM

    def block(j, carry):
        r0 = pl.multiple_of(j * WINDOW, WINDOW)
        kk = kk_ref[pl.ds(r0, 2 * WINDOW), :]
        vv = vv_ref[pl.ds(r0, 2 * WINDOW), :]
        valid = band & ((col >= WINDOW) | (step * qblocks + j > 0))
        for t in range(SWA_REP):
            q2 = q_ref[pl.ds(r0, WINDOW), t * LANES:(t + 1) * LANES]
            outs = []
            for half in range(SWA_KV_HEADS):
                h = t + SWA_REP * half
                qh = jnp.where(low if half == 0 else jnp.logical_not(low), q2, jnp.zeros_like(q2))
                s = lax.dot_general(qh, kk, _TRANS_B, preferred_element_type=F32)
                s = jnp.where(valid, s + bias_ref[h], NEG_INF)
                sink = sink_ref[h]
                m = jnp.maximum(jnp.max(s, axis=-1, keepdims=True), sink)
                e = jnp.exp(s - m)
                den = jnp.sum(e, axis=-1, keepdims=True) + jnp.exp(sink - m)
                outs.append(_dot(e.astype(BF16), vv) * (1.0 / den))
            o_ref[pl.ds(r0, WINDOW), t * LANES:(t + 1) * LANES] = jnp.where(low, outs[0], outs[1]).astype(BF16)
        return carry

    lax.fori_loop(0, qblocks, block, 0)


def _swa_prompt(q, k, v, bias, sinks, nb, t, qblocks):
    nstep = t // (WINDOW * qblocks)
    rows = WINDOW * qblocks
    cur = lambda b, i: (b * nstep + i, 0)
    prev = lambda b, i: (b * nstep * qblocks + jnp.maximum(i * qblocks - 1, 0), 0)
    return pl.pallas_call(
        functools.partial(_swa_prompt_kernel, qblocks=qblocks),
        grid=(nb, nstep),
        in_specs=[pl.BlockSpec(memory_space=pltpu.SMEM),
                  pl.BlockSpec((rows, SWA_WIDTH), cur),
                  pl.BlockSpec((WINDOW, SWA_KV_WIDTH), prev),
                  pl.BlockSpec((rows, SWA_KV_WIDTH), cur),
                  pl.BlockSpec((WINDOW, SWA_KV_WIDTH), prev),
                  pl.BlockSpec((rows, SWA_KV_WIDTH), cur),
                  pl.BlockSpec((SWA_HEADS, WINDOW, 2 * WINDOW), lambda b, i: (0, 0, 0))],
        out_specs=pl.BlockSpec((rows, SWA_WIDTH), cur),
        out_shape=jax.ShapeDtypeStruct((nb * t, SWA_WIDTH), BF16),
        scratch_shapes=[pltpu.VMEM((rows + WINDOW, SWA_KV_WIDTH), BF16),
                        pltpu.VMEM((rows + WINDOW, SWA_KV_WIDTH), BF16)],
        compiler_params=_cparams("parallel", "parallel"),
        name="swa_prompt",
    )(sinks, q, k, k, v, v, bias)


def _swa_decode_kernel(q_ref, k_ref, v_ref, bias_ref, sink_ref, o_ref, nk_ref, nv_ref, *, seqs, tq):
    rows, keys = q_ref.shape[1], k_ref.shape[1]
    nk_ref[...] = k_ref[:, tq:tq + WINDOW, :]
    nv_ref[...] = v_ref[:, tq:tq + WINDOW, :]
    qi = lax.broadcasted_iota(jnp.int32, (rows, keys), 0) % tq
    col = lax.broadcasted_iota(jnp.int32, (rows, keys), 1)
    dist = qi + WINDOW - col
    valid = (dist >= 0) & (dist < WINDOW)
    bias = bias_ref[...]
    sink = sink_ref[...]
    for s_i in range(seqs):
        kk = k_ref[s_i].astype(BF16)
        s = lax.dot_general(q_ref[s_i], kk, _TRANS_B, preferred_element_type=F32)
        s = jnp.where(valid, s + bias, NEG_INF)
        m = jnp.maximum(jnp.max(s, axis=-1, keepdims=True), sink)
        e = jnp.exp(s - m)
        den = jnp.sum(e, axis=-1, keepdims=True) + jnp.exp(sink - m)
        o_ref[s_i] = _dot(e.astype(BF16), v_ref[s_i].astype(BF16)) * (1.0 / den)


def _swa_decode(qz, k_all, v_all, bias, sink_rows, tq, seqs):
    nseq, rows, _ = qz.shape
    keys = k_all.shape[1]
    seqs = min(seqs, nseq)
    return pl.pallas_call(
        functools.partial(_swa_decode_kernel, seqs=seqs, tq=tq),
        grid=(nseq // seqs,),
        in_specs=[pl.BlockSpec((seqs, rows, LANES), lambda i: (i, 0, 0)),
                  pl.BlockSpec((seqs, keys, LANES), lambda i: (i, 0, 0)),
                  pl.BlockSpec((seqs, keys, LANES), lambda i: (i, 0, 0)),
                  pl.BlockSpec((rows, keys), lambda i: (0, 0)),
                  pl.BlockSpec((rows, 1), lambda i: (0, 0))],
        out_specs=[pl.BlockSpec((seqs, rows, LANES), lambda i: (i, 0, 0)),
                   pl.BlockSpec((seqs, WINDOW, LANES), lambda i: (i, 0, 0)),
                   pl.BlockSpec((seqs, WINDOW, LANES), lambda i: (i, 0, 0))],
        out_shape=[jax.ShapeDtypeStruct((nseq, rows, LANES), F32),
                   jax.ShapeDtypeStruct((nseq, WINDOW, LANES), F32),
                   jax.ShapeDtypeStruct((nseq, WINDOW, LANES), F32)],
        compiler_params=_cparams("parallel"),
        name="swa_decode",
    )(qz, k_all, v_all, bias, sink_rows)


def _softmax(s):
    m = jnp.max(s, axis=-1, keepdims=True)
    e = jnp.exp(s - m)
    return e * (1.0 / jnp.sum(e, axis=-1, keepdims=True))


def _mem_prompt_kernel(q_ref, k_ref, v_ref, o_ref, s_ref, p_ref):
    scale = MEM_HEAD_DIM ** -0.5
    heads = [slice(h * MEM_HEAD_DIM, (h + 1) * MEM_HEAD_DIM) for h in range(MEM_HEADS)]
    for h, sl in enumerate(heads):
        s_ref[h] = lax.dot_general(q_ref[:, sl], k_ref[:, sl].astype(BF16), _TRANS_B, preferred_element_type=F32)
    s = s_ref[...] * scale
    e = jnp.exp(s - jnp.max(s, axis=-1, keepdims=True))
    p_ref[...] = e.astype(BF16)
    inv = 1.0 / jnp.sum(e, axis=-1, keepdims=True)
    for h, sl in enumerate(heads):
        o_ref[:, sl] = (_dot(p_ref[h], v_ref[:, sl].astype(BF16)) * inv[h]).astype(BF16)


def _mem_prompt(qm, mk, mv, nb, t, tile):
    tile = min(tile, t)
    nt = t // tile
    return pl.pallas_call(
        _mem_prompt_kernel,
        grid=(nb, nt),
        in_specs=[pl.BlockSpec((tile, MEM_WIDTH), lambda b, i: (b * nt + i, 0)),
                  pl.BlockSpec((MEM_TOKENS, MEM_WIDTH), lambda b, i: (b, 0)),
                  pl.BlockSpec((MEM_TOKENS, MEM_WIDTH), lambda b, i: (b, 0))],
        out_specs=pl.BlockSpec((tile, MEM_WIDTH), lambda b, i: (b * nt + i, 0)),
        out_shape=jax.ShapeDtypeStruct((nb * t, MEM_WIDTH), BF16),
        scratch_shapes=[pltpu.VMEM((MEM_HEADS, tile, MEM_TOKENS), F32), pltpu.VMEM((MEM_HEADS, tile, MEM_TOKENS), BF16)],
        compiler_params=_cparams("parallel", "parallel"),
        name="mem_prompt",
    )(qm, mk, mv)


def _mem_decode_kernel(q_ref, k_ref, v_ref, o_ref, *, seqs):
    tq = q_ref.shape[1]
    rows, cols = MEM_HEADS * tq, MEM_TOKENS * MEM_HEADS
    k2 = k_ref.reshape(seqs, cols, MEM_HEAD_DIM)
    v2 = v_ref.reshape(seqs, cols, MEM_HEAD_DIM)
    scale = MEM_HEAD_DIM ** -0.5
    own = (lax.broadcasted_iota(jnp.int32, (rows, cols), 1) % MEM_HEADS
           == lax.broadcasted_iota(jnp.int32, (rows, cols), 0) // tq)
    for s_i in range(seqs):
        q = q_ref[s_i]
        qb = jnp.concatenate([q[:, h * MEM_HEAD_DIM:(h + 1) * MEM_HEAD_DIM] for h in range(MEM_HEADS)], axis=0)
        s = lax.dot_general(qb.astype(BF16), k2[s_i].astype(BF16), _TRANS_B, preferred_element_type=F32) * scale
        p = _softmax(jnp.where(own, s, NEG_INF)).astype(BF16)
        o = _dot(p, v2[s_i].astype(BF16))
        for h in range(MEM_HEADS):
            o_ref[s_i, :, h * MEM_HEAD_DIM:(h + 1) * MEM_HEAD_DIM] = o[h * tq:(h + 1) * tq, :]


def _mem_decode(q, k, v, layer, seqs):
    nseq, tq, _ = q.shape
    seqs = min(seqs, nseq)
    cache = pl.BlockSpec((None, seqs, MEM_TOKENS, MEM_HEADS, MEM_HEAD_DIM), lambda i: (layer, i, 0, 0, 0))
    return pl.pallas_call(
        functools.partial(_mem_decode_kernel, seqs=seqs),
        grid=(nseq // seqs,),
        in_specs=[pl.BlockSpec((seqs, tq, MEM_WIDTH), lambda i: (i, 0, 0)), cache, cache],
        out_specs=pl.BlockSpec((seqs, tq, MEM_WIDTH), lambda i: (i, 0, 0)),
        out_shape=jax.ShapeDtypeStruct((nseq, tq, MEM_WIDTH), F32),
        compiler_params=_cparams("parallel"),
        name="mem_decode",
    )(q, k, v)


ROUTER_ROWS = 40
ROUTE_ROWS = 8
HALF = D_MODEL // 2


def _pack_halves(xb):
    hi = pltpu.bitcast(xb[:, 0:HALF].astype(F32), jnp.int32)
    lo = pltpu.bitcast(xb[:, HALF:D_MODEL].astype(F32), jnp.int32)
    return hi | lax.shift_right_logical(lo, jnp.int32(16))


def _unpack_halves(p):
    hi = pltpu.bitcast(p & jnp.int32(-65536), F32).astype(BF16)
    lo = pltpu.bitcast(lax.shift_left(p, jnp.int32(16)), F32).astype(BF16)
    return hi, lo


def _merge_kernel(x_ref, u_ref, y_ref, os_ref, om_ref, g1_ref, wg_ref, dsk_ref, wglu_ref, bglu_ref,
                  wbs_ref, wbw_ref, wbm_ref, wout_ref, g2_ref, wr_ref, br_ref,
                  h_ref, xn2_ref, route_ref):
    x = x_ref[...]
    tt = x.shape[0]
    xb = _rms(x, g1_ref[...]).astype(BF16)
    z = jax.nn.gelu(y_ref[...] + dsk_ref[...] * u_ref[...])
    z = z * jax.nn.sigmoid(_dot(z.astype(BF16), wglu_ref[...]) + bglu_ref[...])
    merged = jax.nn.sigmoid(_dot(xb, wg_ref[:, 0:D_MODEL])) * _dot(z.astype(BF16), wbs_ref[...])
    merged = merged + jax.nn.sigmoid(_dot(xb, wg_ref[:, D_MODEL:2 * D_MODEL])) * _dot(os_ref[...], wbw_ref[...])
    merged = merged + jax.nn.sigmoid(_dot(xb, wg_ref[:, 2 * D_MODEL:3 * D_MODEL])) * _dot(om_ref[...], wbm_ref[...])
    h = x + _dot(merged.astype(BF16), wout_ref[...])
    h_ref[...] = h
    xn2 = _rms(h, g2_ref[...]).astype(BF16)
    xn2_ref[...] = _pack_halves(xn2)

    lt = lax.dot_general(wr_ref[...], xn2, _TRANS_B, preferred_element_type=F32) + br_ref[...]
    gl = lt[N_EXPERTS:N_EXPERTS + N_EXPERT_GROUPS]
    ge = jnp.exp(gl - jnp.max(gl, axis=0, keepdims=True))
    gp = ge / jnp.sum(ge, axis=0, keepdims=True)
    gw = jnp.max(gp, axis=0, keepdims=True)
    gidx = jnp.full((1, tt), N_EXPERT_GROUPS - 1, jnp.int32)
    for r in range(N_EXPERT_GROUPS - 2, -1, -1):
        gidx = jnp.where(gp[r:r + 1] == gw, r, gidx)
    ein = lt[(N_EXPERT_GROUPS - 1) * EXPERTS_PER_GROUP:N_EXPERTS]
    for r in range(N_EXPERT_GROUPS - 2, -1, -1):
        ein = jnp.where(gidx == r, lt[r * EXPERTS_PER_GROUP:(r + 1) * EXPERTS_PER_GROUP], ein)
    ee = jnp.exp(ein - jnp.max(ein, axis=0, keepdims=True))
    ep = ee / jnp.sum(ee, axis=0, keepdims=True)
    rowi = lax.broadcasted_iota(jnp.int32, (EXPERTS_PER_GROUP, tt), 0)
    p1 = jnp.max(ep, axis=0, keepdims=True)
    e1 = jnp.min(jnp.where(ep == p1, rowi, EXPERTS_PER_GROUP), axis=0, keepdims=True)
    ep2 = jnp.where(rowi == e1, -1.0, ep)
    p2 = jnp.max(ep2, axis=0, keepdims=True)
    e2 = jnp.min(jnp.where(ep2 == p2, rowi, EXPERTS_PER_GROUP), axis=0, keepdims=True)
    tot = p1 + p2
    w1 = p1 / tot * gw
    w2 = p2 / tot * gw
    id1 = (gidx * EXPERTS_PER_GROUP + e1).astype(F32)
    id2 = (gidx * EXPERTS_PER_GROUP + e2).astype(F32)
    route_ref[...] = jnp.concatenate([id1, id2, w1, w2, jnp.zeros((ROUTE_ROWS - 4, tt), F32)], axis=0)


def _merge(x, u, y, o_swa, o_mem, p, tile):
    n = x.shape[0]
    tile = min(tile, n)
    row = lambda i: (i, 0)
    const = lambda i: (0, 0)
    full = lambda a: pl.BlockSpec(a.shape, const, pipeline_mode=pl.Buffered(1))
    weights = [p['g1'], p['w_gates'], p['d_skip'], p['w_glu'], p['b_glu'], p['w_br_ssm'], p['w_br_swa'],
               p['w_br_mem'], p['w_out'], p['g2'], p['w_router'], p['b_router']]
    return pl.pallas_call(
        _merge_kernel,
        grid=(n // tile,),
        in_specs=[pl.BlockSpec((tile, D_MODEL), row), pl.BlockSpec((tile, SSM_WIDTH), row),
                  pl.BlockSpec((tile, SSM_WIDTH), row), pl.BlockSpec((tile, SWA_WIDTH), row),
                  pl.BlockSpec((tile, MEM_WIDTH), row)] + [full(w) for w in weights],
        out_specs=[pl.BlockSpec((tile, D_MODEL), row), pl.BlockSpec((tile, HALF), row),
                   pl.BlockSpec((ROUTE_ROWS, tile), lambda i: (0, i))],
        out_shape=[jax.ShapeDtypeStruct((n, D_MODEL), F32), jax.ShapeDtypeStruct((n, HALF), jnp.int32),
                   jax.ShapeDtypeStruct((ROUTE_ROWS, n), F32)],
        compiler_params=_cparams("parallel"),
        name="merge_router",
    )(x, u, y, o_swa, o_mem, *weights)


def _expert_mlp(xp, wg, wu, wd):
    hi, lo = _unpack_halves(xp)
    g = _dot(hi, wg[0:HALF, :]) + _dot(lo, wg[HALF:D_MODEL, :])
    u = _dot(hi, wu[0:HALF, :]) + _dot(lo, wu[HALF:D_MODEL, :])
    hh = jax.nn.silu(g) * u
    return _dot(hh.astype(BF16), wd[...])


def _moe_kernel(xn2_ref, rt_ref, wg_ref, wu_ref, wd_ref, h_ref, gf_ref, o_ref, acc_ref):
    e = pl.program_id(1)

    @pl.when(e == 0)
    def _():
        acc_ref[...] = jnp.zeros_like(acc_ref)

    o = _expert_mlp(xn2_ref[...], wg_ref[...].astype(BF16), wu_ref[...].astype(BF16), wd_ref[...].astype(BF16))
    ef = e.astype(F32)
    c = (jnp.where(rt_ref[:, 0:1] == ef, rt_ref[:, 2:3], 0.0)
         + jnp.where(rt_ref[:, 1:2] == ef, rt_ref[:, 3:4], 0.0))
    acc_ref[...] += c * o

    @pl.when(e == N_EXPERTS - 1)
    def _():
        o_ref[...] = _rms(h_ref[...] + acc_ref[...], gf_ref[...])


def _moe(xn2, route_t, w_g, w_u, w_d, h, gf, tile):
    n = h.shape[0]
    tile = min(tile, n)
    return pl.pallas_call(
        _moe_kernel,
        grid=(n // tile, N_EXPERTS),
        in_specs=[pl.BlockSpec((tile, HALF), lambda i, e: (i, 0)),
                  pl.BlockSpec((tile, ROUTE_ROWS), lambda i, e: (i, 0)),
                  pl.BlockSpec((None, D_MODEL, D_EXPERT), lambda i, e: (e, 0, 0)),
                  pl.BlockSpec((None, D_MODEL, D_EXPERT), lambda i, e: (e, 0, 0)),
                  pl.BlockSpec((None, D_EXPERT, D_MODEL), lambda i, e: (e, 0, 0)),
                  pl.BlockSpec((tile, D_MODEL), lambda i, e: (i, 0)),
                  pl.BlockSpec((1, D_MODEL), lambda i, e: (0, 0))],
        out_specs=pl.BlockSpec((tile, D_MODEL), lambda i, e: (i, 0)),
        out_shape=jax.ShapeDtypeStruct((n, D_MODEL), F32),
        scratch_shapes=[pltpu.VMEM((tile, D_MODEL), F32)],
        compiler_params=_cparams("parallel", "arbitrary"),
        name="moe_final_norm",
    )(xn2, route_t, w_g, w_u, w_d, h, gf)


EXPERT_ROW_TILE = 256
EXPERT_SLOTS = 4
SC_CORES = 2
SC_SUBCORES = 16
SC_WORKERS = SC_CORES * SC_SUBCORES
SC_SCATTER_ROWS = 64
SC_GATHER_ROWS = 64


def _route_rank_kernel(r_ref, rank_ref, cnt_ref, base_ref):
    i = pl.program_id(0)
    tt = r_ref.shape[1]

    @pl.when(i == 0)
    def _():
        base_ref[...] = jnp.zeros_like(base_ref)

    ids = r_ref[0:2, :].astype(jnp.int32)
    e_iota = lax.broadcasted_iota(jnp.int32, (N_EXPERTS, tt), 0)
    oh1 = jnp.where(e_iota == ids[0:1], 1.0, 0.0)
    oh2 = jnp.where(e_iota == ids[1:2], 1.0, 0.0)
    before = (lax.broadcasted_iota(jnp.int32, (tt, tt), 0) < lax.broadcasted_iota(jnp.int32, (tt, tt), 1))
    tri = jnp.where(before, 1.0, 0.0).astype(BF16)
    c1 = _dot(oh1.astype(BF16), tri)
    c2 = _dot(oh2.astype(BF16), tri)
    tot1 = jnp.sum(oh1, axis=1, keepdims=True)
    tot2 = jnp.sum(oh2, axis=1, keepdims=True)
    base = base_ref[:, 0:1]
    rank1 = jnp.sum(oh1 * (base + c1), axis=0, keepdims=True)
    rank2 = jnp.sum(oh2 * (base + tot1 + c2), axis=0, keepdims=True)
    rank_ref[...] = jnp.concatenate([rank1, rank2, jnp.zeros((ROUTE_ROWS - 2, tt), F32)], axis=0).astype(jnp.int32)
    new_base = jnp.broadcast_to(base + tot1 + tot2, base_ref.shape)
    base_ref[...] = new_base
    cnt_ref[...] = new_base.astype(jnp.int32)


def _route_rank(route, tile):
    n = route.shape[1]
    tile = min(tile, n)
    return pl.pallas_call(
        _route_rank_kernel,
        grid=(n // tile,),
        in_specs=[pl.BlockSpec((ROUTE_ROWS, tile), lambda i: (0, i))],
        out_specs=[pl.BlockSpec((ROUTE_ROWS, tile), lambda i: (0, i)),
                   pl.BlockSpec((N_EXPERTS, LANES), lambda i: (0, 0))],
        out_shape=[jax.ShapeDtypeStruct((ROUTE_ROWS, n), jnp.int32),
                   jax.ShapeDtypeStruct((N_EXPERTS, LANES), jnp.int32)],
        scratch_shapes=[pltpu.VMEM((N_EXPERTS, LANES), F32)],
        compiler_params=_cparams("arbitrary"),
        name="route_rank",
    )(route)


def _sc_mesh():
    return plsc.VectorSubcoreMesh(core_axis_name="core", subcore_axis_name="subcore")


def _sc_scatter_pairs(x, pos, rows_out):
    n, d = x.shape
    per_w = n // SC_WORKERS
    window = min(SC_SCATTER_ROWS, per_w)

    @pl.kernel(out_type=jax.ShapeDtypeStruct((rows_out, d), x.dtype), mesh=_sc_mesh(),
               scratch_types=[pltpu.VMEM((window,), jnp.int32), pltpu.VMEM((window,), jnp.int32),
                              pltpu.VMEM((window, d), x.dtype), pltpu.SemaphoreType.DMA, pltpu.SemaphoreType.DMA,
                              pltpu.SemaphoreType.DMA])
    def scatter(x_hbm, p_hbm, o_hbm, i1_v, i2_v, rows_v, sem_a, sem_b, sem_c):
        wid = lax.axis_index("subcore") * SC_CORES + lax.axis_index("core")

        @pl.loop(0, per_w // window)
        def _(j):
            base = wid * per_w + j * window
            load_i1 = pltpu.async_copy(p_hbm.at[pl.ds(base, window)], i1_v, sem_a)
            load_i2 = pltpu.async_copy(p_hbm.at[pl.ds(n + base, window)], i2_v, sem_b)
            load_x = pltpu.async_copy(x_hbm.at[pl.ds(base, window)], rows_v, sem_c)
            load_i1.wait()
            load_i2.wait()
            load_x.wait()
            put_1 = pltpu.async_copy(rows_v, o_hbm.at[i1_v], sem_a)
            put_2 = pltpu.async_copy(rows_v, o_hbm.at[i2_v], sem_b)
            put_1.wait()
            put_2.wait()

    return scatter(x, pos)


def _sc_gather_rows(table, idx):
    m = idx.shape[0]
    d = table.shape[1]
    per_w = m // SC_WORKERS
    window = min(SC_GATHER_ROWS, per_w)

    assert per_w % (2 * window) == 0

    @pl.kernel(out_type=jax.ShapeDtypeStruct((m, d), table.dtype), mesh=_sc_mesh(),
               scratch_types=[pltpu.VMEM((window,), jnp.int32), pltpu.VMEM((window,), jnp.int32),
                              pltpu.VMEM((window, d), table.dtype), pltpu.VMEM((window, d), table.dtype),
                              pltpu.SemaphoreType.DMA, pltpu.SemaphoreType.DMA])
    def gather(t_hbm, i_hbm, o_hbm, ia_v, ib_v, ra_v, rb_v, sem_a, sem_b):
        wid = lax.axis_index("subcore") * SC_CORES + lax.axis_index("core")

        @pl.loop(0, per_w // (2 * window))
        def _(j):
            base_a = wid * per_w + j * (2 * window)
            base_b = base_a + window
            idx_a = pltpu.async_copy(i_hbm.at[pl.ds(base_a, window)], ia_v, sem_a)
            idx_b = pltpu.async_copy(i_hbm.at[pl.ds(base_b, window)], ib_v, sem_b)
            idx_a.wait()
            get_a = pltpu.async_copy(t_hbm.at[ia_v], ra_v, sem_a)
            idx_b.wait()
            get_b = pltpu.async_copy(t_hbm.at[ib_v], rb_v, sem_b)
            get_a.wait()
            put_a = pltpu.async_copy(ra_v, o_hbm.at[pl.ds(base_a, window)], sem_a)
            get_b.wait()
            put_b = pltpu.async_copy(rb_v, o_hbm.at[pl.ds(base_b, window)], sem_b)
            put_a.wait()
            put_b.wait()

    return gather(table, idx)


def _expert_tiles_kernel(start_ref, ntile_ref, x_hbm, wg_ref, wu_ref, wd_ref, o_hbm,
                         wg_s, wu_s, wd_s, x_buf, o_buf, in_sem, out_sem):
    e = pl.program_id(0)
    tm = x_buf.shape[1]
    nslot = x_buf.shape[0]
    first = start_ref[e] // tm
    ntile = ntile_ref[e]
    total = start_ref[N_EXPERTS - 1] // tm + ntile_ref[N_EXPERTS - 1]
    wg_s[...] = wg_ref[...].astype(BF16)
    wu_s[...] = wu_ref[...].astype(BF16)
    wd_s[...] = wd_ref[...].astype(BF16)

    def rows_of(g):
        return pl.ds(pl.multiple_of(g * tm, tm), tm)

    def fetch(g):
        slot = g % nslot
        return pltpu.make_async_copy(x_hbm.at[rows_of(g)], x_buf.at[slot], in_sem.at[slot])

    def flush(g):
        slot = g % nslot
        return pltpu.make_async_copy(o_buf.at[slot], o_hbm.at[rows_of(g)], out_sem.at[slot])

    @pl.when(e == 0)
    def _():
        for k in range(nslot - 1):
            @pl.when(k < total)
            def _(k=k):
                fetch(k).start()

    def tile(g, carry):
        @pl.when(g + nslot - 1 < total)
        def _():
            fetch(g + nslot - 1).start()

        fetch(g).wait()

        @pl.when(g >= nslot)
        def _():
            flush(g - nslot).wait()

        slot = g % nslot
        o_buf[slot] = _pack_halves(_expert_mlp(x_buf[slot], wg_s, wu_s, wd_s).astype(BF16))
        flush(g).start()
        return carry

    lax.fori_loop(first, first + ntile, tile, 0)

    @pl.when(e == N_EXPERTS - 1)
    def _():
        for k in range(nslot, 0, -1):
            @pl.when(total >= k)
            def _(k=k):
                flush(total - k).wait()


def _expert_tiles(starts, ntiles, xs, w_g, w_u, w_d):
    rows = xs.shape[0]
    tm = EXPERT_ROW_TILE
    weight = lambda shape: pl.BlockSpec((None,) + shape, lambda e, st, nt: (e, 0, 0))
    grid_spec = pltpu.PrefetchScalarGridSpec(
        num_scalar_prefetch=2,
        grid=(N_EXPERTS,),
        in_specs=[pl.BlockSpec(memory_space=pl.ANY),
                  weight((D_MODEL, D_EXPERT)), weight((D_MODEL, D_EXPERT)), weight((D_EXPERT, D_MODEL))],
        out_specs=pl.BlockSpec(memory_space=pl.ANY),
        scratch_shapes=[pltpu.VMEM((D_MODEL, D_EXPERT), BF16), pltpu.VMEM((D_MODEL, D_EXPERT), BF16),
                        pltpu.VMEM((D_EXPERT, D_MODEL), BF16),
                        pltpu.VMEM((EXPERT_SLOTS, tm, HALF), jnp.int32), pltpu.VMEM((EXPERT_SLOTS, tm, HALF), jnp.int32),
                        pltpu.SemaphoreType.DMA((EXPERT_SLOTS,)), pltpu.SemaphoreType.DMA((EXPERT_SLOTS,))],
    )
    return pl.pallas_call(
        _expert_tiles_kernel,
        grid_spec=grid_spec,
        out_shape=jax.ShapeDtypeStruct((rows, HALF), jnp.int32),
        compiler_params=_cparams("arbitrary"),
        name="expert_tiles",
    )(starts, ntiles, xs, w_g, w_u, w_d)


def _unpack_f32(p):
    return pltpu.bitcast(p & jnp.int32(-65536), F32), pltpu.bitcast(lax.shift_left(p, jnp.int32(16)), F32)


def _combine_kernel(h_ref, o1_ref, o2_ref, rt_ref, gf_ref, y_ref):
    w1, w2 = rt_ref[:, 2:3], rt_ref[:, 3:4]
    a_lo, a_hi = _unpack_f32(o1_ref[...])
    b_lo, b_hi = _unpack_f32(o2_ref[...])
    y_lo = h_ref[:, 0:HALF] + (w1 * a_lo + w2 * b_lo)
    y_hi = h_ref[:, HALF:D_MODEL] + (w1 * a_hi + w2 * b_hi)
    ms = (jnp.sum(y_lo * y_lo, axis=-1, keepdims=True) + jnp.sum(y_hi * y_hi, axis=-1, keepdims=True)) / D_MODEL
    inv = lax.rsqrt(ms + EPS)
    y_ref[:, 0:HALF] = (y_lo * inv) * gf_ref[:, 0:HALF]
    y_ref[:, HALF:D_MODEL] = (y_hi * inv) * gf_ref[:, HALF:D_MODEL]


def _combine(h, o12, route_t, gf, tile):
    n = h.shape[0]
    tile = min(tile, n)
    nt = n // tile
    return pl.pallas_call(
        _combine_kernel,
        grid=(nt,),
        in_specs=[pl.BlockSpec((tile, D_MODEL), lambda i: (i, 0)),
                  pl.BlockSpec((tile, HALF), lambda i: (i, 0)),
                  pl.BlockSpec((tile, HALF), lambda i: (i + nt, 0)),
                  pl.BlockSpec((tile, ROUTE_ROWS), lambda i: (i, 0)),
                  pl.BlockSpec((1, D_MODEL), lambda i: (0, 0))],
        out_specs=pl.BlockSpec((tile, D_MODEL), lambda i: (i, 0)),
        out_shape=jax.ShapeDtypeStruct((n, D_MODEL), F32),
        compiler_params=_cparams("parallel"),
        name="combine_final_norm",
    )(h, o12, o12, route_t, gf)


def _sparse_moe(xn2p, route, h, w_g, w_u, w_d, gf, run_before_experts):
    n = h.shape[0]
    tm = EXPERT_ROW_TILE
    rows = 2 * n + N_EXPERTS * tm
    rank, cnt = _route_rank(route, 1024)
    counts = cnt[:, 0]
    padded = (counts + tm - 1) // tm * tm
    e_idx = jnp.arange(N_EXPERTS, dtype=jnp.int32)
    starts = jnp.sum(jnp.where(e_idx[None, :] < e_idx[:, None], padded[None, :], 0), axis=1)
    ids = route[0:2].astype(jnp.int32)
    start_of = jnp.sum(jnp.where(ids[None] == e_idx[:, None, None], starts[:, None, None], 0), axis=0)
    pos = (start_of + rank[0:2]).reshape(2 * n)
    xs = _sc_scatter_pairs(xn2p, pos, rows)
    xs, _ = lax.optimization_barrier((xs, run_before_experts))
    os_ = _expert_tiles(starts.astype(jnp.int32), (padded // tm).astype(jnp.int32), xs, w_g, w_u, w_d)
    o12 = _sc_gather_rows(os_, pos)
    return _combine(h, o12, route.T, gf, 512)


def _prep_in_weights(w_in):
    o = 0
    w_u = w_in[:, o:o + SSM_WIDTH]; o += SSM_WIDTH
    w_q = w_in[:, o:o + SWA_WIDTH]; o += SWA_WIDTH
    w_k = w_in[:, o:o + SWA_KV_WIDTH]; o += SWA_KV_WIDTH
    w_v = w_in[:, o:o + SWA_KV_WIDTH]; o += SWA_KV_WIDTH
    w_qm = w_in[:, o:o + MEM_WIDTH]; o += MEM_WIDTH
    w_g = w_in[:, o:]
    wq = (w_q * (SWA_HEAD_DIM ** -0.5)).reshape(D_MODEL, SWA_KV_HEADS, SWA_REP, SWA_HEAD_DIM)
    wq = wq.transpose(0, 2, 1, 3).reshape(D_MODEL, SWA_WIDTH)
    w_main = jnp.concatenate([w_u, wq, w_k, w_v, w_qm], axis=1).astype(BF16)
    return w_main, w_g.astype(BF16)


IN_SPLITS = (SSM_WIDTH, SWA_WIDTH, SWA_KV_WIDTH, SWA_KV_WIDTH, MEM_WIDTH)
IN_DTYPES = ((F32, BF16), (BF16,), (F32,), (F32,), (BF16,))


def kernel(x_prompt, x_sample, cache_swa_k, cache_swa_v, state_ssm_re, state_ssm_im, cache_mem_k, cache_mem_v, mem_prompt, norm1_g, w_in, lam_re, lam_im, log_dt, bm_re, bm_im, cm_re, cm_im, d_skip, w_glu, b_glu, sinks, rel_table, mem_norm_g, w_mem_kv, w_br_ssm, w_br_swa, w_br_mem, w_out, norm2_g, w_rg, b_rg, w_rexp, b_rexp, w_e_gate, w_e_up, w_e_down, final_norm_g):
    nb, t, _ = x_prompt.shape
    ns, ts, _ = x_sample.shape
    l = 0
    L = S5_CHUNK

    w_main, w_gates = _prep_in_weights(w_in[l])
    w_swa = (w_br_swa[l].reshape(SWA_KV_HEADS, SWA_REP, SWA_HEAD_DIM, D_MODEL).transpose(1, 0, 2, 3)
             .reshape(SWA_WIDTH, D_MODEL))
    pad_rows = ROUTER_ROWS - N_EXPERTS - N_EXPERT_GROUPS
    w_router = jnp.concatenate([w_rexp[l].T, w_rg[l].T, jnp.zeros((pad_rows, D_MODEL), F32)], axis=0).astype(BF16)
    b_router = jnp.concatenate([b_rexp[l], b_rg[l], jnp.zeros((pad_rows,), F32)]).reshape(ROUTER_ROWS, 1)
    mp = {
        'g1': norm1_g[l].reshape(1, D_MODEL), 'w_gates': w_gates, 'd_skip': d_skip[l].reshape(1, SSM_WIDTH),
        'w_glu': w_glu[l].astype(BF16), 'b_glu': b_glu[l].reshape(1, SSM_WIDTH),
        'w_br_ssm': w_br_ssm[l].astype(BF16), 'w_br_swa': w_swa.astype(BF16),
        'w_br_mem': w_br_mem[l].astype(BF16), 'w_out': w_out[l].astype(BF16),
        'g2': norm2_g[l].reshape(1, D_MODEL), 'w_router': w_router, 'b_router': b_router,
    }
    w_g, w_u, w_d = w_e_gate[l], w_e_up[l], w_e_down[l]
    gf = final_norm_g.reshape(1, D_MODEL)
    s5_w = _s5_weights(lam_re[l], lam_im[l], log_dt[l], bm_re[l], bm_im[l], cm_re[l], cm_im[l], L)

    bias_p = _rel_bias(rel_table, np.arange(WINDOW)[:, None] + WINDOW - np.arange(2 * WINDOW)[None, :])
    keys_s = WINDOW + 2 * ts
    bias_s = _rel_bias(rel_table, np.arange(ts)[:, None] + WINDOW - np.arange(keys_s)[None, :])
    bias_s = bias_s.reshape(SWA_HEADS * ts, keys_s)
    sink_rows = jnp.repeat(sinks[l].astype(F32), ts).reshape(SWA_HEADS * ts, 1)

    n = nb * t
    xp = x_prompt.reshape(n, D_MODEL)
    mk, mv = _norm_proj(mem_prompt.reshape(nb * MEM_TOKENS, D_MODEL), mem_norm_g[l].reshape(1, D_MODEL),
                        w_mem_kv[l].astype(BF16), (MEM_WIDTH, MEM_WIDTH), ((F32,), (F32,)), 512)
    u, ub, qz, k, v, qm = _norm_proj(xp, mp['g1'], w_main, IN_SPLITS, IN_DTYPES, 512)

    y_ssm, fin = _s5(ub, jnp.zeros((nb, N_CH_TILES * 2 * STATE_TILE), F32), s5_w, nb, t // L, L, 64)
    p_re, p_im = _tiles_to_state(fin)

    o_swa = _swa_prompt(qz, k, v, bias_p, sinks[l].astype(F32), nb, t, 4)
    o_mem = _mem_prompt(qm, mk, mv, nb, t, 512)
    h, xn2p, route = _merge(xp, u, y_ssm, o_swa, o_mem, mp, 512)

    k4 = k.reshape(nb, t, SWA_KV_HEADS, SWA_HEAD_DIM)
    v4 = v.reshape(nb, t, SWA_KV_HEADS, SWA_HEAD_DIM)
    new_k_p, new_v_p = k4[:, -WINDOW:][None], v4[:, -WINDOW:][None]
    new_mk = mk.reshape(1, nb, MEM_TOKENS, MEM_HEADS, MEM_HEAD_DIM)
    new_mv = mv.reshape(1, nb, MEM_TOKENS, MEM_HEADS, MEM_HEAD_DIM)

    m = ns * ts
    xs = x_sample.reshape(m, D_MODEL)
    us, ubs, qzs, k_s, v_s, qms = _norm_proj(xs, mp['g1'], w_main, IN_SPLITS, IN_DTYPES, 256)
    ys_ssm, fins = _s5(ubs, _state_to_tiles(state_ssm_re[l], state_ssm_im[l]), s5_w, ns, ts // L, L, 64)
    s_re, s_im = _tiles_to_state(fins)

    kk_all = jnp.concatenate([cache_swa_k[l].reshape(ns, WINDOW, SWA_KV_WIDTH).astype(F32),
                              k_s.reshape(ns, ts, SWA_KV_WIDTH)], axis=1)
    vv_all = jnp.concatenate([cache_swa_v[l].reshape(ns, WINDOW, SWA_KV_WIDTH).astype(F32),
                              v_s.reshape(ns, ts, SWA_KV_WIDTH)], axis=1)
    pad = jnp.zeros((ns, keys_s - WINDOW - ts, SWA_KV_WIDTH), F32)
    q5 = qzs.reshape(ns, ts, SWA_REP, SWA_KV_HEADS, SWA_HEAD_DIM)
    zq = jnp.zeros((ns, ts, SWA_REP, SWA_HEAD_DIM), BF16)
    q_rows = jnp.concatenate([jnp.concatenate([q5[:, :, :, 0], zq], axis=-1),
                              jnp.concatenate([zq, q5[:, :, :, 1]], axis=-1)], axis=2)
    q_rows = q_rows.transpose(0, 2, 1, 3).reshape(ns, SWA_HEADS * ts, LANES)
    o_dec, roll_k, roll_v = _swa_decode(q_rows, jnp.concatenate([kk_all, pad], axis=1),
                                        jnp.concatenate([vv_all, pad], axis=1),
                                        bias_s, sink_rows, ts, 8)
    o_dec = o_dec.reshape(ns, SWA_KV_HEADS, SWA_REP, ts, SWA_KV_HEADS, SWA_HEAD_DIM)
    o_dec = jnp.stack([o_dec[:, g, :, :, g] for g in range(SWA_KV_HEADS)], axis=1)
    o_swa_s = o_dec.transpose(0, 3, 2, 1, 4).reshape(m, SWA_WIDTH).astype(BF16)

    o_mem_s = _mem_decode(qms.astype(F32).reshape(ns, ts, MEM_WIDTH), cache_mem_k, cache_mem_v, l, 8)
    o_mem_s = o_mem_s.reshape(m, MEM_WIDTH).astype(BF16)

    y_prompt = _sparse_moe(xn2p, route, h, w_g, w_u, w_d, gf, (ys_ssm, o_swa_s, o_mem_s)).reshape(nb, t, D_MODEL)
    hs_, xn2ps, routes = _merge(xs, us, ys_ssm, o_swa_s, o_mem_s, mp, 256)
    y_sample = _moe(xn2ps, routes.T, w_g, w_u, w_d, hs_, gf, 1024).reshape(ns, ts, D_MODEL)

    new_k_s = roll_k.reshape(1, ns, WINDOW, SWA_KV_HEADS, SWA_HEAD_DIM).astype(cache_swa_k.dtype)
    new_v_s = roll_v.reshape(1, ns, WINDOW, SWA_KV_HEADS, SWA_HEAD_DIM).astype(cache_swa_v.dtype)

    return (y_prompt, y_sample,
            new_k_p, new_v_p, p_re[None], p_im[None], new_mk, new_mv,
            new_k_s, new_v_s, s_re[None].astype(state_ssm_re.dtype), s_im[None].astype(state_ssm_im.dtype))
```

```python
import functools
import math

import numpy as np
import jax
import jax.numpy as jnp
from jax import lax
from jax.experimental import pallas as pl
from jax.experimental.pallas import tpu as pltpu
from jax.experimental.pallas import tpu_sc as plsc

F32 = jnp.float32
BF16 = jnp.bfloat16

D_MODEL = 1024
SSM_WIDTH = 512
SSM_GROUP = 16
SSM_GROUPS = 32
SSM_STATE = 64
SWA_HEADS = 8
SWA_KV_HEADS = 2
SWA_REP = 4
SWA_HEAD_DIM = 64
SWA_WIDTH = 512
SWA_KV_WIDTH = 128
WINDOW = 128
REL_BUCKETS = 32
REL_MAX_DIST = 128
MEM_TOKENS = 256
MEM_HEADS = 4
MEM_HEAD_DIM = 128
MEM_WIDTH = 512
N_EXPERT_GROUPS = 4
EXPERTS_PER_GROUP = 8
N_EXPERTS = 32
D_EXPERT = 256
EPS = 1e-6
NEG_INF = -1e30

LANES = 128
GROUPS_PER_TILE = LANES // SSM_GROUP
N_CH_TILES = SSM_WIDTH // LANES
STATE_TILE = GROUPS_PER_TILE * SSM_STATE
VMEM_LIMIT = 56 * 1024 * 1024
S5_CHUNK = 8
S5_PANEL = 256

_TRANS_B = (((1,), (1,)), ((), ()))


def _cparams(*sem):
    return pltpu.CompilerParams(dimension_semantics=sem, vmem_limit_bytes=VMEM_LIMIT)


def _rms(x, g):
    return (x * lax.rsqrt(jnp.mean(x * x, axis=-1, keepdims=True) + EPS)) * g


def _dot(a, b):
    return jnp.dot(a, b, preferred_element_type=F32)


def _norm_proj_kernel(x_ref, g_ref, w_ref, *out_refs, splits, dtypes):
    xb = _rms(x_ref[...], g_ref[...]).astype(BF16)
    off = 0
    outs = iter(out_refs)
    for width, dts in zip(splits, dtypes):
        r = _dot(xb, w_ref[:, off:off + width])
        for dt in dts:
            next(outs)[...] = r.astype(dt)
        off += width


def _norm_proj(x, g, w, splits, dtypes, tile):
    n, d = x.shape
    tile = min(tile, n)
    flat = [(wd, dt) for wd, dts in zip(splits, dtypes) for dt in dts]
    return pl.pallas_call(
        functools.partial(_norm_proj_kernel, splits=tuple(splits), dtypes=tuple(dtypes)),
        grid=(n // tile,),
        in_specs=[pl.BlockSpec((tile, d), lambda i: (i, 0)),
                  pl.BlockSpec((1, d), lambda i: (0, 0)),
                  pl.BlockSpec((d, sum(splits)), lambda i: (0, 0), pipeline_mode=pl.Buffered(1))],
        out_specs=[pl.BlockSpec((tile, wd), lambda i: (i, 0)) for wd, _ in flat],
        out_shape=[jax.ShapeDtypeStruct((n, wd), dt) for wd, dt in flat],
        compiler_params=_cparams("parallel"),
        name="norm_proj",
    )(x, g, w)


def _s5_weights(lam_re, lam_im, log_dt, bm_re, bm_im, cm_re, cm_im, L):
    hp = lax.Precision.HIGHEST
    nt, gt, P, H = N_CH_TILES, GROUPS_PER_TILE, SSM_STATE, SSM_GROUP
    lr, li = lam_re.astype(F32), lam_im.astype(F32)
    dt = jnp.exp(log_dt.astype(F32))[:, None]
    mag = jnp.exp(lr * dt)
    a_re = mag * jnp.cos(li * dt)
    a_im = mag * jnp.sin(li * dt)
    den = lr * lr + li * li
    f_re = ((a_re - 1.0) * lr + a_im * li) / den
    f_im = (a_im * lr - (a_re - 1.0) * li) / den
    br, bi = bm_re.astype(F32), bm_im.astype(F32)
    bb_re = f_re[..., None] * br - f_im[..., None] * bi
    bb_im = f_re[..., None] * bi + f_im[..., None] * br
    pr, pi = [jnp.ones_like(a_re)], [jnp.zeros_like(a_im)]
    for _ in range(L):
        pr.append(pr[-1] * a_re - pi[-1] * a_im)
        pi.append(pr[-2] * a_im + pi[-1] * a_re)
    ap_re, ap_im = jnp.stack(pr), jnp.stack(pi)
    cr, ci = cm_re.astype(F32), cm_im.astype(F32)
    ca_re = cr[None] * ap_re[:, :, None, :] - ci[None] * ap_im[:, :, None, :]
    ca_im = cr[None] * ap_im[:, :, None, :] + ci[None] * ap_re[:, :, None, :]

    rev_re = jnp.stack([pr[L - 1 - s] for s in range(L)])
    rev_im = jnp.stack([pi[L - 1 - s] for s in range(L)])
    ws_re = rev_re[..., None] * bb_re[None] - rev_im[..., None] * bb_im[None]
    ws_im = rev_re[..., None] * bb_im[None] + rev_im[..., None] * bb_re[None]
    c_st = jnp.concatenate([ws_re.transpose(0, 1, 3, 2).reshape(L, nt, gt * H, P),
                            ws_im.transpose(0, 1, 3, 2).reshape(L, nt, gt * H, P)], axis=3).transpose(1, 0, 2, 3)
    so = lambda ca: ca[1:].transpose(1, 3, 0, 2).reshape(nt, gt * P, L * H)
    c_so = jnp.concatenate([so(ca_re), so(-ca_im)], axis=1)
    k_lag = (jnp.einsum('tghp,gpk->gkth', ca_re[:L], bb_re, precision=hp)
             - jnp.einsum('tghp,gpk->gkth', ca_im[:L], bb_im, precision=hp))
    c_k = k_lag.reshape(nt, gt * H, L * H)
    w_st, w_out, toep = _s5_expand(c_st, c_so, c_k, L)

    def per_tile(v):
        return v.reshape(nt, 1, STATE_TILE)

    return w_st, w_out, toep, per_tile(pr[L]), per_tile(pi[L])


def _s5_expand_kernel(cst_ref, cso_ref, ck_ref, wst_ref, wso_ref, toep_ref, *, L):
    hp = lax.Precision.HIGHEST
    P, H = SSM_STATE, SSM_GROUP
    iota = lambda shape, d: lax.broadcasted_iota(jnp.int32, shape, d)
    one = lambda cond: jnp.where(cond, 1.0, 0.0).astype(F32)

    r, c = iota((2 * P, 2 * STATE_TILE), 0), iota((2 * P, 2 * STATE_TILE), 1)
    rep_st = one((r // P == c // STATE_TILE) & (r % P == c % P))
    r, c = iota((LANES, 2 * STATE_TILE), 0), iota((LANES, 2 * STATE_TILE), 1)
    own_st = one(r // H == (c % STATE_TILE) // P)
    for s in range(L):
        blk = jnp.dot(cst_ref[s], rep_st, precision=hp, preferred_element_type=F32) * own_st
        wst_ref[s * LANES:(s + 1) * LANES, :] = blk.astype(BF16)

    r, c = iota((LANES, LANES), 0), iota((LANES, LANES), 1)
    pick = [one((r // H == t) & (r % H == c % H)) for t in range(L)]
    own_k = one(r // H == c // H)
    r, c = iota((2 * STATE_TILE, LANES), 0), iota((2 * STATE_TILE, LANES), 1)
    own_so = one((r % STATE_TILE) // P == c // H)
    cso = cso_ref[...]
    for t in range(L):
        blk = jnp.dot(cso, pick[t], precision=hp, preferred_element_type=F32) * own_so
        wso_ref[:, t * LANES:(t + 1) * LANES] = blk.astype(BF16)
    ck = ck_ref[...]
    lag = [(jnp.dot(ck, pick[t], precision=hp, preferred_element_type=F32) * own_k).astype(BF16) for t in range(L)]
    zero = jnp.zeros((LANES, LANES), BF16)
    for s in range(L):
        for t in range(L):
            toep_ref[s * LANES:(s + 1) * LANES, t * LANES:(t + 1) * LANES] = lag[t - s] if t >= s else zero


def _s5_expand(c_st, c_so, c_k, L):
    lk = L * LANES
    st2 = 2 * STATE_TILE
    return pl.pallas_call(
        functools.partial(_s5_expand_kernel, L=L),
        grid=(N_CH_TILES,),
        in_specs=[pl.BlockSpec((None, L, LANES, 2 * SSM_STATE), lambda j: (j, 0, 0, 0)),
                  pl.BlockSpec((None, st2, L * SSM_GROUP), lambda j: (j, 0, 0)),
                  pl.BlockSpec((None, LANES, L * SSM_GROUP), lambda j: (j, 0, 0))],
        out_specs=[pl.BlockSpec((None, lk, st2), lambda j: (j, 0, 0)),
                   pl.BlockSpec((None, st2, lk), lambda j: (j, 0, 0)),
                   pl.BlockSpec((None, lk, lk), lambda j: (j, 0, 0))],
        out_shape=[jax.ShapeDtypeStruct((N_CH_TILES, lk, st2), BF16),
                   jax.ShapeDtypeStruct((N_CH_TILES, st2, lk), BF16),
                   jax.ShapeDtypeStruct((N_CH_TILES, lk, lk), BF16)],
        compiler_params=_cparams("parallel"),
        name="s5_expand_weights",
    )(c_st, c_so, c_k)


def _to_chunks(u, nb, nc, L):
    return (u.reshape(nb, nc, L, N_CH_TILES, LANES).transpose(1, 0, 3, 2, 4)
            .reshape(nc * nb, N_CH_TILES * L * LANES))


def _from_chunks(y, nb, nc, L):
    return (y.reshape(nc, nb, N_CH_TILES, L, LANES).transpose(1, 0, 3, 2, 4)
            .reshape(nb * nc * L, SSM_WIDTH))


def _s5_kernel(x_ref, h0_ref, are_ref, aim_ref, ws_ref, t_ref, wo_ref, y_ref, fin_ref,
               hr_ref, hi_ref, d_ref, hs_ref, *, cb, nb):
    ci = pl.program_id(1)

    @pl.when(ci == 0)
    def _():
        hr_ref[...] = h0_ref[:, 0:STATE_TILE]
        hi_ref[...] = h0_ref[:, STATE_TILE:2 * STATE_TILE]

    x = x_ref[...]
    d_ref[...] = _dot(x, ws_ref[...])
    ar = jnp.broadcast_to(are_ref[...], (nb, STATE_TILE))
    ai = jnp.broadcast_to(aim_ref[...], (nb, STATE_TILE))

    def body(c, carry):
        hr, hi = carry
        r0 = pl.multiple_of(c * nb, nb)
        hs_ref[pl.ds(r0, nb), 0:STATE_TILE] = hr
        hs_ref[pl.ds(r0, nb), STATE_TILE:2 * STATE_TILE] = hi
        d = d_ref[pl.ds(r0, nb), :]
        return (ar * hr - ai * hi + d[:, 0:STATE_TILE],
                ar * hi + ai * hr + d[:, STATE_TILE:2 * STATE_TILE])

    hr, hi = lax.fori_loop(0, cb, body, (hr_ref[...], hi_ref[...]))
    hr_ref[...] = hr
    hi_ref[...] = hi
    hsb = hs_ref[...].astype(BF16)
    for c0 in range(0, t_ref.shape[1], S5_PANEL):
        c1 = c0 + S5_PANEL
        y_ref[:, c0:c1] = _dot(x[:, 0:c1], t_ref[0:c1, c0:c1]) + _dot(hsb, wo_ref[:, c0:c1])

    @pl.when(ci == pl.num_programs(1) - 1)
    def _():
        fin_ref[:, 0:STATE_TILE] = hr
        fin_ref[:, STATE_TILE:2 * STATE_TILE] = hi


def _s5(ub, h0, weights, nb, nc, L, chunk_block):
    w_st, w_so, toep, a_re, a_im = weights
    xc = _to_chunks(ub, nb, nc, L)
    cb = min(chunk_block, nc)
    rows = cb * nb
    lk = L * LANES
    st2 = 2 * STATE_TILE
    tile_w = lambda shape: pl.BlockSpec((None,) + shape, lambda j, c: (j, 0, 0))
    y, fin = pl.pallas_call(
        functools.partial(_s5_kernel, cb=cb, nb=nb),
        grid=(N_CH_TILES, nc // cb),
        in_specs=[pl.BlockSpec((rows, lk), lambda j, c: (c, j)),
                  pl.BlockSpec((nb, st2), lambda j, c: (0, j)),
                  tile_w((1, STATE_TILE)), tile_w((1, STATE_TILE)),
                  tile_w((lk, st2)), tile_w((lk, lk)), tile_w((st2, lk))],
        out_specs=[pl.BlockSpec((rows, lk), lambda j, c: (c, j)),
                   pl.BlockSpec((nb, st2), lambda j, c: (0, j))],
        out_shape=[jax.ShapeDtypeStruct((nc * nb, N_CH_TILES * lk), F32),
                   jax.ShapeDtypeStruct((nb, N_CH_TILES * st2), F32)],
        scratch_shapes=[pltpu.VMEM((nb, STATE_TILE), F32), pltpu.VMEM((nb, STATE_TILE), F32),
                        pltpu.VMEM((rows, st2), F32), pltpu.VMEM((rows, st2), F32)],
        compiler_params=_cparams("parallel", "arbitrary"),
        name="s5_chunked_scan",
    )(xc, h0, a_re, a_im, w_st, toep, w_so)
    return _from_chunks(y, nb, nc, L), fin


def _state_to_tiles(h_re, h_im):
    nb = h_re.shape[0]
    r = h_re.astype(F32).reshape(nb, N_CH_TILES, STATE_TILE)
    i = h_im.astype(F32).reshape(nb, N_CH_TILES, STATE_TILE)
    return jnp.concatenate([r, i], axis=-1).reshape(nb, N_CH_TILES * 2 * STATE_TILE)


def _tiles_to_state(h):
    nb = h.shape[0]
    h = h.reshape(nb, N_CH_TILES, 2, GROUPS_PER_TILE, SSM_STATE)
    return (h[:, :, 0].reshape(nb, SSM_GROUPS, SSM_STATE), h[:, :, 1].reshape(nb, SSM_GROUPS, SSM_STATE))


def _t5_bucket(dist):
    n = np.maximum(dist, 0)
    max_exact = REL_BUCKETS // 2
    nf = np.maximum(n, 1).astype(np.float32)
    large = max_exact + (np.log(nf / np.float32(max_exact)) / np.float32(math.log(REL_MAX_DIST / max_exact))
                         * np.float32(REL_BUCKETS - max_exact)).astype(np.int32)
    large = np.minimum(large, REL_BUCKETS - 1)
    return np.where(n < max_exact, n, large)


def _rel_bias(rel_table, dist):
    bucket = _t5_bucket(dist)
    tab = rel_table.astype(F32)
    out = jnp.zeros((SWA_HEADS,) + dist.shape, F32)
    for b in range(REL_BUCKETS):
        sel = jnp.asarray(bucket == b)
        if bool((bucket == b).any()):
            out = jnp.where(sel[None], tab[b].reshape((SWA_HEADS,) + (1,) * dist.ndim), out)
    return out


def _swa_prompt_kernel(sink_ref, q_ref, kp_ref, kc_ref, vp_ref, vc_ref, bias_ref, o_ref, kk_ref, vv_ref, *, qblocks):
    step = pl.program_id(1)
    kk_ref[0:WINDOW, :] = kp_ref[...].astype(BF16)
    kk_ref[WINDOW:, :] = kc_ref[...].astype(BF16)
    vv_ref[0:WINDOW, :] = vp_ref[...].astype(BF16)
    vv_ref[WINDOW:, :] = vc_ref[...].astype(BF16)
    row = lax.broadcasted_iota(jnp.int32, (WINDOW, 2 * WINDOW), 0)
    col = lax.broadcasted_iota(jnp.int32, (WINDOW, 2 * WINDOW), 1)
    dist = row + WINDOW - col
    band = (dist >= 0) & (dist < WINDOW)
    lane = lax.broadcasted_iota(jnp.int32, (WINDOW, LANES), 1)
    low = lane < SWA_HEAD_DIM

    def block(j, carry):
        r0 = pl.multiple_of(j * WINDOW, WINDOW)
        kk = kk_ref[pl.ds(r0, 2 * WINDOW), :]
        vv = vv_ref[pl.ds(r0, 2 * WINDOW), :]
        valid = band & ((col >= WINDOW) | (step * qblocks + j > 0))
        for t in range(SWA_REP):
            q2 = q_ref[pl.ds(r0, WINDOW), t * LANES:(t + 1) * LANES]
            outs = []
            for half in range(SWA_KV_HEADS):
                h = t + SWA_REP * half
                qh = jnp.where(low if half == 0 else jnp.logical_not(low), q2, jnp.zeros_like(q2))
                s = lax.dot_general(qh, kk, _TRANS_B, preferred_element_type=F32)
                s = jnp.where(valid, s + bias_ref[h], NEG_INF)
                sink = sink_ref[h]
                m = jnp.maximum(jnp.max(s, axis=-1, keepdims=True), sink)
                e = jnp.exp(s - m)
                den = jnp.sum(e, axis=-1, keepdims=True) + jnp.exp(sink - m)
                outs.append(_dot(e.astype(BF16), vv) * (1.0 / den))
            o_ref[pl.ds(r0, WINDOW), t * LANES:(t + 1) * LANES] = jnp.where(low, outs[0], outs[1]).astype(BF16)
        return carry

    lax.fori_loop(0, qblocks, block, 0)


def _swa_prompt(q, k, v, bias, sinks, nb, t, qblocks):
    nstep = t // (WINDOW * qblocks)
    rows = WINDOW * qblocks
    cur = lambda b, i: (b * nstep + i, 0)
    prev = lambda b, i: (b * nstep * qblocks + jnp.maximum(i * qblocks - 1, 0), 0)
    return pl.pallas_call(
        functools.partial(_swa_prompt_kernel, qblocks=qblocks),
        grid=(nb, nstep),
        in_specs=[pl.BlockSpec(memory_space=pltpu.SMEM),
                  pl.BlockSpec((rows, SWA_WIDTH), cur),
                  pl.BlockSpec((WINDOW, SWA_KV_WIDTH), prev),
                  pl.BlockSpec((rows, SWA_KV_WIDTH), cur),
                  pl.BlockSpec((WINDOW, SWA_KV_WIDTH), prev),
                  pl.BlockSpec((rows, SWA_KV_WIDTH), cur),
                  pl.BlockSpec((SWA_HEADS, WINDOW, 2 * WINDOW), lambda b, i: (0, 0, 0))],
        out_specs=pl.BlockSpec((rows, SWA_WIDTH), cur),
        out_shape=jax.ShapeDtypeStruct((nb * t, SWA_WIDTH), BF16),
        scratch_shapes=[pltpu.VMEM((rows + WINDOW, SWA_KV_WIDTH), BF16),
                        pltpu.VMEM((rows + WINDOW, SWA_KV_WIDTH), BF16)],
        compiler_params=_cparams("parallel", "parallel"),
        name="swa_prompt",
    )(sinks, q, k, k, v, v, bias)


def _swa_decode_kernel(q_ref, k_ref, v_ref, bias_ref, sink_ref, o_ref, nk_ref, nv_ref, *, seqs, tq):
    rows, keys = q_ref.shape[1], k_ref.shape[1]
    nk_ref[...] = k_ref[:, tq:tq + WINDOW, :]
    nv_ref[...] = v_ref[:, tq:tq + WINDOW, :]
    qi = lax.broadcasted_iota(jnp.int32, (rows, keys), 0) % tq
    col = lax.broadcasted_iota(jnp.int32, (rows, keys), 1)
    dist = qi + WINDOW - col
    valid = (dist >= 0) & (dist < WINDOW)
    bias = bias_ref[...]
    sink = sink_ref[...]
    for s_i in range(seqs):
        kk = k_ref[s_i].astype(BF16)
        s = lax.dot_general(q_ref[s_i], kk, _TRANS_B, preferred_element_type=F32)
        s = jnp.where(valid, s + bias, NEG_INF)
        m = jnp.maximum(jnp.max(s, axis=-1, keepdims=True), sink)
        e = jnp.exp(s - m)
        den = jnp.sum(e, axis=-1, keepdims=True) + jnp.exp(sink - m)
        o_ref[s_i] = _dot(e.astype(BF16), v_ref[s_i].astype(BF16)) * (1.0 / den)


def _swa_decode(qz, k_all, v_all, bias, sink_rows, tq, seqs):
    nseq, rows, _ = qz.shape
    keys = k_all.shape[1]
    seqs = min(seqs, nseq)
    return pl.pallas_call(
        functools.partial(_swa_decode_kernel, seqs=seqs, tq=tq),
        grid=(nseq // seqs,),
        in_specs=[pl.BlockSpec((seqs, rows, LANES), lambda i: (i, 0, 0)),
                  pl.BlockSpec((seqs, keys, LANES), lambda i: (i, 0, 0)),
                  pl.BlockSpec((seqs, keys, LANES), lambda i: (i, 0, 0)),
                  pl.BlockSpec((rows, keys), lambda i: (0, 0)),
                  pl.BlockSpec((rows, 1), lambda i: (0, 0))],
        out_specs=[pl.BlockSpec((seqs, rows, LANES), lambda i: (i, 0, 0)),
                   pl.BlockSpec((seqs, WINDOW, LANES), lambda i: (i, 0, 0)),
                   pl.BlockSpec((seqs, WINDOW, LANES), lambda i: (i, 0, 0))],
        out_shape=[jax.ShapeDtypeStruct((nseq, rows, LANES), F32),
                   jax.ShapeDtypeStruct((nseq, WINDOW, LANES), F32),
                   jax.ShapeDtypeStruct((nseq, WINDOW, LANES), F32)],
        compiler_params=_cparams("parallel"),
        name="swa_decode",
    )(qz, k_all, v_all, bias, sink_rows)


def _softmax(s):
    m = jnp.max(s, axis=-1, keepdims=True)
    e = jnp.exp(s - m)
    return e * (1.0 / jnp.sum(e, axis=-1, keepdims=True))


def _mem_prompt_kernel(q_ref, k_ref, v_ref, o_ref, s_ref, p_ref):
    scale = MEM_HEAD_DIM ** -0.5
    heads = [slice(h * MEM_HEAD_DIM, (h + 1) * MEM_HEAD_DIM) for h in range(MEM_HEADS)]
    for h, sl in enumerate(heads):
        s_ref[h] = lax.dot_general(q_ref[:, sl], k_ref[:, sl].astype(BF16), _TRANS_B, preferred_element_type=F32)
    s = s_ref[...] * scale
    e = jnp.exp(s - jnp.max(s, axis=-1, keepdims=True))
    p_ref[...] = e.astype(BF16)
    inv = 1.0 / jnp.sum(e, axis=-1, keepdims=True)
    for h, sl in enumerate(heads):
        o_ref[:, sl] = (_dot(p_ref[h], v_ref[:, sl].astype(BF16)) * inv[h]).astype(BF16)


def _mem_prompt(qm, mk, mv, nb, t, tile):
    tile = min(tile, t)
    nt = t // tile
    return pl.pallas_call(
        _mem_prompt_kernel,
        grid=(nb, nt),
        in_specs=[pl.BlockSpec((tile, MEM_WIDTH), lambda b, i: (b * nt + i, 0)),
                  pl.BlockSpec((MEM_TOKENS, MEM_WIDTH), lambda b, i: (b, 0)),
                  pl.BlockSpec((MEM_TOKENS, MEM_WIDTH), lambda b, i: (b, 0))],
        out_specs=pl.BlockSpec((tile, MEM_WIDTH), lambda b, i: (b * nt + i, 0)),
        out_shape=jax.ShapeDtypeStruct((nb * t, MEM_WIDTH), BF16),
        scratch_shapes=[pltpu.VMEM((MEM_HEADS, tile, MEM_TOKENS), F32), pltpu.VMEM((MEM_HEADS, tile, MEM_TOKENS), BF16)],
        compiler_params=_cparams("parallel", "parallel"),
        name="mem_prompt",
    )(qm, mk, mv)


def _mem_decode_kernel(q_ref, k_ref, v_ref, o_ref, *, seqs):
    tq = q_ref.shape[1]
    rows, cols = MEM_HEADS * tq, MEM_TOKENS * MEM_HEADS
    k2 = k_ref.reshape(seqs, cols, MEM_HEAD_DIM)
    v2 = v_ref.reshape(seqs, cols, MEM_HEAD_DIM)
    scale = MEM_HEAD_DIM ** -0.5
    own = (lax.broadcasted_iota(jnp.int32, (rows, cols), 1) % MEM_HEADS
           == lax.broadcasted_iota(jnp.int32, (rows, cols), 0) // tq)
    for s_i in range(seqs):
        q = q_ref[s_i]
        qb = jnp.concatenate([q[:, h * MEM_HEAD_DIM:(h + 1) * MEM_HEAD_DIM] for h in range(MEM_HEADS)], axis=0)
        s = lax.dot_general(qb.astype(BF16), k2[s_i].astype(BF16), _TRANS_B, preferred_element_type=F32) * scale
        p = _softmax(jnp.where(own, s, NEG_INF)).astype(BF16)
        o = _dot(p, v2[s_i].astype(BF16))
        for h in range(MEM_HEADS):
            o_ref[s_i, :, h * MEM_HEAD_DIM:(h + 1) * MEM_HEAD_DIM] = o[h * tq:(h + 1) * tq, :]


def _mem_decode(q, k, v, layer, seqs):
    nseq, tq, _ = q.shape
    seqs = min(seqs, nseq)
    cache = pl.BlockSpec((None, seqs, MEM_TOKENS, MEM_HEADS, MEM_HEAD_DIM), lambda i: (layer, i, 0, 0, 0))
    return pl.pallas_call(
        functools.partial(_mem_decode_kernel, seqs=seqs),
        grid=(nseq // seqs,),
        in_specs=[pl.BlockSpec((seqs, tq, MEM_WIDTH), lambda i: (i, 0, 0)), cache, cache],
        out_specs=pl.BlockSpec((seqs, tq, MEM_WIDTH), lambda i: (i, 0, 0)),
        out_shape=jax.ShapeDtypeStruct((nseq, tq, MEM_WIDTH), F32),
        compiler_params=_cparams("parallel"),
        name="mem_decode",
    )(q, k, v)


ROUTER_ROWS = 40
GATES_COL0 = SSM_WIDTH + SWA_WIDTH + 2 * SWA_KV_WIDTH + MEM_WIDTH
ROUTE_ROWS = 8
HALF = D_MODEL // 2


def _pack_halves(xb):
    hi = pltpu.bitcast(xb[:, 0:HALF].astype(F32), jnp.int32)
    lo = pltpu.bitcast(xb[:, HALF:D_MODEL].astype(F32), jnp.int32)
    return hi | lax.shift_right_logical(lo, jnp.int32(16))


def _unpack_halves(p):
    hi = pltpu.bitcast(p & jnp.int32(-65536), F32).astype(BF16)
    lo = pltpu.bitcast(lax.shift_left(p, jnp.int32(16)), F32).astype(BF16)
    return hi, lo


def _merge_kernel(x_ref, u_ref, y_ref, os_ref, om_ref, g1_ref, wg_ref, dsk_ref, wglu_ref, bglu_ref,
                  wbs_ref, wbw_ref, wbm_ref, wout_ref, g2_ref, wr_ref, br_ref,
                  h_ref, xn2_ref, route_ref):
    x = x_ref[...]
    tt = x.shape[0]
    xb = _rms(x, g1_ref[...]).astype(BF16)
    z = jax.nn.gelu(y_ref[...] + dsk_ref[...] * u_ref[...])
    z = z * jax.nn.sigmoid(_dot(z.astype(BF16), wglu_ref[...]) + bglu_ref[...])
    gate = lambda b: jax.nn.sigmoid(_dot(xb, wg_ref[:, GATES_COL0 + b * D_MODEL:GATES_COL0 + (b + 1) * D_MODEL]))
    merged = gate(0) * _dot(z.astype(BF16), wbs_ref[...])
    merged = merged + gate(1) * _dot(os_ref[...], wbw_ref[...])
    merged = merged + gate(2) * _dot(om_ref[...], wbm_ref[...])
    h = x + _dot(merged.astype(BF16), wout_ref[...])
    h_ref[...] = h
    xn2 = _rms(h, g2_ref[...]).astype(BF16)
    xn2_ref[...] = _pack_halves(xn2)

    lt = lax.dot_general(wr_ref[...], xn2, _TRANS_B, preferred_element_type=F32) + br_ref[...]
    gl = lt[N_EXPERTS:N_EXPERTS + N_EXPERT_GROUPS]
    ge = jnp.exp(gl - jnp.max(gl, axis=0, keepdims=True))
    gp = ge / jnp.sum(ge, axis=0, keepdims=True)
    gw = jnp.max(gp, axis=0, keepdims=True)
    gidx = jnp.full((1, tt), N_EXPERT_GROUPS - 1, jnp.int32)
    for r in range(N_EXPERT_GROUPS - 2, -1, -1):
        gidx = jnp.where(gp[r:r + 1] == gw, r, gidx)
    ein = lt[(N_EXPERT_GROUPS - 1) * EXPERTS_PER_GROUP:N_EXPERTS]
    for r in range(N_EXPERT_GROUPS - 2, -1, -1):
        ein = jnp.where(gidx == r, lt[r * EXPERTS_PER_GROUP:(r + 1) * EXPERTS_PER_GROUP], ein)
    ee = jnp.exp(ein - jnp.max(ein, axis=0, keepdims=True))
    ep = ee / jnp.sum(ee, axis=0, keepdims=True)
    rowi = lax.broadcasted_iota(jnp.int32, (EXPERTS_PER_GROUP, tt), 0)
    p1 = jnp.max(ep, axis=0, keepdims=True)
    e1 = jnp.min(jnp.where(ep == p1, rowi, EXPERTS_PER_GROUP), axis=0, keepdims=True)
    ep2 = jnp.where(rowi == e1, -1.0, ep)
    p2 = jnp.max(ep2, axis=0, keepdims=True)
    e2 = jnp.min(jnp.where(ep2 == p2, rowi, EXPERTS_PER_GROUP), axis=0, keepdims=True)
    tot = p1 + p2
    w1 = p1 / tot * gw
    w2 = p2 / tot * gw
    id1 = (gidx * EXPERTS_PER_GROUP + e1).astype(F32)
    id2 = (gidx * EXPERTS_PER_GROUP + e2).astype(F32)
    route_ref[...] = jnp.concatenate([id1, id2, w1, w2, jnp.zeros((ROUTE_ROWS - 4, tt), F32)], axis=0)


def _merge(x, u, y, o_swa, o_mem, p, tile):
    n = x.shape[0]
    tile = min(tile, n)
    row = lambda i: (i, 0)
    const = lambda i: (0, 0)
    full = lambda a: pl.BlockSpec(a.shape, const, pipeline_mode=pl.Buffered(1))
    weights = [p['g1'], p['w_gates'], p['d_skip'], p['w_glu'], p['b_glu'], p['w_br_ssm'], p['w_br_swa'],
               p['w_br_mem'], p['w_out'], p['g2'], p['w_router'], p['b_router']]
    return pl.pallas_call(
        _merge_kernel,
        grid=(n // tile,),
        in_specs=[pl.BlockSpec((tile, D_MODEL), row), pl.BlockSpec((tile, SSM_WIDTH), row),
                  pl.BlockSpec((tile, SSM_WIDTH), row), pl.BlockSpec((tile, SWA_WIDTH), row),
                  pl.BlockSpec((tile, MEM_WIDTH), row)] + [full(w) for w in weights],
        out_specs=[pl.BlockSpec((tile, D_MODEL), row), pl.BlockSpec((tile, HALF), row),
                   pl.BlockSpec((ROUTE_ROWS, tile), lambda i: (0, i))],
        out_shape=[jax.ShapeDtypeStruct((n, D_MODEL), F32), jax.ShapeDtypeStruct((n, HALF), jnp.int32),
                   jax.ShapeDtypeStruct((ROUTE_ROWS, n), F32)],
        compiler_params=_cparams("parallel"),
        name="merge_router",
    )(x, u, y, o_swa, o_mem, *weights)


def _expert_mlp(xp, wg, wu, wd):
    hi, lo = _unpack_halves(xp)
    g = _dot(hi, wg[0:HALF, :]) + _dot(lo, wg[HALF:D_MODEL, :])
    u = _dot(hi, wu[0:HALF, :]) + _dot(lo, wu[HALF:D_MODEL, :])
    hh = jax.nn.silu(g) * u
    return _dot(hh.astype(BF16), wd[...])


def _moe_kernel(xn2_ref, rt_ref, wg_ref, wu_ref, wd_ref, h_ref, gf_ref, o_ref, acc_ref):
    e = pl.program_id(1)

    @pl.when(e == 0)
    def _():
        acc_ref[...] = jnp.zeros_like(acc_ref)

    o = _expert_mlp(xn2_ref[...], wg_ref[...].astype(BF16), wu_ref[...].astype(BF16), wd_ref[...].astype(BF16))
    ef = e.astype(F32)
    c = (jnp.where(rt_ref[:, 0:1] == ef, rt_ref[:, 2:3], 0.0)
         + jnp.where(rt_ref[:, 1:2] == ef, rt_ref[:, 3:4], 0.0))
    acc_ref[...] += c * o

    @pl.when(e == N_EXPERTS - 1)
    def _():
        o_ref[...] = _rms(h_ref[...] + acc_ref[...], gf_ref[...])


def _moe(xn2, route_t, w_g, w_u, w_d, h, gf, tile):
    n = h.shape[0]
    tile = min(tile, n)
    return pl.pallas_call(
        _moe_kernel,
        grid=(n // tile, N_EXPERTS),
        in_specs=[pl.BlockSpec((tile, HALF), lambda i, e: (i, 0)),
                  pl.BlockSpec((tile, ROUTE_ROWS), lambda i, e: (i, 0)),
                  pl.BlockSpec((None, D_MODEL, D_EXPERT), lambda i, e: (e, 0, 0)),
                  pl.BlockSpec((None, D_MODEL, D_EXPERT), lambda i, e: (e, 0, 0)),
                  pl.BlockSpec((None, D_EXPERT, D_MODEL), lambda i, e: (e, 0, 0)),
                  pl.BlockSpec((tile, D_MODEL), lambda i, e: (i, 0)),
                  pl.BlockSpec((1, D_MODEL), lambda i, e: (0, 0))],
        out_specs=pl.BlockSpec((tile, D_MODEL), lambda i, e: (i, 0)),
        out_shape=jax.ShapeDtypeStruct((n, D_MODEL), F32),
        scratch_shapes=[pltpu.VMEM((tile, D_MODEL), F32)],
        compiler_params=_cparams("parallel", "arbitrary"),
        name="moe_final_norm",
    )(xn2, route_t, w_g, w_u, w_d, h, gf)


EXPERT_ROW_TILE = 256
EXPERT_SLOTS = 4
SC_CORES = 2
SC_SUBCORES = 16
SC_WORKERS = SC_CORES * SC_SUBCORES
SC_SCATTER_ROWS = 64
SC_GATHER_ROWS = 64


def _route_rank_kernel(r_ref, rank_ref, cnt_ref, base_ref):
    i = pl.program_id(0)
    tt = r_ref.shape[1]

    @pl.when(i == 0)
    def _():
        base_ref[...] = jnp.zeros_like(base_ref)

    ids = r_ref[0:2, :].astype(jnp.int32)
    e_iota = lax.broadcasted_iota(jnp.int32, (N_EXPERTS, tt), 0)
    oh1 = jnp.where(e_iota == ids[0:1], 1.0, 0.0)
    oh2 = jnp.where(e_iota == ids[1:2], 1.0, 0.0)
    before = (lax.broadcasted_iota(jnp.int32, (tt, tt), 0) < lax.broadcasted_iota(jnp.int32, (tt, tt), 1))
    tri = jnp.where(before, 1.0, 0.0).astype(BF16)
    c1 = _dot(oh1.astype(BF16), tri)
    c2 = _dot(oh2.astype(BF16), tri)
    tot1 = jnp.sum(oh1, axis=1, keepdims=True)
    tot2 = jnp.sum(oh2, axis=1, keepdims=True)
    base = base_ref[:, 0:1]
    rank1 = jnp.sum(oh1 * (base + c1), axis=0, keepdims=True)
    rank2 = jnp.sum(oh2 * (base + tot1 + c2), axis=0, keepdims=True)
    rank_ref[...] = jnp.concatenate([rank1, rank2, jnp.zeros((ROUTE_ROWS - 2, tt), F32)], axis=0).astype(jnp.int32)
    new_base = jnp.broadcast_to(base + tot1 + tot2, base_ref.shape)
    base_ref[...] = new_base
    cnt_ref[...] = new_base.astype(jnp.int32)


def _route_rank(route, tile):
    n = route.shape[1]
    tile = min(tile, n)
    return pl.pallas_call(
        _route_rank_kernel,
        grid=(n // tile,),
        in_specs=[pl.BlockSpec((ROUTE_ROWS, tile), lambda i: (0, i))],
        out_specs=[pl.BlockSpec((ROUTE_ROWS, tile), lambda i: (0, i)),
                   pl.BlockSpec((N_EXPERTS, LANES), lambda i: (0, 0))],
        out_shape=[jax.ShapeDtypeStruct((ROUTE_ROWS, n), jnp.int32),
                   jax.ShapeDtypeStruct((N_EXPERTS, LANES), jnp.int32)],
        scratch_shapes=[pltpu.VMEM((N_EXPERTS, LANES), F32)],
        compiler_params=_cparams("arbitrary"),
        name="route_rank",
    )(route)


def _sc_mesh():
    return plsc.VectorSubcoreMesh(core_axis_name="core", subcore_axis_name="subcore")


def _sc_scatter_pairs(x, pos, rows_out):
    n, d = x.shape
    per_w = n // SC_WORKERS
    window = min(SC_SCATTER_ROWS, per_w)

    @pl.kernel(out_type=jax.ShapeDtypeStruct((rows_out, d), x.dtype), mesh=_sc_mesh(),
               scratch_types=[pltpu.VMEM((window,), jnp.int32), pltpu.VMEM((window,), jnp.int32),
                              pltpu.VMEM((window, d), x.dtype), pltpu.SemaphoreType.DMA, pltpu.SemaphoreType.DMA,
                              pltpu.SemaphoreType.DMA])
    def scatter(x_hbm, p_hbm, o_hbm, i1_v, i2_v, rows_v, sem_a, sem_b, sem_c):
        wid = lax.axis_index("subcore") * SC_CORES + lax.axis_index("core")

        @pl.loop(0, per_w // window)
        def _(j):
            base = wid * per_w + j * window
            load_i1 = pltpu.async_copy(p_hbm.at[pl.ds(base, window)], i1_v, sem_a)
            load_i2 = pltpu.async_copy(p_hbm.at[pl.ds(n + base, window)], i2_v, sem_b)
            load_x = pltpu.async_copy(x_hbm.at[pl.ds(base, window)], rows_v, sem_c)
            load_i1.wait()
            load_i2.wait()
            load_x.wait()
            put_1 = pltpu.async_copy(rows_v, o_hbm.at[i1_v], sem_a)
            put_2 = pltpu.async_copy(rows_v, o_hbm.at[i2_v], sem_b)
            put_1.wait()
            put_2.wait()

    return scatter(x, pos)


def _sc_gather_rows(table, idx):
    m = idx.shape[0]
    d = table.shape[1]
    per_w = m // SC_WORKERS
    window = min(SC_GATHER_ROWS, per_w)

    assert per_w % (2 * window) == 0

    @pl.kernel(out_type=jax.ShapeDtypeStruct((m, d), table.dtype), mesh=_sc_mesh(),
               scratch_types=[pltpu.VMEM((window,), jnp.int32), pltpu.VMEM((window,), jnp.int32),
                              pltpu.VMEM((window, d), table.dtype), pltpu.VMEM((window, d), table.dtype),
                              pltpu.SemaphoreType.DMA, pltpu.SemaphoreType.DMA])
    def gather(t_hbm, i_hbm, o_hbm, ia_v, ib_v, ra_v, rb_v, sem_a, sem_b):
        wid = lax.axis_index("subcore") * SC_CORES + lax.axis_index("core")

        @pl.loop(0, per_w // (2 * window))
        def _(j):
            base_a = wid * per_w + j * (2 * window)
            base_b = base_a + window
            idx_a = pltpu.async_copy(i_hbm.at[pl.ds(base_a, window)], ia_v, sem_a)
            idx_b = pltpu.async_copy(i_hbm.at[pl.ds(base_b, window)], ib_v, sem_b)
            idx_a.wait()
            get_a = pltpu.async_copy(t_hbm.at[ia_v], ra_v, sem_a)
            idx_b.wait()
            get_b = pltpu.async_copy(t_hbm.at[ib_v], rb_v, sem_b)
            get_a.wait()
            put_a = pltpu.async_copy(ra_v, o_hbm.at[pl.ds(base_a, window)], sem_a)
            get_b.wait()
            put_b = pltpu.async_copy(rb_v, o_hbm.at[pl.ds(base_b, window)], sem_b)
            put_a.wait()
            put_b.wait()

    return gather(table, idx)


def _expert_tiles_kernel(start_ref, ntile_ref, x_hbm, wg_ref, wu_ref, wd_ref, o_hbm,
                         wg_s, wu_s, wd_s, x_buf, o_buf, in_sem, out_sem):
    e = pl.program_id(0)
    tm = x_buf.shape[1]
    nslot = x_buf.shape[0]
    first = start_ref[e] // tm
    ntile = ntile_ref[e]
    total = start_ref[N_EXPERTS - 1] // tm + ntile_ref[N_EXPERTS - 1]
    wg_s[...] = wg_ref[...].astype(BF16)
    wu_s[...] = wu_ref[...].astype(BF16)
    wd_s[...] = wd_ref[...].astype(BF16)

    def rows_of(g):
        return pl.ds(pl.multiple_of(g * tm, tm), tm)

    def fetch(g):
        slot = g % nslot
        return pltpu.make_async_copy(x_hbm.at[rows_of(g)], x_buf.at[slot], in_sem.at[slot])

    def flush(g):
        slot = g % nslot
        return pltpu.make_async_copy(o_buf.at[slot], o_hbm.at[rows_of(g)], out_sem.at[slot])

    @pl.when(e == 0)
    def _():
        for k in range(nslot - 1):
            @pl.when(k < total)
            def _(k=k):
                fetch(k).start()

    def tile(g, carry):
        @pl.when(g + nslot - 1 < total)
        def _():
            fetch(g + nslot - 1).start()

        fetch(g).wait()

        @pl.when(g >= nslot)
        def _():
            flush(g - nslot).wait()

        slot = g % nslot
        o_buf[slot] = _pack_halves(_expert_mlp(x_buf[slot], wg_s, wu_s, wd_s).astype(BF16))
        flush(g).start()
        return carry

    lax.fori_loop(first, first + ntile, tile, 0)

    @pl.when(e == N_EXPERTS - 1)
    def _():
        for k in range(nslot, 0, -1):
            @pl.when(total >= k)
            def _(k=k):
                flush(total - k).wait()


def _expert_tiles(starts, ntiles, xs, w_g, w_u, w_d):
    rows = xs.shape[0]
    tm = EXPERT_ROW_TILE
    weight = lambda shape: pl.BlockSpec((None,) + shape, lambda e, st, nt: (e, 0, 0))
    grid_spec = pltpu.PrefetchScalarGridSpec(
        num_scalar_prefetch=2,
        grid=(N_EXPERTS,),
        in_specs=[pl.BlockSpec(memory_space=pl.ANY),
                  weight((D_MODEL, D_EXPERT)), weight((D_MODEL, D_EXPERT)), weight((D_EXPERT, D_MODEL))],
        out_specs=pl.BlockSpec(memory_space=pl.ANY),
        scratch_shapes=[pltpu.VMEM((D_MODEL, D_EXPERT), BF16), pltpu.VMEM((D_MODEL, D_EXPERT), BF16),
                        pltpu.VMEM((D_EXPERT, D_MODEL), BF16),
                        pltpu.VMEM((EXPERT_SLOTS, tm, HALF), jnp.int32), pltpu.VMEM((EXPERT_SLOTS, tm, HALF), jnp.int32),
                        pltpu.SemaphoreType.DMA((EXPERT_SLOTS,)), pltpu.SemaphoreType.DMA((EXPERT_SLOTS,))],
    )
    return pl.pallas_call(
        _expert_tiles_kernel,
        grid_spec=grid_spec,
        out_shape=jax.ShapeDtypeStruct((rows, HALF), jnp.int32),
        compiler_params=_cparams("arbitrary"),
        name="expert_tiles",
    )(starts, ntiles, xs, w_g, w_u, w_d)


def _unpack_f32(p):
    return pltpu.bitcast(p & jnp.int32(-65536), F32), pltpu.bitcast(lax.shift_left(p, jnp.int32(16)), F32)


def _combine_kernel(h_ref, o1_ref, o2_ref, rt_ref, gf_ref, y_ref):
    w1, w2 = rt_ref[:, 2:3], rt_ref[:, 3:4]
    a_lo, a_hi = _unpack_f32(o1_ref[...])
    b_lo, b_hi = _unpack_f32(o2_ref[...])
    y_lo = h_ref[:, 0:HALF] + (w1 * a_lo + w2 * b_lo)
    y_hi = h_ref[:, HALF:D_MODEL] + (w1 * a_hi + w2 * b_hi)
    ms = (jnp.sum(y_lo * y_lo, axis=-1, keepdims=True) + jnp.sum(y_hi * y_hi, axis=-1, keepdims=True)) / D_MODEL
    inv = lax.rsqrt(ms + EPS)
    y_ref[:, 0:HALF] = (y_lo * inv) * gf_ref[:, 0:HALF]
    y_ref[:, HALF:D_MODEL] = (y_hi * inv) * gf_ref[:, HALF:D_MODEL]


def _combine(h, o12, route_t, gf, tile):
    n = h.shape[0]
    tile = min(tile, n)
    nt = n // tile
    return pl.pallas_call(
        _combine_kernel,
        grid=(nt,),
        in_specs=[pl.BlockSpec((tile, D_MODEL), lambda i: (i, 0)),
                  pl.BlockSpec((tile, HALF), lambda i: (i, 0)),
                  pl.BlockSpec((tile, HALF), lambda i: (i + nt, 0)),
                  pl.BlockSpec((tile, ROUTE_ROWS), lambda i: (i, 0)),
                  pl.BlockSpec((1, D_MODEL), lambda i: (0, 0))],
        out_specs=pl.BlockSpec((tile, D_MODEL), lambda i: (i, 0)),
        out_shape=jax.ShapeDtypeStruct((n, D_MODEL), F32),
        compiler_params=_cparams("parallel"),
        name="combine_final_norm",
    )(h, o12, o12, route_t, gf)


def _sparse_moe(xn2p, route, h, w_g, w_u, w_d, gf, run_before_experts):
    n = h.shape[0]
    tm = EXPERT_ROW_TILE
    rows = 2 * n + N_EXPERTS * tm
    rank, cnt = _route_rank(route, 1024)
    counts = cnt[:, 0]
    padded = (counts + tm - 1) // tm * tm
    e_idx = jnp.arange(N_EXPERTS, dtype=jnp.int32)
    starts = jnp.sum(jnp.where(e_idx[None, :] < e_idx[:, None], padded[None, :], 0), axis=1)
    ids = route[0:2].astype(jnp.int32)
    start_of = jnp.sum(jnp.where(ids[None] == e_idx[:, None, None], starts[:, None, None], 0), axis=0)
    pos = (start_of + rank[0:2]).reshape(2 * n)
    xs = _sc_scatter_pairs(xn2p, pos, rows)
    xs, _ = lax.optimization_barrier((xs, run_before_experts))
    os_ = _expert_tiles(starts.astype(jnp.int32), (padded // tm).astype(jnp.int32), xs, w_g, w_u, w_d)
    o12 = _sc_gather_rows(os_, pos)
    return _combine(h, o12, route.T, gf, 512)


def _prep_in_weights(w_in):
    o = 0
    w_u = w_in[:, o:o + SSM_WIDTH]; o += SSM_WIDTH
    w_q = w_in[:, o:o + SWA_WIDTH]; o += SWA_WIDTH
    w_k = w_in[:, o:o + SWA_KV_WIDTH]; o += SWA_KV_WIDTH
    w_v = w_in[:, o:o + SWA_KV_WIDTH]; o += SWA_KV_WIDTH
    w_qm = w_in[:, o:o + MEM_WIDTH]; o += MEM_WIDTH
    assert o == GATES_COL0
    wq = (w_q * (SWA_HEAD_DIM ** -0.5)).reshape(D_MODEL, SWA_KV_HEADS, SWA_REP, SWA_HEAD_DIM)
    wq = wq.transpose(0, 2, 1, 3).reshape(D_MODEL, SWA_WIDTH)
    w_main = jnp.concatenate([w_u, wq, w_k, w_v, w_qm], axis=1).astype(BF16)
    return w_main, w_in.astype(BF16)


IN_SPLITS = (SSM_WIDTH, SWA_WIDTH, SWA_KV_WIDTH, SWA_KV_WIDTH, MEM_WIDTH)
IN_DTYPES = ((F32, BF16), (BF16,), (F32,), (F32,), (BF16,))


def kernel(x_prompt, x_sample, cache_swa_k, cache_swa_v, state_ssm_re, state_ssm_im, cache_mem_k, cache_mem_v, mem_prompt, norm1_g, w_in, lam_re, lam_im, log_dt, bm_re, bm_im, cm_re, cm_im, d_skip, w_glu, b_glu, sinks, rel_table, mem_norm_g, w_mem_kv, w_br_ssm, w_br_swa, w_br_mem, w_out, norm2_g, w_rg, b_rg, w_rexp, b_rexp, w_e_gate, w_e_up, w_e_down, final_norm_g):
    nb, t, _ = x_prompt.shape
    ns, ts, _ = x_sample.shape
    l = 0
    L = S5_CHUNK

    w_main, w_gates = _prep_in_weights(w_in[l])
    w_swa = (w_br_swa[l].reshape(SWA_KV_HEADS, SWA_REP, SWA_HEAD_DIM, D_MODEL).transpose(1, 0, 2, 3)
             .reshape(SWA_WIDTH, D_MODEL))
    pad_rows = ROUTER_ROWS - N_EXPERTS - N_EXPERT_GROUPS
    w_router = jnp.concatenate([w_rexp[l].T, w_rg[l].T, jnp.zeros((pad_rows, D_MODEL), F32)], axis=0).astype(BF16)
    b_router = jnp.concatenate([b_rexp[l], b_rg[l], jnp.zeros((pad_rows,), F32)]).reshape(ROUTER_ROWS, 1)
    mp = {
        'g1': norm1_g[l].reshape(1, D_MODEL), 'w_gates': w_gates, 'd_skip': d_skip[l].reshape(1, SSM_WIDTH),
        'w_glu': w_glu[l].astype(BF16), 'b_glu': b_glu[l].reshape(1, SSM_WIDTH),
        'w_br_ssm': w_br_ssm[l].astype(BF16), 'w_br_swa': w_swa.astype(BF16),
        'w_br_mem': w_br_mem[l].astype(BF16), 'w_out': w_out[l].astype(BF16),
        'g2': norm2_g[l].reshape(1, D_MODEL), 'w_router': w_router, 'b_router': b_router,
    }
    w_g, w_u, w_d = w_e_gate[l], w_e_up[l], w_e_down[l]
    gf = final_norm_g.reshape(1, D_MODEL)
    s5_w = _s5_weights(lam_re[l], lam_im[l], log_dt[l], bm_re[l], bm_im[l], cm_re[l], cm_im[l], L)

    bias_p = _rel_bias(rel_table, np.arange(WINDOW)[:, None] + WINDOW - np.arange(2 * WINDOW)[None, :])
    keys_s = WINDOW + 2 * ts
    bias_s = _rel_bias(rel_table, np.arange(ts)[:, None] + WINDOW - np.arange(keys_s)[None, :])
    bias_s = bias_s.reshape(SWA_HEADS * ts, keys_s)
    sink_rows = jnp.repeat(sinks[l].astype(F32), ts).reshape(SWA_HEADS * ts, 1)

    n = nb * t
    xp = x_prompt.reshape(n, D_MODEL)
    mk, mv = _norm_proj(mem_prompt.reshape(nb * MEM_TOKENS, D_MODEL), mem_norm_g[l].reshape(1, D_MODEL),
                        w_mem_kv[l].astype(BF16), (MEM_WIDTH, MEM_WIDTH), ((F32,), (F32,)), 512)
    u, ub, qz, k, v, qm = _norm_proj(xp, mp['g1'], w_main, IN_SPLITS, IN_DTYPES, 1024)

    y_ssm, fin = _s5(ub, jnp.zeros((nb, N_CH_TILES * 2 * STATE_TILE), F32), s5_w, nb, t // L, L, 64)
    p_re, p_im = _tiles_to_state(fin)

    o_swa = _swa_prompt(qz, k, v, bias_p, sinks[l].astype(F32), nb, t, 4)
    o_mem = _mem_prompt(qm, mk, mv, nb, t, 512)
    h, xn2p, route = _merge(xp, u, y_ssm, o_swa, o_mem, mp, 512)

    k4 = k.reshape(nb, t, SWA_KV_HEADS, SWA_HEAD_DIM)
    v4 = v.reshape(nb, t, SWA_KV_HEADS, SWA_HEAD_DIM)
    new_k_p, new_v_p = k4[:, -WINDOW:][None], v4[:, -WINDOW:][None]
    new_mk = mk.reshape(1, nb, MEM_TOKENS, MEM_HEADS, MEM_HEAD_DIM)
    new_mv = mv.reshape(1, nb, MEM_TOKENS, MEM_HEADS, MEM_HEAD_DIM)

    m = ns * ts
    xs = x_sample.reshape(m, D_MODEL)
    us, ubs, qzs, k_s, v_s, qms = _norm_proj(xs, mp['g1'], w_main, IN_SPLITS, IN_DTYPES, 256)
    ys_ssm, fins = _s5(ubs, _state_to_tiles(state_ssm_re[l], state_ssm_im[l]), s5_w, ns, ts // L, L, 64)
    s_re, s_im = _tiles_to_state(fins)

    kk_all = jnp.concatenate([cache_swa_k[l].reshape(ns, WINDOW, SWA_KV_WIDTH).astype(F32),
                              k_s.reshape(ns, ts, SWA_KV_WIDTH)], axis=1)
    vv_all = jnp.concatenate([cache_swa_v[l].reshape(ns, WINDOW, SWA_KV_WIDTH).astype(F32),
                              v_s.reshape(ns, ts, SWA_KV_WIDTH)], axis=1)
    pad = jnp.zeros((ns, keys_s - WINDOW - ts, SWA_KV_WIDTH), F32)
    q5 = qzs.reshape(ns, ts, SWA_REP, SWA_KV_HEADS, SWA_HEAD_DIM)
    zq = jnp.zeros((ns, ts, SWA_REP, SWA_HEAD_DIM), BF16)
    q_rows = jnp.concatenate([jnp.concatenate([q5[:, :, :, 0], zq], axis=-1),
                              jnp.concatenate([zq, q5[:, :, :, 1]], axis=-1)], axis=2)
    q_rows = q_rows.transpose(0, 2, 1, 3).reshape(ns, SWA_HEADS * ts, LANES)
    o_dec, roll_k, roll_v = _swa_decode(q_rows, jnp.concatenate([kk_all, pad], axis=1),
                                        jnp.concatenate([vv_all, pad], axis=1),
                                        bias_s, sink_rows, ts, 8)
    o_dec = o_dec.reshape(ns, SWA_KV_HEADS, SWA_REP, ts, SWA_KV_HEADS, SWA_HEAD_DIM)
    o_dec = jnp.stack([o_dec[:, g, :, :, g] for g in range(SWA_KV_HEADS)], axis=1)
    o_swa_s = o_dec.transpose(0, 3, 2, 1, 4).reshape(m, SWA_WIDTH).astype(BF16)

    o_mem_s = _mem_decode(qms.astype(F32).reshape(ns, ts, MEM_WIDTH), cache_mem_k, cache_mem_v, l, 8)
    o_mem_s = o_mem_s.reshape(m, MEM_WIDTH).astype(BF16)

    y_prompt = _sparse_moe(xn2p, route, h, w_g, w_u, w_d, gf, (ys_ssm, o_swa_s, o_mem_s)).reshape(nb, t, D_MODEL)
    hs_, xn2ps, routes = _merge(xs, us, ys_ssm, o_swa_s, o_mem_s, mp, 256)
    y_sample = _moe(xn2ps, routes.T, w_g, w_u, w_d, hs_, gf, 1024).reshape(ns, ts, D_MODEL)

    new_k_s = roll_k.reshape(1, ns, WINDOW, SWA_KV_HEADS, SWA_HEAD_DIM).astype(cache_swa_k.dtype)
    new_v_s = roll_v.reshape(1, ns, WINDOW, SWA_KV_HEADS, SWA_HEAD_DIM).astype(cache_swa_v.dtype)

    return (y_prompt, y_sample,
            new_k_p, new_v_p, p_re[None], p_im[None], new_mk, new_mv,
            new_k_s, new_v_s, s_re[None].astype(state_ssm_re.dtype), s_im[None].astype(state_ssm_im.dtype))
```

```python
import functools
import math

import numpy as np
import jax
import jax.numpy as jnp
from jax import lax
from jax.experimental import pallas as pl
from jax.experimental.pallas import tpu as pltpu
from jax.experimental.pallas import tpu_sc as plsc

F32 = jnp.float32
BF16 = jnp.bfloat16

D_MODEL = 1024
SSM_WIDTH = 512
SSM_GROUP = 16
SSM_GROUPS = 32
SSM_STATE = 64
SWA_HEADS = 8
SWA_KV_HEADS = 2
SWA_REP = 4
SWA_HEAD_DIM = 64
SWA_WIDTH = 512
SWA_KV_WIDTH = 128
WINDOW = 128
REL_BUCKETS = 32
REL_MAX_DIST = 128
MEM_TOKENS = 256
MEM_HEADS = 4
MEM_HEAD_DIM = 128
MEM_WIDTH = 512
N_EXPERT_GROUPS = 4
EXPERTS_PER_GROUP = 8
N_EXPERTS = 32
D_EXPERT = 256
EPS = 1e-6
NEG_INF = -1e30

LANES = 128
GROUPS_PER_TILE = LANES // SSM_GROUP
N_CH_TILES = SSM_WIDTH // LANES
STATE_TILE = GROUPS_PER_TILE * SSM_STATE
VMEM_LIMIT = 56 * 1024 * 1024
S5_CHUNK = 8
S5_PANEL = 256

_TRANS_B = (((1,), (1,)), ((), ()))


def _cparams(*sem):
    return pltpu.CompilerParams(dimension_semantics=sem, vmem_limit_bytes=VMEM_LIMIT)


def _rms(x, g):
    return (x * lax.rsqrt(jnp.mean(x * x, axis=-1, keepdims=True) + EPS)) * g


def _dot(a, b):
    return jnp.dot(a, b, preferred_element_type=F32)


def _norm_proj_kernel(x_ref, g_ref, w_ref, *out_refs, splits, dtypes):
    xb = _rms(x_ref[...], g_ref[...]).astype(BF16)
    off = 0
    outs = iter(out_refs)
    for width, dts in zip(splits, dtypes):
        r = _dot(xb, w_ref[:, off:off + width])
        for dt in dts:
            next(outs)[...] = r.astype(dt)
        off += width


def _norm_proj(x, g, w, splits, dtypes, tile):
    n, d = x.shape
    tile = min(tile, n)
    flat = [(wd, dt) for wd, dts in zip(splits, dtypes) for dt in dts]
    return pl.pallas_call(
        functools.partial(_norm_proj_kernel, splits=tuple(splits), dtypes=tuple(dtypes)),
        grid=(n // tile,),
        in_specs=[pl.BlockSpec((tile, d), lambda i: (i, 0)),
                  pl.BlockSpec((1, d), lambda i: (0, 0)),
                  pl.BlockSpec((d, sum(splits)), lambda i: (0, 0), pipeline_mode=pl.Buffered(1))],
        out_specs=[pl.BlockSpec((tile, wd), lambda i: (i, 0)) for wd, _ in flat],
        out_shape=[jax.ShapeDtypeStruct((n, wd), dt) for wd, dt in flat],
        compiler_params=_cparams("parallel"),
        name="norm_proj",
    )(x, g, w)


def _s5_weights(lam_re, lam_im, log_dt, bm_re, bm_im, cm_re, cm_im, L):
    nt, gt, P, H = N_CH_TILES, GROUPS_PER_TILE, SSM_STATE, SSM_GROUP
    lr, li = lam_re.astype(F32), lam_im.astype(F32)
    dt = jnp.exp(log_dt.astype(F32))[:, None]
    mag = jnp.exp(lr * dt)
    a_re = mag * jnp.cos(li * dt)
    a_im = mag * jnp.sin(li * dt)
    den = lr * lr + li * li
    f_re = ((a_re - 1.0) * lr + a_im * li) / den
    f_im = (a_im * lr - (a_re - 1.0) * li) / den
    br, bi = bm_re.astype(F32), bm_im.astype(F32)
    bb_re = f_re[..., None] * br - f_im[..., None] * bi
    bb_im = f_re[..., None] * bi + f_im[..., None] * br
    pr, pi = [jnp.ones_like(a_re)], [jnp.zeros_like(a_im)]
    for _ in range(L):
        pr.append(pr[-1] * a_re - pi[-1] * a_im)
        pi.append(pr[-2] * a_im + pi[-1] * a_re)
    ap_re, ap_im = jnp.stack(pr), jnp.stack(pi)
    cr, ci = cm_re.astype(F32), cm_im.astype(F32)
    ca_re = cr[None] * ap_re[:, :, None, :] - ci[None] * ap_im[:, :, None, :]
    ca_im = cr[None] * ap_im[:, :, None, :] + ci[None] * ap_re[:, :, None, :]

    rev_re = jnp.stack([pr[L - 1 - s] for s in range(L)])
    rev_im = jnp.stack([pi[L - 1 - s] for s in range(L)])
    ws_re = rev_re[..., None] * bb_re[None] - rev_im[..., None] * bb_im[None]
    ws_im = rev_re[..., None] * bb_im[None] + rev_im[..., None] * bb_re[None]
    c_st = jnp.concatenate([ws_re.transpose(0, 1, 3, 2).reshape(L, nt, gt * H, P),
                            ws_im.transpose(0, 1, 3, 2).reshape(L, nt, gt * H, P)], axis=3).transpose(1, 0, 2, 3)
    so = lambda ca: ca[1:].transpose(1, 3, 0, 2).reshape(nt, gt * P, L * H)
    c_so = jnp.concatenate([so(ca_re), so(-ca_im)], axis=1)
    prod = (ca_re[:L][:, :, None, :, :] * bb_re.transpose(0, 2, 1)[None, :, :, None, :]
            - ca_im[:L][:, :, None, :, :] * bb_im.transpose(0, 2, 1)[None, :, :, None, :])
    k_lag = jnp.sum(prod, axis=-1).transpose(1, 2, 0, 3)
    c_k = k_lag.reshape(nt, gt * H, L * H)
    w_st, w_out, toep = _s5_expand(c_st, c_so, c_k, L)

    def per_tile(v):
        return v.reshape(nt, 1, STATE_TILE)

    return w_st, w_out, toep, per_tile(pr[L]), per_tile(pi[L])


def _s5_expand_kernel(cst_ref, cso_ref, ck_ref, wst_ref, wso_ref, toep_ref, *, L):
    hp = lax.Precision.HIGHEST
    P, H = SSM_STATE, SSM_GROUP
    iota = lambda shape, d: lax.broadcasted_iota(jnp.int32, shape, d)
    one = lambda cond: jnp.where(cond, 1.0, 0.0).astype(F32)

    r, c = iota((2 * P, 2 * STATE_TILE), 0), iota((2 * P, 2 * STATE_TILE), 1)
    rep_st = one((r // P == c // STATE_TILE) & (r % P == c % P))
    r, c = iota((LANES, 2 * STATE_TILE), 0), iota((LANES, 2 * STATE_TILE), 1)
    own_st = one(r // H == (c % STATE_TILE) // P)
    for s in range(L):
        blk = jnp.dot(cst_ref[s], rep_st, precision=hp, preferred_element_type=F32) * own_st
        wst_ref[s * LANES:(s + 1) * LANES, :] = blk.astype(BF16)

    r, c = iota((LANES, LANES), 0), iota((LANES, LANES), 1)
    pick = [one((r // H == t) & (r % H == c % H)) for t in range(L)]
    own_k = one(r // H == c // H)
    r, c = iota((2 * STATE_TILE, LANES), 0), iota((2 * STATE_TILE, LANES), 1)
    own_so = one((r % STATE_TILE) // P == c // H)
    cso = cso_ref[...]
    for t in range(L):
        blk = jnp.dot(cso, pick[t], precision=hp, preferred_element_type=F32) * own_so
        wso_ref[:, t * LANES:(t + 1) * LANES] = blk.astype(BF16)
    ck = ck_ref[...]
    lag = [(jnp.dot(ck, pick[t], precision=hp, preferred_element_type=F32) * own_k).astype(BF16) for t in range(L)]
    zero = jnp.zeros((LANES, LANES), BF16)
    for s in range(L):
        for t in range(L):
            toep_ref[s * LANES:(s + 1) * LANES, t * LANES:(t + 1) * LANES] = lag[t - s] if t >= s else zero


def _s5_expand(c_st, c_so, c_k, L):
    lk = L * LANES
    st2 = 2 * STATE_TILE
    return pl.pallas_call(
        functools.partial(_s5_expand_kernel, L=L),
        grid=(N_CH_TILES,),
        in_specs=[pl.BlockSpec((None, L, LANES, 2 * SSM_STATE), lambda j: (j, 0, 0, 0)),
                  pl.BlockSpec((None, st2, L * SSM_GROUP), lambda j: (j, 0, 0)),
                  pl.BlockSpec((None, LANES, L * SSM_GROUP), lambda j: (j, 0, 0))],
        out_specs=[pl.BlockSpec((None, lk, st2), lambda j: (j, 0, 0)),
                   pl.BlockSpec((None, st2, lk), lambda j: (j, 0, 0)),
                   pl.BlockSpec((None, lk, lk), lambda j: (j, 0, 0))],
        out_shape=[jax.ShapeDtypeStruct((N_CH_TILES, lk, st2), BF16),
                   jax.ShapeDtypeStruct((N_CH_TILES, st2, lk), BF16),
                   jax.ShapeDtypeStruct((N_CH_TILES, lk, lk), BF16)],
        compiler_params=_cparams("parallel"),
        name="s5_expand_weights",
    )(c_st, c_so, c_k)


def _to_chunks(u, nb, nc, L):
    return (u.reshape(nb, nc, L, N_CH_TILES, LANES).transpose(1, 0, 3, 2, 4)
            .reshape(nc * nb, N_CH_TILES * L * LANES))


def _from_chunks(y, nb, nc, L):
    return (y.reshape(nc, nb, N_CH_TILES, L, LANES).transpose(1, 0, 3, 2, 4)
            .reshape(nb * nc * L, SSM_WIDTH))


def _s5_kernel(x_ref, h0_ref, are_ref, aim_ref, ws_ref, t_ref, wo_ref, y_ref, fin_ref,
               hr_ref, hi_ref, d_ref, hs_ref, *, cb, nb):
    ci = pl.program_id(1)

    @pl.when(ci == 0)
    def _():
        hr_ref[...] = h0_ref[:, 0:STATE_TILE]
        hi_ref[...] = h0_ref[:, STATE_TILE:2 * STATE_TILE]

    x = x_ref[...]
    d_ref[...] = _dot(x, ws_ref[...])
    ar = jnp.broadcast_to(are_ref[...], (nb, STATE_TILE))
    ai = jnp.broadcast_to(aim_ref[...], (nb, STATE_TILE))

    def body(c, carry):
        hr, hi = carry
        r0 = pl.multiple_of(c * nb, nb)
        hs_ref[pl.ds(r0, nb), 0:STATE_TILE] = hr
        hs_ref[pl.ds(r0, nb), STATE_TILE:2 * STATE_TILE] = hi
        d = d_ref[pl.ds(r0, nb), :]
        return (ar * hr - ai * hi + d[:, 0:STATE_TILE],
                ar * hi + ai * hr + d[:, STATE_TILE:2 * STATE_TILE])

    hr, hi = lax.fori_loop(0, cb, body, (hr_ref[...], hi_ref[...]))
    hr_ref[...] = hr
    hi_ref[...] = hi
    hsb = hs_ref[...].astype(BF16)
    for c0 in range(0, t_ref.shape[1], S5_PANEL):
        c1 = c0 + S5_PANEL
        y_ref[:, c0:c1] = _dot(x[:, 0:c1], t_ref[0:c1, c0:c1]) + _dot(hsb, wo_ref[:, c0:c1])

    @pl.when(ci == pl.num_programs(1) - 1)
    def _():
        fin_ref[:, 0:STATE_TILE] = hr
        fin_ref[:, STATE_TILE:2 * STATE_TILE] = hi


def _s5(ub, h0, weights, nb, nc, L, chunk_block):
    w_st, w_so, toep, a_re, a_im = weights
    xc = _to_chunks(ub, nb, nc, L)
    cb = min(chunk_block, nc)
    rows = cb * nb
    lk = L * LANES
    st2 = 2 * STATE_TILE
    tile_w = lambda shape: pl.BlockSpec((None,) + shape, lambda j, c: (j, 0, 0))
    y, fin = pl.pallas_call(
        functools.partial(_s5_kernel, cb=cb, nb=nb),
        grid=(N_CH_TILES, nc // cb),
        in_specs=[pl.BlockSpec((rows, lk), lambda j, c: (c, j)),
                  pl.BlockSpec((nb, st2), lambda j, c: (0, j)),
                  tile_w((1, STATE_TILE)), tile_w((1, STATE_TILE)),
                  tile_w((lk, st2)), tile_w((lk, lk)), tile_w((st2, lk))],
        out_specs=[pl.BlockSpec((rows, lk), lambda j, c: (c, j)),
                   pl.BlockSpec((nb, st2), lambda j, c: (0, j))],
        out_shape=[jax.ShapeDtypeStruct((nc * nb, N_CH_TILES * lk), F32),
                   jax.ShapeDtypeStruct((nb, N_CH_TILES * st2), F32)],
        scratch_shapes=[pltpu.VMEM((nb, STATE_TILE), F32), pltpu.VMEM((nb, STATE_TILE), F32),
                        pltpu.VMEM((rows, st2), F32), pltpu.VMEM((rows, st2), F32)],
        compiler_params=_cparams("parallel", "arbitrary"),
        name="s5_chunked_scan",
    )(xc, h0, a_re, a_im, w_st, toep, w_so)
    return _from_chunks(y, nb, nc, L), fin


def _state_to_tiles(h_re, h_im):
    nb = h_re.shape[0]
    r = h_re.astype(F32).reshape(nb, N_CH_TILES, STATE_TILE)
    i = h_im.astype(F32).reshape(nb, N_CH_TILES, STATE_TILE)
    return jnp.concatenate([r, i], axis=-1).reshape(nb, N_CH_TILES * 2 * STATE_TILE)


def _tiles_to_state(h):
    nb = h.shape[0]
    h = h.reshape(nb, N_CH_TILES, 2, GROUPS_PER_TILE, SSM_STATE)
    return (h[:, :, 0].reshape(nb, SSM_GROUPS, SSM_STATE), h[:, :, 1].reshape(nb, SSM_GROUPS, SSM_STATE))


def _t5_bucket(dist):
    n = np.maximum(dist, 0)
    max_exact = REL_BUCKETS // 2
    nf = np.maximum(n, 1).astype(np.float32)
    large = max_exact + (np.log(nf / np.float32(max_exact)) / np.float32(math.log(REL_MAX_DIST / max_exact))
                         * np.float32(REL_BUCKETS - max_exact)).astype(np.int32)
    large = np.minimum(large, REL_BUCKETS - 1)
    return np.where(n < max_exact, n, large)


def _rel_bias(rel_table, dist):
    bucket = _t5_bucket(dist)
    tab = rel_table.astype(F32)
    out = jnp.zeros((SWA_HEADS,) + dist.shape, F32)
    for b in range(REL_BUCKETS):
        sel = jnp.asarray(bucket == b)
        if bool((bucket == b).any()):
            out = jnp.where(sel[None], tab[b].reshape((SWA_HEADS,) + (1,) * dist.ndim), out)
    return out


def _swa_prompt_kernel(sink_ref, q_ref, kp_ref, kc_ref, vp_ref, vc_ref, bias_ref, o_ref, kk_ref, vv_ref, *, qblocks):
    step = pl.program_id(1)
    kk_ref[0:WINDOW, :] = kp_ref[...].astype(BF16)
    kk_ref[WINDOW:, :] = kc_ref[...].astype(BF16)
    vv_ref[0:WINDOW, :] = vp_ref[...].astype(BF16)
    vv_ref[WINDOW:, :] = vc_ref[...].astype(BF16)
    row = lax.broadcasted_iota(jnp.int32, (WINDOW, 2 * WINDOW), 0)
    col = lax.broadcasted_iota(jnp.int32, (WINDOW, 2 * WINDOW), 1)
    dist = row + WINDOW - col
    band = (dist >= 0) & (dist < WINDOW)
    lane = lax.broadcasted_iota(jnp.int32, (WINDOW, LANES), 1)
    low = lane < SWA_HEAD_DIM

    def block(j, carry):
        r0 = pl.multiple_of(j * WINDOW, WINDOW)
        kk = kk_ref[pl.ds(r0, 2 * WINDOW), :]
        vv = vv_ref[pl.ds(r0, 2 * WINDOW), :]
        valid = band & ((col >= WINDOW) | (step * qblocks + j > 0))
        for t in range(SWA_REP):
            q2 = q_ref[pl.ds(r0, WINDOW), t * LANES:(t + 1) * LANES]
            outs = []
            for half in range(SWA_KV_HEADS):
                h = t + SWA_REP * half
                qh = jnp.where(low if half == 0 else jnp.logical_not(low), q2, jnp.zeros_like(q2))
                s = lax.dot_general(qh, kk, _TRANS_B, preferred_element_type=F32)
                s = jnp.where(valid, s + bias_ref[h], NEG_INF)
                sink = sink_ref[h]
                m = jnp.maximum(jnp.max(s, axis=-1, keepdims=True), sink)
                e = jnp.exp(s - m)
                den = jnp.sum(e, axis=-1, keepdims=True) + jnp.exp(sink - m)
                outs.append(_dot(e.astype(BF16), vv) * (1.0 / den))
            o_ref[pl.ds(r0, WINDOW), t * LANES:(t + 1) * LANES] = jnp.where(low, outs[0], outs[1]).astype(BF16)
        return carry

    lax.fori_loop(0, qblocks, block, 0)


def _swa_prompt(q, k, v, bias, sinks, nb, t, qblocks):
    nstep = t // (WINDOW * qblocks)
    rows = WINDOW * qblocks
    cur = lambda b, i: (b * nstep + i, 0)
    prev = lambda b, i: (b * nstep * qblocks + jnp.maximum(i * qblocks - 1, 0), 0)
    return pl.pallas_call(
        functools.partial(_swa_prompt_kernel, qblocks=qblocks),
        grid=(nb, nstep),
        in_specs=[pl.BlockSpec(memory_space=pltpu.SMEM),
                  pl.BlockSpec((rows, SWA_WIDTH), cur),
                  pl.BlockSpec((WINDOW, SWA_KV_WIDTH), prev),
                  pl.BlockSpec((rows, SWA_KV_WIDTH), cur),
                  pl.BlockSpec((WINDOW, SWA_KV_WIDTH), prev),
                  pl.BlockSpec((rows, SWA_KV_WIDTH), cur),
                  pl.BlockSpec((SWA_HEADS, WINDOW, 2 * WINDOW), lambda b, i: (0, 0, 0))],
        out_specs=pl.BlockSpec((rows, SWA_WIDTH), cur),
        out_shape=jax.ShapeDtypeStruct((nb * t, SWA_WIDTH), BF16),
        scratch_shapes=[pltpu.VMEM((rows + WINDOW, SWA_KV_WIDTH), BF16),
                        pltpu.VMEM((rows + WINDOW, SWA_KV_WIDTH), BF16)],
        compiler_params=_cparams("parallel", "parallel"),
        name="swa_prompt",
    )(sinks, q, k, k, v, v, bias)


def _swa_decode_kernel(q_ref, k_ref, v_ref, bias_ref, sink_ref, o_ref, nk_ref, nv_ref, *, seqs, tq):
    rows, keys = SWA_HEADS * tq, k_ref.shape[1]
    nk_ref[...] = k_ref[:, tq:tq + WINDOW, :]
    nv_ref[...] = v_ref[:, tq:tq + WINDOW, :]
    low = lax.broadcasted_iota(jnp.int32, (tq, LANES), 1) < SWA_HEAD_DIM
    qi = lax.broadcasted_iota(jnp.int32, (rows, keys), 0) % tq
    col = lax.broadcasted_iota(jnp.int32, (rows, keys), 1)
    dist = qi + WINDOW - col
    valid = (dist >= 0) & (dist < WINDOW)
    bias = bias_ref[...]
    sink = sink_ref[...]
    for s_i in range(seqs):
        q = q_ref[s_i]
        tiles = [q[:, t * LANES:(t + 1) * LANES] for t in range(SWA_REP)]
        qh = jnp.concatenate([jnp.where(low, x, 0.0) for x in tiles]
                             + [jnp.where(low, 0.0, x) for x in tiles], axis=0)
        kk = k_ref[s_i].astype(BF16)
        s = lax.dot_general(qh.astype(BF16), kk, _TRANS_B, preferred_element_type=F32)
        s = jnp.where(valid, s + bias, NEG_INF)
        m = jnp.maximum(jnp.max(s, axis=-1, keepdims=True), sink)
        e = jnp.exp(s - m)
        den = jnp.sum(e, axis=-1, keepdims=True) + jnp.exp(sink - m)
        o = _dot(e.astype(BF16), v_ref[s_i].astype(BF16)) * (1.0 / den)
        for t in range(SWA_REP):
            o_ref[s_i, :, t * LANES:(t + 1) * LANES] = jnp.where(
                low, o[t * tq:(t + 1) * tq], o[(t + SWA_REP) * tq:(t + SWA_REP + 1) * tq])


def _swa_decode(q, k_all, v_all, bias, sink_rows, seqs):
    nseq, tq, _ = q.shape
    rows = SWA_HEADS * tq
    keys = k_all.shape[1]
    seqs = min(seqs, nseq)
    return pl.pallas_call(
        functools.partial(_swa_decode_kernel, seqs=seqs, tq=tq),
        grid=(nseq // seqs,),
        in_specs=[pl.BlockSpec((seqs, tq, SWA_WIDTH), lambda i: (i, 0, 0)),
                  pl.BlockSpec((seqs, keys, LANES), lambda i: (i, 0, 0)),
                  pl.BlockSpec((seqs, keys, LANES), lambda i: (i, 0, 0)),
                  pl.BlockSpec((rows, keys), lambda i: (0, 0)),
                  pl.BlockSpec((rows, 1), lambda i: (0, 0))],
        out_specs=[pl.BlockSpec((seqs, tq, SWA_WIDTH), lambda i: (i, 0, 0)),
                   pl.BlockSpec((seqs, WINDOW, LANES), lambda i: (i, 0, 0)),
                   pl.BlockSpec((seqs, WINDOW, LANES), lambda i: (i, 0, 0))],
        out_shape=[jax.ShapeDtypeStruct((nseq, tq, SWA_WIDTH), F32),
                   jax.ShapeDtypeStruct((nseq, WINDOW, LANES), F32),
                   jax.ShapeDtypeStruct((nseq, WINDOW, LANES), F32)],
        compiler_params=_cparams("parallel"),
        name="swa_decode",
    )(q, k_all, v_all, bias, sink_rows)


def _softmax(s):
    m = jnp.max(s, axis=-1, keepdims=True)
    e = jnp.exp(s - m)
    return e * (1.0 / jnp.sum(e, axis=-1, keepdims=True))


def _mem_prompt_kernel(q_ref, k_ref, v_ref, o_ref, s_ref, p_ref):
    scale = MEM_HEAD_DIM ** -0.5
    heads = [slice(h * MEM_HEAD_DIM, (h + 1) * MEM_HEAD_DIM) for h in range(MEM_HEADS)]
    for h, sl in enumerate(heads):
        s_ref[h] = lax.dot_general(q_ref[:, sl], k_ref[:, sl].astype(BF16), _TRANS_B, preferred_element_type=F32)
    s = s_ref[...] * scale
    e = jnp.exp(s - jnp.max(s, axis=-1, keepdims=True))
    p_ref[...] = e.astype(BF16)
    inv = 1.0 / jnp.sum(e, axis=-1, keepdims=True)
    for h, sl in enumerate(heads):
        o_ref[:, sl] = (_dot(p_ref[h], v_ref[:, sl].astype(BF16)) * inv[h]).astype(BF16)


def _mem_prompt(qm, mk, mv, nb, t, tile):
    tile = min(tile, t)
    nt = t // tile
    return pl.pallas_call(
        _mem_prompt_kernel,
        grid=(nb, nt),
        in_specs=[pl.BlockSpec((tile, MEM_WIDTH), lambda b, i: (b * nt + i, 0)),
                  pl.BlockSpec((MEM_TOKENS, MEM_WIDTH), lambda b, i: (b, 0)),
                  pl.BlockSpec((MEM_TOKENS, MEM_WIDTH), lambda b, i: (b, 0))],
        out_specs=pl.BlockSpec((tile, MEM_WIDTH), lambda b, i: (b * nt + i, 0)),
        out_shape=jax.ShapeDtypeStruct((nb * t, MEM_WIDTH), BF16),
        scratch_shapes=[pltpu.VMEM((MEM_HEADS, tile, MEM_TOKENS), F32), pltpu.VMEM((MEM_HEADS, tile, MEM_TOKENS), BF16)],
        compiler_params=_cparams("parallel", "parallel"),
        name="mem_prompt",
    )(qm, mk, mv)


def _mem_decode_kernel(q_ref, k_ref, v_ref, o_ref, *, seqs):
    tq = q_ref.shape[1]
    rows, cols = MEM_HEADS * tq, MEM_TOKENS * MEM_HEADS
    k2 = k_ref.reshape(seqs, cols, MEM_HEAD_DIM)
    v2 = v_ref.reshape(seqs, cols, MEM_HEAD_DIM)
    scale = MEM_HEAD_DIM ** -0.5
    own = (lax.broadcasted_iota(jnp.int32, (rows, cols), 1) % MEM_HEADS
           == lax.broadcasted_iota(jnp.int32, (rows, cols), 0) // tq)
    for s_i in range(seqs):
        q = q_ref[s_i]
        qb = jnp.concatenate([q[:, h * MEM_HEAD_DIM:(h + 1) * MEM_HEAD_DIM] for h in range(MEM_HEADS)], axis=0)
        s = lax.dot_general(qb.astype(BF16), k2[s_i].astype(BF16), _TRANS_B, preferred_element_type=F32) * scale
        p = _softmax(jnp.where(own, s, NEG_INF)).astype(BF16)
        o = _dot(p, v2[s_i].astype(BF16))
        for h in range(MEM_HEADS):
            o_ref[s_i, :, h * MEM_HEAD_DIM:(h + 1) * MEM_HEAD_DIM] = o[h * tq:(h + 1) * tq, :]


def _mem_decode(q, k, v, layer, seqs):
    nseq, tq, _ = q.shape
    seqs = min(seqs, nseq)
    cache = pl.BlockSpec((None, seqs, MEM_TOKENS, MEM_HEADS, MEM_HEAD_DIM), lambda i: (layer, i, 0, 0, 0))
    return pl.pallas_call(
        functools.partial(_mem_decode_kernel, seqs=seqs),
        grid=(nseq // seqs,),
        in_specs=[pl.BlockSpec((seqs, tq, MEM_WIDTH), lambda i: (i, 0, 0)), cache, cache],
        out_specs=pl.BlockSpec((seqs, tq, MEM_WIDTH), lambda i: (i, 0, 0)),
        out_shape=jax.ShapeDtypeStruct((nseq, tq, MEM_WIDTH), F32),
        compiler_params=_cparams("parallel"),
        name="mem_decode",
    )(q, k, v)


ROUTER_ROWS = 40
GATES_COL0 = SSM_WIDTH + SWA_WIDTH + 2 * SWA_KV_WIDTH + MEM_WIDTH
ROUTE_ROWS = 8
HALF = D_MODEL // 2


def _pack_halves(xb):
    hi = pltpu.bitcast(xb[:, 0:HALF].astype(F32), jnp.int32)
    lo = pltpu.bitcast(xb[:, HALF:D_MODEL].astype(F32), jnp.int32)
    return hi | lax.shift_right_logical(lo, jnp.int32(16))


def _unpack_halves(p):
    hi = pltpu.bitcast(p & jnp.int32(-65536), F32).astype(BF16)
    lo = pltpu.bitcast(lax.shift_left(p, jnp.int32(16)), F32).astype(BF16)
    return hi, lo


def _merge_kernel(x_ref, u_ref, y_ref, os_ref, om_ref, g1_ref, wg_ref, dsk_ref, wglu_ref, bglu_ref,
                  wbs_ref, wbw_ref, wbm_ref, wout_ref, g2_ref, wr_ref, br_ref,
                  h_ref, xn2_ref, route_ref):
    x = x_ref[...]
    tt = x.shape[0]
    xb = _rms(x, g1_ref[...]).astype(BF16)
    z = jax.nn.gelu(y_ref[...] + dsk_ref[...] * u_ref[...])
    z = z * jax.nn.sigmoid(_dot(z.astype(BF16), wglu_ref[...]) + bglu_ref[...])
    gate = lambda b: jax.nn.sigmoid(_dot(xb, wg_ref[:, GATES_COL0 + b * D_MODEL:GATES_COL0 + (b + 1) * D_MODEL]))
    merged = gate(0) * _dot(z.astype(BF16), wbs_ref[...])
    merged = merged + gate(1) * _dot(os_ref[...], wbw_ref[...])
    merged = merged + gate(2) * _dot(om_ref[...], wbm_ref[...])
    h = x + _dot(merged.astype(BF16), wout_ref[...])
    h_ref[...] = h
    xn2 = _rms(h, g2_ref[...]).astype(BF16)
    xn2_ref[...] = _pack_halves(xn2)

    lt = lax.dot_general(wr_ref[...], xn2, _TRANS_B, preferred_element_type=F32) + br_ref[...]
    gl = lt[N_EXPERTS:N_EXPERTS + N_EXPERT_GROUPS]
    ge = jnp.exp(gl - jnp.max(gl, axis=0, keepdims=True))
    gp = ge / jnp.sum(ge, axis=0, keepdims=True)
    gw = jnp.max(gp, axis=0, keepdims=True)
    gidx = jnp.full((1, tt), N_EXPERT_GROUPS - 1, jnp.int32)
    for r in range(N_EXPERT_GROUPS - 2, -1, -1):
        gidx = jnp.where(gp[r:r + 1] == gw, r, gidx)
    ein = lt[(N_EXPERT_GROUPS - 1) * EXPERTS_PER_GROUP:N_EXPERTS]
    for r in range(N_EXPERT_GROUPS - 2, -1, -1):
        ein = jnp.where(gidx == r, lt[r * EXPERTS_PER_GROUP:(r + 1) * EXPERTS_PER_GROUP], ein)
    ee = jnp.exp(ein - jnp.max(ein, axis=0, keepdims=True))
    ep = ee / jnp.sum(ee, axis=0, keepdims=True)
    rowi = lax.broadcasted_iota(jnp.int32, (EXPERTS_PER_GROUP, tt), 0)
    p1 = jnp.max(ep, axis=0, keepdims=True)
    e1 = jnp.min(jnp.where(ep == p1, rowi, EXPERTS_PER_GROUP), axis=0, keepdims=True)
    ep2 = jnp.where(rowi == e1, -1.0, ep)
    p2 = jnp.max(ep2, axis=0, keepdims=True)
    e2 = jnp.min(jnp.where(ep2 == p2, rowi, EXPERTS_PER_GROUP), axis=0, keepdims=True)
    tot = p1 + p2
    w1 = p1 / tot * gw
    w2 = p2 / tot * gw
    id1 = (gidx * EXPERTS_PER_GROUP + e1).astype(F32)
    id2 = (gidx * EXPERTS_PER_GROUP + e2).astype(F32)
    route_ref[...] = jnp.concatenate([id1, id2, w1, w2, jnp.zeros((ROUTE_ROWS - 4, tt), F32)], axis=0)


def _merge(x, u, y, o_swa, o_mem, p, tile):
    n = x.shape[0]
    tile = min(tile, n)
    row = lambda i: (i, 0)
    const = lambda i: (0, 0)
    full = lambda a: pl.BlockSpec(a.shape, const, pipeline_mode=pl.Buffered(1))
    weights = [p['g1'], p['w_gates'], p['d_skip'], p['w_glu'], p['b_glu'], p['w_br_ssm'], p['w_br_swa'],
               p['w_br_mem'], p['w_out'], p['g2'], p['w_router'], p['b_router']]
    return pl.pallas_call(
        _merge_kernel,
        grid=(n // tile,),
        in_specs=[pl.BlockSpec((tile, D_MODEL), row), pl.BlockSpec((tile, SSM_WIDTH), row),
                  pl.BlockSpec((tile, SSM_WIDTH), row), pl.BlockSpec((tile, SWA_WIDTH), row),
                  pl.BlockSpec((tile, MEM_WIDTH), row)] + [full(w) for w in weights],
        out_specs=[pl.BlockSpec((tile, D_MODEL), row), pl.BlockSpec((tile, HALF), row),
                   pl.BlockSpec((ROUTE_ROWS, tile), lambda i: (0, i))],
        out_shape=[jax.ShapeDtypeStruct((n, D_MODEL), F32), jax.ShapeDtypeStruct((n, HALF), jnp.int32),
                   jax.ShapeDtypeStruct((ROUTE_ROWS, n), F32)],
        compiler_params=_cparams("parallel"),
        name="merge_router",
    )(x, u, y, o_swa, o_mem, *weights)


def _expert_mlp(xp, wg, wu, wd):
    hi, lo = _unpack_halves(xp)
    g = _dot(hi, wg[0:HALF, :]) + _dot(lo, wg[HALF:D_MODEL, :])
    u = _dot(hi, wu[0:HALF, :]) + _dot(lo, wu[HALF:D_MODEL, :])
    hh = jax.nn.silu(g) * u
    return _dot(hh.astype(BF16), wd[...])


def _moe_kernel(xn2_ref, rt_ref, wg_ref, wu_ref, wd_ref, h_ref, gf_ref, o_ref, acc_ref):
    e = pl.program_id(1)

    @pl.when(e == 0)
    def _():
        acc_ref[...] = jnp.zeros_like(acc_ref)

    o = _expert_mlp(xn2_ref[...], wg_ref[...].astype(BF16), wu_ref[...].astype(BF16), wd_ref[...].astype(BF16))
    ef = e.astype(F32)
    c = (jnp.where(rt_ref[:, 0:1] == ef, rt_ref[:, 2:3], 0.0)
         + jnp.where(rt_ref[:, 1:2] == ef, rt_ref[:, 3:4], 0.0))
    acc_ref[...] += c * o

    @pl.when(e == N_EXPERTS - 1)
    def _():
        o_ref[...] = _rms(h_ref[...] + acc_ref[...], gf_ref[...])


def _moe(xn2, route_t, w_g, w_u, w_d, h, gf, tile):
    n = h.shape[0]
    tile = min(tile, n)
    return pl.pallas_call(
        _moe_kernel,
        grid=(n // tile, N_EXPERTS),
        in_specs=[pl.BlockSpec((tile, HALF), lambda i, e: (i, 0)),
                  pl.BlockSpec((tile, ROUTE_ROWS), lambda i, e: (i, 0)),
                  pl.BlockSpec((None, D_MODEL, D_EXPERT), lambda i, e: (e, 0, 0)),
                  pl.BlockSpec((None, D_MODEL, D_EXPERT), lambda i, e: (e, 0, 0)),
                  pl.BlockSpec((None, D_EXPERT, D_MODEL), lambda i, e: (e, 0, 0)),
                  pl.BlockSpec((tile, D_MODEL), lambda i, e: (i, 0)),
                  pl.BlockSpec((1, D_MODEL), lambda i, e: (0, 0))],
        out_specs=pl.BlockSpec((tile, D_MODEL), lambda i, e: (i, 0)),
        out_shape=jax.ShapeDtypeStruct((n, D_MODEL), F32),
        scratch_shapes=[pltpu.VMEM((tile, D_MODEL), F32)],
        compiler_params=_cparams("parallel", "arbitrary"),
        name="moe_final_norm",
    )(xn2, route_t, w_g, w_u, w_d, h, gf)


EXPERT_ROW_TILE = 256
EXPERT_SLOTS = 4
SC_CORES = 2
SC_SUBCORES = 16
SC_WORKERS = SC_CORES * SC_SUBCORES
SC_SCATTER_ROWS = 64
SC_GATHER_ROWS = 64


def _route_rank_kernel(r_ref, rank_ref, cnt_ref, base_ref):
    i = pl.program_id(0)
    tt = r_ref.shape[1]

    @pl.when(i == 0)
    def _():
        base_ref[...] = jnp.zeros_like(base_ref)

    ids = r_ref[0:2, :].astype(jnp.int32)
    e_iota = lax.broadcasted_iota(jnp.int32, (N_EXPERTS, tt), 0)
    oh1 = jnp.where(e_iota == ids[0:1], 1.0, 0.0)
    oh2 = jnp.where(e_iota == ids[1:2], 1.0, 0.0)
    before = (lax.broadcasted_iota(jnp.int32, (tt, tt), 0) < lax.broadcasted_iota(jnp.int32, (tt, tt), 1))
    tri = jnp.where(before, 1.0, 0.0).astype(BF16)
    c1 = _dot(oh1.astype(BF16), tri)
    c2 = _dot(oh2.astype(BF16), tri)
    tot1 = jnp.sum(oh1, axis=1, keepdims=True)
    tot2 = jnp.sum(oh2, axis=1, keepdims=True)
    base = base_ref[:, 0:1]
    rank1 = jnp.sum(oh1 * (base + c1), axis=0, keepdims=True)
    rank2 = jnp.sum(oh2 * (base + tot1 + c2), axis=0, keepdims=True)
    rank_ref[...] = jnp.concatenate([rank1, rank2, jnp.zeros((ROUTE_ROWS - 2, tt), F32)], axis=0).astype(jnp.int32)
    new_base = jnp.broadcast_to(base + tot1 + tot2, base_ref.shape)
    base_ref[...] = new_base
    cnt_ref[...] = new_base.astype(jnp.int32)


def _route_rank(route, tile):
    n = route.shape[1]
    tile = min(tile, n)
    return pl.pallas_call(
        _route_rank_kernel,
        grid=(n // tile,),
        in_specs=[pl.BlockSpec((ROUTE_ROWS, tile), lambda i: (0, i))],
        out_specs=[pl.BlockSpec((ROUTE_ROWS, tile), lambda i: (0, i)),
                   pl.BlockSpec((N_EXPERTS, LANES), lambda i: (0, 0))],
        out_shape=[jax.ShapeDtypeStruct((ROUTE_ROWS, n), jnp.int32),
                   jax.ShapeDtypeStruct((N_EXPERTS, LANES), jnp.int32)],
        scratch_shapes=[pltpu.VMEM((N_EXPERTS, LANES), F32)],
        compiler_params=_cparams("arbitrary"),
        name="route_rank",
    )(route)


def _sc_mesh():
    return plsc.VectorSubcoreMesh(core_axis_name="core", subcore_axis_name="subcore")


def _sc_scatter_pairs(x, pos, rows_out):
    n, d = x.shape
    per_w = n // SC_WORKERS
    window = min(SC_SCATTER_ROWS, per_w)

    @pl.kernel(out_type=jax.ShapeDtypeStruct((rows_out, d), x.dtype), mesh=_sc_mesh(),
               scratch_types=[pltpu.VMEM((window,), jnp.int32), pltpu.VMEM((window,), jnp.int32),
                              pltpu.VMEM((window, d), x.dtype), pltpu.SemaphoreType.DMA, pltpu.SemaphoreType.DMA,
                              pltpu.SemaphoreType.DMA])
    def scatter(x_hbm, p_hbm, o_hbm, i1_v, i2_v, rows_v, sem_a, sem_b, sem_c):
        wid = lax.axis_index("subcore") * SC_CORES + lax.axis_index("core")

        @pl.loop(0, per_w // window)
        def _(j):
            base = wid * per_w + j * window
            load_i1 = pltpu.async_copy(p_hbm.at[pl.ds(base, window)], i1_v, sem_a)
            load_i2 = pltpu.async_copy(p_hbm.at[pl.ds(n + base, window)], i2_v, sem_b)
            load_x = pltpu.async_copy(x_hbm.at[pl.ds(base, window)], rows_v, sem_c)
            load_i1.wait()
            load_i2.wait()
            load_x.wait()
            put_1 = pltpu.async_copy(rows_v, o_hbm.at[i1_v], sem_a)
            put_2 = pltpu.async_copy(rows_v, o_hbm.at[i2_v], sem_b)
            put_1.wait()
            put_2.wait()

    return scatter(x, pos)


def _sc_gather_rows(table, idx):
    m = idx.shape[0]
    d = table.shape[1]
    per_w = m // SC_WORKERS
    window = min(SC_GATHER_ROWS, per_w)

    assert per_w % (2 * window) == 0

    @pl.kernel(out_type=jax.ShapeDtypeStruct((m, d), table.dtype), mesh=_sc_mesh(),
               scratch_types=[pltpu.VMEM((window,), jnp.int32), pltpu.VMEM((window,), jnp.int32),
                              pltpu.VMEM((window, d), table.dtype), pltpu.VMEM((window, d), table.dtype),
                              pltpu.SemaphoreType.DMA, pltpu.SemaphoreType.DMA])
    def gather(t_hbm, i_hbm, o_hbm, ia_v, ib_v, ra_v, rb_v, sem_a, sem_b):
        wid = lax.axis_index("subcore") * SC_CORES + lax.axis_index("core")

        @pl.loop(0, per_w // (2 * window))
        def _(j):
            base_a = wid * per_w + j * (2 * window)
            base_b = base_a + window
            idx_a = pltpu.async_copy(i_hbm.at[pl.ds(base_a, window)], ia_v, sem_a)
            idx_b = pltpu.async_copy(i_hbm.at[pl.ds(base_b, window)], ib_v, sem_b)
            idx_a.wait()
            get_a = pltpu.async_copy(t_hbm.at[ia_v], ra_v, sem_a)
            idx_b.wait()
            get_b = pltpu.async_copy(t_hbm.at[ib_v], rb_v, sem_b)
            get_a.wait()
            put_a = pltpu.async_copy(ra_v, o_hbm.at[pl.ds(base_a, window)], sem_a)
            get_b.wait()
            put_b = pltpu.async_copy(rb_v, o_hbm.at[pl.ds(base_b, window)], sem_b)
            put_a.wait()
            put_b.wait()

    return gather(table, idx)


def _expert_tiles_kernel(start_ref, ntile_ref, x_hbm, wg_ref, wu_ref, wd_ref, o_hbm,
                         wg_s, wu_s, wd_s, x_buf, o_buf, in_sem, out_sem):
    e = pl.program_id(0)
    tm = x_buf.shape[1]
    nslot = x_buf.shape[0]
    first = start_ref[e] // tm
    ntile = ntile_ref[e]
    total = start_ref[N_EXPERTS - 1] // tm + ntile_ref[N_EXPERTS - 1]
    wg_s[...] = wg_ref[...].astype(BF16)
    wu_s[...] = wu_ref[...].astype(BF16)
    wd_s[...] = wd_ref[...].astype(BF16)

    def rows_of(g):
        return pl.ds(pl.multiple_of(g * tm, tm), tm)

    def fetch(g):
        slot = g % nslot
        return pltpu.make_async_copy(x_hbm.at[rows_of(g)], x_buf.at[slot], in_sem.at[slot])

    def flush(g):
        slot = g % nslot
        return pltpu.make_async_copy(o_buf.at[slot], o_hbm.at[rows_of(g)], out_sem.at[slot])

    @pl.when(e == 0)
    def _():
        for k in range(nslot - 1):
            @pl.when(k < total)
            def _(k=k):
                fetch(k).start()

    def tile(g, carry):
        @pl.when(g + nslot - 1 < total)
        def _():
            fetch(g + nslot - 1).start()

        fetch(g).wait()

        @pl.when(g >= nslot)
        def _():
            flush(g - nslot).wait()

        slot = g % nslot
        o_buf[slot] = _pack_halves(_expert_mlp(x_buf[slot], wg_s, wu_s, wd_s).astype(BF16))
        flush(g).start()
        return carry

    lax.fori_loop(first, first + ntile, tile, 0)

    @pl.when(e == N_EXPERTS - 1)
    def _():
        for k in range(nslot, 0, -1):
            @pl.when(total >= k)
            def _(k=k):
                flush(total - k).wait()


def _expert_tiles(starts, ntiles, xs, w_g, w_u, w_d):
    rows = xs.shape[0]
    tm = EXPERT_ROW_TILE
    weight = lambda shape: pl.BlockSpec((None,) + shape, lambda e, st, nt: (e, 0, 0))
    grid_spec = pltpu.PrefetchScalarGridSpec(
        num_scalar_prefetch=2,
        grid=(N_EXPERTS,),
        in_specs=[pl.BlockSpec(memory_space=pl.ANY),
                  weight((D_MODEL, D_EXPERT)), weight((D_MODEL, D_EXPERT)), weight((D_EXPERT, D_MODEL))],
        out_specs=pl.BlockSpec(memory_space=pl.ANY),
        scratch_shapes=[pltpu.VMEM((D_MODEL, D_EXPERT), BF16), pltpu.VMEM((D_MODEL, D_EXPERT), BF16),
                        pltpu.VMEM((D_EXPERT, D_MODEL), BF16),
                        pltpu.VMEM((EXPERT_SLOTS, tm, HALF), jnp.int32), pltpu.VMEM((EXPERT_SLOTS, tm, HALF), jnp.int32),
                        pltpu.SemaphoreType.DMA((EXPERT_SLOTS,)), pltpu.SemaphoreType.DMA((EXPERT_SLOTS,))],
    )
    return pl.pallas_call(
        _expert_tiles_kernel,
        grid_spec=grid_spec,
        out_shape=jax.ShapeDtypeStruct((rows, HALF), jnp.int32),
        compiler_params=_cparams("arbitrary"),
        name="expert_tiles",
    )(starts, ntiles, xs, w_g, w_u, w_d)


def _unpack_f32(p):
    return pltpu.bitcast(p & jnp.int32(-65536), F32), pltpu.bitcast(lax.shift_left(p, jnp.int32(16)), F32)


def _combine_kernel(h_ref, o1_ref, o2_ref, rt_ref, gf_ref, y_ref):
    w1, w2 = rt_ref[:, 2:3], rt_ref[:, 3:4]
    a_lo, a_hi = _unpack_f32(o1_ref[...])
    b_lo, b_hi = _unpack_f32(o2_ref[...])
    y_lo = h_ref[:, 0:HALF] + (w1 * a_lo + w2 * b_lo)
    y_hi = h_ref[:, HALF:D_MODEL] + (w1 * a_hi + w2 * b_hi)
    ms = (jnp.sum(y_lo * y_lo, axis=-1, keepdims=True) + jnp.sum(y_hi * y_hi, axis=-1, keepdims=True)) / D_MODEL
    inv = lax.rsqrt(ms + EPS)
    y_ref[:, 0:HALF] = (y_lo * inv) * gf_ref[:, 0:HALF]
    y_ref[:, HALF:D_MODEL] = (y_hi * inv) * gf_ref[:, HALF:D_MODEL]


def _combine(h, o12, route_t, gf, tile):
    n = h.shape[0]
    tile = min(tile, n)
    nt = n // tile
    return pl.pallas_call(
        _combine_kernel,
        grid=(nt,),
        in_specs=[pl.BlockSpec((tile, D_MODEL), lambda i: (i, 0)),
                  pl.BlockSpec((tile, HALF), lambda i: (i, 0)),
                  pl.BlockSpec((tile, HALF), lambda i: (i + nt, 0)),
                  pl.BlockSpec((tile, ROUTE_ROWS), lambda i: (i, 0)),
                  pl.BlockSpec((1, D_MODEL), lambda i: (0, 0))],
        out_specs=pl.BlockSpec((tile, D_MODEL), lambda i: (i, 0)),
        out_shape=jax.ShapeDtypeStruct((n, D_MODEL), F32),
        compiler_params=_cparams("parallel"),
        name="combine_final_norm",
    )(h, o12, o12, route_t, gf)


def _sparse_moe(xn2p, route, h, w_g, w_u, w_d, gf, run_before_experts):
    n = h.shape[0]
    tm = EXPERT_ROW_TILE
    rows = 2 * n + N_EXPERTS * tm
    rank, cnt = _route_rank(route, 1024)
    counts = cnt[:, 0]
    padded = (counts + tm - 1) // tm * tm
    e_idx = jnp.arange(N_EXPERTS, dtype=jnp.int32)
    starts = jnp.sum(jnp.where(e_idx[None, :] < e_idx[:, None], padded[None, :], 0), axis=1)
    ids = route[0:2].astype(jnp.int32)
    start_of = jnp.sum(jnp.where(ids[None] == e_idx[:, None, None], starts[:, None, None], 0), axis=0)
    pos = (start_of + rank[0:2]).reshape(2 * n)
    xs = _sc_scatter_pairs(xn2p, pos, rows)
    xs, _ = lax.optimization_barrier((xs, run_before_experts))
    os_ = _expert_tiles(starts.astype(jnp.int32), (padded // tm).astype(jnp.int32), xs, w_g, w_u, w_d)
    o12 = _sc_gather_rows(os_, pos)
    return _combine(h, o12, route.T, gf, 512)


def _prep_in_weights(w_in):
    o = 0
    w_u = w_in[:, o:o + SSM_WIDTH]; o += SSM_WIDTH
    w_q = w_in[:, o:o + SWA_WIDTH]; o += SWA_WIDTH
    w_k = w_in[:, o:o + SWA_KV_WIDTH]; o += SWA_KV_WIDTH
    w_v = w_in[:, o:o + SWA_KV_WIDTH]; o += SWA_KV_WIDTH
    w_qm = w_in[:, o:o + MEM_WIDTH]; o += MEM_WIDTH
    assert o == GATES_COL0
    wq = (w_q * (SWA_HEAD_DIM ** -0.5)).reshape(D_MODEL, SWA_KV_HEADS, SWA_REP, SWA_HEAD_DIM)
    wq = wq.transpose(0, 2, 1, 3).reshape(D_MODEL, SWA_WIDTH)
    w_main = jnp.concatenate([w_u, wq, w_k, w_v, w_qm], axis=1).astype(BF16)
    return w_main, w_in.astype(BF16)


IN_SPLITS = (SSM_WIDTH, SWA_WIDTH, SWA_KV_WIDTH, SWA_KV_WIDTH, MEM_WIDTH)
IN_DTYPES = ((F32, BF16), (BF16,), (F32,), (F32,), (BF16,))


def kernel(x_prompt, x_sample, cache_swa_k, cache_swa_v, state_ssm_re, state_ssm_im, cache_mem_k, cache_mem_v, mem_prompt, norm1_g, w_in, lam_re, lam_im, log_dt, bm_re, bm_im, cm_re, cm_im, d_skip, w_glu, b_glu, sinks, rel_table, mem_norm_g, w_mem_kv, w_br_ssm, w_br_swa, w_br_mem, w_out, norm2_g, w_rg, b_rg, w_rexp, b_rexp, w_e_gate, w_e_up, w_e_down, final_norm_g):
    nb, t, _ = x_prompt.shape
    ns, ts, _ = x_sample.shape
    l = 0
    L = S5_CHUNK

    w_main, w_gates = _prep_in_weights(w_in[l])
    w_swa = (w_br_swa[l].reshape(SWA_KV_HEADS, SWA_REP, SWA_HEAD_DIM, D_MODEL).transpose(1, 0, 2, 3)
             .reshape(SWA_WIDTH, D_MODEL))
    pad_rows = ROUTER_ROWS - N_EXPERTS - N_EXPERT_GROUPS
    w_router = jnp.concatenate([w_rexp[l].T, w_rg[l].T, jnp.zeros((pad_rows, D_MODEL), F32)], axis=0).astype(BF16)
    b_router = jnp.concatenate([b_rexp[l], b_rg[l], jnp.zeros((pad_rows,), F32)]).reshape(ROUTER_ROWS, 1)
    mp = {
        'g1': norm1_g[l].reshape(1, D_MODEL), 'w_gates': w_gates, 'd_skip': d_skip[l].reshape(1, SSM_WIDTH),
        'w_glu': w_glu[l].astype(BF16), 'b_glu': b_glu[l].reshape(1, SSM_WIDTH),
        'w_br_ssm': w_br_ssm[l].astype(BF16), 'w_br_swa': w_swa.astype(BF16),
        'w_br_mem': w_br_mem[l].astype(BF16), 'w_out': w_out[l].astype(BF16),
        'g2': norm2_g[l].reshape(1, D_MODEL), 'w_router': w_router, 'b_router': b_router,
    }
    w_g, w_u, w_d = w_e_gate[l], w_e_up[l], w_e_down[l]
    gf = final_norm_g.reshape(1, D_MODEL)
    s5_w = _s5_weights(lam_re[l], lam_im[l], log_dt[l], bm_re[l], bm_im[l], cm_re[l], cm_im[l], L)

    bias_p = _rel_bias(rel_table, np.arange(WINDOW)[:, None] + WINDOW - np.arange(2 * WINDOW)[None, :])
    keys_s = WINDOW + 2 * ts
    bias_s = _rel_bias(rel_table, np.arange(ts)[:, None] + WINDOW - np.arange(keys_s)[None, :])
    bias_s = bias_s.reshape(SWA_HEADS * ts, keys_s)
    sink_rows = jnp.repeat(sinks[l].astype(F32), ts).reshape(SWA_HEADS * ts, 1)

    n = nb * t
    xp = x_prompt.reshape(n, D_MODEL)
    mk, mv = _norm_proj(mem_prompt.reshape(nb * MEM_TOKENS, D_MODEL), mem_norm_g[l].reshape(1, D_MODEL),
                        w_mem_kv[l].astype(BF16), (MEM_WIDTH, MEM_WIDTH), ((F32,), (F32,)), 512)
    u, ub, qz, k, v, qm = _norm_proj(xp, mp['g1'], w_main, IN_SPLITS, IN_DTYPES, 1024)

    y_ssm, fin = _s5(ub, jnp.zeros((nb, N_CH_TILES * 2 * STATE_TILE), F32), s5_w, nb, t // L, L, 64)
    p_re, p_im = _tiles_to_state(fin)

    o_swa = _swa_prompt(qz, k, v, bias_p, sinks[l].astype(F32), nb, t, 4)
    o_mem = _mem_prompt(qm, mk, mv, nb, t, 512)
    h, xn2p, route = _merge(xp, u, y_ssm, o_swa, o_mem, mp, 512)

    k4 = k.reshape(nb, t, SWA_KV_HEADS, SWA_HEAD_DIM)
    v4 = v.reshape(nb, t, SWA_KV_HEADS, SWA_HEAD_DIM)
    new_k_p, new_v_p = k4[:, -WINDOW:][None], v4[:, -WINDOW:][None]
    new_mk = mk.reshape(1, nb, MEM_TOKENS, MEM_HEADS, MEM_HEAD_DIM)
    new_mv = mv.reshape(1, nb, MEM_TOKENS, MEM_HEADS, MEM_HEAD_DIM)

    m = ns * ts
    xs = x_sample.reshape(m, D_MODEL)
    us, ubs, qzs, k_s, v_s, qms = _norm_proj(xs, mp['g1'], w_main, IN_SPLITS, IN_DTYPES, 256)
    ys_ssm, fins = _s5(ubs, _state_to_tiles(state_ssm_re[l], state_ssm_im[l]), s5_w, ns, ts // L, L, 64)
    s_re, s_im = _tiles_to_state(fins)

    kk_all = jnp.concatenate([cache_swa_k[l].reshape(ns, WINDOW, SWA_KV_WIDTH).astype(F32),
                              k_s.reshape(ns, ts, SWA_KV_WIDTH)], axis=1)
    vv_all = jnp.concatenate([cache_swa_v[l].reshape(ns, WINDOW, SWA_KV_WIDTH).astype(F32),
                              v_s.reshape(ns, ts, SWA_KV_WIDTH)], axis=1)
    pad = jnp.zeros((ns, keys_s - WINDOW - ts, SWA_KV_WIDTH), F32)
    o_dec, roll_k, roll_v = _swa_decode(qzs.astype(F32).reshape(ns, ts, SWA_WIDTH),
                                        jnp.concatenate([kk_all, pad], axis=1),
                                        jnp.concatenate([vv_all, pad], axis=1), bias_s, sink_rows, 8)
    o_swa_s = o_dec.reshape(m, SWA_WIDTH).astype(BF16)

    o_mem_s = _mem_decode(qms.astype(F32).reshape(ns, ts, MEM_WIDTH), cache_mem_k, cache_mem_v, l, 8)
    o_mem_s = o_mem_s.reshape(m, MEM_WIDTH).astype(BF16)

    y_prompt = _sparse_moe(xn2p, route, h, w_g, w_u, w_d, gf, (ys_ssm, o_swa_s, o_mem_s)).reshape(nb, t, D_MODEL)
    hs_, xn2ps, routes = _merge(xs, us, ys_ssm, o_swa_s, o_mem_s, mp, 256)
    y_sample = _moe(xn2ps, routes.T, w_g, w_u, w_d, hs_, gf, 1024).reshape(ns, ts, D_MODEL)

    new_k_s = roll_k.reshape(1, ns, WINDOW, SWA_KV_HEADS, SWA_HEAD_DIM).astype(cache_swa_k.dtype)
    new_v_s = roll_v.reshape(1, ns, WINDOW, SWA_KV_HEADS, SWA_HEAD_DIM).astype(cache_swa_v.dtype)

    return (y_prompt, y_sample,
            new_k_p, new_v_p, p_re[None], p_im[None], new_mk, new_mv,
            new_k_s, new_v_s, s_re[None].astype(state_ssm_re.dtype), s_im[None].astype(state_ssm_im.dtype))
```

```python
import functools
import math

import numpy as np
import jax
import jax.numpy as jnp
from jax import lax
from jax.experimental import pallas as pl
from jax.experimental.pallas import tpu as pltpu
from jax.experimental.pallas import tpu_sc as plsc

F32 = jnp.float32
BF16 = jnp.bfloat16

D_MODEL = 1024
SSM_WIDTH = 512
SSM_GROUP = 16
SSM_GROUPS = 32
SSM_STATE = 64
SWA_HEADS = 8
SWA_KV_HEADS = 2
SWA_REP = 4
SWA_HEAD_DIM = 64
SWA_WIDTH = 512
SWA_KV_WIDTH = 128
WINDOW = 128
REL_BUCKETS = 32
REL_MAX_DIST = 128
MEM_TOKENS = 256
MEM_HEADS = 4
MEM_HEAD_DIM = 128
MEM_WIDTH = 512
N_EXPERT_GROUPS = 4
EXPERTS_PER_GROUP = 8
N_EXPERTS = 32
D_EXPERT = 256
EPS = 1e-6
NEG_INF = -1e30

LANES = 128
GROUPS_PER_TILE = LANES // SSM_GROUP
N_CH_TILES = SSM_WIDTH // LANES
STATE_TILE = GROUPS_PER_TILE * SSM_STATE
VMEM_LIMIT = 56 * 1024 * 1024
S5_CHUNK = 8
S5_PANEL = 256

_TRANS_B = (((1,), (1,)), ((), ()))


def _cparams(*sem):
    return pltpu.CompilerParams(dimension_semantics=sem, vmem_limit_bytes=VMEM_LIMIT)


def _rms(x, g):
    return (x * lax.rsqrt(jnp.mean(x * x, axis=-1, keepdims=True) + EPS)) * g


def _dot(a, b):
    return jnp.dot(a, b, preferred_element_type=F32)


def _norm_proj_kernel(x_ref, g_ref, w_ref, *out_refs, splits, dtypes):
    xb = _rms(x_ref[...], g_ref[...]).astype(BF16)
    off = 0
    outs = iter(out_refs)
    for width, dts in zip(splits, dtypes):
        r = _dot(xb, w_ref[:, off:off + width])
        for dt in dts:
            next(outs)[...] = r.astype(dt)
        off += width


def _norm_proj(x, g, w, splits, dtypes, tile):
    n, d = x.shape
    tile = min(tile, n)
    flat = [(wd, dt) for wd, dts in zip(splits, dtypes) for dt in dts]
    return pl.pallas_call(
        functools.partial(_norm_proj_kernel, splits=tuple(splits), dtypes=tuple(dtypes)),
        grid=(n // tile,),
        in_specs=[pl.BlockSpec((tile, d), lambda i: (i, 0)),
                  pl.BlockSpec((1, d), lambda i: (0, 0)),
                  pl.BlockSpec((d, sum(splits)), lambda i: (0, 0), pipeline_mode=pl.Buffered(1))],
        out_specs=[pl.BlockSpec((tile, wd), lambda i: (i, 0)) for wd, _ in flat],
        out_shape=[jax.ShapeDtypeStruct((n, wd), dt) for wd, dt in flat],
        compiler_params=_cparams("parallel"),
        name="norm_proj",
    )(x, g, w)


def _s5_weights(lam_re, lam_im, log_dt, bm_re, bm_im, cm_re, cm_im, L):
    nt, gt, P, H = N_CH_TILES, GROUPS_PER_TILE, SSM_STATE, SSM_GROUP
    lr, li = lam_re.astype(F32), lam_im.astype(F32)
    dt = jnp.exp(log_dt.astype(F32))[:, None]
    mag = jnp.exp(lr * dt)
    a_re = mag * jnp.cos(li * dt)
    a_im = mag * jnp.sin(li * dt)
    den = lr * lr + li * li
    f_re = ((a_re - 1.0) * lr + a_im * li) / den
    f_im = (a_im * lr - (a_re - 1.0) * li) / den
    br, bi = bm_re.astype(F32), bm_im.astype(F32)
    bb_re = f_re[..., None] * br - f_im[..., None] * bi
    bb_im = f_re[..., None] * bi + f_im[..., None] * br
    pr, pi = [jnp.ones_like(a_re)], [jnp.zeros_like(a_im)]
    for _ in range(L):
        pr.append(pr[-1] * a_re - pi[-1] * a_im)
        pi.append(pr[-2] * a_im + pi[-1] * a_re)
    ap_re, ap_im = jnp.stack(pr), jnp.stack(pi)
    cr, ci = cm_re.astype(F32), cm_im.astype(F32)
    ca_re = cr[None] * ap_re[:, :, None, :] - ci[None] * ap_im[:, :, None, :]
    ca_im = cr[None] * ap_im[:, :, None, :] + ci[None] * ap_re[:, :, None, :]

    rev_re = jnp.stack([pr[L - 1 - s] for s in range(L)])
    rev_im = jnp.stack([pi[L - 1 - s] for s in range(L)])
    ws_re = rev_re[..., None] * bb_re[None] - rev_im[..., None] * bb_im[None]
    ws_im = rev_re[..., None] * bb_im[None] + rev_im[..., None] * bb_re[None]
    c_st = jnp.concatenate([ws_re.transpose(0, 1, 3, 2).reshape(L, nt, gt * H, P),
                            ws_im.transpose(0, 1, 3, 2).reshape(L, nt, gt * H, P)], axis=3).transpose(1, 0, 2, 3)
    so = lambda ca: ca[1:].transpose(1, 3, 0, 2).reshape(nt, gt * P, L * H)
    c_so = jnp.concatenate([so(ca_re), so(-ca_im)], axis=1)
    prod = (ca_re[:L][:, :, None, :, :] * bb_re.transpose(0, 2, 1)[None, :, :, None, :]
            - ca_im[:L][:, :, None, :, :] * bb_im.transpose(0, 2, 1)[None, :, :, None, :])
    k_lag = jnp.sum(prod, axis=-1).transpose(1, 2, 0, 3)
    c_k = k_lag.reshape(nt, gt * H, L * H)
    w_st, w_out, toep = _s5_expand(c_st, c_so, c_k, L)

    def per_tile(v):
        return v.reshape(nt, 1, STATE_TILE)

    return w_st, w_out, toep, per_tile(pr[L]), per_tile(pi[L])


def _s5_expand_kernel(cst_ref, cso_ref, ck_ref, wst_ref, wso_ref, toep_ref, *, L):
    hp = lax.Precision.HIGHEST
    P, H = SSM_STATE, SSM_GROUP
    iota = lambda shape, d: lax.broadcasted_iota(jnp.int32, shape, d)
    one = lambda cond: jnp.where(cond, 1.0, 0.0).astype(F32)

    r, c = iota((2 * P, 2 * STATE_TILE), 0), iota((2 * P, 2 * STATE_TILE), 1)
    rep_st = one((r // P == c // STATE_TILE) & (r % P == c % P))
    r, c = iota((LANES, 2 * STATE_TILE), 0), iota((LANES, 2 * STATE_TILE), 1)
    own_st = one(r // H == (c % STATE_TILE) // P)
    for s in range(L):
        blk = jnp.dot(cst_ref[s], rep_st, precision=hp, preferred_element_type=F32) * own_st
        wst_ref[s * LANES:(s + 1) * LANES, :] = blk.astype(BF16)

    r, c = iota((LANES, LANES), 0), iota((LANES, LANES), 1)
    pick = [one((r // H == t) & (r % H == c % H)) for t in range(L)]
    own_k = one(r // H == c // H)
    r, c = iota((2 * STATE_TILE, LANES), 0), iota((2 * STATE_TILE, LANES), 1)
    own_so = one((r % STATE_TILE) // P == c // H)
    cso = cso_ref[...]
    for t in range(L):
        blk = jnp.dot(cso, pick[t], precision=hp, preferred_element_type=F32) * own_so
        wso_ref[:, t * LANES:(t + 1) * LANES] = blk.astype(BF16)
    ck = ck_ref[...]
    lag = [(jnp.dot(ck, pick[t], precision=hp, preferred_element_type=F32) * own_k).astype(BF16) for t in range(L)]
    zero = jnp.zeros((LANES, LANES), BF16)
    for s in range(L):
        for t in range(L):
            toep_ref[s * LANES:(s + 1) * LANES, t * LANES:(t + 1) * LANES] = lag[t - s] if t >= s else zero


def _s5_expand(c_st, c_so, c_k, L):
    lk = L * LANES
    st2 = 2 * STATE_TILE
    return pl.pallas_call(
        functools.partial(_s5_expand_kernel, L=L),
        grid=(N_CH_TILES,),
        in_specs=[pl.BlockSpec((None, L, LANES, 2 * SSM_STATE), lambda j: (j, 0, 0, 0)),
                  pl.BlockSpec((None, st2, L * SSM_GROUP), lambda j: (j, 0, 0)),
                  pl.BlockSpec((None, LANES, L * SSM_GROUP), lambda j: (j, 0, 0))],
        out_specs=[pl.BlockSpec((None, lk, st2), lambda j: (j, 0, 0)),
                   pl.BlockSpec((None, st2, lk), lambda j: (j, 0, 0)),
                   pl.BlockSpec((None, lk, lk), lambda j: (j, 0, 0))],
        out_shape=[jax.ShapeDtypeStruct((N_CH_TILES, lk, st2), BF16),
                   jax.ShapeDtypeStruct((N_CH_TILES, st2, lk), BF16),
                   jax.ShapeDtypeStruct((N_CH_TILES, lk, lk), BF16)],
        compiler_params=_cparams("parallel"),
        name="s5_expand_weights",
    )(c_st, c_so, c_k)


def _to_chunks(u, nb, nc, L):
    return (u.reshape(nb, nc, L, N_CH_TILES, LANES).transpose(1, 0, 3, 2, 4)
            .reshape(nc * nb, N_CH_TILES * L * LANES))


def _from_chunks(y, nb, nc, L):
    return (y.reshape(nc, nb, N_CH_TILES, L, LANES).transpose(1, 0, 3, 2, 4)
            .reshape(nb * nc * L, SSM_WIDTH))


def _s5_kernel(x_ref, h0_ref, are_ref, aim_ref, ws_ref, t_ref, wo_ref, y_ref, fin_ref,
               hr_ref, hi_ref, d_ref, hs_ref, *, cb, nb):
    ci = pl.program_id(1)

    @pl.when(ci == 0)
    def _():
        hr_ref[...] = h0_ref[:, 0:STATE_TILE]
        hi_ref[...] = h0_ref[:, STATE_TILE:2 * STATE_TILE]

    x = x_ref[...]
    d_ref[...] = _dot(x, ws_ref[...])
    ar = jnp.broadcast_to(are_ref[...], (nb, STATE_TILE))
    ai = jnp.broadcast_to(aim_ref[...], (nb, STATE_TILE))

    def body(c, carry):
        hr, hi = carry
        r0 = pl.multiple_of(c * nb, nb)
        hs_ref[pl.ds(r0, nb), 0:STATE_TILE] = hr
        hs_ref[pl.ds(r0, nb), STATE_TILE:2 * STATE_TILE] = hi
        d = d_ref[pl.ds(r0, nb), :]
        return (ar * hr - ai * hi + d[:, 0:STATE_TILE],
                ar * hi + ai * hr + d[:, STATE_TILE:2 * STATE_TILE])

    hr, hi = lax.fori_loop(0, cb, body, (hr_ref[...], hi_ref[...]))
    hr_ref[...] = hr
    hi_ref[...] = hi
    hsb = hs_ref[...].astype(BF16)
    for c0 in range(0, t_ref.shape[1], S5_PANEL):
        c1 = c0 + S5_PANEL
        y_ref[:, c0:c1] = _dot(x[:, 0:c1], t_ref[0:c1, c0:c1]) + _dot(hsb, wo_ref[:, c0:c1])

    @pl.when(ci == pl.num_programs(1) - 1)
    def _():
        fin_ref[:, 0:STATE_TILE] = hr
        fin_ref[:, STATE_TILE:2 * STATE_TILE] = hi


def _s5(ub, h0, weights, nb, nc, L, chunk_block):
    w_st, w_so, toep, a_re, a_im = weights
    xc = _to_chunks(ub, nb, nc, L)
    cb = min(chunk_block, nc)
    rows = cb * nb
    lk = L * LANES
    st2 = 2 * STATE_TILE
    tile_w = lambda shape: pl.BlockSpec((None,) + shape, lambda j, c: (j, 0, 0))
    y, fin = pl.pallas_call(
        functools.partial(_s5_kernel, cb=cb, nb=nb),
        grid=(N_CH_TILES, nc // cb),
        in_specs=[pl.BlockSpec((rows, lk), lambda j, c: (c, j)),
                  pl.BlockSpec((nb, st2), lambda j, c: (0, j)),
                  tile_w((1, STATE_TILE)), tile_w((1, STATE_TILE)),
                  tile_w((lk, st2)), tile_w((lk, lk)), tile_w((st2, lk))],
        out_specs=[pl.BlockSpec((rows, lk), lambda j, c: (c, j)),
                   pl.BlockSpec((nb, st2), lambda j, c: (0, j))],
        out_shape=[jax.ShapeDtypeStruct((nc * nb, N_CH_TILES * lk), F32),
                   jax.ShapeDtypeStruct((nb, N_CH_TILES * st2), F32)],
        scratch_shapes=[pltpu.VMEM((nb, STATE_TILE), F32), pltpu.VMEM((nb, STATE_TILE), F32),
                        pltpu.VMEM((rows, st2), F32), pltpu.VMEM((rows, st2), F32)],
        compiler_params=_cparams("parallel", "arbitrary"),
        name="s5_chunked_scan",
    )(xc, h0, a_re, a_im, w_st, toep, w_so)
    return _from_chunks(y, nb, nc, L), fin


def _state_to_tiles(h_re, h_im):
    nb = h_re.shape[0]
    r = h_re.astype(F32).reshape(nb, N_CH_TILES, STATE_TILE)
    i = h_im.astype(F32).reshape(nb, N_CH_TILES, STATE_TILE)
    return jnp.concatenate([r, i], axis=-1).reshape(nb, N_CH_TILES * 2 * STATE_TILE)


def _tiles_to_state(h):
    nb = h.shape[0]
    h = h.reshape(nb, N_CH_TILES, 2, GROUPS_PER_TILE, SSM_STATE)
    return (h[:, :, 0].reshape(nb, SSM_GROUPS, SSM_STATE), h[:, :, 1].reshape(nb, SSM_GROUPS, SSM_STATE))


def _t5_bucket(dist):
    n = np.maximum(dist, 0)
    max_exact = REL_BUCKETS // 2
    nf = np.maximum(n, 1).astype(np.float32)
    large = max_exact + (np.log(nf / np.float32(max_exact)) / np.float32(math.log(REL_MAX_DIST / max_exact))
                         * np.float32(REL_BUCKETS - max_exact)).astype(np.int32)
    large = np.minimum(large, REL_BUCKETS - 1)
    return np.where(n < max_exact, n, large)


def _rel_bias(rel_table, dist):
    bucket = _t5_bucket(dist)
    tab = rel_table.astype(F32)
    out = jnp.zeros((SWA_HEADS,) + dist.shape, F32)
    for b in range(REL_BUCKETS):
        sel = jnp.asarray(bucket == b)
        if bool((bucket == b).any()):
            out = jnp.where(sel[None], tab[b].reshape((SWA_HEADS,) + (1,) * dist.ndim), out)
    return out


def _swa_prompt_kernel(sink_ref, q_ref, kp_ref, kc_ref, vp_ref, vc_ref, bias_ref, o_ref, kk_ref, vv_ref, *, qblocks):
    step = pl.program_id(1)
    kk_ref[0:WINDOW, :] = kp_ref[...].astype(BF16)
    kk_ref[WINDOW:, :] = kc_ref[...].astype(BF16)
    vv_ref[0:WINDOW, :] = vp_ref[...].astype(BF16)
    vv_ref[WINDOW:, :] = vc_ref[...].astype(BF16)
    row = lax.broadcasted_iota(jnp.int32, (WINDOW, 2 * WINDOW), 0)
    col = lax.broadcasted_iota(jnp.int32, (WINDOW, 2 * WINDOW), 1)
    dist = row + WINDOW - col
    band = (dist >= 0) & (dist < WINDOW)
    lane = lax.broadcasted_iota(jnp.int32, (WINDOW, LANES), 1)
    low = lane < SWA_HEAD_DIM

    def block(j, carry):
        r0 = pl.multiple_of(j * WINDOW, WINDOW)
        kk = kk_ref[pl.ds(r0, 2 * WINDOW), :]
        vv = vv_ref[pl.ds(r0, 2 * WINDOW), :]
        valid = band & ((col >= WINDOW) | (step * qblocks + j > 0))
        for t in range(SWA_REP):
            q2 = q_ref[pl.ds(r0, WINDOW), t * LANES:(t + 1) * LANES]
            outs = []
            for half in range(SWA_KV_HEADS):
                h = t + SWA_REP * half
                qh = jnp.where(low if half == 0 else jnp.logical_not(low), q2, jnp.zeros_like(q2))
                s = lax.dot_general(qh, kk, _TRANS_B, preferred_element_type=F32)
                s = jnp.where(valid, s + bias_ref[h], NEG_INF)
                sink = sink_ref[h]
                m = jnp.maximum(jnp.max(s, axis=-1, keepdims=True), sink)
                e = jnp.exp(s - m)
                den = jnp.sum(e, axis=-1, keepdims=True) + jnp.exp(sink - m)
                outs.append(_dot(e.astype(BF16), vv) * (1.0 / den))
            o_ref[pl.ds(r0, WINDOW), t * LANES:(t + 1) * LANES] = jnp.where(low, outs[0], outs[1]).astype(BF16)
        return carry

    lax.fori_loop(0, qblocks, block, 0)


def _swa_prompt(q, k, v, bias, sinks, nb, t, qblocks):
    nstep = t // (WINDOW * qblocks)
    rows = WINDOW * qblocks
    cur = lambda b, i: (b * nstep + i, 0)
    prev = lambda b, i: (b * nstep * qblocks + jnp.maximum(i * qblocks - 1, 0), 0)
    return pl.pallas_call(
        functools.partial(_swa_prompt_kernel, qblocks=qblocks),
        grid=(nb, nstep),
        in_specs=[pl.BlockSpec(memory_space=pltpu.SMEM),
                  pl.BlockSpec((rows, SWA_WIDTH), cur),
                  pl.BlockSpec((WINDOW, SWA_KV_WIDTH), prev),
                  pl.BlockSpec((rows, SWA_KV_WIDTH), cur),
                  pl.BlockSpec((WINDOW, SWA_KV_WIDTH), prev),
                  pl.BlockSpec((rows, SWA_KV_WIDTH), cur),
                  pl.BlockSpec((SWA_HEADS, WINDOW, 2 * WINDOW), lambda b, i: (0, 0, 0))],
        out_specs=pl.BlockSpec((rows, SWA_WIDTH), cur),
        out_shape=jax.ShapeDtypeStruct((nb * t, SWA_WIDTH), BF16),
        scratch_shapes=[pltpu.VMEM((rows + WINDOW, SWA_KV_WIDTH), BF16),
                        pltpu.VMEM((rows + WINDOW, SWA_KV_WIDTH), BF16)],
        compiler_params=_cparams("parallel", "parallel"),
        name="swa_prompt",
    )(sinks, q, k, k, v, v, bias)


def _swa_decode_kernel(q_ref, k_ref, v_ref, bias_ref, sink_ref, o_ref, nk_ref, nv_ref, *, seqs, tq):
    rows, keys = SWA_HEADS * tq, k_ref.shape[1]
    nk_ref[...] = k_ref[:, tq:tq + WINDOW, :]
    nv_ref[...] = v_ref[:, tq:tq + WINDOW, :]
    low = lax.broadcasted_iota(jnp.int32, (tq, LANES), 1) < SWA_HEAD_DIM
    qi = lax.broadcasted_iota(jnp.int32, (rows, keys), 0) % tq
    col = lax.broadcasted_iota(jnp.int32, (rows, keys), 1)
    dist = qi + WINDOW - col
    valid = (dist >= 0) & (dist < WINDOW)
    bias = bias_ref[...]
    sink = sink_ref[...]
    for s_i in range(seqs):
        q = q_ref[s_i]
        tiles = [q[:, t * LANES:(t + 1) * LANES] for t in range(SWA_REP)]
        qh = jnp.concatenate([jnp.where(low, x, 0.0) for x in tiles]
                             + [jnp.where(low, 0.0, x) for x in tiles], axis=0)
        kk = k_ref[s_i].astype(BF16)
        s = lax.dot_general(qh.astype(BF16), kk, _TRANS_B, preferred_element_type=F32)
        s = jnp.where(valid, s + bias, NEG_INF)
        m = jnp.maximum(jnp.max(s, axis=-1, keepdims=True), sink)
        e = jnp.exp(s - m)
        den = jnp.sum(e, axis=-1, keepdims=True) + jnp.exp(sink - m)
        o = _dot(e.astype(BF16), v_ref[s_i].astype(BF16)) * (1.0 / den)
        for t in range(SWA_REP):
            o_ref[s_i, :, t * LANES:(t + 1) * LANES] = jnp.where(
                low, o[t * tq:(t + 1) * tq], o[(t + SWA_REP) * tq:(t + SWA_REP + 1) * tq])


def _swa_decode(q, k_all, v_all, bias, sink_rows, seqs):
    nseq, tq, _ = q.shape
    rows = SWA_HEADS * tq
    keys = k_all.shape[1]
    seqs = min(seqs, nseq)
    return pl.pallas_call(
        functools.partial(_swa_decode_kernel, seqs=seqs, tq=tq),
        grid=(nseq // seqs,),
        in_specs=[pl.BlockSpec((seqs, tq, SWA_WIDTH), lambda i: (i, 0, 0)),
                  pl.BlockSpec((seqs, keys, LANES), lambda i: (i, 0, 0)),
                  pl.BlockSpec((seqs, keys, LANES), lambda i: (i, 0, 0)),
                  pl.BlockSpec((rows, keys), lambda i: (0, 0)),
                  pl.BlockSpec((rows, 1), lambda i: (0, 0))],
        out_specs=[pl.BlockSpec((seqs, tq, SWA_WIDTH), lambda i: (i, 0, 0)),
                   pl.BlockSpec((seqs, WINDOW, LANES), lambda i: (i, 0, 0)),
                   pl.BlockSpec((seqs, WINDOW, LANES), lambda i: (i, 0, 0))],
        out_shape=[jax.ShapeDtypeStruct((nseq, tq, SWA_WIDTH), F32),
                   jax.ShapeDtypeStruct((nseq, WINDOW, LANES), F32),
                   jax.ShapeDtypeStruct((nseq, WINDOW, LANES), F32)],
        compiler_params=_cparams("parallel"),
        name="swa_decode",
    )(q, k_all, v_all, bias, sink_rows)


def _softmax(s):
    m = jnp.max(s, axis=-1, keepdims=True)
    e = jnp.exp(s - m)
    return e * (1.0 / jnp.sum(e, axis=-1, keepdims=True))


def _mem_prompt_kernel(q_ref, k_ref, v_ref, o_ref, s_ref, p_ref):
    scale = MEM_HEAD_DIM ** -0.5
    heads = [slice(h * MEM_HEAD_DIM, (h + 1) * MEM_HEAD_DIM) for h in range(MEM_HEADS)]
    for h, sl in enumerate(heads):
        s_ref[h] = lax.dot_general(q_ref[:, sl], k_ref[:, sl].astype(BF16), _TRANS_B, preferred_element_type=F32)
    s = s_ref[...] * scale
    e = jnp.exp(s - jnp.max(s, axis=-1, keepdims=True))
    p_ref[...] = e.astype(BF16)
    inv = 1.0 / jnp.sum(e, axis=-1, keepdims=True)
    for h, sl in enumerate(heads):
        o_ref[:, sl] = (_dot(p_ref[h], v_ref[:, sl].astype(BF16)) * inv[h]).astype(BF16)


def _mem_prompt(qm, mk, mv, nb, t, tile):
    tile = min(tile, t)
    nt = t // tile
    return pl.pallas_call(
        _mem_prompt_kernel,
        grid=(nb, nt),
        in_specs=[pl.BlockSpec((tile, MEM_WIDTH), lambda b, i: (b * nt + i, 0)),
                  pl.BlockSpec((MEM_TOKENS, MEM_WIDTH), lambda b, i: (b, 0)),
                  pl.BlockSpec((MEM_TOKENS, MEM_WIDTH), lambda b, i: (b, 0))],
        out_specs=pl.BlockSpec((tile, MEM_WIDTH), lambda b, i: (b * nt + i, 0)),
        out_shape=jax.ShapeDtypeStruct((nb * t, MEM_WIDTH), BF16),
        scratch_shapes=[pltpu.VMEM((MEM_HEADS, tile, MEM_TOKENS), F32), pltpu.VMEM((MEM_HEADS, tile, MEM_TOKENS), BF16)],
        compiler_params=_cparams("parallel", "parallel"),
        name="mem_prompt",
    )(qm, mk, mv)


def _mem_decode_kernel(q_ref, k_ref, v_ref, o_ref, *, seqs):
    tq = q_ref.shape[1]
    rows, cols = MEM_HEADS * tq, MEM_TOKENS * MEM_HEADS
    k2 = k_ref.reshape(seqs, cols, MEM_HEAD_DIM)
    v2 = v_ref.reshape(seqs, cols, MEM_HEAD_DIM)
    scale = MEM_HEAD_DIM ** -0.5
    own = (lax.broadcasted_iota(jnp.int32, (rows, cols), 1) % MEM_HEADS
           == lax.broadcasted_iota(jnp.int32, (rows, cols), 0) // tq)
    for s_i in range(seqs):
        q = q_ref[s_i]
        qb = jnp.concatenate([q[:, h * MEM_HEAD_DIM:(h + 1) * MEM_HEAD_DIM] for h in range(MEM_HEADS)], axis=0)
        s = lax.dot_general(qb.astype(BF16), k2[s_i].astype(BF16), _TRANS_B, preferred_element_type=F32) * scale
        p = _softmax(jnp.where(own, s, NEG_INF)).astype(BF16)
        o = _dot(p, v2[s_i].astype(BF16))
        for h in range(MEM_HEADS):
            o_ref[s_i, :, h * MEM_HEAD_DIM:(h + 1) * MEM_HEAD_DIM] = o[h * tq:(h + 1) * tq, :]


def _mem_decode(q, k, v, layer, seqs):
    nseq, tq, _ = q.shape
    seqs = min(seqs, nseq)
    cache = pl.BlockSpec((None, seqs, MEM_TOKENS, MEM_HEADS, MEM_HEAD_DIM), lambda i: (layer, i, 0, 0, 0))
    return pl.pallas_call(
        functools.partial(_mem_decode_kernel, seqs=seqs),
        grid=(nseq // seqs,),
        in_specs=[pl.BlockSpec((seqs, tq, MEM_WIDTH), lambda i: (i, 0, 0)), cache, cache],
        out_specs=pl.BlockSpec((seqs, tq, MEM_WIDTH), lambda i: (i, 0, 0)),
        out_shape=jax.ShapeDtypeStruct((nseq, tq, MEM_WIDTH), F32),
        compiler_params=_cparams("parallel"),
        name="mem_decode",
    )(q, k, v)


ROUTER_ROWS = 40
GATES_COL0 = SSM_WIDTH + SWA_WIDTH + 2 * SWA_KV_WIDTH + MEM_WIDTH
ROUTE_ROWS = 8
HALF = D_MODEL // 2


def _pack_halves(xb):
    hi = pltpu.bitcast(xb[:, 0:HALF].astype(F32), jnp.int32)
    lo = pltpu.bitcast(xb[:, HALF:D_MODEL].astype(F32), jnp.int32)
    return hi | lax.shift_right_logical(lo, jnp.int32(16))


def _unpack_halves(p):
    hi = pltpu.bitcast(p & jnp.int32(-65536), F32).astype(BF16)
    lo = pltpu.bitcast(lax.shift_left(p, jnp.int32(16)), F32).astype(BF16)
    return hi, lo


def _merge_kernel(x_ref, u_ref, y_ref, os_ref, om_ref, g1_ref, wg_ref, dsk_ref, wglu_ref, bglu_ref,
                  wbs_ref, wbw_ref, wbm_ref, wout_ref, g2_ref, wr_ref, br_ref,
                  h_ref, xn2_ref, route_ref):
    x = x_ref[...]
    tt = x.shape[0]
    xb = _rms(x, g1_ref[...]).astype(BF16)
    z = jax.nn.gelu(y_ref[...] + dsk_ref[...] * u_ref[...])
    z = z * jax.nn.sigmoid(_dot(z.astype(BF16), wglu_ref[...]) + bglu_ref[...])
    gate = lambda b: jax.nn.sigmoid(_dot(xb, wg_ref[:, GATES_COL0 + b * D_MODEL:GATES_COL0 + (b + 1) * D_MODEL]))
    merged = gate(0) * _dot(z.astype(BF16), wbs_ref[...])
    merged = merged + gate(1) * _dot(os_ref[...], wbw_ref[...])
    merged = merged + gate(2) * _dot(om_ref[...], wbm_ref[...])
    h = x + _dot(merged.astype(BF16), wout_ref[...])
    h_ref[...] = h
    xn2 = _rms(h, g2_ref[...]).astype(BF16)
    xn2_ref[...] = _pack_halves(xn2)

    lt = lax.dot_general(wr_ref[...], xn2, _TRANS_B, preferred_element_type=F32) + br_ref[...]
    gl = lt[N_EXPERTS:N_EXPERTS + N_EXPERT_GROUPS]
    ge = jnp.exp(gl - jnp.max(gl, axis=0, keepdims=True))
    gp = ge / jnp.sum(ge, axis=0, keepdims=True)
    gw = jnp.max(gp, axis=0, keepdims=True)
    gidx = jnp.full((1, tt), N_EXPERT_GROUPS - 1, jnp.int32)
    for r in range(N_EXPERT_GROUPS - 2, -1, -1):
        gidx = jnp.where(gp[r:r + 1] == gw, r, gidx)
    ein = lt[(N_EXPERT_GROUPS - 1) * EXPERTS_PER_GROUP:N_EXPERTS]
    for r in range(N_EXPERT_GROUPS - 2, -1, -1):
        ein = jnp.where(gidx == r, lt[r * EXPERTS_PER_GROUP:(r + 1) * EXPERTS_PER_GROUP], ein)
    ee = jnp.exp(ein - jnp.max(ein, axis=0, keepdims=True))
    ep = ee / jnp.sum(ee, axis=0, keepdims=True)
    rowi = lax.broadcasted_iota(jnp.int32, (EXPERTS_PER_GROUP, tt), 0)
    p1 = jnp.max(ep, axis=0, keepdims=True)
    e1 = jnp.min(jnp.where(ep == p1, rowi, EXPERTS_PER_GROUP), axis=0, keepdims=True)
    ep2 = jnp.where(rowi == e1, -1.0, ep)
    p2 = jnp.max(ep2, axis=0, keepdims=True)
    e2 = jnp.min(jnp.where(ep2 == p2, rowi, EXPERTS_PER_GROUP), axis=0, keepdims=True)
    tot = p1 + p2
    w1 = p1 / tot * gw
    w2 = p2 / tot * gw
    id1 = (gidx * EXPERTS_PER_GROUP + e1).astype(F32)
    id2 = (gidx * EXPERTS_PER_GROUP + e2).astype(F32)
    route_ref[...] = jnp.concatenate([id1, id2, w1, w2, jnp.zeros((ROUTE_ROWS - 4, tt), F32)], axis=0)


def _merge(x, u, y, o_swa, o_mem, p, tile):
    n = x.shape[0]
    tile = min(tile, n)
    row = lambda i: (i, 0)
    const = lambda i: (0, 0)
    full = lambda a: pl.BlockSpec(a.shape, const, pipeline_mode=pl.Buffered(1))
    weights = [p['g1'], p['w_gates'], p['d_skip'], p['w_glu'], p['b_glu'], p['w_br_ssm'], p['w_br_swa'],
               p['w_br_mem'], p['w_out'], p['g2'], p['w_router'], p['b_router']]
    return pl.pallas_call(
        _merge_kernel,
        grid=(n // tile,),
        in_specs=[pl.BlockSpec((tile, D_MODEL), row), pl.BlockSpec((tile, SSM_WIDTH), row),
                  pl.BlockSpec((tile, SSM_WIDTH), row), pl.BlockSpec((tile, SWA_WIDTH), row),
                  pl.BlockSpec((tile, MEM_WIDTH), row)] + [full(w) for w in weights],
        out_specs=[pl.BlockSpec((tile, D_MODEL), row), pl.BlockSpec((tile, HALF), row),
                   pl.BlockSpec((ROUTE_ROWS, tile), lambda i: (0, i))],
        out_shape=[jax.ShapeDtypeStruct((n, D_MODEL), F32), jax.ShapeDtypeStruct((n, HALF), jnp.int32),
                   jax.ShapeDtypeStruct((ROUTE_ROWS, n), F32)],
        compiler_params=_cparams("parallel"),
        name="merge_router",
    )(x, u, y, o_swa, o_mem, *weights)


def _expert_mlp(xp, wg, wu, wd):
    hi, lo = _unpack_halves(xp)
    g = _dot(hi, wg[0:HALF, :]) + _dot(lo, wg[HALF:D_MODEL, :])
    u = _dot(hi, wu[0:HALF, :]) + _dot(lo, wu[HALF:D_MODEL, :])
    hh = jax.nn.silu(g) * u
    return _dot(hh.astype(BF16), wd[...])


def _moe_kernel(xn2_ref, rt_ref, wg_ref, wu_ref, wd_ref, h_ref, gf_ref, o_ref, acc_ref):
    e = pl.program_id(1)

    @pl.when(e == 0)
    def _():
        acc_ref[...] = jnp.zeros_like(acc_ref)

    o = _expert_mlp(xn2_ref[...], wg_ref[...].astype(BF16), wu_ref[...].astype(BF16), wd_ref[...].astype(BF16))
    ef = e.astype(F32)
    c = (jnp.where(rt_ref[:, 0:1] == ef, rt_ref[:, 2:3], 0.0)
         + jnp.where(rt_ref[:, 1:2] == ef, rt_ref[:, 3:4], 0.0))
    acc_ref[...] += c * o

    @pl.when(e == N_EXPERTS - 1)
    def _():
        o_ref[...] = _rms(h_ref[...] + acc_ref[...], gf_ref[...])


def _moe(xn2, route_t, w_g, w_u, w_d, h, gf, tile):
    n = h.shape[0]
    tile = min(tile, n)
    return pl.pallas_call(
        _moe_kernel,
        grid=(n // tile, N_EXPERTS),
        in_specs=[pl.BlockSpec((tile, HALF), lambda i, e: (i, 0)),
                  pl.BlockSpec((tile, ROUTE_ROWS), lambda i, e: (i, 0)),
                  pl.BlockSpec((None, D_MODEL, D_EXPERT), lambda i, e: (e, 0, 0)),
                  pl.BlockSpec((None, D_MODEL, D_EXPERT), lambda i, e: (e, 0, 0)),
                  pl.BlockSpec((None, D_EXPERT, D_MODEL), lambda i, e: (e, 0, 0)),
                  pl.BlockSpec((tile, D_MODEL), lambda i, e: (i, 0)),
                  pl.BlockSpec((1, D_MODEL), lambda i, e: (0, 0))],
        out_specs=pl.BlockSpec((tile, D_MODEL), lambda i, e: (i, 0)),
        out_shape=jax.ShapeDtypeStruct((n, D_MODEL), F32),
        scratch_shapes=[pltpu.VMEM((tile, D_MODEL), F32)],
        compiler_params=_cparams("parallel", "arbitrary"),
        name="moe_final_norm",
    )(xn2, route_t, w_g, w_u, w_d, h, gf)


EXPERT_ROW_TILE = 256
EXPERT_SLOTS = 4
SC_CORES = 2
SC_SUBCORES = 16
SC_WORKERS = SC_CORES * SC_SUBCORES
SC_SCATTER_ROWS = 64
SC_GATHER_ROWS = 64


def _route_rank_kernel(r_ref, rank_ref, cnt_ref, base_ref):
    i = pl.program_id(0)
    tt = r_ref.shape[1]

    @pl.when(i == 0)
    def _():
        base_ref[...] = jnp.zeros_like(base_ref)

    ids = r_ref[0:2, :].astype(jnp.int32)
    e_iota = lax.broadcasted_iota(jnp.int32, (N_EXPERTS, tt), 0)
    oh1 = jnp.where(e_iota == ids[0:1], 1.0, 0.0)
    oh2 = jnp.where(e_iota == ids[1:2], 1.0, 0.0)
    before = (lax.broadcasted_iota(jnp.int32, (tt, tt), 0) < lax.broadcasted_iota(jnp.int32, (tt, tt), 1))
    tri = jnp.where(before, 1.0, 0.0).astype(BF16)
    c1 = _dot(oh1.astype(BF16), tri)
    c2 = _dot(oh2.astype(BF16), tri)
    tot1 = jnp.sum(oh1, axis=1, keepdims=True)
    tot2 = jnp.sum(oh2, axis=1, keepdims=True)
    base = base_ref[:, 0:1]
    rank1 = jnp.sum(oh1 * (base + c1), axis=0, keepdims=True)
    rank2 = jnp.sum(oh2 * (base + tot1 + c2), axis=0, keepdims=True)
    rank_ref[...] = jnp.concatenate([rank1, rank2, jnp.zeros((ROUTE_ROWS - 2, tt), F32)], axis=0).astype(jnp.int32)
    new_base = jnp.broadcast_to(base + tot1 + tot2, base_ref.shape)
    base_ref[...] = new_base
    cnt_ref[...] = new_base.astype(jnp.int32)


def _route_rank(route, tile):
    n = route.shape[1]
    tile = min(tile, n)
    return pl.pallas_call(
        _route_rank_kernel,
        grid=(n // tile,),
        in_specs=[pl.BlockSpec((ROUTE_ROWS, tile), lambda i: (0, i))],
        out_specs=[pl.BlockSpec((ROUTE_ROWS, tile), lambda i: (0, i)),
                   pl.BlockSpec((N_EXPERTS, LANES), lambda i: (0, 0))],
        out_shape=[jax.ShapeDtypeStruct((ROUTE_ROWS, n), jnp.int32),
                   jax.ShapeDtypeStruct((N_EXPERTS, LANES), jnp.int32)],
        scratch_shapes=[pltpu.VMEM((N_EXPERTS, LANES), F32)],
        compiler_params=_cparams("arbitrary"),
        name="route_rank",
    )(route)


def _sc_mesh():
    return plsc.VectorSubcoreMesh(core_axis_name="core", subcore_axis_name="subcore")


def _sc_scatter_pairs(x, pos, rows_out):
    n, d = x.shape
    per_w = n // SC_WORKERS
    window = min(SC_SCATTER_ROWS, per_w)

    @pl.kernel(out_type=jax.ShapeDtypeStruct((rows_out, d), x.dtype), mesh=_sc_mesh(),
               scratch_types=[pltpu.VMEM((window,), jnp.int32), pltpu.VMEM((window,), jnp.int32),
                              pltpu.VMEM((window, d), x.dtype), pltpu.SemaphoreType.DMA, pltpu.SemaphoreType.DMA,
                              pltpu.SemaphoreType.DMA])
    def scatter(x_hbm, p_hbm, o_hbm, i1_v, i2_v, rows_v, sem_a, sem_b, sem_c):
        wid = lax.axis_index("subcore") * SC_CORES + lax.axis_index("core")

        @pl.loop(0, per_w // window)
        def _(j):
            base = wid * per_w + j * window
            load_i1 = pltpu.async_copy(p_hbm.at[pl.ds(base, window)], i1_v, sem_a)
            load_i2 = pltpu.async_copy(p_hbm.at[pl.ds(n + base, window)], i2_v, sem_b)
            load_x = pltpu.async_copy(x_hbm.at[pl.ds(base, window)], rows_v, sem_c)
            load_i1.wait()
            load_i2.wait()
            load_x.wait()
            put_1 = pltpu.async_copy(rows_v, o_hbm.at[i1_v], sem_a)
            put_2 = pltpu.async_copy(rows_v, o_hbm.at[i2_v], sem_b)
            put_1.wait()
            put_2.wait()

    return scatter(x, pos)


def _sc_gather_rows(table, idx):
    m = idx.shape[0]
    d = table.shape[1]
    per_w = m // SC_WORKERS
    window = min(SC_GATHER_ROWS, per_w)

    assert per_w % (2 * window) == 0

    @pl.kernel(out_type=jax.ShapeDtypeStruct((m, d), table.dtype), mesh=_sc_mesh(),
               scratch_types=[pltpu.VMEM((window,), jnp.int32), pltpu.VMEM((window,), jnp.int32),
                              pltpu.VMEM((window, d), table.dtype), pltpu.VMEM((window, d), table.dtype),
                              pltpu.SemaphoreType.DMA, pltpu.SemaphoreType.DMA])
    def gather(t_hbm, i_hbm, o_hbm, ia_v, ib_v, ra_v, rb_v, sem_a, sem_b):
        wid = lax.axis_index("subcore") * SC_CORES + lax.axis_index("core")

        @pl.loop(0, per_w // (2 * window))
        def _(j):
            base_a = wid * per_w + j * (2 * window)
            base_b = base_a + window
            idx_a = pltpu.async_copy(i_hbm.at[pl.ds(base_a, window)], ia_v, sem_a)
            idx_b = pltpu.async_copy(i_hbm.at[pl.ds(base_b, window)], ib_v, sem_b)
            idx_a.wait()
            get_a = pltpu.async_copy(t_hbm.at[ia_v], ra_v, sem_a)
            idx_b.wait()
            get_b = pltpu.async_copy(t_hbm.at[ib_v], rb_v, sem_b)
            get_a.wait()
            put_a = pltpu.async_copy(ra_v, o_hbm.at[pl.ds(base_a, window)], sem_a)
            get_b.wait()
            put_b = pltpu.async_copy(rb_v, o_hbm.at[pl.ds(base_b, window)], sem_b)
            put_a.wait()
            put_b.wait()

    return gather(table, idx)


def _expert_tiles_kernel(start_ref, ntile_ref, x_hbm, wg_ref, wu_ref, wd_ref, o_hbm,
                         wg_s, wu_s, wd_s, x_buf, o_buf, in_sem, out_sem):
    e = pl.program_id(0)
    tm = x_buf.shape[1]
    nslot = x_buf.shape[0]
    first = start_ref[e] // tm
    ntile = ntile_ref[e]
    total = start_ref[N_EXPERTS - 1] // tm + ntile_ref[N_EXPERTS - 1]
    wg_s[...] = wg_ref[...].astype(BF16)
    wu_s[...] = wu_ref[...].astype(BF16)
    wd_s[...] = wd_ref[...].astype(BF16)

    def rows_of(g):
        return pl.ds(pl.multiple_of(g * tm, tm), tm)

    def fetch(g):
        slot = g % nslot
        return pltpu.make_async_copy(x_hbm.at[rows_of(g)], x_buf.at[slot], in_sem.at[slot])

    def flush(g):
        slot = g % nslot
        return pltpu.make_async_copy(o_buf.at[slot], o_hbm.at[rows_of(g)], out_sem.at[slot])

    @pl.when(e == 0)
    def _():
        for k in range(nslot - 1):
            @pl.when(k < total)
            def _(k=k):
                fetch(k).start()

    def tile(g, carry):
        @pl.when(g + nslot - 1 < total)
        def _():
            fetch(g + nslot - 1).start()

        fetch(g).wait()

        @pl.when(g >= nslot)
        def _():
            flush(g - nslot).wait()

        slot = g % nslot
        o_buf[slot] = _pack_halves(_expert_mlp(x_buf[slot], wg_s, wu_s, wd_s).astype(BF16))
        flush(g).start()
        return carry

    lax.fori_loop(first, first + ntile, tile, 0)

    @pl.when(e == N_EXPERTS - 1)
    def _():
        for k in range(nslot, 0, -1):
            @pl.when(total >= k)
            def _(k=k):
                flush(total - k).wait()


def _expert_tiles(starts, ntiles, xs, w_g, w_u, w_d):
    rows = xs.shape[0]
    tm = EXPERT_ROW_TILE
    weight = lambda shape: pl.BlockSpec((None,) + shape, lambda e, st, nt: (e, 0, 0))
    grid_spec = pltpu.PrefetchScalarGridSpec(
        num_scalar_prefetch=2,
        grid=(N_EXPERTS,),
        in_specs=[pl.BlockSpec(memory_space=pl.ANY),
                  weight((D_MODEL, D_EXPERT)), weight((D_MODEL, D_EXPERT)), weight((D_EXPERT, D_MODEL))],
        out_specs=pl.BlockSpec(memory_space=pl.ANY),
        scratch_shapes=[pltpu.VMEM((D_MODEL, D_EXPERT), BF16), pltpu.VMEM((D_MODEL, D_EXPERT), BF16),
                        pltpu.VMEM((D_EXPERT, D_MODEL), BF16),
                        pltpu.VMEM((EXPERT_SLOTS, tm, HALF), jnp.int32), pltpu.VMEM((EXPERT_SLOTS, tm, HALF), jnp.int32),
                        pltpu.SemaphoreType.DMA((EXPERT_SLOTS,)), pltpu.SemaphoreType.DMA((EXPERT_SLOTS,))],
    )
    return pl.pallas_call(
        _expert_tiles_kernel,
        grid_spec=grid_spec,
        out_shape=jax.ShapeDtypeStruct((rows, HALF), jnp.int32),
        compiler_params=_cparams("arbitrary"),
        name="expert_tiles",
    )(starts, ntiles, xs, w_g, w_u, w_d)


def _unpack_f32(p):
    return pltpu.bitcast(p & jnp.int32(-65536), F32), pltpu.bitcast(lax.shift_left(p, jnp.int32(16)), F32)


def _combine_kernel(h_ref, o1_ref, o2_ref, rt_ref, gf_ref, y_ref):
    w1, w2 = rt_ref[:, 2:3], rt_ref[:, 3:4]
    a_lo, a_hi = _unpack_f32(o1_ref[...])
    b_lo, b_hi = _unpack_f32(o2_ref[...])
    y_lo = h_ref[:, 0:HALF] + (w1 * a_lo + w2 * b_lo)
    y_hi = h_ref[:, HALF:D_MODEL] + (w1 * a_hi + w2 * b_hi)
    ms = (jnp.sum(y_lo * y_lo, axis=-1, keepdims=True) + jnp.sum(y_hi * y_hi, axis=-1, keepdims=True)) / D_MODEL
    inv = lax.rsqrt(ms + EPS)
    y_ref[:, 0:HALF] = (y_lo * inv) * gf_ref[:, 0:HALF]
    y_ref[:, HALF:D_MODEL] = (y_hi * inv) * gf_ref[:, HALF:D_MODEL]


def _combine(h, o12, route_t, gf, tile):
    n = h.shape[0]
    tile = min(tile, n)
    nt = n // tile
    return pl.pallas_call(
        _combine_kernel,
        grid=(nt,),
        in_specs=[pl.BlockSpec((tile, D_MODEL), lambda i: (i, 0)),
                  pl.BlockSpec((tile, HALF), lambda i: (i, 0)),
                  pl.BlockSpec((tile, HALF), lambda i: (i + nt, 0)),
                  pl.BlockSpec((tile, ROUTE_ROWS), lambda i: (i, 0)),
                  pl.BlockSpec((1, D_MODEL), lambda i: (0, 0))],
        out_specs=pl.BlockSpec((tile, D_MODEL), lambda i: (i, 0)),
        out_shape=jax.ShapeDtypeStruct((n, D_MODEL), F32),
        compiler_params=_cparams("parallel"),
        name="combine_final_norm",
    )(h, o12, o12, route_t, gf)


def _sparse_moe(xn2p, route, h, w_g, w_u, w_d, gf, run_before_experts):
    n = h.shape[0]
    tm = EXPERT_ROW_TILE
    rows = 2 * n + N_EXPERTS * tm
    rank, cnt = _route_rank(route, 1024)
    counts = cnt[:, 0]
    padded = (counts + tm - 1) // tm * tm
    e_idx = jnp.arange(N_EXPERTS, dtype=jnp.int32)
    starts = jnp.sum(jnp.where(e_idx[None, :] < e_idx[:, None], padded[None, :], 0), axis=1)
    ids = route[0:2].astype(jnp.int32)
    start_of = jnp.sum(jnp.where(ids[None] == e_idx[:, None, None], starts[:, None, None], 0), axis=0)
    pos = (start_of + rank[0:2]).reshape(2 * n)
    xs = _sc_scatter_pairs(xn2p, pos, rows)
    xs, _ = lax.optimization_barrier((xs, run_before_experts))
    os_ = _expert_tiles(starts.astype(jnp.int32), (padded // tm).astype(jnp.int32), xs, w_g, w_u, w_d)
    o12 = _sc_gather_rows(os_, pos)
    return _combine(h, o12, route.T, gf, 512)


def _prep_in_weights(w_in):
    o = 0
    w_u = w_in[:, o:o + SSM_WIDTH]; o += SSM_WIDTH
    w_q = w_in[:, o:o + SWA_WIDTH]; o += SWA_WIDTH
    w_k = w_in[:, o:o + SWA_KV_WIDTH]; o += SWA_KV_WIDTH
    w_v = w_in[:, o:o + SWA_KV_WIDTH]; o += SWA_KV_WIDTH
    w_qm = w_in[:, o:o + MEM_WIDTH]; o += MEM_WIDTH
    assert o == GATES_COL0
    wq = (w_q * (SWA_HEAD_DIM ** -0.5)).reshape(D_MODEL, SWA_KV_HEADS, SWA_REP, SWA_HEAD_DIM)
    wq = wq.transpose(0, 2, 1, 3).reshape(D_MODEL, SWA_WIDTH)
    w_main = jnp.concatenate([w_u, wq, w_k, w_v, w_qm], axis=1).astype(BF16)
    return w_main, w_in.astype(BF16)


IN_SPLITS = (SSM_WIDTH, SWA_WIDTH, SWA_KV_WIDTH, SWA_KV_WIDTH, MEM_WIDTH)
IN_DTYPES = ((F32, BF16), (BF16,), (F32,), (F32,), (BF16,))


def kernel(x_prompt, x_sample, cache_swa_k, cache_swa_v, state_ssm_re, state_ssm_im, cache_mem_k, cache_mem_v, mem_prompt, norm1_g, w_in, lam_re, lam_im, log_dt, bm_re, bm_im, cm_re, cm_im, d_skip, w_glu, b_glu, sinks, rel_table, mem_norm_g, w_mem_kv, w_br_ssm, w_br_swa, w_br_mem, w_out, norm2_g, w_rg, b_rg, w_rexp, b_rexp, w_e_gate, w_e_up, w_e_down, final_norm_g):
    nb, t, _ = x_prompt.shape
    ns, ts, _ = x_sample.shape
    l = 0
    L = S5_CHUNK

    w_main, w_gates = _prep_in_weights(w_in[l])
    w_swa = (w_br_swa[l].reshape(SWA_KV_HEADS, SWA_REP, SWA_HEAD_DIM, D_MODEL).transpose(1, 0, 2, 3)
             .reshape(SWA_WIDTH, D_MODEL))
    pad_rows = ROUTER_ROWS - N_EXPERTS - N_EXPERT_GROUPS
    w_router = jnp.concatenate([w_rexp[l].T, w_rg[l].T, jnp.zeros((pad_rows, D_MODEL), F32)], axis=0).astype(BF16)
    b_router = jnp.concatenate([b_rexp[l], b_rg[l], jnp.zeros((pad_rows,), F32)]).reshape(ROUTER_ROWS, 1)
    mp = {
        'g1': norm1_g[l].reshape(1, D_MODEL), 'w_gates': w_gates, 'd_skip': d_skip[l].reshape(1, SSM_WIDTH),
        'w_glu': w_glu[l].astype(BF16), 'b_glu': b_glu[l].reshape(1, SSM_WIDTH),
        'w_br_ssm': w_br_ssm[l].astype(BF16), 'w_br_swa': w_swa.astype(BF16),
        'w_br_mem': w_br_mem[l].astype(BF16), 'w_out': w_out[l].astype(BF16),
        'g2': norm2_g[l].reshape(1, D_MODEL), 'w_router': w_router, 'b_router': b_router,
    }
    w_g, w_u, w_d = w_e_gate[l], w_e_up[l], w_e_down[l]
    gf = final_norm_g.reshape(1, D_MODEL)
    s5_w = _s5_weights(lam_re[l], lam_im[l], log_dt[l], bm_re[l], bm_im[l], cm_re[l], cm_im[l], L)

    bias_p = _rel_bias(rel_table, np.arange(WINDOW)[:, None] + WINDOW - np.arange(2 * WINDOW)[None, :])
    keys_s = WINDOW + 2 * ts
    bias_s = _rel_bias(rel_table, np.arange(ts)[:, None] + WINDOW - np.arange(keys_s)[None, :])
    bias_s = bias_s.reshape(SWA_HEADS * ts, keys_s)
    sink_rows = jnp.repeat(sinks[l].astype(F32), ts).reshape(SWA_HEADS * ts, 1)

    n = nb * t
    xp = x_prompt.reshape(n, D_MODEL)
    mk, mv = _norm_proj(mem_prompt.reshape(nb * MEM_TOKENS, D_MODEL), mem_norm_g[l].reshape(1, D_MODEL),
                        w_mem_kv[l].astype(BF16), (MEM_WIDTH, MEM_WIDTH), ((F32,), (F32,)), 512)
    u, ub, qz, k, v, qm = _norm_proj(xp, mp['g1'], w_main, IN_SPLITS, IN_DTYPES, 1024)

    y_ssm, fin = _s5(ub, jnp.zeros((nb, N_CH_TILES * 2 * STATE_TILE), F32), s5_w, nb, t // L, L, 128)
    p_re, p_im = _tiles_to_state(fin)

    o_swa = _swa_prompt(qz, k, v, bias_p, sinks[l].astype(F32), nb, t, 4)
    o_mem = _mem_prompt(qm, mk, mv, nb, t, 1024)
    h, xn2p, route = _merge(xp, u, y_ssm, o_swa, o_mem, mp, 512)

    k4 = k.reshape(nb, t, SWA_KV_HEADS, SWA_HEAD_DIM)
    v4 = v.reshape(nb, t, SWA_KV_HEADS, SWA_HEAD_DIM)
    new_k_p, new_v_p = k4[:, -WINDOW:][None], v4[:, -WINDOW:][None]
    new_mk = mk.reshape(1, nb, MEM_TOKENS, MEM_HEADS, MEM_HEAD_DIM)
    new_mv = mv.reshape(1, nb, MEM_TOKENS, MEM_HEADS, MEM_HEAD_DIM)

    m = ns * ts
    xs = x_sample.reshape(m, D_MODEL)
    us, ubs, qzs, k_s, v_s, qms = _norm_proj(xs, mp['g1'], w_main, IN_SPLITS, IN_DTYPES, 256)
    ys_ssm, fins = _s5(ubs, _state_to_tiles(state_ssm_re[l], state_ssm_im[l]), s5_w, ns, ts // L, L, 64)
    s_re, s_im = _tiles_to_state(fins)

    kk_all = jnp.concatenate([cache_swa_k[l].reshape(ns, WINDOW, SWA_KV_WIDTH).astype(F32),
                              k_s.reshape(ns, ts, SWA_KV_WIDTH)], axis=1)
    vv_all = jnp.concatenate([cache_swa_v[l].reshape(ns, WINDOW, SWA_KV_WIDTH).astype(F32),
                              v_s.reshape(ns, ts, SWA_KV_WIDTH)], axis=1)
    pad = jnp.zeros((ns, keys_s - WINDOW - ts, SWA_KV_WIDTH), F32)
    o_dec, roll_k, roll_v = _swa_decode(qzs.astype(F32).reshape(ns, ts, SWA_WIDTH),
                                        jnp.concatenate([kk_all, pad], axis=1),
                                        jnp.concatenate([vv_all, pad], axis=1), bias_s, sink_rows, 8)
    o_swa_s = o_dec.reshape(m, SWA_WIDTH).astype(BF16)

    o_mem_s = _mem_decode(qms.astype(F32).reshape(ns, ts, MEM_WIDTH), cache_mem_k, cache_mem_v, l, 8)
    o_mem_s = o_mem_s.reshape(m, MEM_WIDTH).astype(BF16)

    y_prompt = _sparse_moe(xn2p, route, h, w_g, w_u, w_d, gf, (ys_ssm, o_swa_s, o_mem_s)).reshape(nb, t, D_MODEL)
    hs_, xn2ps, routes = _merge(xs, us, ys_ssm, o_swa_s, o_mem_s, mp, 256)
    y_sample = _moe(xn2ps, routes.T, w_g, w_u, w_d, hs_, gf, 1024).reshape(ns, ts, D_MODEL)

    new_k_s = roll_k.reshape(1, ns, WINDOW, SWA_KV_HEADS, SWA_HEAD_DIM).astype(cache_swa_k.dtype)
    new_v_s = roll_v.reshape(1, ns, WINDOW, SWA_KV_HEADS, SWA_HEAD_DIM).astype(cache_swa_v.dtype)

    return (y_prompt, y_sample,
            new_k_p, new_v_p, p_re[None], p_im[None], new_mk, new_mv,
            new_k_s, new_v_s, s_re[None].astype(state_ssm_re.dtype), s_im[None].astype(state_ssm_im.dtype))
```

```python
import functools
import math

import numpy as np
import jax
import jax.numpy as jnp
from jax import lax
from jax.experimental import pallas as pl
from jax.experimental.pallas import tpu as pltpu
from jax.experimental.pallas import tpu_sc as plsc

F32 = jnp.float32
BF16 = jnp.bfloat16

D_MODEL = 1024
SSM_WIDTH = 512
SSM_GROUP = 16
SSM_GROUPS = 32
SSM_STATE = 64
SWA_HEADS = 8
SWA_KV_HEADS = 2
SWA_REP = 4
SWA_HEAD_DIM = 64
SWA_WIDTH = 512
SWA_KV_WIDTH = 128
WINDOW = 128
REL_BUCKETS = 32
REL_MAX_DIST = 128
MEM_TOKENS = 256
MEM_HEADS = 4
MEM_HEAD_DIM = 128
MEM_WIDTH = 512
N_EXPERT_GROUPS = 4
EXPERTS_PER_GROUP = 8
N_EXPERTS = 32
D_EXPERT = 256
EPS = 1e-6
NEG_INF = -1e30

LANES = 128
GROUPS_PER_TILE = LANES // SSM_GROUP
N_CH_TILES = SSM_WIDTH // LANES
STATE_TILE = GROUPS_PER_TILE * SSM_STATE
VMEM_LIMIT = 56 * 1024 * 1024
ROWS_NORM_PROJ = 1024
ROWS_MEM_PROJ = 512
ROWS_MEM_ATTN = 1024
ROWS_MERGE = 512
ROWS_ROUTE_RANK = 1024
ROWS_COMBINE = 512
ROWS_DENSE_MOE = 1024
S5_CHUNKS_PER_STEP = 128
SWA_BLOCKS_PER_STEP = 4
DECODE_SEQS_PER_STEP = 8
S5_CHUNK = 8
S5_PANEL = 256

_TRANS_B = (((1,), (1,)), ((), ()))


def _cparams(*sem):
    return pltpu.CompilerParams(dimension_semantics=sem, vmem_limit_bytes=VMEM_LIMIT)


def _rms(x, g):
    return (x * lax.rsqrt(jnp.mean(x * x, axis=-1, keepdims=True) + EPS)) * g


def _dot(a, b):
    return jnp.dot(a, b, preferred_element_type=F32)


def _norm_proj_kernel(x_ref, g_ref, w_ref, *out_refs, splits, dtypes):
    xb = _rms(x_ref[...], g_ref[...]).astype(BF16)
    off = 0
    outs = iter(out_refs)
    for width, dts in zip(splits, dtypes):
        r = _dot(xb, w_ref[:, off:off + width])
        for dt in dts:
            next(outs)[...] = r.astype(dt)
        off += width


def _norm_proj(x, g, w, splits, dtypes, tile):
    n, d = x.shape
    tile = min(tile, n)
    flat = [(wd, dt) for wd, dts in zip(splits, dtypes) for dt in dts]
    return pl.pallas_call(
        functools.partial(_norm_proj_kernel, splits=tuple(splits), dtypes=tuple(dtypes)),
        grid=(n // tile,),
        in_specs=[pl.BlockSpec((tile, d), lambda i: (i, 0)),
                  pl.BlockSpec((1, d), lambda i: (0, 0)),
                  pl.BlockSpec((d, sum(splits)), lambda i: (0, 0), pipeline_mode=pl.Buffered(1))],
        out_specs=[pl.BlockSpec((tile, wd), lambda i: (i, 0)) for wd, _ in flat],
        out_shape=[jax.ShapeDtypeStruct((n, wd), dt) for wd, dt in flat],
        compiler_params=_cparams("parallel"),
        name="norm_proj",
    )(x, g, w)


def _s5_weights(lam_re, lam_im, log_dt, bm_re, bm_im, cm_re, cm_im, L):
    nt, gt, P, H = N_CH_TILES, GROUPS_PER_TILE, SSM_STATE, SSM_GROUP
    lr, li = lam_re.astype(F32), lam_im.astype(F32)
    dt = jnp.exp(log_dt.astype(F32))[:, None]
    mag = jnp.exp(lr * dt)
    a_re = mag * jnp.cos(li * dt)
    a_im = mag * jnp.sin(li * dt)
    den = lr * lr + li * li
    f_re = ((a_re - 1.0) * lr + a_im * li) / den
    f_im = (a_im * lr - (a_re - 1.0) * li) / den
    br, bi = bm_re.astype(F32), bm_im.astype(F32)
    bb_re = f_re[..., None] * br - f_im[..., None] * bi
    bb_im = f_re[..., None] * bi + f_im[..., None] * br
    pr, pi = [jnp.ones_like(a_re)], [jnp.zeros_like(a_im)]
    for _ in range(L):
        pr.append(pr[-1] * a_re - pi[-1] * a_im)
        pi.append(pr[-2] * a_im + pi[-1] * a_re)
    ap_re, ap_im = jnp.stack(pr), jnp.stack(pi)
    cr, ci = cm_re.astype(F32), cm_im.astype(F32)
    ca_re = cr[None] * ap_re[:, :, None, :] - ci[None] * ap_im[:, :, None, :]
    ca_im = cr[None] * ap_im[:, :, None, :] + ci[None] * ap_re[:, :, None, :]

    rev_re = jnp.stack([pr[L - 1 - s] for s in range(L)])
    rev_im = jnp.stack([pi[L - 1 - s] for s in range(L)])
    ws_re = rev_re[..., None] * bb_re[None] - rev_im[..., None] * bb_im[None]
    ws_im = rev_re[..., None] * bb_im[None] + rev_im[..., None] * bb_re[None]
    c_st = jnp.concatenate([ws_re.transpose(0, 1, 3, 2).reshape(L, nt, gt * H, P),
                            ws_im.transpose(0, 1, 3, 2).reshape(L, nt, gt * H, P)], axis=3).transpose(1, 0, 2, 3)
    so = lambda ca: ca[1:].transpose(1, 3, 0, 2).reshape(nt, gt * P, L * H)
    c_so = jnp.concatenate([so(ca_re), so(-ca_im)], axis=1)
    prod = (ca_re[:L][:, :, None, :, :] * bb_re.transpose(0, 2, 1)[None, :, :, None, :]
            - ca_im[:L][:, :, None, :, :] * bb_im.transpose(0, 2, 1)[None, :, :, None, :])
    k_lag = jnp.sum(prod, axis=-1).transpose(1, 2, 0, 3)
    c_k = k_lag.reshape(nt, gt * H, L * H)
    w_st, w_out, toep = _s5_expand(c_st, c_so, c_k, L)

    def per_tile(v):
        return v.reshape(nt, 1, STATE_TILE)

    return w_st, w_out, toep, per_tile(pr[L]), per_tile(pi[L])


def _s5_expand_kernel(cst_ref, cso_ref, ck_ref, wst_ref, wso_ref, toep_ref, *, L):
    hp = lax.Precision.HIGHEST
    P, H = SSM_STATE, SSM_GROUP
    iota = lambda shape, d: lax.broadcasted_iota(jnp.int32, shape, d)
    one = lambda cond: jnp.where(cond, 1.0, 0.0).astype(F32)

    r, c = iota((2 * P, 2 * STATE_TILE), 0), iota((2 * P, 2 * STATE_TILE), 1)
    rep_st = one((r // P == c // STATE_TILE) & (r % P == c % P))
    r, c = iota((LANES, 2 * STATE_TILE), 0), iota((LANES, 2 * STATE_TILE), 1)
    own_st = one(r // H == (c % STATE_TILE) // P)
    for s in range(L):
        blk = jnp.dot(cst_ref[s], rep_st, precision=hp, preferred_element_type=F32) * own_st
        wst_ref[s * LANES:(s + 1) * LANES, :] = blk.astype(BF16)

    r, c = iota((LANES, LANES), 0), iota((LANES, LANES), 1)
    pick = [one((r // H == t) & (r % H == c % H)) for t in range(L)]
    own_k = one(r // H == c // H)
    r, c = iota((2 * STATE_TILE, LANES), 0), iota((2 * STATE_TILE, LANES), 1)
    own_so = one((r % STATE_TILE) // P == c // H)
    cso = cso_ref[...]
    for t in range(L):
        blk = jnp.dot(cso, pick[t], precision=hp, preferred_element_type=F32) * own_so
        wso_ref[:, t * LANES:(t + 1) * LANES] = blk.astype(BF16)
    ck = ck_ref[...]
    lag = [(jnp.dot(ck, pick[t], precision=hp, preferred_element_type=F32) * own_k).astype(BF16) for t in range(L)]
    zero = jnp.zeros((LANES, LANES), BF16)
    for s in range(L):
        for t in range(L):
            toep_ref[s * LANES:(s + 1) * LANES, t * LANES:(t + 1) * LANES] = lag[t - s] if t >= s else zero


def _s5_expand(c_st, c_so, c_k, L):
    lk = L * LANES
    st2 = 2 * STATE_TILE
    return pl.pallas_call(
        functools.partial(_s5_expand_kernel, L=L),
        grid=(N_CH_TILES,),
        in_specs=[pl.BlockSpec((None, L, LANES, 2 * SSM_STATE), lambda j: (j, 0, 0, 0)),
                  pl.BlockSpec((None, st2, L * SSM_GROUP), lambda j: (j, 0, 0)),
                  pl.BlockSpec((None, LANES, L * SSM_GROUP), lambda j: (j, 0, 0))],
        out_specs=[pl.BlockSpec((None, lk, st2), lambda j: (j, 0, 0)),
                   pl.BlockSpec((None, st2, lk), lambda j: (j, 0, 0)),
                   pl.BlockSpec((None, lk, lk), lambda j: (j, 0, 0))],
        out_shape=[jax.ShapeDtypeStruct((N_CH_TILES, lk, st2), BF16),
                   jax.ShapeDtypeStruct((N_CH_TILES, st2, lk), BF16),
                   jax.ShapeDtypeStruct((N_CH_TILES, lk, lk), BF16)],
        compiler_params=_cparams("parallel"),
        name="s5_expand_weights",
    )(c_st, c_so, c_k)


def _to_chunks(u, nb, nc, L):
    return (u.reshape(nb, nc, L, N_CH_TILES, LANES).transpose(1, 0, 3, 2, 4)
            .reshape(nc * nb, N_CH_TILES * L * LANES))


def _from_chunks(y, nb, nc, L):
    return (y.reshape(nc, nb, N_CH_TILES, L, LANES).transpose(1, 0, 3, 2, 4)
            .reshape(nb * nc * L, SSM_WIDTH))


def _s5_kernel(x_ref, h0_ref, are_ref, aim_ref, ws_ref, t_ref, wo_ref, y_ref, fin_ref,
               hr_ref, hi_ref, d_ref, hs_ref, *, cb, nb):
    ci = pl.program_id(1)

    @pl.when(ci == 0)
    def _():
        hr_ref[...] = h0_ref[:, 0:STATE_TILE]
        hi_ref[...] = h0_ref[:, STATE_TILE:2 * STATE_TILE]

    x = x_ref[...]
    d_ref[...] = _dot(x, ws_ref[...])
    ar = jnp.broadcast_to(are_ref[...], (nb, STATE_TILE))
    ai = jnp.broadcast_to(aim_ref[...], (nb, STATE_TILE))

    def body(c, carry):
        hr, hi = carry
        r0 = pl.multiple_of(c * nb, nb)
        hs_ref[pl.ds(r0, nb), 0:STATE_TILE] = hr
        hs_ref[pl.ds(r0, nb), STATE_TILE:2 * STATE_TILE] = hi
        d = d_ref[pl.ds(r0, nb), :]
        return (ar * hr - ai * hi + d[:, 0:STATE_TILE],
                ar * hi + ai * hr + d[:, STATE_TILE:2 * STATE_TILE])

    hr, hi = lax.fori_loop(0, cb, body, (hr_ref[...], hi_ref[...]))
    hr_ref[...] = hr
    hi_ref[...] = hi
    hsb = hs_ref[...].astype(BF16)
    for c0 in range(0, t_ref.shape[1], S5_PANEL):
        c1 = c0 + S5_PANEL
        y_ref[:, c0:c1] = _dot(x[:, 0:c1], t_ref[0:c1, c0:c1]) + _dot(hsb, wo_ref[:, c0:c1])

    @pl.when(ci == pl.num_programs(1) - 1)
    def _():
        fin_ref[:, 0:STATE_TILE] = hr
        fin_ref[:, STATE_TILE:2 * STATE_TILE] = hi


def _s5(ub, h0, weights, nb, nc, L, chunk_block):
    w_st, w_so, toep, a_re, a_im = weights
    xc = _to_chunks(ub, nb, nc, L)
    cb = min(chunk_block, nc)
    rows = cb * nb
    lk = L * LANES
    st2 = 2 * STATE_TILE
    tile_w = lambda shape: pl.BlockSpec((None,) + shape, lambda j, c: (j, 0, 0))
    y, fin = pl.pallas_call(
        functools.partial(_s5_kernel, cb=cb, nb=nb),
        grid=(N_CH_TILES, nc // cb),
        in_specs=[pl.BlockSpec((rows, lk), lambda j, c: (c, j)),
                  pl.BlockSpec((nb, st2), lambda j, c: (0, j)),
                  tile_w((1, STATE_TILE)), tile_w((1, STATE_TILE)),
                  tile_w((lk, st2)), tile_w((lk, lk)), tile_w((st2, lk))],
        out_specs=[pl.BlockSpec((rows, lk), lambda j, c: (c, j)),
                   pl.BlockSpec((nb, st2), lambda j, c: (0, j))],
        out_shape=[jax.ShapeDtypeStruct((nc * nb, N_CH_TILES * lk), F32),
                   jax.ShapeDtypeStruct((nb, N_CH_TILES * st2), F32)],
        scratch_shapes=[pltpu.VMEM((nb, STATE_TILE), F32), pltpu.VMEM((nb, STATE_TILE), F32),
                        pltpu.VMEM((rows, st2), F32), pltpu.VMEM((rows, st2), F32)],
        compiler_params=_cparams("parallel", "arbitrary"),
        name="s5_chunked_scan",
    )(xc, h0, a_re, a_im, w_st, toep, w_so)
    return _from_chunks(y, nb, nc, L), fin


def _state_to_tiles(h_re, h_im):
    nb = h_re.shape[0]
    r = h_re.astype(F32).reshape(nb, N_CH_TILES, STATE_TILE)
    i = h_im.astype(F32).reshape(nb, N_CH_TILES, STATE_TILE)
    return jnp.concatenate([r, i], axis=-1).reshape(nb, N_CH_TILES * 2 * STATE_TILE)


def _tiles_to_state(h):
    nb = h.shape[0]
    h = h.reshape(nb, N_CH_TILES, 2, GROUPS_PER_TILE, SSM_STATE)
    return (h[:, :, 0].reshape(nb, SSM_GROUPS, SSM_STATE), h[:, :, 1].reshape(nb, SSM_GROUPS, SSM_STATE))


def _t5_bucket(dist):
    n = np.maximum(dist, 0)
    max_exact = REL_BUCKETS // 2
    nf = np.maximum(n, 1).astype(np.float32)
    large = max_exact + (np.log(nf / np.float32(max_exact)) / np.float32(math.log(REL_MAX_DIST / max_exact))
                         * np.float32(REL_BUCKETS - max_exact)).astype(np.int32)
    large = np.minimum(large, REL_BUCKETS - 1)
    return np.where(n < max_exact, n, large)


def _rel_bias(rel_table, dist):
    bucket = _t5_bucket(dist)
    tab = rel_table.astype(F32)
    out = jnp.zeros((SWA_HEADS,) + dist.shape, F32)
    for b in range(REL_BUCKETS):
        sel = jnp.asarray(bucket == b)
        if bool((bucket == b).any()):
            out = jnp.where(sel[None], tab[b].reshape((SWA_HEADS,) + (1,) * dist.ndim), out)
    return out


def _swa_prompt_kernel(sink_ref, q_ref, kp_ref, kc_ref, vp_ref, vc_ref, bias_ref, o_ref, kk_ref, vv_ref, *, qblocks):
    step = pl.program_id(1)
    kk_ref[0:WINDOW, :] = kp_ref[...].astype(BF16)
    kk_ref[WINDOW:, :] = kc_ref[...].astype(BF16)
    vv_ref[0:WINDOW, :] = vp_ref[...].astype(BF16)
    vv_ref[WINDOW:, :] = vc_ref[...].astype(BF16)
    row = lax.broadcasted_iota(jnp.int32, (WINDOW, 2 * WINDOW), 0)
    col = lax.broadcasted_iota(jnp.int32, (WINDOW, 2 * WINDOW), 1)
    dist = row + WINDOW - col
    band = (dist >= 0) & (dist < WINDOW)
    lane = lax.broadcasted_iota(jnp.int32, (WINDOW, LANES), 1)
    low = lane < SWA_HEAD_DIM

    def block(j, carry):
        r0 = pl.multiple_of(j * WINDOW, WINDOW)
        kk = kk_ref[pl.ds(r0, 2 * WINDOW), :]
        vv = vv_ref[pl.ds(r0, 2 * WINDOW), :]
        valid = band & ((col >= WINDOW) | (step * qblocks + j > 0))
        for t in range(SWA_REP):
            q2 = q_ref[pl.ds(r0, WINDOW), t * LANES:(t + 1) * LANES]
            outs = []
            for half in range(SWA_KV_HEADS):
                h = t + SWA_REP * half
                qh = jnp.where(low if half == 0 else jnp.logical_not(low), q2, jnp.zeros_like(q2))
                s = lax.dot_general(qh, kk, _TRANS_B, preferred_element_type=F32)
                s = jnp.where(valid, s + bias_ref[h], NEG_INF)
                sink = sink_ref[h]
                m = jnp.maximum(jnp.max(s, axis=-1, keepdims=True), sink)
                e = jnp.exp(s - m)
                den = jnp.sum(e, axis=-1, keepdims=True) + jnp.exp(sink - m)
                outs.append(_dot(e.astype(BF16), vv) * (1.0 / den))
            o_ref[pl.ds(r0, WINDOW), t * LANES:(t + 1) * LANES] = jnp.where(low, outs[0], outs[1]).astype(BF16)
        return carry

    lax.fori_loop(0, qblocks, block, 0)


def _swa_prompt(q, k, v, bias, sinks, nb, t, qblocks):
    nstep = t // (WINDOW * qblocks)
    rows = WINDOW * qblocks
    cur = lambda b, i: (b * nstep + i, 0)
    prev = lambda b, i: (b * nstep * qblocks + jnp.maximum(i * qblocks - 1, 0), 0)
    return pl.pallas_call(
        functools.partial(_swa_prompt_kernel, qblocks=qblocks),
        grid=(nb, nstep),
        in_specs=[pl.BlockSpec(memory_space=pltpu.SMEM),
                  pl.BlockSpec((rows, SWA_WIDTH), cur),
                  pl.BlockSpec((WINDOW, SWA_KV_WIDTH), prev),
                  pl.BlockSpec((rows, SWA_KV_WIDTH), cur),
                  pl.BlockSpec((WINDOW, SWA_KV_WIDTH), prev),
                  pl.BlockSpec((rows, SWA_KV_WIDTH), cur),
                  pl.BlockSpec((SWA_HEADS, WINDOW, 2 * WINDOW), lambda b, i: (0, 0, 0))],
        out_specs=pl.BlockSpec((rows, SWA_WIDTH), cur),
        out_shape=jax.ShapeDtypeStruct((nb * t, SWA_WIDTH), BF16),
        scratch_shapes=[pltpu.VMEM((rows + WINDOW, SWA_KV_WIDTH), BF16),
                        pltpu.VMEM((rows + WINDOW, SWA_KV_WIDTH), BF16)],
        compiler_params=_cparams("parallel", "parallel"),
        name="swa_prompt",
    )(sinks, q, k, k, v, v, bias)


def _swa_decode_kernel(q_ref, k_ref, v_ref, bias_ref, sink_ref, o_ref, nk_ref, nv_ref, *, seqs, tq):
    rows, keys = SWA_HEADS * tq, k_ref.shape[1]
    nk_ref[...] = k_ref[:, tq:tq + WINDOW, :]
    nv_ref[...] = v_ref[:, tq:tq + WINDOW, :]
    low = lax.broadcasted_iota(jnp.int32, (tq, LANES), 1) < SWA_HEAD_DIM
    qi = lax.broadcasted_iota(jnp.int32, (rows, keys), 0) % tq
    col = lax.broadcasted_iota(jnp.int32, (rows, keys), 1)
    dist = qi + WINDOW - col
    valid = (dist >= 0) & (dist < WINDOW)
    bias = bias_ref[...]
    sink = sink_ref[...]
    for s_i in range(seqs):
        q = q_ref[s_i]
        tiles = [q[:, t * LANES:(t + 1) * LANES] for t in range(SWA_REP)]
        qh = jnp.concatenate([jnp.where(low, x, 0.0) for x in tiles]
                             + [jnp.where(low, 0.0, x) for x in tiles], axis=0)
        kk = k_ref[s_i].astype(BF16)
        s = lax.dot_general(qh.astype(BF16), kk, _TRANS_B, preferred_element_type=F32)
        s = jnp.where(valid, s + bias, NEG_INF)
        m = jnp.maximum(jnp.max(s, axis=-1, keepdims=True), sink)
        e = jnp.exp(s - m)
        den = jnp.sum(e, axis=-1, keepdims=True) + jnp.exp(sink - m)
        o = _dot(e.astype(BF16), v_ref[s_i].astype(BF16)) * (1.0 / den)
        for t in range(SWA_REP):
            o_ref[s_i, :, t * LANES:(t + 1) * LANES] = jnp.where(
                low, o[t * tq:(t + 1) * tq], o[(t + SWA_REP) * tq:(t + SWA_REP + 1) * tq])


def _swa_decode(q, k_all, v_all, bias, sink_rows, seqs):
    nseq, tq, _ = q.shape
    rows = SWA_HEADS * tq
    keys = k_all.shape[1]
    seqs = min(seqs, nseq)
    return pl.pallas_call(
        functools.partial(_swa_decode_kernel, seqs=seqs, tq=tq),
        grid=(nseq // seqs,),
        in_specs=[pl.BlockSpec((seqs, tq, SWA_WIDTH), lambda i: (i, 0, 0)),
                  pl.BlockSpec((seqs, keys, LANES), lambda i: (i, 0, 0)),
                  pl.BlockSpec((seqs, keys, LANES), lambda i: (i, 0, 0)),
                  pl.BlockSpec((rows, keys), lambda i: (0, 0)),
                  pl.BlockSpec((rows, 1), lambda i: (0, 0))],
        out_specs=[pl.BlockSpec((seqs, tq, SWA_WIDTH), lambda i: (i, 0, 0)),
                   pl.BlockSpec((seqs, WINDOW, LANES), lambda i: (i, 0, 0)),
                   pl.BlockSpec((seqs, WINDOW, LANES), lambda i: (i, 0, 0))],
        out_shape=[jax.ShapeDtypeStruct((nseq, tq, SWA_WIDTH), F32),
                   jax.ShapeDtypeStruct((nseq, WINDOW, LANES), F32),
                   jax.ShapeDtypeStruct((nseq, WINDOW, LANES), F32)],
        compiler_params=_cparams("parallel"),
        name="swa_decode",
    )(q, k_all, v_all, bias, sink_rows)


def _softmax(s):
    m = jnp.max(s, axis=-1, keepdims=True)
    e = jnp.exp(s - m)
    return e * (1.0 / jnp.sum(e, axis=-1, keepdims=True))


def _mem_prompt_kernel(q_ref, k_ref, v_ref, o_ref, s_ref, p_ref):
    scale = MEM_HEAD_DIM ** -0.5
    heads = [slice(h * MEM_HEAD_DIM, (h + 1) * MEM_HEAD_DIM) for h in range(MEM_HEADS)]
    for h, sl in enumerate(heads):
        s_ref[h] = lax.dot_general(q_ref[:, sl], k_ref[:, sl].astype(BF16), _TRANS_B, preferred_element_type=F32)
    s = s_ref[...] * scale
    e = jnp.exp(s - jnp.max(s, axis=-1, keepdims=True))
    p_ref[...] = e.astype(BF16)
    inv = 1.0 / jnp.sum(e, axis=-1, keepdims=True)
    for h, sl in enumerate(heads):
        o_ref[:, sl] = (_dot(p_ref[h], v_ref[:, sl].astype(BF16)) * inv[h]).astype(BF16)


def _mem_prompt(qm, mk, mv, nb, t, tile):
    tile = min(tile, t)
    nt = t // tile
    return pl.pallas_call(
        _mem_prompt_kernel,
        grid=(nb, nt),
        in_specs=[pl.BlockSpec((tile, MEM_WIDTH), lambda b, i: (b * nt + i, 0)),
                  pl.BlockSpec((MEM_TOKENS, MEM_WIDTH), lambda b, i: (b, 0)),
                  pl.BlockSpec((MEM_TOKENS, MEM_WIDTH), lambda b, i: (b, 0))],
        out_specs=pl.BlockSpec((tile, MEM_WIDTH), lambda b, i: (b * nt + i, 0)),
        out_shape=jax.ShapeDtypeStruct((nb * t, MEM_WIDTH), BF16),
        scratch_shapes=[pltpu.VMEM((MEM_HEADS, tile, MEM_TOKENS), F32), pltpu.VMEM((MEM_HEADS, tile, MEM_TOKENS), BF16)],
        compiler_params=_cparams("parallel", "parallel"),
        name="mem_prompt",
    )(qm, mk, mv)


def _mem_decode_kernel(q_ref, k_ref, v_ref, o_ref, *, seqs):
    tq = q_ref.shape[1]
    rows, cols = MEM_HEADS * tq, MEM_TOKENS * MEM_HEADS
    k2 = k_ref.reshape(seqs, cols, MEM_HEAD_DIM)
    v2 = v_ref.reshape(seqs, cols, MEM_HEAD_DIM)
    scale = MEM_HEAD_DIM ** -0.5
    own = (lax.broadcasted_iota(jnp.int32, (rows, cols), 1) % MEM_HEADS
           == lax.broadcasted_iota(jnp.int32, (rows, cols), 0) // tq)
    for s_i in range(seqs):
        q = q_ref[s_i]
        qb = jnp.concatenate([q[:, h * MEM_HEAD_DIM:(h + 1) * MEM_HEAD_DIM] for h in range(MEM_HEADS)], axis=0)
        s = lax.dot_general(qb.astype(BF16), k2[s_i].astype(BF16), _TRANS_B, preferred_element_type=F32) * scale
        p = _softmax(jnp.where(own, s, NEG_INF)).astype(BF16)
        o = _dot(p, v2[s_i].astype(BF16))
        for h in range(MEM_HEADS):
            o_ref[s_i, :, h * MEM_HEAD_DIM:(h + 1) * MEM_HEAD_DIM] = o[h * tq:(h + 1) * tq, :]


def _mem_decode(q, k, v, layer, seqs):
    nseq, tq, _ = q.shape
    seqs = min(seqs, nseq)
    cache = pl.BlockSpec((None, seqs, MEM_TOKENS, MEM_HEADS, MEM_HEAD_DIM), lambda i: (layer, i, 0, 0, 0))
    return pl.pallas_call(
        functools.partial(_mem_decode_kernel, seqs=seqs),
        grid=(nseq // seqs,),
        in_specs=[pl.BlockSpec((seqs, tq, MEM_WIDTH), lambda i: (i, 0, 0)), cache, cache],
        out_specs=pl.BlockSpec((seqs, tq, MEM_WIDTH), lambda i: (i, 0, 0)),
        out_shape=jax.ShapeDtypeStruct((nseq, tq, MEM_WIDTH), F32),
        compiler_params=_cparams("parallel"),
        name="mem_decode",
    )(q, k, v)


ROUTER_ROWS = 40
GATES_COL0 = SSM_WIDTH + SWA_WIDTH + 2 * SWA_KV_WIDTH + MEM_WIDTH
ROUTE_ROWS = 8
HALF = D_MODEL // 2


def _pack_halves(xb):
    hi = pltpu.bitcast(xb[:, 0:HALF].astype(F32), jnp.int32)
    lo = pltpu.bitcast(xb[:, HALF:D_MODEL].astype(F32), jnp.int32)
    return hi | lax.shift_right_logical(lo, jnp.int32(16))


def _unpack_halves(p):
    hi = pltpu.bitcast(p & jnp.int32(-65536), F32).astype(BF16)
    lo = pltpu.bitcast(lax.shift_left(p, jnp.int32(16)), F32).astype(BF16)
    return hi, lo


def _merge_kernel(x_ref, u_ref, y_ref, os_ref, om_ref, g1_ref, wg_ref, dsk_ref, wglu_ref, bglu_ref,
                  wbs_ref, wbw_ref, wbm_ref, wout_ref, g2_ref, wr_ref, br_ref,
                  h_ref, xn2_ref, route_ref):
    x = x_ref[...]
    tt = x.shape[0]
    xb = _rms(x, g1_ref[...]).astype(BF16)
    z = jax.nn.gelu(y_ref[...] + dsk_ref[...] * u_ref[...])
    z = z * jax.nn.sigmoid(_dot(z.astype(BF16), wglu_ref[...]) + bglu_ref[...])
    gate = lambda b: jax.nn.sigmoid(_dot(xb, wg_ref[:, GATES_COL0 + b * D_MODEL:GATES_COL0 + (b + 1) * D_MODEL]))
    merged = gate(0) * _dot(z.astype(BF16), wbs_ref[...])
    merged = merged + gate(1) * _dot(os_ref[...], wbw_ref[...])
    merged = merged + gate(2) * _dot(om_ref[...], wbm_ref[...])
    h = x + _dot(merged.astype(BF16), wout_ref[...])
    h_ref[...] = h
    xn2 = _rms(h, g2_ref[...]).astype(BF16)
    xn2_ref[...] = _pack_halves(xn2)

    lt = lax.dot_general(wr_ref[...], xn2, _TRANS_B, preferred_element_type=F32) + br_ref[...]
    gl = lt[N_EXPERTS:N_EXPERTS + N_EXPERT_GROUPS]
    ge = jnp.exp(gl - jnp.max(gl, axis=0, keepdims=True))
    gp = ge / jnp.sum(ge, axis=0, keepdims=True)
    gw = jnp.max(gp, axis=0, keepdims=True)
    gidx = jnp.full((1, tt), N_EXPERT_GROUPS - 1, jnp.int32)
    for r in range(N_EXPERT_GROUPS - 2, -1, -1):
        gidx = jnp.where(gp[r:r + 1] == gw, r, gidx)
    ein = lt[(N_EXPERT_GROUPS - 1) * EXPERTS_PER_GROUP:N_EXPERTS]
    for r in range(N_EXPERT_GROUPS - 2, -1, -1):
        ein = jnp.where(gidx == r, lt[r * EXPERTS_PER_GROUP:(r + 1) * EXPERTS_PER_GROUP], ein)
    ee = jnp.exp(ein - jnp.max(ein, axis=0, keepdims=True))
    ep = ee / jnp.sum(ee, axis=0, keepdims=True)
    rowi = lax.broadcasted_iota(jnp.int32, (EXPERTS_PER_GROUP, tt), 0)
    p1 = jnp.max(ep, axis=0, keepdims=True)
    e1 = jnp.min(jnp.where(ep == p1, rowi, EXPERTS_PER_GROUP), axis=0, keepdims=True)
    ep2 = jnp.where(rowi == e1, -1.0, ep)
    p2 = jnp.max(ep2, axis=0, keepdims=True)
    e2 = jnp.min(jnp.where(ep2 == p2, rowi, EXPERTS_PER_GROUP), axis=0, keepdims=True)
    tot = p1 + p2
    w1 = p1 / tot * gw
    w2 = p2 / tot * gw
    id1 = (gidx * EXPERTS_PER_GROUP + e1).astype(F32)
    id2 = (gidx * EXPERTS_PER_GROUP + e2).astype(F32)
    route_ref[...] = jnp.concatenate([id1, id2, w1, w2, jnp.zeros((ROUTE_ROWS - 4, tt), F32)], axis=0)


def _merge(x, u, y, o_swa, o_mem, p, tile):
    n = x.shape[0]
    tile = min(tile, n)
    row = lambda i: (i, 0)
    const = lambda i: (0, 0)
    full = lambda a: pl.BlockSpec(a.shape, const, pipeline_mode=pl.Buffered(1))
    weights = [p['g1'], p['w_gates'], p['d_skip'], p['w_glu'], p['b_glu'], p['w_br_ssm'], p['w_br_swa'],
               p['w_br_mem'], p['w_out'], p['g2'], p['w_router'], p['b_router']]
    return pl.pallas_call(
        _merge_kernel,
        grid=(n // tile,),
        in_specs=[pl.BlockSpec((tile, D_MODEL), row), pl.BlockSpec((tile, SSM_WIDTH), row),
                  pl.BlockSpec((tile, SSM_WIDTH), row), pl.BlockSpec((tile, SWA_WIDTH), row),
                  pl.BlockSpec((tile, MEM_WIDTH), row)] + [full(w) for w in weights],
        out_specs=[pl.BlockSpec((tile, D_MODEL), row), pl.BlockSpec((tile, HALF), row),
                   pl.BlockSpec((ROUTE_ROWS, tile), lambda i: (0, i))],
        out_shape=[jax.ShapeDtypeStruct((n, D_MODEL), F32), jax.ShapeDtypeStruct((n, HALF), jnp.int32),
                   jax.ShapeDtypeStruct((ROUTE_ROWS, n), F32)],
        compiler_params=_cparams("parallel"),
        name="merge_router",
    )(x, u, y, o_swa, o_mem, *weights)


def _expert_mlp(xp, wg, wu, wd):
    hi, lo = _unpack_halves(xp)
    g = _dot(hi, wg[0:HALF, :]) + _dot(lo, wg[HALF:D_MODEL, :])
    u = _dot(hi, wu[0:HALF, :]) + _dot(lo, wu[HALF:D_MODEL, :])
    hh = jax.nn.silu(g) * u
    return _dot(hh.astype(BF16), wd[...])


def _moe_kernel(xn2_ref, rt_ref, wg_ref, wu_ref, wd_ref, h_ref, gf_ref, o_ref, acc_ref):
    e = pl.program_id(1)

    @pl.when(e == 0)
    def _():
        acc_ref[...] = jnp.zeros_like(acc_ref)

    o = _expert_mlp(xn2_ref[...], wg_ref[...].astype(BF16), wu_ref[...].astype(BF16), wd_ref[...].astype(BF16))
    ef = e.astype(F32)
    c = (jnp.where(rt_ref[:, 0:1] == ef, rt_ref[:, 2:3], 0.0)
         + jnp.where(rt_ref[:, 1:2] == ef, rt_ref[:, 3:4], 0.0))
    acc_ref[...] += c * o

    @pl.when(e == N_EXPERTS - 1)
    def _():
        o_ref[...] = _rms(h_ref[...] + acc_ref[...], gf_ref[...])


def _moe(xn2, route_t, w_g, w_u, w_d, h, gf, tile):
    n = h.shape[0]
    tile = min(tile, n)
    return pl.pallas_call(
        _moe_kernel,
        grid=(n // tile, N_EXPERTS),
        in_specs=[pl.BlockSpec((tile, HALF), lambda i, e: (i, 0)),
                  pl.BlockSpec((tile, ROUTE_ROWS), lambda i, e: (i, 0)),
                  pl.BlockSpec((None, D_MODEL, D_EXPERT), lambda i, e: (e, 0, 0)),
                  pl.BlockSpec((None, D_MODEL, D_EXPERT), lambda i, e: (e, 0, 0)),
                  pl.BlockSpec((None, D_EXPERT, D_MODEL), lambda i, e: (e, 0, 0)),
                  pl.BlockSpec((tile, D_MODEL), lambda i, e: (i, 0)),
                  pl.BlockSpec((1, D_MODEL), lambda i, e: (0, 0))],
        out_specs=pl.BlockSpec((tile, D_MODEL), lambda i, e: (i, 0)),
        out_shape=jax.ShapeDtypeStruct((n, D_MODEL), F32),
        scratch_shapes=[pltpu.VMEM((tile, D_MODEL), F32)],
        compiler_params=_cparams("parallel", "arbitrary"),
        name="moe_final_norm",
    )(xn2, route_t, w_g, w_u, w_d, h, gf)


EXPERT_ROW_TILE = 256
EXPERT_SLOTS = 6
SC_CORES = 2
SC_SUBCORES = 16
SC_WORKERS = SC_CORES * SC_SUBCORES
SC_SCATTER_ROWS = 64
SC_GATHER_ROWS = 64


def _route_rank_kernel(r_ref, rank_ref, cnt_ref, base_ref):
    i = pl.program_id(0)
    tt = r_ref.shape[1]

    @pl.when(i == 0)
    def _():
        base_ref[...] = jnp.zeros_like(base_ref)

    ids = r_ref[0:2, :].astype(jnp.int32)
    e_iota = lax.broadcasted_iota(jnp.int32, (N_EXPERTS, tt), 0)
    oh1 = jnp.where(e_iota == ids[0:1], 1.0, 0.0)
    oh2 = jnp.where(e_iota == ids[1:2], 1.0, 0.0)
    before = (lax.broadcasted_iota(jnp.int32, (tt, tt), 0) < lax.broadcasted_iota(jnp.int32, (tt, tt), 1))
    tri = jnp.where(before, 1.0, 0.0).astype(BF16)
    c1 = _dot(oh1.astype(BF16), tri)
    c2 = _dot(oh2.astype(BF16), tri)
    tot1 = jnp.sum(oh1, axis=1, keepdims=True)
    tot2 = jnp.sum(oh2, axis=1, keepdims=True)
    base = base_ref[:, 0:1]
    rank1 = jnp.sum(oh1 * (base + c1), axis=0, keepdims=True)
    rank2 = jnp.sum(oh2 * (base + tot1 + c2), axis=0, keepdims=True)
    rank_ref[...] = jnp.concatenate([rank1, rank2, jnp.zeros((ROUTE_ROWS - 2, tt), F32)], axis=0).astype(jnp.int32)
    new_base = jnp.broadcast_to(base + tot1 + tot2, base_ref.shape)
    base_ref[...] = new_base
    cnt_ref[...] = new_base.astype(jnp.int32)


def _route_rank(route, tile):
    n = route.shape[1]
    tile = min(tile, n)
    return pl.pallas_call(
        _route_rank_kernel,
        grid=(n // tile,),
        in_specs=[pl.BlockSpec((ROUTE_ROWS, tile), lambda i: (0, i))],
        out_specs=[pl.BlockSpec((ROUTE_ROWS, tile), lambda i: (0, i)),
                   pl.BlockSpec((N_EXPERTS, LANES), lambda i: (0, 0))],
        out_shape=[jax.ShapeDtypeStruct((ROUTE_ROWS, n), jnp.int32),
                   jax.ShapeDtypeStruct((N_EXPERTS, LANES), jnp.int32)],
        scratch_shapes=[pltpu.VMEM((N_EXPERTS, LANES), F32)],
        compiler_params=_cparams("arbitrary"),
        name="route_rank",
    )(route)


def _sc_mesh():
    return plsc.VectorSubcoreMesh(core_axis_name="core", subcore_axis_name="subcore")


def _sc_scatter_pairs(x, pos, rows_out):
    n, d = x.shape
    per_w = n // SC_WORKERS
    window = min(SC_SCATTER_ROWS, per_w)

    @pl.kernel(out_type=jax.ShapeDtypeStruct((rows_out, d), x.dtype), mesh=_sc_mesh(),
               scratch_types=[pltpu.VMEM((window,), jnp.int32), pltpu.VMEM((window,), jnp.int32),
                              pltpu.VMEM((window, d), x.dtype), pltpu.SemaphoreType.DMA, pltpu.SemaphoreType.DMA,
                              pltpu.SemaphoreType.DMA])
    def scatter(x_hbm, p_hbm, o_hbm, i1_v, i2_v, rows_v, sem_a, sem_b, sem_c):
        wid = lax.axis_index("subcore") * SC_CORES + lax.axis_index("core")

        @pl.loop(0, per_w // window)
        def _(j):
            base = wid * per_w + j * window
            load_i1 = pltpu.async_copy(p_hbm.at[pl.ds(base, window)], i1_v, sem_a)
            load_i2 = pltpu.async_copy(p_hbm.at[pl.ds(n + base, window)], i2_v, sem_b)
            load_x = pltpu.async_copy(x_hbm.at[pl.ds(base, window)], rows_v, sem_c)
            load_i1.wait()
            load_i2.wait()
            load_x.wait()
            put_1 = pltpu.async_copy(rows_v, o_hbm.at[i1_v], sem_a)
            put_2 = pltpu.async_copy(rows_v, o_hbm.at[i2_v], sem_b)
            put_1.wait()
            put_2.wait()

    return scatter(x, pos)


def _sc_gather_rows(table, idx):
    m = idx.shape[0]
    d = table.shape[1]
    per_w = m // SC_WORKERS
    window = min(SC_GATHER_ROWS, per_w)

    assert per_w % (2 * window) == 0

    @pl.kernel(out_type=jax.ShapeDtypeStruct((m, d), table.dtype), mesh=_sc_mesh(),
               scratch_types=[pltpu.VMEM((window,), jnp.int32), pltpu.VMEM((window,), jnp.int32),
                              pltpu.VMEM((window, d), table.dtype), pltpu.VMEM((window, d), table.dtype),
                              pltpu.SemaphoreType.DMA, pltpu.SemaphoreType.DMA])
    def gather(t_hbm, i_hbm, o_hbm, ia_v, ib_v, ra_v, rb_v, sem_a, sem_b):
        wid = lax.axis_index("subcore") * SC_CORES + lax.axis_index("core")

        @pl.loop(0, per_w // (2 * window))
        def _(j):
            base_a = wid * per_w + j * (2 * window)
            base_b = base_a + window
            idx_a = pltpu.async_copy(i_hbm.at[pl.ds(base_a, window)], ia_v, sem_a)
            idx_b = pltpu.async_copy(i_hbm.at[pl.ds(base_b, window)], ib_v, sem_b)
            idx_a.wait()
            get_a = pltpu.async_copy(t_hbm.at[ia_v], ra_v, sem_a)
            idx_b.wait()
            get_b = pltpu.async_copy(t_hbm.at[ib_v], rb_v, sem_b)
            get_a.wait()
            put_a = pltpu.async_copy(ra_v, o_hbm.at[pl.ds(base_a, window)], sem_a)
            get_b.wait()
            put_b = pltpu.async_copy(rb_v, o_hbm.at[pl.ds(base_b, window)], sem_b)
            put_a.wait()
            put_b.wait()

    return gather(table, idx)


def _expert_tiles_kernel(start_ref, ntile_ref, x_hbm, wg_ref, wu_ref, wd_ref, o_hbm,
                         wg_s, wu_s, wd_s, x_buf, o_buf, in_sem, out_sem):
    e = pl.program_id(0)
    tm = x_buf.shape[1]
    nslot = x_buf.shape[0]
    first = start_ref[e] // tm
    ntile = ntile_ref[e]
    total = start_ref[N_EXPERTS - 1] // tm + ntile_ref[N_EXPERTS - 1]
    wg_s[...] = wg_ref[...].astype(BF16)
    wu_s[...] = wu_ref[...].astype(BF16)
    wd_s[...] = wd_ref[...].astype(BF16)

    def rows_of(g):
        return pl.ds(pl.multiple_of(g * tm, tm), tm)

    def fetch(g):
        slot = g % nslot
        return pltpu.make_async_copy(x_hbm.at[rows_of(g)], x_buf.at[slot], in_sem.at[slot])

    def flush(g):
        slot = g % nslot
        return pltpu.make_async_copy(o_buf.at[slot], o_hbm.at[rows_of(g)], out_sem.at[slot])

    @pl.when(e == 0)
    def _():
        for k in range(nslot - 1):
            @pl.when(k < total)
            def _(k=k):
                fetch(k).start()

    def tile(g, carry):
        @pl.when(g + nslot - 1 < total)
        def _():
            fetch(g + nslot - 1).start()

        fetch(g).wait()

        @pl.when(g >= nslot)
        def _():
            flush(g - nslot).wait()

        slot = g % nslot
        o_buf[slot] = _pack_halves(_expert_mlp(x_buf[slot], wg_s, wu_s, wd_s).astype(BF16))
        flush(g).start()
        return carry

    lax.fori_loop(first, first + ntile, tile, 0)

    @pl.when(e == N_EXPERTS - 1)
    def _():
        for k in range(nslot, 0, -1):
            @pl.when(total >= k)
            def _(k=k):
                flush(total - k).wait()


def _expert_tiles(starts, ntiles, xs, w_g, w_u, w_d):
    rows = xs.shape[0]
    tm = EXPERT_ROW_TILE
    weight = lambda shape: pl.BlockSpec((None,) + shape, lambda e, st, nt: (e, 0, 0))
    grid_spec = pltpu.PrefetchScalarGridSpec(
        num_scalar_prefetch=2,
        grid=(N_EXPERTS,),
        in_specs=[pl.BlockSpec(memory_space=pl.ANY),
                  weight((D_MODEL, D_EXPERT)), weight((D_MODEL, D_EXPERT)), weight((D_EXPERT, D_MODEL))],
        out_specs=pl.BlockSpec(memory_space=pl.ANY),
        scratch_shapes=[pltpu.VMEM((D_MODEL, D_EXPERT), BF16), pltpu.VMEM((D_MODEL, D_EXPERT), BF16),
                        pltpu.VMEM((D_EXPERT, D_MODEL), BF16),
                        pltpu.VMEM((EXPERT_SLOTS, tm, HALF), jnp.int32), pltpu.VMEM((EXPERT_SLOTS, tm, HALF), jnp.int32),
                        pltpu.SemaphoreType.DMA((EXPERT_SLOTS,)), pltpu.SemaphoreType.DMA((EXPERT_SLOTS,))],
    )
    return pl.pallas_call(
        _expert_tiles_kernel,
        grid_spec=grid_spec,
        out_shape=jax.ShapeDtypeStruct((rows, HALF), jnp.int32),
        compiler_params=_cparams("arbitrary"),
        name="expert_tiles",
    )(starts, ntiles, xs, w_g, w_u, w_d)


def _unpack_f32(p):
    return pltpu.bitcast(p & jnp.int32(-65536), F32), pltpu.bitcast(lax.shift_left(p, jnp.int32(16)), F32)


def _combine_kernel(h_ref, o1_ref, o2_ref, rt_ref, gf_ref, y_ref):
    w1, w2 = rt_ref[:, 2:3], rt_ref[:, 3:4]
    a_lo, a_hi = _unpack_f32(o1_ref[...])
    b_lo, b_hi = _unpack_f32(o2_ref[...])
    y_lo = h_ref[:, 0:HALF] + (w1 * a_lo + w2 * b_lo)
    y_hi = h_ref[:, HALF:D_MODEL] + (w1 * a_hi + w2 * b_hi)
    ms = (jnp.sum(y_lo * y_lo, axis=-1, keepdims=True) + jnp.sum(y_hi * y_hi, axis=-1, keepdims=True)) / D_MODEL
    inv = lax.rsqrt(ms + EPS)
    y_ref[:, 0:HALF] = (y_lo * inv) * gf_ref[:, 0:HALF]
    y_ref[:, HALF:D_MODEL] = (y_hi * inv) * gf_ref[:, HALF:D_MODEL]


def _combine(h, o12, route_t, gf, tile):
    n = h.shape[0]
    tile = min(tile, n)
    nt = n // tile
    return pl.pallas_call(
        _combine_kernel,
        grid=(nt,),
        in_specs=[pl.BlockSpec((tile, D_MODEL), lambda i: (i, 0)),
                  pl.BlockSpec((tile, HALF), lambda i: (i, 0)),
                  pl.BlockSpec((tile, HALF), lambda i: (i + nt, 0)),
                  pl.BlockSpec((tile, ROUTE_ROWS), lambda i: (i, 0)),
                  pl.BlockSpec((1, D_MODEL), lambda i: (0, 0))],
        out_specs=pl.BlockSpec((tile, D_MODEL), lambda i: (i, 0)),
        out_shape=jax.ShapeDtypeStruct((n, D_MODEL), F32),
        compiler_params=_cparams("parallel"),
        name="combine_final_norm",
    )(h, o12, o12, route_t, gf)


def _sparse_moe(xn2p, route, h, w_g, w_u, w_d, gf, run_before_experts):
    n = h.shape[0]
    tm = EXPERT_ROW_TILE
    rows = 2 * n + N_EXPERTS * tm
    rank, cnt = _route_rank(route, ROWS_ROUTE_RANK)
    counts = cnt[:, 0]
    padded = (counts + tm - 1) // tm * tm
    e_idx = jnp.arange(N_EXPERTS, dtype=jnp.int32)
    starts = jnp.sum(jnp.where(e_idx[None, :] < e_idx[:, None], padded[None, :], 0), axis=1)
    ids = route[0:2].astype(jnp.int32)
    start_of = jnp.sum(jnp.where(ids[None] == e_idx[:, None, None], starts[:, None, None], 0), axis=0)
    pos = (start_of + rank[0:2]).reshape(2 * n)
    xs = _sc_scatter_pairs(xn2p, pos, rows)
    xs, _ = lax.optimization_barrier((xs, run_before_experts))
    os_ = _expert_tiles(starts.astype(jnp.int32), (padded // tm).astype(jnp.int32), xs, w_g, w_u, w_d)
    o12 = _sc_gather_rows(os_, pos)
    return _combine(h, o12, route.T, gf, ROWS_COMBINE)


def _prep_in_weights(w_in):
    o = 0
    w_u = w_in[:, o:o + SSM_WIDTH]; o += SSM_WIDTH
    w_q = w_in[:, o:o + SWA_WIDTH]; o += SWA_WIDTH
    w_k = w_in[:, o:o + SWA_KV_WIDTH]; o += SWA_KV_WIDTH
    w_v = w_in[:, o:o + SWA_KV_WIDTH]; o += SWA_KV_WIDTH
    w_qm = w_in[:, o:o + MEM_WIDTH]; o += MEM_WIDTH
    assert o == GATES_COL0
    wq = (w_q * (SWA_HEAD_DIM ** -0.5)).reshape(D_MODEL, SWA_KV_HEADS, SWA_REP, SWA_HEAD_DIM)
    wq = wq.transpose(0, 2, 1, 3).reshape(D_MODEL, SWA_WIDTH)
    w_main = jnp.concatenate([w_u, wq, w_k, w_v, w_qm], axis=1).astype(BF16)
    return w_main, w_in.astype(BF16)


IN_SPLITS = (SSM_WIDTH, SWA_WIDTH, SWA_KV_WIDTH, SWA_KV_WIDTH, MEM_WIDTH)
IN_DTYPES = ((F32, BF16), (BF16,), (F32,), (F32,), (BF16,))


def kernel(x_prompt, x_sample, cache_swa_k, cache_swa_v, state_ssm_re, state_ssm_im, cache_mem_k, cache_mem_v, mem_prompt, norm1_g, w_in, lam_re, lam_im, log_dt, bm_re, bm_im, cm_re, cm_im, d_skip, w_glu, b_glu, sinks, rel_table, mem_norm_g, w_mem_kv, w_br_ssm, w_br_swa, w_br_mem, w_out, norm2_g, w_rg, b_rg, w_rexp, b_rexp, w_e_gate, w_e_up, w_e_down, final_norm_g):
    nb, t, _ = x_prompt.shape
    ns, ts, _ = x_sample.shape
    l = 0
    L = S5_CHUNK

    w_main, w_gates = _prep_in_weights(w_in[l])
    w_swa = (w_br_swa[l].reshape(SWA_KV_HEADS, SWA_REP, SWA_HEAD_DIM, D_MODEL).transpose(1, 0, 2, 3)
             .reshape(SWA_WIDTH, D_MODEL))
    pad_rows = ROUTER_ROWS - N_EXPERTS - N_EXPERT_GROUPS
    w_router = jnp.concatenate([w_rexp[l].T, w_rg[l].T, jnp.zeros((pad_rows, D_MODEL), F32)], axis=0).astype(BF16)
    b_router = jnp.concatenate([b_rexp[l], b_rg[l], jnp.zeros((pad_rows,), F32)]).reshape(ROUTER_ROWS, 1)
    mp = {
        'g1': norm1_g[l].reshape(1, D_MODEL), 'w_gates': w_gates, 'd_skip': d_skip[l].reshape(1, SSM_WIDTH),
        'w_glu': w_glu[l].astype(BF16), 'b_glu': b_glu[l].reshape(1, SSM_WIDTH),
        'w_br_ssm': w_br_ssm[l].astype(BF16), 'w_br_swa': w_swa.astype(BF16),
        'w_br_mem': w_br_mem[l].astype(BF16), 'w_out': w_out[l].astype(BF16),
        'g2': norm2_g[l].reshape(1, D_MODEL), 'w_router': w_router, 'b_router': b_router,
    }
    w_g, w_u, w_d = w_e_gate[l], w_e_up[l], w_e_down[l]
    gf = final_norm_g.reshape(1, D_MODEL)
    s5_w = _s5_weights(lam_re[l], lam_im[l], log_dt[l], bm_re[l], bm_im[l], cm_re[l], cm_im[l], L)

    bias_p = _rel_bias(rel_table, np.arange(WINDOW)[:, None] + WINDOW - np.arange(2 * WINDOW)[None, :])
    keys_s = WINDOW + 2 * ts
    bias_s = _rel_bias(rel_table, np.arange(ts)[:, None] + WINDOW - np.arange(keys_s)[None, :])
    bias_s = bias_s.reshape(SWA_HEADS * ts, keys_s)
    sink_rows = jnp.repeat(sinks[l].astype(F32), ts).reshape(SWA_HEADS * ts, 1)

    n = nb * t
    xp = x_prompt.reshape(n, D_MODEL)
    mk, mv = _norm_proj(mem_prompt.reshape(nb * MEM_TOKENS, D_MODEL), mem_norm_g[l].reshape(1, D_MODEL),
                        w_mem_kv[l].astype(BF16), (MEM_WIDTH, MEM_WIDTH), ((F32,), (F32,)), ROWS_MEM_PROJ)
    u, ub, qz, k, v, qm = _norm_proj(xp, mp['g1'], w_main, IN_SPLITS, IN_DTYPES, ROWS_NORM_PROJ)

    y_ssm, fin = _s5(ub, jnp.zeros((nb, N_CH_TILES * 2 * STATE_TILE), F32), s5_w, nb, t // L, L, S5_CHUNKS_PER_STEP)
    p_re, p_im = _tiles_to_state(fin)

    o_swa = _swa_prompt(qz, k, v, bias_p, sinks[l].astype(F32), nb, t, SWA_BLOCKS_PER_STEP)
    o_mem = _mem_prompt(qm, mk, mv, nb, t, ROWS_MEM_ATTN)
    h, xn2p, route = _merge(xp, u, y_ssm, o_swa, o_mem, mp, ROWS_MERGE)

    k4 = k.reshape(nb, t, SWA_KV_HEADS, SWA_HEAD_DIM)
    v4 = v.reshape(nb, t, SWA_KV_HEADS, SWA_HEAD_DIM)
    new_k_p, new_v_p = k4[:, -WINDOW:][None], v4[:, -WINDOW:][None]
    new_mk = mk.reshape(1, nb, MEM_TOKENS, MEM_HEADS, MEM_HEAD_DIM)
    new_mv = mv.reshape(1, nb, MEM_TOKENS, MEM_HEADS, MEM_HEAD_DIM)

    m = ns * ts
    xs = x_sample.reshape(m, D_MODEL)
    us, ubs, qzs, k_s, v_s, qms = _norm_proj(xs, mp['g1'], w_main, IN_SPLITS, IN_DTYPES, ROWS_NORM_PROJ)
    ys_ssm, fins = _s5(ubs, _state_to_tiles(state_ssm_re[l], state_ssm_im[l]), s5_w, ns, ts // L, L, S5_CHUNKS_PER_STEP)
    s_re, s_im = _tiles_to_state(fins)

    kk_all = jnp.concatenate([cache_swa_k[l].reshape(ns, WINDOW, SWA_KV_WIDTH).astype(F32),
                              k_s.reshape(ns, ts, SWA_KV_WIDTH)], axis=1)
    vv_all = jnp.concatenate([cache_swa_v[l].reshape(ns, WINDOW, SWA_KV_WIDTH).astype(F32),
                              v_s.reshape(ns, ts, SWA_KV_WIDTH)], axis=1)
    pad = jnp.zeros((ns, keys_s - WINDOW - ts, SWA_KV_WIDTH), F32)
    o_dec, roll_k, roll_v = _swa_decode(qzs.astype(F32).reshape(ns, ts, SWA_WIDTH),
                                        jnp.concatenate([kk_all, pad], axis=1),
                                        jnp.concatenate([vv_all, pad], axis=1), bias_s, sink_rows, DECODE_SEQS_PER_STEP)
    o_swa_s = o_dec.reshape(m, SWA_WIDTH).astype(BF16)

    o_mem_s = _mem_decode(qms.astype(F32).reshape(ns, ts, MEM_WIDTH), cache_mem_k, cache_mem_v, l, DECODE_SEQS_PER_STEP)
    o_mem_s = o_mem_s.reshape(m, MEM_WIDTH).astype(BF16)

    y_prompt = _sparse_moe(xn2p, route, h, w_g, w_u, w_d, gf, (ys_ssm, o_swa_s, o_mem_s)).reshape(nb, t, D_MODEL)
    hs_, xn2ps, routes = _merge(xs, us, ys_ssm, o_swa_s, o_mem_s, mp, ROWS_MERGE)
    y_sample = _moe(xn2ps, routes.T, w_g, w_u, w_d, hs_, gf, ROWS_DENSE_MOE).reshape(ns, ts, D_MODEL)

    new_k_s = roll_k.reshape(1, ns, WINDOW, SWA_KV_HEADS, SWA_HEAD_DIM).astype(cache_swa_k.dtype)
    new_v_s = roll_v.reshape(1, ns, WINDOW, SWA_KV_HEADS, SWA_HEAD_DIM).astype(cache_swa_v.dtype)

    return (y_prompt, y_sample,
            new_k_p, new_v_p, p_re[None], p_im[None], new_mk, new_mv,
            new_k_s, new_v_s, s_re[None].astype(state_ssm_re.dtype), s_im[None].astype(state_ssm_im.dtype))
```

```python
import functools
import math

import numpy as np
import jax
import jax.numpy as jnp
from jax import lax
from jax.experimental import pallas as pl
from jax.experimental.pallas import tpu as pltpu
from jax.experimental.pallas import tpu_sc as plsc

F32 = jnp.float32
BF16 = jnp.bfloat16

D_MODEL = 1024
SSM_WIDTH = 512
SSM_GROUP = 16
SSM_GROUPS = 32
SSM_STATE = 64
SWA_HEADS = 8
SWA_KV_HEADS = 2
SWA_REP = 4
SWA_HEAD_DIM = 64
SWA_WIDTH = 512
SWA_KV_WIDTH = 128
WINDOW = 128
REL_BUCKETS = 32
REL_MAX_DIST = 128
MEM_TOKENS = 256
MEM_HEADS = 4
MEM_HEAD_DIM = 128
MEM_WIDTH = 512
N_EXPERT_GROUPS = 4
EXPERTS_PER_GROUP = 8
N_EXPERTS = 32
D_EXPERT = 256
EPS = 1e-6
NEG_INF = -1e30

LANES = 128
GROUPS_PER_TILE = LANES // SSM_GROUP
N_CH_TILES = SSM_WIDTH // LANES
STATE_TILE = GROUPS_PER_TILE * SSM_STATE
VMEM_LIMIT = 56 * 1024 * 1024
ROWS_NORM_PROJ = 1024
ROWS_MEM_PROJ = 512
ROWS_MEM_ATTN = 1024
ROWS_MERGE = 512
ROWS_ROUTE_RANK = 1024
ROWS_COMBINE = 512
ROWS_DENSE_MOE = 1024
S5_CHUNKS_PER_STEP = 128
SWA_BLOCKS_PER_STEP = 4
DECODE_SEQS_PER_STEP = 8
S5_CHUNK = 8
S5_PANEL = 256

_TRANS_B = (((1,), (1,)), ((), ()))


def _cparams(*sem):
    return pltpu.CompilerParams(dimension_semantics=sem, vmem_limit_bytes=VMEM_LIMIT)


def _rms(x, g):
    return (x * lax.rsqrt(jnp.mean(x * x, axis=-1, keepdims=True) + EPS)) * g


def _dot(a, b):
    return jnp.dot(a, b, preferred_element_type=F32)


def _norm_proj_kernel(x_ref, g_ref, w_ref, *out_refs, splits, dtypes):
    xb = _rms(x_ref[...], g_ref[...]).astype(BF16)
    off = 0
    outs = iter(out_refs)
    for width, dts in zip(splits, dtypes):
        r = _dot(xb, w_ref[:, off:off + width])
        for dt in dts:
            next(outs)[...] = r.astype(dt)
        off += width


def _norm_proj(x, g, w, splits, dtypes, tile):
    n, d = x.shape
    tile = min(tile, n)
    flat = [(wd, dt) for wd, dts in zip(splits, dtypes) for dt in dts]
    return pl.pallas_call(
        functools.partial(_norm_proj_kernel, splits=tuple(splits), dtypes=tuple(dtypes)),
        grid=(n // tile,),
        in_specs=[pl.BlockSpec((tile, d), lambda i: (i, 0)),
                  pl.BlockSpec((1, d), lambda i: (0, 0)),
                  pl.BlockSpec((d, sum(splits)), lambda i: (0, 0), pipeline_mode=pl.Buffered(1))],
        out_specs=[pl.BlockSpec((tile, wd), lambda i: (i, 0)) for wd, _ in flat],
        out_shape=[jax.ShapeDtypeStruct((n, wd), dt) for wd, dt in flat],
        compiler_params=_cparams("parallel"),
        name="norm_proj",
    )(x, g, w)


def _s5_weights(lam_re, lam_im, log_dt, bm_re, bm_im, cm_re, cm_im, L):
    nt, gt, P, H = N_CH_TILES, GROUPS_PER_TILE, SSM_STATE, SSM_GROUP
    lr, li = lam_re.astype(F32), lam_im.astype(F32)
    dt = jnp.exp(log_dt.astype(F32))[:, None]
    mag = jnp.exp(lr * dt)
    a_re = mag * jnp.cos(li * dt)
    a_im = mag * jnp.sin(li * dt)
    den = lr * lr + li * li
    f_re = ((a_re - 1.0) * lr + a_im * li) / den
    f_im = (a_im * lr - (a_re - 1.0) * li) / den
    br, bi = bm_re.astype(F32), bm_im.astype(F32)
    bb_re = f_re[..., None] * br - f_im[..., None] * bi
    bb_im = f_re[..., None] * bi + f_im[..., None] * br
    pr, pi = [jnp.ones_like(a_re)], [jnp.zeros_like(a_im)]
    for _ in range(L):
        pr.append(pr[-1] * a_re - pi[-1] * a_im)
        pi.append(pr[-2] * a_im + pi[-1] * a_re)
    ap_re, ap_im = jnp.stack(pr), jnp.stack(pi)
    cr, ci = cm_re.astype(F32), cm_im.astype(F32)
    ca_re = cr[None] * ap_re[:, :, None, :] - ci[None] * ap_im[:, :, None, :]
    ca_im = cr[None] * ap_im[:, :, None, :] + ci[None] * ap_re[:, :, None, :]

    rev_re = jnp.stack([pr[L - 1 - s] for s in range(L)])
    rev_im = jnp.stack([pi[L - 1 - s] for s in range(L)])
    ws_re = rev_re[..., None] * bb_re[None] - rev_im[..., None] * bb_im[None]
    ws_im = rev_re[..., None] * bb_im[None] + rev_im[..., None] * bb_re[None]
    c_st = jnp.concatenate([ws_re.transpose(0, 1, 3, 2).reshape(L, nt, gt * H, P),
                            ws_im.transpose(0, 1, 3, 2).reshape(L, nt, gt * H, P)], axis=3).transpose(1, 0, 2, 3)
    so = lambda ca: ca[1:].transpose(1, 3, 0, 2).reshape(nt, gt * P, L * H)
    c_so = jnp.concatenate([so(ca_re), so(-ca_im)], axis=1)
    prod = (ca_re[:L][:, :, None, :, :] * bb_re.transpose(0, 2, 1)[None, :, :, None, :]
            - ca_im[:L][:, :, None, :, :] * bb_im.transpose(0, 2, 1)[None, :, :, None, :])
    k_lag = jnp.sum(prod, axis=-1).transpose(1, 2, 0, 3)
    c_k = k_lag.reshape(nt, gt * H, L * H)
    w_st, w_out, toep = _s5_expand(c_st, c_so, c_k, L)

    def per_tile(v):
        return v.reshape(nt, 1, STATE_TILE)

    return w_st, w_out, toep, per_tile(pr[L]), per_tile(pi[L])


def _s5_expand_kernel(cst_ref, cso_ref, ck_ref, wst_ref, wso_ref, toep_ref, *, L):
    hp = lax.Precision.HIGHEST
    P, H = SSM_STATE, SSM_GROUP
    iota = lambda shape, d: lax.broadcasted_iota(jnp.int32, shape, d)
    one = lambda cond: jnp.where(cond, 1.0, 0.0).astype(F32)

    r, c = iota((2 * P, 2 * STATE_TILE), 0), iota((2 * P, 2 * STATE_TILE), 1)
    rep_st = one((r // P == c // STATE_TILE) & (r % P == c % P))
    r, c = iota((LANES, 2 * STATE_TILE), 0), iota((LANES, 2 * STATE_TILE), 1)
    own_st = one(r // H == (c % STATE_TILE) // P)
    for s in range(L):
        blk = jnp.dot(cst_ref[s], rep_st, precision=hp, preferred_element_type=F32) * own_st
        wst_ref[s * LANES:(s + 1) * LANES, :] = blk.astype(BF16)

    r, c = iota((LANES, LANES), 0), iota((LANES, LANES), 1)
    pick = [one((r // H == t) & (r % H == c % H)) for t in range(L)]
    own_k = one(r // H == c // H)
    r, c = iota((2 * STATE_TILE, LANES), 0), iota((2 * STATE_TILE, LANES), 1)
    own_so = one((r % STATE_TILE) // P == c // H)
    cso = cso_ref[...]
    for t in range(L):
        blk = jnp.dot(cso, pick[t], precision=hp, preferred_element_type=F32) * own_so
        wso_ref[:, t * LANES:(t + 1) * LANES] = blk.astype(BF16)
    ck = ck_ref[...]
    lag = [(jnp.dot(ck, pick[t], precision=hp, preferred_element_type=F32) * own_k).astype(BF16) for t in range(L)]
    zero = jnp.zeros((LANES, LANES), BF16)
    for s in range(L):
        for t in range(L):
            toep_ref[s * LANES:(s + 1) * LANES, t * LANES:(t + 1) * LANES] = lag[t - s] if t >= s else zero


def _s5_expand(c_st, c_so, c_k, L):
    lk = L * LANES
    st2 = 2 * STATE_TILE
    return pl.pallas_call(
        functools.partial(_s5_expand_kernel, L=L),
        grid=(N_CH_TILES,),
        in_specs=[pl.BlockSpec((None, L, LANES, 2 * SSM_STATE), lambda j: (j, 0, 0, 0)),
                  pl.BlockSpec((None, st2, L * SSM_GROUP), lambda j: (j, 0, 0)),
                  pl.BlockSpec((None, LANES, L * SSM_GROUP), lambda j: (j, 0, 0))],
        out_specs=[pl.BlockSpec((None, lk, st2), lambda j: (j, 0, 0)),
                   pl.BlockSpec((None, st2, lk), lambda j: (j, 0, 0)),
                   pl.BlockSpec((None, lk, lk), lambda j: (j, 0, 0))],
        out_shape=[jax.ShapeDtypeStruct((N_CH_TILES, lk, st2), BF16),
                   jax.ShapeDtypeStruct((N_CH_TILES, st2, lk), BF16),
                   jax.ShapeDtypeStruct((N_CH_TILES, lk, lk), BF16)],
        compiler_params=_cparams("parallel"),
        name="s5_expand_weights",
    )(c_st, c_so, c_k)


def _to_chunks(u, nb, nc, L):
    return (u.reshape(nb, nc, L, N_CH_TILES, LANES).transpose(1, 0, 3, 2, 4)
            .reshape(nc * nb, N_CH_TILES * L * LANES))


def _from_chunks(y, nb, nc, L):
    return (y.reshape(nc, nb, N_CH_TILES, L, LANES).transpose(1, 0, 3, 2, 4)
            .reshape(nb * nc * L, SSM_WIDTH))


def _s5_kernel(x_ref, h0_ref, are_ref, aim_ref, ws_ref, t_ref, wo_ref, y_ref, fin_ref,
               hr_ref, hi_ref, d_ref, hs_ref, *, cb, nb):
    ci = pl.program_id(1)

    @pl.when(ci == 0)
    def _():
        hr_ref[...] = h0_ref[:, 0:STATE_TILE]
        hi_ref[...] = h0_ref[:, STATE_TILE:2 * STATE_TILE]

    x = x_ref[...]
    d_ref[...] = _dot(x, ws_ref[...])
    ar = jnp.broadcast_to(are_ref[...], (nb, STATE_TILE))
    ai = jnp.broadcast_to(aim_ref[...], (nb, STATE_TILE))

    def body(c, carry):
        hr, hi = carry
        r0 = pl.multiple_of(c * nb, nb)
        hs_ref[pl.ds(r0, nb), 0:STATE_TILE] = hr
        hs_ref[pl.ds(r0, nb), STATE_TILE:2 * STATE_TILE] = hi
        d = d_ref[pl.ds(r0, nb), :]
        return (ar * hr - ai * hi + d[:, 0:STATE_TILE],
                ar * hi + ai * hr + d[:, STATE_TILE:2 * STATE_TILE])

    hr, hi = lax.fori_loop(0, cb, body, (hr_ref[...], hi_ref[...]))
    hr_ref[...] = hr
    hi_ref[...] = hi
    hsb = hs_ref[...].astype(BF16)
    for c0 in range(0, t_ref.shape[1], S5_PANEL):
        c1 = c0 + S5_PANEL
        y_ref[:, c0:c1] = _dot(x[:, 0:c1], t_ref[0:c1, c0:c1]) + _dot(hsb, wo_ref[:, c0:c1])

    @pl.when(ci == pl.num_programs(1) - 1)
    def _():
        fin_ref[:, 0:STATE_TILE] = hr
        fin_ref[:, STATE_TILE:2 * STATE_TILE] = hi


def _s5(ub, h0, weights, nb, nc, L, chunk_block):
    w_st, w_so, toep, a_re, a_im = weights
    xc = _to_chunks(ub, nb, nc, L)
    cb = min(chunk_block, nc)
    rows = cb * nb
    lk = L * LANES
    st2 = 2 * STATE_TILE
    tile_w = lambda shape: pl.BlockSpec((None,) + shape, lambda j, c: (j, 0, 0))
    y, fin = pl.pallas_call(
        functools.partial(_s5_kernel, cb=cb, nb=nb),
        grid=(N_CH_TILES, nc // cb),
        in_specs=[pl.BlockSpec((rows, lk), lambda j, c: (c, j)),
                  pl.BlockSpec((nb, st2), lambda j, c: (0, j)),
                  tile_w((1, STATE_TILE)), tile_w((1, STATE_TILE)),
                  tile_w((lk, st2)), tile_w((lk, lk)), tile_w((st2, lk))],
        out_specs=[pl.BlockSpec((rows, lk), lambda j, c: (c, j)),
                   pl.BlockSpec((nb, st2), lambda j, c: (0, j))],
        out_shape=[jax.ShapeDtypeStruct((nc * nb, N_CH_TILES * lk), F32),
                   jax.ShapeDtypeStruct((nb, N_CH_TILES * st2), F32)],
        scratch_shapes=[pltpu.VMEM((nb, STATE_TILE), F32), pltpu.VMEM((nb, STATE_TILE), F32),
                        pltpu.VMEM((rows, st2), F32), pltpu.VMEM((rows, st2), F32)],
        compiler_params=_cparams("parallel", "arbitrary"),
        name="s5_chunked_scan",
    )(xc, h0, a_re, a_im, w_st, toep, w_so)
    return _from_chunks(y, nb, nc, L), fin


def _state_to_tiles(h_re, h_im):
    nb = h_re.shape[0]
    r = h_re.astype(F32).reshape(nb, N_CH_TILES, STATE_TILE)
    i = h_im.astype(F32).reshape(nb, N_CH_TILES, STATE_TILE)
    return jnp.concatenate([r, i], axis=-1).reshape(nb, N_CH_TILES * 2 * STATE_TILE)


def _tiles_to_state(h):
    nb = h.shape[0]
    h = h.reshape(nb, N_CH_TILES, 2, GROUPS_PER_TILE, SSM_STATE)
    return (h[:, :, 0].reshape(nb, SSM_GROUPS, SSM_STATE), h[:, :, 1].reshape(nb, SSM_GROUPS, SSM_STATE))


def _t5_bucket(dist):
    n = np.maximum(dist, 0)
    max_exact = REL_BUCKETS // 2
    nf = np.maximum(n, 1).astype(np.float32)
    large = max_exact + (np.log(nf / np.float32(max_exact)) / np.float32(math.log(REL_MAX_DIST / max_exact))
                         * np.float32(REL_BUCKETS - max_exact)).astype(np.int32)
    large = np.minimum(large, REL_BUCKETS - 1)
    return np.where(n < max_exact, n, large)


def _rel_bias(rel_table, dist):
    bucket = _t5_bucket(dist)
    tab = rel_table.astype(F32)
    out = jnp.zeros((SWA_HEADS,) + dist.shape, F32)
    for b in range(REL_BUCKETS):
        sel = jnp.asarray(bucket == b)
        if bool((bucket == b).any()):
            out = jnp.where(sel[None], tab[b].reshape((SWA_HEADS,) + (1,) * dist.ndim), out)
    return out


def _swa_prompt_kernel(sink_ref, q_ref, kp_ref, kc_ref, vp_ref, vc_ref, bias_ref, o_ref, kk_ref, vv_ref, *, qblocks):
    step = pl.program_id(1)
    kk_ref[0:WINDOW, :] = kp_ref[...].astype(BF16)
    kk_ref[WINDOW:, :] = kc_ref[...].astype(BF16)
    vv_ref[0:WINDOW, :] = vp_ref[...].astype(BF16)
    vv_ref[WINDOW:, :] = vc_ref[...].astype(BF16)
    row = lax.broadcasted_iota(jnp.int32, (WINDOW, 2 * WINDOW), 0)
    col = lax.broadcasted_iota(jnp.int32, (WINDOW, 2 * WINDOW), 1)
    dist = row + WINDOW - col
    band = (dist >= 0) & (dist < WINDOW)
    lane = lax.broadcasted_iota(jnp.int32, (WINDOW, LANES), 1)
    low = lane < SWA_HEAD_DIM

    def block(j, carry):
        r0 = pl.multiple_of(j * WINDOW, WINDOW)
        kk = kk_ref[pl.ds(r0, 2 * WINDOW), :]
        vv = vv_ref[pl.ds(r0, 2 * WINDOW), :]
        valid = band & ((col >= WINDOW) | (step * qblocks + j > 0))
        for t in range(SWA_REP):
            q2 = q_ref[pl.ds(r0, WINDOW), t * LANES:(t + 1) * LANES]
            outs = []
            for half in range(SWA_KV_HEADS):
                h = t + SWA_REP * half
                qh = jnp.where(low if half == 0 else jnp.logical_not(low), q2, jnp.zeros_like(q2))
                s = lax.dot_general(qh, kk, _TRANS_B, preferred_element_type=F32)
                s = jnp.where(valid, s + bias_ref[h], NEG_INF)
                sink = sink_ref[h]
                m = jnp.maximum(jnp.max(s, axis=-1, keepdims=True), sink)
                e = jnp.exp(s - m)
                den = jnp.sum(e, axis=-1, keepdims=True) + jnp.exp(sink - m)
                outs.append(_dot(e.astype(BF16), vv) * (1.0 / den))
            o_ref[pl.ds(r0, WINDOW), t * LANES:(t + 1) * LANES] = jnp.where(low, outs[0], outs[1]).astype(BF16)
        return carry

    lax.fori_loop(0, qblocks, block, 0)


def _swa_prompt(q, k, v, bias, sinks, nb, t, qblocks):
    nstep = t // (WINDOW * qblocks)
    rows = WINDOW * qblocks
    cur = lambda b, i: (b * nstep + i, 0)
    prev = lambda b, i: (b * nstep * qblocks + jnp.maximum(i * qblocks - 1, 0), 0)
    return pl.pallas_call(
        functools.partial(_swa_prompt_kernel, qblocks=qblocks),
        grid=(nb, nstep),
        in_specs=[pl.BlockSpec(memory_space=pltpu.SMEM),
                  pl.BlockSpec((rows, SWA_WIDTH), cur),
                  pl.BlockSpec((WINDOW, SWA_KV_WIDTH), prev),
                  pl.BlockSpec((rows, SWA_KV_WIDTH), cur),
                  pl.BlockSpec((WINDOW, SWA_KV_WIDTH), prev),
                  pl.BlockSpec((rows, SWA_KV_WIDTH), cur),
                  pl.BlockSpec((SWA_HEADS, WINDOW, 2 * WINDOW), lambda b, i: (0, 0, 0))],
        out_specs=pl.BlockSpec((rows, SWA_WIDTH), cur),
        out_shape=jax.ShapeDtypeStruct((nb * t, SWA_WIDTH), BF16),
        scratch_shapes=[pltpu.VMEM((rows + WINDOW, SWA_KV_WIDTH), BF16),
                        pltpu.VMEM((rows + WINDOW, SWA_KV_WIDTH), BF16)],
        compiler_params=_cparams("parallel", "parallel"),
        name="swa_prompt",
    )(sinks, q, k, k, v, v, bias)


def _swa_decode_kernel(q_ref, k_ref, v_ref, bias_ref, sink_ref, o_ref, nk_ref, nv_ref, *, seqs, tq):
    rows, keys = SWA_HEADS * tq, k_ref.shape[1]
    nk_ref[...] = k_ref[:, tq:tq + WINDOW, :]
    nv_ref[...] = v_ref[:, tq:tq + WINDOW, :]
    low = lax.broadcasted_iota(jnp.int32, (tq, LANES), 1) < SWA_HEAD_DIM
    qi = lax.broadcasted_iota(jnp.int32, (rows, keys), 0) % tq
    col = lax.broadcasted_iota(jnp.int32, (rows, keys), 1)
    dist = qi + WINDOW - col
    valid = (dist >= 0) & (dist < WINDOW)
    bias = bias_ref[...]
    sink = sink_ref[...]
    for s_i in range(seqs):
        q = q_ref[s_i]
        tiles = [q[:, t * LANES:(t + 1) * LANES] for t in range(SWA_REP)]
        qh = jnp.concatenate([jnp.where(low, x, 0.0) for x in tiles]
                             + [jnp.where(low, 0.0, x) for x in tiles], axis=0)
        kk = k_ref[s_i].astype(BF16)
        s = lax.dot_general(qh.astype(BF16), kk, _TRANS_B, preferred_element_type=F32)
        s = jnp.where(valid, s + bias, NEG_INF)
        m = jnp.maximum(jnp.max(s, axis=-1, keepdims=True), sink)
        e = jnp.exp(s - m)
        den = jnp.sum(e, axis=-1, keepdims=True) + jnp.exp(sink - m)
        o = _dot(e.astype(BF16), v_ref[s_i].astype(BF16)) * (1.0 / den)
        for t in range(SWA_REP):
            o_ref[s_i, :, t * LANES:(t + 1) * LANES] = jnp.where(
                low, o[t * tq:(t + 1) * tq], o[(t + SWA_REP) * tq:(t + SWA_REP + 1) * tq])


def _swa_decode(q, k_all, v_all, bias, sink_rows, seqs):
    nseq, tq, _ = q.shape
    rows = SWA_HEADS * tq
    keys = k_all.shape[1]
    seqs = min(seqs, nseq)
    return pl.pallas_call(
        functools.partial(_swa_decode_kernel, seqs=seqs, tq=tq),
        grid=(nseq // seqs,),
        in_specs=[pl.BlockSpec((seqs, tq, SWA_WIDTH), lambda i: (i, 0, 0)),
                  pl.BlockSpec((seqs, keys, LANES), lambda i: (i, 0, 0)),
                  pl.BlockSpec((seqs, keys, LANES), lambda i: (i, 0, 0)),
                  pl.BlockSpec((rows, keys), lambda i: (0, 0)),
                  pl.BlockSpec((rows, 1), lambda i: (0, 0))],
        out_specs=[pl.BlockSpec((seqs, tq, SWA_WIDTH), lambda i: (i, 0, 0)),
                   pl.BlockSpec((seqs, WINDOW, LANES), lambda i: (i, 0, 0)),
                   pl.BlockSpec((seqs, WINDOW, LANES), lambda i: (i, 0, 0))],
        out_shape=[jax.ShapeDtypeStruct((nseq, tq, SWA_WIDTH), F32),
                   jax.ShapeDtypeStruct((nseq, WINDOW, LANES), F32),
                   jax.ShapeDtypeStruct((nseq, WINDOW, LANES), F32)],
        compiler_params=_cparams("parallel"),
        name="swa_decode",
    )(q, k_all, v_all, bias, sink_rows)


def _softmax(s):
    m = jnp.max(s, axis=-1, keepdims=True)
    e = jnp.exp(s - m)
    return e * (1.0 / jnp.sum(e, axis=-1, keepdims=True))


def _mem_prompt_kernel(q_ref, k_ref, v_ref, o_ref, s_ref, p_ref):
    scale = MEM_HEAD_DIM ** -0.5
    heads = [slice(h * MEM_HEAD_DIM, (h + 1) * MEM_HEAD_DIM) for h in range(MEM_HEADS)]
    for h, sl in enumerate(heads):
        s_ref[h] = lax.dot_general(q_ref[:, sl], k_ref[:, sl].astype(BF16), _TRANS_B, preferred_element_type=F32)
    s = s_ref[...] * scale
    e = jnp.exp(s - jnp.max(s, axis=-1, keepdims=True))
    p_ref[...] = e.astype(BF16)
    inv = 1.0 / jnp.sum(e, axis=-1, keepdims=True)
    for h, sl in enumerate(heads):
        o_ref[:, sl] = (_dot(p_ref[h], v_ref[:, sl].astype(BF16)) * inv[h]).astype(BF16)


def _mem_prompt(qm, mk, mv, nb, t, tile):
    tile = min(tile, t)
    nt = t // tile
    return pl.pallas_call(
        _mem_prompt_kernel,
        grid=(nb, nt),
        in_specs=[pl.BlockSpec((tile, MEM_WIDTH), lambda b, i: (b * nt + i, 0)),
                  pl.BlockSpec((MEM_TOKENS, MEM_WIDTH), lambda b, i: (b, 0)),
                  pl.BlockSpec((MEM_TOKENS, MEM_WIDTH), lambda b, i: (b, 0))],
        out_specs=pl.BlockSpec((tile, MEM_WIDTH), lambda b, i: (b * nt + i, 0)),
        out_shape=jax.ShapeDtypeStruct((nb * t, MEM_WIDTH), BF16),
        scratch_shapes=[pltpu.VMEM((MEM_HEADS, tile, MEM_TOKENS), F32), pltpu.VMEM((MEM_HEADS, tile, MEM_TOKENS), BF16)],
        compiler_params=_cparams("parallel", "parallel"),
        name="mem_prompt",
    )(qm, mk, mv)


def _mem_decode_kernel(q_ref, k_ref, v_ref, o_ref, *, seqs):
    tq = q_ref.shape[1]
    rows, cols = MEM_HEADS * tq, MEM_TOKENS * MEM_HEADS
    k2 = k_ref.reshape(seqs, cols, MEM_HEAD_DIM)
    v2 = v_ref.reshape(seqs, cols, MEM_HEAD_DIM)
    scale = MEM_HEAD_DIM ** -0.5
    own = (lax.broadcasted_iota(jnp.int32, (rows, cols), 1) % MEM_HEADS
           == lax.broadcasted_iota(jnp.int32, (rows, cols), 0) // tq)
    for s_i in range(seqs):
        q = q_ref[s_i]
        qb = jnp.concatenate([q[:, h * MEM_HEAD_DIM:(h + 1) * MEM_HEAD_DIM] for h in range(MEM_HEADS)], axis=0)
        s = lax.dot_general(qb.astype(BF16), k2[s_i].astype(BF16), _TRANS_B, preferred_element_type=F32) * scale
        p = _softmax(jnp.where(own, s, NEG_INF)).astype(BF16)
        o = _dot(p, v2[s_i].astype(BF16))
        for h in range(MEM_HEADS):
            o_ref[s_i, :, h * MEM_HEAD_DIM:(h + 1) * MEM_HEAD_DIM] = o[h * tq:(h + 1) * tq, :]


def _mem_decode(q, k, v, layer, seqs):
    nseq, tq, _ = q.shape
    seqs = min(seqs, nseq)
    cache = pl.BlockSpec((None, seqs, MEM_TOKENS, MEM_HEADS, MEM_HEAD_DIM), lambda i: (layer, i, 0, 0, 0))
    return pl.pallas_call(
        functools.partial(_mem_decode_kernel, seqs=seqs),
        grid=(nseq // seqs,),
        in_specs=[pl.BlockSpec((seqs, tq, MEM_WIDTH), lambda i: (i, 0, 0)), cache, cache],
        out_specs=pl.BlockSpec((seqs, tq, MEM_WIDTH), lambda i: (i, 0, 0)),
        out_shape=jax.ShapeDtypeStruct((nseq, tq, MEM_WIDTH), F32),
        compiler_params=_cparams("parallel"),
        name="mem_decode",
    )(q, k, v)


ROUTER_ROWS = 40
GATES_COL0 = SSM_WIDTH + SWA_WIDTH + 2 * SWA_KV_WIDTH + MEM_WIDTH
ROUTE_ROWS = 8
HALF = D_MODEL // 2


def _pack_halves(xb):
    hi = pltpu.bitcast(xb[:, 0:HALF].astype(F32), jnp.int32)
    lo = pltpu.bitcast(xb[:, HALF:D_MODEL].astype(F32), jnp.int32)
    return hi | lax.shift_right_logical(lo, jnp.int32(16))


def _unpack_halves(p):
    hi = pltpu.bitcast(p & jnp.int32(-65536), F32).astype(BF16)
    lo = pltpu.bitcast(lax.shift_left(p, jnp.int32(16)), F32).astype(BF16)
    return hi, lo


def _merge_kernel(x_ref, u_ref, y_ref, os_ref, om_ref, g1_ref, wg_ref, dsk_ref, wglu_ref, bglu_ref,
                  wbs_ref, wbw_ref, wbm_ref, wout_ref, g2_ref, wr_ref, br_ref,
                  h_ref, xn2_ref, route_ref):
    x = x_ref[...]
    tt = x.shape[0]
    xb = _rms(x, g1_ref[...]).astype(BF16)
    z = jax.nn.gelu(y_ref[...] + dsk_ref[...] * u_ref[...])
    z = z * jax.nn.sigmoid(_dot(z.astype(BF16), wglu_ref[...]) + bglu_ref[...])
    gate = lambda b: jax.nn.sigmoid(_dot(xb, wg_ref[:, GATES_COL0 + b * D_MODEL:GATES_COL0 + (b + 1) * D_MODEL]))
    merged = gate(0) * _dot(z.astype(BF16), wbs_ref[...])
    merged = merged + gate(1) * _dot(os_ref[...], wbw_ref[...])
    merged = merged + gate(2) * _dot(om_ref[...], wbm_ref[...])
    h = x + _dot(merged.astype(BF16), wout_ref[...])
    h_ref[...] = h
    xn2 = _rms(h, g2_ref[...]).astype(BF16)
    xn2_ref[...] = _pack_halves(xn2)

    lt = lax.dot_general(wr_ref[...], xn2, _TRANS_B, preferred_element_type=F32) + br_ref[...]
    gl = lt[N_EXPERTS:N_EXPERTS + N_EXPERT_GROUPS]
    ge = jnp.exp(gl - jnp.max(gl, axis=0, keepdims=True))
    gp = ge / jnp.sum(ge, axis=0, keepdims=True)
    gw = jnp.max(gp, axis=0, keepdims=True)
    gidx = jnp.full((1, tt), N_EXPERT_GROUPS - 1, jnp.int32)
    for r in range(N_EXPERT_GROUPS - 2, -1, -1):
        gidx = jnp.where(gp[r:r + 1] == gw, r, gidx)
    ein = lt[(N_EXPERT_GROUPS - 1) * EXPERTS_PER_GROUP:N_EXPERTS]
    for r in range(N_EXPERT_GROUPS - 2, -1, -1):
        ein = jnp.where(gidx == r, lt[r * EXPERTS_PER_GROUP:(r + 1) * EXPERTS_PER_GROUP], ein)
    ee = jnp.exp(ein - jnp.max(ein, axis=0, keepdims=True))
    ep = ee / jnp.sum(ee, axis=0, keepdims=True)
    rowi = lax.broadcasted_iota(jnp.int32, (EXPERTS_PER_GROUP, tt), 0)
    p1 = jnp.max(ep, axis=0, keepdims=True)
    e1 = jnp.min(jnp.where(ep == p1, rowi, EXPERTS_PER_GROUP), axis=0, keepdims=True)
    ep2 = jnp.where(rowi == e1, -1.0, ep)
    p2 = jnp.max(ep2, axis=0, keepdims=True)
    e2 = jnp.min(jnp.where(ep2 == p2, rowi, EXPERTS_PER_GROUP), axis=0, keepdims=True)
    tot = p1 + p2
    w1 = p1 / tot * gw
    w2 = p2 / tot * gw
    id1 = (gidx * EXPERTS_PER_GROUP + e1).astype(F32)
    id2 = (gidx * EXPERTS_PER_GROUP + e2).astype(F32)
    route_ref[...] = jnp.concatenate([id1, id2, w1, w2, jnp.zeros((ROUTE_ROWS - 4, tt), F32)], axis=0)


def _merge(x, u, y, o_swa, o_mem, p, tile):
    n = x.shape[0]
    tile = min(tile, n)
    row = lambda i: (i, 0)
    const = lambda i: (0, 0)
    full = lambda a: pl.BlockSpec(a.shape, const, pipeline_mode=pl.Buffered(1))
    weights = [p['g1'], p['w_gates'], p['d_skip'], p['w_glu'], p['b_glu'], p['w_br_ssm'], p['w_br_swa'],
               p['w_br_mem'], p['w_out'], p['g2'], p['w_router'], p['b_router']]
    return pl.pallas_call(
        _merge_kernel,
        grid=(n // tile,),
        in_specs=[pl.BlockSpec((tile, D_MODEL), row), pl.BlockSpec((tile, SSM_WIDTH), row),
                  pl.BlockSpec((tile, SSM_WIDTH), row), pl.BlockSpec((tile, SWA_WIDTH), row),
                  pl.BlockSpec((tile, MEM_WIDTH), row)] + [full(w) for w in weights],
        out_specs=[pl.BlockSpec((tile, D_MODEL), row), pl.BlockSpec((tile, HALF), row),
                   pl.BlockSpec((ROUTE_ROWS, tile), lambda i: (0, i))],
        out_shape=[jax.ShapeDtypeStruct((n, D_MODEL), F32), jax.ShapeDtypeStruct((n, HALF), jnp.int32),
                   jax.ShapeDtypeStruct((ROUTE_ROWS, n), F32)],
        compiler_params=_cparams("parallel"),
        name="merge_router",
    )(x, u, y, o_swa, o_mem, *weights)


def _expert_mlp(xp, wg, wu, wd):
    hi, lo = _unpack_halves(xp)
    g = _dot(hi, wg[0:HALF, :]) + _dot(lo, wg[HALF:D_MODEL, :])
    u = _dot(hi, wu[0:HALF, :]) + _dot(lo, wu[HALF:D_MODEL, :])
    hh = jax.nn.silu(g) * u
    return _dot(hh.astype(BF16), wd[...])


def _moe_kernel(xn2_ref, rt_ref, wg_ref, wu_ref, wd_ref, h_ref, gf_ref, o_ref, acc_ref):
    e = pl.program_id(1)

    @pl.when(e == 0)
    def _():
        acc_ref[...] = jnp.zeros_like(acc_ref)

    o = _expert_mlp(xn2_ref[...], wg_ref[...].astype(BF16), wu_ref[...].astype(BF16), wd_ref[...].astype(BF16))
    ef = e.astype(F32)
    c = (jnp.where(rt_ref[:, 0:1] == ef, rt_ref[:, 2:3], 0.0)
         + jnp.where(rt_ref[:, 1:2] == ef, rt_ref[:, 3:4], 0.0))
    acc_ref[...] += c * o

    @pl.when(e == N_EXPERTS - 1)
    def _():
        o_ref[...] = _rms(h_ref[...] + acc_ref[...], gf_ref[...])


def _moe(xn2, route_t, w_g, w_u, w_d, h, gf, tile):
    n = h.shape[0]
    tile = min(tile, n)
    return pl.pallas_call(
        _moe_kernel,
        grid=(n // tile, N_EXPERTS),
        in_specs=[pl.BlockSpec((tile, HALF), lambda i, e: (i, 0)),
                  pl.BlockSpec((tile, ROUTE_ROWS), lambda i, e: (i, 0)),
                  pl.BlockSpec((None, D_MODEL, D_EXPERT), lambda i, e: (e, 0, 0)),
                  pl.BlockSpec((None, D_MODEL, D_EXPERT), lambda i, e: (e, 0, 0)),
                  pl.BlockSpec((None, D_EXPERT, D_MODEL), lambda i, e: (e, 0, 0)),
                  pl.BlockSpec((tile, D_MODEL), lambda i, e: (i, 0)),
                  pl.BlockSpec((1, D_MODEL), lambda i, e: (0, 0))],
        out_specs=pl.BlockSpec((tile, D_MODEL), lambda i, e: (i, 0)),
        out_shape=jax.ShapeDtypeStruct((n, D_MODEL), F32),
        scratch_shapes=[pltpu.VMEM((tile, D_MODEL), F32)],
        compiler_params=_cparams("parallel", "arbitrary"),
        name="moe_final_norm",
    )(xn2, route_t, w_g, w_u, w_d, h, gf)


EXPERT_ROW_TILE = 256
EXPERT_SLOTS = 4
SC_CORES = 2
SC_SUBCORES = 16
SC_WORKERS = SC_CORES * SC_SUBCORES
SC_SCATTER_ROWS = 64
SC_GATHER_ROWS = 64


def _route_rank_kernel(r_ref, rank_ref, cnt_ref, base_ref):
    i = pl.program_id(0)
    tt = r_ref.shape[1]

    @pl.when(i == 0)
    def _():
        base_ref[...] = jnp.zeros_like(base_ref)

    ids = r_ref[0:2, :].astype(jnp.int32)
    e_iota = lax.broadcasted_iota(jnp.int32, (N_EXPERTS, tt), 0)
    oh1 = jnp.where(e_iota == ids[0:1], 1.0, 0.0)
    oh2 = jnp.where(e_iota == ids[1:2], 1.0, 0.0)
    before = (lax.broadcasted_iota(jnp.int32, (tt, tt), 0) < lax.broadcasted_iota(jnp.int32, (tt, tt), 1))
    tri = jnp.where(before, 1.0, 0.0).astype(BF16)
    c1 = _dot(oh1.astype(BF16), tri)
    c2 = _dot(oh2.astype(BF16), tri)
    tot1 = jnp.sum(oh1, axis=1, keepdims=True)
    tot2 = jnp.sum(oh2, axis=1, keepdims=True)
    base = base_ref[:, 0:1]
    rank1 = jnp.sum(oh1 * (base + c1), axis=0, keepdims=True)
    rank2 = jnp.sum(oh2 * (base + tot1 + c2), axis=0, keepdims=True)
    rank_ref[...] = jnp.concatenate([rank1, rank2, jnp.zeros((ROUTE_ROWS - 2, tt), F32)], axis=0).astype(jnp.int32)
    new_base = jnp.broadcast_to(base + tot1 + tot2, base_ref.shape)
    base_ref[...] = new_base
    cnt_ref[...] = new_base.astype(jnp.int32)


def _route_rank(route, tile):
    n = route.shape[1]
    tile = min(tile, n)
    return pl.pallas_call(
        _route_rank_kernel,
        grid=(n // tile,),
        in_specs=[pl.BlockSpec((ROUTE_ROWS, tile), lambda i: (0, i))],
        out_specs=[pl.BlockSpec((ROUTE_ROWS, tile), lambda i: (0, i)),
                   pl.BlockSpec((N_EXPERTS, LANES), lambda i: (0, 0))],
        out_shape=[jax.ShapeDtypeStruct((ROUTE_ROWS, n), jnp.int32),
                   jax.ShapeDtypeStruct((N_EXPERTS, LANES), jnp.int32)],
        scratch_shapes=[pltpu.VMEM((N_EXPERTS, LANES), F32)],
        compiler_params=_cparams("arbitrary"),
        name="route_rank",
    )(route)


def _sc_mesh():
    return plsc.VectorSubcoreMesh(core_axis_name="core", subcore_axis_name="subcore")


def _sc_scatter_pairs(x, pos, rows_out):
    n, d = x.shape
    per_w = n // SC_WORKERS
    window = min(SC_SCATTER_ROWS, per_w)

    @pl.kernel(out_type=jax.ShapeDtypeStruct((rows_out, d), x.dtype), mesh=_sc_mesh(),
               scratch_types=[pltpu.VMEM((window,), jnp.int32), pltpu.VMEM((window,), jnp.int32),
                              pltpu.VMEM((window, d), x.dtype), pltpu.SemaphoreType.DMA, pltpu.SemaphoreType.DMA,
                              pltpu.SemaphoreType.DMA])
    def scatter(x_hbm, p_hbm, o_hbm, i1_v, i2_v, rows_v, sem_a, sem_b, sem_c):
        wid = lax.axis_index("subcore") * SC_CORES + lax.axis_index("core")

        @pl.loop(0, per_w // window)
        def _(j):
            base = wid * per_w + j * window
            load_i1 = pltpu.async_copy(p_hbm.at[pl.ds(base, window)], i1_v, sem_a)
            load_i2 = pltpu.async_copy(p_hbm.at[pl.ds(n + base, window)], i2_v, sem_b)
            load_x = pltpu.async_copy(x_hbm.at[pl.ds(base, window)], rows_v, sem_c)
            load_i1.wait()
            load_i2.wait()
            load_x.wait()
            put_1 = pltpu.async_copy(rows_v, o_hbm.at[i1_v], sem_a)
            put_2 = pltpu.async_copy(rows_v, o_hbm.at[i2_v], sem_b)
            put_1.wait()
            put_2.wait()

    return scatter(x, pos)


def _sc_gather_rows(table, idx):
    m = idx.shape[0]
    d = table.shape[1]
    per_w = m // SC_WORKERS
    window = min(SC_GATHER_ROWS, per_w)

    assert per_w % (2 * window) == 0

    @pl.kernel(out_type=jax.ShapeDtypeStruct((m, d), table.dtype), mesh=_sc_mesh(),
               scratch_types=[pltpu.VMEM((window,), jnp.int32), pltpu.VMEM((window,), jnp.int32),
                              pltpu.VMEM((window, d), table.dtype), pltpu.VMEM((window, d), table.dtype),
                              pltpu.SemaphoreType.DMA, pltpu.SemaphoreType.DMA])
    def gather(t_hbm, i_hbm, o_hbm, ia_v, ib_v, ra_v, rb_v, sem_a, sem_b):
        wid = lax.axis_index("subcore") * SC_CORES + lax.axis_index("core")

        @pl.loop(0, per_w // (2 * window))
        def _(j):
            base_a = wid * per_w + j * (2 * window)
            base_b = base_a + window
            idx_a = pltpu.async_copy(i_hbm.at[pl.ds(base_a, window)], ia_v, sem_a)
            idx_b = pltpu.async_copy(i_hbm.at[pl.ds(base_b, window)], ib_v, sem_b)
            idx_a.wait()
            get_a = pltpu.async_copy(t_hbm.at[ia_v], ra_v, sem_a)
            idx_b.wait()
            get_b = pltpu.async_copy(t_hbm.at[ib_v], rb_v, sem_b)
            get_a.wait()
            put_a = pltpu.async_copy(ra_v, o_hbm.at[pl.ds(base_a, window)], sem_a)
            get_b.wait()
            put_b = pltpu.async_copy(rb_v, o_hbm.at[pl.ds(base_b, window)], sem_b)
            put_a.wait()
            put_b.wait()

    return gather(table, idx)


def _expert_tiles_kernel(start_ref, ntile_ref, x_hbm, wg_ref, wu_ref, wd_ref, o_hbm,
                         wg_s, wu_s, wd_s, x_buf, o_buf, in_sem, out_sem):
    e = pl.program_id(0)
    tm = x_buf.shape[1]
    nslot = x_buf.shape[0]
    first = start_ref[e] // tm
    ntile = ntile_ref[e]
    total = start_ref[N_EXPERTS - 1] // tm + ntile_ref[N_EXPERTS - 1]
    wg_s[...] = wg_ref[...].astype(BF16)
    wu_s[...] = wu_ref[...].astype(BF16)
    wd_s[...] = wd_ref[...].astype(BF16)

    def rows_of(g):
        return pl.ds(pl.multiple_of(g * tm, tm), tm)

    def fetch(g):
        slot = g % nslot
        return pltpu.make_async_copy(x_hbm.at[rows_of(g)], x_buf.at[slot], in_sem.at[slot])

    def flush(g):
        slot = g % nslot
        return pltpu.make_async_copy(o_buf.at[slot], o_hbm.at[rows_of(g)], out_sem.at[slot])

    @pl.when(e == 0)
    def _():
        for k in range(nslot - 1):
            @pl.when(k < total)
            def _(k=k):
                fetch(k).start()

    def tile(g, carry):
        @pl.when(g + nslot - 1 < total)
        def _():
            fetch(g + nslot - 1).start()

        fetch(g).wait()

        @pl.when(g >= nslot)
        def _():
            flush(g - nslot).wait()

        slot = g % nslot
        o_buf[slot] = _pack_halves(_expert_mlp(x_buf[slot], wg_s, wu_s, wd_s).astype(BF16))
        flush(g).start()
        return carry

    lax.fori_loop(first, first + ntile, tile, 0)

    @pl.when(e == N_EXPERTS - 1)
    def _():
        for k in range(nslot, 0, -1):
            @pl.when(total >= k)
            def _(k=k):
                flush(total - k).wait()


def _expert_tiles(starts, ntiles, xs, w_g, w_u, w_d):
    rows = xs.shape[0]
    tm = EXPERT_ROW_TILE
    weight = lambda shape: pl.BlockSpec((None,) + shape, lambda e, st, nt: (e, 0, 0))
    grid_spec = pltpu.PrefetchScalarGridSpec(
        num_scalar_prefetch=2,
        grid=(N_EXPERTS,),
        in_specs=[pl.BlockSpec(memory_space=pl.ANY),
                  weight((D_MODEL, D_EXPERT)), weight((D_MODEL, D_EXPERT)), weight((D_EXPERT, D_MODEL))],
        out_specs=pl.BlockSpec(memory_space=pl.ANY),
        scratch_shapes=[pltpu.VMEM((D_MODEL, D_EXPERT), BF16), pltpu.VMEM((D_MODEL, D_EXPERT), BF16),
                        pltpu.VMEM((D_EXPERT, D_MODEL), BF16),
                        pltpu.VMEM((EXPERT_SLOTS, tm, HALF), jnp.int32), pltpu.VMEM((EXPERT_SLOTS, tm, HALF), jnp.int32),
                        pltpu.SemaphoreType.DMA((EXPERT_SLOTS,)), pltpu.SemaphoreType.DMA((EXPERT_SLOTS,))],
    )
    return pl.pallas_call(
        _expert_tiles_kernel,
        grid_spec=grid_spec,
        out_shape=jax.ShapeDtypeStruct((rows, HALF), jnp.int32),
        compiler_params=_cparams("arbitrary"),
        name="expert_tiles",
    )(starts, ntiles, xs, w_g, w_u, w_d)


def _unpack_f32(p):
    return pltpu.bitcast(p & jnp.int32(-65536), F32), pltpu.bitcast(lax.shift_left(p, jnp.int32(16)), F32)


def _combine_kernel(h_ref, o1_ref, o2_ref, rt_ref, gf_ref, y_ref):
    w1, w2 = rt_ref[:, 2:3], rt_ref[:, 3:4]
    a_lo, a_hi = _unpack_f32(o1_ref[...])
    b_lo, b_hi = _unpack_f32(o2_ref[...])
    y_lo = h_ref[:, 0:HALF] + (w1 * a_lo + w2 * b_lo)
    y_hi = h_ref[:, HALF:D_MODEL] + (w1 * a_hi + w2 * b_hi)
    ms = (jnp.sum(y_lo * y_lo, axis=-1, keepdims=True) + jnp.sum(y_hi * y_hi, axis=-1, keepdims=True)) / D_MODEL
    inv = lax.rsqrt(ms + EPS)
    y_ref[:, 0:HALF] = (y_lo * inv) * gf_ref[:, 0:HALF]
    y_ref[:, HALF:D_MODEL] = (y_hi * inv) * gf_ref[:, HALF:D_MODEL]


def _combine(h, o12, route_t, gf, tile):
    n = h.shape[0]
    tile = min(tile, n)
    nt = n // tile
    return pl.pallas_call(
        _combine_kernel,
        grid=(nt,),
        in_specs=[pl.BlockSpec((tile, D_MODEL), lambda i: (i, 0)),
                  pl.BlockSpec((tile, HALF), lambda i: (i, 0)),
                  pl.BlockSpec((tile, HALF), lambda i: (i + nt, 0)),
                  pl.BlockSpec((tile, ROUTE_ROWS), lambda i: (i, 0)),
                  pl.BlockSpec((1, D_MODEL), lambda i: (0, 0))],
        out_specs=pl.BlockSpec((tile, D_MODEL), lambda i: (i, 0)),
        out_shape=jax.ShapeDtypeStruct((n, D_MODEL), F32),
        compiler_params=_cparams("parallel"),
        name="combine_final_norm",
    )(h, o12, o12, route_t, gf)


def _sparse_moe(xn2p, route, h, w_g, w_u, w_d, gf, run_before_experts):
    n = h.shape[0]
    tm = EXPERT_ROW_TILE
    rows = 2 * n + N_EXPERTS * tm
    rank, cnt = _route_rank(route, ROWS_ROUTE_RANK)
    counts = cnt[:, 0]
    padded = (counts + tm - 1) // tm * tm
    e_idx = jnp.arange(N_EXPERTS, dtype=jnp.int32)
    starts = jnp.sum(jnp.where(e_idx[None, :] < e_idx[:, None], padded[None, :], 0), axis=1)
    ids = route[0:2].astype(jnp.int32)
    start_of = jnp.sum(jnp.where(ids[None] == e_idx[:, None, None], starts[:, None, None], 0), axis=0)
    pos = (start_of + rank[0:2]).reshape(2 * n)
    xs = _sc_scatter_pairs(xn2p, pos, rows)
    xs, _ = lax.optimization_barrier((xs, run_before_experts))
    os_ = _expert_tiles(starts.astype(jnp.int32), (padded // tm).astype(jnp.int32), xs, w_g, w_u, w_d)
    o12 = _sc_gather_rows(os_, pos)
    return _combine(h, o12, route.T, gf, ROWS_COMBINE)


def _prep_in_weights(w_in):
    o = 0
    w_u = w_in[:, o:o + SSM_WIDTH]; o += SSM_WIDTH
    w_q = w_in[:, o:o + SWA_WIDTH]; o += SWA_WIDTH
    w_k = w_in[:, o:o + SWA_KV_WIDTH]; o += SWA_KV_WIDTH
    w_v = w_in[:, o:o + SWA_KV_WIDTH]; o += SWA_KV_WIDTH
    w_qm = w_in[:, o:o + MEM_WIDTH]; o += MEM_WIDTH
    assert o == GATES_COL0
    wq = (w_q * (SWA_HEAD_DIM ** -0.5)).reshape(D_MODEL, SWA_KV_HEADS, SWA_REP, SWA_HEAD_DIM)
    wq = wq.transpose(0, 2, 1, 3).reshape(D_MODEL, SWA_WIDTH)
    w_main = jnp.concatenate([w_u, wq, w_k, w_v, w_qm], axis=1).astype(BF16)
    return w_main, w_in.astype(BF16)


IN_SPLITS = (SSM_WIDTH, SWA_WIDTH, SWA_KV_WIDTH, SWA_KV_WIDTH, MEM_WIDTH)
IN_DTYPES = ((F32, BF16), (BF16,), (F32,), (F32,), (BF16,))


def kernel(x_prompt, x_sample, cache_swa_k, cache_swa_v, state_ssm_re, state_ssm_im, cache_mem_k, cache_mem_v, mem_prompt, norm1_g, w_in, lam_re, lam_im, log_dt, bm_re, bm_im, cm_re, cm_im, d_skip, w_glu, b_glu, sinks, rel_table, mem_norm_g, w_mem_kv, w_br_ssm, w_br_swa, w_br_mem, w_out, norm2_g, w_rg, b_rg, w_rexp, b_rexp, w_e_gate, w_e_up, w_e_down, final_norm_g):
    nb, t, _ = x_prompt.shape
    ns, ts, _ = x_sample.shape
    l = 0
    L = S5_CHUNK

    w_main, w_gates = _prep_in_weights(w_in[l])
    w_swa = (w_br_swa[l].reshape(SWA_KV_HEADS, SWA_REP, SWA_HEAD_DIM, D_MODEL).transpose(1, 0, 2, 3)
             .reshape(SWA_WIDTH, D_MODEL))
    pad_rows = ROUTER_ROWS - N_EXPERTS - N_EXPERT_GROUPS
    w_router = jnp.concatenate([w_rexp[l].T, w_rg[l].T, jnp.zeros((pad_rows, D_MODEL), F32)], axis=0).astype(BF16)
    b_router = jnp.concatenate([b_rexp[l], b_rg[l], jnp.zeros((pad_rows,), F32)]).reshape(ROUTER_ROWS, 1)
    mp = {
        'g1': norm1_g[l].reshape(1, D_MODEL), 'w_gates': w_gates, 'd_skip': d_skip[l].reshape(1, SSM_WIDTH),
        'w_glu': w_glu[l].astype(BF16), 'b_glu': b_glu[l].reshape(1, SSM_WIDTH),
        'w_br_ssm': w_br_ssm[l].astype(BF16), 'w_br_swa': w_swa.astype(BF16),
        'w_br_mem': w_br_mem[l].astype(BF16), 'w_out': w_out[l].astype(BF16),
        'g2': norm2_g[l].reshape(1, D_MODEL), 'w_router': w_router, 'b_router': b_router,
    }
    w_g, w_u, w_d = w_e_gate[l], w_e_up[l], w_e_down[l]
    gf = final_norm_g.reshape(1, D_MODEL)
    s5_w = _s5_weights(lam_re[l], lam_im[l], log_dt[l], bm_re[l], bm_im[l], cm_re[l], cm_im[l], L)

    bias_p = _rel_bias(rel_table, np.arange(WINDOW)[:, None] + WINDOW - np.arange(2 * WINDOW)[None, :])
    keys_s = WINDOW + 2 * ts
    bias_s = _rel_bias(rel_table, np.arange(ts)[:, None] + WINDOW - np.arange(keys_s)[None, :])
    bias_s = bias_s.reshape(SWA_HEADS * ts, keys_s)
    sink_rows = jnp.repeat(sinks[l].astype(F32), ts).reshape(SWA_HEADS * ts, 1)

    n = nb * t
    xp = x_prompt.reshape(n, D_MODEL)
    mk, mv = _norm_proj(mem_prompt.reshape(nb * MEM_TOKENS, D_MODEL), mem_norm_g[l].reshape(1, D_MODEL),
                        w_mem_kv[l].astype(BF16), (MEM_WIDTH, MEM_WIDTH), ((F32,), (F32,)), ROWS_MEM_PROJ)
    u, ub, qz, k, v, qm = _norm_proj(xp, mp['g1'], w_main, IN_SPLITS, IN_DTYPES, ROWS_NORM_PROJ)

    y_ssm, fin = _s5(ub, jnp.zeros((nb, N_CH_TILES * 2 * STATE_TILE), F32), s5_w, nb, t // L, L, S5_CHUNKS_PER_STEP)
    p_re, p_im = _tiles_to_state(fin)

    o_swa = _swa_prompt(qz, k, v, bias_p, sinks[l].astype(F32), nb, t, SWA_BLOCKS_PER_STEP)
    o_mem = _mem_prompt(qm, mk, mv, nb, t, ROWS_MEM_ATTN)
    h, xn2p, route = _merge(xp, u, y_ssm, o_swa, o_mem, mp, ROWS_MERGE)

    k4 = k.reshape(nb, t, SWA_KV_HEADS, SWA_HEAD_DIM)
    v4 = v.reshape(nb, t, SWA_KV_HEADS, SWA_HEAD_DIM)
    new_k_p, new_v_p = k4[:, -WINDOW:][None], v4[:, -WINDOW:][None]
    new_mk = mk.reshape(1, nb, MEM_TOKENS, MEM_HEADS, MEM_HEAD_DIM)
    new_mv = mv.reshape(1, nb, MEM_TOKENS, MEM_HEADS, MEM_HEAD_DIM)

    m = ns * ts
    xs = x_sample.reshape(m, D_MODEL)
    us, ubs, qzs, k_s, v_s, qms = _norm_proj(xs, mp['g1'], w_main, IN_SPLITS, IN_DTYPES, ROWS_NORM_PROJ)
    ys_ssm, fins = _s5(ubs, _state_to_tiles(state_ssm_re[l], state_ssm_im[l]), s5_w, ns, ts // L, L, S5_CHUNKS_PER_STEP)
    s_re, s_im = _tiles_to_state(fins)

    kk_all = jnp.concatenate([cache_swa_k[l].reshape(ns, WINDOW, SWA_KV_WIDTH).astype(F32),
                              k_s.reshape(ns, ts, SWA_KV_WIDTH)], axis=1)
    vv_all = jnp.concatenate([cache_swa_v[l].reshape(ns, WINDOW, SWA_KV_WIDTH).astype(F32),
                              v_s.reshape(ns, ts, SWA_KV_WIDTH)], axis=1)
    pad = jnp.zeros((ns, keys_s - WINDOW - ts, SWA_KV_WIDTH), F32)
    o_dec, roll_k, roll_v = _swa_decode(qzs.astype(F32).reshape(ns, ts, SWA_WIDTH),
                                        jnp.concatenate([kk_all, pad], axis=1),
                                        jnp.concatenate([vv_all, pad], axis=1), bias_s, sink_rows, DECODE_SEQS_PER_STEP)
    o_swa_s = o_dec.reshape(m, SWA_WIDTH).astype(BF16)

    o_mem_s = _mem_decode(qms.astype(F32).reshape(ns, ts, MEM_WIDTH), cache_mem_k, cache_mem_v, l, DECODE_SEQS_PER_STEP)
    o_mem_s = o_mem_s.reshape(m, MEM_WIDTH).astype(BF16)

    y_prompt = _sparse_moe(xn2p, route, h, w_g, w_u, w_d, gf, (ys_ssm, o_swa_s, o_mem_s)).reshape(nb, t, D_MODEL)
    hs_, xn2ps, routes = _merge(xs, us, ys_ssm, o_swa_s, o_mem_s, mp, ROWS_MERGE)
    y_sample = _moe(xn2ps, routes.T, w_g, w_u, w_d, hs_, gf, ROWS_DENSE_MOE).reshape(ns, ts, D_MODEL)

    new_k_s = roll_k.reshape(1, ns, WINDOW, SWA_KV_HEADS, SWA_HEAD_DIM).astype(cache_swa_k.dtype)
    new_v_s = roll_v.reshape(1, ns, WINDOW, SWA_KV_HEADS, SWA_HEAD_DIM).astype(cache_swa_v.dtype)

    return (y_prompt, y_sample,
            new_k_p, new_v_p, p_re[None], p_im[None], new_mk, new_mv,
            new_k_s, new_v_s, s_re[None].astype(state_ssm_re.dtype), s_im[None].astype(state_ssm_im.dtype))
```

```python
import functools
import math

import numpy as np
import jax
import jax.numpy as jnp
from jax import lax
from jax.experimental import pallas as pl
from jax.experimental.pallas import tpu as pltpu
from jax.experimental.pallas import tpu_sc as plsc

F32 = jnp.float32
BF16 = jnp.bfloat16

D_MODEL = 1024
SSM_WIDTH = 512
SSM_GROUP = 16
SSM_GROUPS = 32
SSM_STATE = 64
SWA_HEADS = 8
SWA_KV_HEADS = 2
SWA_REP = 4
SWA_HEAD_DIM = 64
SWA_WIDTH = 512
SWA_KV_WIDTH = 128
WINDOW = 128
REL_BUCKETS = 32
REL_MAX_DIST = 128
MEM_TOKENS = 256
MEM_HEADS = 4
MEM_HEAD_DIM = 128
MEM_WIDTH = 512
N_EXPERT_GROUPS = 4
EXPERTS_PER_GROUP = 8
N_EXPERTS = 32
D_EXPERT = 256
EPS = 1e-6
NEG_INF = -1e30

LANES = 128
GROUPS_PER_TILE = LANES // SSM_GROUP
N_CH_TILES = SSM_WIDTH // LANES
STATE_TILE = GROUPS_PER_TILE * SSM_STATE
VMEM_LIMIT = 56 * 1024 * 1024
ROWS_NORM_PROJ = 1024
ROWS_MEM_PROJ = 512
ROWS_MEM_ATTN = 1024
ROWS_MERGE = 512
ROWS_ROUTE_RANK = 1024
ROWS_COMBINE = 512
ROWS_DENSE_MOE = 1024
S5_CHUNKS_PER_STEP = 128
SWA_BLOCKS_PER_STEP = 4
DECODE_SEQS_PER_STEP = 8
S5_CHUNK = 8
S5_PANEL = 256

_TRANS_B = (((1,), (1,)), ((), ()))


def _cparams(*sem):
    return pltpu.CompilerParams(dimension_semantics=sem, vmem_limit_bytes=VMEM_LIMIT)


def _rms(x, g):
    return (x * lax.rsqrt(jnp.mean(x * x, axis=-1, keepdims=True) + EPS)) * g


def _dot(a, b):
    return jnp.dot(a, b, preferred_element_type=F32)


def _norm_proj_kernel(x_ref, g_ref, w_ref, *out_refs, splits, dtypes):
    xb = _rms(x_ref[...], g_ref[...]).astype(BF16)
    off = 0
    outs = iter(out_refs)
    for width, dts in zip(splits, dtypes):
        r = _dot(xb, w_ref[:, off:off + width])
        for dt in dts:
            next(outs)[...] = r.astype(dt)
        off += width


def _norm_proj(x, g, w, splits, dtypes, tile):
    n, d = x.shape
    tile = min(tile, n)
    flat = [(wd, dt) for wd, dts in zip(splits, dtypes) for dt in dts]
    return pl.pallas_call(
        functools.partial(_norm_proj_kernel, splits=tuple(splits), dtypes=tuple(dtypes)),
        grid=(n // tile,),
        in_specs=[pl.BlockSpec((tile, d), lambda i: (i, 0)),
                  pl.BlockSpec((1, d), lambda i: (0, 0)),
                  pl.BlockSpec((d, sum(splits)), lambda i: (0, 0), pipeline_mode=pl.Buffered(1))],
        out_specs=[pl.BlockSpec((tile, wd), lambda i: (i, 0)) for wd, _ in flat],
        out_shape=[jax.ShapeDtypeStruct((n, wd), dt) for wd, dt in flat],
        compiler_params=_cparams("parallel"),
        name="norm_proj",
    )(x, g, w)


def _s5_weights(lam_re, lam_im, log_dt, bm_re, bm_im, cm_re, cm_im, L):
    nt, gt, P, H = N_CH_TILES, GROUPS_PER_TILE, SSM_STATE, SSM_GROUP
    lr, li = lam_re.astype(F32), lam_im.astype(F32)
    dt = jnp.exp(log_dt.astype(F32))[:, None]
    mag = jnp.exp(lr * dt)
    a_re = mag * jnp.cos(li * dt)
    a_im = mag * jnp.sin(li * dt)
    den = lr * lr + li * li
    f_re = ((a_re - 1.0) * lr + a_im * li) / den
    f_im = (a_im * lr - (a_re - 1.0) * li) / den
    br, bi = bm_re.astype(F32), bm_im.astype(F32)
    bb_re = f_re[..., None] * br - f_im[..., None] * bi
    bb_im = f_re[..., None] * bi + f_im[..., None] * br
    pr, pi = [jnp.ones_like(a_re)], [jnp.zeros_like(a_im)]
    for _ in range(L):
        pr.append(pr[-1] * a_re - pi[-1] * a_im)
        pi.append(pr[-2] * a_im + pi[-1] * a_re)
    ap_re, ap_im = jnp.stack(pr), jnp.stack(pi)
    cr, ci = cm_re.astype(F32), cm_im.astype(F32)
    ca_re = cr[None] * ap_re[:, :, None, :] - ci[None] * ap_im[:, :, None, :]
    ca_im = cr[None] * ap_im[:, :, None, :] + ci[None] * ap_re[:, :, None, :]

    rev_re = jnp.stack([pr[L - 1 - s] for s in range(L)])
    rev_im = jnp.stack([pi[L - 1 - s] for s in range(L)])
    ws_re = rev_re[..., None] * bb_re[None] - rev_im[..., None] * bb_im[None]
    ws_im = rev_re[..., None] * bb_im[None] + rev_im[..., None] * bb_re[None]
    c_st = jnp.concatenate([ws_re.transpose(0, 1, 3, 2).reshape(L, nt, gt * H, P),
                            ws_im.transpose(0, 1, 3, 2).reshape(L, nt, gt * H, P)], axis=3).transpose(1, 0, 2, 3)
    so = lambda ca: ca[1:].transpose(1, 3, 0, 2).reshape(nt, gt * P, L * H)
    c_so = jnp.concatenate([so(ca_re), so(-ca_im)], axis=1)
    prod = (ca_re[:L][:, :, None, :, :] * bb_re.transpose(0, 2, 1)[None, :, :, None, :]
            - ca_im[:L][:, :, None, :, :] * bb_im.transpose(0, 2, 1)[None, :, :, None, :])
    k_lag = jnp.sum(prod, axis=-1).transpose(1, 2, 0, 3)
    c_k = k_lag.reshape(nt, gt * H, L * H)
    w_st, w_out, toep = _s5_expand(c_st, c_so, c_k, L)

    def per_tile(v):
        return v.reshape(nt, 1, STATE_TILE)

    return w_st, w_out, toep, per_tile(pr[L]), per_tile(pi[L])


def _s5_expand_kernel(cst_ref, cso_ref, ck_ref, wst_ref, wso_ref, toep_ref, *, L):
    hp = lax.Precision.HIGHEST
    P, H = SSM_STATE, SSM_GROUP
    iota = lambda shape, d: lax.broadcasted_iota(jnp.int32, shape, d)
    one = lambda cond: jnp.where(cond, 1.0, 0.0).astype(F32)

    r, c = iota((2 * P, 2 * STATE_TILE), 0), iota((2 * P, 2 * STATE_TILE), 1)
    rep_st = one((r // P == c // STATE_TILE) & (r % P == c % P))
    r, c = iota((LANES, 2 * STATE_TILE), 0), iota((LANES, 2 * STATE_TILE), 1)
    own_st = one(r // H == (c % STATE_TILE) // P)
    for s in range(L):
        blk = jnp.dot(cst_ref[s], rep_st, precision=hp, preferred_element_type=F32) * own_st
        wst_ref[s * LANES:(s + 1) * LANES, :] = blk.astype(BF16)

    r, c = iota((LANES, LANES), 0), iota((LANES, LANES), 1)
    pick = [one((r // H == t) & (r % H == c % H)) for t in range(L)]
    own_k = one(r // H == c // H)
    r, c = iota((2 * STATE_TILE, LANES), 0), iota((2 * STATE_TILE, LANES), 1)
    own_so = one((r % STATE_TILE) // P == c // H)
    cso = cso_ref[...]
    for t in range(L):
        blk = jnp.dot(cso, pick[t], precision=hp, preferred_element_type=F32) * own_so
        wso_ref[:, t * LANES:(t + 1) * LANES] = blk.astype(BF16)
    ck = ck_ref[...]
    lag = [(jnp.dot(ck, pick[t], precision=hp, preferred_element_type=F32) * own_k).astype(BF16) for t in range(L)]
    zero = jnp.zeros((LANES, LANES), BF16)
    for s in range(L):
        for t in range(L):
            toep_ref[s * LANES:(s + 1) * LANES, t * LANES:(t + 1) * LANES] = lag[t - s] if t >= s else zero


def _s5_expand(c_st, c_so, c_k, L):
    lk = L * LANES
    st2 = 2 * STATE_TILE
    return pl.pallas_call(
        functools.partial(_s5_expand_kernel, L=L),
        grid=(N_CH_TILES,),
        in_specs=[pl.BlockSpec((None, L, LANES, 2 * SSM_STATE), lambda j: (j, 0, 0, 0)),
                  pl.BlockSpec((None, st2, L * SSM_GROUP), lambda j: (j, 0, 0)),
                  pl.BlockSpec((None, LANES, L * SSM_GROUP), lambda j: (j, 0, 0))],
        out_specs=[pl.BlockSpec((None, lk, st2), lambda j: (j, 0, 0)),
                   pl.BlockSpec((None, st2, lk), lambda j: (j, 0, 0)),
                   pl.BlockSpec((None, lk, lk), lambda j: (j, 0, 0))],
        out_shape=[jax.ShapeDtypeStruct((N_CH_TILES, lk, st2), BF16),
                   jax.ShapeDtypeStruct((N_CH_TILES, st2, lk), BF16),
                   jax.ShapeDtypeStruct((N_CH_TILES, lk, lk), BF16)],
        compiler_params=_cparams("parallel"),
        name="s5_expand_weights",
    )(c_st, c_so, c_k)


def _to_chunks(u, nb, nc, L):
    return (u.reshape(nb, nc, L, N_CH_TILES, LANES).transpose(1, 0, 3, 2, 4)
            .reshape(nc * nb, N_CH_TILES * L * LANES))


def _from_chunks(y, nb, nc, L):
    return (y.reshape(nc, nb, N_CH_TILES, L, LANES).transpose(1, 0, 3, 2, 4)
            .reshape(nb * nc * L, SSM_WIDTH))


def _s5_kernel(x_ref, h0_ref, are_ref, aim_ref, ws_ref, t_ref, wo_ref, y_ref, fin_ref,
               hr_ref, hi_ref, d_ref, hs_ref, *, cb, nb):
    ci = pl.program_id(1)

    @pl.when(ci == 0)
    def _():
        hr_ref[...] = h0_ref[:, 0:STATE_TILE]
        hi_ref[...] = h0_ref[:, STATE_TILE:2 * STATE_TILE]

    x = x_ref[...]
    d_ref[...] = _dot(x, ws_ref[...])
    ar = jnp.broadcast_to(are_ref[...], (nb, STATE_TILE))
    ai = jnp.broadcast_to(aim_ref[...], (nb, STATE_TILE))

    def body(c, carry):
        hr, hi = carry
        r0 = pl.multiple_of(c * nb, nb)
        hs_ref[pl.ds(r0, nb), 0:STATE_TILE] = hr
        hs_ref[pl.ds(r0, nb), STATE_TILE:2 * STATE_TILE] = hi
        d = d_ref[pl.ds(r0, nb), :]
        return (ar * hr - ai * hi + d[:, 0:STATE_TILE],
                ar * hi + ai * hr + d[:, STATE_TILE:2 * STATE_TILE])

    hr, hi = lax.fori_loop(0, cb, body, (hr_ref[...], hi_ref[...]))
    hr_ref[...] = hr
    hi_ref[...] = hi
    hsb = hs_ref[...].astype(BF16)
    for c0 in range(0, t_ref.shape[1], S5_PANEL):
        c1 = c0 + S5_PANEL
        y_ref[:, c0:c1] = _dot(x[:, 0:c1], t_ref[0:c1, c0:c1]) + _dot(hsb, wo_ref[:, c0:c1])

    @pl.when(ci == pl.num_programs(1) - 1)
    def _():
        fin_ref[:, 0:STATE_TILE] = hr
        fin_ref[:, STATE_TILE:2 * STATE_TILE] = hi


def _s5(ub, h0, weights, nb, nc, L, chunk_block):
    w_st, w_so, toep, a_re, a_im = weights
    xc = _to_chunks(ub, nb, nc, L)
    cb = min(chunk_block, nc)
    rows = cb * nb
    lk = L * LANES
    st2 = 2 * STATE_TILE
    tile_w = lambda shape: pl.BlockSpec((None,) + shape, lambda j, c: (j, 0, 0))
    y, fin = pl.pallas_call(
        functools.partial(_s5_kernel, cb=cb, nb=nb),
        grid=(N_CH_TILES, nc // cb),
        in_specs=[pl.BlockSpec((rows, lk), lambda j, c: (c, j)),
                  pl.BlockSpec((nb, st2), lambda j, c: (0, j)),
                  tile_w((1, STATE_TILE)), tile_w((1, STATE_TILE)),
                  tile_w((lk, st2)), tile_w((lk, lk)), tile_w((st2, lk))],
        out_specs=[pl.BlockSpec((rows, lk), lambda j, c: (c, j)),
                   pl.BlockSpec((nb, st2), lambda j, c: (0, j))],
        out_shape=[jax.ShapeDtypeStruct((nc * nb, N_CH_TILES * lk), F32),
                   jax.ShapeDtypeStruct((nb, N_CH_TILES * st2), F32)],
        scratch_shapes=[pltpu.VMEM((nb, STATE_TILE), F32), pltpu.VMEM((nb, STATE_TILE), F32),
                        pltpu.VMEM((rows, st2), F32), pltpu.VMEM((rows, st2), F32)],
        compiler_params=_cparams("parallel", "arbitrary"),
        name="s5_chunked_scan",
    )(xc, h0, a_re, a_im, w_st, toep, w_so)
    return _from_chunks(y, nb, nc, L), fin


def _state_to_tiles(h_re, h_im):
    nb = h_re.shape[0]
    r = h_re.astype(F32).reshape(nb, N_CH_TILES, STATE_TILE)
    i = h_im.astype(F32).reshape(nb, N_CH_TILES, STATE_TILE)
    return jnp.concatenate([r, i], axis=-1).reshape(nb, N_CH_TILES * 2 * STATE_TILE)


def _tiles_to_state(h):
    nb = h.shape[0]
    h = h.reshape(nb, N_CH_TILES, 2, GROUPS_PER_TILE, SSM_STATE)
    return (h[:, :, 0].reshape(nb, SSM_GROUPS, SSM_STATE), h[:, :, 1].reshape(nb, SSM_GROUPS, SSM_STATE))


def _t5_bucket(dist):
    n = np.maximum(dist, 0)
    max_exact = REL_BUCKETS // 2
    nf = np.maximum(n, 1).astype(np.float32)
    large = max_exact + (np.log(nf / np.float32(max_exact)) / np.float32(math.log(REL_MAX_DIST / max_exact))
                         * np.float32(REL_BUCKETS - max_exact)).astype(np.int32)
    large = np.minimum(large, REL_BUCKETS - 1)
    return np.where(n < max_exact, n, large)


def _rel_bias(rel_table, dist):
    bucket = _t5_bucket(dist)
    tab = rel_table.astype(F32)
    out = jnp.zeros((SWA_HEADS,) + dist.shape, F32)
    for b in range(REL_BUCKETS):
        sel = jnp.asarray(bucket == b)
        if bool((bucket == b).any()):
            out = jnp.where(sel[None], tab[b].reshape((SWA_HEADS,) + (1,) * dist.ndim), out)
    return out


def _swa_prompt_kernel(sink_ref, q_ref, kp_ref, kc_ref, vp_ref, vc_ref, bias_ref, o_ref, kk_ref, vv_ref, *, qblocks):
    step = pl.program_id(1)
    kk_ref[0:WINDOW, :] = kp_ref[...].astype(BF16)
    kk_ref[WINDOW:, :] = kc_ref[...].astype(BF16)
    vv_ref[0:WINDOW, :] = vp_ref[...].astype(BF16)
    vv_ref[WINDOW:, :] = vc_ref[...].astype(BF16)
    row = lax.broadcasted_iota(jnp.int32, (WINDOW, 2 * WINDOW), 0)
    col = lax.broadcasted_iota(jnp.int32, (WINDOW, 2 * WINDOW), 1)
    dist = row + WINDOW - col
    band = (dist >= 0) & (dist < WINDOW)
    lane = lax.broadcasted_iota(jnp.int32, (WINDOW, LANES), 1)
    low = lane < SWA_HEAD_DIM

    def block(j, carry):
        r0 = pl.multiple_of(j * WINDOW, WINDOW)
        kk = kk_ref[pl.ds(r0, 2 * WINDOW), :]
        vv = vv_ref[pl.ds(r0, 2 * WINDOW), :]
        valid = band & ((col >= WINDOW) | (step * qblocks + j > 0))
        for t in range(SWA_REP):
            q2 = q_ref[pl.ds(r0, WINDOW), t * LANES:(t + 1) * LANES]
            outs = []
            for half in range(SWA_KV_HEADS):
                h = t + SWA_REP * half
                qh = jnp.where(low if half == 0 else jnp.logical_not(low), q2, jnp.zeros_like(q2))
                s = lax.dot_general(qh, kk, _TRANS_B, preferred_element_type=F32)
                s = jnp.where(valid, s + bias_ref[h], NEG_INF)
                sink = sink_ref[h]
                m = jnp.maximum(jnp.max(s, axis=-1, keepdims=True), sink)
                e = jnp.exp(s - m)
                den = jnp.sum(e, axis=-1, keepdims=True) + jnp.exp(sink - m)
                outs.append(_dot(e.astype(BF16), vv) * (1.0 / den))
            o_ref[pl.ds(r0, WINDOW), t * LANES:(t + 1) * LANES] = jnp.where(low, outs[0], outs[1]).astype(BF16)
        return carry

    lax.fori_loop(0, qblocks, block, 0)


def _swa_prompt(q, k, v, bias, sinks, nb, t, qblocks):
    nstep = t // (WINDOW * qblocks)
    rows = WINDOW * qblocks
    cur = lambda b, i: (b * nstep + i, 0)
    prev = lambda b, i: (b * nstep * qblocks + jnp.maximum(i * qblocks - 1, 0), 0)
    return pl.pallas_call(
        functools.partial(_swa_prompt_kernel, qblocks=qblocks),
        grid=(nb, nstep),
        in_specs=[pl.BlockSpec(memory_space=pltpu.SMEM),
                  pl.BlockSpec((rows, SWA_WIDTH), cur),
                  pl.BlockSpec((WINDOW, SWA_KV_WIDTH), prev),
                  pl.BlockSpec((rows, SWA_KV_WIDTH), cur),
                  pl.BlockSpec((WINDOW, SWA_KV_WIDTH), prev),
                  pl.BlockSpec((rows, SWA_KV_WIDTH), cur),
                  pl.BlockSpec((SWA_HEADS, WINDOW, 2 * WINDOW), lambda b, i: (0, 0, 0))],
        out_specs=pl.BlockSpec((rows, SWA_WIDTH), cur),
        out_shape=jax.ShapeDtypeStruct((nb * t, SWA_WIDTH), BF16),
        scratch_shapes=[pltpu.VMEM((rows + WINDOW, SWA_KV_WIDTH), BF16),
                        pltpu.VMEM((rows + WINDOW, SWA_KV_WIDTH), BF16)],
        compiler_params=_cparams("parallel", "parallel"),
        name="swa_prompt",
    )(sinks, q, k, k, v, v, bias)


def _swa_decode_kernel(q_ref, k_ref, v_ref, bias_ref, sink_ref, o_ref, nk_ref, nv_ref, *, seqs, tq):
    rows, keys = SWA_HEADS * tq, k_ref.shape[1]
    nk_ref[...] = k_ref[:, tq:tq + WINDOW, :]
    nv_ref[...] = v_ref[:, tq:tq + WINDOW, :]
    low = lax.broadcasted_iota(jnp.int32, (tq, LANES), 1) < SWA_HEAD_DIM
    qi = lax.broadcasted_iota(jnp.int32, (rows, keys), 0) % tq
    col = lax.broadcasted_iota(jnp.int32, (rows, keys), 1)
    dist = qi + WINDOW - col
    valid = (dist >= 0) & (dist < WINDOW)
    bias = bias_ref[...]
    sink = sink_ref[...]
    for s_i in range(seqs):
        q = q_ref[s_i]
        tiles = [q[:, t * LANES:(t + 1) * LANES] for t in range(SWA_REP)]
        qh = jnp.concatenate([jnp.where(low, x, 0.0) for x in tiles]
                             + [jnp.where(low, 0.0, x) for x in tiles], axis=0)
        kk = k_ref[s_i].astype(BF16)
        s = lax.dot_general(qh.astype(BF16), kk, _TRANS_B, preferred_element_type=F32)
        s = jnp.where(valid, s + bias, NEG_INF)
        m = jnp.maximum(jnp.max(s, axis=-1, keepdims=True), sink)
        e = jnp.exp(s - m)
        den = jnp.sum(e, axis=-1, keepdims=True) + jnp.exp(sink - m)
        o = _dot(e.astype(BF16), v_ref[s_i].astype(BF16)) * (1.0 / den)
        for t in range(SWA_REP):
            o_ref[s_i, :, t * LANES:(t + 1) * LANES] = jnp.where(
                low, o[t * tq:(t + 1) * tq], o[(t + SWA_REP) * tq:(t + SWA_REP + 1) * tq])


def _swa_decode(q, k_all, v_all, bias, sink_rows, seqs):
    nseq, tq, _ = q.shape
    rows = SWA_HEADS * tq
    keys = k_all.shape[1]
    seqs = min(seqs, nseq)
    return pl.pallas_call(
        functools.partial(_swa_decode_kernel, seqs=seqs, tq=tq),
        grid=(nseq // seqs,),
        in_specs=[pl.BlockSpec((seqs, tq, SWA_WIDTH), lambda i: (i, 0, 0)),
                  pl.BlockSpec((seqs, keys, LANES), lambda i: (i, 0, 0)),
                  pl.BlockSpec((seqs, keys, LANES), lambda i: (i, 0, 0)),
                  pl.BlockSpec((rows, keys), lambda i: (0, 0)),
                  pl.BlockSpec((rows, 1), lambda i: (0, 0))],
        out_specs=[pl.BlockSpec((seqs, tq, SWA_WIDTH), lambda i: (i, 0, 0)),
                   pl.BlockSpec((seqs, WINDOW, LANES), lambda i: (i, 0, 0)),
                   pl.BlockSpec((seqs, WINDOW, LANES), lambda i: (i, 0, 0))],
        out_shape=[jax.ShapeDtypeStruct((nseq, tq, SWA_WIDTH), F32),
                   jax.ShapeDtypeStruct((nseq, WINDOW, LANES), F32),
                   jax.ShapeDtypeStruct((nseq, WINDOW, LANES), F32)],
        compiler_params=_cparams("parallel"),
        name="swa_decode",
    )(q, k_all, v_all, bias, sink_rows)


def _softmax(s):
    m = jnp.max(s, axis=-1, keepdims=True)
    e = jnp.exp(s - m)
    return e * (1.0 / jnp.sum(e, axis=-1, keepdims=True))


def _mem_prompt_kernel(q_ref, k_ref, v_ref, o_ref, s_ref, p_ref):
    scale = MEM_HEAD_DIM ** -0.5
    heads = [slice(h * MEM_HEAD_DIM, (h + 1) * MEM_HEAD_DIM) for h in range(MEM_HEADS)]
    for h, sl in enumerate(heads):
        s_ref[h] = lax.dot_general(q_ref[:, sl], k_ref[:, sl].astype(BF16), _TRANS_B, preferred_element_type=F32)
    s = s_ref[...] * scale
    e = jnp.exp(s - jnp.max(s, axis=-1, keepdims=True))
    p_ref[...] = e.astype(BF16)
    inv = 1.0 / jnp.sum(e, axis=-1, keepdims=True)
    for h, sl in enumerate(heads):
        o_ref[:, sl] = (_dot(p_ref[h], v_ref[:, sl].astype(BF16)) * inv[h]).astype(BF16)


def _mem_prompt(qm, mk, mv, nb, t, tile):
    tile = min(tile, t)
    nt = t // tile
    return pl.pallas_call(
        _mem_prompt_kernel,
        grid=(nb, nt),
        in_specs=[pl.BlockSpec((tile, MEM_WIDTH), lambda b, i: (b * nt + i, 0)),
                  pl.BlockSpec((MEM_TOKENS, MEM_WIDTH), lambda b, i: (b, 0)),
                  pl.BlockSpec((MEM_TOKENS, MEM_WIDTH), lambda b, i: (b, 0))],
        out_specs=pl.BlockSpec((tile, MEM_WIDTH), lambda b, i: (b * nt + i, 0)),
        out_shape=jax.ShapeDtypeStruct((nb * t, MEM_WIDTH), BF16),
        scratch_shapes=[pltpu.VMEM((MEM_HEADS, tile, MEM_TOKENS), F32), pltpu.VMEM((MEM_HEADS, tile, MEM_TOKENS), BF16)],
        compiler_params=_cparams("parallel", "parallel"),
        name="mem_prompt",
    )(qm, mk, mv)


def _mem_decode_kernel(q_ref, k_ref, v_ref, o_ref, *, seqs):
    tq = q_ref.shape[1]
    rows, cols = MEM_HEADS * tq, MEM_TOKENS * MEM_HEADS
    k2 = k_ref.reshape(seqs, cols, MEM_HEAD_DIM)
    v2 = v_ref.reshape(seqs, cols, MEM_HEAD_DIM)
    scale = MEM_HEAD_DIM ** -0.5
    own = (lax.broadcasted_iota(jnp.int32, (rows, cols), 1) % MEM_HEADS
           == lax.broadcasted_iota(jnp.int32, (rows, cols), 0) // tq)
    for s_i in range(seqs):
        q = q_ref[s_i]
        qb = jnp.concatenate([q[:, h * MEM_HEAD_DIM:(h + 1) * MEM_HEAD_DIM] for h in range(MEM_HEADS)], axis=0)
        s = lax.dot_general(qb.astype(BF16), k2[s_i].astype(BF16), _TRANS_B, preferred_element_type=F32) * scale
        p = _softmax(jnp.where(own, s, NEG_INF)).astype(BF16)
        o = _dot(p, v2[s_i].astype(BF16))
        for h in range(MEM_HEADS):
            o_ref[s_i, :, h * MEM_HEAD_DIM:(h + 1) * MEM_HEAD_DIM] = o[h * tq:(h + 1) * tq, :]


def _mem_decode(q, k, v, layer, seqs):
    nseq, tq, _ = q.shape
    seqs = min(seqs, nseq)
    cache = pl.BlockSpec((None, seqs, MEM_TOKENS, MEM_HEADS, MEM_HEAD_DIM), lambda i: (layer, i, 0, 0, 0))
    return pl.pallas_call(
        functools.partial(_mem_decode_kernel, seqs=seqs),
        grid=(nseq // seqs,),
        in_specs=[pl.BlockSpec((seqs, tq, MEM_WIDTH), lambda i: (i, 0, 0)), cache, cache],
        out_specs=pl.BlockSpec((seqs, tq, MEM_WIDTH), lambda i: (i, 0, 0)),
        out_shape=jax.ShapeDtypeStruct((nseq, tq, MEM_WIDTH), F32),
        compiler_params=_cparams("parallel"),
        name="mem_decode",
    )(q, k, v)


ROUTER_ROWS = 40
GATES_COL0 = SSM_WIDTH + SWA_WIDTH + 2 * SWA_KV_WIDTH + MEM_WIDTH
ROUTE_ROWS = 8
HALF = D_MODEL // 2


def _pack_halves(xb):
    hi = pltpu.bitcast(xb[:, 0:HALF].astype(F32), jnp.int32)
    lo = pltpu.bitcast(xb[:, HALF:D_MODEL].astype(F32), jnp.int32)
    return hi | lax.shift_right_logical(lo, jnp.int32(16))


def _unpack_halves(p):
    hi = pltpu.bitcast(p & jnp.int32(-65536), F32).astype(BF16)
    lo = pltpu.bitcast(lax.shift_left(p, jnp.int32(16)), F32).astype(BF16)
    return hi, lo


def _merge_kernel(x_ref, u_ref, y_ref, os_ref, om_ref, g1_ref, wg_ref, dsk_ref, wglu_ref, bglu_ref,
                  wbs_ref, wbw_ref, wbm_ref, wout_ref, g2_ref, wr_ref, br_ref,
                  h_ref, xn2_ref, route_ref):
    x = x_ref[...]
    tt = x.shape[0]
    xb = _rms(x, g1_ref[...]).astype(BF16)
    z = jax.nn.gelu(y_ref[...] + dsk_ref[...] * u_ref[...])
    z = z * jax.nn.sigmoid(_dot(z.astype(BF16), wglu_ref[...]) + bglu_ref[...])
    gate = lambda b: jax.nn.sigmoid(_dot(xb, wg_ref[:, GATES_COL0 + b * D_MODEL:GATES_COL0 + (b + 1) * D_MODEL]))
    merged = gate(0) * _dot(z.astype(BF16), wbs_ref[...])
    merged = merged + gate(1) * _dot(os_ref[...], wbw_ref[...])
    merged = merged + gate(2) * _dot(om_ref[...], wbm_ref[...])
    h = x + _dot(merged.astype(BF16), wout_ref[...])
    h_ref[...] = h
    xn2 = _rms(h, g2_ref[...]).astype(BF16)
    xn2_ref[...] = _pack_halves(xn2)

    lt = lax.dot_general(wr_ref[...], xn2, _TRANS_B, preferred_element_type=F32) + br_ref[...]
    gl = lt[N_EXPERTS:N_EXPERTS + N_EXPERT_GROUPS]
    ge = jnp.exp(gl - jnp.max(gl, axis=0, keepdims=True))
    gp = ge / jnp.sum(ge, axis=0, keepdims=True)
    gw = jnp.max(gp, axis=0, keepdims=True)
    gidx = jnp.full((1, tt), N_EXPERT_GROUPS - 1, jnp.int32)
    for r in range(N_EXPERT_GROUPS - 2, -1, -1):
        gidx = jnp.where(gp[r:r + 1] == gw, r, gidx)
    ein = lt[(N_EXPERT_GROUPS - 1) * EXPERTS_PER_GROUP:N_EXPERTS]
    for r in range(N_EXPERT_GROUPS - 2, -1, -1):
        ein = jnp.where(gidx == r, lt[r * EXPERTS_PER_GROUP:(r + 1) * EXPERTS_PER_GROUP], ein)
    ee = jnp.exp(ein - jnp.max(ein, axis=0, keepdims=True))
    ep = ee / jnp.sum(ee, axis=0, keepdims=True)
    rowi = lax.broadcasted_iota(jnp.int32, (EXPERTS_PER_GROUP, tt), 0)
    p1 = jnp.max(ep, axis=0, keepdims=True)
    e1 = jnp.min(jnp.where(ep == p1, rowi, EXPERTS_PER_GROUP), axis=0, keepdims=True)
    ep2 = jnp.where(rowi == e1, -1.0, ep)
    p2 = jnp.max(ep2, axis=0, keepdims=True)
    e2 = jnp.min(jnp.where(ep2 == p2, rowi, EXPERTS_PER_GROUP), axis=0, keepdims=True)
    tot = p1 + p2
    w1 = p1 / tot * gw
    w2 = p2 / tot * gw
    id1 = (gidx * EXPERTS_PER_GROUP + e1).astype(F32)
    id2 = (gidx * EXPERTS_PER_GROUP + e2).astype(F32)
    route_ref[...] = jnp.concatenate([id1, id2, w1, w2, jnp.zeros((ROUTE_ROWS - 4, tt), F32)], axis=0)


def _merge(x, u, y, o_swa, o_mem, p, tile):
    n = x.shape[0]
    tile = min(tile, n)
    row = lambda i: (i, 0)
    const = lambda i: (0, 0)
    full = lambda a: pl.BlockSpec(a.shape, const, pipeline_mode=pl.Buffered(1))
    weights = [p['g1'], p['w_gates'], p['d_skip'], p['w_glu'], p['b_glu'], p['w_br_ssm'], p['w_br_swa'],
               p['w_br_mem'], p['w_out'], p['g2'], p['w_router'], p['b_router']]
    return pl.pallas_call(
        _merge_kernel,
        grid=(n // tile,),
        in_specs=[pl.BlockSpec((tile, D_MODEL), row), pl.BlockSpec((tile, SSM_WIDTH), row),
                  pl.BlockSpec((tile, SSM_WIDTH), row), pl.BlockSpec((tile, SWA_WIDTH), row),
                  pl.BlockSpec((tile, MEM_WIDTH), row)] + [full(w) for w in weights],
        out_specs=[pl.BlockSpec((tile, D_MODEL), row), pl.BlockSpec((tile, HALF), row),
                   pl.BlockSpec((ROUTE_ROWS, tile), lambda i: (0, i))],
        out_shape=[jax.ShapeDtypeStruct((n, D_MODEL), F32), jax.ShapeDtypeStruct((n, HALF), jnp.int32),
                   jax.ShapeDtypeStruct((ROUTE_ROWS, n), F32)],
        compiler_params=_cparams("parallel"),
        name="merge_router",
    )(x, u, y, o_swa, o_mem, *weights)


def _expert_mlp(xp, wg, wu, wd):
    hi, lo = _unpack_halves(xp)
    g = _dot(hi, wg[0:HALF, :]) + _dot(lo, wg[HALF:D_MODEL, :])
    u = _dot(hi, wu[0:HALF, :]) + _dot(lo, wu[HALF:D_MODEL, :])
    hh = jax.nn.silu(g) * u
    return _dot(hh.astype(BF16), wd[...])


def _moe_kernel(xn2_ref, rt_ref, wg_ref, wu_ref, wd_ref, h_ref, gf_ref, o_ref, acc_ref):
    e = pl.program_id(1)

    @pl.when(e == 0)
    def _():
        acc_ref[...] = jnp.zeros_like(acc_ref)

    o = _expert_mlp(xn2_ref[...], wg_ref[...].astype(BF16), wu_ref[...].astype(BF16), wd_ref[...].astype(BF16))
    ef = e.astype(F32)
    c = (jnp.where(rt_ref[:, 0:1] == ef, rt_ref[:, 2:3], 0.0)
         + jnp.where(rt_ref[:, 1:2] == ef, rt_ref[:, 3:4], 0.0))
    acc_ref[...] += c * o

    @pl.when(e == N_EXPERTS - 1)
    def _():
        o_ref[...] = _rms(h_ref[...] + acc_ref[...], gf_ref[...])


def _moe(xn2, route_t, w_g, w_u, w_d, h, gf, tile):
    n = h.shape[0]
    tile = min(tile, n)
    return pl.pallas_call(
        _moe_kernel,
        grid=(n // tile, N_EXPERTS),
        in_specs=[pl.BlockSpec((tile, HALF), lambda i, e: (i, 0)),
                  pl.BlockSpec((tile, ROUTE_ROWS), lambda i, e: (i, 0)),
                  pl.BlockSpec((None, D_MODEL, D_EXPERT), lambda i, e: (e, 0, 0)),
                  pl.BlockSpec((None, D_MODEL, D_EXPERT), lambda i, e: (e, 0, 0)),
                  pl.BlockSpec((None, D_EXPERT, D_MODEL), lambda i, e: (e, 0, 0)),
                  pl.BlockSpec((tile, D_MODEL), lambda i, e: (i, 0)),
                  pl.BlockSpec((1, D_MODEL), lambda i, e: (0, 0))],
        out_specs=pl.BlockSpec((tile, D_MODEL), lambda i, e: (i, 0)),
        out_shape=jax.ShapeDtypeStruct((n, D_MODEL), F32),
        scratch_shapes=[pltpu.VMEM((tile, D_MODEL), F32)],
        compiler_params=_cparams("parallel", "arbitrary"),
        name="moe_final_norm",
    )(xn2, route_t, w_g, w_u, w_d, h, gf)


EXPERT_ROW_TILE = 256
EXPERT_SLOTS = 4
SC_CORES = 2
SC_SUBCORES = 16
SC_WORKERS = SC_CORES * SC_SUBCORES
SC_SCATTER_ROWS = 64
SC_GATHER_ROWS = 64


def _route_rank_kernel(r_ref, rank_ref, cnt_ref, base_ref):
    i = pl.program_id(0)
    tt = r_ref.shape[1]

    @pl.when(i == 0)
    def _():
        base_ref[...] = jnp.zeros_like(base_ref)

    ids = r_ref[0:2, :].astype(jnp.int32)
    e_iota = lax.broadcasted_iota(jnp.int32, (N_EXPERTS, tt), 0)
    oh1 = jnp.where(e_iota == ids[0:1], 1.0, 0.0)
    oh2 = jnp.where(e_iota == ids[1:2], 1.0, 0.0)
    before = (lax.broadcasted_iota(jnp.int32, (tt, tt), 0) < lax.broadcasted_iota(jnp.int32, (tt, tt), 1))
    tri = jnp.where(before, 1.0, 0.0).astype(BF16)
    c1 = _dot(oh1.astype(BF16), tri)
    c2 = _dot(oh2.astype(BF16), tri)
    tot1 = jnp.sum(oh1, axis=1, keepdims=True)
    tot2 = jnp.sum(oh2, axis=1, keepdims=True)
    base = base_ref[:, 0:1]
    rank1 = jnp.sum(oh1 * (base + c1), axis=0, keepdims=True)
    rank2 = jnp.sum(oh2 * (base + tot1 + c2), axis=0, keepdims=True)
    rank_ref[...] = jnp.concatenate([rank1, rank2, jnp.zeros((ROUTE_ROWS - 2, tt), F32)], axis=0).astype(jnp.int32)
    new_base = jnp.broadcast_to(base + tot1 + tot2, base_ref.shape)
    base_ref[...] = new_base
    cnt_ref[...] = new_base.astype(jnp.int32)


def _route_rank(route, tile):
    n = route.shape[1]
    tile = min(tile, n)
    return pl.pallas_call(
        _route_rank_kernel,
        grid=(n // tile,),
        in_specs=[pl.BlockSpec((ROUTE_ROWS, tile), lambda i: (0, i))],
        out_specs=[pl.BlockSpec((ROUTE_ROWS, tile), lambda i: (0, i)),
                   pl.BlockSpec((N_EXPERTS, LANES), lambda i: (0, 0))],
        out_shape=[jax.ShapeDtypeStruct((ROUTE_ROWS, n), jnp.int32),
                   jax.ShapeDtypeStruct((N_EXPERTS, LANES), jnp.int32)],
        scratch_shapes=[pltpu.VMEM((N_EXPERTS, LANES), F32)],
        compiler_params=_cparams("arbitrary"),
        name="route_rank",
    )(route)


def _sc_mesh():
    return plsc.VectorSubcoreMesh(core_axis_name="core", subcore_axis_name="subcore")


def _sc_scatter_pairs(x, pos, rows_out):
    n, d = x.shape
    per_w = n // SC_WORKERS
    window = min(SC_SCATTER_ROWS, per_w)

    @pl.kernel(out_type=jax.ShapeDtypeStruct((rows_out, d), x.dtype), mesh=_sc_mesh(),
               scratch_types=[pltpu.VMEM((window,), jnp.int32), pltpu.VMEM((window,), jnp.int32),
                              pltpu.VMEM((window, d), x.dtype), pltpu.SemaphoreType.DMA, pltpu.SemaphoreType.DMA,
                              pltpu.SemaphoreType.DMA])
    def scatter(x_hbm, p_hbm, o_hbm, i1_v, i2_v, rows_v, sem_a, sem_b, sem_c):
        wid = lax.axis_index("subcore") * SC_CORES + lax.axis_index("core")

        @pl.loop(0, per_w // window)
        def _(j):
            base = wid * per_w + j * window
            load_i1 = pltpu.async_copy(p_hbm.at[pl.ds(base, window)], i1_v, sem_a)
            load_i2 = pltpu.async_copy(p_hbm.at[pl.ds(n + base, window)], i2_v, sem_b)
            load_x = pltpu.async_copy(x_hbm.at[pl.ds(base, window)], rows_v, sem_c)
            load_i1.wait()
            load_i2.wait()
            load_x.wait()
            put_1 = pltpu.async_copy(rows_v, o_hbm.at[i1_v], sem_a)
            put_2 = pltpu.async_copy(rows_v, o_hbm.at[i2_v], sem_b)
            put_1.wait()
            put_2.wait()

    return scatter(x, pos)


def _sc_gather_rows(table, idx):
    m = idx.shape[0]
    d = table.shape[1]
    per_w = m // SC_WORKERS
    window = min(SC_GATHER_ROWS, per_w)

    assert per_w % (2 * window) == 0

    @pl.kernel(out_type=jax.ShapeDtypeStruct((m, d), table.dtype), mesh=_sc_mesh(),
               scratch_types=[pltpu.VMEM((window,), jnp.int32), pltpu.VMEM((window,), jnp.int32),
                              pltpu.VMEM((window, d), table.dtype), pltpu.VMEM((window, d), table.dtype),
                              pltpu.SemaphoreType.DMA, pltpu.SemaphoreType.DMA])
    def gather(t_hbm, i_hbm, o_hbm, ia_v, ib_v, ra_v, rb_v, sem_a, sem_b):
        wid = lax.axis_index("subcore") * SC_CORES + lax.axis_index("core")

        @pl.loop(0, per_w // (2 * window))
        def _(j):
            base_a = wid * per_w + j * (2 * window)
            base_b = base_a + window
            idx_a = pltpu.async_copy(i_hbm.at[pl.ds(base_a, window)], ia_v, sem_a)
            idx_b = pltpu.async_copy(i_hbm.at[pl.ds(base_b, window)], ib_v, sem_b)
            idx_a.wait()
            get_a = pltpu.async_copy(t_hbm.at[ia_v], ra_v, sem_a)
            idx_b.wait()
            get_b = pltpu.async_copy(t_hbm.at[ib_v], rb_v, sem_b)
            get_a.wait()
            put_a = pltpu.async_copy(ra_v, o_hbm.at[pl.ds(base_a, window)], sem_a)
            get_b.wait()
            put_b = pltpu.async_copy(rb_v, o_hbm.at[pl.ds(base_b, window)], sem_b)
            put_a.wait()
            put_b.wait()

    return gather(table, idx)


def _expert_tiles_kernel(start_ref, ntile_ref, x_hbm, wg_ref, wu_ref, wd_ref, o_hbm,
                         wg_s, wu_s, wd_s, x_buf, o_buf, in_sem, out_sem):
    e = pl.program_id(0)
    tm = x_buf.shape[1]
    nslot = x_buf.shape[0]
    first = start_ref[e] // tm
    ntile = ntile_ref[e]
    total = start_ref[N_EXPERTS - 1] // tm + ntile_ref[N_EXPERTS - 1]
    wg_s[...] = wg_ref[...].astype(BF16)
    wu_s[...] = wu_ref[...].astype(BF16)
    wd_s[...] = wd_ref[...].astype(BF16)

    def rows_of(g):
        return pl.ds(pl.multiple_of(g * tm, tm), tm)

    def fetch(g):
        slot = g % nslot
        return pltpu.make_async_copy(x_hbm.at[rows_of(g)], x_buf.at[slot], in_sem.at[slot])

    def flush(g):
        slot = g % nslot
        return pltpu.make_async_copy(o_buf.at[slot], o_hbm.at[rows_of(g)], out_sem.at[slot])

    @pl.when(e == 0)
    def _():
        for k in range(nslot - 1):
            @pl.when(k < total)
            def _(k=k):
                fetch(k).start()

    def tile(g, carry):
        @pl.when(g + nslot - 1 < total)
        def _():
            fetch(g + nslot - 1).start()

        fetch(g).wait()

        @pl.when(g >= nslot)
        def _():
            flush(g - nslot).wait()

        slot = g % nslot
        o_buf[slot] = _pack_halves(_expert_mlp(x_buf[slot], wg_s, wu_s, wd_s).astype(BF16))
        flush(g).start()
        return carry

    lax.fori_loop(first, first + ntile, tile, 0)

    @pl.when(e == N_EXPERTS - 1)
    def _():
        for k in range(nslot, 0, -1):
            @pl.when(total >= k)
            def _(k=k):
                flush(total - k).wait()


def _expert_tiles(starts, ntiles, xs, w_g, w_u, w_d):
    rows = xs.shape[0]
    tm = EXPERT_ROW_TILE
    weight = lambda shape: pl.BlockSpec((None,) + shape, lambda e, st, nt: (e, 0, 0))
    grid_spec = pltpu.PrefetchScalarGridSpec(
        num_scalar_prefetch=2,
        grid=(N_EXPERTS,),
        in_specs=[pl.BlockSpec(memory_space=pl.ANY),
                  weight((D_MODEL, D_EXPERT)), weight((D_MODEL, D_EXPERT)), weight((D_EXPERT, D_MODEL))],
        out_specs=pl.BlockSpec(memory_space=pl.ANY),
        scratch_shapes=[pltpu.VMEM((D_MODEL, D_EXPERT), BF16), pltpu.VMEM((D_MODEL, D_EXPERT), BF16),
                        pltpu.VMEM((D_EXPERT, D_MODEL), BF16),
                        pltpu.VMEM((EXPERT_SLOTS, tm, HALF), jnp.int32), pltpu.VMEM((EXPERT_SLOTS, tm, HALF), jnp.int32),
                        pltpu.SemaphoreType.DMA((EXPERT_SLOTS,)), pltpu.SemaphoreType.DMA((EXPERT_SLOTS,))],
    )
    return pl.pallas_call(
        _expert_tiles_kernel,
        grid_spec=grid_spec,
        out_shape=jax.ShapeDtypeStruct((rows, HALF), jnp.int32),
        compiler_params=_cparams("arbitrary"),
        name="expert_tiles",
    )(starts, ntiles, xs, w_g, w_u, w_d)


def _unpack_f32(p):
    return pltpu.bitcast(p & jnp.int32(-65536), F32), pltpu.bitcast(lax.shift_left(p, jnp.int32(16)), F32)


def _combine_kernel(h_ref, o1_ref, o2_ref, rt_ref, gf_ref, y_ref):
    w1, w2 = rt_ref[:, 2:3], rt_ref[:, 3:4]
    a_lo, a_hi = _unpack_f32(o1_ref[...])
    b_lo, b_hi = _unpack_f32(o2_ref[...])
    y_lo = h_ref[:, 0:HALF] + (w1 * a_lo + w2 * b_lo)
    y_hi = h_ref[:, HALF:D_MODEL] + (w1 * a_hi + w2 * b_hi)
    ms = (jnp.sum(y_lo * y_lo, axis=-1, keepdims=True) + jnp.sum(y_hi * y_hi, axis=-1, keepdims=True)) / D_MODEL
    inv = lax.rsqrt(ms + EPS)
    y_ref[:, 0:HALF] = (y_lo * inv) * gf_ref[:, 0:HALF]
    y_ref[:, HALF:D_MODEL] = (y_hi * inv) * gf_ref[:, HALF:D_MODEL]


def _combine(h, o12, route_t, gf, tile):
    n = h.shape[0]
    tile = min(tile, n)
    nt = n // tile
    return pl.pallas_call(
        _combine_kernel,
        grid=(nt,),
        in_specs=[pl.BlockSpec((tile, D_MODEL), lambda i: (i, 0)),
                  pl.BlockSpec((tile, HALF), lambda i: (i, 0)),
                  pl.BlockSpec((tile, HALF), lambda i: (i + nt, 0)),
                  pl.BlockSpec((tile, ROUTE_ROWS), lambda i: (i, 0)),
                  pl.BlockSpec((1, D_MODEL), lambda i: (0, 0))],
        out_specs=pl.BlockSpec((tile, D_MODEL), lambda i: (i, 0)),
        out_shape=jax.ShapeDtypeStruct((n, D_MODEL), F32),
        compiler_params=_cparams("parallel"),
        name="combine_final_norm",
    )(h, o12, o12, route_t, gf)


def _sparse_moe(xn2p, route, h, w_g, w_u, w_d, gf, run_before_experts):
    n = h.shape[0]
    tm = EXPERT_ROW_TILE
    rows = 2 * n + N_EXPERTS * tm
    rank, cnt = _route_rank(route, ROWS_ROUTE_RANK)
    counts = cnt[:, 0]
    padded = (counts + tm - 1) // tm * tm
    e_idx = jnp.arange(N_EXPERTS, dtype=jnp.int32)
    starts = jnp.sum(jnp.where(e_idx[None, :] < e_idx[:, None], padded[None, :], 0), axis=1)
    ids = route[0:2].astype(jnp.int32)
    start_of = jnp.sum(jnp.where(ids[None] == e_idx[:, None, None], starts[:, None, None], 0), axis=0)
    pos = (start_of + rank[0:2]).reshape(2 * n)
    xs = _sc_scatter_pairs(xn2p, pos, rows)
    xs, _ = lax.optimization_barrier((xs, run_before_experts))
    os_ = _expert_tiles(starts.astype(jnp.int32), (padded // tm).astype(jnp.int32), xs, w_g, w_u, w_d)
    o12 = _sc_gather_rows(os_, pos)
    return _combine(h, o12, route.T, gf, ROWS_COMBINE)


def _prep_in_weights(w_in):
    o = 0
    w_u = w_in[:, o:o + SSM_WIDTH]; o += SSM_WIDTH
    w_q = w_in[:, o:o + SWA_WIDTH]; o += SWA_WIDTH
    w_k = w_in[:, o:o + SWA_KV_WIDTH]; o += SWA_KV_WIDTH
    w_v = w_in[:, o:o + SWA_KV_WIDTH]; o += SWA_KV_WIDTH
    w_qm = w_in[:, o:o + MEM_WIDTH]; o += MEM_WIDTH
    assert o == GATES_COL0
    wq = (w_q * (SWA_HEAD_DIM ** -0.5)).reshape(D_MODEL, SWA_KV_HEADS, SWA_REP, SWA_HEAD_DIM)
    wq = wq.transpose(0, 2, 1, 3).reshape(D_MODEL, SWA_WIDTH)
    w_main = jnp.concatenate([w_u, wq, w_k, w_v, w_qm], axis=1).astype(BF16)
    return w_main, w_in.astype(BF16)


IN_SPLITS = (SSM_WIDTH, SWA_WIDTH, SWA_KV_WIDTH, SWA_KV_WIDTH, MEM_WIDTH)
IN_DTYPES = ((F32, BF16), (BF16,), (F32,), (F32,), (BF16,))


def kernel(x_prompt, x_sample, cache_swa_k, cache_swa_v, state_ssm_re, state_ssm_im, cache_mem_k, cache_mem_v, mem_prompt, norm1_g, w_in, lam_re, lam_im, log_dt, bm_re, bm_im, cm_re, cm_im, d_skip, w_glu, b_glu, sinks, rel_table, mem_norm_g, w_mem_kv, w_br_ssm, w_br_swa, w_br_mem, w_out, norm2_g, w_rg, b_rg, w_rexp, b_rexp, w_e_gate, w_e_up, w_e_down, final_norm_g):
    nb, t, _ = x_prompt.shape
    ns, ts, _ = x_sample.shape
    assert w_in.shape[0] == 1 and ts == S5_CHUNK and t % (WINDOW * SWA_BLOCKS_PER_STEP) == 0
    l = 0
    L = S5_CHUNK

    w_main, w_gates = _prep_in_weights(w_in[l])
    w_swa = (w_br_swa[l].reshape(SWA_KV_HEADS, SWA_REP, SWA_HEAD_DIM, D_MODEL).transpose(1, 0, 2, 3)
             .reshape(SWA_WIDTH, D_MODEL))
    pad_rows = ROUTER_ROWS - N_EXPERTS - N_EXPERT_GROUPS
    w_router = jnp.concatenate([w_rexp[l].T, w_rg[l].T, jnp.zeros((pad_rows, D_MODEL), F32)], axis=0).astype(BF16)
    b_router = jnp.concatenate([b_rexp[l], b_rg[l], jnp.zeros((pad_rows,), F32)]).reshape(ROUTER_ROWS, 1)
    mp = {
        'g1': norm1_g[l].reshape(1, D_MODEL), 'w_gates': w_gates, 'd_skip': d_skip[l].reshape(1, SSM_WIDTH),
        'w_glu': w_glu[l].astype(BF16), 'b_glu': b_glu[l].reshape(1, SSM_WIDTH),
        'w_br_ssm': w_br_ssm[l].astype(BF16), 'w_br_swa': w_swa.astype(BF16),
        'w_br_mem': w_br_mem[l].astype(BF16), 'w_out': w_out[l].astype(BF16),
        'g2': norm2_g[l].reshape(1, D_MODEL), 'w_router': w_router, 'b_router': b_router,
    }
    w_g, w_u, w_d = w_e_gate[l], w_e_up[l], w_e_down[l]
    gf = final_norm_g.reshape(1, D_MODEL)
    s5_w = _s5_weights(lam_re[l], lam_im[l], log_dt[l], bm_re[l], bm_im[l], cm_re[l], cm_im[l], L)

    bias_p = _rel_bias(rel_table, np.arange(WINDOW)[:, None] + WINDOW - np.arange(2 * WINDOW)[None, :])
    keys_s = WINDOW + 2 * ts
    bias_s = _rel_bias(rel_table, np.arange(ts)[:, None] + WINDOW - np.arange(keys_s)[None, :])
    bias_s = bias_s.reshape(SWA_HEADS * ts, keys_s)
    sink_rows = jnp.repeat(sinks[l].astype(F32), ts).reshape(SWA_HEADS * ts, 1)

    n = nb * t
    xp = x_prompt.reshape(n, D_MODEL)
    mk, mv = _norm_proj(mem_prompt.reshape(nb * MEM_TOKENS, D_MODEL), mem_norm_g[l].reshape(1, D_MODEL),
                        w_mem_kv[l].astype(BF16), (MEM_WIDTH, MEM_WIDTH), ((F32,), (F32,)), ROWS_MEM_PROJ)
    u, ub, qz, k, v, qm = _norm_proj(xp, mp['g1'], w_main, IN_SPLITS, IN_DTYPES, ROWS_NORM_PROJ)

    y_ssm, fin = _s5(ub, jnp.zeros((nb, N_CH_TILES * 2 * STATE_TILE), F32), s5_w, nb, t // L, L, S5_CHUNKS_PER_STEP)
    p_re, p_im = _tiles_to_state(fin)

    o_swa = _swa_prompt(qz, k, v, bias_p, sinks[l].astype(F32), nb, t, SWA_BLOCKS_PER_STEP)
    o_mem = _mem_prompt(qm, mk, mv, nb, t, ROWS_MEM_ATTN)
    h, xn2p, route = _merge(xp, u, y_ssm, o_swa, o_mem, mp, ROWS_MERGE)

    k4 = k.reshape(nb, t, SWA_KV_HEADS, SWA_HEAD_DIM)
    v4 = v.reshape(nb, t, SWA_KV_HEADS, SWA_HEAD_DIM)
    new_k_p, new_v_p = k4[:, -WINDOW:][None], v4[:, -WINDOW:][None]
    new_mk = mk.reshape(1, nb, MEM_TOKENS, MEM_HEADS, MEM_HEAD_DIM)
    new_mv = mv.reshape(1, nb, MEM_TOKENS, MEM_HEADS, MEM_HEAD_DIM)

    m = ns * ts
    xs = x_sample.reshape(m, D_MODEL)
    us, ubs, qzs, k_s, v_s, qms = _norm_proj(xs, mp['g1'], w_main, IN_SPLITS, IN_DTYPES, ROWS_NORM_PROJ)
    ys_ssm, fins = _s5(ubs, _state_to_tiles(state_ssm_re[l], state_ssm_im[l]), s5_w, ns, ts // L, L, S5_CHUNKS_PER_STEP)
    s_re, s_im = _tiles_to_state(fins)

    kk_all = jnp.concatenate([cache_swa_k[l].reshape(ns, WINDOW, SWA_KV_WIDTH).astype(F32),
                              k_s.reshape(ns, ts, SWA_KV_WIDTH)], axis=1)
    vv_all = jnp.concatenate([cache_swa_v[l].reshape(ns, WINDOW, SWA_KV_WIDTH).astype(F32),
                              v_s.reshape(ns, ts, SWA_KV_WIDTH)], axis=1)
    pad = jnp.zeros((ns, keys_s - WINDOW - ts, SWA_KV_WIDTH), F32)
    o_dec, roll_k, roll_v = _swa_decode(qzs.astype(F32).reshape(ns, ts, SWA_WIDTH),
                                        jnp.concatenate([kk_all, pad], axis=1),
                                        jnp.concatenate([vv_all, pad], axis=1), bias_s, sink_rows, DECODE_SEQS_PER_STEP)
    o_swa_s = o_dec.reshape(m, SWA_WIDTH).astype(BF16)

    o_mem_s = _mem_decode(qms.astype(F32).reshape(ns, ts, MEM_WIDTH), cache_mem_k, cache_mem_v, l, DECODE_SEQS_PER_STEP)
    o_mem_s = o_mem_s.reshape(m, MEM_WIDTH).astype(BF16)

    y_prompt = _sparse_moe(xn2p, route, h, w_g, w_u, w_d, gf, (ys_ssm, o_swa_s, o_mem_s)).reshape(nb, t, D_MODEL)
    hs_, xn2ps, routes = _merge(xs, us, ys_ssm, o_swa_s, o_mem_s, mp, ROWS_MERGE)
    y_sample = _moe(xn2ps, routes.T, w_g, w_u, w_d, hs_, gf, ROWS_DENSE_MOE).reshape(ns, ts, D_MODEL)

    new_k_s = roll_k.reshape(1, ns, WINDOW, SWA_KV_HEADS, SWA_HEAD_DIM).astype(cache_swa_k.dtype)
    new_v_s = roll_v.reshape(1, ns, WINDOW, SWA_KV_HEADS, SWA_HEAD_DIM).astype(cache_swa_v.dtype)

    return (y_prompt, y_sample,
            new_k_p, new_v_p, p_re[None], p_im[None], new_mk, new_mv,
            new_k_s, new_v_s, s_re[None].astype(state_ssm_re.dtype), s_im[None].astype(state_ssm_im.dtype))
```

```python
import functools
import math

import numpy as np
import jax
import jax.numpy as jnp
from jax import lax
from jax.experimental import pallas as pl
from jax.experimental.pallas import tpu as pltpu
from jax.experimental.pallas import tpu_sc as plsc

F32 = jnp.float32
BF16 = jnp.bfloat16

D_MODEL = 1024
SSM_WIDTH = 512
SSM_GROUP = 16
SSM_GROUPS = 32
SSM_STATE = 64
SWA_HEADS = 8
SWA_KV_HEADS = 2
SWA_REP = 4
SWA_HEAD_DIM = 64
SWA_WIDTH = 512
SWA_KV_WIDTH = 128
WINDOW = 128
REL_BUCKETS = 32
REL_MAX_DIST = 128
MEM_TOKENS = 256
MEM_HEADS = 4
MEM_HEAD_DIM = 128
MEM_WIDTH = 512
N_EXPERT_GROUPS = 4
EXPERTS_PER_GROUP = 8
N_EXPERTS = 32
D_EXPERT = 256
EPS = 1e-6
NEG_INF = -1e30

LANES = 128
GROUPS_PER_TILE = LANES // SSM_GROUP
N_CH_TILES = SSM_WIDTH // LANES
STATE_TILE = GROUPS_PER_TILE * SSM_STATE
VMEM_LIMIT = 56 * 1024 * 1024
ROWS_NORM_PROJ = 1024
ROWS_MEM_PROJ = 512
ROWS_MEM_ATTN = 1024
ROWS_MERGE = 512
ROWS_COMBINE = 512
ROWS_DENSE_MOE = 1024
S5_CHUNKS_PER_STEP = 128
SWA_BLOCKS_PER_STEP = 4
DECODE_SEQS_PER_STEP = 8
S5_CHUNK = 8
S5_PANEL = 256

_TRANS_B = (((1,), (1,)), ((), ()))


def _cparams(*sem):
    return pltpu.CompilerParams(dimension_semantics=sem, vmem_limit_bytes=VMEM_LIMIT)


def _rms(x, g):
    return (x * lax.rsqrt(jnp.mean(x * x, axis=-1, keepdims=True) + EPS)) * g


def _dot(a, b):
    return jnp.dot(a, b, preferred_element_type=F32)


def _norm_proj_kernel(x_ref, g_ref, w_ref, *out_refs, splits, dtypes):
    xb = _rms(x_ref[...], g_ref[...]).astype(BF16)
    off = 0
    outs = iter(out_refs)
    for width, dts in zip(splits, dtypes):
        r = _dot(xb, w_ref[:, off:off + width])
        for dt in dts:
            next(outs)[...] = r.astype(dt)
        off += width


def _norm_proj(x, g, w, splits, dtypes, tile):
    n, d = x.shape
    tile = min(tile, n)
    flat = [(wd, dt) for wd, dts in zip(splits, dtypes) for dt in dts]
    return pl.pallas_call(
        functools.partial(_norm_proj_kernel, splits=tuple(splits), dtypes=tuple(dtypes)),
        grid=(n // tile,),
        in_specs=[pl.BlockSpec((tile, d), lambda i: (i, 0)),
                  pl.BlockSpec((1, d), lambda i: (0, 0)),
                  pl.BlockSpec((d, sum(splits)), lambda i: (0, 0), pipeline_mode=pl.Buffered(1))],
        out_specs=[pl.BlockSpec((tile, wd), lambda i: (i, 0)) for wd, _ in flat],
        out_shape=[jax.ShapeDtypeStruct((n, wd), dt) for wd, dt in flat],
        compiler_params=_cparams("parallel"),
        name="norm_proj",
    )(x, g, w)


def _s5_weights(lam_re, lam_im, log_dt, bm_re, bm_im, cm_re, cm_im, L):
    nt, gt, P, H = N_CH_TILES, GROUPS_PER_TILE, SSM_STATE, SSM_GROUP
    lr, li = lam_re.astype(F32), lam_im.astype(F32)
    dt = jnp.exp(log_dt.astype(F32))[:, None]
    mag = jnp.exp(lr * dt)
    a_re = mag * jnp.cos(li * dt)
    a_im = mag * jnp.sin(li * dt)
    den = lr * lr + li * li
    f_re = ((a_re - 1.0) * lr + a_im * li) / den
    f_im = (a_im * lr - (a_re - 1.0) * li) / den
    br, bi = bm_re.astype(F32), bm_im.astype(F32)
    bb_re = f_re[..., None] * br - f_im[..., None] * bi
    bb_im = f_re[..., None] * bi + f_im[..., None] * br
    pr, pi = [jnp.ones_like(a_re)], [jnp.zeros_like(a_im)]
    for _ in range(L):
        pr.append(pr[-1] * a_re - pi[-1] * a_im)
        pi.append(pr[-2] * a_im + pi[-1] * a_re)
    ap_re, ap_im = jnp.stack(pr), jnp.stack(pi)
    cr, ci = cm_re.astype(F32), cm_im.astype(F32)
    ca_re = cr[None] * ap_re[:, :, None, :] - ci[None] * ap_im[:, :, None, :]
    ca_im = cr[None] * ap_im[:, :, None, :] + ci[None] * ap_re[:, :, None, :]

    rev_re = jnp.stack([pr[L - 1 - s] for s in range(L)])
    rev_im = jnp.stack([pi[L - 1 - s] for s in range(L)])
    ws_re = rev_re[..., None] * bb_re[None] - rev_im[..., None] * bb_im[None]
    ws_im = rev_re[..., None] * bb_im[None] + rev_im[..., None] * bb_re[None]
    c_st = jnp.concatenate([ws_re.transpose(0, 1, 3, 2).reshape(L, nt, gt * H, P),
                            ws_im.transpose(0, 1, 3, 2).reshape(L, nt, gt * H, P)], axis=3).transpose(1, 0, 2, 3)
    so = lambda ca: ca[1:].transpose(1, 3, 0, 2).reshape(nt, gt * P, L * H)
    c_so = jnp.concatenate([so(ca_re), so(-ca_im)], axis=1)
    prod = (ca_re[:L][:, :, None, :, :] * bb_re.transpose(0, 2, 1)[None, :, :, None, :]
            - ca_im[:L][:, :, None, :, :] * bb_im.transpose(0, 2, 1)[None, :, :, None, :])
    k_lag = jnp.sum(prod, axis=-1).transpose(1, 2, 0, 3)
    c_k = k_lag.reshape(nt, gt * H, L * H)
    w_st, w_out, toep = _s5_expand(c_st, c_so, c_k, L)

    def per_tile(v):
        return v.reshape(nt, 1, STATE_TILE)

    return w_st, w_out, toep, per_tile(pr[L]), per_tile(pi[L])


def _s5_expand_kernel(cst_ref, cso_ref, ck_ref, wst_ref, wso_ref, toep_ref, *, L):
    hp = lax.Precision.HIGHEST
    P, H = SSM_STATE, SSM_GROUP
    iota = lambda shape, d: lax.broadcasted_iota(jnp.int32, shape, d)
    one = lambda cond: jnp.where(cond, 1.0, 0.0).astype(F32)

    r, c = iota((2 * P, 2 * STATE_TILE), 0), iota((2 * P, 2 * STATE_TILE), 1)
    rep_st = one((r // P == c // STATE_TILE) & (r % P == c % P))
    r, c = iota((LANES, 2 * STATE_TILE), 0), iota((LANES, 2 * STATE_TILE), 1)
    own_st = one(r // H == (c % STATE_TILE) // P)
    for s in range(L):
        blk = jnp.dot(cst_ref[s], rep_st, precision=hp, preferred_element_type=F32) * own_st
        wst_ref[s * LANES:(s + 1) * LANES, :] = blk.astype(BF16)

    r, c = iota((LANES, LANES), 0), iota((LANES, LANES), 1)
    pick = [one((r // H == t) & (r % H == c % H)) for t in range(L)]
    own_k = one(r // H == c // H)
    r, c = iota((2 * STATE_TILE, LANES), 0), iota((2 * STATE_TILE, LANES), 1)
    own_so = one((r % STATE_TILE) // P == c // H)
    cso = cso_ref[...]
    for t in range(L):
        blk = jnp.dot(cso, pick[t], precision=hp, preferred_element_type=F32) * own_so
        wso_ref[:, t * LANES:(t + 1) * LANES] = blk.astype(BF16)
    ck = ck_ref[...]
    lag = [(jnp.dot(ck, pick[t], precision=hp, preferred_element_type=F32) * own_k).astype(BF16) for t in range(L)]
    zero = jnp.zeros((LANES, LANES), BF16)
    for s in range(L):
        for t in range(L):
            toep_ref[s * LANES:(s + 1) * LANES, t * LANES:(t + 1) * LANES] = lag[t - s] if t >= s else zero


def _s5_expand(c_st, c_so, c_k, L):
    lk = L * LANES
    st2 = 2 * STATE_TILE
    return pl.pallas_call(
        functools.partial(_s5_expand_kernel, L=L),
        grid=(N_CH_TILES,),
        in_specs=[pl.BlockSpec((None, L, LANES, 2 * SSM_STATE), lambda j: (j, 0, 0, 0)),
                  pl.BlockSpec((None, st2, L * SSM_GROUP), lambda j: (j, 0, 0)),
                  pl.BlockSpec((None, LANES, L * SSM_GROUP), lambda j: (j, 0, 0))],
        out_specs=[pl.BlockSpec((None, lk, st2), lambda j: (j, 0, 0)),
                   pl.BlockSpec((None, st2, lk), lambda j: (j, 0, 0)),
                   pl.BlockSpec((None, lk, lk), lambda j: (j, 0, 0))],
        out_shape=[jax.ShapeDtypeStruct((N_CH_TILES, lk, st2), BF16),
                   jax.ShapeDtypeStruct((N_CH_TILES, st2, lk), BF16),
                   jax.ShapeDtypeStruct((N_CH_TILES, lk, lk), BF16)],
        compiler_params=_cparams("parallel"),
        name="s5_expand_weights",
    )(c_st, c_so, c_k)


def _to_chunks(u, nb, nc, L):
    return (u.reshape(nb, nc, L, N_CH_TILES, LANES).transpose(1, 0, 3, 2, 4)
            .reshape(nc * nb, N_CH_TILES * L * LANES))


def _from_chunks(y, nb, nc, L):
    return (y.reshape(nc, nb, N_CH_TILES, L, LANES).transpose(1, 0, 3, 2, 4)
            .reshape(nb * nc * L, SSM_WIDTH))


def _s5_kernel(x_ref, h0_ref, are_ref, aim_ref, ws_ref, t_ref, wo_ref, y_ref, fin_ref,
               hr_ref, hi_ref, d_ref, hs_ref, *, cb, nb, jt):
    ci = pl.program_id(1)
    last = ci == pl.num_programs(1) - 1
    lk, st2 = t_ref.shape[1], 2 * STATE_TILE
    for jj in range(jt):
        s0 = jj * st2

        @pl.when(ci == 0)
        def _(jj=jj, s0=s0):
            hr_ref[jj] = h0_ref[:, s0:s0 + STATE_TILE]
            hi_ref[jj] = h0_ref[:, s0 + STATE_TILE:s0 + st2]

        x = x_ref[:, jj * lk:(jj + 1) * lk]
        d_ref[...] = _dot(x, ws_ref[jj])
        ar = jnp.broadcast_to(are_ref[jj], (nb, STATE_TILE))
        ai = jnp.broadcast_to(aim_ref[jj], (nb, STATE_TILE))

        def body(c, carry, ar=ar, ai=ai):
            hr, hi = carry
            r0 = pl.multiple_of(c * nb, nb)
            hs_ref[pl.ds(r0, nb), 0:STATE_TILE] = hr
            hs_ref[pl.ds(r0, nb), STATE_TILE:st2] = hi
            d = d_ref[pl.ds(r0, nb), :]
            return (ar * hr - ai * hi + d[:, 0:STATE_TILE],
                    ar * hi + ai * hr + d[:, STATE_TILE:st2])

        hr, hi = lax.fori_loop(0, cb, body, (hr_ref[jj], hi_ref[jj]))
        hr_ref[jj] = hr
        hi_ref[jj] = hi
        hsb = hs_ref[...].astype(BF16)
        for c0 in range(0, lk, S5_PANEL):
            c1 = c0 + S5_PANEL
            y_ref[:, jj * lk + c0:jj * lk + c1] = (_dot(x[:, 0:c1], t_ref[jj, 0:c1, c0:c1])
                                                   + _dot(hsb, wo_ref[jj, :, c0:c1]))

        @pl.when(last)
        def _(s0=s0, hr=hr, hi=hi):
            fin_ref[:, s0:s0 + STATE_TILE] = hr
            fin_ref[:, s0 + STATE_TILE:s0 + st2] = hi


def _s5(ub, h0, weights, nb, nc, L, chunk_block):
    w_st, w_so, toep, a_re, a_im = weights
    xc = _to_chunks(ub, nb, nc, L)
    cb = min(chunk_block, nc)
    rows = cb * nb
    lk = L * LANES
    st2 = 2 * STATE_TILE
    jt = N_CH_TILES if nc == cb else 1
    mode = dict(pipeline_mode=pl.Buffered(1)) if jt == N_CH_TILES else {}
    tile_w = lambda shape: pl.BlockSpec((jt,) + shape, lambda j, c: (j, 0, 0), **mode)
    y, fin = pl.pallas_call(
        functools.partial(_s5_kernel, cb=cb, nb=nb, jt=jt),
        grid=(N_CH_TILES // jt, nc // cb),
        in_specs=[pl.BlockSpec((rows, jt * lk), lambda j, c: (c, j)),
                  pl.BlockSpec((nb, jt * st2), lambda j, c: (0, j)),
                  tile_w((1, STATE_TILE)), tile_w((1, STATE_TILE)),
                  tile_w((lk, st2)), tile_w((lk, lk)), tile_w((st2, lk))],
        out_specs=[pl.BlockSpec((rows, jt * lk), lambda j, c: (c, j)),
                   pl.BlockSpec((nb, jt * st2), lambda j, c: (0, j))],
        out_shape=[jax.ShapeDtypeStruct((nc * nb, N_CH_TILES * lk), F32),
                   jax.ShapeDtypeStruct((nb, N_CH_TILES * st2), F32)],
        scratch_shapes=[pltpu.VMEM((jt, nb, STATE_TILE), F32), pltpu.VMEM((jt, nb, STATE_TILE), F32),
                        pltpu.VMEM((rows, st2), F32), pltpu.VMEM((rows, st2), F32)],
        compiler_params=_cparams("parallel", "arbitrary"),
        name="s5_chunked_scan",
    )(xc, h0, a_re, a_im, w_st, toep, w_so)
    return _from_chunks(y, nb, nc, L), fin


def _state_to_tiles(h_re, h_im):
    nb = h_re.shape[0]
    r = h_re.astype(F32).reshape(nb, N_CH_TILES, STATE_TILE)
    i = h_im.astype(F32).reshape(nb, N_CH_TILES, STATE_TILE)
    return jnp.concatenate([r, i], axis=-1).reshape(nb, N_CH_TILES * 2 * STATE_TILE)


def _tiles_to_state(h):
    nb = h.shape[0]
    h = h.reshape(nb, N_CH_TILES, 2, GROUPS_PER_TILE, SSM_STATE)
    return (h[:, :, 0].reshape(nb, SSM_GROUPS, SSM_STATE), h[:, :, 1].reshape(nb, SSM_GROUPS, SSM_STATE))


def _t5_bucket(dist):
    n = np.maximum(dist, 0)
    max_exact = REL_BUCKETS // 2
    nf = np.maximum(n, 1).astype(np.float32)
    large = max_exact + (np.log(nf / np.float32(max_exact)) / np.float32(math.log(REL_MAX_DIST / max_exact))
                         * np.float32(REL_BUCKETS - max_exact)).astype(np.int32)
    large = np.minimum(large, REL_BUCKETS - 1)
    return np.where(n < max_exact, n, large)


def _rel_bias(rel_table, dist):
    bucket = _t5_bucket(dist)
    tab = rel_table.astype(F32)
    out = jnp.zeros((SWA_HEADS,) + dist.shape, F32)
    for b in range(REL_BUCKETS):
        sel = jnp.asarray(bucket == b)
        if bool((bucket == b).any()):
            out = jnp.where(sel[None], tab[b].reshape((SWA_HEADS,) + (1,) * dist.ndim), out)
    return out


def _swa_prompt_kernel(sink_ref, q_ref, kp_ref, kc_ref, vp_ref, vc_ref, bias_ref, o_ref, kk_ref, vv_ref, *, qblocks):
    step = pl.program_id(1)
    kk_ref[0:WINDOW, :] = kp_ref[...].astype(BF16)
    kk_ref[WINDOW:, :] = kc_ref[...].astype(BF16)
    vv_ref[0:WINDOW, :] = vp_ref[...].astype(BF16)
    vv_ref[WINDOW:, :] = vc_ref[...].astype(BF16)
    row = lax.broadcasted_iota(jnp.int32, (WINDOW, 2 * WINDOW), 0)
    col = lax.broadcasted_iota(jnp.int32, (WINDOW, 2 * WINDOW), 1)
    dist = row + WINDOW - col
    band = (dist >= 0) & (dist < WINDOW)
    lane = lax.broadcasted_iota(jnp.int32, (WINDOW, LANES), 1)
    low = lane < SWA_HEAD_DIM

    def block(j, carry):
        r0 = pl.multiple_of(j * WINDOW, WINDOW)
        kk = kk_ref[pl.ds(r0, 2 * WINDOW), :]
        vv = vv_ref[pl.ds(r0, 2 * WINDOW), :]
        valid = band & ((col >= WINDOW) | (step * qblocks + j > 0))
        for t in range(SWA_REP):
            q2 = q_ref[pl.ds(r0, WINDOW), t * LANES:(t + 1) * LANES]
            outs = []
            for half in range(SWA_KV_HEADS):
                h = t + SWA_REP * half
                qh = jnp.where(low if half == 0 else jnp.logical_not(low), q2, jnp.zeros_like(q2))
                s = lax.dot_general(qh, kk, _TRANS_B, preferred_element_type=F32)
                s = jnp.where(valid, s + bias_ref[h], NEG_INF)
                sink = sink_ref[h]
                m = jnp.maximum(jnp.max(s, axis=-1, keepdims=True), sink)
                e = jnp.exp(s - m)
                den = jnp.sum(e, axis=-1, keepdims=True) + jnp.exp(sink - m)
                outs.append(_dot(e.astype(BF16), vv) * (1.0 / den))
            o_ref[pl.ds(r0, WINDOW), t * LANES:(t + 1) * LANES] = jnp.where(low, outs[0], outs[1]).astype(BF16)
        return carry

    lax.fori_loop(0, qblocks, block, 0)


def _swa_prompt(q, k, v, bias, sinks, nb, t, qblocks):
    nstep = t // (WINDOW * qblocks)
    rows = WINDOW * qblocks
    cur = lambda b, i: (b * nstep + i, 0)
    prev = lambda b, i: (b * nstep * qblocks + jnp.maximum(i * qblocks - 1, 0), 0)
    return pl.pallas_call(
        functools.partial(_swa_prompt_kernel, qblocks=qblocks),
        grid=(nb, nstep),
        in_specs=[pl.BlockSpec(memory_space=pltpu.SMEM),
                  pl.BlockSpec((rows, SWA_WIDTH), cur),
                  pl.BlockSpec((WINDOW, SWA_KV_WIDTH), prev),
                  pl.BlockSpec((rows, SWA_KV_WIDTH), cur),
                  pl.BlockSpec((WINDOW, SWA_KV_WIDTH), prev),
                  pl.BlockSpec((rows, SWA_KV_WIDTH), cur),
                  pl.BlockSpec((SWA_HEADS, WINDOW, 2 * WINDOW), lambda b, i: (0, 0, 0))],
        out_specs=pl.BlockSpec((rows, SWA_WIDTH), cur),
        out_shape=jax.ShapeDtypeStruct((nb * t, SWA_WIDTH), BF16),
        scratch_shapes=[pltpu.VMEM((rows + WINDOW, SWA_KV_WIDTH), BF16),
                        pltpu.VMEM((rows + WINDOW, SWA_KV_WIDTH), BF16)],
        compiler_params=_cparams("parallel", "parallel"),
        name="swa_prompt",
    )(sinks, q, k, k, v, v, bias)


def _swa_decode_kernel(q_ref, k_ref, v_ref, bias_ref, sink_ref, o_ref, nk_ref, nv_ref, *, seqs, tq):
    rows, keys = SWA_HEADS * tq, k_ref.shape[1]
    nk_ref[...] = k_ref[:, tq:tq + WINDOW, :]
    nv_ref[...] = v_ref[:, tq:tq + WINDOW, :]
    low = lax.broadcasted_iota(jnp.int32, (tq, LANES), 1) < SWA_HEAD_DIM
    qi = lax.broadcasted_iota(jnp.int32, (rows, keys), 0) % tq
    col = lax.broadcasted_iota(jnp.int32, (rows, keys), 1)
    dist = qi + WINDOW - col
    valid = (dist >= 0) & (dist < WINDOW)
    bias = bias_ref[...]
    sink = sink_ref[...]
    for s_i in range(seqs):
        q = q_ref[s_i]
        tiles = [q[:, t * LANES:(t + 1) * LANES] for t in range(SWA_REP)]
        qh = jnp.concatenate([jnp.where(low, x, 0.0) for x in tiles]
                             + [jnp.where(low, 0.0, x) for x in tiles], axis=0)
        kk = k_ref[s_i].astype(BF16)
        s = lax.dot_general(qh.astype(BF16), kk, _TRANS_B, preferred_element_type=F32)
        s = jnp.where(valid, s + bias, NEG_INF)
        m = jnp.maximum(jnp.max(s, axis=-1, keepdims=True), sink)
        e = jnp.exp(s - m)
        den = jnp.sum(e, axis=-1, keepdims=True) + jnp.exp(sink - m)
        o = _dot(e.astype(BF16), v_ref[s_i].astype(BF16)) * (1.0 / den)
        for t in range(SWA_REP):
            o_ref[s_i, :, t * LANES:(t + 1) * LANES] = jnp.where(
                low, o[t * tq:(t + 1) * tq], o[(t + SWA_REP) * tq:(t + SWA_REP + 1) * tq])


def _swa_decode(q, k_all, v_all, bias, sink_rows, seqs):
    nseq, tq, _ = q.shape
    rows = SWA_HEADS * tq
    keys = k_all.shape[1]
    seqs = min(seqs, nseq)
    return pl.pallas_call(
        functools.partial(_swa_decode_kernel, seqs=seqs, tq=tq),
        grid=(nseq // seqs,),
        in_specs=[pl.BlockSpec((seqs, tq, SWA_WIDTH), lambda i: (i, 0, 0)),
                  pl.BlockSpec((seqs, keys, LANES), lambda i: (i, 0, 0)),
                  pl.BlockSpec((seqs, keys, LANES), lambda i: (i, 0, 0)),
                  pl.BlockSpec((rows, keys), lambda i: (0, 0)),
                  pl.BlockSpec((rows, 1), lambda i: (0, 0))],
        out_specs=[pl.BlockSpec((seqs, tq, SWA_WIDTH), lambda i: (i, 0, 0)),
                   pl.BlockSpec((seqs, WINDOW, LANES), lambda i: (i, 0, 0)),
                   pl.BlockSpec((seqs, WINDOW, LANES), lambda i: (i, 0, 0))],
        out_shape=[jax.ShapeDtypeStruct((nseq, tq, SWA_WIDTH), F32),
                   jax.ShapeDtypeStruct((nseq, WINDOW, LANES), F32),
                   jax.ShapeDtypeStruct((nseq, WINDOW, LANES), F32)],
        compiler_params=_cparams("parallel"),
        name="swa_decode",
    )(q, k_all, v_all, bias, sink_rows)


def _softmax(s):
    m = jnp.max(s, axis=-1, keepdims=True)
    e = jnp.exp(s - m)
    return e * (1.0 / jnp.sum(e, axis=-1, keepdims=True))


def _mem_prompt_kernel(q_ref, k_ref, v_ref, o_ref, s_ref, p_ref):
    scale = MEM_HEAD_DIM ** -0.5
    heads = [slice(h * MEM_HEAD_DIM, (h + 1) * MEM_HEAD_DIM) for h in range(MEM_HEADS)]
    for h, sl in enumerate(heads):
        s_ref[h] = lax.dot_general(q_ref[:, sl], k_ref[:, sl].astype(BF16), _TRANS_B, preferred_element_type=F32)
    s = s_ref[...] * scale
    e = jnp.exp(s - jnp.max(s, axis=-1, keepdims=True))
    p_ref[...] = e.astype(BF16)
    inv = 1.0 / jnp.sum(e, axis=-1, keepdims=True)
    for h, sl in enumerate(heads):
        o_ref[:, sl] = (_dot(p_ref[h], v_ref[:, sl].astype(BF16)) * inv[h]).astype(BF16)


def _mem_prompt(qm, mk, mv, nb, t, tile):
    tile = min(tile, t)
    nt = t // tile
    return pl.pallas_call(
        _mem_prompt_kernel,
        grid=(nb, nt),
        in_specs=[pl.BlockSpec((tile, MEM_WIDTH), lambda b, i: (b * nt + i, 0)),
                  pl.BlockSpec((MEM_TOKENS, MEM_WIDTH), lambda b, i: (b, 0)),
                  pl.BlockSpec((MEM_TOKENS, MEM_WIDTH), lambda b, i: (b, 0))],
        out_specs=pl.BlockSpec((tile, MEM_WIDTH), lambda b, i: (b * nt + i, 0)),
        out_shape=jax.ShapeDtypeStruct((nb * t, MEM_WIDTH), BF16),
        scratch_shapes=[pltpu.VMEM((MEM_HEADS, tile, MEM_TOKENS), F32), pltpu.VMEM((MEM_HEADS, tile, MEM_TOKENS), BF16)],
        compiler_params=_cparams("parallel", "parallel"),
        name="mem_prompt",
    )(qm, mk, mv)


def _mem_decode_kernel(q_ref, k_ref, v_ref, o_ref, *, seqs):
    tq = q_ref.shape[1]
    rows, cols = MEM_HEADS * tq, MEM_TOKENS * MEM_HEADS
    k2 = k_ref.reshape(seqs, cols, MEM_HEAD_DIM)
    v2 = v_ref.reshape(seqs, cols, MEM_HEAD_DIM)
    scale = MEM_HEAD_DIM ** -0.5
    own = (lax.broadcasted_iota(jnp.int32, (rows, cols), 1) % MEM_HEADS
           == lax.broadcasted_iota(jnp.int32, (rows, cols), 0) // tq)
    for s_i in range(seqs):
        q = q_ref[s_i]
        qb = jnp.concatenate([q[:, h * MEM_HEAD_DIM:(h + 1) * MEM_HEAD_DIM] for h in range(MEM_HEADS)], axis=0)
        s = lax.dot_general(qb.astype(BF16), k2[s_i].astype(BF16), _TRANS_B, preferred_element_type=F32) * scale
        p = _softmax(jnp.where(own, s, NEG_INF)).astype(BF16)
        o = _dot(p, v2[s_i].astype(BF16))
        for h in range(MEM_HEADS):
            o_ref[s_i, :, h * MEM_HEAD_DIM:(h + 1) * MEM_HEAD_DIM] = o[h * tq:(h + 1) * tq, :]


def _mem_decode(q, k, v, layer, seqs):
    nseq, tq, _ = q.shape
    seqs = min(seqs, nseq)
    cache = pl.BlockSpec((None, seqs, MEM_TOKENS, MEM_HEADS, MEM_HEAD_DIM), lambda i: (layer, i, 0, 0, 0))
    return pl.pallas_call(
        functools.partial(_mem_decode_kernel, seqs=seqs),
        grid=(nseq // seqs,),
        in_specs=[pl.BlockSpec((seqs, tq, MEM_WIDTH), lambda i: (i, 0, 0)), cache, cache],
        out_specs=pl.BlockSpec((seqs, tq, MEM_WIDTH), lambda i: (i, 0, 0)),
        out_shape=jax.ShapeDtypeStruct((nseq, tq, MEM_WIDTH), F32),
        compiler_params=_cparams("parallel"),
        name="mem_decode",
    )(q, k, v)


ROUTER_ROWS = 40
GATES_COL0 = SSM_WIDTH + SWA_WIDTH + 2 * SWA_KV_WIDTH + MEM_WIDTH
ROUTE_ROWS = 8
HALF = D_MODEL // 2


def _pack_halves(xb):
    hi = pltpu.bitcast(xb[:, 0:HALF].astype(F32), jnp.int32)
    lo = pltpu.bitcast(xb[:, HALF:D_MODEL].astype(F32), jnp.int32)
    return hi | lax.shift_right_logical(lo, jnp.int32(16))


def _unpack_halves(p):
    hi = pltpu.bitcast(p & jnp.int32(-65536), F32).astype(BF16)
    lo = pltpu.bitcast(lax.shift_left(p, jnp.int32(16)), F32).astype(BF16)
    return hi, lo


def _merge_kernel(x_ref, u_ref, y_ref, os_ref, om_ref, g1_ref, wg_ref, dsk_ref, wglu_ref, bglu_ref,
                  wbs_ref, wbw_ref, wbm_ref, wout_ref, g2_ref, wr_ref, br_ref,
                  h_ref, xn2_ref, route_ref, cnt_ref, base_ref, tri_ref):
    x = x_ref[...]
    tt = x.shape[0]
    xb = _rms(x, g1_ref[...]).astype(BF16)
    z = jax.nn.gelu(y_ref[...] + dsk_ref[...] * u_ref[...])
    z = z * jax.nn.sigmoid(_dot(z.astype(BF16), wglu_ref[...]) + bglu_ref[...])
    gate = lambda b: jax.nn.sigmoid(_dot(xb, wg_ref[:, GATES_COL0 + b * D_MODEL:GATES_COL0 + (b + 1) * D_MODEL]))
    merged = gate(0) * _dot(z.astype(BF16), wbs_ref[...])
    merged = merged + gate(1) * _dot(os_ref[...], wbw_ref[...])
    merged = merged + gate(2) * _dot(om_ref[...], wbm_ref[...])
    h = x + _dot(merged.astype(BF16), wout_ref[...])
    h_ref[...] = h
    xn2 = _rms(h, g2_ref[...]).astype(BF16)
    xn2_ref[...] = _pack_halves(xn2)

    lt = lax.dot_general(wr_ref[...], xn2, _TRANS_B, preferred_element_type=F32) + br_ref[...]
    gl = lt[N_EXPERTS:N_EXPERTS + N_EXPERT_GROUPS]
    ge = jnp.exp(gl - jnp.max(gl, axis=0, keepdims=True))
    gp = ge / jnp.sum(ge, axis=0, keepdims=True)
    gw = jnp.max(gp, axis=0, keepdims=True)
    gidx = jnp.full((1, tt), N_EXPERT_GROUPS - 1, jnp.int32)
    for r in range(N_EXPERT_GROUPS - 2, -1, -1):
        gidx = jnp.where(gp[r:r + 1] == gw, r, gidx)
    ein = lt[(N_EXPERT_GROUPS - 1) * EXPERTS_PER_GROUP:N_EXPERTS]
    for r in range(N_EXPERT_GROUPS - 2, -1, -1):
        ein = jnp.where(gidx == r, lt[r * EXPERTS_PER_GROUP:(r + 1) * EXPERTS_PER_GROUP], ein)
    ee = jnp.exp(ein - jnp.max(ein, axis=0, keepdims=True))
    ep = ee / jnp.sum(ee, axis=0, keepdims=True)
    rowi = lax.broadcasted_iota(jnp.int32, (EXPERTS_PER_GROUP, tt), 0)
    p1 = jnp.max(ep, axis=0, keepdims=True)
    e1 = jnp.min(jnp.where(ep == p1, rowi, EXPERTS_PER_GROUP), axis=0, keepdims=True)
    ep2 = jnp.where(rowi == e1, -1.0, ep)
    p2 = jnp.max(ep2, axis=0, keepdims=True)
    e2 = jnp.min(jnp.where(ep2 == p2, rowi, EXPERTS_PER_GROUP), axis=0, keepdims=True)
    tot = p1 + p2
    w1 = p1 / tot * gw
    w2 = p2 / tot * gw
    id1 = gidx * EXPERTS_PER_GROUP + e1
    id2 = gidx * EXPERTS_PER_GROUP + e2

    step = pl.program_id(0)

    @pl.when(step == 0)
    def _():
        base_ref[...] = jnp.zeros_like(base_ref)
        before = lax.broadcasted_iota(jnp.int32, (tt, tt), 0) < lax.broadcasted_iota(jnp.int32, (tt, tt), 1)
        tri_ref[...] = jnp.where(before, 1.0, 0.0).astype(BF16)

    r32 = lax.broadcasted_iota(jnp.int32, (N_EXPERTS, tt), 0)
    oh1 = jnp.where(r32 == id1, 1.0, 0.0)
    oh2 = jnp.where(r32 == id2, 1.0, 0.0)
    c1 = _dot(oh1.astype(BF16), tri_ref[...])
    c2 = _dot(oh2.astype(BF16), tri_ref[...])
    tot1 = jnp.sum(oh1, axis=1, keepdims=True)
    tot2 = jnp.sum(oh2, axis=1, keepdims=True)
    base = base_ref[:, 0:1]
    rank1 = jnp.sum(oh1 * (base + c1), axis=0, keepdims=True)
    rank2 = jnp.sum(oh2 * (base + tot1 + c2), axis=0, keepdims=True)
    new_base = jnp.broadcast_to(base + tot1 + tot2, base_ref.shape)
    base_ref[...] = new_base
    cnt_ref[...] = new_base
    route_ref[...] = jnp.concatenate([id1.astype(F32), id2.astype(F32), w1, w2, rank1, rank2,
                                      jnp.zeros((ROUTE_ROWS - 6, tt), F32)], axis=0)


def _merge(x, u, y, o_swa, o_mem, p, tile):
    n = x.shape[0]
    tile = min(tile, n)
    row = lambda i: (i, 0)
    const = lambda i: (0, 0)
    full = lambda a: pl.BlockSpec(a.shape, const, pipeline_mode=pl.Buffered(1))
    weights = [p['g1'], p['w_gates'], p['d_skip'], p['w_glu'], p['b_glu'], p['w_br_ssm'], p['w_br_swa'],
               p['w_br_mem'], p['w_out'], p['g2'], p['w_router'], p['b_router']]
    return pl.pallas_call(
        _merge_kernel,
        grid=(n // tile,),
        in_specs=[pl.BlockSpec((tile, D_MODEL), row), pl.BlockSpec((tile, SSM_WIDTH), row),
                  pl.BlockSpec((tile, SSM_WIDTH), row), pl.BlockSpec((tile, SWA_WIDTH), row),
                  pl.BlockSpec((tile, MEM_WIDTH), row)] + [full(w) for w in weights],
        out_specs=[pl.BlockSpec((tile, D_MODEL), row), pl.BlockSpec((tile, HALF), row),
                   pl.BlockSpec((ROUTE_ROWS, tile), lambda i: (0, i)),
                   pl.BlockSpec((N_EXPERTS, LANES), const)],
        out_shape=[jax.ShapeDtypeStruct((n, D_MODEL), F32), jax.ShapeDtypeStruct((n, HALF), jnp.int32),
                   jax.ShapeDtypeStruct((ROUTE_ROWS, n), F32), jax.ShapeDtypeStruct((N_EXPERTS, LANES), F32)],
        scratch_shapes=[pltpu.VMEM((N_EXPERTS, LANES), F32), pltpu.VMEM((tile, tile), BF16)],
        compiler_params=_cparams("arbitrary"),
        name="merge_router",
    )(x, u, y, o_swa, o_mem, *weights)


def _expert_mlp(xp, wg, wu, wd):
    hi, lo = _unpack_halves(xp)
    g = _dot(hi, wg[0:HALF, :]) + _dot(lo, wg[HALF:D_MODEL, :])
    u = _dot(hi, wu[0:HALF, :]) + _dot(lo, wu[HALF:D_MODEL, :])
    hh = jax.nn.silu(g) * u
    return _dot(hh.astype(BF16), wd[...])


def _moe_kernel(xn2_ref, rt_ref, wg_ref, wu_ref, wd_ref, h_ref, gf_ref, o_ref, acc_ref):
    e = pl.program_id(1)

    @pl.when(e == 0)
    def _():
        acc_ref[...] = jnp.zeros_like(acc_ref)

    o = _expert_mlp(xn2_ref[...], wg_ref[...].astype(BF16), wu_ref[...].astype(BF16), wd_ref[...].astype(BF16))
    ef = e.astype(F32)
    c = (jnp.where(rt_ref[:, 0:1] == ef, rt_ref[:, 2:3], 0.0)
         + jnp.where(rt_ref[:, 1:2] == ef, rt_ref[:, 3:4], 0.0))
    acc_ref[...] += c * o

    @pl.when(e == N_EXPERTS - 1)
    def _():
        o_ref[...] = _rms(h_ref[...] + acc_ref[...], gf_ref[...])


def _moe(xn2, route_t, w_g, w_u, w_d, h, gf, tile):
    n = h.shape[0]
    tile = min(tile, n)
    return pl.pallas_call(
        _moe_kernel,
        grid=(n // tile, N_EXPERTS),
        in_specs=[pl.BlockSpec((tile, HALF), lambda i, e: (i, 0)),
                  pl.BlockSpec((tile, ROUTE_ROWS), lambda i, e: (i, 0)),
                  pl.BlockSpec((None, D_MODEL, D_EXPERT), lambda i, e: (e, 0, 0)),
                  pl.BlockSpec((None, D_MODEL, D_EXPERT), lambda i, e: (e, 0, 0)),
                  pl.BlockSpec((None, D_EXPERT, D_MODEL), lambda i, e: (e, 0, 0)),
                  pl.BlockSpec((tile, D_MODEL), lambda i, e: (i, 0)),
                  pl.BlockSpec((1, D_MODEL), lambda i, e: (0, 0))],
        out_specs=pl.BlockSpec((tile, D_MODEL), lambda i, e: (i, 0)),
        out_shape=jax.ShapeDtypeStruct((n, D_MODEL), F32),
        scratch_shapes=[pltpu.VMEM((tile, D_MODEL), F32)],
        compiler_params=_cparams("parallel", "arbitrary"),
        name="moe_final_norm",
    )(xn2, route_t, w_g, w_u, w_d, h, gf)


EXPERT_ROW_TILE = 256
EXPERT_SLOTS = 4
SC_CORES = 2
SC_SUBCORES = 16
SC_WORKERS = SC_CORES * SC_SUBCORES
SC_SCATTER_ROWS = 64
SC_GATHER_ROWS = 64


def _sc_mesh():
    return plsc.VectorSubcoreMesh(core_axis_name="core", subcore_axis_name="subcore")


def _sc_scatter_pairs(x, pos, rows_out):
    n, d = x.shape
    per_w = n // SC_WORKERS
    window = min(SC_SCATTER_ROWS, per_w)

    @pl.kernel(out_type=jax.ShapeDtypeStruct((rows_out, d), x.dtype), mesh=_sc_mesh(),
               scratch_types=[pltpu.VMEM((window,), jnp.int32), pltpu.VMEM((window,), jnp.int32),
                              pltpu.VMEM((window, d), x.dtype), pltpu.SemaphoreType.DMA, pltpu.SemaphoreType.DMA,
                              pltpu.SemaphoreType.DMA])
    def scatter(x_hbm, p_hbm, o_hbm, i1_v, i2_v, rows_v, sem_a, sem_b, sem_c):
        wid = lax.axis_index("subcore") * SC_CORES + lax.axis_index("core")

        @pl.loop(0, per_w // window)
        def _(j):
            base = wid * per_w + j * window
            load_i1 = pltpu.async_copy(p_hbm.at[pl.ds(base, window)], i1_v, sem_a)
            load_i2 = pltpu.async_copy(p_hbm.at[pl.ds(n + base, window)], i2_v, sem_b)
            load_x = pltpu.async_copy(x_hbm.at[pl.ds(base, window)], rows_v, sem_c)
            load_i1.wait()
            load_i2.wait()
            load_x.wait()
            put_1 = pltpu.async_copy(rows_v, o_hbm.at[i1_v], sem_a)
            put_2 = pltpu.async_copy(rows_v, o_hbm.at[i2_v], sem_b)
            put_1.wait()
            put_2.wait()

    return scatter(x, pos)


def _sc_gather_rows(table, idx):
    m = idx.shape[0]
    d = table.shape[1]
    per_w = m // SC_WORKERS
    window = min(SC_GATHER_ROWS, per_w)

    assert per_w % (2 * window) == 0

    @pl.kernel(out_type=jax.ShapeDtypeStruct((m, d), table.dtype), mesh=_sc_mesh(),
               scratch_types=[pltpu.VMEM((window,), jnp.int32), pltpu.VMEM((window,), jnp.int32),
                              pltpu.VMEM((window, d), table.dtype), pltpu.VMEM((window, d), table.dtype),
                              pltpu.SemaphoreType.DMA, pltpu.SemaphoreType.DMA])
    def gather(t_hbm, i_hbm, o_hbm, ia_v, ib_v, ra_v, rb_v, sem_a, sem_b):
        wid = lax.axis_index("subcore") * SC_CORES + lax.axis_index("core")

        @pl.loop(0, per_w // (2 * window))
        def _(j):
            base_a = wid * per_w + j * (2 * window)
            base_b = base_a + window
            idx_a = pltpu.async_copy(i_hbm.at[pl.ds(base_a, window)], ia_v, sem_a)
            idx_b = pltpu.async_copy(i_hbm.at[pl.ds(base_b, window)], ib_v, sem_b)
            idx_a.wait()
            get_a = pltpu.async_copy(t_hbm.at[ia_v], ra_v, sem_a)
            idx_b.wait()
            get_b = pltpu.async_copy(t_hbm.at[ib_v], rb_v, sem_b)
            get_a.wait()
            put_a = pltpu.async_copy(ra_v, o_hbm.at[pl.ds(base_a, window)], sem_a)
            get_b.wait()
            put_b = pltpu.async_copy(rb_v, o_hbm.at[pl.ds(base_b, window)], sem_b)
            put_a.wait()
            put_b.wait()

    return gather(table, idx)


def _expert_tiles_kernel(start_ref, ntile_ref, x_hbm, wg_ref, wu_ref, wd_ref, o_hbm,
                         wg_s, wu_s, wd_s, x_buf, o_buf, in_sem, out_sem):
    e = pl.program_id(0)
    tm = x_buf.shape[1]
    nslot = x_buf.shape[0]
    first = start_ref[e] // tm
    ntile = ntile_ref[e]
    total = start_ref[N_EXPERTS - 1] // tm + ntile_ref[N_EXPERTS - 1]
    wg_s[...] = wg_ref[...].astype(BF16)
    wu_s[...] = wu_ref[...].astype(BF16)
    wd_s[...] = wd_ref[...].astype(BF16)

    def rows_of(g):
        return pl.ds(pl.multiple_of(g * tm, tm), tm)

    def fetch(g):
        slot = g % nslot
        return pltpu.make_async_copy(x_hbm.at[rows_of(g)], x_buf.at[slot], in_sem.at[slot])

    def flush(g):
        slot = g % nslot
        return pltpu.make_async_copy(o_buf.at[slot], o_hbm.at[rows_of(g)], out_sem.at[slot])

    @pl.when(e == 0)
    def _():
        for k in range(nslot - 1):
            @pl.when(k < total)
            def _(k=k):
                fetch(k).start()

    def tile(g, carry):
        @pl.when(g + nslot - 1 < total)
        def _():
            fetch(g + nslot - 1).start()

        fetch(g).wait()

        @pl.when(g >= nslot)
        def _():
            flush(g - nslot).wait()

        slot = g % nslot
        o_buf[slot] = _pack_halves(_expert_mlp(x_buf[slot], wg_s, wu_s, wd_s).astype(BF16))
        flush(g).start()
        return carry

    lax.fori_loop(first, first + ntile, tile, 0)

    @pl.when(e == N_EXPERTS - 1)
    def _():
        for k in range(nslot, 0, -1):
            @pl.when(total >= k)
            def _(k=k):
                flush(total - k).wait()


def _expert_tiles(starts, ntiles, xs, w_g, w_u, w_d):
    rows = xs.shape[0]
    tm = EXPERT_ROW_TILE
    weight = lambda shape: pl.BlockSpec((None,) + shape, lambda e, st, nt: (e, 0, 0))
    grid_spec = pltpu.PrefetchScalarGridSpec(
        num_scalar_prefetch=2,
        grid=(N_EXPERTS,),
        in_specs=[pl.BlockSpec(memory_space=pl.ANY),
                  weight((D_MODEL, D_EXPERT)), weight((D_MODEL, D_EXPERT)), weight((D_EXPERT, D_MODEL))],
        out_specs=pl.BlockSpec(memory_space=pl.ANY),
        scratch_shapes=[pltpu.VMEM((D_MODEL, D_EXPERT), BF16), pltpu.VMEM((D_MODEL, D_EXPERT), BF16),
                        pltpu.VMEM((D_EXPERT, D_MODEL), BF16),
                        pltpu.VMEM((EXPERT_SLOTS, tm, HALF), jnp.int32), pltpu.VMEM((EXPERT_SLOTS, tm, HALF), jnp.int32),
                        pltpu.SemaphoreType.DMA((EXPERT_SLOTS,)), pltpu.SemaphoreType.DMA((EXPERT_SLOTS,))],
    )
    return pl.pallas_call(
        _expert_tiles_kernel,
        grid_spec=grid_spec,
        out_shape=jax.ShapeDtypeStruct((rows, HALF), jnp.int32),
        compiler_params=_cparams("arbitrary"),
        name="expert_tiles",
    )(starts, ntiles, xs, w_g, w_u, w_d)


def _unpack_f32(p):
    return pltpu.bitcast(p & jnp.int32(-65536), F32), pltpu.bitcast(lax.shift_left(p, jnp.int32(16)), F32)


def _combine_kernel(h_ref, o1_ref, o2_ref, rt_ref, gf_ref, y_ref):
    w1, w2 = rt_ref[:, 2:3], rt_ref[:, 3:4]
    a_lo, a_hi = _unpack_f32(o1_ref[...])
    b_lo, b_hi = _unpack_f32(o2_ref[...])
    y_lo = h_ref[:, 0:HALF] + (w1 * a_lo + w2 * b_lo)
    y_hi = h_ref[:, HALF:D_MODEL] + (w1 * a_hi + w2 * b_hi)
    ms = (jnp.sum(y_lo * y_lo, axis=-1, keepdims=True) + jnp.sum(y_hi * y_hi, axis=-1, keepdims=True)) / D_MODEL
    inv = lax.rsqrt(ms + EPS)
    y_ref[:, 0:HALF] = (y_lo * inv) * gf_ref[:, 0:HALF]
    y_ref[:, HALF:D_MODEL] = (y_hi * inv) * gf_ref[:, HALF:D_MODEL]


def _combine(h, o12, route_t, gf, tile):
    n = h.shape[0]
    tile = min(tile, n)
    nt = n // tile
    return pl.pallas_call(
        _combine_kernel,
        grid=(nt,),
        in_specs=[pl.BlockSpec((tile, D_MODEL), lambda i: (i, 0)),
                  pl.BlockSpec((tile, HALF), lambda i: (i, 0)),
                  pl.BlockSpec((tile, HALF), lambda i: (i + nt, 0)),
                  pl.BlockSpec((tile, ROUTE_ROWS), lambda i: (i, 0)),
                  pl.BlockSpec((1, D_MODEL), lambda i: (0, 0))],
        out_specs=pl.BlockSpec((tile, D_MODEL), lambda i: (i, 0)),
        out_shape=jax.ShapeDtypeStruct((n, D_MODEL), F32),
        compiler_params=_cparams("parallel"),
        name="combine_final_norm",
    )(h, o12, o12, route_t, gf)


def _sparse_moe(xn2p, route, cnt, h, w_g, w_u, w_d, gf, run_before_experts):
    n = h.shape[0]
    tm = EXPERT_ROW_TILE
    rows = 2 * n + N_EXPERTS * tm
    rank = route[4:6].astype(jnp.int32)
    counts = cnt[:, 0].astype(jnp.int32)
    padded = (counts + tm - 1) // tm * tm
    e_idx = jnp.arange(N_EXPERTS, dtype=jnp.int32)
    starts = jnp.sum(jnp.where(e_idx[None, :] < e_idx[:, None], padded[None, :], 0), axis=1)
    ids = route[0:2].astype(jnp.int32)
    start_of = jnp.sum(jnp.where(ids[None] == e_idx[:, None, None], starts[:, None, None], 0), axis=0)
    pos = (start_of + rank).reshape(2 * n)
    xs = _sc_scatter_pairs(xn2p, pos, rows)
    xs, _ = lax.optimization_barrier((xs, run_before_experts))
    os_ = _expert_tiles(starts.astype(jnp.int32), (padded // tm).astype(jnp.int32), xs, w_g, w_u, w_d)
    o12 = _sc_gather_rows(os_, pos)
    return _combine(h, o12, route.T, gf, ROWS_COMBINE)


def _prep_in_weights(w_in):
    o = 0
    w_u = w_in[:, o:o + SSM_WIDTH]; o += SSM_WIDTH
    w_q = w_in[:, o:o + SWA_WIDTH]; o += SWA_WIDTH
    w_k = w_in[:, o:o + SWA_KV_WIDTH]; o += SWA_KV_WIDTH
    w_v = w_in[:, o:o + SWA_KV_WIDTH]; o += SWA_KV_WIDTH
    w_qm = w_in[:, o:o + MEM_WIDTH]; o += MEM_WIDTH
    assert o == GATES_COL0
    wq = (w_q * (SWA_HEAD_DIM ** -0.5)).reshape(D_MODEL, SWA_KV_HEADS, SWA_REP, SWA_HEAD_DIM)
    wq = wq.transpose(0, 2, 1, 3).reshape(D_MODEL, SWA_WIDTH)
    w_main = jnp.concatenate([w_u, wq, w_k, w_v, w_qm], axis=1).astype(BF16)
    return w_main, w_in.astype(BF16)


IN_SPLITS = (SSM_WIDTH, SWA_WIDTH, SWA_KV_WIDTH, SWA_KV_WIDTH, MEM_WIDTH)
IN_DTYPES = ((F32, BF16), (BF16,), (F32,), (F32,), (BF16,))


def kernel(x_prompt, x_sample, cache_swa_k, cache_swa_v, state_ssm_re, state_ssm_im, cache_mem_k, cache_mem_v, mem_prompt, norm1_g, w_in, lam_re, lam_im, log_dt, bm_re, bm_im, cm_re, cm_im, d_skip, w_glu, b_glu, sinks, rel_table, mem_norm_g, w_mem_kv, w_br_ssm, w_br_swa, w_br_mem, w_out, norm2_g, w_rg, b_rg, w_rexp, b_rexp, w_e_gate, w_e_up, w_e_down, final_norm_g):
    nb, t, _ = x_prompt.shape
    ns, ts, _ = x_sample.shape
    assert w_in.shape[0] == 1 and ts == S5_CHUNK and t % (WINDOW * SWA_BLOCKS_PER_STEP) == 0
    l = 0
    L = S5_CHUNK

    w_main, w_gates = _prep_in_weights(w_in[l])
    w_swa = (w_br_swa[l].reshape(SWA_KV_HEADS, SWA_REP, SWA_HEAD_DIM, D_MODEL).transpose(1, 0, 2, 3)
             .reshape(SWA_WIDTH, D_MODEL))
    pad_rows = ROUTER_ROWS - N_EXPERTS - N_EXPERT_GROUPS
    w_router = jnp.concatenate([w_rexp[l].T, w_rg[l].T, jnp.zeros((pad_rows, D_MODEL), F32)], axis=0).astype(BF16)
    b_router = jnp.concatenate([b_rexp[l], b_rg[l], jnp.zeros((pad_rows,), F32)]).reshape(ROUTER_ROWS, 1)
    mp = {
        'g1': norm1_g[l].reshape(1, D_MODEL), 'w_gates': w_gates, 'd_skip': d_skip[l].reshape(1, SSM_WIDTH),
        'w_glu': w_glu[l].astype(BF16), 'b_glu': b_glu[l].reshape(1, SSM_WIDTH),
        'w_br_ssm': w_br_ssm[l].astype(BF16), 'w_br_swa': w_swa.astype(BF16),
        'w_br_mem': w_br_mem[l].astype(BF16), 'w_out': w_out[l].astype(BF16),
        'g2': norm2_g[l].reshape(1, D_MODEL), 'w_router': w_router, 'b_router': b_router,
    }
    w_g, w_u, w_d = w_e_gate[l], w_e_up[l], w_e_down[l]
    gf = final_norm_g.reshape(1, D_MODEL)
    s5_w = _s5_weights(lam_re[l], lam_im[l], log_dt[l], bm_re[l], bm_im[l], cm_re[l], cm_im[l], L)

    bias_p = _rel_bias(rel_table, np.arange(WINDOW)[:, None] + WINDOW - np.arange(2 * WINDOW)[None, :])
    keys_s = WINDOW + 2 * ts
    bias_s = _rel_bias(rel_table, np.arange(ts)[:, None] + WINDOW - np.arange(keys_s)[None, :])
    bias_s = bias_s.reshape(SWA_HEADS * ts, keys_s)
    sink_rows = jnp.repeat(sinks[l].astype(F32), ts).reshape(SWA_HEADS * ts, 1)

    n = nb * t
    xp = x_prompt.reshape(n, D_MODEL)
    mk, mv = _norm_proj(mem_prompt.reshape(nb * MEM_TOKENS, D_MODEL), mem_norm_g[l].reshape(1, D_MODEL),
                        w_mem_kv[l].astype(BF16), (MEM_WIDTH, MEM_WIDTH), ((F32,), (F32,)), ROWS_MEM_PROJ)
    u, ub, qz, k, v, qm = _norm_proj(xp, mp['g1'], w_main, IN_SPLITS, IN_DTYPES, ROWS_NORM_PROJ)

    y_ssm, fin = _s5(ub, jnp.zeros((nb, N_CH_TILES * 2 * STATE_TILE), F32), s5_w, nb, t // L, L, S5_CHUNKS_PER_STEP)
    p_re, p_im = _tiles_to_state(fin)

    o_swa = _swa_prompt(qz, k, v, bias_p, sinks[l].astype(F32), nb, t, SWA_BLOCKS_PER_STEP)
    o_mem = _mem_prompt(qm, mk, mv, nb, t, ROWS_MEM_ATTN)
    h, xn2p, route, cnt = _merge(xp, u, y_ssm, o_swa, o_mem, mp, ROWS_MERGE)

    k4 = k.reshape(nb, t, SWA_KV_HEADS, SWA_HEAD_DIM)
    v4 = v.reshape(nb, t, SWA_KV_HEADS, SWA_HEAD_DIM)
    new_k_p, new_v_p = k4[:, -WINDOW:][None], v4[:, -WINDOW:][None]
    new_mk = mk.reshape(1, nb, MEM_TOKENS, MEM_HEADS, MEM_HEAD_DIM)
    new_mv = mv.reshape(1, nb, MEM_TOKENS, MEM_HEADS, MEM_HEAD_DIM)

    m = ns * ts
    xs = x_sample.reshape(m, D_MODEL)
    us, ubs, qzs, k_s, v_s, qms = _norm_proj(xs, mp['g1'], w_main, IN_SPLITS, IN_DTYPES, ROWS_NORM_PROJ)
    ys_ssm, fins = _s5(ubs, _state_to_tiles(state_ssm_re[l], state_ssm_im[l]), s5_w, ns, ts // L, L, S5_CHUNKS_PER_STEP)
    s_re, s_im = _tiles_to_state(fins)

    kk_all = jnp.concatenate([cache_swa_k[l].reshape(ns, WINDOW, SWA_KV_WIDTH).astype(F32),
                              k_s.reshape(ns, ts, SWA_KV_WIDTH)], axis=1)
    vv_all = jnp.concatenate([cache_swa_v[l].reshape(ns, WINDOW, SWA_KV_WIDTH).astype(F32),
                              v_s.reshape(ns, ts, SWA_KV_WIDTH)], axis=1)
    pad = jnp.zeros((ns, keys_s - WINDOW - ts, SWA_KV_WIDTH), F32)
    o_dec, roll_k, roll_v = _swa_decode(qzs.astype(F32).reshape(ns, ts, SWA_WIDTH),
                                        jnp.concatenate([kk_all, pad], axis=1),
                                        jnp.concatenate([vv_all, pad], axis=1), bias_s, sink_rows, DECODE_SEQS_PER_STEP)
    o_swa_s = o_dec.reshape(m, SWA_WIDTH).astype(BF16)

    o_mem_s = _mem_decode(qms.astype(F32).reshape(ns, ts, MEM_WIDTH), cache_mem_k, cache_mem_v, l, DECODE_SEQS_PER_STEP)
    o_mem_s = o_mem_s.reshape(m, MEM_WIDTH).astype(BF16)

    y_prompt = _sparse_moe(xn2p, route, cnt, h, w_g, w_u, w_d, gf, (ys_ssm, o_swa_s, o_mem_s)).reshape(nb, t, D_MODEL)
    hs_, xn2ps, routes, _ = _merge(xs, us, ys_ssm, o_swa_s, o_mem_s, mp, ROWS_MERGE)
    y_sample = _moe(xn2ps, routes.T, w_g, w_u, w_d, hs_, gf, ROWS_DENSE_MOE).reshape(ns, ts, D_MODEL)

    new_k_s = roll_k.reshape(1, ns, WINDOW, SWA_KV_HEADS, SWA_HEAD_DIM).astype(cache_swa_k.dtype)
    new_v_s = roll_v.reshape(1, ns, WINDOW, SWA_KV_HEADS, SWA_HEAD_DIM).astype(cache_swa_v.dtype)

    return (y_prompt, y_sample,
            new_k_p, new_v_p, p_re[None], p_im[None], new_mk, new_mv,
            new_k_s, new_v_s, s_re[None].astype(state_ssm_re.dtype), s_im[None].astype(state_ssm_im.dtype))
```

```python
import functools
import math

import numpy as np
import jax
import jax.numpy as jnp
from jax import lax
from jax.experimental import pallas as pl
from jax.experimental.pallas import tpu as pltpu
from jax.experimental.pallas import tpu_sc as plsc

F32 = jnp.float32
BF16 = jnp.bfloat16

D_MODEL = 1024
SSM_WIDTH = 512
SSM_GROUP = 16
SSM_GROUPS = 32
SSM_STATE = 64
SWA_HEADS = 8
SWA_KV_HEADS = 2
SWA_REP = 4
SWA_HEAD_DIM = 64
SWA_WIDTH = 512
SWA_KV_WIDTH = 128
WINDOW = 128
REL_BUCKETS = 32
REL_MAX_DIST = 128
MEM_TOKENS = 256
MEM_HEADS = 4
MEM_HEAD_DIM = 128
MEM_WIDTH = 512
N_EXPERT_GROUPS = 4
EXPERTS_PER_GROUP = 8
N_EXPERTS = 32
D_EXPERT = 256
EPS = 1e-6
NEG_INF = -1e30

LANES = 128
GROUPS_PER_TILE = LANES // SSM_GROUP
N_CH_TILES = SSM_WIDTH // LANES
STATE_TILE = GROUPS_PER_TILE * SSM_STATE
VMEM_LIMIT = 56 * 1024 * 1024
ROWS_NORM_PROJ = 1024
ROWS_MEM_PROJ = 512
ROWS_MEM_ATTN = 1024
ROWS_MERGE = 512
ROWS_COMBINE = 512
ROWS_DENSE_MOE = 1024
S5_CHUNKS_PER_STEP = 128
SWA_BLOCKS_PER_STEP = 4
DECODE_SEQS_PER_STEP = 8
S5_CHUNK = 8
S5_PANEL = 256

_TRANS_B = (((1,), (1,)), ((), ()))


def _cparams(*sem):
    return pltpu.CompilerParams(dimension_semantics=sem, vmem_limit_bytes=VMEM_LIMIT)


def _rms(x, g):
    return (x * lax.rsqrt(jnp.mean(x * x, axis=-1, keepdims=True) + EPS)) * g


def _dot(a, b):
    return jnp.dot(a, b, preferred_element_type=F32)


def _norm_proj_kernel(x_ref, g_ref, w_ref, *out_refs, splits, dtypes):
    xb = _rms(x_ref[...], g_ref[...]).astype(BF16)
    off = 0
    outs = iter(out_refs)
    for width, dts in zip(splits, dtypes):
        r = _dot(xb, w_ref[:, off:off + width])
        for dt in dts:
            next(outs)[...] = r.astype(dt)
        off += width


def _norm_proj(x, g, w, splits, dtypes, tile):
    n, d = x.shape
    tile = min(tile, n)
    flat = [(wd, dt) for wd, dts in zip(splits, dtypes) for dt in dts]
    return pl.pallas_call(
        functools.partial(_norm_proj_kernel, splits=tuple(splits), dtypes=tuple(dtypes)),
        grid=(n // tile,),
        in_specs=[pl.BlockSpec((tile, d), lambda i: (i, 0)),
                  pl.BlockSpec((1, d), lambda i: (0, 0)),
                  pl.BlockSpec((d, sum(splits)), lambda i: (0, 0), pipeline_mode=pl.Buffered(1))],
        out_specs=[pl.BlockSpec((tile, wd), lambda i: (i, 0)) for wd, _ in flat],
        out_shape=[jax.ShapeDtypeStruct((n, wd), dt) for wd, dt in flat],
        compiler_params=_cparams("parallel"),
        name="norm_proj",
    )(x, g, w)


def _s5_weights(lam_re, lam_im, log_dt, bm_re, bm_im, cm_re, cm_im, L):
    nt, gt, P, H = N_CH_TILES, GROUPS_PER_TILE, SSM_STATE, SSM_GROUP
    lr, li = lam_re.astype(F32), lam_im.astype(F32)
    dt = jnp.exp(log_dt.astype(F32))[:, None]
    mag = jnp.exp(lr * dt)
    a_re = mag * jnp.cos(li * dt)
    a_im = mag * jnp.sin(li * dt)
    den = lr * lr + li * li
    f_re = ((a_re - 1.0) * lr + a_im * li) / den
    f_im = (a_im * lr - (a_re - 1.0) * li) / den
    br, bi = bm_re.astype(F32), bm_im.astype(F32)
    bb_re = f_re[..., None] * br - f_im[..., None] * bi
    bb_im = f_re[..., None] * bi + f_im[..., None] * br
    pr, pi = [jnp.ones_like(a_re)], [jnp.zeros_like(a_im)]
    for _ in range(L):
        pr.append(pr[-1] * a_re - pi[-1] * a_im)
        pi.append(pr[-2] * a_im + pi[-1] * a_re)
    ap_re, ap_im = jnp.stack(pr), jnp.stack(pi)
    cr, ci = cm_re.astype(F32), cm_im.astype(F32)
    ca_re = cr[None] * ap_re[:, :, None, :] - ci[None] * ap_im[:, :, None, :]
    ca_im = cr[None] * ap_im[:, :, None, :] + ci[None] * ap_re[:, :, None, :]

    rev_re = jnp.stack([pr[L - 1 - s] for s in range(L)])
    rev_im = jnp.stack([pi[L - 1 - s] for s in range(L)])
    ws_re = rev_re[..., None] * bb_re[None] - rev_im[..., None] * bb_im[None]
    ws_im = rev_re[..., None] * bb_im[None] + rev_im[..., None] * bb_re[None]
    c_st = jnp.concatenate([ws_re.transpose(0, 1, 3, 2).reshape(L, nt, gt * H, P),
                            ws_im.transpose(0, 1, 3, 2).reshape(L, nt, gt * H, P)], axis=3).transpose(1, 0, 2, 3)
    so = lambda ca: ca[1:].transpose(1, 3, 0, 2).reshape(nt, gt * P, L * H)
    c_so = jnp.concatenate([so(ca_re), so(-ca_im)], axis=1)
    prod = (ca_re[:L][:, :, None, :, :] * bb_re.transpose(0, 2, 1)[None, :, :, None, :]
            - ca_im[:L][:, :, None, :, :] * bb_im.transpose(0, 2, 1)[None, :, :, None, :])
    k_lag = jnp.sum(prod, axis=-1).transpose(1, 2, 0, 3)
    c_k = k_lag.reshape(nt, gt * H, L * H)
    w_st, w_out, toep = _s5_expand(c_st, c_so, c_k, L)

    def per_tile(v):
        return v.reshape(nt, 1, STATE_TILE)

    return w_st, w_out, toep, per_tile(pr[L]), per_tile(pi[L])


def _s5_expand_kernel(cst_ref, cso_ref, ck_ref, wst_ref, wso_ref, toep_ref, *, L):
    hp = lax.Precision.HIGHEST
    P, H = SSM_STATE, SSM_GROUP
    iota = lambda shape, d: lax.broadcasted_iota(jnp.int32, shape, d)
    one = lambda cond: jnp.where(cond, 1.0, 0.0).astype(F32)

    r, c = iota((2 * P, 2 * STATE_TILE), 0), iota((2 * P, 2 * STATE_TILE), 1)
    rep_st = one((r // P == c // STATE_TILE) & (r % P == c % P))
    r, c = iota((LANES, 2 * STATE_TILE), 0), iota((LANES, 2 * STATE_TILE), 1)
    own_st = one(r // H == (c % STATE_TILE) // P)
    for s in range(L):
        blk = jnp.dot(cst_ref[s], rep_st, precision=hp, preferred_element_type=F32) * own_st
        wst_ref[s * LANES:(s + 1) * LANES, :] = blk.astype(BF16)

    r, c = iota((LANES, LANES), 0), iota((LANES, LANES), 1)
    pick = [one((r // H == t) & (r % H == c % H)) for t in range(L)]
    own_k = one(r // H == c // H)
    r, c = iota((2 * STATE_TILE, LANES), 0), iota((2 * STATE_TILE, LANES), 1)
    own_so = one((r % STATE_TILE) // P == c // H)
    cso = cso_ref[...]
    for t in range(L):
        blk = jnp.dot(cso, pick[t], precision=hp, preferred_element_type=F32) * own_so
        wso_ref[:, t * LANES:(t + 1) * LANES] = blk.astype(BF16)
    ck = ck_ref[...]
    lag = [(jnp.dot(ck, pick[t], precision=hp, preferred_element_type=F32) * own_k).astype(BF16) for t in range(L)]
    zero = jnp.zeros((LANES, LANES), BF16)
    for s in range(L):
        for t in range(L):
            toep_ref[s * LANES:(s + 1) * LANES, t * LANES:(t + 1) * LANES] = lag[t - s] if t >= s else zero


def _s5_expand(c_st, c_so, c_k, L):
    lk = L * LANES
    st2 = 2 * STATE_TILE
    return pl.pallas_call(
        functools.partial(_s5_expand_kernel, L=L),
        grid=(N_CH_TILES,),
        in_specs=[pl.BlockSpec((None, L, LANES, 2 * SSM_STATE), lambda j: (j, 0, 0, 0)),
                  pl.BlockSpec((None, st2, L * SSM_GROUP), lambda j: (j, 0, 0)),
                  pl.BlockSpec((None, LANES, L * SSM_GROUP), lambda j: (j, 0, 0))],
        out_specs=[pl.BlockSpec((None, lk, st2), lambda j: (j, 0, 0)),
                   pl.BlockSpec((None, st2, lk), lambda j: (j, 0, 0)),
                   pl.BlockSpec((None, lk, lk), lambda j: (j, 0, 0))],
        out_shape=[jax.ShapeDtypeStruct((N_CH_TILES, lk, st2), BF16),
                   jax.ShapeDtypeStruct((N_CH_TILES, st2, lk), BF16),
                   jax.ShapeDtypeStruct((N_CH_TILES, lk, lk), BF16)],
        compiler_params=_cparams("parallel"),
        name="s5_expand_weights",
    )(c_st, c_so, c_k)


def _to_chunks(u, nb, nc, L):
    return (u.reshape(nb, nc, L, N_CH_TILES, LANES).transpose(1, 0, 3, 2, 4)
            .reshape(nc * nb, N_CH_TILES * L * LANES))


def _from_chunks(y, nb, nc, L):
    return (y.reshape(nc, nb, N_CH_TILES, L, LANES).transpose(1, 0, 3, 2, 4)
            .reshape(nb * nc * L, SSM_WIDTH))


def _s5_kernel(x_ref, h0_ref, are_ref, aim_ref, ws_ref, t_ref, wo_ref, y_ref, fin_ref,
               hr_ref, hi_ref, d_ref, hs_ref, *, cb, nb):
    ci = pl.program_id(1)

    @pl.when(ci == 0)
    def _():
        hr_ref[...] = h0_ref[:, 0:STATE_TILE]
        hi_ref[...] = h0_ref[:, STATE_TILE:2 * STATE_TILE]

    x = x_ref[...]
    d_ref[...] = _dot(x, ws_ref[...])
    ar = jnp.broadcast_to(are_ref[...], (nb, STATE_TILE))
    ai = jnp.broadcast_to(aim_ref[...], (nb, STATE_TILE))

    def body(c, carry):
        hr, hi = carry
        r0 = pl.multiple_of(c * nb, nb)
        hs_ref[pl.ds(r0, nb), 0:STATE_TILE] = hr
        hs_ref[pl.ds(r0, nb), STATE_TILE:2 * STATE_TILE] = hi
        d = d_ref[pl.ds(r0, nb), :]
        return (ar * hr - ai * hi + d[:, 0:STATE_TILE],
                ar * hi + ai * hr + d[:, STATE_TILE:2 * STATE_TILE])

    hr, hi = lax.fori_loop(0, cb, body, (hr_ref[...], hi_ref[...]))
    hr_ref[...] = hr
    hi_ref[...] = hi
    hsb = hs_ref[...].astype(BF16)
    for c0 in range(0, t_ref.shape[1], S5_PANEL):
        c1 = c0 + S5_PANEL
        y_ref[:, c0:c1] = _dot(x[:, 0:c1], t_ref[0:c1, c0:c1]) + _dot(hsb, wo_ref[:, c0:c1])

    @pl.when(ci == pl.num_programs(1) - 1)
    def _():
        fin_ref[:, 0:STATE_TILE] = hr
        fin_ref[:, STATE_TILE:2 * STATE_TILE] = hi


def _s5(ub, h0, weights, nb, nc, L, chunk_block):
    w_st, w_so, toep, a_re, a_im = weights
    xc = _to_chunks(ub, nb, nc, L)
    cb = min(chunk_block, nc)
    rows = cb * nb
    lk = L * LANES
    st2 = 2 * STATE_TILE
    tile_w = lambda shape: pl.BlockSpec((None,) + shape, lambda j, c: (j, 0, 0))
    y, fin = pl.pallas_call(
        functools.partial(_s5_kernel, cb=cb, nb=nb),
        grid=(N_CH_TILES, nc // cb),
        in_specs=[pl.BlockSpec((rows, lk), lambda j, c: (c, j)),
                  pl.BlockSpec((nb, st2), lambda j, c: (0, j)),
                  tile_w((1, STATE_TILE)), tile_w((1, STATE_TILE)),
                  tile_w((lk, st2)), tile_w((lk, lk)), tile_w((st2, lk))],
        out_specs=[pl.BlockSpec((rows, lk), lambda j, c: (c, j)),
                   pl.BlockSpec((nb, st2), lambda j, c: (0, j))],
        out_shape=[jax.ShapeDtypeStruct((nc * nb, N_CH_TILES * lk), F32),
                   jax.ShapeDtypeStruct((nb, N_CH_TILES * st2), F32)],
        scratch_shapes=[pltpu.VMEM((nb, STATE_TILE), F32), pltpu.VMEM((nb, STATE_TILE), F32),
                        pltpu.VMEM((rows, st2), F32), pltpu.VMEM((rows, st2), F32)],
        compiler_params=_cparams("parallel", "arbitrary"),
        name="s5_chunked_scan",
    )(xc, h0, a_re, a_im, w_st, toep, w_so)
    return _from_chunks(y, nb, nc, L), fin


def _state_to_tiles(h_re, h_im):
    nb = h_re.shape[0]
    r = h_re.astype(F32).reshape(nb, N_CH_TILES, STATE_TILE)
    i = h_im.astype(F32).reshape(nb, N_CH_TILES, STATE_TILE)
    return jnp.concatenate([r, i], axis=-1).reshape(nb, N_CH_TILES * 2 * STATE_TILE)


def _tiles_to_state(h):
    nb = h.shape[0]
    h = h.reshape(nb, N_CH_TILES, 2, GROUPS_PER_TILE, SSM_STATE)
    return (h[:, :, 0].reshape(nb, SSM_GROUPS, SSM_STATE), h[:, :, 1].reshape(nb, SSM_GROUPS, SSM_STATE))


def _t5_bucket(dist):
    n = np.maximum(dist, 0)
    max_exact = REL_BUCKETS // 2
    nf = np.maximum(n, 1).astype(np.float32)
    large = max_exact + (np.log(nf / np.float32(max_exact)) / np.float32(math.log(REL_MAX_DIST / max_exact))
                         * np.float32(REL_BUCKETS - max_exact)).astype(np.int32)
    large = np.minimum(large, REL_BUCKETS - 1)
    return np.where(n < max_exact, n, large)


def _rel_bias(rel_table, dist):
    bucket = _t5_bucket(dist)
    tab = rel_table.astype(F32)
    out = jnp.zeros((SWA_HEADS,) + dist.shape, F32)
    for b in range(REL_BUCKETS):
        sel = jnp.asarray(bucket == b)
        if bool((bucket == b).any()):
            out = jnp.where(sel[None], tab[b].reshape((SWA_HEADS,) + (1,) * dist.ndim), out)
    return out


def _swa_prompt_kernel(sink_ref, q_ref, kp_ref, kc_ref, vp_ref, vc_ref, bias_ref, o_ref, kk_ref, vv_ref, *, qblocks):
    step = pl.program_id(1)
    kk_ref[0:WINDOW, :] = kp_ref[...].astype(BF16)
    kk_ref[WINDOW:, :] = kc_ref[...].astype(BF16)
    vv_ref[0:WINDOW, :] = vp_ref[...].astype(BF16)
    vv_ref[WINDOW:, :] = vc_ref[...].astype(BF16)
    row = lax.broadcasted_iota(jnp.int32, (WINDOW, 2 * WINDOW), 0)
    col = lax.broadcasted_iota(jnp.int32, (WINDOW, 2 * WINDOW), 1)
    dist = row + WINDOW - col
    band = (dist >= 0) & (dist < WINDOW)
    lane = lax.broadcasted_iota(jnp.int32, (WINDOW, LANES), 1)
    low = lane < SWA_HEAD_DIM

    def block(j, carry):
        r0 = pl.multiple_of(j * WINDOW, WINDOW)
        kk = kk_ref[pl.ds(r0, 2 * WINDOW), :]
        vv = vv_ref[pl.ds(r0, 2 * WINDOW), :]
        valid = band & ((col >= WINDOW) | (step * qblocks + j > 0))
        for t in range(SWA_REP):
            q2 = q_ref[pl.ds(r0, WINDOW), t * LANES:(t + 1) * LANES]
            outs = []
            for half in range(SWA_KV_HEADS):
                h = t + SWA_REP * half
                qh = jnp.where(low if half == 0 else jnp.logical_not(low), q2, jnp.zeros_like(q2))
                s = lax.dot_general(qh, kk, _TRANS_B, preferred_element_type=F32)
                s = jnp.where(valid, s + bias_ref[h], NEG_INF)
                sink = sink_ref[h]
                m = jnp.maximum(jnp.max(s, axis=-1, keepdims=True), sink)
                e = jnp.exp(s - m)
                den = jnp.sum(e, axis=-1, keepdims=True) + jnp.exp(sink - m)
                outs.append(_dot(e.astype(BF16), vv) * (1.0 / den))
            o_ref[pl.ds(r0, WINDOW), t * LANES:(t + 1) * LANES] = jnp.where(low, outs[0], outs[1]).astype(BF16)
        return carry

    lax.fori_loop(0, qblocks, block, 0)


def _swa_prompt(q, k, v, bias, sinks, nb, t, qblocks):
    nstep = t // (WINDOW * qblocks)
    rows = WINDOW * qblocks
    cur = lambda b, i: (b * nstep + i, 0)
    prev = lambda b, i: (b * nstep * qblocks + jnp.maximum(i * qblocks - 1, 0), 0)
    return pl.pallas_call(
        functools.partial(_swa_prompt_kernel, qblocks=qblocks),
        grid=(nb, nstep),
        in_specs=[pl.BlockSpec(memory_space=pltpu.SMEM),
                  pl.BlockSpec((rows, SWA_WIDTH), cur),
                  pl.BlockSpec((WINDOW, SWA_KV_WIDTH), prev),
                  pl.BlockSpec((rows, SWA_KV_WIDTH), cur),
                  pl.BlockSpec((WINDOW, SWA_KV_WIDTH), prev),
                  pl.BlockSpec((rows, SWA_KV_WIDTH), cur),
                  pl.BlockSpec((SWA_HEADS, WINDOW, 2 * WINDOW), lambda b, i: (0, 0, 0))],
        out_specs=pl.BlockSpec((rows, SWA_WIDTH), cur),
        out_shape=jax.ShapeDtypeStruct((nb * t, SWA_WIDTH), BF16),
        scratch_shapes=[pltpu.VMEM((rows + WINDOW, SWA_KV_WIDTH), BF16),
                        pltpu.VMEM((rows + WINDOW, SWA_KV_WIDTH), BF16)],
        compiler_params=_cparams("parallel", "parallel"),
        name="swa_prompt",
    )(sinks, q, k, k, v, v, bias)


def _swa_decode_kernel(q_ref, k_ref, v_ref, bias_ref, sink_ref, o_ref, nk_ref, nv_ref, *, seqs, tq):
    rows, keys = SWA_HEADS * tq, k_ref.shape[1]
    nk_ref[...] = k_ref[:, tq:tq + WINDOW, :]
    nv_ref[...] = v_ref[:, tq:tq + WINDOW, :]
    low = lax.broadcasted_iota(jnp.int32, (tq, LANES), 1) < SWA_HEAD_DIM
    qi = lax.broadcasted_iota(jnp.int32, (rows, keys), 0) % tq
    col = lax.broadcasted_iota(jnp.int32, (rows, keys), 1)
    dist = qi + WINDOW - col
    valid = (dist >= 0) & (dist < WINDOW)
    bias = bias_ref[...]
    sink = sink_ref[...]
    for s_i in range(seqs):
        q = q_ref[s_i]
        tiles = [q[:, t * LANES:(t + 1) * LANES] for t in range(SWA_REP)]
        qh = jnp.concatenate([jnp.where(low, x, 0.0) for x in tiles]
                             + [jnp.where(low, 0.0, x) for x in tiles], axis=0)
        kk = k_ref[s_i].astype(BF16)
        s = lax.dot_general(qh.astype(BF16), kk, _TRANS_B, preferred_element_type=F32)
        s = jnp.where(valid, s + bias, NEG_INF)
        m = jnp.maximum(jnp.max(s, axis=-1, keepdims=True), sink)
        e = jnp.exp(s - m)
        den = jnp.sum(e, axis=-1, keepdims=True) + jnp.exp(sink - m)
        o = _dot(e.astype(BF16), v_ref[s_i].astype(BF16)) * (1.0 / den)
        for t in range(SWA_REP):
            o_ref[s_i, :, t * LANES:(t + 1) * LANES] = jnp.where(
                low, o[t * tq:(t + 1) * tq], o[(t + SWA_REP) * tq:(t + SWA_REP + 1) * tq])


def _swa_decode(q, k_all, v_all, bias, sink_rows, seqs):
    nseq, tq, _ = q.shape
    rows = SWA_HEADS * tq
    keys = k_all.shape[1]
    seqs = min(seqs, nseq)
    return pl.pallas_call(
        functools.partial(_swa_decode_kernel, seqs=seqs, tq=tq),
        grid=(nseq // seqs,),
        in_specs=[pl.BlockSpec((seqs, tq, SWA_WIDTH), lambda i: (i, 0, 0)),
                  pl.BlockSpec((seqs, keys, LANES), lambda i: (i, 0, 0)),
                  pl.BlockSpec((seqs, keys, LANES), lambda i: (i, 0, 0)),
                  pl.BlockSpec((rows, keys), lambda i: (0, 0)),
                  pl.BlockSpec((rows, 1), lambda i: (0, 0))],
        out_specs=[pl.BlockSpec((seqs, tq, SWA_WIDTH), lambda i: (i, 0, 0)),
                   pl.BlockSpec((seqs, WINDOW, LANES), lambda i: (i, 0, 0)),
                   pl.BlockSpec((seqs, WINDOW, LANES), lambda i: (i, 0, 0))],
        out_shape=[jax.ShapeDtypeStruct((nseq, tq, SWA_WIDTH), F32),
                   jax.ShapeDtypeStruct((nseq, WINDOW, LANES), F32),
                   jax.ShapeDtypeStruct((nseq, WINDOW, LANES), F32)],
        compiler_params=_cparams("parallel"),
        name="swa_decode",
    )(q, k_all, v_all, bias, sink_rows)


def _softmax(s):
    m = jnp.max(s, axis=-1, keepdims=True)
    e = jnp.exp(s - m)
    return e * (1.0 / jnp.sum(e, axis=-1, keepdims=True))


def _mem_prompt_kernel(q_ref, k_ref, v_ref, o_ref, s_ref, p_ref):
    scale = MEM_HEAD_DIM ** -0.5
    heads = [slice(h * MEM_HEAD_DIM, (h + 1) * MEM_HEAD_DIM) for h in range(MEM_HEADS)]
    for h, sl in enumerate(heads):
        s_ref[h] = lax.dot_general(q_ref[:, sl], k_ref[:, sl].astype(BF16), _TRANS_B, preferred_element_type=F32)
    s = s_ref[...] * scale
    e = jnp.exp(s - jnp.max(s, axis=-1, keepdims=True))
    p_ref[...] = e.astype(BF16)
    inv = 1.0 / jnp.sum(e, axis=-1, keepdims=True)
    for h, sl in enumerate(heads):
        o_ref[:, sl] = (_dot(p_ref[h], v_ref[:, sl].astype(BF16)) * inv[h]).astype(BF16)


def _mem_prompt(qm, mk, mv, nb, t, tile):
    tile = min(tile, t)
    nt = t // tile
    return pl.pallas_call(
        _mem_prompt_kernel,
        grid=(nb, nt),
        in_specs=[pl.BlockSpec((tile, MEM_WIDTH), lambda b, i: (b * nt + i, 0)),
                  pl.BlockSpec((MEM_TOKENS, MEM_WIDTH), lambda b, i: (b, 0)),
                  pl.BlockSpec((MEM_TOKENS, MEM_WIDTH), lambda b, i: (b, 0))],
        out_specs=pl.BlockSpec((tile, MEM_WIDTH), lambda b, i: (b * nt + i, 0)),
        out_shape=jax.ShapeDtypeStruct((nb * t, MEM_WIDTH), BF16),
        scratch_shapes=[pltpu.VMEM((MEM_HEADS, tile, MEM_TOKENS), F32), pltpu.VMEM((MEM_HEADS, tile, MEM_TOKENS), BF16)],
        compiler_params=_cparams("parallel", "parallel"),
        name="mem_prompt",
    )(qm, mk, mv)


def _mem_decode_kernel(q_ref, k_ref, v_ref, o_ref, *, seqs):
    tq = q_ref.shape[1]
    rows, cols = MEM_HEADS * tq, MEM_TOKENS * MEM_HEADS
    k2 = k_ref.reshape(seqs, cols, MEM_HEAD_DIM)
    v2 = v_ref.reshape(seqs, cols, MEM_HEAD_DIM)
    scale = MEM_HEAD_DIM ** -0.5
    own = (lax.broadcasted_iota(jnp.int32, (rows, cols), 1) % MEM_HEADS
           == lax.broadcasted_iota(jnp.int32, (rows, cols), 0) // tq)
    for s_i in range(seqs):
        q = q_ref[s_i]
        qb = jnp.concatenate([q[:, h * MEM_HEAD_DIM:(h + 1) * MEM_HEAD_DIM] for h in range(MEM_HEADS)], axis=0)
        s = lax.dot_general(qb.astype(BF16), k2[s_i].astype(BF16), _TRANS_B, preferred_element_type=F32) * scale
        p = _softmax(jnp.where(own, s, NEG_INF)).astype(BF16)
        o = _dot(p, v2[s_i].astype(BF16))
        for h in range(MEM_HEADS):
            o_ref[s_i, :, h * MEM_HEAD_DIM:(h + 1) * MEM_HEAD_DIM] = o[h * tq:(h + 1) * tq, :]


def _mem_decode(q, k, v, layer, seqs):
    nseq, tq, _ = q.shape
    seqs = min(seqs, nseq)
    cache = pl.BlockSpec((None, seqs, MEM_TOKENS, MEM_HEADS, MEM_HEAD_DIM), lambda i: (layer, i, 0, 0, 0))
    return pl.pallas_call(
        functools.partial(_mem_decode_kernel, seqs=seqs),
        grid=(nseq // seqs,),
        in_specs=[pl.BlockSpec((seqs, tq, MEM_WIDTH), lambda i: (i, 0, 0)), cache, cache],
        out_specs=pl.BlockSpec((seqs, tq, MEM_WIDTH), lambda i: (i, 0, 0)),
        out_shape=jax.ShapeDtypeStruct((nseq, tq, MEM_WIDTH), F32),
        compiler_params=_cparams("parallel"),
        name="mem_decode",
    )(q, k, v)


ROUTER_ROWS = 40
GATES_COL0 = SSM_WIDTH + SWA_WIDTH + 2 * SWA_KV_WIDTH + MEM_WIDTH
ROUTE_ROWS = 8
HALF = D_MODEL // 2


def _pack_halves(xb):
    hi = pltpu.bitcast(xb[:, 0:HALF].astype(F32), jnp.int32)
    lo = pltpu.bitcast(xb[:, HALF:D_MODEL].astype(F32), jnp.int32)
    return hi | lax.shift_right_logical(lo, jnp.int32(16))


def _unpack_halves(p):
    hi = pltpu.bitcast(p & jnp.int32(-65536), F32).astype(BF16)
    lo = pltpu.bitcast(lax.shift_left(p, jnp.int32(16)), F32).astype(BF16)
    return hi, lo


def _merge_kernel(x_ref, u_ref, y_ref, os_ref, om_ref, g1_ref, wg_ref, dsk_ref, wglu_ref, bglu_ref,
                  wbs_ref, wbw_ref, wbm_ref, wout_ref, g2_ref, wr_ref, br_ref,
                  h_ref, xn2_ref, route_ref, cnt_ref, base_ref, tri_ref):
    x = x_ref[...]
    tt = x.shape[0]
    xb = _rms(x, g1_ref[...]).astype(BF16)
    z = jax.nn.gelu(y_ref[...] + dsk_ref[...] * u_ref[...])
    z = z * jax.nn.sigmoid(_dot(z.astype(BF16), wglu_ref[...]) + bglu_ref[...])
    gate = lambda b: jax.nn.sigmoid(_dot(xb, wg_ref[:, GATES_COL0 + b * D_MODEL:GATES_COL0 + (b + 1) * D_MODEL]))
    merged = gate(0) * _dot(z.astype(BF16), wbs_ref[...])
    merged = merged + gate(1) * _dot(os_ref[...], wbw_ref[...])
    merged = merged + gate(2) * _dot(om_ref[...], wbm_ref[...])
    h = x + _dot(merged.astype(BF16), wout_ref[...])
    h_ref[...] = h
    xn2 = _rms(h, g2_ref[...]).astype(BF16)
    xn2_ref[...] = _pack_halves(xn2)

    lt = lax.dot_general(wr_ref[...], xn2, _TRANS_B, preferred_element_type=F32) + br_ref[...]
    gl = lt[N_EXPERTS:N_EXPERTS + N_EXPERT_GROUPS]
    ge = jnp.exp(gl - jnp.max(gl, axis=0, keepdims=True))
    gp = ge / jnp.sum(ge, axis=0, keepdims=True)
    gw = jnp.max(gp, axis=0, keepdims=True)
    gidx = jnp.full((1, tt), N_EXPERT_GROUPS - 1, jnp.int32)
    for r in range(N_EXPERT_GROUPS - 2, -1, -1):
        gidx = jnp.where(gp[r:r + 1] == gw, r, gidx)
    ein = lt[(N_EXPERT_GROUPS - 1) * EXPERTS_PER_GROUP:N_EXPERTS]
    for r in range(N_EXPERT_GROUPS - 2, -1, -1):
        ein = jnp.where(gidx == r, lt[r * EXPERTS_PER_GROUP:(r + 1) * EXPERTS_PER_GROUP], ein)
    ee = jnp.exp(ein - jnp.max(ein, axis=0, keepdims=True))
    ep = ee / jnp.sum(ee, axis=0, keepdims=True)
    rowi = lax.broadcasted_iota(jnp.int32, (EXPERTS_PER_GROUP, tt), 0)
    p1 = jnp.max(ep, axis=0, keepdims=True)
    e1 = jnp.min(jnp.where(ep == p1, rowi, EXPERTS_PER_GROUP), axis=0, keepdims=True)
    ep2 = jnp.where(rowi == e1, -1.0, ep)
    p2 = jnp.max(ep2, axis=0, keepdims=True)
    e2 = jnp.min(jnp.where(ep2 == p2, rowi, EXPERTS_PER_GROUP), axis=0, keepdims=True)
    tot = p1 + p2
    w1 = p1 / tot * gw
    w2 = p2 / tot * gw
    id1 = gidx * EXPERTS_PER_GROUP + e1
    id2 = gidx * EXPERTS_PER_GROUP + e2

    step = pl.program_id(0)

    @pl.when(step == 0)
    def _():
        base_ref[...] = jnp.zeros_like(base_ref)
        before = lax.broadcasted_iota(jnp.int32, (tt, tt), 0) < lax.broadcasted_iota(jnp.int32, (tt, tt), 1)
        tri_ref[...] = jnp.where(before, 1.0, 0.0).astype(BF16)

    r32 = lax.broadcasted_iota(jnp.int32, (N_EXPERTS, tt), 0)
    oh1 = jnp.where(r32 == id1, 1.0, 0.0)
    oh2 = jnp.where(r32 == id2, 1.0, 0.0)
    c1 = _dot(oh1.astype(BF16), tri_ref[...])
    c2 = _dot(oh2.astype(BF16), tri_ref[...])
    tot1 = jnp.sum(oh1, axis=1, keepdims=True)
    tot2 = jnp.sum(oh2, axis=1, keepdims=True)
    base = base_ref[:, 0:1]
    rank1 = jnp.sum(oh1 * (base + c1), axis=0, keepdims=True)
    rank2 = jnp.sum(oh2 * (base + tot1 + c2), axis=0, keepdims=True)
    new_base = jnp.broadcast_to(base + tot1 + tot2, base_ref.shape)
    base_ref[...] = new_base
    cnt_ref[...] = new_base
    route_ref[...] = jnp.concatenate([id1.astype(F32), id2.astype(F32), w1, w2, rank1, rank2,
                                      jnp.zeros((ROUTE_ROWS - 6, tt), F32)], axis=0)


def _merge(x, u, y, o_swa, o_mem, p, tile):
    n = x.shape[0]
    tile = min(tile, n)
    row = lambda i: (i, 0)
    const = lambda i: (0, 0)
    full = lambda a: pl.BlockSpec(a.shape, const, pipeline_mode=pl.Buffered(1))
    weights = [p['g1'], p['w_gates'], p['d_skip'], p['w_glu'], p['b_glu'], p['w_br_ssm'], p['w_br_swa'],
               p['w_br_mem'], p['w_out'], p['g2'], p['w_router'], p['b_router']]
    return pl.pallas_call(
        _merge_kernel,
        grid=(n // tile,),
        in_specs=[pl.BlockSpec((tile, D_MODEL), row), pl.BlockSpec((tile, SSM_WIDTH), row),
                  pl.BlockSpec((tile, SSM_WIDTH), row), pl.BlockSpec((tile, SWA_WIDTH), row),
                  pl.BlockSpec((tile, MEM_WIDTH), row)] + [full(w) for w in weights],
        out_specs=[pl.BlockSpec((tile, D_MODEL), row), pl.BlockSpec((tile, HALF), row),
                   pl.BlockSpec((ROUTE_ROWS, tile), lambda i: (0, i)),
                   pl.BlockSpec((N_EXPERTS, LANES), const)],
        out_shape=[jax.ShapeDtypeStruct((n, D_MODEL), F32), jax.ShapeDtypeStruct((n, HALF), jnp.int32),
                   jax.ShapeDtypeStruct((ROUTE_ROWS, n), F32), jax.ShapeDtypeStruct((N_EXPERTS, LANES), F32)],
        scratch_shapes=[pltpu.VMEM((N_EXPERTS, LANES), F32), pltpu.VMEM((tile, tile), BF16)],
        compiler_params=_cparams("arbitrary"),
        name="merge_router",
    )(x, u, y, o_swa, o_mem, *weights)


def _expert_mlp(xp, wg, wu, wd):
    hi, lo = _unpack_halves(xp)
    g = _dot(hi, wg[0:HALF, :]) + _dot(lo, wg[HALF:D_MODEL, :])
    u = _dot(hi, wu[0:HALF, :]) + _dot(lo, wu[HALF:D_MODEL, :])
    hh = jax.nn.silu(g) * u
    return _dot(hh.astype(BF16), wd[...])


def _moe_kernel(xn2_ref, rt_ref, wg_ref, wu_ref, wd_ref, h_ref, gf_ref, o_ref, acc_ref):
    e = pl.program_id(1)

    @pl.when(e == 0)
    def _():
        acc_ref[...] = jnp.zeros_like(acc_ref)

    o = _expert_mlp(xn2_ref[...], wg_ref[...].astype(BF16), wu_ref[...].astype(BF16), wd_ref[...].astype(BF16))
    ef = e.astype(F32)
    c = (jnp.where(rt_ref[:, 0:1] == ef, rt_ref[:, 2:3], 0.0)
         + jnp.where(rt_ref[:, 1:2] == ef, rt_ref[:, 3:4], 0.0))
    acc_ref[...] += c * o

    @pl.when(e == N_EXPERTS - 1)
    def _():
        o_ref[...] = _rms(h_ref[...] + acc_ref[...], gf_ref[...])


def _moe(xn2, route_t, w_g, w_u, w_d, h, gf, tile):
    n = h.shape[0]
    tile = min(tile, n)
    return pl.pallas_call(
        _moe_kernel,
        grid=(n // tile, N_EXPERTS),
        in_specs=[pl.BlockSpec((tile, HALF), lambda i, e: (i, 0)),
                  pl.BlockSpec((tile, ROUTE_ROWS), lambda i, e: (i, 0)),
                  pl.BlockSpec((None, D_MODEL, D_EXPERT), lambda i, e: (e, 0, 0)),
                  pl.BlockSpec((None, D_MODEL, D_EXPERT), lambda i, e: (e, 0, 0)),
                  pl.BlockSpec((None, D_EXPERT, D_MODEL), lambda i, e: (e, 0, 0)),
                  pl.BlockSpec((tile, D_MODEL), lambda i, e: (i, 0)),
                  pl.BlockSpec((1, D_MODEL), lambda i, e: (0, 0))],
        out_specs=pl.BlockSpec((tile, D_MODEL), lambda i, e: (i, 0)),
        out_shape=jax.ShapeDtypeStruct((n, D_MODEL), F32),
        scratch_shapes=[pltpu.VMEM((tile, D_MODEL), F32)],
        compiler_params=_cparams("parallel", "arbitrary"),
        name="moe_final_norm",
    )(xn2, route_t, w_g, w_u, w_d, h, gf)


EXPERT_ROW_TILE = 256
EXPERT_SLOTS = 4
SC_CORES = 2
SC_SUBCORES = 16
SC_WORKERS = SC_CORES * SC_SUBCORES
SC_SCATTER_ROWS = 64
SC_GATHER_ROWS = 64


def _sc_mesh():
    return plsc.VectorSubcoreMesh(core_axis_name="core", subcore_axis_name="subcore")


def _sc_scatter_pairs(x, pos, rows_out):
    n, d = x.shape
    per_w = n // SC_WORKERS
    window = min(SC_SCATTER_ROWS, per_w)

    @pl.kernel(out_type=jax.ShapeDtypeStruct((rows_out, d), x.dtype), mesh=_sc_mesh(),
               scratch_types=[pltpu.VMEM((window,), jnp.int32), pltpu.VMEM((window,), jnp.int32),
                              pltpu.VMEM((window, d), x.dtype), pltpu.SemaphoreType.DMA, pltpu.SemaphoreType.DMA,
                              pltpu.SemaphoreType.DMA])
    def scatter(x_hbm, p_hbm, o_hbm, i1_v, i2_v, rows_v, sem_a, sem_b, sem_c):
        wid = lax.axis_index("subcore") * SC_CORES + lax.axis_index("core")

        @pl.loop(0, per_w // window)
        def _(j):
            base = wid * per_w + j * window
            load_i1 = pltpu.async_copy(p_hbm.at[pl.ds(base, window)], i1_v, sem_a)
            load_i2 = pltpu.async_copy(p_hbm.at[pl.ds(n + base, window)], i2_v, sem_b)
            load_x = pltpu.async_copy(x_hbm.at[pl.ds(base, window)], rows_v, sem_c)
            load_i1.wait()
            load_i2.wait()
            load_x.wait()
            put_1 = pltpu.async_copy(rows_v, o_hbm.at[i1_v], sem_a)
            put_2 = pltpu.async_copy(rows_v, o_hbm.at[i2_v], sem_b)
            put_1.wait()
            put_2.wait()

    return scatter(x, pos)


def _sc_gather_rows(table, idx):
    m = idx.shape[0]
    d = table.shape[1]
    per_w = m // SC_WORKERS
    window = min(SC_GATHER_ROWS, per_w)

    assert per_w % (2 * window) == 0

    @pl.kernel(out_type=jax.ShapeDtypeStruct((m, d), table.dtype), mesh=_sc_mesh(),
               scratch_types=[pltpu.VMEM((window,), jnp.int32), pltpu.VMEM((window,), jnp.int32),
                              pltpu.VMEM((window, d), table.dtype), pltpu.VMEM((window, d), table.dtype),
                              pltpu.SemaphoreType.DMA, pltpu.SemaphoreType.DMA])
    def gather(t_hbm, i_hbm, o_hbm, ia_v, ib_v, ra_v, rb_v, sem_a, sem_b):
        wid = lax.axis_index("subcore") * SC_CORES + lax.axis_index("core")

        @pl.loop(0, per_w // (2 * window))
        def _(j):
            base_a = wid * per_w + j * (2 * window)
            base_b = base_a + window
            idx_a = pltpu.async_copy(i_hbm.at[pl.ds(base_a, window)], ia_v, sem_a)
            idx_b = pltpu.async_copy(i_hbm.at[pl.ds(base_b, window)], ib_v, sem_b)
            idx_a.wait()
            get_a = pltpu.async_copy(t_hbm.at[ia_v], ra_v, sem_a)
            idx_b.wait()
            get_b = pltpu.async_copy(t_hbm.at[ib_v], rb_v, sem_b)
            get_a.wait()
            put_a = pltpu.async_copy(ra_v, o_hbm.at[pl.ds(base_a, window)], sem_a)
            get_b.wait()
            put_b = pltpu.async_copy(rb_v, o_hbm.at[pl.ds(base_b, window)], sem_b)
            put_a.wait()
            put_b.wait()

    return gather(table, idx)


def _expert_tiles_kernel(start_ref, ntile_ref, x_hbm, wg_ref, wu_ref, wd_ref, o_hbm,
                         wg_s, wu_s, wd_s, x_buf, o_buf, in_sem, out_sem):
    e = pl.program_id(0)
    tm = x_buf.shape[1]
    nslot = x_buf.shape[0]
    first = start_ref[e] // tm
    ntile = ntile_ref[e]
    total = start_ref[N_EXPERTS - 1] // tm + ntile_ref[N_EXPERTS - 1]
    wg_s[...] = wg_ref[...].astype(BF16)
    wu_s[...] = wu_ref[...].astype(BF16)
    wd_s[...] = wd_ref[...].astype(BF16)

    def rows_of(g):
        return pl.ds(pl.multiple_of(g * tm, tm), tm)

    def fetch(g):
        slot = g % nslot
        return pltpu.make_async_copy(x_hbm.at[rows_of(g)], x_buf.at[slot], in_sem.at[slot])

    def flush(g):
        slot = g % nslot
        return pltpu.make_async_copy(o_buf.at[slot], o_hbm.at[rows_of(g)], out_sem.at[slot])

    @pl.when(e == 0)
    def _():
        for k in range(nslot - 1):
            @pl.when(k < total)
            def _(k=k):
                fetch(k).start()

    def tile(g, carry):
        @pl.when(g + nslot - 1 < total)
        def _():
            fetch(g + nslot - 1).start()

        fetch(g).wait()

        @pl.when(g >= nslot)
        def _():
            flush(g - nslot).wait()

        slot = g % nslot
        o_buf[slot] = _pack_halves(_expert_mlp(x_buf[slot], wg_s, wu_s, wd_s).astype(BF16))
        flush(g).start()
        return carry

    lax.fori_loop(first, first + ntile, tile, 0)

    @pl.when(e == N_EXPERTS - 1)
    def _():
        for k in range(nslot, 0, -1):
            @pl.when(total >= k)
            def _(k=k):
                flush(total - k).wait()


def _expert_tiles(starts, ntiles, xs, w_g, w_u, w_d):
    rows = xs.shape[0]
    tm = EXPERT_ROW_TILE
    weight = lambda shape: pl.BlockSpec((None,) + shape, lambda e, st, nt: (e, 0, 0))
    grid_spec = pltpu.PrefetchScalarGridSpec(
        num_scalar_prefetch=2,
        grid=(N_EXPERTS,),
        in_specs=[pl.BlockSpec(memory_space=pl.ANY),
                  weight((D_MODEL, D_EXPERT)), weight((D_MODEL, D_EXPERT)), weight((D_EXPERT, D_MODEL))],
        out_specs=pl.BlockSpec(memory_space=pl.ANY),
        scratch_shapes=[pltpu.VMEM((D_MODEL, D_EXPERT), BF16), pltpu.VMEM((D_MODEL, D_EXPERT), BF16),
                        pltpu.VMEM((D_EXPERT, D_MODEL), BF16),
                        pltpu.VMEM((EXPERT_SLOTS, tm, HALF), jnp.int32), pltpu.VMEM((EXPERT_SLOTS, tm, HALF), jnp.int32),
                        pltpu.SemaphoreType.DMA((EXPERT_SLOTS,)), pltpu.SemaphoreType.DMA((EXPERT_SLOTS,))],
    )
    return pl.pallas_call(
        _expert_tiles_kernel,
        grid_spec=grid_spec,
        out_shape=jax.ShapeDtypeStruct((rows, HALF), jnp.int32),
        compiler_params=_cparams("arbitrary"),
        name="expert_tiles",
    )(starts, ntiles, xs, w_g, w_u, w_d)


def _unpack_f32(p):
    return pltpu.bitcast(p & jnp.int32(-65536), F32), pltpu.bitcast(lax.shift_left(p, jnp.int32(16)), F32)


def _combine_kernel(h_ref, o1_ref, o2_ref, rt_ref, gf_ref, y_ref):
    w1, w2 = rt_ref[:, 2:3], rt_ref[:, 3:4]
    a_lo, a_hi = _unpack_f32(o1_ref[...])
    b_lo, b_hi = _unpack_f32(o2_ref[...])
    y_lo = h_ref[:, 0:HALF] + (w1 * a_lo + w2 * b_lo)
    y_hi = h_ref[:, HALF:D_MODEL] + (w1 * a_hi + w2 * b_hi)
    ms = (jnp.sum(y_lo * y_lo, axis=-1, keepdims=True) + jnp.sum(y_hi * y_hi, axis=-1, keepdims=True)) / D_MODEL
    inv = lax.rsqrt(ms + EPS)
    y_ref[:, 0:HALF] = (y_lo * inv) * gf_ref[:, 0:HALF]
    y_ref[:, HALF:D_MODEL] = (y_hi * inv) * gf_ref[:, HALF:D_MODEL]


def _combine(h, o12, route_t, gf, tile):
    n = h.shape[0]
    tile = min(tile, n)
    nt = n // tile
    return pl.pallas_call(
        _combine_kernel,
        grid=(nt,),
        in_specs=[pl.BlockSpec((tile, D_MODEL), lambda i: (i, 0)),
                  pl.BlockSpec((tile, HALF), lambda i: (i, 0)),
                  pl.BlockSpec((tile, HALF), lambda i: (i + nt, 0)),
                  pl.BlockSpec((tile, ROUTE_ROWS), lambda i: (i, 0)),
                  pl.BlockSpec((1, D_MODEL), lambda i: (0, 0))],
        out_specs=pl.BlockSpec((tile, D_MODEL), lambda i: (i, 0)),
        out_shape=jax.ShapeDtypeStruct((n, D_MODEL), F32),
        compiler_params=_cparams("parallel"),
        name="combine_final_norm",
    )(h, o12, o12, route_t, gf)


def _sparse_moe(xn2p, route, cnt, h, w_g, w_u, w_d, gf, run_before_experts):
    n = h.shape[0]
    tm = EXPERT_ROW_TILE
    rows = 2 * n + N_EXPERTS * tm
    rank = route[4:6].astype(jnp.int32)
    counts = cnt[:, 0].astype(jnp.int32)
    padded = (counts + tm - 1) // tm * tm
    e_idx = jnp.arange(N_EXPERTS, dtype=jnp.int32)
    starts = jnp.sum(jnp.where(e_idx[None, :] < e_idx[:, None], padded[None, :], 0), axis=1)
    ids = route[0:2].astype(jnp.int32)
    start_of = jnp.sum(jnp.where(ids[None] == e_idx[:, None, None], starts[:, None, None], 0), axis=0)
    pos = (start_of + rank).reshape(2 * n)
    xs = _sc_scatter_pairs(xn2p, pos, rows)
    xs, _ = lax.optimization_barrier((xs, run_before_experts))
    os_ = _expert_tiles(starts.astype(jnp.int32), (padded // tm).astype(jnp.int32), xs, w_g, w_u, w_d)
    o12 = _sc_gather_rows(os_, pos)
    return _combine(h, o12, route.T, gf, ROWS_COMBINE)


def _prep_in_weights(w_in):
    o = 0
    w_u = w_in[:, o:o + SSM_WIDTH]; o += SSM_WIDTH
    w_q = w_in[:, o:o + SWA_WIDTH]; o += SWA_WIDTH
    w_k = w_in[:, o:o + SWA_KV_WIDTH]; o += SWA_KV_WIDTH
    w_v = w_in[:, o:o + SWA_KV_WIDTH]; o += SWA_KV_WIDTH
    w_qm = w_in[:, o:o + MEM_WIDTH]; o += MEM_WIDTH
    assert o == GATES_COL0
    wq = (w_q * (SWA_HEAD_DIM ** -0.5)).reshape(D_MODEL, SWA_KV_HEADS, SWA_REP, SWA_HEAD_DIM)
    wq = wq.transpose(0, 2, 1, 3).reshape(D_MODEL, SWA_WIDTH)
    w_main = jnp.concatenate([w_u, wq, w_k, w_v, w_qm], axis=1).astype(BF16)
    return w_main, w_in.astype(BF16)


IN_SPLITS = (SSM_WIDTH, SWA_WIDTH, SWA_KV_WIDTH, SWA_KV_WIDTH, MEM_WIDTH)
IN_DTYPES = ((F32, BF16), (BF16,), (F32,), (F32,), (BF16,))


def kernel(x_prompt, x_sample, cache_swa_k, cache_swa_v, state_ssm_re, state_ssm_im, cache_mem_k, cache_mem_v, mem_prompt, norm1_g, w_in, lam_re, lam_im, log_dt, bm_re, bm_im, cm_re, cm_im, d_skip, w_glu, b_glu, sinks, rel_table, mem_norm_g, w_mem_kv, w_br_ssm, w_br_swa, w_br_mem, w_out, norm2_g, w_rg, b_rg, w_rexp, b_rexp, w_e_gate, w_e_up, w_e_down, final_norm_g):
    nb, t, _ = x_prompt.shape
    ns, ts, _ = x_sample.shape
    assert w_in.shape[0] == 1 and ts == S5_CHUNK and t % (WINDOW * SWA_BLOCKS_PER_STEP) == 0
    l = 0
    L = S5_CHUNK

    w_main, w_gates = _prep_in_weights(w_in[l])
    w_swa = (w_br_swa[l].reshape(SWA_KV_HEADS, SWA_REP, SWA_HEAD_DIM, D_MODEL).transpose(1, 0, 2, 3)
             .reshape(SWA_WIDTH, D_MODEL))
    pad_rows = ROUTER_ROWS - N_EXPERTS - N_EXPERT_GROUPS
    w_router = jnp.concatenate([w_rexp[l].T, w_rg[l].T, jnp.zeros((pad_rows, D_MODEL), F32)], axis=0).astype(BF16)
    b_router = jnp.concatenate([b_rexp[l], b_rg[l], jnp.zeros((pad_rows,), F32)]).reshape(ROUTER_ROWS, 1)
    mp = {
        'g1': norm1_g[l].reshape(1, D_MODEL), 'w_gates': w_gates, 'd_skip': d_skip[l].reshape(1, SSM_WIDTH),
        'w_glu': w_glu[l].astype(BF16), 'b_glu': b_glu[l].reshape(1, SSM_WIDTH),
        'w_br_ssm': w_br_ssm[l].astype(BF16), 'w_br_swa': w_swa.astype(BF16),
        'w_br_mem': w_br_mem[l].astype(BF16), 'w_out': w_out[l].astype(BF16),
        'g2': norm2_g[l].reshape(1, D_MODEL), 'w_router': w_router, 'b_router': b_router,
    }
    w_g, w_u, w_d = w_e_gate[l], w_e_up[l], w_e_down[l]
    gf = final_norm_g.reshape(1, D_MODEL)
    s5_w = _s5_weights(lam_re[l], lam_im[l], log_dt[l], bm_re[l], bm_im[l], cm_re[l], cm_im[l], L)

    bias_p = _rel_bias(rel_table, np.arange(WINDOW)[:, None] + WINDOW - np.arange(2 * WINDOW)[None, :])
    keys_s = WINDOW + 2 * ts
    bias_s = _rel_bias(rel_table, np.arange(ts)[:, None] + WINDOW - np.arange(keys_s)[None, :])
    bias_s = bias_s.reshape(SWA_HEADS * ts, keys_s)
    sink_rows = jnp.repeat(sinks[l].astype(F32), ts).reshape(SWA_HEADS * ts, 1)

    n = nb * t
    xp = x_prompt.reshape(n, D_MODEL)
    mk, mv = _norm_proj(mem_prompt.reshape(nb * MEM_TOKENS, D_MODEL), mem_norm_g[l].reshape(1, D_MODEL),
                        w_mem_kv[l].astype(BF16), (MEM_WIDTH, MEM_WIDTH), ((F32,), (F32,)), ROWS_MEM_PROJ)
    u, ub, qz, k, v, qm = _norm_proj(xp, mp['g1'], w_main, IN_SPLITS, IN_DTYPES, ROWS_NORM_PROJ)

    y_ssm, fin = _s5(ub, jnp.zeros((nb, N_CH_TILES * 2 * STATE_TILE), F32), s5_w, nb, t // L, L, S5_CHUNKS_PER_STEP)
    p_re, p_im = _tiles_to_state(fin)

    o_swa = _swa_prompt(qz, k, v, bias_p, sinks[l].astype(F32), nb, t, SWA_BLOCKS_PER_STEP)
    o_mem = _mem_prompt(qm, mk, mv, nb, t, ROWS_MEM_ATTN)
    h, xn2p, route, cnt = _merge(xp, u, y_ssm, o_swa, o_mem, mp, ROWS_MERGE)

    k4 = k.reshape(nb, t, SWA_KV_HEADS, SWA_HEAD_DIM)
    v4 = v.reshape(nb, t, SWA_KV_HEADS, SWA_HEAD_DIM)
    new_k_p, new_v_p = k4[:, -WINDOW:][None], v4[:, -WINDOW:][None]
    new_mk = mk.reshape(1, nb, MEM_TOKENS, MEM_HEADS, MEM_HEAD_DIM)
    new_mv = mv.reshape(1, nb, MEM_TOKENS, MEM_HEADS, MEM_HEAD_DIM)

    m = ns * ts
    xs = x_sample.reshape(m, D_MODEL)
    us, ubs, qzs, k_s, v_s, qms = _norm_proj(xs, mp['g1'], w_main, IN_SPLITS, IN_DTYPES, ROWS_NORM_PROJ)
    ys_ssm, fins = _s5(ubs, _state_to_tiles(state_ssm_re[l], state_ssm_im[l]), s5_w, ns, ts // L, L, S5_CHUNKS_PER_STEP)
    s_re, s_im = _tiles_to_state(fins)

    kk_all = jnp.concatenate([cache_swa_k[l].reshape(ns, WINDOW, SWA_KV_WIDTH).astype(F32),
                              k_s.reshape(ns, ts, SWA_KV_WIDTH)], axis=1)
    vv_all = jnp.concatenate([cache_swa_v[l].reshape(ns, WINDOW, SWA_KV_WIDTH).astype(F32),
                              v_s.reshape(ns, ts, SWA_KV_WIDTH)], axis=1)
    pad = jnp.zeros((ns, keys_s - WINDOW - ts, SWA_KV_WIDTH), F32)
    o_dec, roll_k, roll_v = _swa_decode(qzs.astype(F32).reshape(ns, ts, SWA_WIDTH),
                                        jnp.concatenate([kk_all, pad], axis=1),
                                        jnp.concatenate([vv_all, pad], axis=1), bias_s, sink_rows, DECODE_SEQS_PER_STEP)
    o_dec, q_mem = lax.optimization_barrier((o_dec, qms.astype(F32).reshape(ns, ts, MEM_WIDTH)))
    o_swa_s = o_dec.reshape(m, SWA_WIDTH).astype(BF16)

    o_mem_s = _mem_decode(q_mem, cache_mem_k, cache_mem_v, l, DECODE_SEQS_PER_STEP)
    o_mem_s = o_mem_s.reshape(m, MEM_WIDTH).astype(BF16)

    y_prompt = _sparse_moe(xn2p, route, cnt, h, w_g, w_u, w_d, gf, (ys_ssm, o_swa_s, o_mem_s)).reshape(nb, t, D_MODEL)
    hs_, xn2ps, routes, _ = _merge(xs, us, ys_ssm, o_swa_s, o_mem_s, mp, ROWS_MERGE)
    y_sample = _moe(xn2ps, routes.T, w_g, w_u, w_d, hs_, gf, ROWS_DENSE_MOE).reshape(ns, ts, D_MODEL)

    new_k_s = roll_k.reshape(1, ns, WINDOW, SWA_KV_HEADS, SWA_HEAD_DIM).astype(cache_swa_k.dtype)
    new_v_s = roll_v.reshape(1, ns, WINDOW, SWA_KV_HEADS, SWA_HEAD_DIM).astype(cache_swa_v.dtype)

    return (y_prompt, y_sample,
            new_k_p, new_v_p, p_re[None], p_im[None], new_mk, new_mv,
            new_k_s, new_v_s, s_re[None].astype(state_ssm_re.dtype), s_im[None].astype(state_ssm_im.dtype))
```

```python
import functools
import math

import numpy as np
import jax
import jax.numpy as jnp
from jax import lax
from jax.experimental import pallas as pl
from jax.experimental.pallas import tpu as pltpu
from jax.experimental.pallas import tpu_sc as plsc

F32 = jnp.float32
BF16 = jnp.bfloat16

D_MODEL = 1024
SSM_WIDTH = 512
SSM_GROUP = 16
SSM_GROUPS = 32
SSM_STATE = 64
SWA_HEADS = 8
SWA_KV_HEADS = 2
SWA_REP = 4
SWA_HEAD_DIM = 64
SWA_WIDTH = 512
SWA_KV_WIDTH = 128
WINDOW = 128
REL_BUCKETS = 32
REL_MAX_DIST = 128
MEM_TOKENS = 256
MEM_HEADS = 4
MEM_HEAD_DIM = 128
MEM_WIDTH = 512
N_EXPERT_GROUPS = 4
EXPERTS_PER_GROUP = 8
N_EXPERTS = 32
D_EXPERT = 256
EPS = 1e-6
NEG_INF = -1e30

LANES = 128
GROUPS_PER_TILE = LANES // SSM_GROUP
N_CH_TILES = SSM_WIDTH // LANES
STATE_TILE = GROUPS_PER_TILE * SSM_STATE
VMEM_LIMIT = 56 * 1024 * 1024
ROWS_NORM_PROJ = 1024
ROWS_MEM_PROJ = 512
ROWS_MEM_ATTN = 1024
ROWS_MERGE = 512
ROWS_COMBINE = 1024
ROWS_DENSE_MOE = 1024
S5_CHUNKS_PER_STEP = 128
SWA_BLOCKS_PER_STEP = 8
DECODE_SEQS_PER_STEP = 8
S5_CHUNK = 8
S5_PANEL = 256

_TRANS_B = (((1,), (1,)), ((), ()))


def _cparams(*sem):
    return pltpu.CompilerParams(dimension_semantics=sem, vmem_limit_bytes=VMEM_LIMIT)


def _rms(x, g):
    return (x * lax.rsqrt(jnp.mean(x * x, axis=-1, keepdims=True) + EPS)) * g


def _dot(a, b):
    return jnp.dot(a, b, preferred_element_type=F32)


def _norm_proj_kernel(x_ref, g_ref, w_ref, *out_refs, splits, dtypes):
    xb = _rms(x_ref[...], g_ref[...]).astype(BF16)
    off = 0
    outs = iter(out_refs)
    for width, dts in zip(splits, dtypes):
        r = _dot(xb, w_ref[:, off:off + width])
        for dt in dts:
            next(outs)[...] = r.astype(dt)
        off += width


def _norm_proj(x, g, w, splits, dtypes, tile):
    n, d = x.shape
    tile = min(tile, n)
    flat = [(wd, dt) for wd, dts in zip(splits, dtypes) for dt in dts]
    return pl.pallas_call(
        functools.partial(_norm_proj_kernel, splits=tuple(splits), dtypes=tuple(dtypes)),
        grid=(n // tile,),
        in_specs=[pl.BlockSpec((tile, d), lambda i: (i, 0)),
                  pl.BlockSpec((1, d), lambda i: (0, 0)),
                  pl.BlockSpec((d, sum(splits)), lambda i: (0, 0), pipeline_mode=pl.Buffered(1))],
        out_specs=[pl.BlockSpec((tile, wd), lambda i: (i, 0)) for wd, _ in flat],
        out_shape=[jax.ShapeDtypeStruct((n, wd), dt) for wd, dt in flat],
        compiler_params=_cparams("parallel"),
        name="norm_proj",
    )(x, g, w)


def _s5_weights(lam_re, lam_im, log_dt, bm_re, bm_im, cm_re, cm_im, L):
    nt, gt, P, H = N_CH_TILES, GROUPS_PER_TILE, SSM_STATE, SSM_GROUP
    lr, li = lam_re.astype(F32), lam_im.astype(F32)
    dt = jnp.exp(log_dt.astype(F32))[:, None]
    mag = jnp.exp(lr * dt)
    a_re = mag * jnp.cos(li * dt)
    a_im = mag * jnp.sin(li * dt)
    den = lr * lr + li * li
    f_re = ((a_re - 1.0) * lr + a_im * li) / den
    f_im = (a_im * lr - (a_re - 1.0) * li) / den
    br, bi = bm_re.astype(F32), bm_im.astype(F32)
    bb_re = f_re[..., None] * br - f_im[..., None] * bi
    bb_im = f_re[..., None] * bi + f_im[..., None] * br
    pr, pi = [jnp.ones_like(a_re)], [jnp.zeros_like(a_im)]
    for _ in range(L):
        pr.append(pr[-1] * a_re - pi[-1] * a_im)
        pi.append(pr[-2] * a_im + pi[-1] * a_re)
    ap_re, ap_im = jnp.stack(pr), jnp.stack(pi)
    cr, ci = cm_re.astype(F32), cm_im.astype(F32)
    ca_re = cr[None] * ap_re[:, :, None, :] - ci[None] * ap_im[:, :, None, :]
    ca_im = cr[None] * ap_im[:, :, None, :] + ci[None] * ap_re[:, :, None, :]

    rev_re = jnp.stack([pr[L - 1 - s] for s in range(L)])
    rev_im = jnp.stack([pi[L - 1 - s] for s in range(L)])
    ws_re = rev_re[..., None] * bb_re[None] - rev_im[..., None] * bb_im[None]
    ws_im = rev_re[..., None] * bb_im[None] + rev_im[..., None] * bb_re[None]
    c_st = jnp.concatenate([ws_re.transpose(0, 1, 3, 2).reshape(L, nt, gt * H, P),
                            ws_im.transpose(0, 1, 3, 2).reshape(L, nt, gt * H, P)], axis=3).transpose(1, 0, 2, 3)
    so = lambda ca: ca[1:].transpose(1, 3, 0, 2).reshape(nt, gt * P, L * H)
    c_so = jnp.concatenate([so(ca_re), so(-ca_im)], axis=1)
    prod = (ca_re[:L][:, :, None, :, :] * bb_re.transpose(0, 2, 1)[None, :, :, None, :]
            - ca_im[:L][:, :, None, :, :] * bb_im.transpose(0, 2, 1)[None, :, :, None, :])
    k_lag = jnp.sum(prod, axis=-1).transpose(1, 2, 0, 3)
    c_k = k_lag.reshape(nt, gt * H, L * H)
    w_st, w_out, toep = _s5_expand(c_st, c_so, c_k, L)

    def per_tile(v):
        return v.reshape(nt, 1, STATE_TILE)

    return w_st, w_out, toep, per_tile(pr[L]), per_tile(pi[L])


def _s5_expand_kernel(cst_ref, cso_ref, ck_ref, wst_ref, wso_ref, toep_ref, *, L):
    hp = lax.Precision.HIGHEST
    P, H = SSM_STATE, SSM_GROUP
    iota = lambda shape, d: lax.broadcasted_iota(jnp.int32, shape, d)
    one = lambda cond: jnp.where(cond, 1.0, 0.0).astype(F32)

    r, c = iota((2 * P, 2 * STATE_TILE), 0), iota((2 * P, 2 * STATE_TILE), 1)
    rep_st = one((r // P == c // STATE_TILE) & (r % P == c % P))
    r, c = iota((LANES, 2 * STATE_TILE), 0), iota((LANES, 2 * STATE_TILE), 1)
    own_st = one(r // H == (c % STATE_TILE) // P)
    for s in range(L):
        blk = jnp.dot(cst_ref[s], rep_st, precision=hp, preferred_element_type=F32) * own_st
        wst_ref[s * LANES:(s + 1) * LANES, :] = blk.astype(BF16)

    r, c = iota((LANES, LANES), 0), iota((LANES, LANES), 1)
    pick = [one((r // H == t) & (r % H == c % H)) for t in range(L)]
    own_k = one(r // H == c // H)
    r, c = iota((2 * STATE_TILE, LANES), 0), iota((2 * STATE_TILE, LANES), 1)
    own_so = one((r % STATE_TILE) // P == c // H)
    cso = cso_ref[...]
    for t in range(L):
        blk = jnp.dot(cso, pick[t], precision=hp, preferred_element_type=F32) * own_so
        wso_ref[:, t * LANES:(t + 1) * LANES] = blk.astype(BF16)
    ck = ck_ref[...]
    lag = [(jnp.dot(ck, pick[t], precision=hp, preferred_element_type=F32) * own_k).astype(BF16) for t in range(L)]
    zero = jnp.zeros((LANES, LANES), BF16)
    for s in range(L):
        for t in range(L):
            toep_ref[s * LANES:(s + 1) * LANES, t * LANES:(t + 1) * LANES] = lag[t - s] if t >= s else zero


def _s5_expand(c_st, c_so, c_k, L):
    lk = L * LANES
    st2 = 2 * STATE_TILE
    return pl.pallas_call(
        functools.partial(_s5_expand_kernel, L=L),
        grid=(N_CH_TILES,),
        in_specs=[pl.BlockSpec((None, L, LANES, 2 * SSM_STATE), lambda j: (j, 0, 0, 0)),
                  pl.BlockSpec((None, st2, L * SSM_GROUP), lambda j: (j, 0, 0)),
                  pl.BlockSpec((None, LANES, L * SSM_GROUP), lambda j: (j, 0, 0))],
        out_specs=[pl.BlockSpec((None, lk, st2), lambda j: (j, 0, 0)),
                   pl.BlockSpec((None, st2, lk), lambda j: (j, 0, 0)),
                   pl.BlockSpec((None, lk, lk), lambda j: (j, 0, 0))],
        out_shape=[jax.ShapeDtypeStruct((N_CH_TILES, lk, st2), BF16),
                   jax.ShapeDtypeStruct((N_CH_TILES, st2, lk), BF16),
                   jax.ShapeDtypeStruct((N_CH_TILES, lk, lk), BF16)],
        compiler_params=_cparams("parallel"),
        name="s5_expand_weights",
    )(c_st, c_so, c_k)


def _to_chunks(u, nb, nc, L):
    return (u.reshape(nb, nc, L, N_CH_TILES, LANES).transpose(1, 0, 3, 2, 4)
            .reshape(nc * nb, N_CH_TILES * L * LANES))


def _from_chunks(y, nb, nc, L):
    return (y.reshape(nc, nb, N_CH_TILES, L, LANES).transpose(1, 0, 3, 2, 4)
            .reshape(nb * nc * L, SSM_WIDTH))


def _s5_kernel(x_ref, h0_ref, are_ref, aim_ref, ws_ref, t_ref, wo_ref, y_ref, fin_ref,
               hr_ref, hi_ref, d_ref, hs_ref, *, cb, nb):
    ci = pl.program_id(1)

    @pl.when(ci == 0)
    def _():
        hr_ref[...] = h0_ref[:, 0:STATE_TILE]
        hi_ref[...] = h0_ref[:, STATE_TILE:2 * STATE_TILE]

    x = x_ref[...]
    d_ref[...] = _dot(x, ws_ref[...])
    ar = jnp.broadcast_to(are_ref[...], (nb, STATE_TILE))
    ai = jnp.broadcast_to(aim_ref[...], (nb, STATE_TILE))

    def body(c, carry):
        hr, hi = carry
        r0 = pl.multiple_of(c * nb, nb)
        hs_ref[pl.ds(r0, nb), 0:STATE_TILE] = hr
        hs_ref[pl.ds(r0, nb), STATE_TILE:2 * STATE_TILE] = hi
        d = d_ref[pl.ds(r0, nb), :]
        return (ar * hr - ai * hi + d[:, 0:STATE_TILE],
                ar * hi + ai * hr + d[:, STATE_TILE:2 * STATE_TILE])

    hr, hi = lax.fori_loop(0, cb, body, (hr_ref[...], hi_ref[...]))
    hr_ref[...] = hr
    hi_ref[...] = hi
    hsb = hs_ref[...].astype(BF16)
    for c0 in range(0, t_ref.shape[1], S5_PANEL):
        c1 = c0 + S5_PANEL
        y_ref[:, c0:c1] = _dot(x[:, 0:c1], t_ref[0:c1, c0:c1]) + _dot(hsb, wo_ref[:, c0:c1])

    @pl.when(ci == pl.num_programs(1) - 1)
    def _():
        fin_ref[:, 0:STATE_TILE] = hr
        fin_ref[:, STATE_TILE:2 * STATE_TILE] = hi


def _s5(ub, h0, weights, nb, nc, L, chunk_block):
    w_st, w_so, toep, a_re, a_im = weights
    xc = _to_chunks(ub, nb, nc, L)
    cb = min(chunk_block, nc)
    rows = cb * nb
    lk = L * LANES
    st2 = 2 * STATE_TILE
    tile_w = lambda shape: pl.BlockSpec((None,) + shape, lambda j, c: (j, 0, 0))
    y, fin = pl.pallas_call(
        functools.partial(_s5_kernel, cb=cb, nb=nb),
        grid=(N_CH_TILES, nc // cb),
        in_specs=[pl.BlockSpec((rows, lk), lambda j, c: (c, j)),
                  pl.BlockSpec((nb, st2), lambda j, c: (0, j)),
                  tile_w((1, STATE_TILE)), tile_w((1, STATE_TILE)),
                  tile_w((lk, st2)), tile_w((lk, lk)), tile_w((st2, lk))],
        out_specs=[pl.BlockSpec((rows, lk), lambda j, c: (c, j)),
                   pl.BlockSpec((nb, st2), lambda j, c: (0, j))],
        out_shape=[jax.ShapeDtypeStruct((nc * nb, N_CH_TILES * lk), F32),
                   jax.ShapeDtypeStruct((nb, N_CH_TILES * st2), F32)],
        scratch_shapes=[pltpu.VMEM((nb, STATE_TILE), F32), pltpu.VMEM((nb, STATE_TILE), F32),
                        pltpu.VMEM((rows, st2), F32), pltpu.VMEM((rows, st2), F32)],
        compiler_params=_cparams("parallel", "arbitrary"),
        name="s5_chunked_scan",
    )(xc, h0, a_re, a_im, w_st, toep, w_so)
    return _from_chunks(y, nb, nc, L), fin


def _state_to_tiles(h_re, h_im):
    nb = h_re.shape[0]
    r = h_re.astype(F32).reshape(nb, N_CH_TILES, STATE_TILE)
    i = h_im.astype(F32).reshape(nb, N_CH_TILES, STATE_TILE)
    return jnp.concatenate([r, i], axis=-1).reshape(nb, N_CH_TILES * 2 * STATE_TILE)


def _tiles_to_state(h):
    nb = h.shape[0]
    h = h.reshape(nb, N_CH_TILES, 2, GROUPS_PER_TILE, SSM_STATE)
    return (h[:, :, 0].reshape(nb, SSM_GROUPS, SSM_STATE), h[:, :, 1].reshape(nb, SSM_GROUPS, SSM_STATE))


def _t5_bucket(dist):
    n = np.maximum(dist, 0)
    max_exact = REL_BUCKETS // 2
    nf = np.maximum(n, 1).astype(np.float32)
    large = max_exact + (np.log(nf / np.float32(max_exact)) / np.float32(math.log(REL_MAX_DIST / max_exact))
                         * np.float32(REL_BUCKETS - max_exact)).astype(np.int32)
    large = np.minimum(large, REL_BUCKETS - 1)
    return np.where(n < max_exact, n, large)


def _rel_bias(rel_table, dist):
    bucket = _t5_bucket(dist)
    tab = rel_table.astype(F32)
    out = jnp.zeros((SWA_HEADS,) + dist.shape, F32)
    for b in range(REL_BUCKETS):
        sel = jnp.asarray(bucket == b)
        if bool((bucket == b).any()):
            out = jnp.where(sel[None], tab[b].reshape((SWA_HEADS,) + (1,) * dist.ndim), out)
    return out


def _swa_prompt_kernel(sink_ref, q_ref, kp_ref, kc_ref, vp_ref, vc_ref, bias_ref, o_ref, kk_ref, vv_ref, *, qblocks):
    step = pl.program_id(1)
    kk_ref[0:WINDOW, :] = kp_ref[...].astype(BF16)
    kk_ref[WINDOW:, :] = kc_ref[...].astype(BF16)
    vv_ref[0:WINDOW, :] = vp_ref[...].astype(BF16)
    vv_ref[WINDOW:, :] = vc_ref[...].astype(BF16)
    row = lax.broadcasted_iota(jnp.int32, (WINDOW, 2 * WINDOW), 0)
    col = lax.broadcasted_iota(jnp.int32, (WINDOW, 2 * WINDOW), 1)
    dist = row + WINDOW - col
    band = (dist >= 0) & (dist < WINDOW)
    lane = lax.broadcasted_iota(jnp.int32, (WINDOW, LANES), 1)
    low = lane < SWA_HEAD_DIM

    def block(j, carry):
        r0 = pl.multiple_of(j * WINDOW, WINDOW)
        kk = kk_ref[pl.ds(r0, 2 * WINDOW), :]
        vv = vv_ref[pl.ds(r0, 2 * WINDOW), :]
        valid = band & ((col >= WINDOW) | (step * qblocks + j > 0))
        for t in range(SWA_REP):
            q2 = q_ref[pl.ds(r0, WINDOW), t * LANES:(t + 1) * LANES]
            outs = []
            for half in range(SWA_KV_HEADS):
                h = t + SWA_REP * half
                qh = jnp.where(low if half == 0 else jnp.logical_not(low), q2, jnp.zeros_like(q2))
                s = lax.dot_general(qh, kk, _TRANS_B, preferred_element_type=F32)
                s = jnp.where(valid, s + bias_ref[h], NEG_INF)
                sink = sink_ref[h]
                m = jnp.maximum(jnp.max(s, axis=-1, keepdims=True), sink)
                e = jnp.exp(s - m)
                den = jnp.sum(e, axis=-1, keepdims=True) + jnp.exp(sink - m)
                outs.append(_dot(e.astype(BF16), vv) * (1.0 / den))
            o_ref[pl.ds(r0, WINDOW), t * LANES:(t + 1) * LANES] = jnp.where(low, outs[0], outs[1]).astype(BF16)
        return carry

    lax.fori_loop(0, qblocks, block, 0)


def _swa_prompt(q, k, v, bias, sinks, nb, t, qblocks):
    nstep = t // (WINDOW * qblocks)
    rows = WINDOW * qblocks
    cur = lambda b, i: (b * nstep + i, 0)
    prev = lambda b, i: (b * nstep * qblocks + jnp.maximum(i * qblocks - 1, 0), 0)
    return pl.pallas_call(
        functools.partial(_swa_prompt_kernel, qblocks=qblocks),
        grid=(nb, nstep),
        in_specs=[pl.BlockSpec(memory_space=pltpu.SMEM),
                  pl.BlockSpec((rows, SWA_WIDTH), cur),
                  pl.BlockSpec((WINDOW, SWA_KV_WIDTH), prev),
                  pl.BlockSpec((rows, SWA_KV_WIDTH), cur),
                  pl.BlockSpec((WINDOW, SWA_KV_WIDTH), prev),
                  pl.BlockSpec((rows, SWA_KV_WIDTH), cur),
                  pl.BlockSpec((SWA_HEADS, WINDOW, 2 * WINDOW), lambda b, i: (0, 0, 0))],
        out_specs=pl.BlockSpec((rows, SWA_WIDTH), cur),
        out_shape=jax.ShapeDtypeStruct((nb * t, SWA_WIDTH), BF16),
        scratch_shapes=[pltpu.VMEM((rows + WINDOW, SWA_KV_WIDTH), BF16),
                        pltpu.VMEM((rows + WINDOW, SWA_KV_WIDTH), BF16)],
        compiler_params=_cparams("parallel", "parallel"),
        name="swa_prompt",
    )(sinks, q, k, k, v, v, bias)


def _swa_decode_kernel(q_ref, k_ref, v_ref, bias_ref, sink_ref, o_ref, nk_ref, nv_ref, *, seqs, tq):
    rows, keys = SWA_HEADS * tq, k_ref.shape[1]
    nk_ref[...] = k_ref[:, tq:tq + WINDOW, :]
    nv_ref[...] = v_ref[:, tq:tq + WINDOW, :]
    low = lax.broadcasted_iota(jnp.int32, (tq, LANES), 1) < SWA_HEAD_DIM
    qi = lax.broadcasted_iota(jnp.int32, (rows, keys), 0) % tq
    col = lax.broadcasted_iota(jnp.int32, (rows, keys), 1)
    dist = qi + WINDOW - col
    valid = (dist >= 0) & (dist < WINDOW)
    bias = bias_ref[...]
    sink = sink_ref[...]
    for s_i in range(seqs):
        q = q_ref[s_i]
        tiles = [q[:, t * LANES:(t + 1) * LANES] for t in range(SWA_REP)]
        qh = jnp.concatenate([jnp.where(low, x, 0.0) for x in tiles]
                             + [jnp.where(low, 0.0, x) for x in tiles], axis=0)
        kk = k_ref[s_i].astype(BF16)
        s = lax.dot_general(qh.astype(BF16), kk, _TRANS_B, preferred_element_type=F32)
        s = jnp.where(valid, s + bias, NEG_INF)
        m = jnp.maximum(jnp.max(s, axis=-1, keepdims=True), sink)
        e = jnp.exp(s - m)
        den = jnp.sum(e, axis=-1, keepdims=True) + jnp.exp(sink - m)
        o = _dot(e.astype(BF16), v_ref[s_i].astype(BF16)) * (1.0 / den)
        for t in range(SWA_REP):
            o_ref[s_i, :, t * LANES:(t + 1) * LANES] = jnp.where(
                low, o[t * tq:(t + 1) * tq], o[(t + SWA_REP) * tq:(t + SWA_REP + 1) * tq])


def _swa_decode(q, k_all, v_all, bias, sink_rows, seqs):
    nseq, tq, _ = q.shape
    rows = SWA_HEADS * tq
    keys = k_all.shape[1]
    seqs = min(seqs, nseq)
    return pl.pallas_call(
        functools.partial(_swa_decode_kernel, seqs=seqs, tq=tq),
        grid=(nseq // seqs,),
        in_specs=[pl.BlockSpec((seqs, tq, SWA_WIDTH), lambda i: (i, 0, 0)),
                  pl.BlockSpec((seqs, keys, LANES), lambda i: (i, 0, 0)),
                  pl.BlockSpec((seqs, keys, LANES), lambda i: (i, 0, 0)),
                  pl.BlockSpec((rows, keys), lambda i: (0, 0)),
                  pl.BlockSpec((rows, 1), lambda i: (0, 0))],
        out_specs=[pl.BlockSpec((seqs, tq, SWA_WIDTH), lambda i: (i, 0, 0)),
                   pl.BlockSpec((seqs, WINDOW, LANES), lambda i: (i, 0, 0)),
                   pl.BlockSpec((seqs, WINDOW, LANES), lambda i: (i, 0, 0))],
        out_shape=[jax.ShapeDtypeStruct((nseq, tq, SWA_WIDTH), F32),
                   jax.ShapeDtypeStruct((nseq, WINDOW, LANES), F32),
                   jax.ShapeDtypeStruct((nseq, WINDOW, LANES), F32)],
        compiler_params=_cparams("parallel"),
        name="swa_decode",
    )(q, k_all, v_all, bias, sink_rows)


def _softmax(s):
    m = jnp.max(s, axis=-1, keepdims=True)
    e = jnp.exp(s - m)
    return e * (1.0 / jnp.sum(e, axis=-1, keepdims=True))


def _mem_prompt_kernel(q_ref, k_ref, v_ref, o_ref, s_ref, p_ref):
    scale = MEM_HEAD_DIM ** -0.5
    heads = [slice(h * MEM_HEAD_DIM, (h + 1) * MEM_HEAD_DIM) for h in range(MEM_HEADS)]
    for h, sl in enumerate(heads):
        s_ref[h] = lax.dot_general(q_ref[:, sl], k_ref[:, sl].astype(BF16), _TRANS_B, preferred_element_type=F32)
    s = s_ref[...] * scale
    e = jnp.exp(s - jnp.max(s, axis=-1, keepdims=True))
    p_ref[...] = e.astype(BF16)
    inv = 1.0 / jnp.sum(e, axis=-1, keepdims=True)
    for h, sl in enumerate(heads):
        o_ref[:, sl] = (_dot(p_ref[h], v_ref[:, sl].astype(BF16)) * inv[h]).astype(BF16)


def _mem_prompt(qm, mk, mv, nb, t, tile):
    tile = min(tile, t)
    nt = t // tile
    return pl.pallas_call(
        _mem_prompt_kernel,
        grid=(nb, nt),
        in_specs=[pl.BlockSpec((tile, MEM_WIDTH), lambda b, i: (b * nt + i, 0)),
                  pl.BlockSpec((MEM_TOKENS, MEM_WIDTH), lambda b, i: (b, 0)),
                  pl.BlockSpec((MEM_TOKENS, MEM_WIDTH), lambda b, i: (b, 0))],
        out_specs=pl.BlockSpec((tile, MEM_WIDTH), lambda b, i: (b * nt + i, 0)),
        out_shape=jax.ShapeDtypeStruct((nb * t, MEM_WIDTH), BF16),
        scratch_shapes=[pltpu.VMEM((MEM_HEADS, tile, MEM_TOKENS), F32), pltpu.VMEM((MEM_HEADS, tile, MEM_TOKENS), BF16)],
        compiler_params=_cparams("parallel", "parallel"),
        name="mem_prompt",
    )(qm, mk, mv)


def _mem_decode_kernel(q_ref, k_ref, v_ref, o_ref, *, seqs):
    tq = q_ref.shape[1]
    rows, cols = MEM_HEADS * tq, MEM_TOKENS * MEM_HEADS
    k2 = k_ref.reshape(seqs, cols, MEM_HEAD_DIM)
    v2 = v_ref.reshape(seqs, cols, MEM_HEAD_DIM)
    scale = MEM_HEAD_DIM ** -0.5
    own = (lax.broadcasted_iota(jnp.int32, (rows, cols), 1) % MEM_HEADS
           == lax.broadcasted_iota(jnp.int32, (rows, cols), 0) // tq)
    for s_i in range(seqs):
        q = q_ref[s_i]
        qb = jnp.concatenate([q[:, h * MEM_HEAD_DIM:(h + 1) * MEM_HEAD_DIM] for h in range(MEM_HEADS)], axis=0)
        s = lax.dot_general(qb.astype(BF16), k2[s_i].astype(BF16), _TRANS_B, preferred_element_type=F32) * scale
        p = _softmax(jnp.where(own, s, NEG_INF)).astype(BF16)
        o = _dot(p, v2[s_i].astype(BF16))
        for h in range(MEM_HEADS):
            o_ref[s_i, :, h * MEM_HEAD_DIM:(h + 1) * MEM_HEAD_DIM] = o[h * tq:(h + 1) * tq, :]


def _mem_decode(q, k, v, layer, seqs):
    nseq, tq, _ = q.shape
    seqs = min(seqs, nseq)
    cache = pl.BlockSpec((None, seqs, MEM_TOKENS, MEM_HEADS, MEM_HEAD_DIM), lambda i: (layer, i, 0, 0, 0))
    return pl.pallas_call(
        functools.partial(_mem_decode_kernel, seqs=seqs),
        grid=(nseq // seqs,),
        in_specs=[pl.BlockSpec((seqs, tq, MEM_WIDTH), lambda i: (i, 0, 0)), cache, cache],
        out_specs=pl.BlockSpec((seqs, tq, MEM_WIDTH), lambda i: (i, 0, 0)),
        out_shape=jax.ShapeDtypeStruct((nseq, tq, MEM_WIDTH), F32),
        compiler_params=_cparams("parallel"),
        name="mem_decode",
    )(q, k, v)


ROUTER_ROWS = 40
GATES_COL0 = SSM_WIDTH + SWA_WIDTH + 2 * SWA_KV_WIDTH + MEM_WIDTH
ROUTE_ROWS = 8
HALF = D_MODEL // 2


def _pack_halves(xb):
    hi = pltpu.bitcast(xb[:, 0:HALF].astype(F32), jnp.int32)
    lo = pltpu.bitcast(xb[:, HALF:D_MODEL].astype(F32), jnp.int32)
    return hi | lax.shift_right_logical(lo, jnp.int32(16))


def _unpack_halves(p):
    hi = pltpu.bitcast(p & jnp.int32(-65536), F32).astype(BF16)
    lo = pltpu.bitcast(lax.shift_left(p, jnp.int32(16)), F32).astype(BF16)
    return hi, lo


def _merge_kernel(x_ref, u_ref, y_ref, os_ref, om_ref, g1_ref, wg_ref, dsk_ref, wglu_ref, bglu_ref,
                  wbs_ref, wbw_ref, wbm_ref, wout_ref, g2_ref, wr_ref, br_ref,
                  h_ref, xn2_ref, route_ref, cnt_ref, base_ref, tri_ref):
    x = x_ref[...]
    tt = x.shape[0]
    xb = _rms(x, g1_ref[...]).astype(BF16)
    z = jax.nn.gelu(y_ref[...] + dsk_ref[...] * u_ref[...])
    z = z * jax.nn.sigmoid(_dot(z.astype(BF16), wglu_ref[...]) + bglu_ref[...])
    gate = lambda b: jax.nn.sigmoid(_dot(xb, wg_ref[:, GATES_COL0 + b * D_MODEL:GATES_COL0 + (b + 1) * D_MODEL]))
    merged = gate(0) * _dot(z.astype(BF16), wbs_ref[...])
    merged = merged + gate(1) * _dot(os_ref[...], wbw_ref[...])
    merged = merged + gate(2) * _dot(om_ref[...], wbm_ref[...])
    h = x + _dot(merged.astype(BF16), wout_ref[...])
    h_ref[...] = h
    xn2 = _rms(h, g2_ref[...]).astype(BF16)
    xn2_ref[...] = _pack_halves(xn2)

    lt = lax.dot_general(wr_ref[...], xn2, _TRANS_B, preferred_element_type=F32) + br_ref[...]
    gl = lt[N_EXPERTS:N_EXPERTS + N_EXPERT_GROUPS]
    ge = jnp.exp(gl - jnp.max(gl, axis=0, keepdims=True))
    gp = ge / jnp.sum(ge, axis=0, keepdims=True)
    gw = jnp.max(gp, axis=0, keepdims=True)
    gidx = jnp.full((1, tt), N_EXPERT_GROUPS - 1, jnp.int32)
    for r in range(N_EXPERT_GROUPS - 2, -1, -1):
        gidx = jnp.where(gp[r:r + 1] == gw, r, gidx)
    ein = lt[(N_EXPERT_GROUPS - 1) * EXPERTS_PER_GROUP:N_EXPERTS]
    for r in range(N_EXPERT_GROUPS - 2, -1, -1):
        ein = jnp.where(gidx == r, lt[r * EXPERTS_PER_GROUP:(r + 1) * EXPERTS_PER_GROUP], ein)
    ee = jnp.exp(ein - jnp.max(ein, axis=0, keepdims=True))
    ep = ee / jnp.sum(ee, axis=0, keepdims=True)
    rowi = lax.broadcasted_iota(jnp.int32, (EXPERTS_PER_GROUP, tt), 0)
    p1 = jnp.max(ep, axis=0, keepdims=True)
    e1 = jnp.min(jnp.where(ep == p1, rowi, EXPERTS_PER_GROUP), axis=0, keepdims=True)
    ep2 = jnp.where(rowi == e1, -1.0, ep)
    p2 = jnp.max(ep2, axis=0, keepdims=True)
    e2 = jnp.min(jnp.where(ep2 == p2, rowi, EXPERTS_PER_GROUP), axis=0, keepdims=True)
    tot = p1 + p2
    w1 = p1 / tot * gw
    w2 = p2 / tot * gw
    id1 = gidx * EXPERTS_PER_GROUP + e1
    id2 = gidx * EXPERTS_PER_GROUP + e2

    step = pl.program_id(0)

    @pl.when(step == 0)
    def _():
        base_ref[...] = jnp.zeros_like(base_ref)
        before = lax.broadcasted_iota(jnp.int32, (tt, tt), 0) < lax.broadcasted_iota(jnp.int32, (tt, tt), 1)
        tri_ref[...] = jnp.where(before, 1.0, 0.0).astype(BF16)

    r32 = lax.broadcasted_iota(jnp.int32, (N_EXPERTS, tt), 0)
    oh1 = jnp.where(r32 == id1, 1.0, 0.0)
    oh2 = jnp.where(r32 == id2, 1.0, 0.0)
    c1 = _dot(oh1.astype(BF16), tri_ref[...])
    c2 = _dot(oh2.astype(BF16), tri_ref[...])
    tot1 = jnp.sum(oh1, axis=1, keepdims=True)
    tot2 = jnp.sum(oh2, axis=1, keepdims=True)
    base = base_ref[:, 0:1]
    rank1 = jnp.sum(oh1 * (base + c1), axis=0, keepdims=True)
    rank2 = jnp.sum(oh2 * (base + tot1 + c2), axis=0, keepdims=True)
    new_base = jnp.broadcast_to(base + tot1 + tot2, base_ref.shape)
    base_ref[...] = new_base
    cnt_ref[...] = new_base
    route_ref[...] = jnp.concatenate([id1.astype(F32), id2.astype(F32), w1, w2, rank1, rank2,
                                      jnp.zeros((ROUTE_ROWS - 6, tt), F32)], axis=0)


def _merge(x, u, y, o_swa, o_mem, p, tile):
    n = x.shape[0]
    tile = min(tile, n)
    row = lambda i: (i, 0)
    const = lambda i: (0, 0)
    full = lambda a: pl.BlockSpec(a.shape, const, pipeline_mode=pl.Buffered(1))
    weights = [p['g1'], p['w_gates'], p['d_skip'], p['w_glu'], p['b_glu'], p['w_br_ssm'], p['w_br_swa'],
               p['w_br_mem'], p['w_out'], p['g2'], p['w_router'], p['b_router']]
    return pl.pallas_call(
        _merge_kernel,
        grid=(n // tile,),
        in_specs=[pl.BlockSpec((tile, D_MODEL), row), pl.BlockSpec((tile, SSM_WIDTH), row),
                  pl.BlockSpec((tile, SSM_WIDTH), row), pl.BlockSpec((tile, SWA_WIDTH), row),
                  pl.BlockSpec((tile, MEM_WIDTH), row)] + [full(w) for w in weights],
        out_specs=[pl.BlockSpec((tile, D_MODEL), row), pl.BlockSpec((tile, HALF), row),
                   pl.BlockSpec((ROUTE_ROWS, tile), lambda i: (0, i)),
                   pl.BlockSpec((N_EXPERTS, LANES), const)],
        out_shape=[jax.ShapeDtypeStruct((n, D_MODEL), F32), jax.ShapeDtypeStruct((n, HALF), jnp.int32),
                   jax.ShapeDtypeStruct((ROUTE_ROWS, n), F32), jax.ShapeDtypeStruct((N_EXPERTS, LANES), F32)],
        scratch_shapes=[pltpu.VMEM((N_EXPERTS, LANES), F32), pltpu.VMEM((tile, tile), BF16)],
        compiler_params=_cparams("arbitrary"),
        name="merge_router",
    )(x, u, y, o_swa, o_mem, *weights)


def _expert_mlp(xp, wg, wu, wd):
    hi, lo = _unpack_halves(xp)
    g = _dot(hi, wg[0:HALF, :]) + _dot(lo, wg[HALF:D_MODEL, :])
    u = _dot(hi, wu[0:HALF, :]) + _dot(lo, wu[HALF:D_MODEL, :])
    hh = jax.nn.silu(g) * u
    return _dot(hh.astype(BF16), wd[...])


def _moe_kernel(xn2_ref, rt_ref, wg_ref, wu_ref, wd_ref, h_ref, gf_ref, o_ref, acc_ref):
    e = pl.program_id(1)

    @pl.when(e == 0)
    def _():
        acc_ref[...] = jnp.zeros_like(acc_ref)

    o = _expert_mlp(xn2_ref[...], wg_ref[...].astype(BF16), wu_ref[...].astype(BF16), wd_ref[...].astype(BF16))
    ef = e.astype(F32)
    c = (jnp.where(rt_ref[:, 0:1] == ef, rt_ref[:, 2:3], 0.0)
         + jnp.where(rt_ref[:, 1:2] == ef, rt_ref[:, 3:4], 0.0))
    acc_ref[...] += c * o

    @pl.when(e == N_EXPERTS - 1)
    def _():
        o_ref[...] = _rms(h_ref[...] + acc_ref[...], gf_ref[...])


def _moe(xn2, route_t, w_g, w_u, w_d, h, gf, tile):
    n = h.shape[0]
    tile = min(tile, n)
    return pl.pallas_call(
        _moe_kernel,
        grid=(n // tile, N_EXPERTS),
        in_specs=[pl.BlockSpec((tile, HALF), lambda i, e: (i, 0)),
                  pl.BlockSpec((tile, ROUTE_ROWS), lambda i, e: (i, 0)),
                  pl.BlockSpec((None, D_MODEL, D_EXPERT), lambda i, e: (e, 0, 0)),
                  pl.BlockSpec((None, D_MODEL, D_EXPERT), lambda i, e: (e, 0, 0)),
                  pl.BlockSpec((None, D_EXPERT, D_MODEL), lambda i, e: (e, 0, 0)),
                  pl.BlockSpec((tile, D_MODEL), lambda i, e: (i, 0)),
                  pl.BlockSpec((1, D_MODEL), lambda i, e: (0, 0))],
        out_specs=pl.BlockSpec((tile, D_MODEL), lambda i, e: (i, 0)),
        out_shape=jax.ShapeDtypeStruct((n, D_MODEL), F32),
        scratch_shapes=[pltpu.VMEM((tile, D_MODEL), F32)],
        compiler_params=_cparams("parallel", "arbitrary"),
        name="moe_final_norm",
    )(xn2, route_t, w_g, w_u, w_d, h, gf)


EXPERT_ROW_TILE = 256
EXPERT_SLOTS = 4
SC_CORES = 2
SC_SUBCORES = 16
SC_WORKERS = SC_CORES * SC_SUBCORES
SC_SCATTER_ROWS = 64
SC_GATHER_ROWS = 64


def _sc_mesh():
    return plsc.VectorSubcoreMesh(core_axis_name="core", subcore_axis_name="subcore")


def _sc_scatter_pairs(x, pos, rows_out):
    n, d = x.shape
    per_w = n // SC_WORKERS
    window = min(SC_SCATTER_ROWS, per_w)

    @pl.kernel(out_type=jax.ShapeDtypeStruct((rows_out, d), x.dtype), mesh=_sc_mesh(),
               scratch_types=[pltpu.VMEM((window,), jnp.int32), pltpu.VMEM((window,), jnp.int32),
                              pltpu.VMEM((window, d), x.dtype), pltpu.SemaphoreType.DMA, pltpu.SemaphoreType.DMA,
                              pltpu.SemaphoreType.DMA])
    def scatter(x_hbm, p_hbm, o_hbm, i1_v, i2_v, rows_v, sem_a, sem_b, sem_c):
        wid = lax.axis_index("subcore") * SC_CORES + lax.axis_index("core")

        @pl.loop(0, per_w // window)
        def _(j):
            base = wid * per_w + j * window
            load_i1 = pltpu.async_copy(p_hbm.at[pl.ds(base, window)], i1_v, sem_a)
            load_i2 = pltpu.async_copy(p_hbm.at[pl.ds(n + base, window)], i2_v, sem_b)
            load_x = pltpu.async_copy(x_hbm.at[pl.ds(base, window)], rows_v, sem_c)
            load_i1.wait()
            load_i2.wait()
            load_x.wait()
            put_1 = pltpu.async_copy(rows_v, o_hbm.at[i1_v], sem_a)
            put_2 = pltpu.async_copy(rows_v, o_hbm.at[i2_v], sem_b)
            put_1.wait()
            put_2.wait()

    return scatter(x, pos)


def _sc_gather_rows(table, idx):
    m = idx.shape[0]
    d = table.shape[1]
    per_w = m // SC_WORKERS
    window = min(SC_GATHER_ROWS, per_w)

    assert per_w % (2 * window) == 0

    @pl.kernel(out_type=jax.ShapeDtypeStruct((m, d), table.dtype), mesh=_sc_mesh(),
               scratch_types=[pltpu.VMEM((window,), jnp.int32), pltpu.VMEM((window,), jnp.int32),
                              pltpu.VMEM((window, d), table.dtype), pltpu.VMEM((window, d), table.dtype),
                              pltpu.SemaphoreType.DMA, pltpu.SemaphoreType.DMA])
    def gather(t_hbm, i_hbm, o_hbm, ia_v, ib_v, ra_v, rb_v, sem_a, sem_b):
        wid = lax.axis_index("subcore") * SC_CORES + lax.axis_index("core")

        @pl.loop(0, per_w // (2 * window))
        def _(j):
            base_a = wid * per_w + j * (2 * window)
            base_b = base_a + window
            idx_a = pltpu.async_copy(i_hbm.at[pl.ds(base_a, window)], ia_v, sem_a)
            idx_b = pltpu.async_copy(i_hbm.at[pl.ds(base_b, window)], ib_v, sem_b)
            idx_a.wait()
            get_a = pltpu.async_copy(t_hbm.at[ia_v], ra_v, sem_a)
            idx_b.wait()
            get_b = pltpu.async_copy(t_hbm.at[ib_v], rb_v, sem_b)
            get_a.wait()
            put_a = pltpu.async_copy(ra_v, o_hbm.at[pl.ds(base_a, window)], sem_a)
            get_b.wait()
            put_b = pltpu.async_copy(rb_v, o_hbm.at[pl.ds(base_b, window)], sem_b)
            put_a.wait()
            put_b.wait()

    return gather(table, idx)


def _expert_tiles_kernel(start_ref, ntile_ref, x_hbm, wg_ref, wu_ref, wd_ref, o_hbm,
                         wg_s, wu_s, wd_s, x_buf, o_buf, in_sem, out_sem):
    e = pl.program_id(0)
    tm = x_buf.shape[1]
    nslot = x_buf.shape[0]
    first = start_ref[e] // tm
    ntile = ntile_ref[e]
    total = start_ref[N_EXPERTS - 1] // tm + ntile_ref[N_EXPERTS - 1]
    wg_s[...] = wg_ref[...].astype(BF16)
    wu_s[...] = wu_ref[...].astype(BF16)
    wd_s[...] = wd_ref[...].astype(BF16)

    def rows_of(g):
        return pl.ds(pl.multiple_of(g * tm, tm), tm)

    def fetch(g):
        slot = g % nslot
        return pltpu.make_async_copy(x_hbm.at[rows_of(g)], x_buf.at[slot], in_sem.at[slot])

    def flush(g):
        slot = g % nslot
        return pltpu.make_async_copy(o_buf.at[slot], o_hbm.at[rows_of(g)], out_sem.at[slot])

    @pl.when(e == 0)
    def _():
        for k in range(nslot - 1):
            @pl.when(k < total)
            def _(k=k):
                fetch(k).start()

    def tile(g, carry):
        @pl.when(g + nslot - 1 < total)
        def _():
            fetch(g + nslot - 1).start()

        fetch(g).wait()

        @pl.when(g >= nslot)
        def _():
            flush(g - nslot).wait()

        slot = g % nslot
        o_buf[slot] = _pack_halves(_expert_mlp(x_buf[slot], wg_s, wu_s, wd_s).astype(BF16))
        flush(g).start()
        return carry

    lax.fori_loop(first, first + ntile, tile, 0)

    @pl.when(e == N_EXPERTS - 1)
    def _():
        for k in range(nslot, 0, -1):
            @pl.when(total >= k)
            def _(k=k):
                flush(total - k).wait()


def _expert_tiles(starts, ntiles, xs, w_g, w_u, w_d):
    rows = xs.shape[0]
    tm = EXPERT_ROW_TILE
    weight = lambda shape: pl.BlockSpec((None,) + shape, lambda e, st, nt: (e, 0, 0))
    grid_spec = pltpu.PrefetchScalarGridSpec(
        num_scalar_prefetch=2,
        grid=(N_EXPERTS,),
        in_specs=[pl.BlockSpec(memory_space=pl.ANY),
                  weight((D_MODEL, D_EXPERT)), weight((D_MODEL, D_EXPERT)), weight((D_EXPERT, D_MODEL))],
        out_specs=pl.BlockSpec(memory_space=pl.ANY),
        scratch_shapes=[pltpu.VMEM((D_MODEL, D_EXPERT), BF16), pltpu.VMEM((D_MODEL, D_EXPERT), BF16),
                        pltpu.VMEM((D_EXPERT, D_MODEL), BF16),
                        pltpu.VMEM((EXPERT_SLOTS, tm, HALF), jnp.int32), pltpu.VMEM((EXPERT_SLOTS, tm, HALF), jnp.int32),
                        pltpu.SemaphoreType.DMA((EXPERT_SLOTS,)), pltpu.SemaphoreType.DMA((EXPERT_SLOTS,))],
    )
    return pl.pallas_call(
        _expert_tiles_kernel,
        grid_spec=grid_spec,
        out_shape=jax.ShapeDtypeStruct((rows, HALF), jnp.int32),
        compiler_params=_cparams("arbitrary"),
        name="expert_tiles",
    )(starts, ntiles, xs, w_g, w_u, w_d)


def _unpack_f32(p):
    return pltpu.bitcast(p & jnp.int32(-65536), F32), pltpu.bitcast(lax.shift_left(p, jnp.int32(16)), F32)


def _combine_kernel(h_ref, o1_ref, o2_ref, rt_ref, gf_ref, y_ref):
    w1, w2 = rt_ref[:, 2:3], rt_ref[:, 3:4]
    a_lo, a_hi = _unpack_f32(o1_ref[...])
    b_lo, b_hi = _unpack_f32(o2_ref[...])
    y_lo = h_ref[:, 0:HALF] + (w1 * a_lo + w2 * b_lo)
    y_hi = h_ref[:, HALF:D_MODEL] + (w1 * a_hi + w2 * b_hi)
    ms = (jnp.sum(y_lo * y_lo, axis=-1, keepdims=True) + jnp.sum(y_hi * y_hi, axis=-1, keepdims=True)) / D_MODEL
    inv = lax.rsqrt(ms + EPS)
    y_ref[:, 0:HALF] = (y_lo * inv) * gf_ref[:, 0:HALF]
    y_ref[:, HALF:D_MODEL] = (y_hi * inv) * gf_ref[:, HALF:D_MODEL]


def _combine(h, o12, route_t, gf, tile):
    n = h.shape[0]
    tile = min(tile, n)
    nt = n // tile
    return pl.pallas_call(
        _combine_kernel,
        grid=(nt,),
        in_specs=[pl.BlockSpec((tile, D_MODEL), lambda i: (i, 0)),
                  pl.BlockSpec((tile, HALF), lambda i: (i, 0)),
                  pl.BlockSpec((tile, HALF), lambda i: (i + nt, 0)),
                  pl.BlockSpec((tile, ROUTE_ROWS), lambda i: (i, 0)),
                  pl.BlockSpec((1, D_MODEL), lambda i: (0, 0))],
        out_specs=pl.BlockSpec((tile, D_MODEL), lambda i: (i, 0)),
        out_shape=jax.ShapeDtypeStruct((n, D_MODEL), F32),
        compiler_params=_cparams("parallel"),
        name="combine_final_norm",
    )(h, o12, o12, route_t, gf)


def _sparse_moe(xn2p, route, cnt, h, w_g, w_u, w_d, gf, run_before_experts):
    n = h.shape[0]
    tm = EXPERT_ROW_TILE
    rows = 2 * n + N_EXPERTS * tm
    rank = route[4:6].astype(jnp.int32)
    counts = cnt[:, 0].astype(jnp.int32)
    padded = (counts + tm - 1) // tm * tm
    e_idx = jnp.arange(N_EXPERTS, dtype=jnp.int32)
    starts = jnp.sum(jnp.where(e_idx[None, :] < e_idx[:, None], padded[None, :], 0), axis=1)
    ids = route[0:2].astype(jnp.int32)
    start_of = jnp.sum(jnp.where(ids[None] == e_idx[:, None, None], starts[:, None, None], 0), axis=0)
    pos = (start_of + rank).reshape(2 * n)
    xs = _sc_scatter_pairs(xn2p, pos, rows)
    xs, _ = lax.optimization_barrier((xs, run_before_experts))
    os_ = _expert_tiles(starts.astype(jnp.int32), (padded // tm).astype(jnp.int32), xs, w_g, w_u, w_d)
    o12 = _sc_gather_rows(os_, pos)
    return _combine(h, o12, route.T, gf, ROWS_COMBINE)


def _prep_in_weights(w_in):
    o = 0
    w_u = w_in[:, o:o + SSM_WIDTH]; o += SSM_WIDTH
    w_q = w_in[:, o:o + SWA_WIDTH]; o += SWA_WIDTH
    w_k = w_in[:, o:o + SWA_KV_WIDTH]; o += SWA_KV_WIDTH
    w_v = w_in[:, o:o + SWA_KV_WIDTH]; o += SWA_KV_WIDTH
    w_qm = w_in[:, o:o + MEM_WIDTH]; o += MEM_WIDTH
    assert o == GATES_COL0
    wq = (w_q * (SWA_HEAD_DIM ** -0.5)).reshape(D_MODEL, SWA_KV_HEADS, SWA_REP, SWA_HEAD_DIM)
    wq = wq.transpose(0, 2, 1, 3).reshape(D_MODEL, SWA_WIDTH)
    w_main = jnp.concatenate([w_u, wq, w_k, w_v, w_qm], axis=1).astype(BF16)
    return w_main, w_in.astype(BF16)


IN_SPLITS = (SSM_WIDTH, SWA_WIDTH, SWA_KV_WIDTH, SWA_KV_WIDTH, MEM_WIDTH)
IN_DTYPES = ((F32, BF16), (BF16,), (F32,), (F32,), (BF16,))


def kernel(x_prompt, x_sample, cache_swa_k, cache_swa_v, state_ssm_re, state_ssm_im, cache_mem_k, cache_mem_v, mem_prompt, norm1_g, w_in, lam_re, lam_im, log_dt, bm_re, bm_im, cm_re, cm_im, d_skip, w_glu, b_glu, sinks, rel_table, mem_norm_g, w_mem_kv, w_br_ssm, w_br_swa, w_br_mem, w_out, norm2_g, w_rg, b_rg, w_rexp, b_rexp, w_e_gate, w_e_up, w_e_down, final_norm_g):
    nb, t, _ = x_prompt.shape
    ns, ts, _ = x_sample.shape
    assert w_in.shape[0] == 1 and ts == S5_CHUNK and t % (WINDOW * SWA_BLOCKS_PER_STEP) == 0
    l = 0
    L = S5_CHUNK

    w_main, w_gates = _prep_in_weights(w_in[l])
    w_swa = (w_br_swa[l].reshape(SWA_KV_HEADS, SWA_REP, SWA_HEAD_DIM, D_MODEL).transpose(1, 0, 2, 3)
             .reshape(SWA_WIDTH, D_MODEL))
    pad_rows = ROUTER_ROWS - N_EXPERTS - N_EXPERT_GROUPS
    w_router = jnp.concatenate([w_rexp[l].T, w_rg[l].T, jnp.zeros((pad_rows, D_MODEL), F32)], axis=0).astype(BF16)
    b_router = jnp.concatenate([b_rexp[l], b_rg[l], jnp.zeros((pad_rows,), F32)]).reshape(ROUTER_ROWS, 1)
    mp = {
        'g1': norm1_g[l].reshape(1, D_MODEL), 'w_gates': w_gates, 'd_skip': d_skip[l].reshape(1, SSM_WIDTH),
        'w_glu': w_glu[l].astype(BF16), 'b_glu': b_glu[l].reshape(1, SSM_WIDTH),
        'w_br_ssm': w_br_ssm[l].astype(BF16), 'w_br_swa': w_swa.astype(BF16),
        'w_br_mem': w_br_mem[l].astype(BF16), 'w_out': w_out[l].astype(BF16),
        'g2': norm2_g[l].reshape(1, D_MODEL), 'w_router': w_router, 'b_router': b_router,
    }
    w_g, w_u, w_d = w_e_gate[l], w_e_up[l], w_e_down[l]
    gf = final_norm_g.reshape(1, D_MODEL)
    s5_w = _s5_weights(lam_re[l], lam_im[l], log_dt[l], bm_re[l], bm_im[l], cm_re[l], cm_im[l], L)

    bias_p = _rel_bias(rel_table, np.arange(WINDOW)[:, None] + WINDOW - np.arange(2 * WINDOW)[None, :])
    keys_s = WINDOW + 2 * ts
    bias_s = _rel_bias(rel_table, np.arange(ts)[:, None] + WINDOW - np.arange(keys_s)[None, :])
    bias_s = bias_s.reshape(SWA_HEADS * ts, keys_s)
    sink_rows = jnp.repeat(sinks[l].astype(F32), ts).reshape(SWA_HEADS * ts, 1)

    n = nb * t
    xp = x_prompt.reshape(n, D_MODEL)
    mk, mv = _norm_proj(mem_prompt.reshape(nb * MEM_TOKENS, D_MODEL), mem_norm_g[l].reshape(1, D_MODEL),
                        w_mem_kv[l].astype(BF16), (MEM_WIDTH, MEM_WIDTH), ((F32,), (F32,)), ROWS_MEM_PROJ)
    u, ub, qz, k, v, qm = _norm_proj(xp, mp['g1'], w_main, IN_SPLITS, IN_DTYPES, ROWS_NORM_PROJ)

    y_ssm, fin = _s5(ub, jnp.zeros((nb, N_CH_TILES * 2 * STATE_TILE), F32), s5_w, nb, t // L, L, S5_CHUNKS_PER_STEP)
    p_re, p_im = _tiles_to_state(fin)

    o_swa = _swa_prompt(qz, k, v, bias_p, sinks[l].astype(F32), nb, t, SWA_BLOCKS_PER_STEP)
    o_mem = _mem_prompt(qm, mk, mv, nb, t, ROWS_MEM_ATTN)
    h, xn2p, route, cnt = _merge(xp, u, y_ssm, o_swa, o_mem, mp, ROWS_MERGE)

    k4 = k.reshape(nb, t, SWA_KV_HEADS, SWA_HEAD_DIM)
    v4 = v.reshape(nb, t, SWA_KV_HEADS, SWA_HEAD_DIM)
    new_k_p, new_v_p = k4[:, -WINDOW:][None], v4[:, -WINDOW:][None]
    new_mk = mk.reshape(1, nb, MEM_TOKENS, MEM_HEADS, MEM_HEAD_DIM)
    new_mv = mv.reshape(1, nb, MEM_TOKENS, MEM_HEADS, MEM_HEAD_DIM)

    m = ns * ts
    xs = x_sample.reshape(m, D_MODEL)
    us, ubs, qzs, k_s, v_s, qms = _norm_proj(xs, mp['g1'], w_main, IN_SPLITS, IN_DTYPES, ROWS_NORM_PROJ)
    ys_ssm, fins = _s5(ubs, _state_to_tiles(state_ssm_re[l], state_ssm_im[l]), s5_w, ns, ts // L, L, S5_CHUNKS_PER_STEP)
    s_re, s_im = _tiles_to_state(fins)

    kk_all = jnp.concatenate([cache_swa_k[l].reshape(ns, WINDOW, SWA_KV_WIDTH).astype(F32),
                              k_s.reshape(ns, ts, SWA_KV_WIDTH)], axis=1)
    vv_all = jnp.concatenate([cache_swa_v[l].reshape(ns, WINDOW, SWA_KV_WIDTH).astype(F32),
                              v_s.reshape(ns, ts, SWA_KV_WIDTH)], axis=1)
    pad = jnp.zeros((ns, keys_s - WINDOW - ts, SWA_KV_WIDTH), F32)
    o_dec, roll_k, roll_v = _swa_decode(qzs.astype(F32).reshape(ns, ts, SWA_WIDTH),
                                        jnp.concatenate([kk_all, pad], axis=1),
                                        jnp.concatenate([vv_all, pad], axis=1), bias_s, sink_rows, DECODE_SEQS_PER_STEP)
    o_dec, q_mem = lax.optimization_barrier((o_dec, qms.astype(F32).reshape(ns, ts, MEM_WIDTH)))
    o_swa_s = o_dec.reshape(m, SWA_WIDTH).astype(BF16)

    o_mem_s = _mem_decode(q_mem, cache_mem_k, cache_mem_v, l, DECODE_SEQS_PER_STEP)
    o_mem_s = o_mem_s.reshape(m, MEM_WIDTH).astype(BF16)

    y_prompt = _sparse_moe(xn2p, route, cnt, h, w_g, w_u, w_d, gf, (ys_ssm, o_swa_s, o_mem_s)).reshape(nb, t, D_MODEL)
    hs_, xn2ps, routes, _ = _merge(xs, us, ys_ssm, o_swa_s, o_mem_s, mp, ROWS_MERGE)
    y_sample = _moe(xn2ps, routes.T, w_g, w_u, w_d, hs_, gf, ROWS_DENSE_MOE).reshape(ns, ts, D_MODEL)

    new_k_s = roll_k.reshape(1, ns, WINDOW, SWA_KV_HEADS, SWA_HEAD_DIM).astype(cache_swa_k.dtype)
    new_v_s = roll_v.reshape(1, ns, WINDOW, SWA_KV_HEADS, SWA_HEAD_DIM).astype(cache_swa_v.dtype)

    return (y_prompt, y_sample,
            new_k_p, new_v_p, p_re[None], p_im[None], new_mk, new_mv,
            new_k_s, new_v_s, s_re[None].astype(state_ssm_re.dtype), s_im[None].astype(state_ssm_im.dtype))
```

```python
import functools
import math

import numpy as np
import jax
import jax.numpy as jnp
from jax import lax
from jax.experimental import pallas as pl
from jax.experimental.pallas import tpu as pltpu
from jax.experimental.pallas import tpu_sc as plsc

F32 = jnp.float32
BF16 = jnp.bfloat16

D_MODEL = 1024
SSM_WIDTH = 512
SSM_GROUP = 16
SSM_GROUPS = 32
SSM_STATE = 64
SWA_HEADS = 8
SWA_KV_HEADS = 2
SWA_REP = 4
SWA_HEAD_DIM = 64
SWA_WIDTH = 512
SWA_KV_WIDTH = 128
WINDOW = 128
REL_BUCKETS = 32
REL_MAX_DIST = 128
MEM_TOKENS = 256
MEM_HEADS = 4
MEM_HEAD_DIM = 128
MEM_WIDTH = 512
N_EXPERT_GROUPS = 4
EXPERTS_PER_GROUP = 8
N_EXPERTS = 32
D_EXPERT = 256
EPS = 1e-6
NEG_INF = -1e30

LANES = 128
GROUPS_PER_TILE = LANES // SSM_GROUP
N_CH_TILES = SSM_WIDTH // LANES
STATE_TILE = GROUPS_PER_TILE * SSM_STATE
VMEM_LIMIT = 56 * 1024 * 1024
ROWS_NORM_PROJ = 2048
ROWS_MEM_PROJ = 512
ROWS_MEM_ATTN = 2048
ROWS_MERGE = 512
ROWS_COMBINE = 1024
ROWS_DENSE_MOE = 1024
S5_CHUNKS_PER_STEP = 128
SWA_BLOCKS_PER_STEP = 16
DECODE_SEQS_PER_STEP = 8
S5_CHUNK = 8
S5_PANEL = 256

_TRANS_B = (((1,), (1,)), ((), ()))


def _cparams(*sem):
    return pltpu.CompilerParams(dimension_semantics=sem, vmem_limit_bytes=VMEM_LIMIT)


def _rms(x, g):
    return (x * lax.rsqrt(jnp.mean(x * x, axis=-1, keepdims=True) + EPS)) * g


def _dot(a, b):
    return jnp.dot(a, b, preferred_element_type=F32)


def _norm_proj_kernel(x_ref, g_ref, w_ref, *out_refs, splits, dtypes):
    xb = _rms(x_ref[...], g_ref[...]).astype(BF16)
    off = 0
    outs = iter(out_refs)
    for width, dts in zip(splits, dtypes):
        r = _dot(xb, w_ref[:, off:off + width])
        for dt in dts:
            next(outs)[...] = r.astype(dt)
        off += width


def _norm_proj(x, g, w, splits, dtypes, tile):
    n, d = x.shape
    tile = min(tile, n)
    flat = [(wd, dt) for wd, dts in zip(splits, dtypes) for dt in dts]
    return pl.pallas_call(
        functools.partial(_norm_proj_kernel, splits=tuple(splits), dtypes=tuple(dtypes)),
        grid=(n // tile,),
        in_specs=[pl.BlockSpec((tile, d), lambda i: (i, 0)),
                  pl.BlockSpec((1, d), lambda i: (0, 0)),
                  pl.BlockSpec((d, sum(splits)), lambda i: (0, 0), pipeline_mode=pl.Buffered(1))],
        out_specs=[pl.BlockSpec((tile, wd), lambda i: (i, 0)) for wd, _ in flat],
        out_shape=[jax.ShapeDtypeStruct((n, wd), dt) for wd, dt in flat],
        compiler_params=_cparams("parallel"),
        name="norm_proj",
    )(x, g, w)


def _s5_weights(lam_re, lam_im, log_dt, bm_re, bm_im, cm_re, cm_im, L):
    nt, gt, P, H = N_CH_TILES, GROUPS_PER_TILE, SSM_STATE, SSM_GROUP
    lr, li = lam_re.astype(F32), lam_im.astype(F32)
    dt = jnp.exp(log_dt.astype(F32))[:, None]
    mag = jnp.exp(lr * dt)
    a_re = mag * jnp.cos(li * dt)
    a_im = mag * jnp.sin(li * dt)
    den = lr * lr + li * li
    f_re = ((a_re - 1.0) * lr + a_im * li) / den
    f_im = (a_im * lr - (a_re - 1.0) * li) / den
    br, bi = bm_re.astype(F32), bm_im.astype(F32)
    bb_re = f_re[..., None] * br - f_im[..., None] * bi
    bb_im = f_re[..., None] * bi + f_im[..., None] * br
    pr, pi = [jnp.ones_like(a_re)], [jnp.zeros_like(a_im)]
    for _ in range(L):
        pr.append(pr[-1] * a_re - pi[-1] * a_im)
        pi.append(pr[-2] * a_im + pi[-1] * a_re)
    ap_re, ap_im = jnp.stack(pr), jnp.stack(pi)
    cr, ci = cm_re.astype(F32), cm_im.astype(F32)
    ca_re = cr[None] * ap_re[:, :, None, :] - ci[None] * ap_im[:, :, None, :]
    ca_im = cr[None] * ap_im[:, :, None, :] + ci[None] * ap_re[:, :, None, :]

    rev_re = jnp.stack([pr[L - 1 - s] for s in range(L)])
    rev_im = jnp.stack([pi[L - 1 - s] for s in range(L)])
    ws_re = rev_re[..., None] * bb_re[None] - rev_im[..., None] * bb_im[None]
    ws_im = rev_re[..., None] * bb_im[None] + rev_im[..., None] * bb_re[None]
    c_st = jnp.concatenate([ws_re.transpose(0, 1, 3, 2).reshape(L, nt, gt * H, P),
                            ws_im.transpose(0, 1, 3, 2).reshape(L, nt, gt * H, P)], axis=3).transpose(1, 0, 2, 3)
    so = lambda ca: ca[1:].transpose(1, 3, 0, 2).reshape(nt, gt * P, L * H)
    c_so = jnp.concatenate([so(ca_re), so(-ca_im)], axis=1)
    prod = (ca_re[:L][:, :, None, :, :] * bb_re.transpose(0, 2, 1)[None, :, :, None, :]
            - ca_im[:L][:, :, None, :, :] * bb_im.transpose(0, 2, 1)[None, :, :, None, :])
    k_lag = jnp.sum(prod, axis=-1).transpose(1, 2, 0, 3)
    c_k = k_lag.reshape(nt, gt * H, L * H)
    w_st, w_out, toep = _s5_expand(c_st, c_so, c_k, L)

    def per_tile(v):
        return v.reshape(nt, 1, STATE_TILE)

    return w_st, w_out, toep, per_tile(pr[L]), per_tile(pi[L])


def _s5_expand_kernel(cst_ref, cso_ref, ck_ref, wst_ref, wso_ref, toep_ref, *, L):
    hp = lax.Precision.HIGHEST
    P, H = SSM_STATE, SSM_GROUP
    iota = lambda shape, d: lax.broadcasted_iota(jnp.int32, shape, d)
    one = lambda cond: jnp.where(cond, 1.0, 0.0).astype(F32)

    r, c = iota((2 * P, 2 * STATE_TILE), 0), iota((2 * P, 2 * STATE_TILE), 1)
    rep_st = one((r // P == c // STATE_TILE) & (r % P == c % P))
    r, c = iota((LANES, 2 * STATE_TILE), 0), iota((LANES, 2 * STATE_TILE), 1)
    own_st = one(r // H == (c % STATE_TILE) // P)
    for s in range(L):
        blk = jnp.dot(cst_ref[s], rep_st, precision=hp, preferred_element_type=F32) * own_st
        wst_ref[s * LANES:(s + 1) * LANES, :] = blk.astype(BF16)

    r, c = iota((LANES, LANES), 0), iota((LANES, LANES), 1)
    pick = [one((r // H == t) & (r % H == c % H)) for t in range(L)]
    own_k = one(r // H == c // H)
    r, c = iota((2 * STATE_TILE, LANES), 0), iota((2 * STATE_TILE, LANES), 1)
    own_so = one((r % STATE_TILE) // P == c // H)
    cso = cso_ref[...]
    for t in range(L):
        blk = jnp.dot(cso, pick[t], precision=hp, preferred_element_type=F32) * own_so
        wso_ref[:, t * LANES:(t + 1) * LANES] = blk.astype(BF16)
    ck = ck_ref[...]
    lag = [(jnp.dot(ck, pick[t], precision=hp, preferred_element_type=F32) * own_k).astype(BF16) for t in range(L)]
    zero = jnp.zeros((LANES, LANES), BF16)
    for s in range(L):
        for t in range(L):
            toep_ref[s * LANES:(s + 1) * LANES, t * LANES:(t + 1) * LANES] = lag[t - s] if t >= s else zero


def _s5_expand(c_st, c_so, c_k, L):
    lk = L * LANES
    st2 = 2 * STATE_TILE
    return pl.pallas_call(
        functools.partial(_s5_expand_kernel, L=L),
        grid=(N_CH_TILES,),
        in_specs=[pl.BlockSpec((None, L, LANES, 2 * SSM_STATE), lambda j: (j, 0, 0, 0)),
                  pl.BlockSpec((None, st2, L * SSM_GROUP), lambda j: (j, 0, 0)),
                  pl.BlockSpec((None, LANES, L * SSM_GROUP), lambda j: (j, 0, 0))],
        out_specs=[pl.BlockSpec((None, lk, st2), lambda j: (j, 0, 0)),
                   pl.BlockSpec((None, st2, lk), lambda j: (j, 0, 0)),
                   pl.BlockSpec((None, lk, lk), lambda j: (j, 0, 0))],
        out_shape=[jax.ShapeDtypeStruct((N_CH_TILES, lk, st2), BF16),
                   jax.ShapeDtypeStruct((N_CH_TILES, st2, lk), BF16),
                   jax.ShapeDtypeStruct((N_CH_TILES, lk, lk), BF16)],
        compiler_params=_cparams("parallel"),
        name="s5_expand_weights",
    )(c_st, c_so, c_k)


def _to_chunks(u, nb, nc, L):
    return (u.reshape(nb, nc, L, N_CH_TILES, LANES).transpose(1, 0, 3, 2, 4)
            .reshape(nc * nb, N_CH_TILES * L * LANES))


def _from_chunks(y, nb, nc, L):
    return (y.reshape(nc, nb, N_CH_TILES, L, LANES).transpose(1, 0, 3, 2, 4)
            .reshape(nb * nc * L, SSM_WIDTH))


def _s5_kernel(x_ref, h0_ref, are_ref, aim_ref, ws_ref, t_ref, wo_ref, y_ref, fin_ref,
               hr_ref, hi_ref, d_ref, hs_ref, *, cb, nb):
    ci = pl.program_id(1)

    @pl.when(ci == 0)
    def _():
        hr_ref[...] = h0_ref[:, 0:STATE_TILE]
        hi_ref[...] = h0_ref[:, STATE_TILE:2 * STATE_TILE]

    x = x_ref[...]
    d_ref[...] = _dot(x, ws_ref[...])
    ar = jnp.broadcast_to(are_ref[...], (nb, STATE_TILE))
    ai = jnp.broadcast_to(aim_ref[...], (nb, STATE_TILE))

    def body(c, carry):
        hr, hi = carry
        r0 = pl.multiple_of(c * nb, nb)
        hs_ref[pl.ds(r0, nb), 0:STATE_TILE] = hr
        hs_ref[pl.ds(r0, nb), STATE_TILE:2 * STATE_TILE] = hi
        d = d_ref[pl.ds(r0, nb), :]
        return (ar * hr - ai * hi + d[:, 0:STATE_TILE],
                ar * hi + ai * hr + d[:, STATE_TILE:2 * STATE_TILE])

    hr, hi = lax.fori_loop(0, cb, body, (hr_ref[...], hi_ref[...]))
    hr_ref[...] = hr
    hi_ref[...] = hi
    hsb = hs_ref[...].astype(BF16)
    for c0 in range(0, t_ref.shape[1], S5_PANEL):
        c1 = c0 + S5_PANEL
        y_ref[:, c0:c1] = _dot(x[:, 0:c1], t_ref[0:c1, c0:c1]) + _dot(hsb, wo_ref[:, c0:c1])

    @pl.when(ci == pl.num_programs(1) - 1)
    def _():
        fin_ref[:, 0:STATE_TILE] = hr
        fin_ref[:, STATE_TILE:2 * STATE_TILE] = hi


def _s5(ub, h0, weights, nb, nc, L, chunk_block):
    w_st, w_so, toep, a_re, a_im = weights
    xc = _to_chunks(ub, nb, nc, L)
    cb = min(chunk_block, nc)
    rows = cb * nb
    lk = L * LANES
    st2 = 2 * STATE_TILE
    tile_w = lambda shape: pl.BlockSpec((None,) + shape, lambda j, c: (j, 0, 0))
    y, fin = pl.pallas_call(
        functools.partial(_s5_kernel, cb=cb, nb=nb),
        grid=(N_CH_TILES, nc // cb),
        in_specs=[pl.BlockSpec((rows, lk), lambda j, c: (c, j)),
                  pl.BlockSpec((nb, st2), lambda j, c: (0, j)),
                  tile_w((1, STATE_TILE)), tile_w((1, STATE_TILE)),
                  tile_w((lk, st2)), tile_w((lk, lk)), tile_w((st2, lk))],
        out_specs=[pl.BlockSpec((rows, lk), lambda j, c: (c, j)),
                   pl.BlockSpec((nb, st2), lambda j, c: (0, j))],
        out_shape=[jax.ShapeDtypeStruct((nc * nb, N_CH_TILES * lk), F32),
                   jax.ShapeDtypeStruct((nb, N_CH_TILES * st2), F32)],
        scratch_shapes=[pltpu.VMEM((nb, STATE_TILE), F32), pltpu.VMEM((nb, STATE_TILE), F32),
                        pltpu.VMEM((rows, st2), F32), pltpu.VMEM((rows, st2), F32)],
        compiler_params=_cparams("parallel", "arbitrary"),
        name="s5_chunked_scan",
    )(xc, h0, a_re, a_im, w_st, toep, w_so)
    return _from_chunks(y, nb, nc, L), fin


def _state_to_tiles(h_re, h_im):
    nb = h_re.shape[0]
    r = h_re.astype(F32).reshape(nb, N_CH_TILES, STATE_TILE)
    i = h_im.astype(F32).reshape(nb, N_CH_TILES, STATE_TILE)
    return jnp.concatenate([r, i], axis=-1).reshape(nb, N_CH_TILES * 2 * STATE_TILE)


def _tiles_to_state(h):
    nb = h.shape[0]
    h = h.reshape(nb, N_CH_TILES, 2, GROUPS_PER_TILE, SSM_STATE)
    return (h[:, :, 0].reshape(nb, SSM_GROUPS, SSM_STATE), h[:, :, 1].reshape(nb, SSM_GROUPS, SSM_STATE))


def _t5_bucket(dist):
    n = np.maximum(dist, 0)
    max_exact = REL_BUCKETS // 2
    nf = np.maximum(n, 1).astype(np.float32)
    large = max_exact + (np.log(nf / np.float32(max_exact)) / np.float32(math.log(REL_MAX_DIST / max_exact))
                         * np.float32(REL_BUCKETS - max_exact)).astype(np.int32)
    large = np.minimum(large, REL_BUCKETS - 1)
    return np.where(n < max_exact, n, large)


def _rel_bias(rel_table, dist):
    bucket = _t5_bucket(dist)
    tab = rel_table.astype(F32)
    out = jnp.zeros((SWA_HEADS,) + dist.shape, F32)
    for b in range(REL_BUCKETS):
        sel = jnp.asarray(bucket == b)
        if bool((bucket == b).any()):
            out = jnp.where(sel[None], tab[b].reshape((SWA_HEADS,) + (1,) * dist.ndim), out)
    return out


def _swa_prompt_kernel(sink_ref, q_ref, kp_ref, kc_ref, vp_ref, vc_ref, bias_ref, o_ref, kk_ref, vv_ref, *, qblocks):
    step = pl.program_id(1)
    kk_ref[0:WINDOW, :] = kp_ref[...].astype(BF16)
    kk_ref[WINDOW:, :] = kc_ref[...].astype(BF16)
    vv_ref[0:WINDOW, :] = vp_ref[...].astype(BF16)
    vv_ref[WINDOW:, :] = vc_ref[...].astype(BF16)
    row = lax.broadcasted_iota(jnp.int32, (WINDOW, 2 * WINDOW), 0)
    col = lax.broadcasted_iota(jnp.int32, (WINDOW, 2 * WINDOW), 1)
    dist = row + WINDOW - col
    band = (dist >= 0) & (dist < WINDOW)
    lane = lax.broadcasted_iota(jnp.int32, (WINDOW, LANES), 1)
    low = lane < SWA_HEAD_DIM

    def block(j, carry):
        r0 = pl.multiple_of(j * WINDOW, WINDOW)
        kk = kk_ref[pl.ds(r0, 2 * WINDOW), :]
        vv = vv_ref[pl.ds(r0, 2 * WINDOW), :]
        valid = band & ((col >= WINDOW) | (step * qblocks + j > 0))
        for t in range(SWA_REP):
            q2 = q_ref[pl.ds(r0, WINDOW), t * LANES:(t + 1) * LANES]
            outs = []
            for half in range(SWA_KV_HEADS):
                h = t + SWA_REP * half
                qh = jnp.where(low if half == 0 else jnp.logical_not(low), q2, jnp.zeros_like(q2))
                s = lax.dot_general(qh, kk, _TRANS_B, preferred_element_type=F32)
                s = jnp.where(valid, s + bias_ref[h], NEG_INF)
                sink = sink_ref[h]
                m = jnp.maximum(jnp.max(s, axis=-1, keepdims=True), sink)
                e = jnp.exp(s - m)
                den = jnp.sum(e, axis=-1, keepdims=True) + jnp.exp(sink - m)
                outs.append(_dot(e.astype(BF16), vv) * (1.0 / den))
            o_ref[pl.ds(r0, WINDOW), t * LANES:(t + 1) * LANES] = jnp.where(low, outs[0], outs[1]).astype(BF16)
        return carry

    lax.fori_loop(0, qblocks, block, 0)


def _swa_prompt(q, k, v, bias, sinks, nb, t, qblocks):
    nstep = t // (WINDOW * qblocks)
    rows = WINDOW * qblocks
    cur = lambda b, i: (b * nstep + i, 0)
    prev = lambda b, i: (b * nstep * qblocks + jnp.maximum(i * qblocks - 1, 0), 0)
    return pl.pallas_call(
        functools.partial(_swa_prompt_kernel, qblocks=qblocks),
        grid=(nb, nstep),
        in_specs=[pl.BlockSpec(memory_space=pltpu.SMEM),
                  pl.BlockSpec((rows, SWA_WIDTH), cur),
                  pl.BlockSpec((WINDOW, SWA_KV_WIDTH), prev),
                  pl.BlockSpec((rows, SWA_KV_WIDTH), cur),
                  pl.BlockSpec((WINDOW, SWA_KV_WIDTH), prev),
                  pl.BlockSpec((rows, SWA_KV_WIDTH), cur),
                  pl.BlockSpec((SWA_HEADS, WINDOW, 2 * WINDOW), lambda b, i: (0, 0, 0))],
        out_specs=pl.BlockSpec((rows, SWA_WIDTH), cur),
        out_shape=jax.ShapeDtypeStruct((nb * t, SWA_WIDTH), BF16),
        scratch_shapes=[pltpu.VMEM((rows + WINDOW, SWA_KV_WIDTH), BF16),
                        pltpu.VMEM((rows + WINDOW, SWA_KV_WIDTH), BF16)],
        compiler_params=_cparams("parallel", "parallel"),
        name="swa_prompt",
    )(sinks, q, k, k, v, v, bias)


def _swa_decode_kernel(q_ref, k_ref, v_ref, bias_ref, sink_ref, o_ref, nk_ref, nv_ref, *, seqs, tq):
    rows, keys = SWA_HEADS * tq, k_ref.shape[1]
    nk_ref[...] = k_ref[:, tq:tq + WINDOW, :]
    nv_ref[...] = v_ref[:, tq:tq + WINDOW, :]
    low = lax.broadcasted_iota(jnp.int32, (tq, LANES), 1) < SWA_HEAD_DIM
    qi = lax.broadcasted_iota(jnp.int32, (rows, keys), 0) % tq
    col = lax.broadcasted_iota(jnp.int32, (rows, keys), 1)
    dist = qi + WINDOW - col
    valid = (dist >= 0) & (dist < WINDOW)
    bias = bias_ref[...]
    sink = sink_ref[...]
    for s_i in range(seqs):
        q = q_ref[s_i]
        tiles = [q[:, t * LANES:(t + 1) * LANES] for t in range(SWA_REP)]
        qh = jnp.concatenate([jnp.where(low, x, 0.0) for x in tiles]
                             + [jnp.where(low, 0.0, x) for x in tiles], axis=0)
        kk = k_ref[s_i].astype(BF16)
        s = lax.dot_general(qh.astype(BF16), kk, _TRANS_B, preferred_element_type=F32)
        s = jnp.where(valid, s + bias, NEG_INF)
        m = jnp.maximum(jnp.max(s, axis=-1, keepdims=True), sink)
        e = jnp.exp(s - m)
        den = jnp.sum(e, axis=-1, keepdims=True) + jnp.exp(sink - m)
        o = _dot(e.astype(BF16), v_ref[s_i].astype(BF16)) * (1.0 / den)
        for t in range(SWA_REP):
            o_ref[s_i, :, t * LANES:(t + 1) * LANES] = jnp.where(
                low, o[t * tq:(t + 1) * tq], o[(t + SWA_REP) * tq:(t + SWA_REP + 1) * tq])


def _swa_decode(q, k_all, v_all, bias, sink_rows, seqs):
    nseq, tq, _ = q.shape
    rows = SWA_HEADS * tq
    keys = k_all.shape[1]
    seqs = min(seqs, nseq)
    return pl.pallas_call(
        functools.partial(_swa_decode_kernel, seqs=seqs, tq=tq),
        grid=(nseq // seqs,),
        in_specs=[pl.BlockSpec((seqs, tq, SWA_WIDTH), lambda i: (i, 0, 0)),
                  pl.BlockSpec((seqs, keys, LANES), lambda i: (i, 0, 0)),
                  pl.BlockSpec((seqs, keys, LANES), lambda i: (i, 0, 0)),
                  pl.BlockSpec((rows, keys), lambda i: (0, 0)),
                  pl.BlockSpec((rows, 1), lambda i: (0, 0))],
        out_specs=[pl.BlockSpec((seqs, tq, SWA_WIDTH), lambda i: (i, 0, 0)),
                   pl.BlockSpec((seqs, WINDOW, LANES), lambda i: (i, 0, 0)),
                   pl.BlockSpec((seqs, WINDOW, LANES), lambda i: (i, 0, 0))],
        out_shape=[jax.ShapeDtypeStruct((nseq, tq, SWA_WIDTH), F32),
                   jax.ShapeDtypeStruct((nseq, WINDOW, LANES), F32),
                   jax.ShapeDtypeStruct((nseq, WINDOW, LANES), F32)],
        compiler_params=_cparams("parallel"),
        name="swa_decode",
    )(q, k_all, v_all, bias, sink_rows)


def _softmax(s):
    m = jnp.max(s, axis=-1, keepdims=True)
    e = jnp.exp(s - m)
    return e * (1.0 / jnp.sum(e, axis=-1, keepdims=True))


def _mem_prompt_kernel(q_ref, k_ref, v_ref, o_ref, s_ref, p_ref):
    scale = MEM_HEAD_DIM ** -0.5
    heads = [slice(h * MEM_HEAD_DIM, (h + 1) * MEM_HEAD_DIM) for h in range(MEM_HEADS)]
    for h, sl in enumerate(heads):
        s_ref[h] = lax.dot_general(q_ref[:, sl], k_ref[:, sl].astype(BF16), _TRANS_B, preferred_element_type=F32)
    s = s_ref[...] * scale
    e = jnp.exp(s - jnp.max(s, axis=-1, keepdims=True))
    p_ref[...] = e.astype(BF16)
    inv = 1.0 / jnp.sum(e, axis=-1, keepdims=True)
    for h, sl in enumerate(heads):
        o_ref[:, sl] = (_dot(p_ref[h], v_ref[:, sl].astype(BF16)) * inv[h]).astype(BF16)


def _mem_prompt(qm, mk, mv, nb, t, tile):
    tile = min(tile, t)
    nt = t // tile
    return pl.pallas_call(
        _mem_prompt_kernel,
        grid=(nb, nt),
        in_specs=[pl.BlockSpec((tile, MEM_WIDTH), lambda b, i: (b * nt + i, 0)),
                  pl.BlockSpec((MEM_TOKENS, MEM_WIDTH), lambda b, i: (b, 0)),
                  pl.BlockSpec((MEM_TOKENS, MEM_WIDTH), lambda b, i: (b, 0))],
        out_specs=pl.BlockSpec((tile, MEM_WIDTH), lambda b, i: (b * nt + i, 0)),
        out_shape=jax.ShapeDtypeStruct((nb * t, MEM_WIDTH), BF16),
        scratch_shapes=[pltpu.VMEM((MEM_HEADS, tile, MEM_TOKENS), F32), pltpu.VMEM((MEM_HEADS, tile, MEM_TOKENS), BF16)],
        compiler_params=_cparams("parallel", "parallel"),
        name="mem_prompt",
    )(qm, mk, mv)


def _mem_decode_kernel(q_ref, k_ref, v_ref, o_ref, *, seqs):
    tq = q_ref.shape[1]
    rows, cols = MEM_HEADS * tq, MEM_TOKENS * MEM_HEADS
    k2 = k_ref.reshape(seqs, cols, MEM_HEAD_DIM)
    v2 = v_ref.reshape(seqs, cols, MEM_HEAD_DIM)
    scale = MEM_HEAD_DIM ** -0.5
    own = (lax.broadcasted_iota(jnp.int32, (rows, cols), 1) % MEM_HEADS
           == lax.broadcasted_iota(jnp.int32, (rows, cols), 0) // tq)
    for s_i in range(seqs):
        q = q_ref[s_i]
        qb = jnp.concatenate([q[:, h * MEM_HEAD_DIM:(h + 1) * MEM_HEAD_DIM] for h in range(MEM_HEADS)], axis=0)
        s = lax.dot_general(qb.astype(BF16), k2[s_i].astype(BF16), _TRANS_B, preferred_element_type=F32) * scale
        p = _softmax(jnp.where(own, s, NEG_INF)).astype(BF16)
        o = _dot(p, v2[s_i].astype(BF16))
        for h in range(MEM_HEADS):
            o_ref[s_i, :, h * MEM_HEAD_DIM:(h + 1) * MEM_HEAD_DIM] = o[h * tq:(h + 1) * tq, :]


def _mem_decode(q, k, v, layer, seqs):
    nseq, tq, _ = q.shape
    seqs = min(seqs, nseq)
    cache = pl.BlockSpec((None, seqs, MEM_TOKENS, MEM_HEADS, MEM_HEAD_DIM), lambda i: (layer, i, 0, 0, 0))
    return pl.pallas_call(
        functools.partial(_mem_decode_kernel, seqs=seqs),
        grid=(nseq // seqs,),
        in_specs=[pl.BlockSpec((seqs, tq, MEM_WIDTH), lambda i: (i, 0, 0)), cache, cache],
        out_specs=pl.BlockSpec((seqs, tq, MEM_WIDTH), lambda i: (i, 0, 0)),
        out_shape=jax.ShapeDtypeStruct((nseq, tq, MEM_WIDTH), F32),
        compiler_params=_cparams("parallel"),
        name="mem_decode",
    )(q, k, v)


ROUTER_ROWS = 40
GATES_COL0 = SSM_WIDTH + SWA_WIDTH + 2 * SWA_KV_WIDTH + MEM_WIDTH
ROUTE_ROWS = 8
HALF = D_MODEL // 2


def _pack_halves(xb):
    hi = pltpu.bitcast(xb[:, 0:HALF].astype(F32), jnp.int32)
    lo = pltpu.bitcast(xb[:, HALF:D_MODEL].astype(F32), jnp.int32)
    return hi | lax.shift_right_logical(lo, jnp.int32(16))


def _unpack_halves(p):
    hi = pltpu.bitcast(p & jnp.int32(-65536), F32).astype(BF16)
    lo = pltpu.bitcast(lax.shift_left(p, jnp.int32(16)), F32).astype(BF16)
    return hi, lo


def _merge_kernel(x_ref, u_ref, y_ref, os_ref, om_ref, g1_ref, wg_ref, dsk_ref, wglu_ref, bglu_ref,
                  wbs_ref, wbw_ref, wbm_ref, wout_ref, g2_ref, wr_ref, br_ref,
                  h_ref, xn2_ref, route_ref, cnt_ref, base_ref, tri_ref):
    x = x_ref[...]
    tt = x.shape[0]
    xb = _rms(x, g1_ref[...]).astype(BF16)
    z = jax.nn.gelu(y_ref[...] + dsk_ref[...] * u_ref[...])
    z = z * jax.nn.sigmoid(_dot(z.astype(BF16), wglu_ref[...]) + bglu_ref[...])
    gate = lambda b: jax.nn.sigmoid(_dot(xb, wg_ref[:, GATES_COL0 + b * D_MODEL:GATES_COL0 + (b + 1) * D_MODEL]))
    merged = gate(0) * _dot(z.astype(BF16), wbs_ref[...])
    merged = merged + gate(1) * _dot(os_ref[...], wbw_ref[...])
    merged = merged + gate(2) * _dot(om_ref[...], wbm_ref[...])
    h = x + _dot(merged.astype(BF16), wout_ref[...])
    h_ref[...] = h
    xn2 = _rms(h, g2_ref[...]).astype(BF16)
    xn2_ref[...] = _pack_halves(xn2)

    lt = lax.dot_general(wr_ref[...], xn2, _TRANS_B, preferred_element_type=F32) + br_ref[...]
    gl = lt[N_EXPERTS:N_EXPERTS + N_EXPERT_GROUPS]
    ge = jnp.exp(gl - jnp.max(gl, axis=0, keepdims=True))
    gp = ge / jnp.sum(ge, axis=0, keepdims=True)
    gw = jnp.max(gp, axis=0, keepdims=True)
    gidx = jnp.full((1, tt), N_EXPERT_GROUPS - 1, jnp.int32)
    for r in range(N_EXPERT_GROUPS - 2, -1, -1):
        gidx = jnp.where(gp[r:r + 1] == gw, r, gidx)
    ein = lt[(N_EXPERT_GROUPS - 1) * EXPERTS_PER_GROUP:N_EXPERTS]
    for r in range(N_EXPERT_GROUPS - 2, -1, -1):
        ein = jnp.where(gidx == r, lt[r * EXPERTS_PER_GROUP:(r + 1) * EXPERTS_PER_GROUP], ein)
    ee = jnp.exp(ein - jnp.max(ein, axis=0, keepdims=True))
    ep = ee / jnp.sum(ee, axis=0, keepdims=True)
    rowi = lax.broadcasted_iota(jnp.int32, (EXPERTS_PER_GROUP, tt), 0)
    p1 = jnp.max(ep, axis=0, keepdims=True)
    e1 = jnp.min(jnp.where(ep == p1, rowi, EXPERTS_PER_GROUP), axis=0, keepdims=True)
    ep2 = jnp.where(rowi == e1, -1.0, ep)
    p2 = jnp.max(ep2, axis=0, keepdims=True)
    e2 = jnp.min(jnp.where(ep2 == p2, rowi, EXPERTS_PER_GROUP), axis=0, keepdims=True)
    tot = p1 + p2
    w1 = p1 / tot * gw
    w2 = p2 / tot * gw
    id1 = gidx * EXPERTS_PER_GROUP + e1
    id2 = gidx * EXPERTS_PER_GROUP + e2

    step = pl.program_id(0)

    @pl.when(step == 0)
    def _():
        base_ref[...] = jnp.zeros_like(base_ref)
        before = lax.broadcasted_iota(jnp.int32, (tt, tt), 0) < lax.broadcasted_iota(jnp.int32, (tt, tt), 1)
        tri_ref[...] = jnp.where(before, 1.0, 0.0).astype(BF16)

    r32 = lax.broadcasted_iota(jnp.int32, (N_EXPERTS, tt), 0)
    oh1 = jnp.where(r32 == id1, 1.0, 0.0)
    oh2 = jnp.where(r32 == id2, 1.0, 0.0)
    c1 = _dot(oh1.astype(BF16), tri_ref[...])
    c2 = _dot(oh2.astype(BF16), tri_ref[...])
    tot1 = jnp.sum(oh1, axis=1, keepdims=True)
    tot2 = jnp.sum(oh2, axis=1, keepdims=True)
    base = base_ref[:, 0:1]
    rank1 = jnp.sum(oh1 * (base + c1), axis=0, keepdims=True)
    rank2 = jnp.sum(oh2 * (base + tot1 + c2), axis=0, keepdims=True)
    new_base = jnp.broadcast_to(base + tot1 + tot2, base_ref.shape)
    base_ref[...] = new_base
    cnt_ref[...] = new_base
    route_ref[...] = jnp.concatenate([id1.astype(F32), id2.astype(F32), w1, w2, rank1, rank2,
                                      jnp.zeros((ROUTE_ROWS - 6, tt), F32)], axis=0)


def _merge(x, u, y, o_swa, o_mem, p, tile):
    n = x.shape[0]
    tile = min(tile, n)
    row = lambda i: (i, 0)
    const = lambda i: (0, 0)
    full = lambda a: pl.BlockSpec(a.shape, const, pipeline_mode=pl.Buffered(1))
    weights = [p['g1'], p['w_gates'], p['d_skip'], p['w_glu'], p['b_glu'], p['w_br_ssm'], p['w_br_swa'],
               p['w_br_mem'], p['w_out'], p['g2'], p['w_router'], p['b_router']]
    return pl.pallas_call(
        _merge_kernel,
        grid=(n // tile,),
        in_specs=[pl.BlockSpec((tile, D_MODEL), row), pl.BlockSpec((tile, SSM_WIDTH), row),
                  pl.BlockSpec((tile, SSM_WIDTH), row), pl.BlockSpec((tile, SWA_WIDTH), row),
                  pl.BlockSpec((tile, MEM_WIDTH), row)] + [full(w) for w in weights],
        out_specs=[pl.BlockSpec((tile, D_MODEL), row), pl.BlockSpec((tile, HALF), row),
                   pl.BlockSpec((ROUTE_ROWS, tile), lambda i: (0, i)),
                   pl.BlockSpec((N_EXPERTS, LANES), const)],
        out_shape=[jax.ShapeDtypeStruct((n, D_MODEL), F32), jax.ShapeDtypeStruct((n, HALF), jnp.int32),
                   jax.ShapeDtypeStruct((ROUTE_ROWS, n), F32), jax.ShapeDtypeStruct((N_EXPERTS, LANES), F32)],
        scratch_shapes=[pltpu.VMEM((N_EXPERTS, LANES), F32), pltpu.VMEM((tile, tile), BF16)],
        compiler_params=_cparams("arbitrary"),
        name="merge_router",
    )(x, u, y, o_swa, o_mem, *weights)


def _expert_mlp(xp, wg, wu, wd):
    hi, lo = _unpack_halves(xp)
    g = _dot(hi, wg[0:HALF, :]) + _dot(lo, wg[HALF:D_MODEL, :])
    u = _dot(hi, wu[0:HALF, :]) + _dot(lo, wu[HALF:D_MODEL, :])
    hh = jax.nn.silu(g) * u
    return _dot(hh.astype(BF16), wd[...])


def _moe_kernel(xn2_ref, rt_ref, wg_ref, wu_ref, wd_ref, h_ref, gf_ref, o_ref, acc_ref):
    e = pl.program_id(1)

    @pl.when(e == 0)
    def _():
        acc_ref[...] = jnp.zeros_like(acc_ref)

    o = _expert_mlp(xn2_ref[...], wg_ref[...].astype(BF16), wu_ref[...].astype(BF16), wd_ref[...].astype(BF16))
    ef = e.astype(F32)
    c = (jnp.where(rt_ref[:, 0:1] == ef, rt_ref[:, 2:3], 0.0)
         + jnp.where(rt_ref[:, 1:2] == ef, rt_ref[:, 3:4], 0.0))
    acc_ref[...] += c * o

    @pl.when(e == N_EXPERTS - 1)
    def _():
        o_ref[...] = _rms(h_ref[...] + acc_ref[...], gf_ref[...])


def _moe(xn2, route_t, w_g, w_u, w_d, h, gf, tile):
    n = h.shape[0]
    tile = min(tile, n)
    return pl.pallas_call(
        _moe_kernel,
        grid=(n // tile, N_EXPERTS),
        in_specs=[pl.BlockSpec((tile, HALF), lambda i, e: (i, 0)),
                  pl.BlockSpec((tile, ROUTE_ROWS), lambda i, e: (i, 0)),
                  pl.BlockSpec((None, D_MODEL, D_EXPERT), lambda i, e: (e, 0, 0)),
                  pl.BlockSpec((None, D_MODEL, D_EXPERT), lambda i, e: (e, 0, 0)),
                  pl.BlockSpec((None, D_EXPERT, D_MODEL), lambda i, e: (e, 0, 0)),
                  pl.BlockSpec((tile, D_MODEL), lambda i, e: (i, 0)),
                  pl.BlockSpec((1, D_MODEL), lambda i, e: (0, 0))],
        out_specs=pl.BlockSpec((tile, D_MODEL), lambda i, e: (i, 0)),
        out_shape=jax.ShapeDtypeStruct((n, D_MODEL), F32),
        scratch_shapes=[pltpu.VMEM((tile, D_MODEL), F32)],
        compiler_params=_cparams("parallel", "arbitrary"),
        name="moe_final_norm",
    )(xn2, route_t, w_g, w_u, w_d, h, gf)


EXPERT_ROW_TILE = 256
EXPERT_SLOTS = 4
SC_CORES = 2
SC_SUBCORES = 16
SC_WORKERS = SC_CORES * SC_SUBCORES
SC_SCATTER_ROWS = 64
SC_GATHER_ROWS = 64


def _sc_mesh():
    return plsc.VectorSubcoreMesh(core_axis_name="core", subcore_axis_name="subcore")


def _sc_scatter_pairs(x, pos, rows_out):
    n, d = x.shape
    per_w = n // SC_WORKERS
    window = min(SC_SCATTER_ROWS, per_w)

    @pl.kernel(out_type=jax.ShapeDtypeStruct((rows_out, d), x.dtype), mesh=_sc_mesh(),
               scratch_types=[pltpu.VMEM((window,), jnp.int32), pltpu.VMEM((window,), jnp.int32),
                              pltpu.VMEM((window, d), x.dtype), pltpu.SemaphoreType.DMA, pltpu.SemaphoreType.DMA,
                              pltpu.SemaphoreType.DMA])
    def scatter(x_hbm, p_hbm, o_hbm, i1_v, i2_v, rows_v, sem_a, sem_b, sem_c):
        wid = lax.axis_index("subcore") * SC_CORES + lax.axis_index("core")

        @pl.loop(0, per_w // window)
        def _(j):
            base = wid * per_w + j * window
            load_i1 = pltpu.async_copy(p_hbm.at[pl.ds(base, window)], i1_v, sem_a)
            load_i2 = pltpu.async_copy(p_hbm.at[pl.ds(n + base, window)], i2_v, sem_b)
            load_x = pltpu.async_copy(x_hbm.at[pl.ds(base, window)], rows_v, sem_c)
            load_i1.wait()
            load_i2.wait()
            load_x.wait()
            put_1 = pltpu.async_copy(rows_v, o_hbm.at[i1_v], sem_a)
            put_2 = pltpu.async_copy(rows_v, o_hbm.at[i2_v], sem_b)
            put_1.wait()
            put_2.wait()

    return scatter(x, pos)


def _sc_gather_rows(table, idx):
    m = idx.shape[0]
    d = table.shape[1]
    per_w = m // SC_WORKERS
    window = min(SC_GATHER_ROWS, per_w)

    assert per_w % (2 * window) == 0

    @pl.kernel(out_type=jax.ShapeDtypeStruct((m, d), table.dtype), mesh=_sc_mesh(),
               scratch_types=[pltpu.VMEM((window,), jnp.int32), pltpu.VMEM((window,), jnp.int32),
                              pltpu.VMEM((window, d), table.dtype), pltpu.VMEM((window, d), table.dtype),
                              pltpu.SemaphoreType.DMA, pltpu.SemaphoreType.DMA])
    def gather(t_hbm, i_hbm, o_hbm, ia_v, ib_v, ra_v, rb_v, sem_a, sem_b):
        wid = lax.axis_index("subcore") * SC_CORES + lax.axis_index("core")

        @pl.loop(0, per_w // (2 * window))
        def _(j):
            base_a = wid * per_w + j * (2 * window)
            base_b = base_a + window
            idx_a = pltpu.async_copy(i_hbm.at[pl.ds(base_a, window)], ia_v, sem_a)
            idx_b = pltpu.async_copy(i_hbm.at[pl.ds(base_b, window)], ib_v, sem_b)
            idx_a.wait()
            get_a = pltpu.async_copy(t_hbm.at[ia_v], ra_v, sem_a)
            idx_b.wait()
            get_b = pltpu.async_copy(t_hbm.at[ib_v], rb_v, sem_b)
            get_a.wait()
            put_a = pltpu.async_copy(ra_v, o_hbm.at[pl.ds(base_a, window)], sem_a)
            get_b.wait()
            put_b = pltpu.async_copy(rb_v, o_hbm.at[pl.ds(base_b, window)], sem_b)
            put_a.wait()
            put_b.wait()

    return gather(table, idx)


def _expert_tiles_kernel(start_ref, ntile_ref, x_hbm, wg_ref, wu_ref, wd_ref, o_hbm,
                         wg_s, wu_s, wd_s, x_buf, o_buf, in_sem, out_sem):
    e = pl.program_id(0)
    tm = x_buf.shape[1]
    nslot = x_buf.shape[0]
    first = start_ref[e] // tm
    ntile = ntile_ref[e]
    total = start_ref[N_EXPERTS - 1] // tm + ntile_ref[N_EXPERTS - 1]
    wg_s[...] = wg_ref[...].astype(BF16)
    wu_s[...] = wu_ref[...].astype(BF16)
    wd_s[...] = wd_ref[...].astype(BF16)

    def rows_of(g):
        return pl.ds(pl.multiple_of(g * tm, tm), tm)

    def fetch(g):
        slot = g % nslot
        return pltpu.make_async_copy(x_hbm.at[rows_of(g)], x_buf.at[slot], in_sem.at[slot])

    def flush(g):
        slot = g % nslot
        return pltpu.make_async_copy(o_buf.at[slot], o_hbm.at[rows_of(g)], out_sem.at[slot])

    @pl.when(e == 0)
    def _():
        for k in range(nslot - 1):
            @pl.when(k < total)
            def _(k=k):
                fetch(k).start()

    def tile(g, carry):
        @pl.when(g + nslot - 1 < total)
        def _():
            fetch(g + nslot - 1).start()

        fetch(g).wait()

        @pl.when(g >= nslot)
        def _():
            flush(g - nslot).wait()

        slot = g % nslot
        o_buf[slot] = _pack_halves(_expert_mlp(x_buf[slot], wg_s, wu_s, wd_s).astype(BF16))
        flush(g).start()
        return carry

    lax.fori_loop(first, first + ntile, tile, 0)

    @pl.when(e == N_EXPERTS - 1)
    def _():
        for k in range(nslot, 0, -1):
            @pl.when(total >= k)
            def _(k=k):
                flush(total - k).wait()


def _expert_tiles(starts, ntiles, xs, w_g, w_u, w_d):
    rows = xs.shape[0]
    tm = EXPERT_ROW_TILE
    weight = lambda shape: pl.BlockSpec((None,) + shape, lambda e, st, nt: (e, 0, 0))
    grid_spec = pltpu.PrefetchScalarGridSpec(
        num_scalar_prefetch=2,
        grid=(N_EXPERTS,),
        in_specs=[pl.BlockSpec(memory_space=pl.ANY),
                  weight((D_MODEL, D_EXPERT)), weight((D_MODEL, D_EXPERT)), weight((D_EXPERT, D_MODEL))],
        out_specs=pl.BlockSpec(memory_space=pl.ANY),
        scratch_shapes=[pltpu.VMEM((D_MODEL, D_EXPERT), BF16), pltpu.VMEM((D_MODEL, D_EXPERT), BF16),
                        pltpu.VMEM((D_EXPERT, D_MODEL), BF16),
                        pltpu.VMEM((EXPERT_SLOTS, tm, HALF), jnp.int32), pltpu.VMEM((EXPERT_SLOTS, tm, HALF), jnp.int32),
                        pltpu.SemaphoreType.DMA((EXPERT_SLOTS,)), pltpu.SemaphoreType.DMA((EXPERT_SLOTS,))],
    )
    return pl.pallas_call(
        _expert_tiles_kernel,
        grid_spec=grid_spec,
        out_shape=jax.ShapeDtypeStruct((rows, HALF), jnp.int32),
        compiler_params=_cparams("arbitrary"),
        name="expert_tiles",
    )(starts, ntiles, xs, w_g, w_u, w_d)


def _unpack_f32(p):
    return pltpu.bitcast(p & jnp.int32(-65536), F32), pltpu.bitcast(lax.shift_left(p, jnp.int32(16)), F32)


def _combine_kernel(h_ref, o1_ref, o2_ref, rt_ref, gf_ref, y_ref):
    w1, w2 = rt_ref[:, 2:3], rt_ref[:, 3:4]
    a_lo, a_hi = _unpack_f32(o1_ref[...])
    b_lo, b_hi = _unpack_f32(o2_ref[...])
    y_lo = h_ref[:, 0:HALF] + (w1 * a_lo + w2 * b_lo)
    y_hi = h_ref[:, HALF:D_MODEL] + (w1 * a_hi + w2 * b_hi)
    ms = (jnp.sum(y_lo * y_lo, axis=-1, keepdims=True) + jnp.sum(y_hi * y_hi, axis=-1, keepdims=True)) / D_MODEL
    inv = lax.rsqrt(ms + EPS)
    y_ref[:, 0:HALF] = (y_lo * inv) * gf_ref[:, 0:HALF]
    y_ref[:, HALF:D_MODEL] = (y_hi * inv) * gf_ref[:, HALF:D_MODEL]


def _combine(h, o12, route_t, gf, tile):
    n = h.shape[0]
    tile = min(tile, n)
    nt = n // tile
    return pl.pallas_call(
        _combine_kernel,
        grid=(nt,),
        in_specs=[pl.BlockSpec((tile, D_MODEL), lambda i: (i, 0)),
                  pl.BlockSpec((tile, HALF), lambda i: (i, 0)),
                  pl.BlockSpec((tile, HALF), lambda i: (i + nt, 0)),
                  pl.BlockSpec((tile, ROUTE_ROWS), lambda i: (i, 0)),
                  pl.BlockSpec((1, D_MODEL), lambda i: (0, 0))],
        out_specs=pl.BlockSpec((tile, D_MODEL), lambda i: (i, 0)),
        out_shape=jax.ShapeDtypeStruct((n, D_MODEL), F32),
        compiler_params=_cparams("parallel"),
        name="combine_final_norm",
    )(h, o12, o12, route_t, gf)


def _sparse_moe(xn2p, route, cnt, h, w_g, w_u, w_d, gf, run_before_experts):
    n = h.shape[0]
    tm = EXPERT_ROW_TILE
    rows = 2 * n + N_EXPERTS * tm
    rank = route[4:6].astype(jnp.int32)
    counts = cnt[:, 0].astype(jnp.int32)
    padded = (counts + tm - 1) // tm * tm
    e_idx = jnp.arange(N_EXPERTS, dtype=jnp.int32)
    starts = jnp.sum(jnp.where(e_idx[None, :] < e_idx[:, None], padded[None, :], 0), axis=1)
    ids = route[0:2].astype(jnp.int32)
    start_of = jnp.sum(jnp.where(ids[None] == e_idx[:, None, None], starts[:, None, None], 0), axis=0)
    pos = (start_of + rank).reshape(2 * n)
    xs = _sc_scatter_pairs(xn2p, pos, rows)
    xs, _ = lax.optimization_barrier((xs, run_before_experts))
    os_ = _expert_tiles(starts.astype(jnp.int32), (padded // tm).astype(jnp.int32), xs, w_g, w_u, w_d)
    o12 = _sc_gather_rows(os_, pos)
    return _combine(h, o12, route.T, gf, ROWS_COMBINE)


def _prep_in_weights(w_in):
    o = 0
    w_u = w_in[:, o:o + SSM_WIDTH]; o += SSM_WIDTH
    w_q = w_in[:, o:o + SWA_WIDTH]; o += SWA_WIDTH
    w_k = w_in[:, o:o + SWA_KV_WIDTH]; o += SWA_KV_WIDTH
    w_v = w_in[:, o:o + SWA_KV_WIDTH]; o += SWA_KV_WIDTH
    w_qm = w_in[:, o:o + MEM_WIDTH]; o += MEM_WIDTH
    assert o == GATES_COL0
    wq = (w_q * (SWA_HEAD_DIM ** -0.5)).reshape(D_MODEL, SWA_KV_HEADS, SWA_REP, SWA_HEAD_DIM)
    wq = wq.transpose(0, 2, 1, 3).reshape(D_MODEL, SWA_WIDTH)
    w_main = jnp.concatenate([w_u, wq, w_k, w_v, w_qm], axis=1).astype(BF16)
    return w_main, w_in.astype(BF16)


IN_SPLITS = (SSM_WIDTH, SWA_WIDTH, SWA_KV_WIDTH, SWA_KV_WIDTH, MEM_WIDTH)
IN_DTYPES = ((F32, BF16), (BF16,), (F32,), (F32,), (BF16,))


def kernel(x_prompt, x_sample, cache_swa_k, cache_swa_v, state_ssm_re, state_ssm_im, cache_mem_k, cache_mem_v, mem_prompt, norm1_g, w_in, lam_re, lam_im, log_dt, bm_re, bm_im, cm_re, cm_im, d_skip, w_glu, b_glu, sinks, rel_table, mem_norm_g, w_mem_kv, w_br_ssm, w_br_swa, w_br_mem, w_out, norm2_g, w_rg, b_rg, w_rexp, b_rexp, w_e_gate, w_e_up, w_e_down, final_norm_g):
    nb, t, _ = x_prompt.shape
    ns, ts, _ = x_sample.shape
    assert w_in.shape[0] == 1 and ts == S5_CHUNK and t % (WINDOW * SWA_BLOCKS_PER_STEP) == 0
    l = 0
    L = S5_CHUNK

    w_main, w_gates = _prep_in_weights(w_in[l])
    w_swa = (w_br_swa[l].reshape(SWA_KV_HEADS, SWA_REP, SWA_HEAD_DIM, D_MODEL).transpose(1, 0, 2, 3)
             .reshape(SWA_WIDTH, D_MODEL))
    pad_rows = ROUTER_ROWS - N_EXPERTS - N_EXPERT_GROUPS
    w_router = jnp.concatenate([w_rexp[l].T, w_rg[l].T, jnp.zeros((pad_rows, D_MODEL), F32)], axis=0).astype(BF16)
    b_router = jnp.concatenate([b_rexp[l], b_rg[l], jnp.zeros((pad_rows,), F32)]).reshape(ROUTER_ROWS, 1)
    mp = {
        'g1': norm1_g[l].reshape(1, D_MODEL), 'w_gates': w_gates, 'd_skip': d_skip[l].reshape(1, SSM_WIDTH),
        'w_glu': w_glu[l].astype(BF16), 'b_glu': b_glu[l].reshape(1, SSM_WIDTH),
        'w_br_ssm': w_br_ssm[l].astype(BF16), 'w_br_swa': w_swa.astype(BF16),
        'w_br_mem': w_br_mem[l].astype(BF16), 'w_out': w_out[l].astype(BF16),
        'g2': norm2_g[l].reshape(1, D_MODEL), 'w_router': w_router, 'b_router': b_router,
    }
    w_g, w_u, w_d = w_e_gate[l], w_e_up[l], w_e_down[l]
    gf = final_norm_g.reshape(1, D_MODEL)
    s5_w = _s5_weights(lam_re[l], lam_im[l], log_dt[l], bm_re[l], bm_im[l], cm_re[l], cm_im[l], L)

    bias_p = _rel_bias(rel_table, np.arange(WINDOW)[:, None] + WINDOW - np.arange(2 * WINDOW)[None, :])
    keys_s = WINDOW + 2 * ts
    bias_s = _rel_bias(rel_table, np.arange(ts)[:, None] + WINDOW - np.arange(keys_s)[None, :])
    bias_s = bias_s.reshape(SWA_HEADS * ts, keys_s)
    sink_rows = jnp.repeat(sinks[l].astype(F32), ts).reshape(SWA_HEADS * ts, 1)

    n = nb * t
    xp = x_prompt.reshape(n, D_MODEL)
    mk, mv = _norm_proj(mem_prompt.reshape(nb * MEM_TOKENS, D_MODEL), mem_norm_g[l].reshape(1, D_MODEL),
                        w_mem_kv[l].astype(BF16), (MEM_WIDTH, MEM_WIDTH), ((F32,), (F32,)), ROWS_MEM_PROJ)
    u, ub, qz, k, v, qm = _norm_proj(xp, mp['g1'], w_main, IN_SPLITS, IN_DTYPES, ROWS_NORM_PROJ)

    y_ssm, fin = _s5(ub, jnp.zeros((nb, N_CH_TILES * 2 * STATE_TILE), F32), s5_w, nb, t // L, L, S5_CHUNKS_PER_STEP)
    p_re, p_im = _tiles_to_state(fin)

    o_swa = _swa_prompt(qz, k, v, bias_p, sinks[l].astype(F32), nb, t, SWA_BLOCKS_PER_STEP)
    o_mem = _mem_prompt(qm, mk, mv, nb, t, ROWS_MEM_ATTN)
    h, xn2p, route, cnt = _merge(xp, u, y_ssm, o_swa, o_mem, mp, ROWS_MERGE)

    k4 = k.reshape(nb, t, SWA_KV_HEADS, SWA_HEAD_DIM)
    v4 = v.reshape(nb, t, SWA_KV_HEADS, SWA_HEAD_DIM)
    new_k_p, new_v_p = k4[:, -WINDOW:][None], v4[:, -WINDOW:][None]
    new_mk = mk.reshape(1, nb, MEM_TOKENS, MEM_HEADS, MEM_HEAD_DIM)
    new_mv = mv.reshape(1, nb, MEM_TOKENS, MEM_HEADS, MEM_HEAD_DIM)

    m = ns * ts
    xs = x_sample.reshape(m, D_MODEL)
    us, ubs, qzs, k_s, v_s, qms = _norm_proj(xs, mp['g1'], w_main, IN_SPLITS, IN_DTYPES, ROWS_NORM_PROJ)
    ys_ssm, fins = _s5(ubs, _state_to_tiles(state_ssm_re[l], state_ssm_im[l]), s5_w, ns, ts // L, L, S5_CHUNKS_PER_STEP)
    s_re, s_im = _tiles_to_state(fins)

    kk_all = jnp.concatenate([cache_swa_k[l].reshape(ns, WINDOW, SWA_KV_WIDTH).astype(F32),
                              k_s.reshape(ns, ts, SWA_KV_WIDTH)], axis=1)
    vv_all = jnp.concatenate([cache_swa_v[l].reshape(ns, WINDOW, SWA_KV_WIDTH).astype(F32),
                              v_s.reshape(ns, ts, SWA_KV_WIDTH)], axis=1)
    pad = jnp.zeros((ns, keys_s - WINDOW - ts, SWA_KV_WIDTH), F32)
    o_dec, roll_k, roll_v = _swa_decode(qzs.astype(F32).reshape(ns, ts, SWA_WIDTH),
                                        jnp.concatenate([kk_all, pad], axis=1),
                                        jnp.concatenate([vv_all, pad], axis=1), bias_s, sink_rows, DECODE_SEQS_PER_STEP)
    o_dec, q_mem = lax.optimization_barrier((o_dec, qms.astype(F32).reshape(ns, ts, MEM_WIDTH)))
    o_swa_s = o_dec.reshape(m, SWA_WIDTH).astype(BF16)

    o_mem_s = _mem_decode(q_mem, cache_mem_k, cache_mem_v, l, DECODE_SEQS_PER_STEP)
    o_mem_s = o_mem_s.reshape(m, MEM_WIDTH).astype(BF16)

    y_prompt = _sparse_moe(xn2p, route, cnt, h, w_g, w_u, w_d, gf, (ys_ssm, o_swa_s, o_mem_s)).reshape(nb, t, D_MODEL)
    hs_, xn2ps, routes, _ = _merge(xs, us, ys_ssm, o_swa_s, o_mem_s, mp, ROWS_MERGE)
    y_sample = _moe(xn2ps, routes.T, w_g, w_u, w_d, hs_, gf, ROWS_DENSE_MOE).reshape(ns, ts, D_MODEL)

    new_k_s = roll_k.reshape(1, ns, WINDOW, SWA_KV_HEADS, SWA_HEAD_DIM).astype(cache_swa_k.dtype)
    new_v_s = roll_v.reshape(1, ns, WINDOW, SWA_KV_HEADS, SWA_HEAD_DIM).astype(cache_swa_v.dtype)

    return (y_prompt, y_sample,
            new_k_p, new_v_p, p_re[None], p_im[None], new_mk, new_mv,
            new_k_s, new_v_s, s_re[None].astype(state_ssm_re.dtype), s_im[None].astype(state_ssm_im.dtype))
```

```python
import functools
import math

import numpy as np
import jax
import jax.numpy as jnp
from jax import lax
from jax.experimental import pallas as pl
from jax.experimental.pallas import tpu as pltpu
from jax.experimental.pallas import tpu_sc as plsc

F32 = jnp.float32
BF16 = jnp.bfloat16

D_MODEL = 1024
SSM_WIDTH = 512
SSM_GROUP = 16
SSM_GROUPS = 32
SSM_STATE = 64
SWA_HEADS = 8
SWA_KV_HEADS = 2
SWA_REP = 4
SWA_HEAD_DIM = 64
SWA_WIDTH = 512
SWA_KV_WIDTH = 128
WINDOW = 128
REL_BUCKETS = 32
REL_MAX_DIST = 128
MEM_TOKENS = 256
MEM_HEADS = 4
MEM_HEAD_DIM = 128
MEM_WIDTH = 512
N_EXPERT_GROUPS = 4
EXPERTS_PER_GROUP = 8
N_EXPERTS = 32
D_EXPERT = 256
EPS = 1e-6
NEG_INF = -1e30

LANES = 128
GROUPS_PER_TILE = LANES // SSM_GROUP
N_CH_TILES = SSM_WIDTH // LANES
STATE_TILE = GROUPS_PER_TILE * SSM_STATE
VMEM_LIMIT = 56 * 1024 * 1024
ROWS_NORM_PROJ = 2048
ROWS_MEM_PROJ = 1024
ROWS_MEM_ATTN = 2048
ROWS_MERGE = 512
ROWS_COMBINE = 1024
ROWS_DENSE_MOE = 1024
S5_CHUNKS_PER_STEP = 128
SWA_BLOCKS_PER_STEP = 16
SWA_DECODE_SEQS_PER_STEP = 16
MEM_DECODE_SEQS_PER_STEP = 8
S5_CHUNK = 8
S5_PANEL = 256

_TRANS_B = (((1,), (1,)), ((), ()))


def _cparams(*sem):
    return pltpu.CompilerParams(dimension_semantics=sem, vmem_limit_bytes=VMEM_LIMIT)


def _rms(x, g):
    return (x * lax.rsqrt(jnp.mean(x * x, axis=-1, keepdims=True) + EPS)) * g


def _dot(a, b):
    return jnp.dot(a, b, preferred_element_type=F32)


def _norm_proj_kernel(x_ref, g_ref, w_ref, *out_refs, splits, dtypes):
    xb = _rms(x_ref[...], g_ref[...]).astype(BF16)
    off = 0
    outs = iter(out_refs)
    for width, dts in zip(splits, dtypes):
        r = _dot(xb, w_ref[:, off:off + width])
        for dt in dts:
            next(outs)[...] = r.astype(dt)
        off += width


def _norm_proj(x, g, w, splits, dtypes, tile):
    n, d = x.shape
    tile = min(tile, n)
    flat = [(wd, dt) for wd, dts in zip(splits, dtypes) for dt in dts]
    return pl.pallas_call(
        functools.partial(_norm_proj_kernel, splits=tuple(splits), dtypes=tuple(dtypes)),
        grid=(n // tile,),
        in_specs=[pl.BlockSpec((tile, d), lambda i: (i, 0)),
                  pl.BlockSpec((1, d), lambda i: (0, 0)),
                  pl.BlockSpec((d, sum(splits)), lambda i: (0, 0), pipeline_mode=pl.Buffered(1))],
        out_specs=[pl.BlockSpec((tile, wd), lambda i: (i, 0)) for wd, _ in flat],
        out_shape=[jax.ShapeDtypeStruct((n, wd), dt) for wd, dt in flat],
        compiler_params=_cparams("parallel"),
        name="norm_proj",
    )(x, g, w)


def _s5_weights(lam_re, lam_im, log_dt, bm_re, bm_im, cm_re, cm_im, L):
    nt, gt, P, H = N_CH_TILES, GROUPS_PER_TILE, SSM_STATE, SSM_GROUP
    lr, li = lam_re.astype(F32), lam_im.astype(F32)
    dt = jnp.exp(log_dt.astype(F32))[:, None]
    mag = jnp.exp(lr * dt)
    a_re = mag * jnp.cos(li * dt)
    a_im = mag * jnp.sin(li * dt)
    den = lr * lr + li * li
    f_re = ((a_re - 1.0) * lr + a_im * li) / den
    f_im = (a_im * lr - (a_re - 1.0) * li) / den
    br, bi = bm_re.astype(F32), bm_im.astype(F32)
    bb_re = f_re[..., None] * br - f_im[..., None] * bi
    bb_im = f_re[..., None] * bi + f_im[..., None] * br
    pr, pi = [jnp.ones_like(a_re)], [jnp.zeros_like(a_im)]
    for _ in range(L):
        pr.append(pr[-1] * a_re - pi[-1] * a_im)
        pi.append(pr[-2] * a_im + pi[-1] * a_re)
    ap_re, ap_im = jnp.stack(pr), jnp.stack(pi)
    cr, ci = cm_re.astype(F32), cm_im.astype(F32)
    ca_re = cr[None] * ap_re[:, :, None, :] - ci[None] * ap_im[:, :, None, :]
    ca_im = cr[None] * ap_im[:, :, None, :] + ci[None] * ap_re[:, :, None, :]

    rev_re = jnp.stack([pr[L - 1 - s] for s in range(L)])
    rev_im = jnp.stack([pi[L - 1 - s] for s in range(L)])
    ws_re = rev_re[..., None] * bb_re[None] - rev_im[..., None] * bb_im[None]
    ws_im = rev_re[..., None] * bb_im[None] + rev_im[..., None] * bb_re[None]
    c_st = jnp.concatenate([ws_re.transpose(0, 1, 3, 2).reshape(L, nt, gt * H, P),
                            ws_im.transpose(0, 1, 3, 2).reshape(L, nt, gt * H, P)], axis=3).transpose(1, 0, 2, 3)
    so = lambda ca: ca[1:].transpose(1, 3, 0, 2).reshape(nt, gt * P, L * H)
    c_so = jnp.concatenate([so(ca_re), so(-ca_im)], axis=1)
    prod = (ca_re[:L][:, :, None, :, :] * bb_re.transpose(0, 2, 1)[None, :, :, None, :]
            - ca_im[:L][:, :, None, :, :] * bb_im.transpose(0, 2, 1)[None, :, :, None, :])
    k_lag = jnp.sum(prod, axis=-1).transpose(1, 2, 0, 3)
    c_k = k_lag.reshape(nt, gt * H, L * H)
    w_st, w_out, toep = _s5_expand(c_st, c_so, c_k, L)

    def per_tile(v):
        return v.reshape(nt, 1, STATE_TILE)

    return w_st, w_out, toep, per_tile(pr[L]), per_tile(pi[L])


def _s5_expand_kernel(cst_ref, cso_ref, ck_ref, wst_ref, wso_ref, toep_ref, *, L):
    hp = lax.Precision.HIGHEST
    P, H = SSM_STATE, SSM_GROUP
    iota = lambda shape, d: lax.broadcasted_iota(jnp.int32, shape, d)
    one = lambda cond: jnp.where(cond, 1.0, 0.0).astype(F32)

    r, c = iota((2 * P, 2 * STATE_TILE), 0), iota((2 * P, 2 * STATE_TILE), 1)
    rep_st = one((r // P == c // STATE_TILE) & (r % P == c % P))
    r, c = iota((LANES, 2 * STATE_TILE), 0), iota((LANES, 2 * STATE_TILE), 1)
    own_st = one(r // H == (c % STATE_TILE) // P)
    for s in range(L):
        blk = jnp.dot(cst_ref[s], rep_st, precision=hp, preferred_element_type=F32) * own_st
        wst_ref[s * LANES:(s + 1) * LANES, :] = blk.astype(BF16)

    r, c = iota((LANES, LANES), 0), iota((LANES, LANES), 1)
    pick = [one((r // H == t) & (r % H == c % H)) for t in range(L)]
    own_k = one(r // H == c // H)
    r, c = iota((2 * STATE_TILE, LANES), 0), iota((2 * STATE_TILE, LANES), 1)
    own_so = one((r % STATE_TILE) // P == c // H)
    cso = cso_ref[...]
    for t in range(L):
        blk = jnp.dot(cso, pick[t], precision=hp, preferred_element_type=F32) * own_so
        wso_ref[:, t * LANES:(t + 1) * LANES] = blk.astype(BF16)
    ck = ck_ref[...]
    lag = [(jnp.dot(ck, pick[t], precision=hp, preferred_element_type=F32) * own_k).astype(BF16) for t in range(L)]
    zero = jnp.zeros((LANES, LANES), BF16)
    for s in range(L):
        for t in range(L):
            toep_ref[s * LANES:(s + 1) * LANES, t * LANES:(t + 1) * LANES] = lag[t - s] if t >= s else zero


def _s5_expand(c_st, c_so, c_k, L):
    lk = L * LANES
    st2 = 2 * STATE_TILE
    return pl.pallas_call(
        functools.partial(_s5_expand_kernel, L=L),
        grid=(N_CH_TILES,),
        in_specs=[pl.BlockSpec((None, L, LANES, 2 * SSM_STATE), lambda j: (j, 0, 0, 0)),
                  pl.BlockSpec((None, st2, L * SSM_GROUP), lambda j: (j, 0, 0)),
                  pl.BlockSpec((None, LANES, L * SSM_GROUP), lambda j: (j, 0, 0))],
        out_specs=[pl.BlockSpec((None, lk, st2), lambda j: (j, 0, 0)),
                   pl.BlockSpec((None, st2, lk), lambda j: (j, 0, 0)),
                   pl.BlockSpec((None, lk, lk), lambda j: (j, 0, 0))],
        out_shape=[jax.ShapeDtypeStruct((N_CH_TILES, lk, st2), BF16),
                   jax.ShapeDtypeStruct((N_CH_TILES, st2, lk), BF16),
                   jax.ShapeDtypeStruct((N_CH_TILES, lk, lk), BF16)],
        compiler_params=_cparams("parallel"),
        name="s5_expand_weights",
    )(c_st, c_so, c_k)


def _to_chunks(u, nb, nc, L):
    return (u.reshape(nb, nc, L, N_CH_TILES, LANES).transpose(1, 0, 3, 2, 4)
            .reshape(nc * nb, N_CH_TILES * L * LANES))


def _from_chunks(y, nb, nc, L):
    return (y.reshape(nc, nb, N_CH_TILES, L, LANES).transpose(1, 0, 3, 2, 4)
            .reshape(nb * nc * L, SSM_WIDTH))


def _s5_kernel(x_ref, h0_ref, are_ref, aim_ref, ws_ref, t_ref, wo_ref, y_ref, fin_ref,
               hr_ref, hi_ref, d_ref, hs_ref, *, cb, nb):
    ci = pl.program_id(1)

    @pl.when(ci == 0)
    def _():
        hr_ref[...] = h0_ref[:, 0:STATE_TILE]
        hi_ref[...] = h0_ref[:, STATE_TILE:2 * STATE_TILE]

    x = x_ref[...]
    d_ref[...] = _dot(x, ws_ref[...])
    ar = jnp.broadcast_to(are_ref[...], (nb, STATE_TILE))
    ai = jnp.broadcast_to(aim_ref[...], (nb, STATE_TILE))

    def body(c, carry):
        hr, hi = carry
        r0 = pl.multiple_of(c * nb, nb)
        hs_ref[pl.ds(r0, nb), 0:STATE_TILE] = hr
        hs_ref[pl.ds(r0, nb), STATE_TILE:2 * STATE_TILE] = hi
        d = d_ref[pl.ds(r0, nb), :]
        return (ar * hr - ai * hi + d[:, 0:STATE_TILE],
                ar * hi + ai * hr + d[:, STATE_TILE:2 * STATE_TILE])

    hr, hi = lax.fori_loop(0, cb, body, (hr_ref[...], hi_ref[...]))
    hr_ref[...] = hr
    hi_ref[...] = hi
    hsb = hs_ref[...].astype(BF16)
    for c0 in range(0, t_ref.shape[1], S5_PANEL):
        c1 = c0 + S5_PANEL
        y_ref[:, c0:c1] = _dot(x[:, 0:c1], t_ref[0:c1, c0:c1]) + _dot(hsb, wo_ref[:, c0:c1])

    @pl.when(ci == pl.num_programs(1) - 1)
    def _():
        fin_ref[:, 0:STATE_TILE] = hr
        fin_ref[:, STATE_TILE:2 * STATE_TILE] = hi


def _s5(ub, h0, weights, nb, nc, L, chunk_block):
    w_st, w_so, toep, a_re, a_im = weights
    xc = _to_chunks(ub, nb, nc, L)
    cb = min(chunk_block, nc)
    rows = cb * nb
    lk = L * LANES
    st2 = 2 * STATE_TILE
    tile_w = lambda shape: pl.BlockSpec((None,) + shape, lambda j, c: (j, 0, 0))
    y, fin = pl.pallas_call(
        functools.partial(_s5_kernel, cb=cb, nb=nb),
        grid=(N_CH_TILES, nc // cb),
        in_specs=[pl.BlockSpec((rows, lk), lambda j, c: (c, j)),
                  pl.BlockSpec((nb, st2), lambda j, c: (0, j)),
                  tile_w((1, STATE_TILE)), tile_w((1, STATE_TILE)),
                  tile_w((lk, st2)), tile_w((lk, lk)), tile_w((st2, lk))],
        out_specs=[pl.BlockSpec((rows, lk), lambda j, c: (c, j)),
                   pl.BlockSpec((nb, st2), lambda j, c: (0, j))],
        out_shape=[jax.ShapeDtypeStruct((nc * nb, N_CH_TILES * lk), F32),
                   jax.ShapeDtypeStruct((nb, N_CH_TILES * st2), F32)],
        scratch_shapes=[pltpu.VMEM((nb, STATE_TILE), F32), pltpu.VMEM((nb, STATE_TILE), F32),
                        pltpu.VMEM((rows, st2), F32), pltpu.VMEM((rows, st2), F32)],
        compiler_params=_cparams("parallel", "arbitrary"),
        name="s5_chunked_scan",
    )(xc, h0, a_re, a_im, w_st, toep, w_so)
    return _from_chunks(y, nb, nc, L), fin


def _state_to_tiles(h_re, h_im):
    nb = h_re.shape[0]
    r = h_re.astype(F32).reshape(nb, N_CH_TILES, STATE_TILE)
    i = h_im.astype(F32).reshape(nb, N_CH_TILES, STATE_TILE)
    return jnp.concatenate([r, i], axis=-1).reshape(nb, N_CH_TILES * 2 * STATE_TILE)


def _tiles_to_state(h):
    nb = h.shape[0]
    h = h.reshape(nb, N_CH_TILES, 2, GROUPS_PER_TILE, SSM_STATE)
    return (h[:, :, 0].reshape(nb, SSM_GROUPS, SSM_STATE), h[:, :, 1].reshape(nb, SSM_GROUPS, SSM_STATE))


def _t5_bucket(dist):
    n = np.maximum(dist, 0)
    max_exact = REL_BUCKETS // 2
    nf = np.maximum(n, 1).astype(np.float32)
    large = max_exact + (np.log(nf / np.float32(max_exact)) / np.float32(math.log(REL_MAX_DIST / max_exact))
                         * np.float32(REL_BUCKETS - max_exact)).astype(np.int32)
    large = np.minimum(large, REL_BUCKETS - 1)
    return np.where(n < max_exact, n, large)


def _rel_bias(rel_table, dist):
    bucket = _t5_bucket(dist)
    tab = rel_table.astype(F32)
    out = jnp.zeros((SWA_HEADS,) + dist.shape, F32)
    for b in range(REL_BUCKETS):
        sel = jnp.asarray(bucket == b)
        if bool((bucket == b).any()):
            out = jnp.where(sel[None], tab[b].reshape((SWA_HEADS,) + (1,) * dist.ndim), out)
    return out


def _swa_prompt_kernel(sink_ref, q_ref, kp_ref, kc_ref, vp_ref, vc_ref, bias_ref, o_ref, kk_ref, vv_ref, *, qblocks):
    step = pl.program_id(1)
    kk_ref[0:WINDOW, :] = kp_ref[...].astype(BF16)
    kk_ref[WINDOW:, :] = kc_ref[...].astype(BF16)
    vv_ref[0:WINDOW, :] = vp_ref[...].astype(BF16)
    vv_ref[WINDOW:, :] = vc_ref[...].astype(BF16)
    row = lax.broadcasted_iota(jnp.int32, (WINDOW, 2 * WINDOW), 0)
    col = lax.broadcasted_iota(jnp.int32, (WINDOW, 2 * WINDOW), 1)
    dist = row + WINDOW - col
    band = (dist >= 0) & (dist < WINDOW)
    lane = lax.broadcasted_iota(jnp.int32, (WINDOW, LANES), 1)
    low = lane < SWA_HEAD_DIM

    def block(j, carry):
        r0 = pl.multiple_of(j * WINDOW, WINDOW)
        kk = kk_ref[pl.ds(r0, 2 * WINDOW), :]
        vv = vv_ref[pl.ds(r0, 2 * WINDOW), :]
        valid = band & ((col >= WINDOW) | (step * qblocks + j > 0))
        for t in range(SWA_REP):
            q2 = q_ref[pl.ds(r0, WINDOW), t * LANES:(t + 1) * LANES]
            outs = []
            for half in range(SWA_KV_HEADS):
                h = t + SWA_REP * half
                qh = jnp.where(low if half == 0 else jnp.logical_not(low), q2, jnp.zeros_like(q2))
                s = lax.dot_general(qh, kk, _TRANS_B, preferred_element_type=F32)
                s = jnp.where(valid, s + bias_ref[h], NEG_INF)
                sink = sink_ref[h]
                m = jnp.maximum(jnp.max(s, axis=-1, keepdims=True), sink)
                e = jnp.exp(s - m)
                den = jnp.sum(e, axis=-1, keepdims=True) + jnp.exp(sink - m)
                outs.append(_dot(e.astype(BF16), vv) * (1.0 / den))
            o_ref[pl.ds(r0, WINDOW), t * LANES:(t + 1) * LANES] = jnp.where(low, outs[0], outs[1]).astype(BF16)
        return carry

    lax.fori_loop(0, qblocks, block, 0)


def _swa_prompt(q, k, v, bias, sinks, nb, t, qblocks):
    nstep = t // (WINDOW * qblocks)
    rows = WINDOW * qblocks
    cur = lambda b, i: (b * nstep + i, 0)
    prev = lambda b, i: (b * nstep * qblocks + jnp.maximum(i * qblocks - 1, 0), 0)
    return pl.pallas_call(
        functools.partial(_swa_prompt_kernel, qblocks=qblocks),
        grid=(nb, nstep),
        in_specs=[pl.BlockSpec(memory_space=pltpu.SMEM),
                  pl.BlockSpec((rows, SWA_WIDTH), cur),
                  pl.BlockSpec((WINDOW, SWA_KV_WIDTH), prev),
                  pl.BlockSpec((rows, SWA_KV_WIDTH), cur),
                  pl.BlockSpec((WINDOW, SWA_KV_WIDTH), prev),
                  pl.BlockSpec((rows, SWA_KV_WIDTH), cur),
                  pl.BlockSpec((SWA_HEADS, WINDOW, 2 * WINDOW), lambda b, i: (0, 0, 0))],
        out_specs=pl.BlockSpec((rows, SWA_WIDTH), cur),
        out_shape=jax.ShapeDtypeStruct((nb * t, SWA_WIDTH), BF16),
        scratch_shapes=[pltpu.VMEM((rows + WINDOW, SWA_KV_WIDTH), BF16),
                        pltpu.VMEM((rows + WINDOW, SWA_KV_WIDTH), BF16)],
        compiler_params=_cparams("parallel", "parallel"),
        name="swa_prompt",
    )(sinks, q, k, k, v, v, bias)


def _swa_decode_kernel(q_ref, k_ref, v_ref, bias_ref, sink_ref, o_ref, nk_ref, nv_ref, *, seqs, tq):
    rows, keys = SWA_HEADS * tq, k_ref.shape[1]
    nk_ref[...] = k_ref[:, tq:tq + WINDOW, :]
    nv_ref[...] = v_ref[:, tq:tq + WINDOW, :]
    low = lax.broadcasted_iota(jnp.int32, (tq, LANES), 1) < SWA_HEAD_DIM
    qi = lax.broadcasted_iota(jnp.int32, (rows, keys), 0) % tq
    col = lax.broadcasted_iota(jnp.int32, (rows, keys), 1)
    dist = qi + WINDOW - col
    valid = (dist >= 0) & (dist < WINDOW)
    bias = bias_ref[...]
    sink = sink_ref[...]
    for s_i in range(seqs):
        q = q_ref[s_i]
        tiles = [q[:, t * LANES:(t + 1) * LANES] for t in range(SWA_REP)]
        qh = jnp.concatenate([jnp.where(low, x, 0.0) for x in tiles]
                             + [jnp.where(low, 0.0, x) for x in tiles], axis=0)
        kk = k_ref[s_i].astype(BF16)
        s = lax.dot_general(qh.astype(BF16), kk, _TRANS_B, preferred_element_type=F32)
        s = jnp.where(valid, s + bias, NEG_INF)
        m = jnp.maximum(jnp.max(s, axis=-1, keepdims=True), sink)
        e = jnp.exp(s - m)
        den = jnp.sum(e, axis=-1, keepdims=True) + jnp.exp(sink - m)
        o = _dot(e.astype(BF16), v_ref[s_i].astype(BF16)) * (1.0 / den)
        for t in range(SWA_REP):
            o_ref[s_i, :, t * LANES:(t + 1) * LANES] = jnp.where(
                low, o[t * tq:(t + 1) * tq], o[(t + SWA_REP) * tq:(t + SWA_REP + 1) * tq])


def _swa_decode(q, k_all, v_all, bias, sink_rows, seqs):
    nseq, tq, _ = q.shape
    rows = SWA_HEADS * tq
    keys = k_all.shape[1]
    seqs = min(seqs, nseq)
    return pl.pallas_call(
        functools.partial(_swa_decode_kernel, seqs=seqs, tq=tq),
        grid=(nseq // seqs,),
        in_specs=[pl.BlockSpec((seqs, tq, SWA_WIDTH), lambda i: (i, 0, 0)),
                  pl.BlockSpec((seqs, keys, LANES), lambda i: (i, 0, 0)),
                  pl.BlockSpec((seqs, keys, LANES), lambda i: (i, 0, 0)),
                  pl.BlockSpec((rows, keys), lambda i: (0, 0)),
                  pl.BlockSpec((rows, 1), lambda i: (0, 0))],
        out_specs=[pl.BlockSpec((seqs, tq, SWA_WIDTH), lambda i: (i, 0, 0)),
                   pl.BlockSpec((seqs, WINDOW, LANES), lambda i: (i, 0, 0)),
                   pl.BlockSpec((seqs, WINDOW, LANES), lambda i: (i, 0, 0))],
        out_shape=[jax.ShapeDtypeStruct((nseq, tq, SWA_WIDTH), F32),
                   jax.ShapeDtypeStruct((nseq, WINDOW, LANES), F32),
                   jax.ShapeDtypeStruct((nseq, WINDOW, LANES), F32)],
        compiler_params=_cparams("parallel"),
        name="swa_decode",
    )(q, k_all, v_all, bias, sink_rows)


def _softmax(s):
    m = jnp.max(s, axis=-1, keepdims=True)
    e = jnp.exp(s - m)
    return e * (1.0 / jnp.sum(e, axis=-1, keepdims=True))


def _mem_prompt_kernel(q_ref, k_ref, v_ref, o_ref, s_ref, p_ref):
    scale = MEM_HEAD_DIM ** -0.5
    heads = [slice(h * MEM_HEAD_DIM, (h + 1) * MEM_HEAD_DIM) for h in range(MEM_HEADS)]
    for h, sl in enumerate(heads):
        s_ref[h] = lax.dot_general(q_ref[:, sl], k_ref[:, sl].astype(BF16), _TRANS_B, preferred_element_type=F32)
    s = s_ref[...] * scale
    e = jnp.exp(s - jnp.max(s, axis=-1, keepdims=True))
    p_ref[...] = e.astype(BF16)
    inv = 1.0 / jnp.sum(e, axis=-1, keepdims=True)
    for h, sl in enumerate(heads):
        o_ref[:, sl] = (_dot(p_ref[h], v_ref[:, sl].astype(BF16)) * inv[h]).astype(BF16)


def _mem_prompt(qm, mk, mv, nb, t, tile):
    tile = min(tile, t)
    nt = t // tile
    return pl.pallas_call(
        _mem_prompt_kernel,
        grid=(nb, nt),
        in_specs=[pl.BlockSpec((tile, MEM_WIDTH), lambda b, i: (b * nt + i, 0)),
                  pl.BlockSpec((MEM_TOKENS, MEM_WIDTH), lambda b, i: (b, 0)),
                  pl.BlockSpec((MEM_TOKENS, MEM_WIDTH), lambda b, i: (b, 0))],
        out_specs=pl.BlockSpec((tile, MEM_WIDTH), lambda b, i: (b * nt + i, 0)),
        out_shape=jax.ShapeDtypeStruct((nb * t, MEM_WIDTH), BF16),
        scratch_shapes=[pltpu.VMEM((MEM_HEADS, tile, MEM_TOKENS), F32), pltpu.VMEM((MEM_HEADS, tile, MEM_TOKENS), BF16)],
        compiler_params=_cparams("parallel", "parallel"),
        name="mem_prompt",
    )(qm, mk, mv)


def _mem_decode_kernel(q_ref, k_ref, v_ref, o_ref, *, seqs):
    tq = q_ref.shape[1]
    rows, cols = MEM_HEADS * tq, MEM_TOKENS * MEM_HEADS
    k2 = k_ref.reshape(seqs, cols, MEM_HEAD_DIM)
    v2 = v_ref.reshape(seqs, cols, MEM_HEAD_DIM)
    scale = MEM_HEAD_DIM ** -0.5
    own = (lax.broadcasted_iota(jnp.int32, (rows, cols), 1) % MEM_HEADS
           == lax.broadcasted_iota(jnp.int32, (rows, cols), 0) // tq)
    for s_i in range(seqs):
        q = q_ref[s_i]
        qb = jnp.concatenate([q[:, h * MEM_HEAD_DIM:(h + 1) * MEM_HEAD_DIM] for h in range(MEM_HEADS)], axis=0)
        s = lax.dot_general(qb.astype(BF16), k2[s_i].astype(BF16), _TRANS_B, preferred_element_type=F32) * scale
        p = _softmax(jnp.where(own, s, NEG_INF)).astype(BF16)
        o = _dot(p, v2[s_i].astype(BF16))
        for h in range(MEM_HEADS):
            o_ref[s_i, :, h * MEM_HEAD_DIM:(h + 1) * MEM_HEAD_DIM] = o[h * tq:(h + 1) * tq, :]


def _mem_decode(q, k, v, layer, seqs):
    nseq, tq, _ = q.shape
    seqs = min(seqs, nseq)
    cache = pl.BlockSpec((None, seqs, MEM_TOKENS, MEM_HEADS, MEM_HEAD_DIM), lambda i: (layer, i, 0, 0, 0))
    return pl.pallas_call(
        functools.partial(_mem_decode_kernel, seqs=seqs),
        grid=(nseq // seqs,),
        in_specs=[pl.BlockSpec((seqs, tq, MEM_WIDTH), lambda i: (i, 0, 0)), cache, cache],
        out_specs=pl.BlockSpec((seqs, tq, MEM_WIDTH), lambda i: (i, 0, 0)),
        out_shape=jax.ShapeDtypeStruct((nseq, tq, MEM_WIDTH), F32),
        compiler_params=_cparams("parallel"),
        name="mem_decode",
    )(q, k, v)


ROUTER_ROWS = 40
GATES_COL0 = SSM_WIDTH + SWA_WIDTH + 2 * SWA_KV_WIDTH + MEM_WIDTH
ROUTE_ROWS = 8
HALF = D_MODEL // 2


def _pack_halves(xb):
    hi = pltpu.bitcast(xb[:, 0:HALF].astype(F32), jnp.int32)
    lo = pltpu.bitcast(xb[:, HALF:D_MODEL].astype(F32), jnp.int32)
    return hi | lax.shift_right_logical(lo, jnp.int32(16))


def _unpack_halves(p):
    hi = pltpu.bitcast(p & jnp.int32(-65536), F32).astype(BF16)
    lo = pltpu.bitcast(lax.shift_left(p, jnp.int32(16)), F32).astype(BF16)
    return hi, lo


def _merge_kernel(x_ref, u_ref, y_ref, os_ref, om_ref, g1_ref, wg_ref, dsk_ref, wglu_ref, bglu_ref,
                  wbs_ref, wbw_ref, wbm_ref, wout_ref, g2_ref, wr_ref, br_ref,
                  h_ref, xn2_ref, route_ref, cnt_ref, base_ref, tri_ref):
    x = x_ref[...]
    tt = x.shape[0]
    xb = _rms(x, g1_ref[...]).astype(BF16)
    z = jax.nn.gelu(y_ref[...] + dsk_ref[...] * u_ref[...])
    z = z * jax.nn.sigmoid(_dot(z.astype(BF16), wglu_ref[...]) + bglu_ref[...])
    gate = lambda b: jax.nn.sigmoid(_dot(xb, wg_ref[:, GATES_COL0 + b * D_MODEL:GATES_COL0 + (b + 1) * D_MODEL]))
    merged = gate(0) * _dot(z.astype(BF16), wbs_ref[...])
    merged = merged + gate(1) * _dot(os_ref[...], wbw_ref[...])
    merged = merged + gate(2) * _dot(om_ref[...], wbm_ref[...])
    h = x + _dot(merged.astype(BF16), wout_ref[...])
    h_ref[...] = h
    xn2 = _rms(h, g2_ref[...]).astype(BF16)
    xn2_ref[...] = _pack_halves(xn2)

    lt = lax.dot_general(wr_ref[...], xn2, _TRANS_B, preferred_element_type=F32) + br_ref[...]
    gl = lt[N_EXPERTS:N_EXPERTS + N_EXPERT_GROUPS]
    ge = jnp.exp(gl - jnp.max(gl, axis=0, keepdims=True))
    gp = ge / jnp.sum(ge, axis=0, keepdims=True)
    gw = jnp.max(gp, axis=0, keepdims=True)
    gidx = jnp.full((1, tt), N_EXPERT_GROUPS - 1, jnp.int32)
    for r in range(N_EXPERT_GROUPS - 2, -1, -1):
        gidx = jnp.where(gp[r:r + 1] == gw, r, gidx)
    ein = lt[(N_EXPERT_GROUPS - 1) * EXPERTS_PER_GROUP:N_EXPERTS]
    for r in range(N_EXPERT_GROUPS - 2, -1, -1):
        ein = jnp.where(gidx == r, lt[r * EXPERTS_PER_GROUP:(r + 1) * EXPERTS_PER_GROUP], ein)
    ee = jnp.exp(ein - jnp.max(ein, axis=0, keepdims=True))
    ep = ee / jnp.sum(ee, axis=0, keepdims=True)
    rowi = lax.broadcasted_iota(jnp.int32, (EXPERTS_PER_GROUP, tt), 0)
    p1 = jnp.max(ep, axis=0, keepdims=True)
    e1 = jnp.min(jnp.where(ep == p1, rowi, EXPERTS_PER_GROUP), axis=0, keepdims=True)
    ep2 = jnp.where(rowi == e1, -1.0, ep)
    p2 = jnp.max(ep2, axis=0, keepdims=True)
    e2 = jnp.min(jnp.where(ep2 == p2, rowi, EXPERTS_PER_GROUP), axis=0, keepdims=True)
    tot = p1 + p2
    w1 = p1 / tot * gw
    w2 = p2 / tot * gw
    id1 = gidx * EXPERTS_PER_GROUP + e1
    id2 = gidx * EXPERTS_PER_GROUP + e2

    step = pl.program_id(0)

    @pl.when(step == 0)
    def _():
        base_ref[...] = jnp.zeros_like(base_ref)
        before = lax.broadcasted_iota(jnp.int32, (tt, tt), 0) < lax.broadcasted_iota(jnp.int32, (tt, tt), 1)
        tri_ref[...] = jnp.where(before, 1.0, 0.0).astype(BF16)

    r32 = lax.broadcasted_iota(jnp.int32, (N_EXPERTS, tt), 0)
    oh1 = jnp.where(r32 == id1, 1.0, 0.0)
    oh2 = jnp.where(r32 == id2, 1.0, 0.0)
    c1 = _dot(oh1.astype(BF16), tri_ref[...])
    c2 = _dot(oh2.astype(BF16), tri_ref[...])
    tot1 = jnp.sum(oh1, axis=1, keepdims=True)
    tot2 = jnp.sum(oh2, axis=1, keepdims=True)
    base = base_ref[:, 0:1]
    rank1 = jnp.sum(oh1 * (base + c1), axis=0, keepdims=True)
    rank2 = jnp.sum(oh2 * (base + tot1 + c2), axis=0, keepdims=True)
    new_base = jnp.broadcast_to(base + tot1 + tot2, base_ref.shape)
    base_ref[...] = new_base
    cnt_ref[...] = new_base
    route_ref[...] = jnp.concatenate([id1.astype(F32), id2.astype(F32), w1, w2, rank1, rank2,
                                      jnp.zeros((ROUTE_ROWS - 6, tt), F32)], axis=0)


def _merge(x, u, y, o_swa, o_mem, p, tile):
    n = x.shape[0]
    tile = min(tile, n)
    row = lambda i: (i, 0)
    const = lambda i: (0, 0)
    full = lambda a: pl.BlockSpec(a.shape, const, pipeline_mode=pl.Buffered(1))
    weights = [p['g1'], p['w_gates'], p['d_skip'], p['w_glu'], p['b_glu'], p['w_br_ssm'], p['w_br_swa'],
               p['w_br_mem'], p['w_out'], p['g2'], p['w_router'], p['b_router']]
    return pl.pallas_call(
        _merge_kernel,
        grid=(n // tile,),
        in_specs=[pl.BlockSpec((tile, D_MODEL), row), pl.BlockSpec((tile, SSM_WIDTH), row),
                  pl.BlockSpec((tile, SSM_WIDTH), row), pl.BlockSpec((tile, SWA_WIDTH), row),
                  pl.BlockSpec((tile, MEM_WIDTH), row)] + [full(w) for w in weights],
        out_specs=[pl.BlockSpec((tile, D_MODEL), row), pl.BlockSpec((tile, HALF), row),
                   pl.BlockSpec((ROUTE_ROWS, tile), lambda i: (0, i)),
                   pl.BlockSpec((N_EXPERTS, LANES), const)],
        out_shape=[jax.ShapeDtypeStruct((n, D_MODEL), F32), jax.ShapeDtypeStruct((n, HALF), jnp.int32),
                   jax.ShapeDtypeStruct((ROUTE_ROWS, n), F32), jax.ShapeDtypeStruct((N_EXPERTS, LANES), F32)],
        scratch_shapes=[pltpu.VMEM((N_EXPERTS, LANES), F32), pltpu.VMEM((tile, tile), BF16)],
        compiler_params=_cparams("arbitrary"),
        name="merge_router",
    )(x, u, y, o_swa, o_mem, *weights)


def _expert_mlp(xp, wg, wu, wd):
    hi, lo = _unpack_halves(xp)
    g = _dot(hi, wg[0:HALF, :]) + _dot(lo, wg[HALF:D_MODEL, :])
    u = _dot(hi, wu[0:HALF, :]) + _dot(lo, wu[HALF:D_MODEL, :])
    hh = jax.nn.silu(g) * u
    return _dot(hh.astype(BF16), wd[...])


def _moe_kernel(xn2_ref, rt_ref, wg_ref, wu_ref, wd_ref, h_ref, gf_ref, o_ref, acc_ref):
    e = pl.program_id(1)

    @pl.when(e == 0)
    def _():
        acc_ref[...] = jnp.zeros_like(acc_ref)

    o = _expert_mlp(xn2_ref[...], wg_ref[...].astype(BF16), wu_ref[...].astype(BF16), wd_ref[...].astype(BF16))
    ef = e.astype(F32)
    c = (jnp.where(rt_ref[:, 0:1] == ef, rt_ref[:, 2:3], 0.0)
         + jnp.where(rt_ref[:, 1:2] == ef, rt_ref[:, 3:4], 0.0))
    acc_ref[...] += c * o

    @pl.when(e == N_EXPERTS - 1)
    def _():
        o_ref[...] = _rms(h_ref[...] + acc_ref[...], gf_ref[...])


def _moe(xn2, route_t, w_g, w_u, w_d, h, gf, tile):
    n = h.shape[0]
    tile = min(tile, n)
    return pl.pallas_call(
        _moe_kernel,
        grid=(n // tile, N_EXPERTS),
        in_specs=[pl.BlockSpec((tile, HALF), lambda i, e: (i, 0)),
                  pl.BlockSpec((tile, ROUTE_ROWS), lambda i, e: (i, 0)),
                  pl.BlockSpec((None, D_MODEL, D_EXPERT), lambda i, e: (e, 0, 0)),
                  pl.BlockSpec((None, D_MODEL, D_EXPERT), lambda i, e: (e, 0, 0)),
                  pl.BlockSpec((None, D_EXPERT, D_MODEL), lambda i, e: (e, 0, 0)),
                  pl.BlockSpec((tile, D_MODEL), lambda i, e: (i, 0)),
                  pl.BlockSpec((1, D_MODEL), lambda i, e: (0, 0))],
        out_specs=pl.BlockSpec((tile, D_MODEL), lambda i, e: (i, 0)),
        out_shape=jax.ShapeDtypeStruct((n, D_MODEL), F32),
        scratch_shapes=[pltpu.VMEM((tile, D_MODEL), F32)],
        compiler_params=_cparams("parallel", "arbitrary"),
        name="moe_final_norm",
    )(xn2, route_t, w_g, w_u, w_d, h, gf)


EXPERT_ROW_TILE = 256
EXPERT_SLOTS = 4
SC_CORES = 2
SC_SUBCORES = 16
SC_WORKERS = SC_CORES * SC_SUBCORES
SC_SCATTER_ROWS = 64
SC_GATHER_ROWS = 64


def _sc_mesh():
    return plsc.VectorSubcoreMesh(core_axis_name="core", subcore_axis_name="subcore")


def _sc_scatter_pairs(x, pos, rows_out):
    n, d = x.shape
    per_w = n // SC_WORKERS
    window = min(SC_SCATTER_ROWS, per_w)

    @pl.kernel(out_type=jax.ShapeDtypeStruct((rows_out, d), x.dtype), mesh=_sc_mesh(),
               scratch_types=[pltpu.VMEM((window,), jnp.int32), pltpu.VMEM((window,), jnp.int32),
                              pltpu.VMEM((window, d), x.dtype), pltpu.SemaphoreType.DMA, pltpu.SemaphoreType.DMA,
                              pltpu.SemaphoreType.DMA])
    def scatter(x_hbm, p_hbm, o_hbm, i1_v, i2_v, rows_v, sem_a, sem_b, sem_c):
        wid = lax.axis_index("subcore") * SC_CORES + lax.axis_index("core")

        @pl.loop(0, per_w // window)
        def _(j):
            base = wid * per_w + j * window
            load_i1 = pltpu.async_copy(p_hbm.at[pl.ds(base, window)], i1_v, sem_a)
            load_i2 = pltpu.async_copy(p_hbm.at[pl.ds(n + base, window)], i2_v, sem_b)
            load_x = pltpu.async_copy(x_hbm.at[pl.ds(base, window)], rows_v, sem_c)
            load_i1.wait()
            load_i2.wait()
            load_x.wait()
            put_1 = pltpu.async_copy(rows_v, o_hbm.at[i1_v], sem_a)
            put_2 = pltpu.async_copy(rows_v, o_hbm.at[i2_v], sem_b)
            put_1.wait()
            put_2.wait()

    return scatter(x, pos)


def _sc_gather_rows(table, idx):
    m = idx.shape[0]
    d = table.shape[1]
    per_w = m // SC_WORKERS
    window = min(SC_GATHER_ROWS, per_w)

    assert per_w % (2 * window) == 0

    @pl.kernel(out_type=jax.ShapeDtypeStruct((m, d), table.dtype), mesh=_sc_mesh(),
               scratch_types=[pltpu.VMEM((window,), jnp.int32), pltpu.VMEM((window,), jnp.int32),
                              pltpu.VMEM((window, d), table.dtype), pltpu.VMEM((window, d), table.dtype),
                              pltpu.SemaphoreType.DMA, pltpu.SemaphoreType.DMA])
    def gather(t_hbm, i_hbm, o_hbm, ia_v, ib_v, ra_v, rb_v, sem_a, sem_b):
        wid = lax.axis_index("subcore") * SC_CORES + lax.axis_index("core")

        @pl.loop(0, per_w // (2 * window))
        def _(j):
            base_a = wid * per_w + j * (2 * window)
            base_b = base_a + window
            idx_a = pltpu.async_copy(i_hbm.at[pl.ds(base_a, window)], ia_v, sem_a)
            idx_b = pltpu.async_copy(i_hbm.at[pl.ds(base_b, window)], ib_v, sem_b)
            idx_a.wait()
            get_a = pltpu.async_copy(t_hbm.at[ia_v], ra_v, sem_a)
            idx_b.wait()
            get_b = pltpu.async_copy(t_hbm.at[ib_v], rb_v, sem_b)
            get_a.wait()
            put_a = pltpu.async_copy(ra_v, o_hbm.at[pl.ds(base_a, window)], sem_a)
            get_b.wait()
            put_b = pltpu.async_copy(rb_v, o_hbm.at[pl.ds(base_b, window)], sem_b)
            put_a.wait()
            put_b.wait()

    return gather(table, idx)


def _expert_tiles_kernel(start_ref, ntile_ref, x_hbm, wg_ref, wu_ref, wd_ref, o_hbm,
                         wg_s, wu_s, wd_s, x_buf, o_buf, in_sem, out_sem):
    e = pl.program_id(0)
    tm = x_buf.shape[1]
    nslot = x_buf.shape[0]
    first = start_ref[e] // tm
    ntile = ntile_ref[e]
    total = start_ref[N_EXPERTS - 1] // tm + ntile_ref[N_EXPERTS - 1]
    wg_s[...] = wg_ref[...].astype(BF16)
    wu_s[...] = wu_ref[...].astype(BF16)
    wd_s[...] = wd_ref[...].astype(BF16)

    def rows_of(g):
        return pl.ds(pl.multiple_of(g * tm, tm), tm)

    def fetch(g):
        slot = g % nslot
        return pltpu.make_async_copy(x_hbm.at[rows_of(g)], x_buf.at[slot], in_sem.at[slot])

    def flush(g):
        slot = g % nslot
        return pltpu.make_async_copy(o_buf.at[slot], o_hbm.at[rows_of(g)], out_sem.at[slot])

    @pl.when(e == 0)
    def _():
        for k in range(nslot - 1):
            @pl.when(k < total)
            def _(k=k):
                fetch(k).start()

    def tile(g, carry):
        @pl.when(g + nslot - 1 < total)
        def _():
            fetch(g + nslot - 1).start()

        fetch(g).wait()

        @pl.when(g >= nslot)
        def _():
            flush(g - nslot).wait()

        slot = g % nslot
        o_buf[slot] = _pack_halves(_expert_mlp(x_buf[slot], wg_s, wu_s, wd_s).astype(BF16))
        flush(g).start()
        return carry

    lax.fori_loop(first, first + ntile, tile, 0)

    @pl.when(e == N_EXPERTS - 1)
    def _():
        for k in range(nslot, 0, -1):
            @pl.when(total >= k)
            def _(k=k):
                flush(total - k).wait()


def _expert_tiles(starts, ntiles, xs, w_g, w_u, w_d):
    rows = xs.shape[0]
    tm = EXPERT_ROW_TILE
    weight = lambda shape: pl.BlockSpec((None,) + shape, lambda e, st, nt: (e, 0, 0))
    grid_spec = pltpu.PrefetchScalarGridSpec(
        num_scalar_prefetch=2,
        grid=(N_EXPERTS,),
        in_specs=[pl.BlockSpec(memory_space=pl.ANY),
                  weight((D_MODEL, D_EXPERT)), weight((D_MODEL, D_EXPERT)), weight((D_EXPERT, D_MODEL))],
        out_specs=pl.BlockSpec(memory_space=pl.ANY),
        scratch_shapes=[pltpu.VMEM((D_MODEL, D_EXPERT), BF16), pltpu.VMEM((D_MODEL, D_EXPERT), BF16),
                        pltpu.VMEM((D_EXPERT, D_MODEL), BF16),
                        pltpu.VMEM((EXPERT_SLOTS, tm, HALF), jnp.int32), pltpu.VMEM((EXPERT_SLOTS, tm, HALF), jnp.int32),
                        pltpu.SemaphoreType.DMA((EXPERT_SLOTS,)), pltpu.SemaphoreType.DMA((EXPERT_SLOTS,))],
    )
    return pl.pallas_call(
        _expert_tiles_kernel,
        grid_spec=grid_spec,
        out_shape=jax.ShapeDtypeStruct((rows, HALF), jnp.int32),
        compiler_params=_cparams("arbitrary"),
        name="expert_tiles",
    )(starts, ntiles, xs, w_g, w_u, w_d)


def _unpack_f32(p):
    return pltpu.bitcast(p & jnp.int32(-65536), F32), pltpu.bitcast(lax.shift_left(p, jnp.int32(16)), F32)


def _combine_kernel(h_ref, o1_ref, o2_ref, rt_ref, gf_ref, y_ref):
    w1, w2 = rt_ref[:, 2:3], rt_ref[:, 3:4]
    a_lo, a_hi = _unpack_f32(o1_ref[...])
    b_lo, b_hi = _unpack_f32(o2_ref[...])
    y_lo = h_ref[:, 0:HALF] + (w1 * a_lo + w2 * b_lo)
    y_hi = h_ref[:, HALF:D_MODEL] + (w1 * a_hi + w2 * b_hi)
    ms = (jnp.sum(y_lo * y_lo, axis=-1, keepdims=True) + jnp.sum(y_hi * y_hi, axis=-1, keepdims=True)) / D_MODEL
    inv = lax.rsqrt(ms + EPS)
    y_ref[:, 0:HALF] = (y_lo * inv) * gf_ref[:, 0:HALF]
    y_ref[:, HALF:D_MODEL] = (y_hi * inv) * gf_ref[:, HALF:D_MODEL]


def _combine(h, o12, route_t, gf, tile):
    n = h.shape[0]
    tile = min(tile, n)
    nt = n // tile
    return pl.pallas_call(
        _combine_kernel,
        grid=(nt,),
        in_specs=[pl.BlockSpec((tile, D_MODEL), lambda i: (i, 0)),
                  pl.BlockSpec((tile, HALF), lambda i: (i, 0)),
                  pl.BlockSpec((tile, HALF), lambda i: (i + nt, 0)),
                  pl.BlockSpec((tile, ROUTE_ROWS), lambda i: (i, 0)),
                  pl.BlockSpec((1, D_MODEL), lambda i: (0, 0))],
        out_specs=pl.BlockSpec((tile, D_MODEL), lambda i: (i, 0)),
        out_shape=jax.ShapeDtypeStruct((n, D_MODEL), F32),
        compiler_params=_cparams("parallel"),
        name="combine_final_norm",
    )(h, o12, o12, route_t, gf)


def _sparse_moe(xn2p, route, cnt, h, w_g, w_u, w_d, gf, run_before_experts):
    n = h.shape[0]
    tm = EXPERT_ROW_TILE
    rows = 2 * n + N_EXPERTS * tm
    rank = route[4:6].astype(jnp.int32)
    counts = cnt[:, 0].astype(jnp.int32)
    padded = (counts + tm - 1) // tm * tm
    e_idx = jnp.arange(N_EXPERTS, dtype=jnp.int32)
    starts = jnp.sum(jnp.where(e_idx[None, :] < e_idx[:, None], padded[None, :], 0), axis=1)
    ids = route[0:2].astype(jnp.int32)
    start_of = jnp.sum(jnp.where(ids[None] == e_idx[:, None, None], starts[:, None, None], 0), axis=0)
    pos = (start_of + rank).reshape(2 * n)
    xs = _sc_scatter_pairs(xn2p, pos, rows)
    xs, _ = lax.optimization_barrier((xs, run_before_experts))
    os_ = _expert_tiles(starts.astype(jnp.int32), (padded // tm).astype(jnp.int32), xs, w_g, w_u, w_d)
    o12 = _sc_gather_rows(os_, pos)
    return _combine(h, o12, route.T, gf, ROWS_COMBINE)


def _prep_in_weights(w_in):
    o = 0
    w_u = w_in[:, o:o + SSM_WIDTH]; o += SSM_WIDTH
    w_q = w_in[:, o:o + SWA_WIDTH]; o += SWA_WIDTH
    w_k = w_in[:, o:o + SWA_KV_WIDTH]; o += SWA_KV_WIDTH
    w_v = w_in[:, o:o + SWA_KV_WIDTH]; o += SWA_KV_WIDTH
    w_qm = w_in[:, o:o + MEM_WIDTH]; o += MEM_WIDTH
    assert o == GATES_COL0
    wq = (w_q * (SWA_HEAD_DIM ** -0.5)).reshape(D_MODEL, SWA_KV_HEADS, SWA_REP, SWA_HEAD_DIM)
    wq = wq.transpose(0, 2, 1, 3).reshape(D_MODEL, SWA_WIDTH)
    w_main = jnp.concatenate([w_u, wq, w_k, w_v, w_qm], axis=1).astype(BF16)
    return w_main, w_in.astype(BF16)


IN_SPLITS = (SSM_WIDTH, SWA_WIDTH, SWA_KV_WIDTH, SWA_KV_WIDTH, MEM_WIDTH)
IN_DTYPES = ((F32, BF16), (BF16,), (F32,), (F32,), (BF16,))


def kernel(x_prompt, x_sample, cache_swa_k, cache_swa_v, state_ssm_re, state_ssm_im, cache_mem_k, cache_mem_v, mem_prompt, norm1_g, w_in, lam_re, lam_im, log_dt, bm_re, bm_im, cm_re, cm_im, d_skip, w_glu, b_glu, sinks, rel_table, mem_norm_g, w_mem_kv, w_br_ssm, w_br_swa, w_br_mem, w_out, norm2_g, w_rg, b_rg, w_rexp, b_rexp, w_e_gate, w_e_up, w_e_down, final_norm_g):
    nb, t, _ = x_prompt.shape
    ns, ts, _ = x_sample.shape
    assert w_in.shape[0] == 1 and ts == S5_CHUNK and t % (WINDOW * SWA_BLOCKS_PER_STEP) == 0
    l = 0
    L = S5_CHUNK

    w_main, w_gates = _prep_in_weights(w_in[l])
    w_swa = (w_br_swa[l].reshape(SWA_KV_HEADS, SWA_REP, SWA_HEAD_DIM, D_MODEL).transpose(1, 0, 2, 3)
             .reshape(SWA_WIDTH, D_MODEL))
    pad_rows = ROUTER_ROWS - N_EXPERTS - N_EXPERT_GROUPS
    w_router = jnp.concatenate([w_rexp[l].T, w_rg[l].T, jnp.zeros((pad_rows, D_MODEL), F32)], axis=0).astype(BF16)
    b_router = jnp.concatenate([b_rexp[l], b_rg[l], jnp.zeros((pad_rows,), F32)]).reshape(ROUTER_ROWS, 1)
    mp = {
        'g1': norm1_g[l].reshape(1, D_MODEL), 'w_gates': w_gates, 'd_skip': d_skip[l].reshape(1, SSM_WIDTH),
        'w_glu': w_glu[l].astype(BF16), 'b_glu': b_glu[l].reshape(1, SSM_WIDTH),
        'w_br_ssm': w_br_ssm[l].astype(BF16), 'w_br_swa': w_swa.astype(BF16),
        'w_br_mem': w_br_mem[l].astype(BF16), 'w_out': w_out[l].astype(BF16),
        'g2': norm2_g[l].reshape(1, D_MODEL), 'w_router': w_router, 'b_router': b_router,
    }
    w_g, w_u, w_d = w_e_gate[l], w_e_up[l], w_e_down[l]
    gf = final_norm_g.reshape(1, D_MODEL)
    s5_w = _s5_weights(lam_re[l], lam_im[l], log_dt[l], bm_re[l], bm_im[l], cm_re[l], cm_im[l], L)

    bias_p = _rel_bias(rel_table, np.arange(WINDOW)[:, None] + WINDOW - np.arange(2 * WINDOW)[None, :])
    keys_s = WINDOW + 2 * ts
    bias_s = _rel_bias(rel_table, np.arange(ts)[:, None] + WINDOW - np.arange(keys_s)[None, :])
    bias_s = bias_s.reshape(SWA_HEADS * ts, keys_s)
    sink_rows = jnp.repeat(sinks[l].astype(F32), ts).reshape(SWA_HEADS * ts, 1)

    n = nb * t
    xp = x_prompt.reshape(n, D_MODEL)
    mk, mv = _norm_proj(mem_prompt.reshape(nb * MEM_TOKENS, D_MODEL), mem_norm_g[l].reshape(1, D_MODEL),
                        w_mem_kv[l].astype(BF16), (MEM_WIDTH, MEM_WIDTH), ((F32,), (F32,)), ROWS_MEM_PROJ)
    u, ub, qz, k, v, qm = _norm_proj(xp, mp['g1'], w_main, IN_SPLITS, IN_DTYPES, ROWS_NORM_PROJ)

    y_ssm, fin = _s5(ub, jnp.zeros((nb, N_CH_TILES * 2 * STATE_TILE), F32), s5_w, nb, t // L, L, S5_CHUNKS_PER_STEP)
    p_re, p_im = _tiles_to_state(fin)

    o_swa = _swa_prompt(qz, k, v, bias_p, sinks[l].astype(F32), nb, t, SWA_BLOCKS_PER_STEP)
    o_mem = _mem_prompt(qm, mk, mv, nb, t, ROWS_MEM_ATTN)
    h, xn2p, route, cnt = _merge(xp, u, y_ssm, o_swa, o_mem, mp, ROWS_MERGE)

    k4 = k.reshape(nb, t, SWA_KV_HEADS, SWA_HEAD_DIM)
    v4 = v.reshape(nb, t, SWA_KV_HEADS, SWA_HEAD_DIM)
    new_k_p, new_v_p = k4[:, -WINDOW:][None], v4[:, -WINDOW:][None]
    new_mk = mk.reshape(1, nb, MEM_TOKENS, MEM_HEADS, MEM_HEAD_DIM)
    new_mv = mv.reshape(1, nb, MEM_TOKENS, MEM_HEADS, MEM_HEAD_DIM)

    m = ns * ts
    xs = x_sample.reshape(m, D_MODEL)
    us, ubs, qzs, k_s, v_s, qms = _norm_proj(xs, mp['g1'], w_main, IN_SPLITS, IN_DTYPES, ROWS_NORM_PROJ)
    ys_ssm, fins = _s5(ubs, _state_to_tiles(state_ssm_re[l], state_ssm_im[l]), s5_w, ns, ts // L, L, S5_CHUNKS_PER_STEP)
    s_re, s_im = _tiles_to_state(fins)

    kk_all = jnp.concatenate([cache_swa_k[l].reshape(ns, WINDOW, SWA_KV_WIDTH).astype(F32),
                              k_s.reshape(ns, ts, SWA_KV_WIDTH)], axis=1)
    vv_all = jnp.concatenate([cache_swa_v[l].reshape(ns, WINDOW, SWA_KV_WIDTH).astype(F32),
                              v_s.reshape(ns, ts, SWA_KV_WIDTH)], axis=1)
    pad = jnp.zeros((ns, keys_s - WINDOW - ts, SWA_KV_WIDTH), F32)
    o_dec, roll_k, roll_v = _swa_decode(qzs.astype(F32).reshape(ns, ts, SWA_WIDTH),
                                        jnp.concatenate([kk_all, pad], axis=1),
                                        jnp.concatenate([vv_all, pad], axis=1), bias_s, sink_rows, SWA_DECODE_SEQS_PER_STEP)
    o_dec, q_mem = lax.optimization_barrier((o_dec, qms.astype(F32).reshape(ns, ts, MEM_WIDTH)))
    o_swa_s = o_dec.reshape(m, SWA_WIDTH).astype(BF16)

    o_mem_s = _mem_decode(q_mem, cache_mem_k, cache_mem_v, l, MEM_DECODE_SEQS_PER_STEP)
    o_mem_s = o_mem_s.reshape(m, MEM_WIDTH).astype(BF16)

    y_prompt = _sparse_moe(xn2p, route, cnt, h, w_g, w_u, w_d, gf, (ys_ssm, o_swa_s, o_mem_s)).reshape(nb, t, D_MODEL)
    hs_, xn2ps, routes, _ = _merge(xs, us, ys_ssm, o_swa_s, o_mem_s, mp, ROWS_MERGE)
    y_sample = _moe(xn2ps, routes.T, w_g, w_u, w_d, hs_, gf, ROWS_DENSE_MOE).reshape(ns, ts, D_MODEL)

    new_k_s = roll_k.reshape(1, ns, WINDOW, SWA_KV_HEADS, SWA_HEAD_DIM).astype(cache_swa_k.dtype)
    new_v_s = roll_v.reshape(1, ns, WINDOW, SWA_KV_HEADS, SWA_HEAD_DIM).astype(cache_swa_v.dtype)

    return (y_prompt, y_sample,
            new_k_p, new_v_p, p_re[None], p_im[None], new_mk, new_mv,
            new_k_s, new_v_s, s_re[None].astype(state_ssm_re.dtype), s_im[None].astype(state_ssm_im.dtype))
```

```python
import functools
import math

import numpy as np
import jax
import jax.numpy as jnp
from jax import lax
from jax.experimental import pallas as pl
from jax.experimental.pallas import tpu as pltpu
from jax.experimental.pallas import tpu_sc as plsc

F32 = jnp.float32
BF16 = jnp.bfloat16

D_MODEL = 1024
SSM_WIDTH = 512
SSM_GROUP = 16
SSM_GROUPS = 32
SSM_STATE = 64
SWA_HEADS = 8
SWA_KV_HEADS = 2
SWA_REP = 4
SWA_HEAD_DIM = 64
SWA_WIDTH = 512
SWA_KV_WIDTH = 128
WINDOW = 128
REL_BUCKETS = 32
REL_MAX_DIST = 128
MEM_TOKENS = 256
MEM_HEADS = 4
MEM_HEAD_DIM = 128
MEM_WIDTH = 512
N_EXPERT_GROUPS = 4
EXPERTS_PER_GROUP = 8
N_EXPERTS = 32
D_EXPERT = 256
EPS = 1e-6
NEG_INF = -1e30

LANES = 128
GROUPS_PER_TILE = LANES // SSM_GROUP
N_CH_TILES = SSM_WIDTH // LANES
STATE_TILE = GROUPS_PER_TILE * SSM_STATE
VMEM_LIMIT = 56 * 1024 * 1024
ROWS_NORM_PROJ = 2048
ROWS_MEM_PROJ = 1024
ROWS_MEM_ATTN = 2048
ROWS_MERGE = 512
ROWS_COMBINE = 1024
ROWS_DENSE_MOE = 1024
S5_CHUNKS_PER_STEP = 128
SWA_BLOCKS_PER_STEP = 16
SWA_DECODE_SEQS_PER_STEP = 16
MEM_DECODE_SEQS_PER_STEP = 8
S5_CHUNK = 8
S5_PANEL = 256

_TRANS_B = (((1,), (1,)), ((), ()))


def _cparams(*sem):
    return pltpu.CompilerParams(dimension_semantics=sem, vmem_limit_bytes=VMEM_LIMIT)


def _rms(x, g):
    return (x * lax.rsqrt(jnp.mean(x * x, axis=-1, keepdims=True) + EPS)) * g


def _dot(a, b):
    return jnp.dot(a, b, preferred_element_type=F32)


def _norm_proj_kernel(x_ref, g_ref, w_ref, *out_refs, splits, dtypes):
    xb = _rms(x_ref[...], g_ref[...]).astype(BF16)
    off = 0
    outs = iter(out_refs)
    for width, dts in zip(splits, dtypes):
        r = _dot(xb, w_ref[:, off:off + width])
        for dt in dts:
            next(outs)[...] = r.astype(dt)
        off += width


def _norm_proj(x, g, w, splits, dtypes, tile):
    n, d = x.shape
    tile = min(tile, n)
    flat = [(wd, dt) for wd, dts in zip(splits, dtypes) for dt in dts]
    return pl.pallas_call(
        functools.partial(_norm_proj_kernel, splits=tuple(splits), dtypes=tuple(dtypes)),
        grid=(n // tile,),
        in_specs=[pl.BlockSpec((tile, d), lambda i: (i, 0)),
                  pl.BlockSpec((1, d), lambda i: (0, 0)),
                  pl.BlockSpec((d, sum(splits)), lambda i: (0, 0), pipeline_mode=pl.Buffered(1))],
        out_specs=[pl.BlockSpec((tile, wd), lambda i: (i, 0)) for wd, _ in flat],
        out_shape=[jax.ShapeDtypeStruct((n, wd), dt) for wd, dt in flat],
        compiler_params=_cparams("parallel"),
        name="norm_proj",
    )(x, g, w)


def _s5_weights(lam_re, lam_im, log_dt, bm_re, bm_im, cm_re, cm_im, L):
    nt, gt, P, H = N_CH_TILES, GROUPS_PER_TILE, SSM_STATE, SSM_GROUP
    lr, li = lam_re.astype(F32), lam_im.astype(F32)
    dt = jnp.exp(log_dt.astype(F32))[:, None]
    mag = jnp.exp(lr * dt)
    a_re = mag * jnp.cos(li * dt)
    a_im = mag * jnp.sin(li * dt)
    den = lr * lr + li * li
    f_re = ((a_re - 1.0) * lr + a_im * li) / den
    f_im = (a_im * lr - (a_re - 1.0) * li) / den
    br, bi = bm_re.astype(F32), bm_im.astype(F32)
    bb_re = f_re[..., None] * br - f_im[..., None] * bi
    bb_im = f_re[..., None] * bi + f_im[..., None] * br
    pr, pi = [jnp.ones_like(a_re)], [jnp.zeros_like(a_im)]
    for _ in range(L):
        pr.append(pr[-1] * a_re - pi[-1] * a_im)
        pi.append(pr[-2] * a_im + pi[-1] * a_re)
    ap_re, ap_im = jnp.stack(pr), jnp.stack(pi)
    cr, ci = cm_re.astype(F32), cm_im.astype(F32)
    ca_re = cr[None] * ap_re[:, :, None, :] - ci[None] * ap_im[:, :, None, :]
    ca_im = cr[None] * ap_im[:, :, None, :] + ci[None] * ap_re[:, :, None, :]

    rev_re = jnp.stack([pr[L - 1 - s] for s in range(L)])
    rev_im = jnp.stack([pi[L - 1 - s] for s in range(L)])
    ws_re = rev_re[..., None] * bb_re[None] - rev_im[..., None] * bb_im[None]
    ws_im = rev_re[..., None] * bb_im[None] + rev_im[..., None] * bb_re[None]
    c_st = jnp.concatenate([ws_re.transpose(0, 1, 3, 2).reshape(L, nt, gt * H, P),
                            ws_im.transpose(0, 1, 3, 2).reshape(L, nt, gt * H, P)], axis=3).transpose(1, 0, 2, 3)
    so = lambda ca: ca[1:].transpose(1, 3, 0, 2).reshape(nt, gt * P, L * H)
    c_so = jnp.concatenate([so(ca_re), so(-ca_im)], axis=1)
    prod = (ca_re[:L][:, :, None, :, :] * bb_re.transpose(0, 2, 1)[None, :, :, None, :]
            - ca_im[:L][:, :, None, :, :] * bb_im.transpose(0, 2, 1)[None, :, :, None, :])
    k_lag = jnp.sum(prod, axis=-1).transpose(1, 2, 0, 3)
    c_k = k_lag.reshape(nt, gt * H, L * H)
    w_st, w_out, toep = _s5_expand(c_st, c_so, c_k, L)

    def per_tile(v):
        return v.reshape(nt, 1, STATE_TILE)

    return w_st, w_out, toep, per_tile(pr[L]), per_tile(pi[L])


def _s5_expand_kernel(cst_ref, cso_ref, ck_ref, wst_ref, wso_ref, toep_ref, *, L):
    hp = lax.Precision.HIGHEST
    P, H = SSM_STATE, SSM_GROUP
    iota = lambda shape, d: lax.broadcasted_iota(jnp.int32, shape, d)
    one = lambda cond: jnp.where(cond, 1.0, 0.0).astype(F32)

    r, c = iota((2 * P, 2 * STATE_TILE), 0), iota((2 * P, 2 * STATE_TILE), 1)
    rep_st = one((r // P == c // STATE_TILE) & (r % P == c % P))
    r, c = iota((LANES, 2 * STATE_TILE), 0), iota((LANES, 2 * STATE_TILE), 1)
    own_st = one(r // H == (c % STATE_TILE) // P)
    for s in range(L):
        blk = jnp.dot(cst_ref[s], rep_st, precision=hp, preferred_element_type=F32) * own_st
        wst_ref[s * LANES:(s + 1) * LANES, :] = blk.astype(BF16)

    r, c = iota((LANES, LANES), 0), iota((LANES, LANES), 1)
    pick = [one((r // H == t) & (r % H == c % H)) for t in range(L)]
    own_k = one(r // H == c // H)
    r, c = iota((2 * STATE_TILE, LANES), 0), iota((2 * STATE_TILE, LANES), 1)
    own_so = one((r % STATE_TILE) // P == c // H)
    cso = cso_ref[...]
    for t in range(L):
        blk = jnp.dot(cso, pick[t], precision=hp, preferred_element_type=F32) * own_so
        wso_ref[:, t * LANES:(t + 1) * LANES] = blk.astype(BF16)
    ck = ck_ref[...]
    lag = [(jnp.dot(ck, pick[t], precision=hp, preferred_element_type=F32) * own_k).astype(BF16) for t in range(L)]
    zero = jnp.zeros((LANES, LANES), BF16)
    for s in range(L):
        for t in range(L):
            toep_ref[s * LANES:(s + 1) * LANES, t * LANES:(t + 1) * LANES] = lag[t - s] if t >= s else zero


def _s5_expand(c_st, c_so, c_k, L):
    lk = L * LANES
    st2 = 2 * STATE_TILE
    return pl.pallas_call(
        functools.partial(_s5_expand_kernel, L=L),
        grid=(N_CH_TILES,),
        in_specs=[pl.BlockSpec((None, L, LANES, 2 * SSM_STATE), lambda j: (j, 0, 0, 0)),
                  pl.BlockSpec((None, st2, L * SSM_GROUP), lambda j: (j, 0, 0)),
                  pl.BlockSpec((None, LANES, L * SSM_GROUP), lambda j: (j, 0, 0))],
        out_specs=[pl.BlockSpec((None, lk, st2), lambda j: (j, 0, 0)),
                   pl.BlockSpec((None, st2, lk), lambda j: (j, 0, 0)),
                   pl.BlockSpec((None, lk, lk), lambda j: (j, 0, 0))],
        out_shape=[jax.ShapeDtypeStruct((N_CH_TILES, lk, st2), BF16),
                   jax.ShapeDtypeStruct((N_CH_TILES, st2, lk), BF16),
                   jax.ShapeDtypeStruct((N_CH_TILES, lk, lk), BF16)],
        compiler_params=_cparams("parallel"),
        name="s5_expand_weights",
    )(c_st, c_so, c_k)


def _to_chunks(u, nb, nc, L):
    return (u.reshape(nb, nc, L, N_CH_TILES, LANES).transpose(1, 0, 3, 2, 4)
            .reshape(nc * nb, N_CH_TILES * L * LANES))


def _from_chunks(y, nb, nc, L):
    return (y.reshape(nc, nb, N_CH_TILES, L, LANES).transpose(1, 0, 3, 2, 4)
            .reshape(nb * nc * L, SSM_WIDTH))


def _s5_kernel(x_ref, h0_ref, are_ref, aim_ref, ws_ref, t_ref, wo_ref, y_ref, fin_ref,
               hr_ref, hi_ref, d_ref, hs_ref, *, cb, nb):
    ci = pl.program_id(1)

    @pl.when(ci == 0)
    def _():
        hr_ref[...] = h0_ref[:, 0:STATE_TILE]
        hi_ref[...] = h0_ref[:, STATE_TILE:2 * STATE_TILE]

    x = x_ref[...]
    d_ref[...] = _dot(x, ws_ref[...])
    ar = jnp.broadcast_to(are_ref[...], (nb, STATE_TILE))
    ai = jnp.broadcast_to(aim_ref[...], (nb, STATE_TILE))

    def body(c, carry):
        hr, hi = carry
        r0 = pl.multiple_of(c * nb, nb)
        hs_ref[pl.ds(r0, nb), 0:STATE_TILE] = hr
        hs_ref[pl.ds(r0, nb), STATE_TILE:2 * STATE_TILE] = hi
        d = d_ref[pl.ds(r0, nb), :]
        return (ar * hr - ai * hi + d[:, 0:STATE_TILE],
                ar * hi + ai * hr + d[:, STATE_TILE:2 * STATE_TILE])

    hr, hi = lax.fori_loop(0, cb, body, (hr_ref[...], hi_ref[...]))
    hr_ref[...] = hr
    hi_ref[...] = hi
    hsb = hs_ref[...].astype(BF16)
    for c0 in range(0, t_ref.shape[1], S5_PANEL):
        c1 = c0 + S5_PANEL
        y_ref[:, c0:c1] = _dot(x[:, 0:c1], t_ref[0:c1, c0:c1]) + _dot(hsb, wo_ref[:, c0:c1])

    @pl.when(ci == pl.num_programs(1) - 1)
    def _():
        fin_ref[:, 0:STATE_TILE] = hr
        fin_ref[:, STATE_TILE:2 * STATE_TILE] = hi


def _s5(ub, h0, weights, nb, nc, L, chunk_block):
    w_st, w_so, toep, a_re, a_im = weights
    xc = _to_chunks(ub, nb, nc, L)
    cb = min(chunk_block, nc)
    rows = cb * nb
    lk = L * LANES
    st2 = 2 * STATE_TILE
    tile_w = lambda shape: pl.BlockSpec((None,) + shape, lambda j, c: (j, 0, 0))
    y, fin = pl.pallas_call(
        functools.partial(_s5_kernel, cb=cb, nb=nb),
        grid=(N_CH_TILES, nc // cb),
        in_specs=[pl.BlockSpec((rows, lk), lambda j, c: (c, j)),
                  pl.BlockSpec((nb, st2), lambda j, c: (0, j)),
                  tile_w((1, STATE_TILE)), tile_w((1, STATE_TILE)),
                  tile_w((lk, st2)), tile_w((lk, lk)), tile_w((st2, lk))],
        out_specs=[pl.BlockSpec((rows, lk), lambda j, c: (c, j)),
                   pl.BlockSpec((nb, st2), lambda j, c: (0, j))],
        out_shape=[jax.ShapeDtypeStruct((nc * nb, N_CH_TILES * lk), F32),
                   jax.ShapeDtypeStruct((nb, N_CH_TILES * st2), F32)],
        scratch_shapes=[pltpu.VMEM((nb, STATE_TILE), F32), pltpu.VMEM((nb, STATE_TILE), F32),
                        pltpu.VMEM((rows, st2), F32), pltpu.VMEM((rows, st2), F32)],
        compiler_params=_cparams("parallel", "arbitrary"),
        name="s5_chunked_scan",
    )(xc, h0, a_re, a_im, w_st, toep, w_so)
    return _from_chunks(y, nb, nc, L), fin


def _state_to_tiles(h_re, h_im):
    nb = h_re.shape[0]
    r = h_re.astype(F32).reshape(nb, N_CH_TILES, STATE_TILE)
    i = h_im.astype(F32).reshape(nb, N_CH_TILES, STATE_TILE)
    return jnp.concatenate([r, i], axis=-1).reshape(nb, N_CH_TILES * 2 * STATE_TILE)


def _tiles_to_state(h):
    nb = h.shape[0]
    h = h.reshape(nb, N_CH_TILES, 2, GROUPS_PER_TILE, SSM_STATE)
    return (h[:, :, 0].reshape(nb, SSM_GROUPS, SSM_STATE), h[:, :, 1].reshape(nb, SSM_GROUPS, SSM_STATE))


def _t5_bucket(dist):
    n = np.maximum(dist, 0)
    max_exact = REL_BUCKETS // 2
    nf = np.maximum(n, 1).astype(np.float32)
    large = max_exact + (np.log(nf / np.float32(max_exact)) / np.float32(math.log(REL_MAX_DIST / max_exact))
                         * np.float32(REL_BUCKETS - max_exact)).astype(np.int32)
    large = np.minimum(large, REL_BUCKETS - 1)
    return np.where(n < max_exact, n, large)


def _rel_bias(rel_table, dist):
    bucket = _t5_bucket(dist)
    tab = rel_table.astype(F32)
    out = jnp.zeros((SWA_HEADS,) + dist.shape, F32)
    for b in range(REL_BUCKETS):
        sel = jnp.asarray(bucket == b)
        if bool((bucket == b).any()):
            out = jnp.where(sel[None], tab[b].reshape((SWA_HEADS,) + (1,) * dist.ndim), out)
    return out


def _swa_prompt_kernel(sink_ref, q_ref, kp_ref, kc_ref, vp_ref, vc_ref, bias_ref, o_ref, kk_ref, vv_ref, *, qblocks):
    step = pl.program_id(1)
    kk_ref[0:WINDOW, :] = kp_ref[...].astype(BF16)
    kk_ref[WINDOW:, :] = kc_ref[...].astype(BF16)
    vv_ref[0:WINDOW, :] = vp_ref[...].astype(BF16)
    vv_ref[WINDOW:, :] = vc_ref[...].astype(BF16)
    row = lax.broadcasted_iota(jnp.int32, (WINDOW, 2 * WINDOW), 0)
    col = lax.broadcasted_iota(jnp.int32, (WINDOW, 2 * WINDOW), 1)
    dist = row + WINDOW - col
    band = (dist >= 0) & (dist < WINDOW)
    lane = lax.broadcasted_iota(jnp.int32, (WINDOW, LANES), 1)
    low = lane < SWA_HEAD_DIM

    def block(j, carry):
        r0 = pl.multiple_of(j * WINDOW, WINDOW)
        kk = kk_ref[pl.ds(r0, 2 * WINDOW), :]
        vv = vv_ref[pl.ds(r0, 2 * WINDOW), :]
        valid = band & ((col >= WINDOW) | (step * qblocks + j > 0))
        for t in range(SWA_REP):
            q2 = q_ref[pl.ds(r0, WINDOW), t * LANES:(t + 1) * LANES]
            outs = []
            for half in range(SWA_KV_HEADS):
                h = t + SWA_REP * half
                qh = jnp.where(low if half == 0 else jnp.logical_not(low), q2, jnp.zeros_like(q2))
                s = lax.dot_general(qh, kk, _TRANS_B, preferred_element_type=F32)
                s = jnp.where(valid, s + bias_ref[h], NEG_INF)
                sink = sink_ref[h]
                m = jnp.maximum(jnp.max(s, axis=-1, keepdims=True), sink)
                e = jnp.exp(s - m)
                den = jnp.sum(e, axis=-1, keepdims=True) + jnp.exp(sink - m)
                outs.append(_dot(e.astype(BF16), vv) * (1.0 / den))
            o_ref[pl.ds(r0, WINDOW), t * LANES:(t + 1) * LANES] = jnp.where(low, outs[0], outs[1]).astype(BF16)
        return carry

    lax.fori_loop(0, qblocks, block, 0)


def _swa_prompt(q, k, v, bias, sinks, nb, t, qblocks):
    nstep = t // (WINDOW * qblocks)
    rows = WINDOW * qblocks
    cur = lambda b, i: (b * nstep + i, 0)
    prev = lambda b, i: (b * nstep * qblocks + jnp.maximum(i * qblocks - 1, 0), 0)
    return pl.pallas_call(
        functools.partial(_swa_prompt_kernel, qblocks=qblocks),
        grid=(nb, nstep),
        in_specs=[pl.BlockSpec(memory_space=pltpu.SMEM),
                  pl.BlockSpec((rows, SWA_WIDTH), cur),
                  pl.BlockSpec((WINDOW, SWA_KV_WIDTH), prev),
                  pl.BlockSpec((rows, SWA_KV_WIDTH), cur),
                  pl.BlockSpec((WINDOW, SWA_KV_WIDTH), prev),
                  pl.BlockSpec((rows, SWA_KV_WIDTH), cur),
                  pl.BlockSpec((SWA_HEADS, WINDOW, 2 * WINDOW), lambda b, i: (0, 0, 0))],
        out_specs=pl.BlockSpec((rows, SWA_WIDTH), cur),
        out_shape=jax.ShapeDtypeStruct((nb * t, SWA_WIDTH), BF16),
        scratch_shapes=[pltpu.VMEM((rows + WINDOW, SWA_KV_WIDTH), BF16),
                        pltpu.VMEM((rows + WINDOW, SWA_KV_WIDTH), BF16)],
        compiler_params=_cparams("parallel", "parallel"),
        name="swa_prompt",
    )(sinks, q, k, k, v, v, bias)


def _swa_decode_kernel(q_ref, k_ref, v_ref, bias_ref, sink_ref, o_ref, nk_ref, nv_ref, *, seqs, tq):
    rows, keys = SWA_HEADS * tq, k_ref.shape[1]
    nk_ref[...] = k_ref[:, tq:tq + WINDOW, :]
    nv_ref[...] = v_ref[:, tq:tq + WINDOW, :]
    low = lax.broadcasted_iota(jnp.int32, (tq, LANES), 1) < SWA_HEAD_DIM
    qi = lax.broadcasted_iota(jnp.int32, (rows, keys), 0) % tq
    col = lax.broadcasted_iota(jnp.int32, (rows, keys), 1)
    dist = qi + WINDOW - col
    valid = (dist >= 0) & (dist < WINDOW)
    bias = bias_ref[...]
    sink = sink_ref[...]
    for s_i in range(seqs):
        q = q_ref[s_i]
        tiles = [q[:, t * LANES:(t + 1) * LANES] for t in range(SWA_REP)]
        qh = jnp.concatenate([jnp.where(low, x, 0.0) for x in tiles]
                             + [jnp.where(low, 0.0, x) for x in tiles], axis=0)
        kk = k_ref[s_i].astype(BF16)
        s = lax.dot_general(qh.astype(BF16), kk, _TRANS_B, preferred_element_type=F32)
        s = jnp.where(valid, s + bias, NEG_INF)
        m = jnp.maximum(jnp.max(s, axis=-1, keepdims=True), sink)
        e = jnp.exp(s - m)
        den = jnp.sum(e, axis=-1, keepdims=True) + jnp.exp(sink - m)
        o = _dot(e.astype(BF16), v_ref[s_i].astype(BF16)) * (1.0 / den)
        for t in range(SWA_REP):
            o_ref[s_i, :, t * LANES:(t + 1) * LANES] = jnp.where(
                low, o[t * tq:(t + 1) * tq], o[(t + SWA_REP) * tq:(t + SWA_REP + 1) * tq])


def _swa_decode(q, k_all, v_all, bias, sink_rows, seqs):
    nseq, tq, _ = q.shape
    rows = SWA_HEADS * tq
    keys = k_all.shape[1]
    seqs = min(seqs, nseq)
    return pl.pallas_call(
        functools.partial(_swa_decode_kernel, seqs=seqs, tq=tq),
        grid=(nseq // seqs,),
        in_specs=[pl.BlockSpec((seqs, tq, SWA_WIDTH), lambda i: (i, 0, 0)),
                  pl.BlockSpec((seqs, keys, LANES), lambda i: (i, 0, 0)),
                  pl.BlockSpec((seqs, keys, LANES), lambda i: (i, 0, 0)),
                  pl.BlockSpec((rows, keys), lambda i: (0, 0)),
                  pl.BlockSpec((rows, 1), lambda i: (0, 0))],
        out_specs=[pl.BlockSpec((seqs, tq, SWA_WIDTH), lambda i: (i, 0, 0)),
                   pl.BlockSpec((seqs, WINDOW, LANES), lambda i: (i, 0, 0)),
                   pl.BlockSpec((seqs, WINDOW, LANES), lambda i: (i, 0, 0))],
        out_shape=[jax.ShapeDtypeStruct((nseq, tq, SWA_WIDTH), F32),
                   jax.ShapeDtypeStruct((nseq, WINDOW, LANES), F32),
                   jax.ShapeDtypeStruct((nseq, WINDOW, LANES), F32)],
        compiler_params=_cparams("parallel"),
        name="swa_decode",
    )(q, k_all, v_all, bias, sink_rows)


def _softmax(s):
    m = jnp.max(s, axis=-1, keepdims=True)
    e = jnp.exp(s - m)
    return e * (1.0 / jnp.sum(e, axis=-1, keepdims=True))


def _mem_prompt_kernel(q_ref, k_ref, v_ref, o_ref, s_ref, p_ref):
    scale = MEM_HEAD_DIM ** -0.5
    heads = [slice(h * MEM_HEAD_DIM, (h + 1) * MEM_HEAD_DIM) for h in range(MEM_HEADS)]
    for h, sl in enumerate(heads):
        s_ref[h] = lax.dot_general(q_ref[:, sl], k_ref[:, sl].astype(BF16), _TRANS_B, preferred_element_type=F32)
    s = s_ref[...] * scale
    e = jnp.exp(s - jnp.max(s, axis=-1, keepdims=True))
    p_ref[...] = e.astype(BF16)
    inv = 1.0 / jnp.sum(e, axis=-1, keepdims=True)
    for h, sl in enumerate(heads):
        o_ref[:, sl] = (_dot(p_ref[h], v_ref[:, sl].astype(BF16)) * inv[h]).astype(BF16)


def _mem_prompt(qm, mk, mv, nb, t, tile):
    tile = min(tile, t)
    nt = t // tile
    return pl.pallas_call(
        _mem_prompt_kernel,
        grid=(nb, nt),
        in_specs=[pl.BlockSpec((tile, MEM_WIDTH), lambda b, i: (b * nt + i, 0)),
                  pl.BlockSpec((MEM_TOKENS, MEM_WIDTH), lambda b, i: (b, 0)),
                  pl.BlockSpec((MEM_TOKENS, MEM_WIDTH), lambda b, i: (b, 0))],
        out_specs=pl.BlockSpec((tile, MEM_WIDTH), lambda b, i: (b * nt + i, 0)),
        out_shape=jax.ShapeDtypeStruct((nb * t, MEM_WIDTH), BF16),
        scratch_shapes=[pltpu.VMEM((MEM_HEADS, tile, MEM_TOKENS), F32), pltpu.VMEM((MEM_HEADS, tile, MEM_TOKENS), BF16)],
        compiler_params=_cparams("parallel", "parallel"),
        name="mem_prompt",
    )(qm, mk, mv)


def _mem_decode_kernel(q_ref, k_ref, v_ref, o_ref, *, seqs):
    tq = q_ref.shape[1]
    rows, cols = MEM_HEADS * tq, MEM_TOKENS * MEM_HEADS
    k2 = k_ref.reshape(seqs, cols, MEM_HEAD_DIM)
    v2 = v_ref.reshape(seqs, cols, MEM_HEAD_DIM)
    scale = MEM_HEAD_DIM ** -0.5
    own = (lax.broadcasted_iota(jnp.int32, (rows, cols), 1) % MEM_HEADS
           == lax.broadcasted_iota(jnp.int32, (rows, cols), 0) // tq)
    for s_i in range(seqs):
        q = q_ref[s_i]
        qb = jnp.concatenate([q[:, h * MEM_HEAD_DIM:(h + 1) * MEM_HEAD_DIM] for h in range(MEM_HEADS)], axis=0)
        s = lax.dot_general(qb.astype(BF16), k2[s_i].astype(BF16), _TRANS_B, preferred_element_type=F32) * scale
        p = _softmax(jnp.where(own, s, NEG_INF)).astype(BF16)
        o = _dot(p, v2[s_i].astype(BF16))
        for h in range(MEM_HEADS):
            o_ref[s_i, :, h * MEM_HEAD_DIM:(h + 1) * MEM_HEAD_DIM] = o[h * tq:(h + 1) * tq, :]


def _mem_decode(q, k, v, layer, seqs):
    nseq, tq, _ = q.shape
    seqs = min(seqs, nseq)
    cache = pl.BlockSpec((None, seqs, MEM_TOKENS, MEM_HEADS, MEM_HEAD_DIM), lambda i: (layer, i, 0, 0, 0))
    return pl.pallas_call(
        functools.partial(_mem_decode_kernel, seqs=seqs),
        grid=(nseq // seqs,),
        in_specs=[pl.BlockSpec((seqs, tq, MEM_WIDTH), lambda i: (i, 0, 0)), cache, cache],
        out_specs=pl.BlockSpec((seqs, tq, MEM_WIDTH), lambda i: (i, 0, 0)),
        out_shape=jax.ShapeDtypeStruct((nseq, tq, MEM_WIDTH), F32),
        compiler_params=_cparams("parallel"),
        name="mem_decode",
    )(q, k, v)


ROUTER_ROWS = 40
GATES_COL0 = SSM_WIDTH + SWA_WIDTH + 2 * SWA_KV_WIDTH + MEM_WIDTH
ROUTE_ROWS = 8
HALF = D_MODEL // 2


def _pack_halves(xb):
    hi = pltpu.bitcast(xb[:, 0:HALF].astype(F32), jnp.int32)
    lo = pltpu.bitcast(xb[:, HALF:D_MODEL].astype(F32), jnp.int32)
    return hi | lax.shift_right_logical(lo, jnp.int32(16))


def _unpack_halves(p):
    hi = pltpu.bitcast(p & jnp.int32(-65536), F32).astype(BF16)
    lo = pltpu.bitcast(lax.shift_left(p, jnp.int32(16)), F32).astype(BF16)
    return hi, lo


def _merge_kernel(x_ref, u_ref, y_ref, os_ref, om_ref, g1_ref, wg_ref, dsk_ref, wglu_ref, bglu_ref,
                  wbs_ref, wbw_ref, wbm_ref, wout_ref, g2_ref, wr_ref, br_ref,
                  h_ref, xn2_ref, route_ref, cnt_ref, base_ref, tri_ref):
    x = x_ref[...]
    tt = x.shape[0]
    xb = _rms(x, g1_ref[...]).astype(BF16)
    z = jax.nn.gelu(y_ref[...] + dsk_ref[...] * u_ref[...])
    z = z * jax.nn.sigmoid(_dot(z.astype(BF16), wglu_ref[...]) + bglu_ref[...])
    gate = lambda b: jax.nn.sigmoid(_dot(xb, wg_ref[:, GATES_COL0 + b * D_MODEL:GATES_COL0 + (b + 1) * D_MODEL]))
    merged = gate(0) * _dot(z.astype(BF16), wbs_ref[...])
    merged = merged + gate(1) * _dot(os_ref[...], wbw_ref[...])
    merged = merged + gate(2) * _dot(om_ref[...], wbm_ref[...])
    h = x + _dot(merged.astype(BF16), wout_ref[...])
    h_ref[...] = h
    xn2 = _rms(h, g2_ref[...]).astype(BF16)
    xn2_ref[...] = _pack_halves(xn2)

    lt = lax.dot_general(wr_ref[...], xn2, _TRANS_B, preferred_element_type=F32) + br_ref[...]
    gl = lt[N_EXPERTS:N_EXPERTS + N_EXPERT_GROUPS]
    ge = jnp.exp(gl - jnp.max(gl, axis=0, keepdims=True))
    gp = ge / jnp.sum(ge, axis=0, keepdims=True)
    gw = jnp.max(gp, axis=0, keepdims=True)
    gidx = jnp.full((1, tt), N_EXPERT_GROUPS - 1, jnp.int32)
    for r in range(N_EXPERT_GROUPS - 2, -1, -1):
        gidx = jnp.where(gp[r:r + 1] == gw, r, gidx)
    ein = lt[(N_EXPERT_GROUPS - 1) * EXPERTS_PER_GROUP:N_EXPERTS]
    for r in range(N_EXPERT_GROUPS - 2, -1, -1):
        ein = jnp.where(gidx == r, lt[r * EXPERTS_PER_GROUP:(r + 1) * EXPERTS_PER_GROUP], ein)
    ee = jnp.exp(ein - jnp.max(ein, axis=0, keepdims=True))
    ep = ee / jnp.sum(ee, axis=0, keepdims=True)
    rowi = lax.broadcasted_iota(jnp.int32, (EXPERTS_PER_GROUP, tt), 0)
    p1 = jnp.max(ep, axis=0, keepdims=True)
    e1 = jnp.min(jnp.where(ep == p1, rowi, EXPERTS_PER_GROUP), axis=0, keepdims=True)
    ep2 = jnp.where(rowi == e1, -1.0, ep)
    p2 = jnp.max(ep2, axis=0, keepdims=True)
    e2 = jnp.min(jnp.where(ep2 == p2, rowi, EXPERTS_PER_GROUP), axis=0, keepdims=True)
    tot = p1 + p2
    w1 = p1 / tot * gw
    w2 = p2 / tot * gw
    id1 = gidx * EXPERTS_PER_GROUP + e1
    id2 = gidx * EXPERTS_PER_GROUP + e2

    step = pl.program_id(0)

    @pl.when(step == 0)
    def _():
        base_ref[...] = jnp.zeros_like(base_ref)
        before = lax.broadcasted_iota(jnp.int32, (tt, tt), 0) < lax.broadcasted_iota(jnp.int32, (tt, tt), 1)
        tri_ref[...] = jnp.where(before, 1.0, 0.0).astype(BF16)

    r32 = lax.broadcasted_iota(jnp.int32, (N_EXPERTS, tt), 0)
    oh1 = jnp.where(r32 == id1, 1.0, 0.0)
    oh2 = jnp.where(r32 == id2, 1.0, 0.0)
    c1 = _dot(oh1.astype(BF16), tri_ref[...])
    c2 = _dot(oh2.astype(BF16), tri_ref[...])
    tot1 = jnp.sum(oh1, axis=1, keepdims=True)
    tot2 = jnp.sum(oh2, axis=1, keepdims=True)
    base = base_ref[:, 0:1]
    rank1 = jnp.sum(oh1 * (base + c1), axis=0, keepdims=True)
    rank2 = jnp.sum(oh2 * (base + tot1 + c2), axis=0, keepdims=True)
    new_base = jnp.broadcast_to(base + tot1 + tot2, base_ref.shape)
    base_ref[...] = new_base
    cnt_ref[...] = new_base
    route_ref[...] = jnp.concatenate([id1.astype(F32), id2.astype(F32), w1, w2, rank1, rank2,
                                      jnp.zeros((ROUTE_ROWS - 6, tt), F32)], axis=0)


def _merge(x, u, y, o_swa, o_mem, p, tile):
    n = x.shape[0]
    tile = min(tile, n)
    row = lambda i: (i, 0)
    const = lambda i: (0, 0)
    full = lambda a: pl.BlockSpec(a.shape, const, pipeline_mode=pl.Buffered(1))
    weights = [p['g1'], p['w_gates'], p['d_skip'], p['w_glu'], p['b_glu'], p['w_br_ssm'], p['w_br_swa'],
               p['w_br_mem'], p['w_out'], p['g2'], p['w_router'], p['b_router']]
    return pl.pallas_call(
        _merge_kernel,
        grid=(n // tile,),
        in_specs=[pl.BlockSpec((tile, D_MODEL), row), pl.BlockSpec((tile, SSM_WIDTH), row),
                  pl.BlockSpec((tile, SSM_WIDTH), row), pl.BlockSpec((tile, SWA_WIDTH), row),
                  pl.BlockSpec((tile, MEM_WIDTH), row)] + [full(w) for w in weights],
        out_specs=[pl.BlockSpec((tile, D_MODEL), row), pl.BlockSpec((tile, HALF), row),
                   pl.BlockSpec((ROUTE_ROWS, tile), lambda i: (0, i)),
                   pl.BlockSpec((N_EXPERTS, LANES), const)],
        out_shape=[jax.ShapeDtypeStruct((n, D_MODEL), F32), jax.ShapeDtypeStruct((n, HALF), jnp.int32),
                   jax.ShapeDtypeStruct((ROUTE_ROWS, n), F32), jax.ShapeDtypeStruct((N_EXPERTS, LANES), F32)],
        scratch_shapes=[pltpu.VMEM((N_EXPERTS, LANES), F32), pltpu.VMEM((tile, tile), BF16)],
        compiler_params=_cparams("arbitrary"),
        name="merge_router",
    )(x, u, y, o_swa, o_mem, *weights)


def _expert_mlp(xp, wg, wu, wd):
    hi, lo = _unpack_halves(xp)
    g = _dot(hi, wg[0:HALF, :]) + _dot(lo, wg[HALF:D_MODEL, :])
    u = _dot(hi, wu[0:HALF, :]) + _dot(lo, wu[HALF:D_MODEL, :])
    hh = jax.nn.silu(g) * u
    return _dot(hh.astype(BF16), wd[...])


def _moe_kernel(xn2_ref, rt_ref, wg_ref, wu_ref, wd_ref, h_ref, gf_ref, o_ref, acc_ref):
    e = pl.program_id(1)

    @pl.when(e == 0)
    def _():
        acc_ref[...] = jnp.zeros_like(acc_ref)

    o = _expert_mlp(xn2_ref[...], wg_ref[...].astype(BF16), wu_ref[...].astype(BF16), wd_ref[...].astype(BF16))
    ef = e.astype(F32)
    c = (jnp.where(rt_ref[:, 0:1] == ef, rt_ref[:, 2:3], 0.0)
         + jnp.where(rt_ref[:, 1:2] == ef, rt_ref[:, 3:4], 0.0))
    acc_ref[...] += c * o

    @pl.when(e == N_EXPERTS - 1)
    def _():
        o_ref[...] = _rms(h_ref[...] + acc_ref[...], gf_ref[...])


def _moe(xn2, route_t, w_g, w_u, w_d, h, gf, tile):
    n = h.shape[0]
    tile = min(tile, n)
    return pl.pallas_call(
        _moe_kernel,
        grid=(n // tile, N_EXPERTS),
        in_specs=[pl.BlockSpec((tile, HALF), lambda i, e: (i, 0)),
                  pl.BlockSpec((tile, ROUTE_ROWS), lambda i, e: (i, 0)),
                  pl.BlockSpec((None, D_MODEL, D_EXPERT), lambda i, e: (e, 0, 0)),
                  pl.BlockSpec((None, D_MODEL, D_EXPERT), lambda i, e: (e, 0, 0)),
                  pl.BlockSpec((None, D_EXPERT, D_MODEL), lambda i, e: (e, 0, 0)),
                  pl.BlockSpec((tile, D_MODEL), lambda i, e: (i, 0)),
                  pl.BlockSpec((1, D_MODEL), lambda i, e: (0, 0))],
        out_specs=pl.BlockSpec((tile, D_MODEL), lambda i, e: (i, 0)),
        out_shape=jax.ShapeDtypeStruct((n, D_MODEL), F32),
        scratch_shapes=[pltpu.VMEM((tile, D_MODEL), F32)],
        compiler_params=_cparams("parallel", "arbitrary"),
        name="moe_final_norm",
    )(xn2, route_t, w_g, w_u, w_d, h, gf)


EXPERT_ROW_TILE = 256
EXPERT_SLOTS = 4
SC_CORES = 2
SC_SUBCORES = 16
SC_WORKERS = SC_CORES * SC_SUBCORES
SC_SCATTER_ROWS = 64
SC_GATHER_ROWS = 64


def _sc_mesh():
    return plsc.VectorSubcoreMesh(core_axis_name="core", subcore_axis_name="subcore")


def _sc_scatter_pairs(x, pos, rows_out):
    n, d = x.shape
    per_w = n // SC_WORKERS
    window = min(SC_SCATTER_ROWS, per_w)

    @pl.kernel(out_type=jax.ShapeDtypeStruct((rows_out, d), x.dtype), mesh=_sc_mesh(),
               scratch_types=[pltpu.VMEM((window,), jnp.int32), pltpu.VMEM((window,), jnp.int32),
                              pltpu.VMEM((window, d), x.dtype), pltpu.SemaphoreType.DMA, pltpu.SemaphoreType.DMA,
                              pltpu.SemaphoreType.DMA])
    def scatter(x_hbm, p_hbm, o_hbm, i1_v, i2_v, rows_v, sem_a, sem_b, sem_c):
        wid = lax.axis_index("subcore") * SC_CORES + lax.axis_index("core")

        @pl.loop(0, per_w // window)
        def _(j):
            base = wid * per_w + j * window
            load_i1 = pltpu.async_copy(p_hbm.at[pl.ds(base, window)], i1_v, sem_a)
            load_i2 = pltpu.async_copy(p_hbm.at[pl.ds(n + base, window)], i2_v, sem_b)
            load_x = pltpu.async_copy(x_hbm.at[pl.ds(base, window)], rows_v, sem_c)
            load_i1.wait()
            load_i2.wait()
            load_x.wait()
            put_1 = pltpu.async_copy(rows_v, o_hbm.at[i1_v], sem_a)
            put_2 = pltpu.async_copy(rows_v, o_hbm.at[i2_v], sem_b)
            put_1.wait()
            put_2.wait()

    return scatter(x, pos)


def _sc_gather_rows(table, idx):
    m = idx.shape[0]
    d = table.shape[1]
    per_w = m // SC_WORKERS
    window = min(SC_GATHER_ROWS, per_w)

    assert per_w % (2 * window) == 0

    @pl.kernel(out_type=jax.ShapeDtypeStruct((m, d), table.dtype), mesh=_sc_mesh(),
               scratch_types=[pltpu.VMEM((window,), jnp.int32), pltpu.VMEM((window,), jnp.int32),
                              pltpu.VMEM((window, d), table.dtype), pltpu.VMEM((window, d), table.dtype),
                              pltpu.SemaphoreType.DMA, pltpu.SemaphoreType.DMA])
    def gather(t_hbm, i_hbm, o_hbm, ia_v, ib_v, ra_v, rb_v, sem_a, sem_b):
        wid = lax.axis_index("subcore") * SC_CORES + lax.axis_index("core")

        @pl.loop(0, per_w // (2 * window))
        def _(j):
            base_a = wid * per_w + j * (2 * window)
            base_b = base_a + window
            idx_a = pltpu.async_copy(i_hbm.at[pl.ds(base_a, window)], ia_v, sem_a)
            idx_b = pltpu.async_copy(i_hbm.at[pl.ds(base_b, window)], ib_v, sem_b)
            idx_a.wait()
            get_a = pltpu.async_copy(t_hbm.at[ia_v], ra_v, sem_a)
            idx_b.wait()
            get_b = pltpu.async_copy(t_hbm.at[ib_v], rb_v, sem_b)
            get_a.wait()
            put_a = pltpu.async_copy(ra_v, o_hbm.at[pl.ds(base_a, window)], sem_a)
            get_b.wait()
            put_b = pltpu.async_copy(rb_v, o_hbm.at[pl.ds(base_b, window)], sem_b)
            put_a.wait()
            put_b.wait()

    return gather(table, idx)


def _expert_tiles_kernel(start_ref, ntile_ref, x_hbm, wg_ref, wu_ref, wd_ref, o_hbm,
                         wg_s, wu_s, wd_s, x_buf, o_buf, in_sem, out_sem):
    e = pl.program_id(0)
    tm = x_buf.shape[1]
    nslot = x_buf.shape[0]
    first = start_ref[e] // tm
    ntile = ntile_ref[e]
    total = start_ref[N_EXPERTS - 1] // tm + ntile_ref[N_EXPERTS - 1]
    wg_s[...] = wg_ref[...].astype(BF16)
    wu_s[...] = wu_ref[...].astype(BF16)
    wd_s[...] = wd_ref[...].astype(BF16)

    def rows_of(g):
        return pl.ds(pl.multiple_of(g * tm, tm), tm)

    def fetch(g):
        slot = g % nslot
        return pltpu.make_async_copy(x_hbm.at[rows_of(g)], x_buf.at[slot], in_sem.at[slot])

    def flush(g):
        slot = g % nslot
        return pltpu.make_async_copy(o_buf.at[slot], o_hbm.at[rows_of(g)], out_sem.at[slot])

    @pl.when(e == 0)
    def _():
        for k in range(nslot - 1):
            @pl.when(k < total)
            def _(k=k):
                fetch(k).start()

    def tile(g, carry):
        @pl.when(g + nslot - 1 < total)
        def _():
            fetch(g + nslot - 1).start()

        fetch(g).wait()

        @pl.when(g >= nslot)
        def _():
            flush(g - nslot).wait()

        slot = g % nslot
        o_buf[slot] = _pack_halves(_expert_mlp(x_buf[slot], wg_s, wu_s, wd_s).astype(BF16))
        flush(g).start()
        return carry

    lax.fori_loop(first, first + ntile, tile, 0)

    @pl.when(e == N_EXPERTS - 1)
    def _():
        for k in range(nslot, 0, -1):
            @pl.when(total >= k)
            def _(k=k):
                flush(total - k).wait()


def _expert_tiles(starts, ntiles, xs, w_g, w_u, w_d):
    rows = xs.shape[0]
    tm = EXPERT_ROW_TILE
    weight = lambda shape: pl.BlockSpec((None,) + shape, lambda e, st, nt: (e, 0, 0))
    grid_spec = pltpu.PrefetchScalarGridSpec(
        num_scalar_prefetch=2,
        grid=(N_EXPERTS,),
        in_specs=[pl.BlockSpec(memory_space=pl.ANY),
                  weight((D_MODEL, D_EXPERT)), weight((D_MODEL, D_EXPERT)), weight((D_EXPERT, D_MODEL))],
        out_specs=pl.BlockSpec(memory_space=pl.ANY),
        scratch_shapes=[pltpu.VMEM((D_MODEL, D_EXPERT), BF16), pltpu.VMEM((D_MODEL, D_EXPERT), BF16),
                        pltpu.VMEM((D_EXPERT, D_MODEL), BF16),
                        pltpu.VMEM((EXPERT_SLOTS, tm, HALF), jnp.int32), pltpu.VMEM((EXPERT_SLOTS, tm, HALF), jnp.int32),
                        pltpu.SemaphoreType.DMA((EXPERT_SLOTS,)), pltpu.SemaphoreType.DMA((EXPERT_SLOTS,))],
    )
    return pl.pallas_call(
        _expert_tiles_kernel,
        grid_spec=grid_spec,
        out_shape=jax.ShapeDtypeStruct((rows, HALF), jnp.int32),
        compiler_params=_cparams("arbitrary"),
        name="expert_tiles",
    )(starts, ntiles, xs, w_g, w_u, w_d)


def _unpack_f32(p):
    return pltpu.bitcast(p & jnp.int32(-65536), F32), pltpu.bitcast(lax.shift_left(p, jnp.int32(16)), F32)


def _combine_kernel(h_ref, o1_ref, o2_ref, rt_ref, gf_ref, y_ref):
    w1, w2 = rt_ref[:, 2:3], rt_ref[:, 3:4]
    a_lo, a_hi = _unpack_f32(o1_ref[...])
    b_lo, b_hi = _unpack_f32(o2_ref[...])
    y_lo = h_ref[:, 0:HALF] + (w1 * a_lo + w2 * b_lo)
    y_hi = h_ref[:, HALF:D_MODEL] + (w1 * a_hi + w2 * b_hi)
    ms = (jnp.sum(y_lo * y_lo, axis=-1, keepdims=True) + jnp.sum(y_hi * y_hi, axis=-1, keepdims=True)) / D_MODEL
    inv = lax.rsqrt(ms + EPS)
    y_ref[:, 0:HALF] = (y_lo * inv) * gf_ref[:, 0:HALF]
    y_ref[:, HALF:D_MODEL] = (y_hi * inv) * gf_ref[:, HALF:D_MODEL]


def _combine(h, o12, route_t, gf, tile):
    n = h.shape[0]
    tile = min(tile, n)
    nt = n // tile
    return pl.pallas_call(
        _combine_kernel,
        grid=(nt,),
        in_specs=[pl.BlockSpec((tile, D_MODEL), lambda i: (i, 0)),
                  pl.BlockSpec((tile, HALF), lambda i: (i, 0)),
                  pl.BlockSpec((tile, HALF), lambda i: (i + nt, 0)),
                  pl.BlockSpec((tile, ROUTE_ROWS), lambda i: (i, 0)),
                  pl.BlockSpec((1, D_MODEL), lambda i: (0, 0))],
        out_specs=pl.BlockSpec((tile, D_MODEL), lambda i: (i, 0)),
        out_shape=jax.ShapeDtypeStruct((n, D_MODEL), F32),
        compiler_params=_cparams("parallel"),
        name="combine_final_norm",
    )(h, o12, o12, route_t, gf)


def _sparse_moe(xn2p, route, cnt, h, w_g, w_u, w_d, gf, run_before_experts):
    n = h.shape[0]
    tm = EXPERT_ROW_TILE
    rows = 2 * n + N_EXPERTS * tm
    rank = route[4:6].astype(jnp.int32)
    counts = cnt[:, 0].astype(jnp.int32)
    padded = (counts + tm - 1) // tm * tm
    e_idx = jnp.arange(N_EXPERTS, dtype=jnp.int32)
    starts = jnp.sum(jnp.where(e_idx[None, :] < e_idx[:, None], padded[None, :], 0), axis=1)
    ids = route[0:2].astype(jnp.int32)
    start_of = jnp.sum(jnp.where(ids[None] == e_idx[:, None, None], starts[:, None, None], 0), axis=0)
    pos = (start_of + rank).reshape(2 * n)
    xs = _sc_scatter_pairs(xn2p, pos, rows)
    xs, _ = lax.optimization_barrier((xs, run_before_experts))
    os_ = _expert_tiles(starts.astype(jnp.int32), (padded // tm).astype(jnp.int32), xs, w_g, w_u, w_d)
    o12 = _sc_gather_rows(os_, pos)
    return _combine(h, o12, route.T, gf, ROWS_COMBINE)


def _prep_in_weights(w_in):
    o = 0
    w_u = w_in[:, o:o + SSM_WIDTH]; o += SSM_WIDTH
    w_q = w_in[:, o:o + SWA_WIDTH]; o += SWA_WIDTH
    w_k = w_in[:, o:o + SWA_KV_WIDTH]; o += SWA_KV_WIDTH
    w_v = w_in[:, o:o + SWA_KV_WIDTH]; o += SWA_KV_WIDTH
    w_qm = w_in[:, o:o + MEM_WIDTH]; o += MEM_WIDTH
    assert o == GATES_COL0
    wq = (w_q * (SWA_HEAD_DIM ** -0.5)).reshape(D_MODEL, SWA_KV_HEADS, SWA_REP, SWA_HEAD_DIM)
    wq = wq.transpose(0, 2, 1, 3).reshape(D_MODEL, SWA_WIDTH)
    w_main = jnp.concatenate([w_u, wq, w_k, w_v, w_qm], axis=1).astype(BF16)
    return w_main, w_in.astype(BF16)


IN_SPLITS = (SSM_WIDTH, SWA_WIDTH, SWA_KV_WIDTH, SWA_KV_WIDTH, MEM_WIDTH)
IN_DTYPES = ((F32, BF16), (BF16,), (F32,), (F32,), (BF16,))


def kernel(x_prompt, x_sample, cache_swa_k, cache_swa_v, state_ssm_re, state_ssm_im, cache_mem_k, cache_mem_v, mem_prompt, norm1_g, w_in, lam_re, lam_im, log_dt, bm_re, bm_im, cm_re, cm_im, d_skip, w_glu, b_glu, sinks, rel_table, mem_norm_g, w_mem_kv, w_br_ssm, w_br_swa, w_br_mem, w_out, norm2_g, w_rg, b_rg, w_rexp, b_rexp, w_e_gate, w_e_up, w_e_down, final_norm_g):
    nb, t, _ = x_prompt.shape
    ns, ts, _ = x_sample.shape
    assert w_in.shape[0] == 1 and ts == S5_CHUNK and t % (WINDOW * SWA_BLOCKS_PER_STEP) == 0
    l = 0
    L = S5_CHUNK

    w_main, w_gates = _prep_in_weights(w_in[l])
    w_swa = (w_br_swa[l].reshape(SWA_KV_HEADS, SWA_REP, SWA_HEAD_DIM, D_MODEL).transpose(1, 0, 2, 3)
             .reshape(SWA_WIDTH, D_MODEL))
    pad_rows = ROUTER_ROWS - N_EXPERTS - N_EXPERT_GROUPS
    w_router = jnp.concatenate([w_rexp[l].T, w_rg[l].T, jnp.zeros((pad_rows, D_MODEL), F32)], axis=0).astype(BF16)
    b_router = jnp.concatenate([b_rexp[l], b_rg[l], jnp.zeros((pad_rows,), F32)]).reshape(ROUTER_ROWS, 1)
    mp = {
        'g1': norm1_g[l].reshape(1, D_MODEL), 'w_gates': w_gates, 'd_skip': d_skip[l].reshape(1, SSM_WIDTH),
        'w_glu': w_glu[l].astype(BF16), 'b_glu': b_glu[l].reshape(1, SSM_WIDTH),
        'w_br_ssm': w_br_ssm[l].astype(BF16), 'w_br_swa': w_swa.astype(BF16),
        'w_br_mem': w_br_mem[l].astype(BF16), 'w_out': w_out[l].astype(BF16),
        'g2': norm2_g[l].reshape(1, D_MODEL), 'w_router': w_router, 'b_router': b_router,
    }
    w_g, w_u, w_d = w_e_gate[l], w_e_up[l], w_e_down[l]
    gf = final_norm_g.reshape(1, D_MODEL)
    s5_w = _s5_weights(lam_re[l], lam_im[l], log_dt[l], bm_re[l], bm_im[l], cm_re[l], cm_im[l], L)

    bias_p = _rel_bias(rel_table, np.arange(WINDOW)[:, None] + WINDOW - np.arange(2 * WINDOW)[None, :])
    keys_s = WINDOW + 2 * ts
    bias_s = _rel_bias(rel_table, np.arange(ts)[:, None] + WINDOW - np.arange(keys_s)[None, :])
    bias_s = bias_s.reshape(SWA_HEADS * ts, keys_s)
    sink_rows = jnp.repeat(sinks[l].astype(F32), ts).reshape(SWA_HEADS * ts, 1)

    n = nb * t
    xp = x_prompt.reshape(n, D_MODEL)
    mk, mv = _norm_proj(mem_prompt.reshape(nb * MEM_TOKENS, D_MODEL), mem_norm_g[l].reshape(1, D_MODEL),
                        w_mem_kv[l].astype(BF16), (MEM_WIDTH, MEM_WIDTH), ((F32,), (F32,)), ROWS_MEM_PROJ)
    u, ub, qz, k, v, qm = _norm_proj(xp, mp['g1'], w_main, IN_SPLITS, IN_DTYPES, ROWS_NORM_PROJ)

    y_ssm, fin = _s5(ub, jnp.zeros((nb, N_CH_TILES * 2 * STATE_TILE), F32), s5_w, nb, t // L, L, S5_CHUNKS_PER_STEP)
    p_re, p_im = _tiles_to_state(fin)

    o_swa = _swa_prompt(qz, k, v, bias_p, sinks[l].astype(F32), nb, t, SWA_BLOCKS_PER_STEP)
    o_mem = _mem_prompt(qm, mk, mv, nb, t, ROWS_MEM_ATTN)

    m = ns * ts
    xs = x_sample.reshape(m, D_MODEL)
    us, ubs, qzs, k_s, v_s, qms = _norm_proj(xs, mp['g1'], w_main, IN_SPLITS, IN_DTYPES, ROWS_NORM_PROJ)
    ys_ssm, fins = _s5(ubs, _state_to_tiles(state_ssm_re[l], state_ssm_im[l]), s5_w, ns, ts // L, L, S5_CHUNKS_PER_STEP)
    o_mem, ys_ssm = lax.optimization_barrier((o_mem, ys_ssm))

    h, xn2p, route, cnt = _merge(xp, u, y_ssm, o_swa, o_mem, mp, ROWS_MERGE)

    k4 = k.reshape(nb, t, SWA_KV_HEADS, SWA_HEAD_DIM)
    v4 = v.reshape(nb, t, SWA_KV_HEADS, SWA_HEAD_DIM)
    new_k_p, new_v_p = k4[:, -WINDOW:][None], v4[:, -WINDOW:][None]
    new_mk = mk.reshape(1, nb, MEM_TOKENS, MEM_HEADS, MEM_HEAD_DIM)
    new_mv = mv.reshape(1, nb, MEM_TOKENS, MEM_HEADS, MEM_HEAD_DIM)

    s_re, s_im = _tiles_to_state(fins)

    kk_all = jnp.concatenate([cache_swa_k[l].reshape(ns, WINDOW, SWA_KV_WIDTH).astype(F32),
                              k_s.reshape(ns, ts, SWA_KV_WIDTH)], axis=1)
    vv_all = jnp.concatenate([cache_swa_v[l].reshape(ns, WINDOW, SWA_KV_WIDTH).astype(F32),
                              v_s.reshape(ns, ts, SWA_KV_WIDTH)], axis=1)
    pad = jnp.zeros((ns, keys_s - WINDOW - ts, SWA_KV_WIDTH), F32)
    o_dec, roll_k, roll_v = _swa_decode(qzs.astype(F32).reshape(ns, ts, SWA_WIDTH),
                                        jnp.concatenate([kk_all, pad], axis=1),
                                        jnp.concatenate([vv_all, pad], axis=1), bias_s, sink_rows, SWA_DECODE_SEQS_PER_STEP)
    o_dec, q_mem = lax.optimization_barrier((o_dec, qms.astype(F32).reshape(ns, ts, MEM_WIDTH)))
    o_swa_s = o_dec.reshape(m, SWA_WIDTH).astype(BF16)

    o_mem_s = _mem_decode(q_mem, cache_mem_k, cache_mem_v, l, MEM_DECODE_SEQS_PER_STEP)
    o_mem_s = o_mem_s.reshape(m, MEM_WIDTH).astype(BF16)

    y_prompt = _sparse_moe(xn2p, route, cnt, h, w_g, w_u, w_d, gf, (ys_ssm, o_swa_s, o_mem_s)).reshape(nb, t, D_MODEL)
    hs_, xn2ps, routes, _ = _merge(xs, us, ys_ssm, o_swa_s, o_mem_s, mp, ROWS_MERGE)
    y_sample = _moe(xn2ps, routes.T, w_g, w_u, w_d, hs_, gf, ROWS_DENSE_MOE).reshape(ns, ts, D_MODEL)

    new_k_s = roll_k.reshape(1, ns, WINDOW, SWA_KV_HEADS, SWA_HEAD_DIM).astype(cache_swa_k.dtype)
    new_v_s = roll_v.reshape(1, ns, WINDOW, SWA_KV_HEADS, SWA_HEAD_DIM).astype(cache_swa_v.dtype)

    return (y_prompt, y_sample,
            new_k_p, new_v_p, p_re[None], p_im[None], new_mk, new_mv,
            new_k_s, new_v_s, s_re[None].astype(state_ssm_re.dtype), s_im[None].astype(state_ssm_im.dtype))
```
